```python
import jax, jax.numpy as jnp
from jax import lax
import numpy as np

D_MODEL = 2048
BATCH = 4
SEQ = 2048
DEPTH = 1

D_MIX = D_MODEL
NSA_WIDTH = D_MIX // 2
POOL_WIDTH = D_MIX - NSA_WIDTH
HEAD_DIM = 128
N_HEADS = NSA_WIDTH // HEAD_DIM
N_KV = 2
GQA = N_HEADS // N_KV
KV_WIDTH = N_KV * HEAD_DIM
N_GATES = 3 * N_HEADS
ROT_DIM = HEAD_DIM // 4
ROPE_THETA = 500000.0
CMP_LEN = 32
CMP_STRIDE = 16
CMP_HIDDEN = 2 * HEAD_DIM
SEL_LEN = 64
SEL_TOPK = 16
N_LOCAL = 2
WINDOW = 512
WIN_QBLOCK = 128
SEL_QBLOCK = 64
POOL_SIZES = (2, 4, 8, 16)
N_POOL_GROUPS = 4
POOL_GROUP = POOL_WIDTH // N_POOL_GROUPS
N_EXPERT_GROUPS = 4
EXPERTS_PER_GROUP = 8
N_EXPERTS = N_EXPERT_GROUPS * EXPERTS_PER_GROUP
TOP_K_INNER = 2
D_FF_EXPERT = D_MODEL // 4
MOE_BLOCK = 128
EPS = 1e-6
NEG = -1e30
BIG = 1e30
IN_WIDTH = NSA_WIDTH + 6 * KV_WIDTH + N_GATES + POOL_WIDTH

kernel_name = "hymba_nsa_pool_hmoe_layer"


def rmsnorm(x, g):
    xf = x.astype(jnp.float32)
    y = xf * lax.rsqrt(jnp.mean(xf * xf, axis=-1, keepdims=True) + EPS)
    return (y * g.astype(jnp.float32)).astype(x.dtype)


def rope_tables(positions, dtype):
    inv_freq = ROPE_THETA ** (-jnp.arange(0, ROT_DIM, 2, dtype=jnp.float32) / ROT_DIM)
    ang = positions.astype(jnp.float32)[..., None] * inv_freq
    return jnp.cos(ang)[:, None].astype(dtype), jnp.sin(ang)[:, None].astype(dtype)


def apply_partial_rope(x, cos, sin):
    xr, xp = x[..., :ROT_DIM], x[..., ROT_DIM:]
    x1, x2 = xr[..., : ROT_DIM // 2], xr[..., ROT_DIM // 2:]
    rot = jnp.concatenate([x1 * cos - x2 * sin, x2 * cos + x1 * sin], axis=-1)
    return jnp.concatenate([rot, xp], axis=-1)


def gather_blocks(blocks, idx):
    return jax.vmap(jax.vmap(lambda b_, i_: b_[i_]))(blocks, idx)


def cmp_to_sel_weights(n_cmp, n_sel):
    cs = jnp.arange(n_cmp)[:, None] * CMP_STRIDE
    ss = jnp.arange(n_sel)[None, :] * SEL_LEN
    ov = jnp.clip(jnp.minimum(cs + CMP_LEN, ss + SEL_LEN) - jnp.maximum(cs, ss), 0)
    return ov.astype(jnp.float32) / CMP_LEN


def nsa_mixer(q, kc, vc, ks, vs, kw, vw, gates, positions,
              pe_k, w_ck1, w_ck2, pe_v, w_cv1, w_cv2):
    B, S = q.shape[0], q.shape[1]
    scale = HEAD_DIM ** -0.5
    def heads(t, n):
        return t.reshape(B, S, n, HEAD_DIM).transpose(0, 2, 1, 3)
    q = heads(q, N_HEADS)
    kc, vc, ks, vs, kw, vw = [heads(t, N_KV) for t in (kc, vc, ks, vs, kw, vw)]
    cos, sin = rope_tables(positions, q.dtype)
    q_rope = apply_partial_rope(q, cos, sin)
    ks = apply_partial_rope(ks, cos, sin)
    kw = apply_partial_rope(kw, cos, sin)
    qg = q.reshape(B, N_KV, GQA, S, HEAD_DIM)
    qrg = q_rope.reshape(B, N_KV, GQA, S, HEAD_DIM)
    t_idx = jnp.arange(S)

    n_cmp = (S - CMP_LEN) // CMP_STRIDE + 1
    cmp_tok = jnp.arange(n_cmp)[:, None] * CMP_STRIDE + jnp.arange(CMP_LEN)[None, :]
    def compress(t, pe, w1, w2):
        blk = (t[:, :, cmp_tok] + pe).reshape(B, N_KV, n_cmp, CMP_LEN * HEAD_DIM)
        return jax.nn.gelu(blk @ w1) @ w2
    k_cmp = compress(kc, pe_k, w_ck1, w_ck2)
    v_cmp = compress(vc, pe_v, w_cv1, w_cv2)
    cmp_end = jnp.arange(n_cmp) * CMP_STRIDE + CMP_LEN - 1
    cmp_ok = cmp_end[None, :] <= t_idx[:, None]
    s_cmp = jnp.einsum('bgrsd,bgnd->bgrsn', qg, k_cmp).astype(jnp.float32) * scale
    s_cmp = jnp.where(cmp_ok, s_cmp, NEG)
    p_cmp = jnp.where(cmp_ok, jax.nn.softmax(s_cmp, axis=-1), 0.0)
    o_cmp = jnp.einsum('bgrsn,bgnd->bgrsd', p_cmp.astype(v_cmp.dtype), v_cmp)

    n_sel = S // SEL_LEN
    imp = jnp.einsum('bgrsn,nj->bgsj', p_cmp, cmp_to_sel_weights(n_cmp, n_sel))
    blk_j = jnp.arange(n_sel)[None, :]
    cur = (t_idx // SEL_LEN)[:, None]
    sel_ok = blk_j * SEL_LEN <= t_idx[:, None]
    forced = (blk_j == 0) | ((cur - blk_j >= 0) & (cur - blk_j < N_LOCAL))
    imp = jnp.where(sel_ok & forced, BIG, jnp.where(sel_ok, imp, -BIG))
    k_top = min(SEL_TOPK, n_sel)
    top_val, top_idx = lax.top_k(imp, k_top)
    top_ok = top_val >= 0.0
    ks_blocks = ks.reshape(B, N_KV, n_sel, SEL_LEN, HEAD_DIM)
    vs_blocks = vs.reshape(B, N_KV, n_sel, SEL_LEN, HEAD_DIM)
    nq = S // SEL_QBLOCK
    q_chunks = jnp.moveaxis(qrg.reshape(B, N_KV, GQA, nq, SEL_QBLOCK, HEAD_DIM), 3, 0)
    i_chunks = jnp.moveaxis(top_idx.reshape(B, N_KV, nq, SEL_QBLOCK, k_top), 2, 0)
    ok_chunks = jnp.moveaxis(top_ok.reshape(B, N_KV, nq, SEL_QBLOCK, k_top), 2, 0)
    t_chunks = t_idx.reshape(nq, SEL_QBLOCK)
    def sel_block(args):
        qb, ib, okb, tb = args
        flat = ib.reshape(B, N_KV, SEL_QBLOCK * k_top)
        kb = gather_blocks(ks_blocks, flat).reshape(B, N_KV, SEL_QBLOCK, k_top * SEL_LEN, HEAD_DIM)
        vb = gather_blocks(vs_blocks, flat).reshape(B, N_KV, SEL_QBLOCK, k_top * SEL_LEN, HEAD_DIM)
        kpos = (ib[..., None] * SEL_LEN + jnp.arange(SEL_LEN)).reshape(B, N_KV, SEL_QBLOCK, k_top * SEL_LEN)
        ok = jnp.repeat(okb, SEL_LEN, axis=-1) & (kpos <= tb[:, None])
        s = jnp.einsum('bgrqd,bgqld->bgrql', qb, kb).astype(jnp.float32) * scale
        s = jnp.where(ok[:, :, None], s, NEG)
        p = jax.nn.softmax(s, axis=-1).astype(vb.dtype)
        return jnp.einsum('bgrql,bgqld->bgrqd', p, vb)
    o_sel = lax.map(sel_block, (q_chunks, i_chunks, ok_chunks, t_chunks))
    o_sel = jnp.moveaxis(o_sel, 0, 3).reshape(B, N_KV, GQA, S, HEAD_DIM)

    nwb = S // WIN_QBLOCK
    kw_pad = jnp.pad(kw, ((0, 0), (0, 0), (WINDOW, 0), (0, 0)))
    vw_pad = jnp.pad(vw, ((0, 0), (0, 0), (WINDOW, 0), (0, 0)))
    win_idx = jnp.arange(nwb)[:, None] * WIN_QBLOCK + jnp.arange(WIN_QBLOCK + WINDOW)[None, :]
    kwb = kw_pad[:, :, win_idx]
    vwb = vw_pad[:, :, win_idx]
    kpos = (win_idx - WINDOW)[:, None, :]
    qpos = t_idx.reshape(nwb, WIN_QBLOCK)[:, :, None]
    w_ok = (kpos <= qpos) & (kpos > qpos - WINDOW) & (kpos >= 0)
    qw = qrg.reshape(B, N_KV, GQA, nwb, WIN_QBLOCK, HEAD_DIM)
    s_w = jnp.einsum('bgrnqd,bgnkd->bgrnqk', qw, kwb).astype(jnp.float32) * scale
    s_w = jnp.where(w_ok, s_w, NEG)
    p_w = jax.nn.softmax(s_w, axis=-1).astype(vwb.dtype)
    o_win = jnp.einsum('bgrnqk,bgnkd->bgrnqd', p_w, vwb).reshape(B, N_KV, GQA, S, HEAD_DIM)

    g = jax.nn.sigmoid(gates.reshape(B, S, N_HEADS, 3).transpose(0, 2, 1, 3))
    g = g.reshape(B, N_KV, GQA, S, 3)
    o = g[..., 0:1] * o_cmp + g[..., 1:2] * o_sel + g[..., 2:3] * o_win
    return o.reshape(B, N_HEADS, S, HEAD_DIM).transpose(0, 2, 1, 3).reshape(B, S, NSA_WIDTH)


def pool_mixer(u, w_pool, b_pool, pool_scale):
    B, S, C = u.shape
    uf = u.astype(jnp.float32)
    c0 = jnp.pad(jnp.cumsum(uf, axis=1), ((0, 0), (1, 0), (0, 0)))
    t1 = jnp.arange(1, S + 1, dtype=jnp.float32)[:, None]
    outs = []
    for gi, w in enumerate(POOL_SIZES):
        sl = slice(gi * POOL_GROUP, (gi + 1) * POOL_GROUP)
        cg = c0[:, :, sl]
        prev = jnp.pad(cg[:, : S + 1 - w], ((0, 0), (w - 1, 0), (0, 0)))
        mean = (cg[:, 1:] - prev) / jnp.minimum(t1, float(w))
        outs.append(mean - uf[:, :, sl])
    d = jnp.stack(outs, axis=2).astype(u.dtype)
    y = jnp.einsum('bsgc,gcd->bsgd', d, w_pool).reshape(B, S, C) + b_pool
    return y * pool_scale


def hier_moe(h, w_rg, b_rg, w_re, b_re, w_gate, w_up, w_down):
    B, S, D = h.shape
    N = B * S
    hf = h.reshape(N, D)
    rows = jnp.arange(N)
    lg = (hf @ w_rg).astype(jnp.float32) + b_rg.astype(jnp.float32)
    pg = jax.nn.softmax(lg, axis=-1)
    g_sel = jnp.argmax(lg, axis=-1)
    p_g = pg[rows, g_sel]
    le = jnp.einsum('nd,gde->nge', hf, w_re).astype(jnp.float32) + b_re.astype(jnp.float32)
    pe = jax.nn.softmax(le[rows, g_sel], axis=-1)
    top_p, top_e = lax.top_k(pe, TOP_K_INNER)
    gate = p_g[:, None] * top_p / jnp.sum(top_p, axis=-1, keepdims=True)
    expert = g_sel[:, None] * EXPERTS_PER_GROUP + top_e
    na = N * TOP_K_INNER
    e_flat = expert.reshape(na)
    tok_flat = jnp.repeat(rows, TOP_K_INNER)
    g_flat = gate.reshape(na)
    order = jnp.argsort(e_flat)
    e_s, tok_s, g_s = e_flat[order], tok_flat[order], g_flat[order]
    counts = jnp.bincount(e_flat, length=N_EXPERTS)
    starts = jnp.cumsum(counts) - counts
    pcounts = (counts + MOE_BLOCK - 1) // MOE_BLOCK * MOE_BLOCK
    pends = jnp.cumsum(pcounts)
    pstarts = pends - pcounts
    dest = pstarts[e_s] + jnp.arange(na) - starts[e_s]
    n_blocks = (na + N_EXPERTS * (MOE_BLOCK - 1) + MOE_BLOCK - 1) // MOE_BLOCK
    cap = n_blocks * MOE_BLOCK
    buf_tok = jnp.full((cap,), N, dtype=jnp.int32).at[dest].set(tok_s.astype(jnp.int32))
    buf_gate = jnp.zeros((cap,), jnp.float32).at[dest].set(g_s)
    blk_expert = jnp.minimum(jnp.searchsorted(pends, jnp.arange(n_blocks) * MOE_BLOCK, side='right'),
                             N_EXPERTS - 1)
    h_pad = jnp.concatenate([hf, jnp.zeros((1, D), hf.dtype)], axis=0)
    def run_block(args):
        tk, gt, e = args
        xb = h_pad[tk]
        y = (jax.nn.silu(xb @ w_gate[e]) * (xb @ w_up[e])) @ w_down[e]
        return y * gt[:, None].astype(y.dtype)
    ys = lax.map(run_block, (buf_tok.reshape(n_blocks, MOE_BLOCK),
                             buf_gate.reshape(n_blocks, MOE_BLOCK), blk_expert))
    out = jnp.zeros((N + 1, D), h.dtype).at[buf_tok].add(ys.reshape(cap, D))[:N]
    return out.reshape(B, S, D)


def setup_inputs(seed: int = 0) -> dict:
    key = jax.random.key(seed)
    ks = jax.random.split(key, 24)
    f32 = jnp.float32
    def nrm(k, shape, fan_in):
        return jax.random.normal(k, shape, f32) * (fan_in ** -0.5)
    def gain(k, shape):
        return 1.0 + 0.1 * jax.random.normal(k, shape, f32)
    L = DEPTH
    return {
        "x": jax.random.normal(ks[0], (BATCH, SEQ, D_MODEL), f32),
        "positions": (jnp.arange(SEQ, dtype=jnp.int32)[None, :]
                      + jax.random.randint(ks[1], (BATCH, 1), 0, 1024, dtype=jnp.int32)),
        "ln_mix": gain(ks[2], (L, D_MODEL)),
        "w_in": nrm(ks[3], (L, D_MODEL, IN_WIDTH), D_MODEL),
        "pe_cmp_k": 0.1 * jax.random.normal(ks[4], (L, CMP_LEN, HEAD_DIM), f32),
        "w_cmp_k1": nrm(ks[5], (L, CMP_LEN * HEAD_DIM, CMP_HIDDEN), CMP_LEN * HEAD_DIM),
        "w_cmp_k2": nrm(ks[6], (L, CMP_HIDDEN, HEAD_DIM), CMP_HIDDEN),
        "pe_cmp_v": 0.1 * jax.random.normal(ks[7], (L, CMP_LEN, HEAD_DIM), f32),
        "w_cmp_v1": nrm(ks[8], (L, CMP_LEN * HEAD_DIM, CMP_HIDDEN), CMP_LEN * HEAD_DIM),
        "w_cmp_v2": nrm(ks[9], (L, CMP_HIDDEN, HEAD_DIM), CMP_HIDDEN),
        "w_pool": nrm(ks[10], (L, N_POOL_GROUPS, POOL_GROUP, POOL_GROUP), POOL_GROUP),
        "b_pool": 0.01 * jax.random.normal(ks[11], (L, POOL_WIDTH), f32),
        "pool_scale": gain(ks[12], (L, POOL_WIDTH)),
        "gn_nsa": gain(ks[13], (L, NSA_WIDTH)),
        "gn_pool": gain(ks[14], (L, POOL_WIDTH)),
        "w_out": nrm(ks[15], (L, D_MIX, D_MODEL), D_MIX),
        "ln_moe": gain(ks[16], (L, D_MODEL)),
        "w_router_group": nrm(ks[17], (L, D_MODEL, N_EXPERT_GROUPS), D_MODEL),
        "b_router_group": 0.01 * jax.random.normal(ks[18], (L, N_EXPERT_GROUPS), f32),
        "w_router_expert": nrm(ks[19], (L, N_EXPERT_GROUPS, D_MODEL, EXPERTS_PER_GROUP), D_MODEL),
        "b_router_expert": 0.01 * jax.random.normal(ks[20], (L, N_EXPERT_GROUPS, EXPERTS_PER_GROUP), f32),
        "w_gate": nrm(ks[21], (L, N_EXPERTS, D_MODEL, D_FF_EXPERT), D_MODEL),
        "w_up": nrm(ks[22], (L, N_EXPERTS, D_MODEL, D_FF_EXPERT), D_MODEL),
        "w_down": nrm(ks[23], (L, N_EXPERTS, D_FF_EXPERT, D_MODEL), D_FF_EXPERT),
        "ln_final": gain(jax.random.fold_in(key, 99), (D_MODEL,)),
    }


def reference(x, positions, ln_mix, w_in, pe_cmp_k, w_cmp_k1, w_cmp_k2, pe_cmp_v, w_cmp_v1,
              w_cmp_v2, w_pool, b_pool, pool_scale, gn_nsa, gn_pool, w_out, ln_moe,
              w_router_group, b_router_group, w_router_expert, b_router_expert,
              w_gate, w_up, w_down, ln_final):
    sizes = (NSA_WIDTH, KV_WIDTH, KV_WIDTH, KV_WIDTH, KV_WIDTH, KV_WIDTH, KV_WIDTH, N_GATES, POOL_WIDTH)
    cuts, acc = [], 0
    for s_ in sizes[:-1]:
        acc += s_
        cuts.append(acc)
    for l in range(DEPTH):
        h = rmsnorm(x, ln_mix[l])
        proj = h @ w_in[l]
        q, kc, vc, ksel, vsel, kwin, vwin, gates, u = jnp.split(proj, cuts, axis=-1)
        o_nsa = nsa_mixer(q, kc, vc, ksel, vsel, kwin, vwin, gates, positions,
                          pe_cmp_k[l], w_cmp_k1[l], w_cmp_k2[l],
                          pe_cmp_v[l], w_cmp_v1[l], w_cmp_v2[l])
        o_pool = pool_mixer(u, w_pool[l], b_pool[l], pool_scale[l])
        mix = jnp.concatenate([rmsnorm(o_nsa, gn_nsa[l]), rmsnorm(o_pool, gn_pool[l])], axis=-1)
        x = x + mix @ w_out[l]
        h2 = rmsnorm(x, ln_moe[l])
        x = x + hier_moe(h2, w_router_group[l], b_router_group[l], w_router_expert[l],
                         b_router_expert[l], w_gate[l], w_up[l], w_down[l])
    return rmsnorm(x, ln_final)
```

```python
import functools

import jax
import jax.numpy as jnp
from jax import lax
from jax.experimental import pallas as pl
from jax.experimental.pallas import tpu as pltpu

F32 = jnp.float32
BF16 = jnp.bfloat16
I32 = jnp.int32

HEAD_DIM = 128
N_HEADS = 8
N_KV = 2
GQA = N_HEADS // N_KV
N_GATES = 3 * N_HEADS
ROT_DIM = HEAD_DIM // 4
ROPE_THETA = 500000.0
CMP_LEN = 32
CMP_STRIDE = 16
CMP_HIDDEN = 2 * HEAD_DIM
SEL_LEN = 64
SEL_TOPK = 16
N_LOCAL = 2
WINDOW = 512
POOL_SIZES = (2, 4, 8, 16)
N_EXPERT_GROUPS = 4
EXPERTS_PER_GROUP = 8
N_EXPERTS = N_EXPERT_GROUPS * EXPERTS_PER_GROUP
EPS = 1e-6
NEG = -1e30
BIG = 1e30
SCALE = HEAD_DIM ** -0.5

LANES = 128
SUBLANES = 8
VMEM_LIMIT_BYTES = 56 * 1024 * 1024

TM_PROJ = 256
TQ = 256
TK = 256
SEL_MASK_BIAS = 32768.0
TM_MOE = 256
TS_SCATTER = 512
TM_COMBINE = 128
T_RANK = 512


def _cparams(n_axes):
    return pltpu.CompilerParams(dimension_semantics=("arbitrary",) * n_axes,
                                vmem_limit_bytes=VMEM_LIMIT_BYTES)


def _dot(a, b):
    return jnp.dot(a, b, preferred_element_type=F32)


def _dot_nt(a, b):
    return lax.dot_general(a, b, (((1,), (1,)), ((), ())), preferred_element_type=F32)


def _inproj_kernel(x_ref, ln_ref, ang_ref, wa_ref, wg_ref, wu_ref,
                   qn_ref, qr_ref, kvc_ref, kv8_ref, gates_ref, u_ref):
    x = x_ref[...]
    ms = jnp.mean(x * x, axis=-1, keepdims=True)
    hb = (x * lax.rsqrt(ms + EPS) * ln_ref[...]).astype(BF16)
    ang = ang_ref[...]
    cos = jnp.cos(ang)
    sin = jnp.sin(ang)
    lane = lax.broadcasted_iota(I32, ang.shape, 1)

    def rope(t):
        partner = jnp.where(lane < ROT_DIM // 2,
                            pltpu.roll(t, HEAD_DIM - ROT_DIM // 2, 1),
                            pltpu.roll(t, ROT_DIM // 2, 1))
        return t * cos + partner * sin

    chunk = 4 * HEAD_DIM
    for ch in range(2):
        r = _dot(hb, wa_ref[:, ch * chunk:(ch + 1) * chunk])
        for j in range(4):
            t = r[:, j * HEAD_DIM:(j + 1) * HEAD_DIM]
            qn_ref[0, ch * 4 + j] = (t * SCALE).astype(BF16)
            qr_ref[0, ch * 4 + j] = (rope(t) * SCALE).astype(BF16)
    base = N_HEADS * HEAD_DIM
    r = _dot(hb, wa_ref[:, base:base + chunk])
    for j in range(4):
        kvc_ref[0, j] = r[:, j * HEAD_DIM:(j + 1) * HEAD_DIM].astype(BF16)
    for ch in range(2):
        r = _dot(hb, wa_ref[:, base + (ch + 1) * chunk:base + (ch + 2) * chunk])
        for j in range(4):
            t = r[:, j * HEAD_DIM:(j + 1) * HEAD_DIM]
            if j < 2:
                t = rope(t)
            kv8_ref[0, ch * 4 + j] = t.astype(BF16)
    gates_ref[...] = _dot(hb, wg_ref[...])
    for ch in range(2):
        u_ref[:, ch * chunk:(ch + 1) * chunk] = _dot(hb, wu_ref[:, ch * chunk:(ch + 1) * chunk])


def _inproj(x2d, ln, ang, wa, wg, wu, batch, seq):
    n, d = x2d.shape
    tm = TM_PROJ
    spb = seq // tm
    const = lambda i: (0, 0)
    head_map = lambda i: (i // spb, 0, i % spb, 0)
    return pl.pallas_call(
        _inproj_kernel,
        grid=(n // tm,),
        in_specs=[
            pl.BlockSpec((tm, d), lambda i: (i, 0)),
            pl.BlockSpec((1, d), const),
            pl.BlockSpec((tm, HEAD_DIM), lambda i: (i, 0)),
            pl.BlockSpec(wa.shape, const, pipeline_mode=pl.Buffered(1)),
            pl.BlockSpec(wg.shape, const, pipeline_mode=pl.Buffered(1)),
            pl.BlockSpec(wu.shape, const, pipeline_mode=pl.Buffered(1)),
        ],
        out_specs=[
            pl.BlockSpec((1, N_HEADS, tm, HEAD_DIM), head_map),
            pl.BlockSpec((1, N_HEADS, tm, HEAD_DIM), head_map),
            pl.BlockSpec((1, 4, tm, HEAD_DIM), head_map),
            pl.BlockSpec((1, 8, tm, HEAD_DIM), head_map),
            pl.BlockSpec((tm, LANES), lambda i: (i, 0)),
            pl.BlockSpec((tm, wu.shape[1]), lambda i: (i, 0)),
        ],
        out_shape=[
            jax.ShapeDtypeStruct((batch, N_HEADS, seq, HEAD_DIM), BF16),
            jax.ShapeDtypeStruct((batch, N_HEADS, seq, HEAD_DIM), BF16),
            jax.ShapeDtypeStruct((batch, 4, seq, HEAD_DIM), BF16),
            jax.ShapeDtypeStruct((batch, 8, seq, HEAD_DIM), BF16),
            jax.ShapeDtypeStruct((n, LANES), F32),
            jax.ShapeDtypeStruct((n, wu.shape[1]), F32),
        ],
        compiler_params=_cparams(1),
        name="inproj",
    )(x2d, ln, ang, wa, wg, wu)


def _compress_kernel(t_ref, pe_ref, w1_ref, w2_ref, o_ref):
    t2 = t_ref[0, 0]
    w1 = w1_ref[0]
    a = _dot(t2, w1)
    pb = _dot(pe_ref[0], w1)
    bias = pb[0:1, :CMP_HIDDEN] + pb[1:2, CMP_HIDDEN:]
    nrow = a.shape[0]
    nxt = pltpu.roll(a[:, CMP_HIDDEN:], nrow - 1, 0)
    hid = jax.nn.gelu(a[:, :CMP_HIDDEN] + nxt + bias)
    out = _dot(hid.astype(BF16), w2_ref[0])
    row = lax.broadcasted_iota(I32, out.shape, 0)
    o_ref[0, 0] = jnp.where(row < nrow - 1, out, 0.0).astype(BF16)


def _compress(kvc2, pe2, w1, w2):
    batch, four, nrow, width = kvc2.shape
    return pl.pallas_call(
        _compress_kernel,
        grid=(batch, four),
        in_specs=[
            pl.BlockSpec((1, 1, nrow, width), lambda b, j: (b, j, 0, 0)),
            pl.BlockSpec((1, SUBLANES, width), lambda b, j: (j // N_KV, 0, 0)),
            pl.BlockSpec((1, width, 2 * CMP_HIDDEN), lambda b, j: (j // N_KV, 0, 0)),
            pl.BlockSpec((1, CMP_HIDDEN, HEAD_DIM), lambda b, j: (j // N_KV, 0, 0)),
        ],
        out_specs=pl.BlockSpec((1, 1, nrow, HEAD_DIM), lambda b, j: (b, j, 0, 0)),
        out_shape=jax.ShapeDtypeStruct((batch, four, nrow, HEAD_DIM), BF16),
        compiler_params=_cparams(2),
        name="compress",
    )(kvc2, pe2, w1, w2)


def _nsa_kernel(qn_ref, qr_ref, cmp_ref, kv_ref, kblk_ref, gates_ref, gn_ref, o_ref,
                m_scr, l_scr, acc_scr, o_scr):
    qi = pl.program_id(1)
    tq = qn_ref.shape[2]
    rows = GQA * tq
    n_cmp_pad = cmp_ref.shape[2]
    n_sel = kv_ref.shape[2] // SEL_LEN
    q0 = qi * tq

    def flash(q_aug, k_of, v_of, lo, hi, mask_of):
        m_scr[...] = jnp.full(m_scr.shape, NEG, F32)
        l_scr[...] = jnp.zeros(l_scr.shape, F32)
        acc_scr[...] = jnp.zeros(acc_scr.shape, F32)

        def body(kj, carry):
            s = _dot_nt(q_aug, k_of(kj))
            ok = mask_of(kj)
            s = jnp.where(ok[None], s.reshape(GQA, tq, TK), NEG).reshape(rows, TK)
            m_old = m_scr[...]
            m_new = jnp.maximum(m_old, jnp.max(s, axis=-1, keepdims=True))
            alpha = jnp.exp(m_old - m_new)
            p = jnp.exp(s - m_new)
            l_scr[...] = alpha * l_scr[...] + jnp.sum(p, axis=-1, keepdims=True)
            acc_scr[...] = alpha * acc_scr[...] + _dot(p.astype(BF16), v_of(kj))
            m_scr[...] = m_new
            return carry

        lax.fori_loop(lo, hi, body, 0)
        return acc_scr[...] / l_scr[...]

    t_row = q0 + lax.broadcasted_iota(I32, (tq, TK), 0)
    k_col = lax.broadcasted_iota(I32, (tq, TK), 1)

    for g in range(N_KV):
        qn = qn_ref[0, g * GQA:(g + 1) * GQA].reshape(rows, HEAD_DIM)
        k_cmp = cmp_ref[0, g]
        v_cmp = cmp_ref[0, N_KV + g]
        s = _dot_nt(qn, k_cmp).reshape(GQA, tq, n_cmp_pad)
        n_idx = lax.broadcasted_iota(I32, (tq, n_cmp_pad), 1)
        t_idx = q0 + lax.broadcasted_iota(I32, (tq, n_cmp_pad), 0)
        cmp_ok = (n_idx * CMP_STRIDE + (CMP_LEN - 1) <= t_idx)[None]
        s = jnp.where(cmp_ok, s, NEG)
        m = jnp.max(s, axis=-1, keepdims=True)
        e = jnp.where(cmp_ok, jnp.exp(s - m), 0.0)
        l = jnp.sum(e, axis=-1, keepdims=True)
        p = e * jnp.where(l > 0.0, 1.0 / l, 0.0)
        pb = p.astype(BF16).reshape(rows, n_cmp_pad)
        o_cmp = _dot(pb, v_cmp)

        jn = lax.broadcasted_iota(I32, (n_sel, n_cmp_pad), 0) * SEL_LEN
        cn = lax.broadcasted_iota(I32, (n_sel, n_cmp_pad), 1) * CMP_STRIDE
        ov = jnp.clip(jnp.minimum(cn + CMP_LEN, jn + SEL_LEN) - jnp.maximum(cn, jn), 0, None)
        w_t = (ov.astype(F32) * (1.0 / CMP_LEN)).astype(BF16)
        imp4 = _dot_nt(w_t, pb)
        imp = imp4[:, 0:tq]
        for r in range(1, GQA):
            imp = imp + imp4[:, r * tq:(r + 1) * tq]
        j_blk = lax.broadcasted_iota(I32, (n_sel, tq), 0)
        t_q = q0 + lax.broadcasted_iota(I32, (n_sel, tq), 1)
        sel_ok = j_blk * SEL_LEN <= t_q
        back = t_q // SEL_LEN - j_blk
        forced = (j_blk == 0) | ((back >= 0) & (back < N_LOCAL))
        val = jnp.where(sel_ok & forced, BIG, jnp.where(sel_ok, imp, -BIG))
        rank = jnp.zeros((n_sel, tq), F32)
        for i in range(n_sel):
            vi = val[i:i + 1, :]
            beats = (vi > val) | ((vi == val) & (j_blk > i))
            rank = rank + beats.astype(F32)
        keep = (rank < float(min(SEL_TOPK, n_sel))) & sel_ok
        bias_t = jnp.where(keep, 0.0, -SEL_MASK_BIAS)
        bias_t = jnp.concatenate([bias_t, jnp.zeros((LANES - n_sel, tq), F32)], axis=0)
        bias = bias_t.T.astype(BF16)
        bias4 = jnp.concatenate([bias] * GQA, axis=0)

        qr = qr_ref[0, g * GQA:(g + 1) * GQA].reshape(rows, HEAD_DIM)
        q_aug = jnp.concatenate([qr, bias4], axis=1)

        def k_sel(kj, g=g):
            st = pl.multiple_of(kj * TK, TK)
            return jnp.concatenate([kv_ref[0, g, pl.ds(st, TK), :], kblk_ref[pl.ds(st, TK), :]], axis=1)

        def v_sel(kj, g=g):
            return kv_ref[0, N_KV + g, pl.ds(pl.multiple_of(kj * TK, TK), TK), :]

        def causal(kj):
            return kj * TK + k_col <= t_row

        o_sel = flash(q_aug, k_sel, v_sel, 0, (q0 + tq) // TK, causal)

        def k_win(kj, g=g):
            return kv_ref[0, 2 * N_KV + g, pl.ds(pl.multiple_of(kj * TK, TK), TK), :]

        def v_win(kj, g=g):
            return kv_ref[0, 3 * N_KV + g, pl.ds(pl.multiple_of(kj * TK, TK), TK), :]

        def band(kj):
            kpos = kj * TK + k_col
            return (kpos <= t_row) & (kpos > t_row - WINDOW)

        lo = jnp.maximum(q0 - WINDOW, 0) // TK
        o_win = flash(qr, k_win, v_win, lo, (q0 + tq) // TK, band)

        gsig = jax.nn.sigmoid(gates_ref[...])
        for r in range(GQA):
            h = g * GQA + r
            sl = slice(r * tq, (r + 1) * tq)
            o_h = (gsig[:, 3 * h:3 * h + 1] * o_cmp[sl]
                   + gsig[:, 3 * h + 1:3 * h + 2] * o_sel[sl]
                   + gsig[:, 3 * h + 2:3 * h + 3] * o_win[sl])
            o_scr[:, h * HEAD_DIM:(h + 1) * HEAD_DIM] = o_h

    o = o_scr[...]
    ms = jnp.mean(o * o, axis=-1, keepdims=True)
    o_ref[...] = (o * lax.rsqrt(ms + EPS) * gn_ref[...]).astype(BF16)


def _nsa(qn, qr, cmp, kv8, kblk, gates, gn):
    batch, _, seq, _ = qn.shape
    nq = seq // TQ
    width = N_HEADS * HEAD_DIM
    return pl.pallas_call(
        _nsa_kernel,
        grid=(batch, nq),
        in_specs=[
            pl.BlockSpec((1, N_HEADS, TQ, HEAD_DIM), lambda b, i: (b, 0, i, 0)),
            pl.BlockSpec((1, N_HEADS, TQ, HEAD_DIM), lambda b, i: (b, 0, i, 0)),
            pl.BlockSpec((1, 4, cmp.shape[2], HEAD_DIM), lambda b, i: (b, 0, 0, 0)),
            pl.BlockSpec((1, 8, seq, HEAD_DIM), lambda b, i: (b, 0, 0, 0)),
            pl.BlockSpec((seq, LANES), lambda b, i: (0, 0)),
            pl.BlockSpec((TQ, LANES), lambda b, i: (b * nq + i, 0)),
            pl.BlockSpec((1, width), lambda b, i: (0, 0)),
        ],
        out_specs=pl.BlockSpec((TQ, width), lambda b, i: (b * nq + i, 0)),
        out_shape=jax.ShapeDtypeStruct((batch * seq, width), BF16),
        scratch_shapes=[
            pltpu.VMEM((GQA * TQ, 1), F32),
            pltpu.VMEM((GQA * TQ, 1), F32),
            pltpu.VMEM((GQA * TQ, HEAD_DIM), F32),
            pltpu.VMEM((TQ, width), F32),
        ],
        compiler_params=_cparams(2),
        name="nsa",
    )(qn, qr, cmp, kv8, kblk, gates, gn)


def _pool_kernel(u_ref, up_ref, w_ref, b_ref, sc_ref, gn_ref, o_ref, y_scr):
    si = pl.program_id(1)
    tm = u_ref.shape[1]
    halo = max(POOL_SIZES)
    cur = u_ref[0]
    prev = jnp.where(si > 0, up_ref[0, tm - halo:, :], 0.0)
    ext = jnp.concatenate([prev, cur], axis=0)
    t1 = (si * tm + 1 + lax.broadcasted_iota(I32, (tm, 1), 0)).astype(F32)
    cg = cur.shape[1] // len(POOL_SIZES)
    for gi, w in enumerate(POOL_SIZES):
        sl = slice(gi * cg, (gi + 1) * cg)
        acc = ext[:, sl]
        span = 1
        while span < w:
            acc = acc + jnp.concatenate([jnp.zeros((span, cg), F32), acc[:-span]], axis=0)
            span *= 2
        mean = acc[halo:] / jnp.minimum(t1, float(w))
        d = (mean - cur[:, sl]).astype(BF16)
        y = _dot(d, w_ref[gi]) + b_ref[:, sl]
        y_scr[:, sl] = y * sc_ref[:, sl]
    y = y_scr[...]
    ms = jnp.mean(y * y, axis=-1, keepdims=True)
    o_ref[...] = (y * lax.rsqrt(ms + EPS) * gn_ref[...]).astype(BF16)


def _pool(u3, w_pool, b_pool, pool_scale, gn_pool):
    batch, seq, c = u3.shape
    tm = TM_PROJ
    ns = seq // tm
    vec = lambda b, i: (0, 0)
    return pl.pallas_call(
        _pool_kernel,
        grid=(batch, ns),
        in_specs=[
            pl.BlockSpec((1, tm, c), lambda b, i: (b, i, 0)),
            pl.BlockSpec((1, tm, c), lambda b, i: (b, jnp.maximum(i - 1, 0), 0)),
            pl.BlockSpec(w_pool.shape, lambda b, i: (0, 0, 0)),
            pl.BlockSpec((1, c), vec),
            pl.BlockSpec((1, c), vec),
            pl.BlockSpec((1, c), vec),
        ],
        out_specs=pl.BlockSpec((tm, c), lambda b, i: (b * ns + i, 0)),
        out_shape=jax.ShapeDtypeStruct((batch * seq, c), BF16),
        scratch_shapes=[pltpu.VMEM((tm, c), F32)],
        compiler_params=_cparams(2),
        name="pool",
    )(u3, u3, w_pool, b_pool, pool_scale, gn_pool)


def _outproj_kernel(mn_ref, mp_ref, x_ref, w_ref, ln_ref, wr_ref, br_ref,
                    x1_ref, h2_ref, eid_ref, gate_ref):
    tm, d = x_ref.shape
    half = mn_ref.shape[1]
    chunk = 512
    mn = mn_ref[...]
    mp = mp_ref[...]
    ssq = jnp.zeros((tm, 1), F32)
    for c in range(d // chunk):
        sl = slice(c * chunk, (c + 1) * chunk)
        x1 = x_ref[:, sl] + (_dot(mn, w_ref[:half, sl]) + _dot(mp, w_ref[half:, sl]))
        x1_ref[:, sl] = x1
        ssq = ssq + jnp.sum(x1 * x1, axis=-1, keepdims=True)
    h2 = x1_ref[...] * lax.rsqrt(ssq * (1.0 / d) + EPS) * ln_ref[...]
    h2_ref[...] = h2

    logits = _dot_nt(wr_ref[...], h2.astype(BF16)) + br_ref[:, 0:1]
    ng, ne = N_EXPERT_GROUPS, EXPERTS_PER_GROUP
    row = lax.broadcasted_iota(I32, (ne, tm), 0)
    lg = jnp.where(row < ng, logits[0:ne], NEG)
    mg = jnp.max(lg, axis=0, keepdims=True)
    g_sel = jnp.min(jnp.where(lg == mg, row, ne), axis=0, keepdims=True)
    p_g = 1.0 / jnp.sum(jnp.exp(lg - mg), axis=0, keepdims=True)
    le = jnp.zeros((ne, tm), F32)
    for g in range(ng):
        le = jnp.where(g_sel == g, logits[ne * (g + 1):ne * (g + 2)], le)
    ex = jnp.exp(le - jnp.max(le, axis=0, keepdims=True))
    pe = ex / jnp.sum(ex, axis=0, keepdims=True)
    p1 = jnp.max(pe, axis=0, keepdims=True)
    i1 = jnp.min(jnp.where(pe == p1, row, ne), axis=0, keepdims=True)
    rest = jnp.where(row == i1, -1.0, pe)
    p2 = jnp.max(rest, axis=0, keepdims=True)
    i2 = jnp.min(jnp.where(rest == p2, row, ne), axis=0, keepdims=True)
    psum = p1 + p2
    zero_i = jnp.zeros((ne, tm), I32)
    eid = jnp.where(row == 0, g_sel * ne + i1, jnp.where(row == 1, g_sel * ne + i2, zero_i))
    gate = jnp.where(row == 0, p_g * p1 / psum, jnp.where(row == 1, p_g * p2 / psum, 0.0))
    eid_ref[...] = eid
    gate_ref[...] = gate


def _outproj(mix_nsa, mix_pool, x2d, w_out, ln_moe, w_router_t, b_router):
    n, d = x2d.shape
    tm = TM_PROJ
    half = mix_nsa.shape[1]
    const = lambda i: (0, 0)
    return pl.pallas_call(
        _outproj_kernel,
        grid=(n // tm,),
        in_specs=[
            pl.BlockSpec((tm, half), lambda i: (i, 0)),
            pl.BlockSpec((tm, half), lambda i: (i, 0)),
            pl.BlockSpec((tm, d), lambda i: (i, 0)),
            pl.BlockSpec(w_out.shape, const, pipeline_mode=pl.Buffered(1)),
            pl.BlockSpec((1, d), const),
            pl.BlockSpec(w_router_t.shape, const),
            pl.BlockSpec(b_router.shape, const),
        ],
        out_specs=[
            pl.BlockSpec((tm, d), lambda i: (i, 0)),
            pl.BlockSpec((tm, d), lambda i: (i, 0)),
            pl.BlockSpec((SUBLANES, tm), lambda i: (0, i)),
            pl.BlockSpec((SUBLANES, tm), lambda i: (0, i)),
        ],
        out_shape=[
            jax.ShapeDtypeStruct((n, d), F32),
            jax.ShapeDtypeStruct((n, d), F32),
            jax.ShapeDtypeStruct((SUBLANES, n), I32),
            jax.ShapeDtypeStruct((SUBLANES, n), F32),
        ],
        compiler_params=_cparams(1),
        name="outproj",
    )(mix_nsa, mix_pool, x2d, w_out, ln_moe, w_router_t, b_router)


def _dispatch_kernel(eid_ref, dest_ref, meta_ref, rank_scr):
    n = eid_ref.shape[1]
    t = T_RANK
    row = lax.broadcasted_iota(I32, (N_EXPERTS, t), 0)
    before = (lax.broadcasted_iota(I32, (t, t), 0) < lax.broadcasted_iota(I32, (t, t), 1)).astype(BF16)
    carry = jnp.zeros((N_EXPERTS, 1), F32)
    for k in range(2):
        for j in range(n // t):
            sl = slice(j * t, (j + 1) * t)
            oh = row == eid_ref[k:k + 1, sl]
            ohf = oh.astype(F32)
            prior = _dot(ohf.astype(BF16), before) + carry
            rank_scr[k:k + 1, sl] = jnp.sum(jnp.where(oh, prior, 0.0), axis=0, keepdims=True)
            carry = carry + jnp.sum(ohf, axis=1, keepdims=True)
    counts = jnp.broadcast_to(carry, (N_EXPERTS, LANES))
    padded = jnp.floor((counts + (TM_MOE - 1.0)) * (1.0 / TM_MOE)) * TM_MOE
    ends = padded
    r_idx = lax.broadcasted_iota(I32, (N_EXPERTS, LANES), 0)
    sh = 1
    while sh < N_EXPERTS:
        ends = ends + jnp.where(r_idx >= sh, pltpu.roll(ends, sh, 0), 0.0)
        sh *= 2
    starts = (ends - padded)[:, 0:1]
    for k in range(2):
        for j in range(n // t):
            sl = slice(j * t, (j + 1) * t)
            oh = row == eid_ref[k:k + 1, sl]
            base = jnp.sum(jnp.where(oh, starts, 0.0), axis=0, keepdims=True)
            dest_ref[k:k + 1, sl] = (rank_scr[k:k + 1, sl] + base).astype(I32)
    for k in range(2, SUBLANES):
        dest_ref[k:k + 1, :] = jnp.zeros((1, n), I32)
    blk_start = lax.broadcasted_iota(I32, (N_EXPERTS, LANES), 1).astype(F32) * TM_MOE
    blk_expert = jnp.sum((ends <= blk_start).astype(F32), axis=0, keepdims=True)
    blk_expert = jnp.minimum(blk_expert, N_EXPERTS - 1.0)
    n_used = ends[N_EXPERTS - 1:N_EXPERTS, :] * (1.0 / TM_MOE)
    r8 = lax.broadcasted_iota(I32, (SUBLANES, LANES), 0)
    meta = jnp.where(r8 == 0, blk_expert, jnp.where(r8 == 1, n_used, 0.0))
    meta_ref[...] = meta.astype(I32)


def _dispatch(eid):
    n = eid.shape[1]
    return pl.pallas_call(
        _dispatch_kernel,
        out_shape=[jax.ShapeDtypeStruct((SUBLANES, n), I32),
                   jax.ShapeDtypeStruct((SUBLANES, LANES), I32)],
        scratch_shapes=[pltpu.VMEM((SUBLANES, n), F32)],
        compiler_params=pltpu.CompilerParams(vmem_limit_bytes=VMEM_LIMIT_BYTES),
        name="dispatch",
    )(eid)


def _row_copy(src_ref, src_row, dst_ref, dst_row, sem):
    return pltpu.make_async_copy(src_ref.at[pl.ds(src_row, 1)], dst_ref.at[pl.ds(dst_row, 1)], sem)


def _scatter_kernel(dest_ref, h_ref, init_ref, xs_ref, sem):
    del init_ref
    i = pl.program_id(0)
    n = h_ref.shape[0]
    ts = dest_ref.shape[2]
    tok0 = (i * ts) % n

    def start(r, c):
        _row_copy(h_ref, tok0 + r, xs_ref, dest_ref[0, 0, r], sem).start()
        return c

    lax.fori_loop(0, ts, start, 0)

    def wait(r, c):
        _row_copy(h_ref, 0, xs_ref, 0, sem).wait()
        return c

    lax.fori_loop(0, ts, wait, 0)


def _scatter(dest3, h2, xs_init):
    nsteps, _, ts = dest3.shape
    return pl.pallas_call(
        _scatter_kernel,
        grid=(nsteps,),
        in_specs=[
            pl.BlockSpec((1, 1, ts), lambda i: (i, 0, 0), memory_space=pltpu.SMEM),
            pl.BlockSpec(memory_space=pl.ANY),
            pl.BlockSpec(memory_space=pl.ANY),
        ],
        out_specs=pl.BlockSpec(memory_space=pl.ANY),
        out_shape=jax.ShapeDtypeStruct(xs_init.shape, xs_init.dtype),
        scratch_shapes=[pltpu.SemaphoreType.DMA(())],
        input_output_aliases={2: 0},
        compiler_params=_cparams(1),
        name="scatter",
    )(dest3, h2, xs_init)


def _experts_kernel(meta_ref, xs_ref, wg_ref, wu_ref, wd_ref, ys_ref):
    c = pl.program_id(0)

    @pl.when(c < meta_ref[1, 0])
    def _():
        xb = xs_ref[...].astype(BF16)
        hg = _dot(xb, wg_ref[0].astype(BF16))
        hu = _dot(xb, wu_ref[0].astype(BF16))
        act = (jax.nn.silu(hg) * hu).astype(BF16)
        ys_ref[...] = _dot(act, wd_ref[0].astype(BF16))

    @pl.when(c >= meta_ref[1, 0])
    def _():
        ys_ref[...] = jnp.zeros(ys_ref.shape, F32)


def _experts(meta, xs, w_gate, w_up, w_down):
    cap, d = xs.shape
    nblk = cap // TM_MOE
    ff = w_gate.shape[2]

    def row_map(c, meta):
        return (jnp.minimum(c, meta[1, 0] - 1), 0)

    def w_map(c, meta):
        return (meta[0, jnp.minimum(c, meta[1, 0] - 1)], 0, 0)

    return pl.pallas_call(
        _experts_kernel,
        grid_spec=pltpu.PrefetchScalarGridSpec(
            num_scalar_prefetch=1,
            grid=(nblk,),
            in_specs=[
                pl.BlockSpec((TM_MOE, d), row_map),
                pl.BlockSpec((1, d, ff), w_map),
                pl.BlockSpec((1, d, ff), w_map),
                pl.BlockSpec((1, ff, d), w_map),
            ],
            out_specs=pl.BlockSpec((TM_MOE, d), lambda c, meta: (c, 0)),
        ),
        out_shape=jax.ShapeDtypeStruct((cap, d), F32),
        compiler_params=_cparams(1),
        name="experts",
    )(meta, xs, w_gate, w_up, w_down)


def _combine_kernel(dest_ref, x1_ref, gate_ref, ln_ref, ys_ref, o_ref, ybuf, sem):
    tm = x1_ref.shape[0]

    def start(r, c):
        for k in range(2):
            _row_copy(ys_ref, dest_ref[0, k, r], ybuf.at[k], r, sem).start()
        return c

    lax.fori_loop(0, tm, start, 0)

    def wait(r, c):
        for k in range(2):
            _row_copy(ys_ref, 0, ybuf.at[k], 0, sem).wait()
        return c

    lax.fori_loop(0, tm, wait, 0)
    gate = gate_ref[...]
    moe = ybuf[0] * gate[:, 0:1] + ybuf[1] * gate[:, 1:2]
    x2 = x1_ref[...] + moe
    ms = jnp.mean(x2 * x2, axis=-1, keepdims=True)
    o_ref[...] = x2 * lax.rsqrt(ms + EPS) * ln_ref[...]


def _combine(dest3, x1, gate_rows, ln_final, ys):
    n, d = x1.shape
    tm = TM_COMBINE
    return pl.pallas_call(
        _combine_kernel,
        grid=(n // tm,),
        in_specs=[
            pl.BlockSpec((1, 2, tm), lambda i: (i, 0, 0), memory_space=pltpu.SMEM),
            pl.BlockSpec((tm, d), lambda i: (i, 0)),
            pl.BlockSpec((tm, LANES), lambda i: (i, 0)),
            pl.BlockSpec((1, d), lambda i: (0, 0)),
            pl.BlockSpec(memory_space=pl.ANY),
        ],
        out_specs=pl.BlockSpec((tm, d), lambda i: (i, 0)),
        out_shape=jax.ShapeDtypeStruct((n, d), F32),
        scratch_shapes=[pltpu.VMEM((2, tm, d), F32), pltpu.SemaphoreType.DMA(())],
        compiler_params=_cparams(1),
        name="combine",
    )(dest3, x1, gate_rows, ln_final, ys)


def kernel(x, positions, ln_mix, w_in, pe_cmp_k, w_cmp_k1, w_cmp_k2, pe_cmp_v, w_cmp_v1, w_cmp_v2, w_pool, b_pool, pool_scale, gn_nsa, gn_pool, w_out, ln_moe, w_router_group, b_router_group, w_router_expert, b_router_expert, w_gate, w_up, w_down, ln_final):
    batch, seq, d = x.shape
    n = batch * seq
    depth = w_in.shape[0]
    nsa_w = N_HEADS * HEAD_DIM
    kv_w = N_KV * HEAD_DIM
    cut_gates = nsa_w + 6 * kv_w
    cut_pool = cut_gates + N_GATES

    inv_freq = ROPE_THETA ** (-jnp.arange(0, ROT_DIM, 2, dtype=F32) / ROT_DIM)
    ang = positions.astype(F32).reshape(n, 1) * inv_freq[None, :]
    ang = jnp.concatenate([-ang, ang, jnp.zeros((n, HEAD_DIM - ROT_DIM), F32)], axis=1)
    kblk = (jnp.arange(seq, dtype=I32)[:, None] // SEL_LEN == jnp.arange(LANES, dtype=I32)[None, :]).astype(BF16)

    x2d = x.reshape(n, d)
    assert depth == 1, "single-layer operation"
    for l in range(depth):
        wi = w_in[l]
        wa = wi[:, :cut_gates].astype(BF16)
        wg = jnp.pad(wi[:, cut_gates:cut_pool], ((0, 0), (0, LANES - N_GATES))).astype(BF16)
        wu = wi[:, cut_pool:].astype(BF16)
        qn, qr, kvc, kv8, gates, u = _inproj(x2d, ln_mix[l][None], ang, wa, wg, wu, batch, seq)

        half = CMP_LEN * HEAD_DIM // 2
        def stack_w1(w):
            return jnp.concatenate([w[:half], w[half:]], axis=1)
        w1 = jnp.stack([stack_w1(w_cmp_k1[l]), stack_w1(w_cmp_v1[l])]).astype(BF16)
        w2 = jnp.stack([w_cmp_k2[l], w_cmp_v2[l]]).astype(BF16)
        def pe_rows(pe):
            return jnp.pad(pe.reshape(2, half), ((0, SUBLANES - 2), (0, 0)))
        pe2 = jnp.stack([pe_rows(pe_cmp_k[l]), pe_rows(pe_cmp_v[l])]).astype(BF16)
        kvc2 = kvc.reshape(batch, 4, seq // CMP_STRIDE, CMP_STRIDE * HEAD_DIM)
        cmp = _compress(kvc2, pe2, w1, w2)

        mix_nsa = _nsa(qn, qr, cmp, kv8, kblk, gates, gn_nsa[l][None])
        mix_pool = _pool(u.reshape(batch, seq, -1), w_pool[l].astype(BF16), b_pool[l][None],
                         pool_scale[l][None], gn_pool[l][None])

        ng, ne = N_EXPERT_GROUPS, EXPERTS_PER_GROUP
        w_r = jnp.concatenate([
            w_router_group[l].T, jnp.zeros((ne - ng, d), F32),
            jnp.transpose(w_router_expert[l], (0, 2, 1)).reshape(ng * ne, d),
            jnp.zeros((LANES - ne - ng * ne, d), F32)], axis=0).astype(BF16)
        b_r = jnp.concatenate([
            b_router_group[l], jnp.zeros((ne - ng,), F32), b_router_expert[l].reshape(-1),
            jnp.zeros((LANES - ne - ng * ne,), F32)])
        b_r = jnp.broadcast_to(b_r[:, None], (LANES, LANES))
        x1, h2, eid, gate = _outproj(mix_nsa, mix_pool, x2d, w_out[l].astype(BF16), ln_moe[l][None], w_r, b_r)

        dest, meta = _dispatch(eid)
        cap = (2 * n // TM_MOE + N_EXPERTS) * TM_MOE
        dest_flat = dest[:2].reshape(2 * n // TS_SCATTER, 1, TS_SCATTER)
        xs = _scatter(dest_flat, h2, jnp.zeros((cap, d), F32))
        ys = _experts(meta, xs, w_gate[l], w_up[l], w_down[l])
        dest_tok = jnp.transpose(dest[:2].reshape(2, n // TM_COMBINE, TM_COMBINE), (1, 0, 2))
        gate_rows = jnp.pad(gate[:2].T, ((0, 0), (0, LANES - 2)))
        x2d = _combine(dest_tok, x1, gate_rows, ln_final[None], ys)
    return x2d.reshape(batch, seq, d)
```

```python
import functools

import jax
import jax.numpy as jnp
from jax import lax
from jax.experimental import pallas as pl
from jax.experimental.pallas import tpu as pltpu

F32 = jnp.float32
BF16 = jnp.bfloat16
I32 = jnp.int32

HEAD_DIM = 128
N_HEADS = 8
N_KV = 2
GQA = N_HEADS // N_KV
N_GATES = 3 * N_HEADS
ROT_DIM = HEAD_DIM // 4
ROPE_THETA = 500000.0
CMP_LEN = 32
CMP_STRIDE = 16
CMP_HIDDEN = 2 * HEAD_DIM
SEL_LEN = 64
SEL_TOPK = 16
N_LOCAL = 2
WINDOW = 512
POOL_SIZES = (2, 4, 8, 16)
N_EXPERT_GROUPS = 4
EXPERTS_PER_GROUP = 8
N_EXPERTS = N_EXPERT_GROUPS * EXPERTS_PER_GROUP
EPS = 1e-6
NEG = -1e30
BIG = 1e30
SCALE = HEAD_DIM ** -0.5

LANES = 128
SUBLANES = 8
VMEM_LIMIT_BYTES = 56 * 1024 * 1024

TM_PROJ = 256
TQ = 256
TK = 256
SEL_MASK_BIAS = 32768.0
TM_MOE = 256
TS_SCATTER = 256
TM_COMBINE = 128
T_RANK = 512


def _cparams(n_axes):
    return pltpu.CompilerParams(dimension_semantics=("arbitrary",) * n_axes,
                                vmem_limit_bytes=VMEM_LIMIT_BYTES)


def _dot(a, b):
    return jnp.dot(a, b, preferred_element_type=F32)


def _dot_nt(a, b):
    return lax.dot_general(a, b, (((1,), (1,)), ((), ())), preferred_element_type=F32)


def _inproj_kernel(x_ref, ln_ref, ang_ref, wa_ref, wg_ref, wu_ref,
                   qn_ref, qr_ref, kvc_ref, kv8_ref, gates_ref, u_ref):
    x = x_ref[...]
    ms = jnp.mean(x * x, axis=-1, keepdims=True)
    hb = (x * lax.rsqrt(ms + EPS) * ln_ref[...]).astype(BF16)
    ang = ang_ref[...]
    cos = jnp.cos(ang)
    sin = jnp.sin(ang)
    lane = lax.broadcasted_iota(I32, ang.shape, 1)

    def rope(t):
        partner = jnp.where(lane < ROT_DIM // 2,
                            pltpu.roll(t, HEAD_DIM - ROT_DIM // 2, 1),
                            pltpu.roll(t, ROT_DIM // 2, 1))
        return t * cos + partner * sin

    chunk = 4 * HEAD_DIM
    for ch in range(2):
        r = _dot(hb, wa_ref[:, ch * chunk:(ch + 1) * chunk])
        for j in range(4):
            t = r[:, j * HEAD_DIM:(j + 1) * HEAD_DIM]
            qn_ref[0, ch * 4 + j] = (t * SCALE).astype(BF16)
            qr_ref[0, ch * 4 + j] = (rope(t) * SCALE).astype(BF16)
    base = N_HEADS * HEAD_DIM
    r = _dot(hb, wa_ref[:, base:base + chunk])
    for j in range(4):
        kvc_ref[0, j] = r[:, j * HEAD_DIM:(j + 1) * HEAD_DIM].astype(BF16)
    for ch in range(2):
        r = _dot(hb, wa_ref[:, base + (ch + 1) * chunk:base + (ch + 2) * chunk])
        for j in range(4):
            t = r[:, j * HEAD_DIM:(j + 1) * HEAD_DIM]
            if j < 2:
                t = rope(t)
            kv8_ref[0, ch * 4 + j] = t.astype(BF16)
    gates_ref[...] = _dot(hb, wg_ref[...])
    for ch in range(2):
        u_ref[:, ch * chunk:(ch + 1) * chunk] = _dot(hb, wu_ref[:, ch * chunk:(ch + 1) * chunk])


def _inproj(x2d, ln, ang, wa, wg, wu, batch, seq):
    n, d = x2d.shape
    tm = TM_PROJ
    spb = seq // tm
    const = lambda i: (0, 0)
    head_map = lambda i: (i // spb, 0, i % spb, 0)
    return pl.pallas_call(
        _inproj_kernel,
        grid=(n // tm,),
        in_specs=[
            pl.BlockSpec((tm, d), lambda i: (i, 0)),
            pl.BlockSpec((1, d), const),
            pl.BlockSpec((tm, HEAD_DIM), lambda i: (i, 0)),
            pl.BlockSpec(wa.shape, const, pipeline_mode=pl.Buffered(1)),
            pl.BlockSpec(wg.shape, const, pipeline_mode=pl.Buffered(1)),
            pl.BlockSpec(wu.shape, const, pipeline_mode=pl.Buffered(1)),
        ],
        out_specs=[
            pl.BlockSpec((1, N_HEADS, tm, HEAD_DIM), head_map),
            pl.BlockSpec((1, N_HEADS, tm, HEAD_DIM), head_map),
            pl.BlockSpec((1, 4, tm, HEAD_DIM), head_map),
            pl.BlockSpec((1, 8, tm, HEAD_DIM), head_map),
            pl.BlockSpec((tm, LANES), lambda i: (i, 0)),
            pl.BlockSpec((tm, wu.shape[1]), lambda i: (i, 0)),
        ],
        out_shape=[
            jax.ShapeDtypeStruct((batch, N_HEADS, seq, HEAD_DIM), BF16),
            jax.ShapeDtypeStruct((batch, N_HEADS, seq, HEAD_DIM), BF16),
            jax.ShapeDtypeStruct((batch, 4, seq, HEAD_DIM), BF16),
            jax.ShapeDtypeStruct((batch, 8, seq, HEAD_DIM), BF16),
            jax.ShapeDtypeStruct((n, LANES), F32),
            jax.ShapeDtypeStruct((n, wu.shape[1]), F32),
        ],
        compiler_params=_cparams(1),
        name="inproj",
    )(x2d, ln, ang, wa, wg, wu)


def _compress_kernel(t_ref, pe_ref, w1_ref, w2_ref, o_ref):
    t2 = t_ref[0, 0]
    w1 = w1_ref[0]
    a = _dot(t2, w1)
    pb = _dot(pe_ref[0], w1)
    bias = pb[0:1, :CMP_HIDDEN] + pb[1:2, CMP_HIDDEN:]
    nrow = a.shape[0]
    nxt = pltpu.roll(a[:, CMP_HIDDEN:], nrow - 1, 0)
    hid = jax.nn.gelu(a[:, :CMP_HIDDEN] + nxt + bias)
    out = _dot(hid.astype(BF16), w2_ref[0])
    row = lax.broadcasted_iota(I32, out.shape, 0)
    o_ref[0, 0] = jnp.where(row < nrow - 1, out, 0.0).astype(BF16)


def _compress(kvc2, pe2, w1, w2):
    batch, four, nrow, width = kvc2.shape
    return pl.pallas_call(
        _compress_kernel,
        grid=(batch, four),
        in_specs=[
            pl.BlockSpec((1, 1, nrow, width), lambda b, j: (b, j, 0, 0)),
            pl.BlockSpec((1, SUBLANES, width), lambda b, j: (j // N_KV, 0, 0)),
            pl.BlockSpec((1, width, 2 * CMP_HIDDEN), lambda b, j: (j // N_KV, 0, 0)),
            pl.BlockSpec((1, CMP_HIDDEN, HEAD_DIM), lambda b, j: (j // N_KV, 0, 0)),
        ],
        out_specs=pl.BlockSpec((1, 1, nrow, HEAD_DIM), lambda b, j: (b, j, 0, 0)),
        out_shape=jax.ShapeDtypeStruct((batch, four, nrow, HEAD_DIM), BF16),
        compiler_params=_cparams(2),
        name="compress",
    )(kvc2, pe2, w1, w2)


def _nsa_kernel(qn_ref, qr_ref, cmp_ref, kv_ref, kblk_ref, gates_ref, gn_ref, o_ref,
                m_scr, l_scr, acc_scr, o_scr):
    qi = pl.program_id(1)
    tq = qn_ref.shape[2]
    rows = GQA * tq
    n_cmp_pad = cmp_ref.shape[2]
    n_sel = kv_ref.shape[2] // SEL_LEN
    q0 = qi * tq

    def flash(q_aug, k_of, v_of, lo, hi, mask_of):
        m_scr[...] = jnp.full(m_scr.shape, NEG, F32)
        l_scr[...] = jnp.zeros(l_scr.shape, F32)
        acc_scr[...] = jnp.zeros(acc_scr.shape, F32)

        def body(kj, carry):
            s = _dot_nt(q_aug, k_of(kj))
            ok = mask_of(kj)
            s = jnp.where(ok[None], s.reshape(GQA, tq, TK), NEG).reshape(rows, TK)
            m_old = m_scr[...]
            m_new = jnp.maximum(m_old, jnp.max(s, axis=-1, keepdims=True))
            alpha = jnp.exp(m_old - m_new)
            p = jnp.exp(s - m_new)
            l_scr[...] = alpha * l_scr[...] + jnp.sum(p, axis=-1, keepdims=True)
            acc_scr[...] = alpha * acc_scr[...] + _dot(p.astype(BF16), v_of(kj))
            m_scr[...] = m_new
            return carry

        lax.fori_loop(lo, hi, body, 0)
        return acc_scr[...] / l_scr[...]

    t_row = q0 + lax.broadcasted_iota(I32, (tq, TK), 0)
    k_col = lax.broadcasted_iota(I32, (tq, TK), 1)

    for g in range(N_KV):
        qn = qn_ref[0, g * GQA:(g + 1) * GQA].reshape(rows, HEAD_DIM)
        k_cmp = cmp_ref[0, g]
        v_cmp = cmp_ref[0, N_KV + g]
        s = _dot_nt(qn, k_cmp).reshape(GQA, tq, n_cmp_pad)
        n_idx = lax.broadcasted_iota(I32, (tq, n_cmp_pad), 1)
        t_idx = q0 + lax.broadcasted_iota(I32, (tq, n_cmp_pad), 0)
        cmp_ok = (n_idx * CMP_STRIDE + (CMP_LEN - 1) <= t_idx)[None]
        s = jnp.where(cmp_ok, s, NEG)
        m = jnp.max(s, axis=-1, keepdims=True)
        e = jnp.where(cmp_ok, jnp.exp(s - m), 0.0)
        l = jnp.sum(e, axis=-1, keepdims=True)
        p = e * jnp.where(l > 0.0, 1.0 / l, 0.0)
        pb = p.astype(BF16).reshape(rows, n_cmp_pad)
        o_cmp = _dot(pb, v_cmp)

        jn = lax.broadcasted_iota(I32, (n_sel, n_cmp_pad), 0) * SEL_LEN
        cn = lax.broadcasted_iota(I32, (n_sel, n_cmp_pad), 1) * CMP_STRIDE
        ov = jnp.clip(jnp.minimum(cn + CMP_LEN, jn + SEL_LEN) - jnp.maximum(cn, jn), 0, None)
        w_t = (ov.astype(F32) * (1.0 / CMP_LEN)).astype(BF16)
        imp4 = _dot_nt(w_t, pb)
        imp = imp4[:, 0:tq]
        for r in range(1, GQA):
            imp = imp + imp4[:, r * tq:(r + 1) * tq]
        j_blk = lax.broadcasted_iota(I32, (n_sel, tq), 0)
        t_q = q0 + lax.broadcasted_iota(I32, (n_sel, tq), 1)
        sel_ok = j_blk * SEL_LEN <= t_q
        back = t_q // SEL_LEN - j_blk
        forced = (j_blk == 0) | ((back >= 0) & (back < N_LOCAL))
        val = jnp.where(sel_ok & forced, BIG, jnp.where(sel_ok, imp, -BIG))
        rank = jnp.zeros((n_sel, tq), F32)
        for i in range(n_sel):
            vi = val[i:i + 1, :]
            beats = (vi > val) | ((vi == val) & (j_blk > i))
            rank = rank + beats.astype(F32)
        keep = (rank < float(min(SEL_TOPK, n_sel))) & sel_ok
        bias_t = jnp.where(keep, 0.0, -SEL_MASK_BIAS)
        bias_t = jnp.concatenate([bias_t, jnp.zeros((LANES - n_sel, tq), F32)], axis=0)
        bias = bias_t.T.astype(BF16)
        bias4 = jnp.concatenate([bias] * GQA, axis=0)

        qr = qr_ref[0, g * GQA:(g + 1) * GQA].reshape(rows, HEAD_DIM)
        q_aug = jnp.concatenate([qr, bias4], axis=1)

        def k_sel(kj, g=g):
            st = pl.multiple_of(kj * TK, TK)
            return jnp.concatenate([kv_ref[0, g, pl.ds(st, TK), :], kblk_ref[pl.ds(st, TK), :]], axis=1)

        def v_sel(kj, g=g):
            return kv_ref[0, N_KV + g, pl.ds(pl.multiple_of(kj * TK, TK), TK), :]

        def causal(kj):
            return kj * TK + k_col <= t_row

        o_sel = flash(q_aug, k_sel, v_sel, 0, (q0 + tq) // TK, causal)

        def k_win(kj, g=g):
            return kv_ref[0, 2 * N_KV + g, pl.ds(pl.multiple_of(kj * TK, TK), TK), :]

        def v_win(kj, g=g):
            return kv_ref[0, 3 * N_KV + g, pl.ds(pl.multiple_of(kj * TK, TK), TK), :]

        def band(kj):
            kpos = kj * TK + k_col
            return (kpos <= t_row) & (kpos > t_row - WINDOW)

        lo = jnp.maximum(q0 - WINDOW, 0) // TK
        o_win = flash(qr, k_win, v_win, lo, (q0 + tq) // TK, band)

        gsig = jax.nn.sigmoid(gates_ref[...])
        for r in range(GQA):
            h = g * GQA + r
            sl = slice(r * tq, (r + 1) * tq)
            o_h = (gsig[:, 3 * h:3 * h + 1] * o_cmp[sl]
                   + gsig[:, 3 * h + 1:3 * h + 2] * o_sel[sl]
                   + gsig[:, 3 * h + 2:3 * h + 3] * o_win[sl])
            o_scr[:, h * HEAD_DIM:(h + 1) * HEAD_DIM] = o_h

    o = o_scr[...]
    ms = jnp.mean(o * o, axis=-1, keepdims=True)
    o_ref[...] = (o * lax.rsqrt(ms + EPS) * gn_ref[...]).astype(BF16)


def _nsa(qn, qr, cmp, kv8, kblk, gates, gn):
    batch, _, seq, _ = qn.shape
    nq = seq // TQ
    width = N_HEADS * HEAD_DIM
    return pl.pallas_call(
        _nsa_kernel,
        grid=(batch, nq),
        in_specs=[
            pl.BlockSpec((1, N_HEADS, TQ, HEAD_DIM), lambda b, i: (b, 0, i, 0)),
            pl.BlockSpec((1, N_HEADS, TQ, HEAD_DIM), lambda b, i: (b, 0, i, 0)),
            pl.BlockSpec((1, 4, cmp.shape[2], HEAD_DIM), lambda b, i: (b, 0, 0, 0)),
            pl.BlockSpec((1, 8, seq, HEAD_DIM), lambda b, i: (b, 0, 0, 0)),
            pl.BlockSpec((seq, LANES), lambda b, i: (0, 0)),
            pl.BlockSpec((TQ, LANES), lambda b, i: (b * nq + i, 0)),
            pl.BlockSpec((1, width), lambda b, i: (0, 0)),
        ],
        out_specs=pl.BlockSpec((TQ, width), lambda b, i: (b * nq + i, 0)),
        out_shape=jax.ShapeDtypeStruct((batch * seq, width), BF16),
        scratch_shapes=[
            pltpu.VMEM((GQA * TQ, 1), F32),
            pltpu.VMEM((GQA * TQ, 1), F32),
            pltpu.VMEM((GQA * TQ, HEAD_DIM), F32),
            pltpu.VMEM((TQ, width), F32),
        ],
        compiler_params=_cparams(2),
        name="nsa",
    )(qn, qr, cmp, kv8, kblk, gates, gn)


def _pool_kernel(u_ref, up_ref, w_ref, b_ref, sc_ref, gn_ref, o_ref, y_scr):
    si = pl.program_id(1)
    tm = u_ref.shape[1]
    halo = max(POOL_SIZES)
    cur = u_ref[0]
    prev = jnp.where(si > 0, up_ref[0, tm - halo:, :], 0.0)
    ext = jnp.concatenate([prev, cur], axis=0)
    t1 = (si * tm + 1 + lax.broadcasted_iota(I32, (tm, 1), 0)).astype(F32)
    cg = cur.shape[1] // len(POOL_SIZES)
    for gi, w in enumerate(POOL_SIZES):
        sl = slice(gi * cg, (gi + 1) * cg)
        acc = ext[:, sl]
        span = 1
        while span < w:
            acc = acc + jnp.concatenate([jnp.zeros((span, cg), F32), acc[:-span]], axis=0)
            span *= 2
        mean = acc[halo:] / jnp.minimum(t1, float(w))
        d = (mean - cur[:, sl]).astype(BF16)
        y = _dot(d, w_ref[gi]) + b_ref[:, sl]
        y_scr[:, sl] = y * sc_ref[:, sl]
    y = y_scr[...]
    ms = jnp.mean(y * y, axis=-1, keepdims=True)
    o_ref[...] = (y * lax.rsqrt(ms + EPS) * gn_ref[...]).astype(BF16)


def _pool(u3, w_pool, b_pool, pool_scale, gn_pool):
    batch, seq, c = u3.shape
    tm = TM_PROJ
    ns = seq // tm
    vec = lambda b, i: (0, 0)
    return pl.pallas_call(
        _pool_kernel,
        grid=(batch, ns),
        in_specs=[
            pl.BlockSpec((1, tm, c), lambda b, i: (b, i, 0)),
            pl.BlockSpec((1, tm, c), lambda b, i: (b, jnp.maximum(i - 1, 0), 0)),
            pl.BlockSpec(w_pool.shape, lambda b, i: (0, 0, 0)),
            pl.BlockSpec((1, c), vec),
            pl.BlockSpec((1, c), vec),
            pl.BlockSpec((1, c), vec),
        ],
        out_specs=pl.BlockSpec((tm, c), lambda b, i: (b * ns + i, 0)),
        out_shape=jax.ShapeDtypeStruct((batch * seq, c), BF16),
        scratch_shapes=[pltpu.VMEM((tm, c), F32)],
        compiler_params=_cparams(2),
        name="pool",
    )(u3, u3, w_pool, b_pool, pool_scale, gn_pool)


def _outproj_kernel(mn_ref, mp_ref, x_ref, w_ref, ln_ref, wr_ref, br_ref,
                    x1_ref, h2_ref, eid_ref, gate_ref):
    tm, d = x_ref.shape
    half = mn_ref.shape[1]
    chunk = 512
    mn = mn_ref[...]
    mp = mp_ref[...]
    ssq = jnp.zeros((tm, 1), F32)
    for c in range(d // chunk):
        sl = slice(c * chunk, (c + 1) * chunk)
        x1 = x_ref[:, sl] + (_dot(mn, w_ref[:half, sl]) + _dot(mp, w_ref[half:, sl]))
        x1_ref[:, sl] = x1
        ssq = ssq + jnp.sum(x1 * x1, axis=-1, keepdims=True)
    h2 = x1_ref[...] * lax.rsqrt(ssq * (1.0 / d) + EPS) * ln_ref[...]
    h2_ref[...] = h2

    logits = _dot_nt(wr_ref[...], h2.astype(BF16)) + br_ref[:, 0:1]
    ng, ne = N_EXPERT_GROUPS, EXPERTS_PER_GROUP
    row = lax.broadcasted_iota(I32, (ne, tm), 0)
    lg = jnp.where(row < ng, logits[0:ne], NEG)
    mg = jnp.max(lg, axis=0, keepdims=True)
    g_sel = jnp.min(jnp.where(lg == mg, row, ne), axis=0, keepdims=True)
    p_g = 1.0 / jnp.sum(jnp.exp(lg - mg), axis=0, keepdims=True)
    le = jnp.zeros((ne, tm), F32)
    for g in range(ng):
        le = jnp.where(g_sel == g, logits[ne * (g + 1):ne * (g + 2)], le)
    ex = jnp.exp(le - jnp.max(le, axis=0, keepdims=True))
    pe = ex / jnp.sum(ex, axis=0, keepdims=True)
    p1 = jnp.max(pe, axis=0, keepdims=True)
    i1 = jnp.min(jnp.where(pe == p1, row, ne), axis=0, keepdims=True)
    rest = jnp.where(row == i1, -1.0, pe)
    p2 = jnp.max(rest, axis=0, keepdims=True)
    i2 = jnp.min(jnp.where(rest == p2, row, ne), axis=0, keepdims=True)
    psum = p1 + p2
    zero_i = jnp.zeros((ne, tm), I32)
    eid = jnp.where(row == 0, g_sel * ne + i1, jnp.where(row == 1, g_sel * ne + i2, zero_i))
    gate = jnp.where(row == 0, p_g * p1 / psum, jnp.where(row == 1, p_g * p2 / psum, 0.0))
    eid_ref[...] = eid
    gate_ref[...] = gate


def _outproj(mix_nsa, mix_pool, x2d, w_out, ln_moe, w_router_t, b_router):
    n, d = x2d.shape
    tm = TM_PROJ
    half = mix_nsa.shape[1]
    const = lambda i: (0, 0)
    return pl.pallas_call(
        _outproj_kernel,
        grid=(n // tm,),
        in_specs=[
            pl.BlockSpec((tm, half), lambda i: (i, 0)),
            pl.BlockSpec((tm, half), lambda i: (i, 0)),
            pl.BlockSpec((tm, d), lambda i: (i, 0)),
            pl.BlockSpec(w_out.shape, const, pipeline_mode=pl.Buffered(1)),
            pl.BlockSpec((1, d), const),
            pl.BlockSpec(w_router_t.shape, const),
            pl.BlockSpec(b_router.shape, const),
        ],
        out_specs=[
            pl.BlockSpec((tm, d), lambda i: (i, 0)),
            pl.BlockSpec((tm, d), lambda i: (i, 0)),
            pl.BlockSpec((SUBLANES, tm), lambda i: (0, i)),
            pl.BlockSpec((SUBLANES, tm), lambda i: (0, i)),
        ],
        out_shape=[
            jax.ShapeDtypeStruct((n, d), F32),
            jax.ShapeDtypeStruct((n, d), F32),
            jax.ShapeDtypeStruct((SUBLANES, n), I32),
            jax.ShapeDtypeStruct((SUBLANES, n), F32),
        ],
        compiler_params=_cparams(1),
        name="outproj",
    )(mix_nsa, mix_pool, x2d, w_out, ln_moe, w_router_t, b_router)


def _dispatch_kernel(eid_ref, dest_ref, meta_ref, rank_scr):
    n = eid_ref.shape[1]
    t = T_RANK
    row = lax.broadcasted_iota(I32, (N_EXPERTS, t), 0)
    before = (lax.broadcasted_iota(I32, (t, t), 0) < lax.broadcasted_iota(I32, (t, t), 1)).astype(BF16)
    carry = jnp.zeros((N_EXPERTS, 1), F32)
    for k in range(2):
        for j in range(n // t):
            sl = slice(j * t, (j + 1) * t)
            oh = row == eid_ref[k:k + 1, sl]
            ohf = oh.astype(F32)
            prior = _dot(ohf.astype(BF16), before) + carry
            rank_scr[k:k + 1, sl] = jnp.sum(jnp.where(oh, prior, 0.0), axis=0, keepdims=True)
            carry = carry + jnp.sum(ohf, axis=1, keepdims=True)
    counts = jnp.broadcast_to(carry, (N_EXPERTS, LANES))
    padded = jnp.floor((counts + (TM_MOE - 1.0)) * (1.0 / TM_MOE)) * TM_MOE
    ends = padded
    r_idx = lax.broadcasted_iota(I32, (N_EXPERTS, LANES), 0)
    sh = 1
    while sh < N_EXPERTS:
        ends = ends + jnp.where(r_idx >= sh, pltpu.roll(ends, sh, 0), 0.0)
        sh *= 2
    starts = (ends - padded)[:, 0:1]
    for k in range(2):
        for j in range(n // t):
            sl = slice(j * t, (j + 1) * t)
            oh = row == eid_ref[k:k + 1, sl]
            base = jnp.sum(jnp.where(oh, starts, 0.0), axis=0, keepdims=True)
            dest_ref[k:k + 1, sl] = (rank_scr[k:k + 1, sl] + base).astype(I32)
    for k in range(2, SUBLANES):
        dest_ref[k:k + 1, :] = jnp.zeros((1, n), I32)
    blk_start = lax.broadcasted_iota(I32, (N_EXPERTS, LANES), 1).astype(F32) * TM_MOE
    blk_expert = jnp.sum((ends <= blk_start).astype(F32), axis=0, keepdims=True)
    blk_expert = jnp.minimum(blk_expert, N_EXPERTS - 1.0)
    n_used = ends[N_EXPERTS - 1:N_EXPERTS, :] * (1.0 / TM_MOE)
    r8 = lax.broadcasted_iota(I32, (SUBLANES, LANES), 0)
    meta = jnp.where(r8 == 0, blk_expert, jnp.where(r8 == 1, n_used, 0.0))
    meta_ref[...] = meta.astype(I32)


def _dispatch(eid):
    n = eid.shape[1]
    return pl.pallas_call(
        _dispatch_kernel,
        out_shape=[jax.ShapeDtypeStruct((SUBLANES, n), I32),
                   jax.ShapeDtypeStruct((SUBLANES, LANES), I32)],
        scratch_shapes=[pltpu.VMEM((SUBLANES, n), F32)],
        compiler_params=pltpu.CompilerParams(vmem_limit_bytes=VMEM_LIMIT_BYTES),
        name="dispatch",
    )(eid)


def _row_copy(src_ref, src_row, dst_ref, dst_row, sem):
    return pltpu.make_async_copy(src_ref.at[pl.ds(src_row, 1)], dst_ref.at[pl.ds(dst_row, 1)], sem)


def _scatter_kernel(dest_ref, h_ref, init_ref, xs_ref, sem):
    del init_ref
    tm = h_ref.shape[0]

    def start(r, c):
        for k in range(2):
            _row_copy(h_ref, r, xs_ref, dest_ref[0, k, r], sem).start()
        return c

    lax.fori_loop(0, tm, start, 0)

    def wait(r, c):
        for k in range(2):
            _row_copy(h_ref, 0, xs_ref, 0, sem).wait()
        return c

    lax.fori_loop(0, tm, wait, 0)


def _scatter(dest3, h2, xs_init):
    nsteps, _, tm = dest3.shape
    d = h2.shape[1]
    return pl.pallas_call(
        _scatter_kernel,
        grid=(nsteps,),
        in_specs=[
            pl.BlockSpec((1, 2, tm), lambda i: (i, 0, 0), memory_space=pltpu.SMEM),
            pl.BlockSpec((tm, d), lambda i: (i, 0)),
            pl.BlockSpec(memory_space=pl.ANY),
        ],
        out_specs=pl.BlockSpec(memory_space=pl.ANY),
        out_shape=jax.ShapeDtypeStruct(xs_init.shape, xs_init.dtype),
        scratch_shapes=[pltpu.SemaphoreType.DMA(())],
        input_output_aliases={2: 0},
        compiler_params=_cparams(1),
        name="scatter",
    )(dest3, h2, xs_init)


def _experts_kernel(meta_ref, xs_ref, wg_ref, wu_ref, wd_ref, ys_ref):
    c = pl.program_id(0)

    @pl.when(c < meta_ref[1, 0])
    def _():
        xb = xs_ref[...].astype(BF16)
        hg = _dot(xb, wg_ref[0].astype(BF16))
        hu = _dot(xb, wu_ref[0].astype(BF16))
        act = (jax.nn.silu(hg) * hu).astype(BF16)
        ys_ref[...] = _dot(act, wd_ref[0].astype(BF16))

    @pl.when(c >= meta_ref[1, 0])
    def _():
        ys_ref[...] = jnp.zeros(ys_ref.shape, F32)


def _experts(meta, xs, w_gate, w_up, w_down):
    cap, d = xs.shape
    nblk = cap // TM_MOE
    ff = w_gate.shape[2]

    def row_map(c, meta):
        return (jnp.minimum(c, meta[1, 0] - 1), 0)

    def w_map(c, meta):
        return (meta[0, jnp.minimum(c, meta[1, 0] - 1)], 0, 0)

    return pl.pallas_call(
        _experts_kernel,
        grid_spec=pltpu.PrefetchScalarGridSpec(
            num_scalar_prefetch=1,
            grid=(nblk,),
            in_specs=[
                pl.BlockSpec((TM_MOE, d), row_map),
                pl.BlockSpec((1, d, ff), w_map),
                pl.BlockSpec((1, d, ff), w_map),
                pl.BlockSpec((1, ff, d), w_map),
            ],
            out_specs=pl.BlockSpec((TM_MOE, d), lambda c, meta: (c, 0)),
        ),
        out_shape=jax.ShapeDtypeStruct((cap, d), F32),
        compiler_params=_cparams(1),
        name="experts",
    )(meta, xs, w_gate, w_up, w_down)


def _combine_kernel(dest_ref, x1_ref, gate_ref, ln_ref, ys_ref, o_ref, ybuf, sem):
    tm = x1_ref.shape[0]

    def start(r, c):
        for k in range(2):
            _row_copy(ys_ref, dest_ref[0, k, r], ybuf.at[k], r, sem).start()
        return c

    lax.fori_loop(0, tm, start, 0)

    def wait(r, c):
        for k in range(2):
            _row_copy(ys_ref, 0, ybuf.at[k], 0, sem).wait()
        return c

    lax.fori_loop(0, tm, wait, 0)
    gate = gate_ref[...]
    moe = ybuf[0] * gate[:, 0:1] + ybuf[1] * gate[:, 1:2]
    x2 = x1_ref[...] + moe
    ms = jnp.mean(x2 * x2, axis=-1, keepdims=True)
    o_ref[...] = x2 * lax.rsqrt(ms + EPS) * ln_ref[...]


def _combine(dest3, x1, gate_rows, ln_final, ys):
    n, d = x1.shape
    tm = TM_COMBINE
    return pl.pallas_call(
        _combine_kernel,
        grid=(n // tm,),
        in_specs=[
            pl.BlockSpec((1, 2, tm), lambda i: (i, 0, 0), memory_space=pltpu.SMEM),
            pl.BlockSpec((tm, d), lambda i: (i, 0)),
            pl.BlockSpec((tm, LANES), lambda i: (i, 0)),
            pl.BlockSpec((1, d), lambda i: (0, 0)),
            pl.BlockSpec(memory_space=pl.ANY),
        ],
        out_specs=pl.BlockSpec((tm, d), lambda i: (i, 0)),
        out_shape=jax.ShapeDtypeStruct((n, d), F32),
        scratch_shapes=[pltpu.VMEM((2, tm, d), F32), pltpu.SemaphoreType.DMA(())],
        compiler_params=_cparams(1),
        name="combine",
    )(dest3, x1, gate_rows, ln_final, ys)


def kernel(x, positions, ln_mix, w_in, pe_cmp_k, w_cmp_k1, w_cmp_k2, pe_cmp_v, w_cmp_v1, w_cmp_v2, w_pool, b_pool, pool_scale, gn_nsa, gn_pool, w_out, ln_moe, w_router_group, b_router_group, w_router_expert, b_router_expert, w_gate, w_up, w_down, ln_final):
    batch, seq, d = x.shape
    n = batch * seq
    depth = w_in.shape[0]
    nsa_w = N_HEADS * HEAD_DIM
    kv_w = N_KV * HEAD_DIM
    cut_gates = nsa_w + 6 * kv_w
    cut_pool = cut_gates + N_GATES

    inv_freq = ROPE_THETA ** (-jnp.arange(0, ROT_DIM, 2, dtype=F32) / ROT_DIM)
    ang = positions.astype(F32).reshape(n, 1) * inv_freq[None, :]
    ang = jnp.concatenate([-ang, ang, jnp.zeros((n, HEAD_DIM - ROT_DIM), F32)], axis=1)
    kblk = (jnp.arange(seq, dtype=I32)[:, None] // SEL_LEN == jnp.arange(LANES, dtype=I32)[None, :]).astype(BF16)

    x2d = x.reshape(n, d)
    assert depth == 1, "single-layer operation"
    for l in range(depth):
        wi = w_in[l]
        wa = wi[:, :cut_gates].astype(BF16)
        wg = jnp.pad(wi[:, cut_gates:cut_pool], ((0, 0), (0, LANES - N_GATES))).astype(BF16)
        wu = wi[:, cut_pool:].astype(BF16)
        qn, qr, kvc, kv8, gates, u = _inproj(x2d, ln_mix[l][None], ang, wa, wg, wu, batch, seq)

        half = CMP_LEN * HEAD_DIM // 2
        def stack_w1(w):
            return jnp.concatenate([w[:half], w[half:]], axis=1)
        w1 = jnp.stack([stack_w1(w_cmp_k1[l]), stack_w1(w_cmp_v1[l])]).astype(BF16)
        w2 = jnp.stack([w_cmp_k2[l], w_cmp_v2[l]]).astype(BF16)
        def pe_rows(pe):
            return jnp.pad(pe.reshape(2, half), ((0, SUBLANES - 2), (0, 0)))
        pe2 = jnp.stack([pe_rows(pe_cmp_k[l]), pe_rows(pe_cmp_v[l])]).astype(BF16)
        kvc2 = kvc.reshape(batch, 4, seq // CMP_STRIDE, CMP_STRIDE * HEAD_DIM)
        cmp = _compress(kvc2, pe2, w1, w2)

        mix_nsa = _nsa(qn, qr, cmp, kv8, kblk, gates, gn_nsa[l][None])
        mix_pool = _pool(u.reshape(batch, seq, -1), w_pool[l].astype(BF16), b_pool[l][None],
                         pool_scale[l][None], gn_pool[l][None])

        ng, ne = N_EXPERT_GROUPS, EXPERTS_PER_GROUP
        w_r = jnp.concatenate([
            w_router_group[l].T, jnp.zeros((ne - ng, d), F32),
            jnp.transpose(w_router_expert[l], (0, 2, 1)).reshape(ng * ne, d),
            jnp.zeros((LANES - ne - ng * ne, d), F32)], axis=0).astype(BF16)
        b_r = jnp.concatenate([
            b_router_group[l], jnp.zeros((ne - ng,), F32), b_router_expert[l].reshape(-1),
            jnp.zeros((LANES - ne - ng * ne,), F32)])
        b_r = jnp.broadcast_to(b_r[:, None], (LANES, LANES))
        x1, h2, eid, gate = _outproj(mix_nsa, mix_pool, x2d, w_out[l].astype(BF16), ln_moe[l][None], w_r, b_r)

        dest, meta = _dispatch(eid)
        cap = (2 * n // TM_MOE + N_EXPERTS) * TM_MOE
        def per_tile(tm):
            return jnp.transpose(dest[:2].reshape(2, n // tm, tm), (1, 0, 2))
        xs = _scatter(per_tile(TS_SCATTER), h2, jnp.zeros((cap, d), F32))
        ys = _experts(meta, xs, w_gate[l], w_up[l], w_down[l])
        dest_tok = per_tile(TM_COMBINE)
        gate_rows = jnp.pad(gate[:2].T, ((0, 0), (0, LANES - 2)))
        x2d = _combine(dest_tok, x1, gate_rows, ln_final[None], ys)
    return x2d.reshape(batch, seq, d)
```

```python
import functools

import jax
import jax.numpy as jnp
from jax import lax
from jax.experimental import pallas as pl
from jax.experimental.pallas import tpu as pltpu

F32 = jnp.float32
BF16 = jnp.bfloat16
I32 = jnp.int32

HEAD_DIM = 128
N_HEADS = 8
N_KV = 2
GQA = N_HEADS // N_KV
N_GATES = 3 * N_HEADS
ROT_DIM = HEAD_DIM // 4
ROPE_THETA = 500000.0
CMP_LEN = 32
CMP_STRIDE = 16
CMP_HIDDEN = 2 * HEAD_DIM
SEL_LEN = 64
SEL_TOPK = 16
N_LOCAL = 2
WINDOW = 512
POOL_SIZES = (2, 4, 8, 16)
N_EXPERT_GROUPS = 4
EXPERTS_PER_GROUP = 8
N_EXPERTS = N_EXPERT_GROUPS * EXPERTS_PER_GROUP
EPS = 1e-6
NEG = -1e30
BIG = 1e30
SCALE = HEAD_DIM ** -0.5

LANES = 128
SUBLANES = 8
VMEM_LIMIT_BYTES = 56 * 1024 * 1024

TM_PROJ = 256
TQ = 256
TK = 256
SEL_MASK_BIAS = 32768.0
TM_MOE = 256
TS_SCATTER = 256
TM_COMBINE = 128
T_RANK = 512


def _cparams(n_axes):
    return pltpu.CompilerParams(dimension_semantics=("arbitrary",) * n_axes,
                                vmem_limit_bytes=VMEM_LIMIT_BYTES)


def _dot(a, b):
    return jnp.dot(a, b, preferred_element_type=F32)


def _dot_nt(a, b):
    return lax.dot_general(a, b, (((1,), (1,)), ((), ())), preferred_element_type=F32)


def _rope(t, cos, sin, axis):
    idx = lax.broadcasted_iota(I32, t.shape, axis)
    partner = jnp.where(idx < ROT_DIM // 2,
                        pltpu.roll(t, HEAD_DIM - ROT_DIM // 2, axis),
                        pltpu.roll(t, ROT_DIM // 2, axis))
    return t * cos + partner * sin


def _inproj_kernel(x_ref, ln_ref, ang_ref, angt_ref, wqt_ref, wvt_ref, wgt_ref, wkc_ref, wk_ref, wu_ref,
                   qnt_ref, qrt_ref, kvc_ref, kk_ref, vt_ref, gt_ref, u_ref):
    x = x_ref[...]
    ms = jnp.mean(x * x, axis=-1, keepdims=True)
    hb = (x * lax.rsqrt(ms + EPS) * ln_ref[...]).astype(BF16)
    chunk = 4 * HEAD_DIM

    angt = angt_ref[...]
    cost, sint = jnp.cos(angt), jnp.sin(angt)
    for ch in range(2):
        rt = _dot_nt(wqt_ref[ch * chunk:(ch + 1) * chunk, :], hb)
        for j in range(4):
            t = rt[j * HEAD_DIM:(j + 1) * HEAD_DIM, :]
            qnt_ref[0, ch * 4 + j] = (t * SCALE).astype(BF16)
            qrt_ref[0, ch * 4 + j] = (_rope(t, cost, sint, 0) * SCALE).astype(BF16)
    rt = _dot_nt(wvt_ref[...], hb)
    for j in range(4):
        vt_ref[0, j, 0] = rt[j * HEAD_DIM:(j + 1) * HEAD_DIM, :].astype(BF16)
    gt_ref[...] = _dot_nt(wgt_ref[...], hb)

    r = _dot(hb, wkc_ref[...])
    for j in range(4):
        kvc_ref[0, j] = r[:, j * HEAD_DIM:(j + 1) * HEAD_DIM].astype(BF16)
    ang = ang_ref[...]
    cos, sin = jnp.cos(ang), jnp.sin(ang)
    r = _dot(hb, wk_ref[...])
    for j in range(4):
        kk_ref[0, j] = _rope(r[:, j * HEAD_DIM:(j + 1) * HEAD_DIM], cos, sin, 1).astype(BF16)
    for ch in range(2):
        u_ref[:, ch * chunk:(ch + 1) * chunk] = _dot(hb, wu_ref[:, ch * chunk:(ch + 1) * chunk])


def _inproj(x2d, ln, ang, angt, wqt, wvt, wgt, wkc, wk, wu, batch, seq):
    n, d = x2d.shape
    tm = TM_PROJ
    assert tm == TK
    spb = seq // tm
    const = lambda i: (0, 0)
    resident = lambda w: pl.BlockSpec(w.shape, const, pipeline_mode=pl.Buffered(1))
    row_heads = lambda i: (i // spb, 0, i % spb, 0)
    col_heads = lambda i: (i // spb, 0, 0, i % spb)
    return pl.pallas_call(
        _inproj_kernel,
        grid=(n // tm,),
        in_specs=[
            pl.BlockSpec((tm, d), lambda i: (i, 0)),
            pl.BlockSpec((1, d), const),
            pl.BlockSpec((tm, HEAD_DIM), lambda i: (i, 0)),
            pl.BlockSpec((HEAD_DIM, tm), lambda i: (0, i)),
            resident(wqt), resident(wvt), resident(wgt), resident(wkc), resident(wk), resident(wu),
        ],
        out_specs=[
            pl.BlockSpec((1, N_HEADS, HEAD_DIM, tm), col_heads),
            pl.BlockSpec((1, N_HEADS, HEAD_DIM, tm), col_heads),
            pl.BlockSpec((1, 4, tm, HEAD_DIM), row_heads),
            pl.BlockSpec((1, 4, tm, HEAD_DIM), row_heads),
            pl.BlockSpec((1, 4, 1, HEAD_DIM, tm), lambda i: (i // spb, 0, i % spb, 0, 0)),
            pl.BlockSpec((wgt.shape[0], tm), lambda i: (0, i)),
            pl.BlockSpec((tm, wu.shape[1]), lambda i: (i, 0)),
        ],
        out_shape=[
            jax.ShapeDtypeStruct((batch, N_HEADS, HEAD_DIM, seq), BF16),
            jax.ShapeDtypeStruct((batch, N_HEADS, HEAD_DIM, seq), BF16),
            jax.ShapeDtypeStruct((batch, 4, seq, HEAD_DIM), BF16),
            jax.ShapeDtypeStruct((batch, 4, seq, HEAD_DIM), BF16),
            jax.ShapeDtypeStruct((batch, 4, spb, HEAD_DIM, tm), BF16),
            jax.ShapeDtypeStruct((wgt.shape[0], n), F32),
            jax.ShapeDtypeStruct((n, wu.shape[1]), F32),
        ],
        compiler_params=_cparams(1),
        name="inproj",
    )(x2d, ln, ang, angt, wqt, wvt, wgt, wkc, wk, wu)


def _compress_kernel(t_ref, pe_ref, w1_ref, w2_ref, o_ref):
    t2 = t_ref[0, 0]
    w1 = w1_ref[0]
    a = _dot(t2, w1)
    pb = _dot(pe_ref[0], w1)
    bias = pb[0:1, :CMP_HIDDEN] + pb[1:2, CMP_HIDDEN:]
    nrow = a.shape[0]
    nxt = pltpu.roll(a[:, CMP_HIDDEN:], nrow - 1, 0)
    hid = jax.nn.gelu(a[:, :CMP_HIDDEN] + nxt + bias)
    out = _dot(hid.astype(BF16), w2_ref[0])
    row = lax.broadcasted_iota(I32, out.shape, 0)
    out = jnp.where(row < nrow - 1, out, 0.0)
    is_value = pl.program_id(1) >= N_KV
    o_ref[0, 0] = jnp.where(is_value, out.T, out).astype(BF16)


def _compress(kvc2, pe2, w1, w2):
    batch, four, nrow, width = kvc2.shape
    return pl.pallas_call(
        _compress_kernel,
        grid=(batch, four),
        in_specs=[
            pl.BlockSpec((1, 1, nrow, width), lambda b, j: (b, j, 0, 0)),
            pl.BlockSpec((1, SUBLANES, width), lambda b, j: (j // N_KV, 0, 0)),
            pl.BlockSpec((1, width, 2 * CMP_HIDDEN), lambda b, j: (j // N_KV, 0, 0)),
            pl.BlockSpec((1, CMP_HIDDEN, HEAD_DIM), lambda b, j: (j // N_KV, 0, 0)),
        ],
        out_specs=pl.BlockSpec((1, 1, nrow, HEAD_DIM), lambda b, j: (b, j, 0, 0)),
        out_shape=jax.ShapeDtypeStruct((batch, four, nrow, HEAD_DIM), BF16),
        compiler_params=_cparams(2),
        name="compress",
    )(kvc2, pe2, w1, w2)


def _nsa_kernel(qnt_ref, qrt_ref, cmp_ref, kk_ref, vt_ref, kblk_ref, gt_ref, gn_ref, o_ref,
                m_scr, l_scr, acc_scr, o_scr):
    qi = pl.program_id(1)
    tq = qnt_ref.shape[3]
    rows = GQA * tq
    n_cmp_pad = cmp_ref.shape[2]
    n_sel = kk_ref.shape[2] // SEL_LEN
    q0 = qi * tq

    def flash_init():
        m_scr[...] = jnp.full(m_scr.shape, NEG, F32)
        l_scr[...] = jnp.zeros(l_scr.shape, F32)
        acc_scr[...] = jnp.zeros(acc_scr.shape, F32)

    def flash_step(q_t, k_tile, v_t, bias):
        s = _dot(k_tile, q_t)
        if bias is not None:
            s = s + bias
        m_old = m_scr[...]
        m_new = jnp.maximum(m_old, jnp.max(s, axis=0, keepdims=True))
        alpha = jnp.exp(m_old - m_new)
        p = jnp.exp(s - m_new)
        l_scr[...] = alpha * l_scr[...] + jnp.sum(p, axis=0, keepdims=True)
        acc_scr[...] = alpha * acc_scr[...] + _dot(v_t, p.astype(BF16))
        m_scr[...] = m_new

    def flash_out():
        return acc_scr[...] / l_scr[...]

    def tile_heads(a):
        return jnp.concatenate([a] * GQA, axis=1)

    kk_i = lax.broadcasted_iota(I32, (TK, tq), 0)
    qq_i = lax.broadcasted_iota(I32, (TK, tq), 1)
    causal_bias = tile_heads(jnp.where(kk_i <= qq_i, 0.0, NEG))
    band_bias = tile_heads(jnp.where(kk_i > qq_i, 0.0, NEG))
    n_win = WINDOW // TK

    for g in range(N_KV):
        qn4 = jnp.concatenate([qnt_ref[0, h] for h in range(g * GQA, (g + 1) * GQA)], axis=1)
        s = _dot(cmp_ref[0, g], qn4)
        n_idx = lax.broadcasted_iota(I32, (n_cmp_pad, rows), 0)
        t_idx = q0 + jnp.bitwise_and(lax.broadcasted_iota(I32, (n_cmp_pad, rows), 1), tq - 1)
        cmp_ok = n_idx * CMP_STRIDE + (CMP_LEN - 1) <= t_idx
        s = jnp.where(cmp_ok, s, NEG)
        m = jnp.max(s, axis=0, keepdims=True)
        e = jnp.where(cmp_ok, jnp.exp(s - m), 0.0)
        l = jnp.sum(e, axis=0, keepdims=True)
        p = e * jnp.where(l > 0.0, 1.0 / l, 0.0)
        pb = p.astype(BF16)
        o_cmp = _dot(cmp_ref[0, N_KV + g], pb)

        jn = lax.broadcasted_iota(I32, (n_sel, n_cmp_pad), 0) * SEL_LEN
        cn = lax.broadcasted_iota(I32, (n_sel, n_cmp_pad), 1) * CMP_STRIDE
        ov = jnp.clip(jnp.minimum(cn + CMP_LEN, jn + SEL_LEN) - jnp.maximum(cn, jn), 0, None)
        w_t = (ov.astype(F32) * (1.0 / CMP_LEN)).astype(BF16)
        imp4 = _dot(w_t, pb)
        imp = imp4[:, 0:tq]
        for r in range(1, GQA):
            imp = imp + imp4[:, r * tq:(r + 1) * tq]
        j_blk = lax.broadcasted_iota(I32, (n_sel, tq), 0)
        t_q = q0 + lax.broadcasted_iota(I32, (n_sel, tq), 1)
        sel_ok = j_blk * SEL_LEN <= t_q
        back = t_q // SEL_LEN - j_blk
        forced = (j_blk == 0) | ((back >= 0) & (back < N_LOCAL))
        val = jnp.where(sel_ok & forced, BIG, jnp.where(sel_ok, imp, -BIG))
        rank = jnp.zeros((n_sel, tq), F32)
        for i in range(n_sel):
            vi = val[i:i + 1, :]
            beats = (vi > val) | ((vi == val) & (j_blk > i))
            rank = rank + beats.astype(F32)
        keep = (rank < float(min(SEL_TOPK, n_sel))) & sel_ok
        bias_t = jnp.where(keep, 0.0, -SEL_MASK_BIAS)
        bias_t = jnp.concatenate([bias_t, jnp.zeros((HEAD_DIM - n_sel, tq), F32)], axis=0)
        bias4 = tile_heads(bias_t.astype(BF16))

        qr4 = jnp.concatenate([qrt_ref[0, h] for h in range(g * GQA, (g + 1) * GQA)], axis=1)
        q_aug = jnp.concatenate([qr4, bias4], axis=0)

        def k_sel(kj, g=g):
            st = pl.multiple_of(kj * TK, TK)
            return jnp.concatenate([kk_ref[0, g, pl.ds(st, TK), :], kblk_ref[pl.ds(st, TK), :]], axis=1)

        flash_init()

        def sel_body(kj, carry, g=g, q_aug=q_aug, k_sel=k_sel):
            flash_step(q_aug, k_sel(kj), vt_ref[0, g, kj], None)
            return carry

        lax.fori_loop(0, qi, sel_body, 0)
        flash_step(q_aug, k_sel(qi), vt_ref[0, g, qi], causal_bias)
        o_sel = flash_out()

        def win_step(kj, bias, g=g, qr4=qr4):
            st = pl.multiple_of(kj * TK, TK)
            flash_step(qr4, kk_ref[0, N_KV + g, pl.ds(st, TK), :], vt_ref[0, N_KV + g, kj], bias)

        flash_init()
        for back in range(n_win, 0, -1):
            @pl.when(qi >= back)
            def _(back=back):
                win_step(qi - back, band_bias if back == n_win else None)
        win_step(qi, causal_bias)
        o_win = flash_out()

        gsig = jax.nn.sigmoid(gt_ref[...])
        for r in range(GQA):
            h = g * GQA + r
            sl = slice(r * tq, (r + 1) * tq)
            o_h = (gsig[3 * h:3 * h + 1, :] * o_cmp[:, sl]
                   + gsig[3 * h + 1:3 * h + 2, :] * o_sel[:, sl]
                   + gsig[3 * h + 2:3 * h + 3, :] * o_win[:, sl])
            o_scr[:, h * HEAD_DIM:(h + 1) * HEAD_DIM] = o_h.T

    o = o_scr[...]
    ms = jnp.mean(o * o, axis=-1, keepdims=True)
    o_ref[...] = (o * lax.rsqrt(ms + EPS) * gn_ref[...]).astype(BF16)


def _nsa(qnt, qrt, cmp, kk, vt, kblk, gt, gn):
    batch, _, _, seq = qnt.shape
    assert TQ == TK and WINDOW % TK == 0 and TQ & (TQ - 1) == 0
    nq = seq // TQ
    width = N_HEADS * HEAD_DIM
    return pl.pallas_call(
        _nsa_kernel,
        grid=(batch, nq),
        in_specs=[
            pl.BlockSpec((1, N_HEADS, HEAD_DIM, TQ), lambda b, i: (b, 0, 0, i)),
            pl.BlockSpec((1, N_HEADS, HEAD_DIM, TQ), lambda b, i: (b, 0, 0, i)),
            pl.BlockSpec((1,) + cmp.shape[1:], lambda b, i: (b, 0, 0, 0)),
            pl.BlockSpec((1,) + kk.shape[1:], lambda b, i: (b, 0, 0, 0)),
            pl.BlockSpec((1,) + vt.shape[1:], lambda b, i: (b, 0, 0, 0, 0)),
            pl.BlockSpec((seq, LANES), lambda b, i: (0, 0)),
            pl.BlockSpec((gt.shape[0], TQ), lambda b, i: (0, b * nq + i)),
            pl.BlockSpec((1, width), lambda b, i: (0, 0)),
        ],
        out_specs=pl.BlockSpec((TQ, width), lambda b, i: (b * nq + i, 0)),
        out_shape=jax.ShapeDtypeStruct((batch * seq, width), BF16),
        scratch_shapes=[
            pltpu.VMEM((1, GQA * TQ), F32),
            pltpu.VMEM((1, GQA * TQ), F32),
            pltpu.VMEM((HEAD_DIM, GQA * TQ), F32),
            pltpu.VMEM((TQ, width), F32),
        ],
        compiler_params=_cparams(2),
        name="nsa",
    )(qnt, qrt, cmp, kk, vt, kblk, gt, gn)


def _pool_kernel(u_ref, up_ref, w_ref, b_ref, sc_ref, gn_ref, o_ref, y_scr):
    si = pl.program_id(1)
    tm = u_ref.shape[1]
    halo = max(POOL_SIZES)
    cur = u_ref[0]
    prev = jnp.where(si > 0, up_ref[0, tm - halo:, :], 0.0)
    ext = jnp.concatenate([prev, cur], axis=0)
    t1 = (si * tm + 1 + lax.broadcasted_iota(I32, (tm, 1), 0)).astype(F32)
    cg = cur.shape[1] // len(POOL_SIZES)
    for gi, w in enumerate(POOL_SIZES):
        sl = slice(gi * cg, (gi + 1) * cg)
        acc = ext[:, sl]
        span = 1
        while span < w:
            acc = acc + jnp.concatenate([jnp.zeros((span, cg), F32), acc[:-span]], axis=0)
            span *= 2
        mean = acc[halo:] / jnp.minimum(t1, float(w))
        d = (mean - cur[:, sl]).astype(BF16)
        y = _dot(d, w_ref[gi]) + b_ref[:, sl]
        y_scr[:, sl] = y * sc_ref[:, sl]
    y = y_scr[...]
    ms = jnp.mean(y * y, axis=-1, keepdims=True)
    o_ref[...] = (y * lax.rsqrt(ms + EPS) * gn_ref[...]).astype(BF16)


def _pool(u3, w_pool, b_pool, pool_scale, gn_pool):
    batch, seq, c = u3.shape
    tm = TM_PROJ
    ns = seq // tm
    vec = lambda b, i: (0, 0)
    return pl.pallas_call(
        _pool_kernel,
        grid=(batch, ns),
        in_specs=[
            pl.BlockSpec((1, tm, c), lambda b, i: (b, i, 0)),
            pl.BlockSpec((1, tm, c), lambda b, i: (b, jnp.maximum(i - 1, 0), 0)),
            pl.BlockSpec(w_pool.shape, lambda b, i: (0, 0, 0)),
            pl.BlockSpec((1, c), vec),
            pl.BlockSpec((1, c), vec),
            pl.BlockSpec((1, c), vec),
        ],
        out_specs=pl.BlockSpec((tm, c), lambda b, i: (b * ns + i, 0)),
        out_shape=jax.ShapeDtypeStruct((batch * seq, c), BF16),
        scratch_shapes=[pltpu.VMEM((tm, c), F32)],
        compiler_params=_cparams(2),
        name="pool",
    )(u3, u3, w_pool, b_pool, pool_scale, gn_pool)


def _outproj_kernel(mn_ref, mp_ref, x_ref, w_ref, ln_ref, wr_ref, br_ref,
                    x1_ref, h2_ref, eid_ref, gate_ref):
    tm, d = x_ref.shape
    half = mn_ref.shape[1]
    chunk = 512
    mn = mn_ref[...]
    mp = mp_ref[...]
    ssq = jnp.zeros((tm, 1), F32)
    for c in range(d // chunk):
        sl = slice(c * chunk, (c + 1) * chunk)
        x1 = x_ref[:, sl] + (_dot(mn, w_ref[:half, sl]) + _dot(mp, w_ref[half:, sl]))
        x1_ref[:, sl] = x1
        ssq = ssq + jnp.sum(x1 * x1, axis=-1, keepdims=True)
    h2 = x1_ref[...] * lax.rsqrt(ssq * (1.0 / d) + EPS) * ln_ref[...]
    h2_ref[...] = h2

    logits = _dot_nt(wr_ref[...], h2.astype(BF16)) + br_ref[:, 0:1]
    ng, ne = N_EXPERT_GROUPS, EXPERTS_PER_GROUP
    row = lax.broadcasted_iota(I32, (ne, tm), 0)
    lg = jnp.where(row < ng, logits[0:ne], NEG)
    mg = jnp.max(lg, axis=0, keepdims=True)
    g_sel = jnp.min(jnp.where(lg == mg, row, ne), axis=0, keepdims=True)
    p_g = 1.0 / jnp.sum(jnp.exp(lg - mg), axis=0, keepdims=True)
    le = jnp.zeros((ne, tm), F32)
    for g in range(ng):
        le = jnp.where(g_sel == g, logits[ne * (g + 1):ne * (g + 2)], le)
    ex = jnp.exp(le - jnp.max(le, axis=0, keepdims=True))
    pe = ex / jnp.sum(ex, axis=0, keepdims=True)
    p1 = jnp.max(pe, axis=0, keepdims=True)
    i1 = jnp.min(jnp.where(pe == p1, row, ne), axis=0, keepdims=True)
    rest = jnp.where(row == i1, -1.0, pe)
    p2 = jnp.max(rest, axis=0, keepdims=True)
    i2 = jnp.min(jnp.where(rest == p2, row, ne), axis=0, keepdims=True)
    psum = p1 + p2
    zero_i = jnp.zeros((ne, tm), I32)
    eid = jnp.where(row == 0, g_sel * ne + i1, jnp.where(row == 1, g_sel * ne + i2, zero_i))
    gate = jnp.where(row == 0, p_g * p1 / psum, jnp.where(row == 1, p_g * p2 / psum, 0.0))
    eid_ref[...] = eid
    gate_ref[...] = gate


def _outproj(mix_nsa, mix_pool, x2d, w_out, ln_moe, w_router_t, b_router):
    n, d = x2d.shape
    tm = TM_PROJ
    half = mix_nsa.shape[1]
    const = lambda i: (0, 0)
    return pl.pallas_call(
        _outproj_kernel,
        grid=(n // tm,),
        in_specs=[
            pl.BlockSpec((tm, half), lambda i: (i, 0)),
            pl.BlockSpec((tm, half), lambda i: (i, 0)),
            pl.BlockSpec((tm, d), lambda i: (i, 0)),
            pl.BlockSpec(w_out.shape, const, pipeline_mode=pl.Buffered(1)),
            pl.BlockSpec((1, d), const),
            pl.BlockSpec(w_router_t.shape, const),
            pl.BlockSpec(b_router.shape, const),
        ],
        out_specs=[
            pl.BlockSpec((tm, d), lambda i: (i, 0)),
            pl.BlockSpec((tm, d), lambda i: (i, 0)),
            pl.BlockSpec((SUBLANES, tm), lambda i: (0, i)),
            pl.BlockSpec((SUBLANES, tm), lambda i: (0, i)),
        ],
        out_shape=[
            jax.ShapeDtypeStruct((n, d), F32),
            jax.ShapeDtypeStruct((n, d), F32),
            jax.ShapeDtypeStruct((SUBLANES, n), I32),
            jax.ShapeDtypeStruct((SUBLANES, n), F32),
        ],
        compiler_params=_cparams(1),
        name="outproj",
    )(mix_nsa, mix_pool, x2d, w_out, ln_moe, w_router_t, b_router)


def _dispatch_kernel(eid_ref, dest_ref, meta_ref, rank_scr):
    n = eid_ref.shape[1]
    t = T_RANK
    row = lax.broadcasted_iota(I32, (N_EXPERTS, t), 0)
    before = (lax.broadcasted_iota(I32, (t, t), 0) < lax.broadcasted_iota(I32, (t, t), 1)).astype(BF16)
    carry = jnp.zeros((N_EXPERTS, 1), F32)
    for k in range(2):
        for j in range(n // t):
            sl = slice(j * t, (j + 1) * t)
            oh = row == eid_ref[k:k + 1, sl]
            ohf = oh.astype(F32)
            prior = _dot(ohf.astype(BF16), before) + carry
            rank_scr[k:k + 1, sl] = jnp.sum(jnp.where(oh, prior, 0.0), axis=0, keepdims=True)
            carry = carry + jnp.sum(ohf, axis=1, keepdims=True)
    counts = jnp.broadcast_to(carry, (N_EXPERTS, LANES))
    padded = jnp.floor((counts + (TM_MOE - 1.0)) * (1.0 / TM_MOE)) * TM_MOE
    ends = padded
    r_idx = lax.broadcasted_iota(I32, (N_EXPERTS, LANES), 0)
    sh = 1
    while sh < N_EXPERTS:
        ends = ends + jnp.where(r_idx >= sh, pltpu.roll(ends, sh, 0), 0.0)
        sh *= 2
    starts = (ends - padded)[:, 0:1]
    for k in range(2):
        for j in range(n // t):
            sl = slice(j * t, (j + 1) * t)
            oh = row == eid_ref[k:k + 1, sl]
            base = jnp.sum(jnp.where(oh, starts, 0.0), axis=0, keepdims=True)
            dest_ref[k:k + 1, sl] = (rank_scr[k:k + 1, sl] + base).astype(I32)
    for k in range(2, SUBLANES):
        dest_ref[k:k + 1, :] = jnp.zeros((1, n), I32)
    blk_start = lax.broadcasted_iota(I32, (N_EXPERTS, LANES), 1).astype(F32) * TM_MOE
    blk_expert = jnp.sum((ends <= blk_start).astype(F32), axis=0, keepdims=True)
    blk_expert = jnp.minimum(blk_expert, N_EXPERTS - 1.0)
    n_used = ends[N_EXPERTS - 1:N_EXPERTS, :] * (1.0 / TM_MOE)
    r8 = lax.broadcasted_iota(I32, (SUBLANES, LANES), 0)
    meta = jnp.where(r8 == 0, blk_expert, jnp.where(r8 == 1, n_used, 0.0))
    meta_ref[...] = meta.astype(I32)


def _dispatch(eid):
    n = eid.shape[1]
    return pl.pallas_call(
        _dispatch_kernel,
        out_shape=[jax.ShapeDtypeStruct((SUBLANES, n), I32),
                   jax.ShapeDtypeStruct((SUBLANES, LANES), I32)],
        scratch_shapes=[pltpu.VMEM((SUBLANES, n), F32)],
        compiler_params=pltpu.CompilerParams(vmem_limit_bytes=VMEM_LIMIT_BYTES),
        name="dispatch",
    )(eid)


def _row_copy(src_ref, src_row, dst_ref, dst_row, sem):
    return pltpu.make_async_copy(src_ref.at[pl.ds(src_row, 1)], dst_ref.at[pl.ds(dst_row, 1)], sem)


def _scatter_kernel(dest_ref, h_ref, init_ref, xs_ref, sem):
    del init_ref
    tm = h_ref.shape[0]

    def start(r, c):
        for k in range(2):
            _row_copy(h_ref, r, xs_ref, dest_ref[0, k, r], sem).start()
        return c

    lax.fori_loop(0, tm, start, 0)

    def wait(r, c):
        for k in range(2):
            _row_copy(h_ref, 0, xs_ref, 0, sem).wait()
        return c

    lax.fori_loop(0, tm, wait, 0)


def _scatter(dest3, h2, xs_init):
    nsteps, _, tm = dest3.shape
    d = h2.shape[1]
    return pl.pallas_call(
        _scatter_kernel,
        grid=(nsteps,),
        in_specs=[
            pl.BlockSpec((1, 2, tm), lambda i: (i, 0, 0), memory_space=pltpu.SMEM),
            pl.BlockSpec((tm, d), lambda i: (i, 0)),
            pl.BlockSpec(memory_space=pl.ANY),
        ],
        out_specs=pl.BlockSpec(memory_space=pl.ANY),
        out_shape=jax.ShapeDtypeStruct(xs_init.shape, xs_init.dtype),
        scratch_shapes=[pltpu.SemaphoreType.DMA(())],
        input_output_aliases={2: 0},
        compiler_params=_cparams(1),
        name="scatter",
    )(dest3, h2, xs_init)


def _experts_kernel(meta_ref, xs_ref, wg_ref, wu_ref, wd_ref, ys_ref):
    c = pl.program_id(0)

    @pl.when(c < meta_ref[1, 0])
    def _():
        xb = xs_ref[...].astype(BF16)
        hg = _dot(xb, wg_ref[0].astype(BF16))
        hu = _dot(xb, wu_ref[0].astype(BF16))
        act = (jax.nn.silu(hg) * hu).astype(BF16)
        ys_ref[...] = _dot(act, wd_ref[0].astype(BF16))

    @pl.when(c >= meta_ref[1, 0])
    def _():
        ys_ref[...] = jnp.zeros(ys_ref.shape, F32)


def _experts(meta, xs, w_gate, w_up, w_down):
    cap, d = xs.shape
    nblk = cap // TM_MOE
    ff = w_gate.shape[2]

    def row_map(c, meta):
        return (jnp.minimum(c, meta[1, 0] - 1), 0)

    def w_map(c, meta):
        return (meta[0, jnp.minimum(c, meta[1, 0] - 1)], 0, 0)

    return pl.pallas_call(
        _experts_kernel,
        grid_spec=pltpu.PrefetchScalarGridSpec(
            num_scalar_prefetch=1,
            grid=(nblk,),
            in_specs=[
                pl.BlockSpec((TM_MOE, d), row_map),
                pl.BlockSpec((1, d, ff), w_map),
                pl.BlockSpec((1, d, ff), w_map),
                pl.BlockSpec((1, ff, d), w_map),
            ],
            out_specs=pl.BlockSpec((TM_MOE, d), lambda c, meta: (c, 0)),
        ),
        out_shape=jax.ShapeDtypeStruct((cap, d), F32),
        compiler_params=_cparams(1),
        name="experts",
    )(meta, xs, w_gate, w_up, w_down)


def _combine_kernel(dest_ref, x1_ref, gate_ref, ln_ref, ys_ref, o_ref, ybuf, sem):
    tm = x1_ref.shape[0]

    def start(r, c):
        for k in range(2):
            _row_copy(ys_ref, dest_ref[0, k, r], ybuf.at[k], r, sem).start()
        return c

    lax.fori_loop(0, tm, start, 0)

    def wait(r, c):
        for k in range(2):
            _row_copy(ys_ref, 0, ybuf.at[k], 0, sem).wait()
        return c

    lax.fori_loop(0, tm, wait, 0)
    gate = gate_ref[...]
    moe = ybuf[0] * gate[:, 0:1] + ybuf[1] * gate[:, 1:2]
    x2 = x1_ref[...] + moe
    ms = jnp.mean(x2 * x2, axis=-1, keepdims=True)
    o_ref[...] = x2 * lax.rsqrt(ms + EPS) * ln_ref[...]


def _combine(dest3, x1, gate_rows, ln_final, ys):
    n, d = x1.shape
    tm = TM_COMBINE
    return pl.pallas_call(
        _combine_kernel,
        grid=(n // tm,),
        in_specs=[
            pl.BlockSpec((1, 2, tm), lambda i: (i, 0, 0), memory_space=pltpu.SMEM),
            pl.BlockSpec((tm, d), lambda i: (i, 0)),
            pl.BlockSpec((tm, LANES), lambda i: (i, 0)),
            pl.BlockSpec((1, d), lambda i: (0, 0)),
            pl.BlockSpec(memory_space=pl.ANY),
        ],
        out_specs=pl.BlockSpec((tm, d), lambda i: (i, 0)),
        out_shape=jax.ShapeDtypeStruct((n, d), F32),
        scratch_shapes=[pltpu.VMEM((2, tm, d), F32), pltpu.SemaphoreType.DMA(())],
        compiler_params=_cparams(1),
        name="combine",
    )(dest3, x1, gate_rows, ln_final, ys)


def kernel(x, positions, ln_mix, w_in, pe_cmp_k, w_cmp_k1, w_cmp_k2, pe_cmp_v, w_cmp_v1, w_cmp_v2, w_pool, b_pool, pool_scale, gn_nsa, gn_pool, w_out, ln_moe, w_router_group, b_router_group, w_router_expert, b_router_expert, w_gate, w_up, w_down, ln_final):
    batch, seq, d = x.shape
    n = batch * seq
    depth = w_in.shape[0]
    nsa_w = N_HEADS * HEAD_DIM
    kv_w = N_KV * HEAD_DIM
    cut_gates = nsa_w + 6 * kv_w
    cut_pool = cut_gates + N_GATES

    inv_freq = ROPE_THETA ** (-jnp.arange(0, ROT_DIM, 2, dtype=F32) / ROT_DIM)
    ang = positions.astype(F32).reshape(n, 1) * inv_freq[None, :]
    ang = jnp.concatenate([-ang, ang, jnp.zeros((n, HEAD_DIM - ROT_DIM), F32)], axis=1)
    angt = ang.T
    kblk = (jnp.arange(seq, dtype=I32)[:, None] // SEL_LEN == jnp.arange(LANES, dtype=I32)[None, :]).astype(BF16)

    x2d = x.reshape(n, d)
    assert depth == 1, "single-layer operation"
    for l in range(depth):
        wi = w_in[l]
        o_kc, o_ks = nsa_w, nsa_w + 2 * kv_w
        o_vs, o_kw, o_vw = o_ks + kv_w, o_ks + 2 * kv_w, o_ks + 3 * kv_w
        gate_rows_pad = -(-N_GATES // SUBLANES) * SUBLANES
        wqt = wi[:, :nsa_w].T.astype(BF16)
        wvt = jnp.concatenate([wi[:, o_vs:o_kw], wi[:, o_vw:cut_gates]], axis=1).T.astype(BF16)
        wgt = jnp.pad(wi[:, cut_gates:cut_pool], ((0, 0), (0, gate_rows_pad - N_GATES))).T.astype(BF16)
        wkc = wi[:, o_kc:o_ks].astype(BF16)
        wk = jnp.concatenate([wi[:, o_ks:o_vs], wi[:, o_kw:o_vw]], axis=1).astype(BF16)
        wu = wi[:, cut_pool:].astype(BF16)
        qnt, qrt, kvc, kk, vt, gt, u = _inproj(x2d, ln_mix[l][None], ang, angt, wqt, wvt, wgt, wkc, wk, wu,
                                               batch, seq)

        half = CMP_LEN * HEAD_DIM // 2
        def stack_w1(w):
            return jnp.concatenate([w[:half], w[half:]], axis=1)
        w1 = jnp.stack([stack_w1(w_cmp_k1[l]), stack_w1(w_cmp_v1[l])]).astype(BF16)
        w2 = jnp.stack([w_cmp_k2[l], w_cmp_v2[l]]).astype(BF16)
        def pe_rows(pe):
            return jnp.pad(pe.reshape(2, half), ((0, SUBLANES - 2), (0, 0)))
        pe2 = jnp.stack([pe_rows(pe_cmp_k[l]), pe_rows(pe_cmp_v[l])]).astype(BF16)
        kvc2 = kvc.reshape(batch, 4, seq // CMP_STRIDE, CMP_STRIDE * HEAD_DIM)
        cmp = _compress(kvc2, pe2, w1, w2)

        mix_nsa = _nsa(qnt, qrt, cmp, kk, vt, kblk, gt, gn_nsa[l][None])
        mix_pool = _pool(u.reshape(batch, seq, -1), w_pool[l].astype(BF16), b_pool[l][None],
                         pool_scale[l][None], gn_pool[l][None])

        ng, ne = N_EXPERT_GROUPS, EXPERTS_PER_GROUP
        w_r = jnp.concatenate([
            w_router_group[l].T, jnp.zeros((ne - ng, d), F32),
            jnp.transpose(w_router_expert[l], (0, 2, 1)).reshape(ng * ne, d),
            jnp.zeros((LANES - ne - ng * ne, d), F32)], axis=0).astype(BF16)
        b_r = jnp.concatenate([
            b_router_group[l], jnp.zeros((ne - ng,), F32), b_router_expert[l].reshape(-1),
            jnp.zeros((LANES - ne - ng * ne,), F32)])
        b_r = jnp.broadcast_to(b_r[:, None], (LANES, LANES))
        x1, h2, eid, gate = _outproj(mix_nsa, mix_pool, x2d, w_out[l].astype(BF16), ln_moe[l][None], w_r, b_r)

        dest, meta = _dispatch(eid)
        cap = (2 * n // TM_MOE + N_EXPERTS) * TM_MOE
        def per_tile(tm):
            return jnp.transpose(dest[:2].reshape(2, n // tm, tm), (1, 0, 2))
        xs = _scatter(per_tile(TS_SCATTER), h2, jnp.zeros((cap, d), F32))
        ys = _experts(meta, xs, w_gate[l], w_up[l], w_down[l])
        dest_tok = per_tile(TM_COMBINE)
        gate_rows = jnp.pad(gate[:2].T, ((0, 0), (0, LANES - 2)))
        x2d = _combine(dest_tok, x1, gate_rows, ln_final[None], ys)
    return x2d.reshape(batch, seq, d)
```

```python
import functools

import jax
import jax.numpy as jnp
from jax import lax
from jax.experimental import pallas as pl
from jax.experimental.pallas import tpu as pltpu

F32 = jnp.float32
BF16 = jnp.bfloat16
I32 = jnp.int32

HEAD_DIM = 128
N_HEADS = 8
N_KV = 2
GQA = N_HEADS // N_KV
N_GATES = 3 * N_HEADS
ROT_DIM = HEAD_DIM // 4
ROPE_THETA = 500000.0
CMP_LEN = 32
CMP_STRIDE = 16
CMP_HIDDEN = 2 * HEAD_DIM
SEL_LEN = 64
SEL_TOPK = 16
N_LOCAL = 2
WINDOW = 512
POOL_SIZES = (2, 4, 8, 16)
N_EXPERT_GROUPS = 4
EXPERTS_PER_GROUP = 8
N_EXPERTS = N_EXPERT_GROUPS * EXPERTS_PER_GROUP
EPS = 1e-6
NEG = -1e30
BIG = 1e30
SCALE = HEAD_DIM ** -0.5

LANES = 128
SUBLANES = 8
VMEM_LIMIT_BYTES = 56 * 1024 * 1024

TM_PROJ = 256
TQ = 256
TK = 256
SEL_MASK_BIAS = 32768.0
TM_MOE = 256
TS_SCATTER = 256
TM_COMBINE = 128
DMA_ISSUE_UNROLL = 8
T_RANK = 512


def _cparams(n_axes):
    return pltpu.CompilerParams(dimension_semantics=("arbitrary",) * n_axes,
                                vmem_limit_bytes=VMEM_LIMIT_BYTES)


def _dot(a, b):
    return jnp.dot(a, b, preferred_element_type=F32)


def _dot_nt(a, b):
    return lax.dot_general(a, b, (((1,), (1,)), ((), ())), preferred_element_type=F32)


U32 = jnp.uint32
_HI16 = 0xFFFF0000


def _pack_bf16_halves(x):
    c = x.shape[1] // 2
    bits = lambda t: lax.bitcast_convert_type(t.astype(BF16).astype(F32), U32)
    return lax.shift_right_logical(bits(x[:, :c]), U32(16)) | (bits(x[:, c:]) & U32(_HI16))


def _unpack_bf16_halves(w):
    lo = lax.bitcast_convert_type(lax.shift_left(w, U32(16)), F32)
    hi = lax.bitcast_convert_type(w & U32(_HI16), F32)
    return lo, hi


def _rope(t, cos, sin, axis):
    idx = lax.broadcasted_iota(I32, t.shape, axis)
    partner = jnp.where(idx < ROT_DIM // 2,
                        pltpu.roll(t, HEAD_DIM - ROT_DIM // 2, axis),
                        pltpu.roll(t, ROT_DIM // 2, axis))
    return t * cos + partner * sin


def _inproj_kernel(x_ref, ln_ref, ang_ref, angt_ref, wqt_ref, wvt_ref, wgt_ref, wkc_ref, wk_ref, wu_ref,
                   qnt_ref, qrt_ref, kvc_ref, kk_ref, vt_ref, gt_ref, u_ref):
    x = x_ref[...]
    ms = jnp.mean(x * x, axis=-1, keepdims=True)
    hb = (x * lax.rsqrt(ms + EPS) * ln_ref[...]).astype(BF16)
    chunk = 4 * HEAD_DIM

    angt = angt_ref[...]
    cost, sint = jnp.cos(angt), jnp.sin(angt)
    for ch in range(2):
        rt = _dot_nt(wqt_ref[ch * chunk:(ch + 1) * chunk, :], hb)
        for j in range(4):
            t = rt[j * HEAD_DIM:(j + 1) * HEAD_DIM, :]
            qnt_ref[0, ch * 4 + j] = (t * SCALE).astype(BF16)
            qrt_ref[0, ch * 4 + j] = (_rope(t, cost, sint, 0) * SCALE).astype(BF16)
    rt = _dot_nt(wvt_ref[...], hb)
    for j in range(4):
        vt_ref[0, j, 0] = rt[j * HEAD_DIM:(j + 1) * HEAD_DIM, :].astype(BF16)
    gt_ref[...] = _dot_nt(wgt_ref[...], hb)

    r = _dot(hb, wkc_ref[...])
    for j in range(4):
        kvc_ref[0, j] = r[:, j * HEAD_DIM:(j + 1) * HEAD_DIM].astype(BF16)
    ang = ang_ref[...]
    cos, sin = jnp.cos(ang), jnp.sin(ang)
    r = _dot(hb, wk_ref[...])
    for j in range(4):
        kk_ref[0, j] = _rope(r[:, j * HEAD_DIM:(j + 1) * HEAD_DIM], cos, sin, 1).astype(BF16)
    for ch in range(2):
        u_ref[:, ch * chunk:(ch + 1) * chunk] = _dot(hb, wu_ref[:, ch * chunk:(ch + 1) * chunk])


def _inproj(x2d, ln, ang, angt, wqt, wvt, wgt, wkc, wk, wu, batch, seq):
    n, d = x2d.shape
    tm = TM_PROJ
    assert tm == TK
    spb = seq // tm
    const = lambda i: (0, 0)
    resident = lambda w: pl.BlockSpec(w.shape, const, pipeline_mode=pl.Buffered(1))
    row_heads = lambda i: (i // spb, 0, i % spb, 0)
    col_heads = lambda i: (i // spb, 0, 0, i % spb)
    return pl.pallas_call(
        _inproj_kernel,
        grid=(n // tm,),
        in_specs=[
            pl.BlockSpec((tm, d), lambda i: (i, 0)),
            pl.BlockSpec((1, d), const),
            pl.BlockSpec((tm, HEAD_DIM), lambda i: (i, 0)),
            pl.BlockSpec((HEAD_DIM, tm), lambda i: (0, i)),
            resident(wqt), resident(wvt), resident(wgt), resident(wkc), resident(wk), resident(wu),
        ],
        out_specs=[
            pl.BlockSpec((1, N_HEADS, HEAD_DIM, tm), col_heads),
            pl.BlockSpec((1, N_HEADS, HEAD_DIM, tm), col_heads),
            pl.BlockSpec((1, 4, tm, HEAD_DIM), row_heads),
            pl.BlockSpec((1, 4, tm, HEAD_DIM), row_heads),
            pl.BlockSpec((1, 4, 1, HEAD_DIM, tm), lambda i: (i // spb, 0, i % spb, 0, 0)),
            pl.BlockSpec((wgt.shape[0], tm), lambda i: (0, i)),
            pl.BlockSpec((tm, wu.shape[1]), lambda i: (i, 0)),
        ],
        out_shape=[
            jax.ShapeDtypeStruct((batch, N_HEADS, HEAD_DIM, seq), BF16),
            jax.ShapeDtypeStruct((batch, N_HEADS, HEAD_DIM, seq), BF16),
            jax.ShapeDtypeStruct((batch, 4, seq, HEAD_DIM), BF16),
            jax.ShapeDtypeStruct((batch, 4, seq, HEAD_DIM), BF16),
            jax.ShapeDtypeStruct((batch, 4, spb, HEAD_DIM, tm), BF16),
            jax.ShapeDtypeStruct((wgt.shape[0], n), F32),
            jax.ShapeDtypeStruct((n, wu.shape[1]), F32),
        ],
        compiler_params=_cparams(1),
        name="inproj",
    )(x2d, ln, ang, angt, wqt, wvt, wgt, wkc, wk, wu)


def _compress_kernel(t_ref, pe_ref, w1_ref, w2_ref, o_ref):
    t2 = t_ref[0, 0]
    w1 = w1_ref[0]
    a = _dot(t2, w1)
    pb = _dot(pe_ref[0], w1)
    bias = pb[0:1, :CMP_HIDDEN] + pb[1:2, CMP_HIDDEN:]
    nrow = a.shape[0]
    nxt = pltpu.roll(a[:, CMP_HIDDEN:], nrow - 1, 0)
    hid = jax.nn.gelu(a[:, :CMP_HIDDEN] + nxt + bias)
    out = _dot(hid.astype(BF16), w2_ref[0])
    row = lax.broadcasted_iota(I32, out.shape, 0)
    out = jnp.where(row < nrow - 1, out, 0.0)
    is_value = pl.program_id(1) >= N_KV
    o_ref[0, 0] = jnp.where(is_value, out.T, out).astype(BF16)


def _compress(kvc2, pe2, w1, w2):
    batch, four, nrow, width = kvc2.shape
    return pl.pallas_call(
        _compress_kernel,
        grid=(batch, four),
        in_specs=[
            pl.BlockSpec((1, 1, nrow, width), lambda b, j: (b, j, 0, 0)),
            pl.BlockSpec((1, SUBLANES, width), lambda b, j: (j // N_KV, 0, 0)),
            pl.BlockSpec((1, width, 2 * CMP_HIDDEN), lambda b, j: (j // N_KV, 0, 0)),
            pl.BlockSpec((1, CMP_HIDDEN, HEAD_DIM), lambda b, j: (j // N_KV, 0, 0)),
        ],
        out_specs=pl.BlockSpec((1, 1, nrow, HEAD_DIM), lambda b, j: (b, j, 0, 0)),
        out_shape=jax.ShapeDtypeStruct((batch, four, nrow, HEAD_DIM), BF16),
        compiler_params=_cparams(2),
        name="compress",
    )(kvc2, pe2, w1, w2)


def _nsa_kernel(qnt_ref, qrt_ref, cmp_ref, kk_ref, vt_ref, kblk_ref, gt_ref, gn_ref, o_ref,
                m_scr, l_scr, acc_scr, o_scr):
    qi = pl.program_id(1)
    tq = qnt_ref.shape[3]
    rows = GQA * tq
    n_cmp_pad = cmp_ref.shape[2]
    n_sel = kk_ref.shape[2] // SEL_LEN
    q0 = qi * tq

    def flash_init():
        m_scr[...] = jnp.full(m_scr.shape, NEG, F32)
        l_scr[...] = jnp.zeros(l_scr.shape, F32)
        acc_scr[...] = jnp.zeros(acc_scr.shape, F32)

    def flash_step(q_t, k_tile, v_t, bias):
        s = _dot(k_tile, q_t)
        if bias is not None:
            s = s + bias
        m_old = m_scr[...]
        m_new = jnp.maximum(m_old, jnp.max(s, axis=0, keepdims=True))
        alpha = jnp.exp(m_old - m_new)
        p = jnp.exp(s - m_new)
        l_scr[...] = alpha * l_scr[...] + jnp.sum(p, axis=0, keepdims=True)
        acc_scr[...] = alpha * acc_scr[...] + _dot(v_t, p.astype(BF16))
        m_scr[...] = m_new

    def flash_out():
        return acc_scr[...] / l_scr[...]

    def tile_heads(a):
        return jnp.concatenate([a] * GQA, axis=1)

    kk_i = lax.broadcasted_iota(I32, (TK, tq), 0)
    qq_i = lax.broadcasted_iota(I32, (TK, tq), 1)
    causal_bias = tile_heads(jnp.where(kk_i <= qq_i, 0.0, NEG))
    band_bias = tile_heads(jnp.where(kk_i > qq_i, 0.0, NEG))
    n_win = WINDOW // TK

    for g in range(N_KV):
        qn4 = jnp.concatenate([qnt_ref[0, h] for h in range(g * GQA, (g + 1) * GQA)], axis=1)
        s = _dot(cmp_ref[0, g], qn4)
        n_idx = lax.broadcasted_iota(I32, (n_cmp_pad, rows), 0)
        t_idx = q0 + jnp.bitwise_and(lax.broadcasted_iota(I32, (n_cmp_pad, rows), 1), tq - 1)
        cmp_ok = n_idx * CMP_STRIDE + (CMP_LEN - 1) <= t_idx
        s = jnp.where(cmp_ok, s, NEG)
        m = jnp.max(s, axis=0, keepdims=True)
        e = jnp.where(cmp_ok, jnp.exp(s - m), 0.0)
        l = jnp.sum(e, axis=0, keepdims=True)
        p = e * jnp.where(l > 0.0, 1.0 / l, 0.0)
        pb = p.astype(BF16)
        o_cmp = _dot(cmp_ref[0, N_KV + g], pb)

        jn = lax.broadcasted_iota(I32, (n_sel, n_cmp_pad), 0) * SEL_LEN
        cn = lax.broadcasted_iota(I32, (n_sel, n_cmp_pad), 1) * CMP_STRIDE
        ov = jnp.clip(jnp.minimum(cn + CMP_LEN, jn + SEL_LEN) - jnp.maximum(cn, jn), 0, None)
        w_t = (ov.astype(F32) * (1.0 / CMP_LEN)).astype(BF16)
        imp4 = _dot(w_t, pb)
        imp = imp4[:, 0:tq]
        for r in range(1, GQA):
            imp = imp + imp4[:, r * tq:(r + 1) * tq]
        j_blk = lax.broadcasted_iota(I32, (n_sel, tq), 0)
        t_q = q0 + lax.broadcasted_iota(I32, (n_sel, tq), 1)
        sel_ok = j_blk * SEL_LEN <= t_q
        back = t_q // SEL_LEN - j_blk
        forced = (j_blk == 0) | ((back >= 0) & (back < N_LOCAL))
        val = jnp.where(sel_ok & forced, BIG, jnp.where(sel_ok, imp, -BIG))
        rank = jnp.zeros((n_sel, tq), F32)
        for i in range(n_sel):
            vi = val[i:i + 1, :]
            beats = (vi > val) | ((vi == val) & (j_blk > i))
            rank = rank + beats.astype(F32)
        keep = (rank < float(min(SEL_TOPK, n_sel))) & sel_ok
        bias_t = jnp.where(keep, 0.0, -SEL_MASK_BIAS)
        bias_t = jnp.concatenate([bias_t, jnp.zeros((HEAD_DIM - n_sel, tq), F32)], axis=0)
        bias4 = tile_heads(bias_t.astype(BF16))

        qr4 = jnp.concatenate([qrt_ref[0, h] for h in range(g * GQA, (g + 1) * GQA)], axis=1)
        q_aug = jnp.concatenate([qr4, bias4], axis=0)

        def k_sel(kj, g=g):
            st = pl.multiple_of(kj * TK, TK)
            return jnp.concatenate([kk_ref[0, g, pl.ds(st, TK), :], kblk_ref[pl.ds(st, TK), :]], axis=1)

        flash_init()

        def sel_body(kj, carry, g=g, q_aug=q_aug, k_sel=k_sel):
            flash_step(q_aug, k_sel(kj), vt_ref[0, g, kj], None)
            return carry

        lax.fori_loop(0, qi, sel_body, 0)
        flash_step(q_aug, k_sel(qi), vt_ref[0, g, qi], causal_bias)
        o_sel = flash_out()

        def win_step(kj, bias, g=g, qr4=qr4):
            st = pl.multiple_of(kj * TK, TK)
            flash_step(qr4, kk_ref[0, N_KV + g, pl.ds(st, TK), :], vt_ref[0, N_KV + g, kj], bias)

        flash_init()
        for back in range(n_win, 0, -1):
            @pl.when(qi >= back)
            def _(back=back):
                win_step(qi - back, band_bias if back == n_win else None)
        win_step(qi, causal_bias)
        o_win = flash_out()

        gsig = jax.nn.sigmoid(gt_ref[...])
        for r in range(GQA):
            h = g * GQA + r
            sl = slice(r * tq, (r + 1) * tq)
            o_h = (gsig[3 * h:3 * h + 1, :] * o_cmp[:, sl]
                   + gsig[3 * h + 1:3 * h + 2, :] * o_sel[:, sl]
                   + gsig[3 * h + 2:3 * h + 3, :] * o_win[:, sl])
            o_scr[:, h * HEAD_DIM:(h + 1) * HEAD_DIM] = o_h.T

    o = o_scr[...]
    ms = jnp.mean(o * o, axis=-1, keepdims=True)
    o_ref[...] = (o * lax.rsqrt(ms + EPS) * gn_ref[...]).astype(BF16)


def _nsa(qnt, qrt, cmp, kk, vt, kblk, gt, gn):
    batch, _, _, seq = qnt.shape
    assert TQ == TK and WINDOW % TK == 0 and TQ & (TQ - 1) == 0
    nq = seq // TQ
    width = N_HEADS * HEAD_DIM
    return pl.pallas_call(
        _nsa_kernel,
        grid=(batch, nq),
        in_specs=[
            pl.BlockSpec((1, N_HEADS, HEAD_DIM, TQ), lambda b, i: (b, 0, 0, i)),
            pl.BlockSpec((1, N_HEADS, HEAD_DIM, TQ), lambda b, i: (b, 0, 0, i)),
            pl.BlockSpec((1,) + cmp.shape[1:], lambda b, i: (b, 0, 0, 0)),
            pl.BlockSpec((1,) + kk.shape[1:], lambda b, i: (b, 0, 0, 0)),
            pl.BlockSpec((1,) + vt.shape[1:], lambda b, i: (b, 0, 0, 0, 0)),
            pl.BlockSpec((seq, LANES), lambda b, i: (0, 0)),
            pl.BlockSpec((gt.shape[0], TQ), lambda b, i: (0, b * nq + i)),
            pl.BlockSpec((1, width), lambda b, i: (0, 0)),
        ],
        out_specs=pl.BlockSpec((TQ, width), lambda b, i: (b * nq + i, 0)),
        out_shape=jax.ShapeDtypeStruct((batch * seq, width), BF16),
        scratch_shapes=[
            pltpu.VMEM((1, GQA * TQ), F32),
            pltpu.VMEM((1, GQA * TQ), F32),
            pltpu.VMEM((HEAD_DIM, GQA * TQ), F32),
            pltpu.VMEM((TQ, width), F32),
        ],
        compiler_params=_cparams(2),
        name="nsa",
    )(qnt, qrt, cmp, kk, vt, kblk, gt, gn)


def _pool_kernel(u_ref, up_ref, w_ref, b_ref, sc_ref, gn_ref, o_ref, y_scr):
    si = pl.program_id(1)
    tm = u_ref.shape[1]
    halo = max(POOL_SIZES)
    cur = u_ref[0]
    prev = jnp.where(si > 0, up_ref[0, tm - halo:, :], 0.0)
    ext = jnp.concatenate([prev, cur], axis=0)
    t1 = (si * tm + 1 + lax.broadcasted_iota(I32, (tm, 1), 0)).astype(F32)
    cg = cur.shape[1] // len(POOL_SIZES)
    for gi, w in enumerate(POOL_SIZES):
        sl = slice(gi * cg, (gi + 1) * cg)
        acc = ext[:, sl]
        span = 1
        while span < w:
            acc = acc + jnp.concatenate([jnp.zeros((span, cg), F32), acc[:-span]], axis=0)
            span *= 2
        mean = acc[halo:] / jnp.minimum(t1, float(w))
        d = (mean - cur[:, sl]).astype(BF16)
        y = _dot(d, w_ref[gi]) + b_ref[:, sl]
        y_scr[:, sl] = y * sc_ref[:, sl]
    y = y_scr[...]
    ms = jnp.mean(y * y, axis=-1, keepdims=True)
    o_ref[...] = (y * lax.rsqrt(ms + EPS) * gn_ref[...]).astype(BF16)


def _pool(u3, w_pool, b_pool, pool_scale, gn_pool):
    batch, seq, c = u3.shape
    tm = TM_PROJ
    ns = seq // tm
    vec = lambda b, i: (0, 0)
    return pl.pallas_call(
        _pool_kernel,
        grid=(batch, ns),
        in_specs=[
            pl.BlockSpec((1, tm, c), lambda b, i: (b, i, 0)),
            pl.BlockSpec((1, tm, c), lambda b, i: (b, jnp.maximum(i - 1, 0), 0)),
            pl.BlockSpec(w_pool.shape, lambda b, i: (0, 0, 0)),
            pl.BlockSpec((1, c), vec),
            pl.BlockSpec((1, c), vec),
            pl.BlockSpec((1, c), vec),
        ],
        out_specs=pl.BlockSpec((tm, c), lambda b, i: (b * ns + i, 0)),
        out_shape=jax.ShapeDtypeStruct((batch * seq, c), BF16),
        scratch_shapes=[pltpu.VMEM((tm, c), F32)],
        compiler_params=_cparams(2),
        name="pool",
    )(u3, u3, w_pool, b_pool, pool_scale, gn_pool)


def _outproj_kernel(mn_ref, mp_ref, x_ref, w_ref, ln_ref, wr_ref, br_ref,
                    x1_ref, h2_ref, eid_ref, gate_ref):
    tm, d = x_ref.shape
    half = mn_ref.shape[1]
    chunk = 512
    mn = mn_ref[...]
    mp = mp_ref[...]
    ssq = jnp.zeros((tm, 1), F32)
    for c in range(d // chunk):
        sl = slice(c * chunk, (c + 1) * chunk)
        x1 = x_ref[:, sl] + (_dot(mn, w_ref[:half, sl]) + _dot(mp, w_ref[half:, sl]))
        x1_ref[:, sl] = x1
        ssq = ssq + jnp.sum(x1 * x1, axis=-1, keepdims=True)
    h2 = x1_ref[...] * lax.rsqrt(ssq * (1.0 / d) + EPS) * ln_ref[...]
    h2_ref[...] = _pack_bf16_halves(h2)

    logits = _dot_nt(wr_ref[...], h2.astype(BF16)) + br_ref[:, 0:1]
    ng, ne = N_EXPERT_GROUPS, EXPERTS_PER_GROUP
    row = lax.broadcasted_iota(I32, (ne, tm), 0)
    lg = jnp.where(row < ng, logits[0:ne], NEG)
    mg = jnp.max(lg, axis=0, keepdims=True)
    g_sel = jnp.min(jnp.where(lg == mg, row, ne), axis=0, keepdims=True)
    p_g = 1.0 / jnp.sum(jnp.exp(lg - mg), axis=0, keepdims=True)
    le = jnp.zeros((ne, tm), F32)
    for g in range(ng):
        le = jnp.where(g_sel == g, logits[ne * (g + 1):ne * (g + 2)], le)
    ex = jnp.exp(le - jnp.max(le, axis=0, keepdims=True))
    pe = ex / jnp.sum(ex, axis=0, keepdims=True)
    p1 = jnp.max(pe, axis=0, keepdims=True)
    i1 = jnp.min(jnp.where(pe == p1, row, ne), axis=0, keepdims=True)
    rest = jnp.where(row == i1, -1.0, pe)
    p2 = jnp.max(rest, axis=0, keepdims=True)
    i2 = jnp.min(jnp.where(rest == p2, row, ne), axis=0, keepdims=True)
    psum = p1 + p2
    zero_i = jnp.zeros((ne, tm), I32)
    eid = jnp.where(row == 0, g_sel * ne + i1, jnp.where(row == 1, g_sel * ne + i2, zero_i))
    gate = jnp.where(row == 0, p_g * p1 / psum, jnp.where(row == 1, p_g * p2 / psum, 0.0))
    eid_ref[...] = eid
    gate_ref[...] = gate


def _outproj(mix_nsa, mix_pool, x2d, w_out, ln_moe, w_router_t, b_router):
    n, d = x2d.shape
    tm = TM_PROJ
    half = mix_nsa.shape[1]
    const = lambda i: (0, 0)
    return pl.pallas_call(
        _outproj_kernel,
        grid=(n // tm,),
        in_specs=[
            pl.BlockSpec((tm, half), lambda i: (i, 0)),
            pl.BlockSpec((tm, half), lambda i: (i, 0)),
            pl.BlockSpec((tm, d), lambda i: (i, 0)),
            pl.BlockSpec(w_out.shape, const, pipeline_mode=pl.Buffered(1)),
            pl.BlockSpec((1, d), const),
            pl.BlockSpec(w_router_t.shape, const),
            pl.BlockSpec(b_router.shape, const),
        ],
        out_specs=[
            pl.BlockSpec((tm, d), lambda i: (i, 0)),
            pl.BlockSpec((tm, d // 2), lambda i: (i, 0)),
            pl.BlockSpec((SUBLANES, tm), lambda i: (0, i)),
            pl.BlockSpec((SUBLANES, tm), lambda i: (0, i)),
        ],
        out_shape=[
            jax.ShapeDtypeStruct((n, d), F32),
            jax.ShapeDtypeStruct((n, d // 2), U32),
            jax.ShapeDtypeStruct((SUBLANES, n), I32),
            jax.ShapeDtypeStruct((SUBLANES, n), F32),
        ],
        compiler_params=_cparams(1),
        name="outproj",
    )(mix_nsa, mix_pool, x2d, w_out, ln_moe, w_router_t, b_router)


def _dispatch_kernel(eid_ref, dest_ref, meta_ref, rank_scr):
    n = eid_ref.shape[1]
    t = T_RANK
    row = lax.broadcasted_iota(I32, (N_EXPERTS, t), 0)
    before = (lax.broadcasted_iota(I32, (t, t), 0) < lax.broadcasted_iota(I32, (t, t), 1)).astype(BF16)
    carry = jnp.zeros((N_EXPERTS, 1), F32)
    for k in range(2):
        for j in range(n // t):
            sl = slice(j * t, (j + 1) * t)
            oh = row == eid_ref[k:k + 1, sl]
            ohf = oh.astype(F32)
            prior = _dot(ohf.astype(BF16), before) + carry
            rank_scr[k:k + 1, sl] = jnp.sum(jnp.where(oh, prior, 0.0), axis=0, keepdims=True)
            carry = carry + jnp.sum(ohf, axis=1, keepdims=True)
    counts = jnp.broadcast_to(carry, (N_EXPERTS, LANES))
    padded = jnp.floor((counts + (TM_MOE - 1.0)) * (1.0 / TM_MOE)) * TM_MOE
    ends = padded
    r_idx = lax.broadcasted_iota(I32, (N_EXPERTS, LANES), 0)
    sh = 1
    while sh < N_EXPERTS:
        ends = ends + jnp.where(r_idx >= sh, pltpu.roll(ends, sh, 0), 0.0)
        sh *= 2
    starts = (ends - padded)[:, 0:1]
    for k in range(2):
        for j in range(n // t):
            sl = slice(j * t, (j + 1) * t)
            oh = row == eid_ref[k:k + 1, sl]
            base = jnp.sum(jnp.where(oh, starts, 0.0), axis=0, keepdims=True)
            dest_ref[k:k + 1, sl] = (rank_scr[k:k + 1, sl] + base).astype(I32)
    for k in range(2, SUBLANES):
        dest_ref[k:k + 1, :] = jnp.zeros((1, n), I32)
    blk_start = lax.broadcasted_iota(I32, (N_EXPERTS, LANES), 1).astype(F32) * TM_MOE
    blk_expert = jnp.sum((ends <= blk_start).astype(F32), axis=0, keepdims=True)
    blk_expert = jnp.minimum(blk_expert, N_EXPERTS - 1.0)
    n_used = ends[N_EXPERTS - 1:N_EXPERTS, :] * (1.0 / TM_MOE)
    r8 = lax.broadcasted_iota(I32, (SUBLANES, LANES), 0)
    meta = jnp.where(r8 == 0, blk_expert, jnp.where(r8 == 1, n_used, 0.0))
    meta_ref[...] = meta.astype(I32)


def _dispatch(eid):
    n = eid.shape[1]
    return pl.pallas_call(
        _dispatch_kernel,
        out_shape=[jax.ShapeDtypeStruct((SUBLANES, n), I32),
                   jax.ShapeDtypeStruct((SUBLANES, LANES), I32)],
        scratch_shapes=[pltpu.VMEM((SUBLANES, n), F32)],
        compiler_params=pltpu.CompilerParams(vmem_limit_bytes=VMEM_LIMIT_BYTES),
        name="dispatch",
    )(eid)


def _row_copy(src_ref, src_row, dst_ref, dst_row, sem):
    return pltpu.make_async_copy(src_ref.at[pl.ds(src_row, 1)], dst_ref.at[pl.ds(dst_row, 1)], sem)


def _scatter_kernel(dest_ref, h_ref, init_ref, xs_ref, sem):
    del init_ref
    tm = h_ref.shape[0]

    def start(r, c):
        for k in range(2):
            _row_copy(h_ref, r, xs_ref, dest_ref[0, k, r], sem).start()
        return c

    lax.fori_loop(0, tm, start, 0, unroll=DMA_ISSUE_UNROLL)
    for k in range(2):
        pltpu.make_async_copy(h_ref, xs_ref.at[pl.ds(0, tm)], sem).wait()


def _scatter(dest3, h2, xs_init):
    nsteps, _, tm = dest3.shape
    d = h2.shape[1]
    return pl.pallas_call(
        _scatter_kernel,
        grid=(nsteps,),
        in_specs=[
            pl.BlockSpec((1, 2, tm), lambda i: (i, 0, 0), memory_space=pltpu.SMEM),
            pl.BlockSpec((tm, d), lambda i: (i, 0)),
            pl.BlockSpec(memory_space=pl.ANY),
        ],
        out_specs=pl.BlockSpec(memory_space=pl.ANY),
        out_shape=jax.ShapeDtypeStruct(xs_init.shape, xs_init.dtype),
        scratch_shapes=[pltpu.SemaphoreType.DMA(())],
        input_output_aliases={2: 0},
        compiler_params=_cparams(1),
        name="scatter",
    )(dest3, h2, xs_init)


def _experts_kernel(meta_ref, xs_ref, wg_ref, wu_ref, wd_ref, ys_ref):
    c = pl.program_id(0)

    @pl.when(c < meta_ref[1, 0])
    def _():
        lo, hi = _unpack_bf16_halves(xs_ref[...])
        lo, hi = lo.astype(BF16), hi.astype(BF16)
        half = lo.shape[1]

        def proj(w_ref):
            return _dot(lo, w_ref[0, :half, :].astype(BF16)) + _dot(hi, w_ref[0, half:, :].astype(BF16))

        act = (jax.nn.silu(proj(wg_ref)) * proj(wu_ref)).astype(BF16)
        ys_ref[...] = _pack_bf16_halves(_dot(act, wd_ref[0].astype(BF16)))

    @pl.when(c >= meta_ref[1, 0])
    def _():
        ys_ref[...] = jnp.zeros(ys_ref.shape, U32)


def _experts(meta, xs, w_gate, w_up, w_down):
    cap, dw = xs.shape
    nblk = cap // TM_MOE
    d, ff = w_gate.shape[1:]

    def row_map(c, meta):
        return (jnp.minimum(c, meta[1, 0] - 1), 0)

    def w_map(c, meta):
        return (meta[0, jnp.minimum(c, meta[1, 0] - 1)], 0, 0)

    return pl.pallas_call(
        _experts_kernel,
        grid_spec=pltpu.PrefetchScalarGridSpec(
            num_scalar_prefetch=1,
            grid=(nblk,),
            in_specs=[
                pl.BlockSpec((TM_MOE, dw), row_map),
                pl.BlockSpec((1, d, ff), w_map),
                pl.BlockSpec((1, d, ff), w_map),
                pl.BlockSpec((1, ff, d), w_map),
            ],
            out_specs=pl.BlockSpec((TM_MOE, dw), lambda c, meta: (c, 0)),
        ),
        out_shape=jax.ShapeDtypeStruct((cap, dw), U32),
        compiler_params=_cparams(1),
        name="experts",
    )(meta, xs, w_gate, w_up, w_down)


def _combine_kernel(dest_ref, x1_ref, gate_ref, ln_ref, ys_ref, o_ref, ybuf, sem):
    tm = x1_ref.shape[0]

    def start(r, c):
        for k in range(2):
            _row_copy(ys_ref, dest_ref[0, k, r], ybuf.at[k], r, sem).start()
        return c

    lax.fori_loop(0, tm, start, 0, unroll=DMA_ISSUE_UNROLL)
    for k in range(2):
        pltpu.make_async_copy(ys_ref.at[pl.ds(0, tm)], ybuf.at[k], sem).wait()
    gate = gate_ref[...]
    lo0, hi0 = _unpack_bf16_halves(ybuf[0])
    lo1, hi1 = _unpack_bf16_halves(ybuf[1])
    g0, g1 = gate[:, 0:1], gate[:, 1:2]
    moe = jnp.concatenate([lo0 * g0 + lo1 * g1, hi0 * g0 + hi1 * g1], axis=1)
    x2 = x1_ref[...] + moe
    ms = jnp.mean(x2 * x2, axis=-1, keepdims=True)
    o_ref[...] = x2 * lax.rsqrt(ms + EPS) * ln_ref[...]


def _combine(dest3, x1, gate_rows, ln_final, ys):
    n, d = x1.shape
    tm = TM_COMBINE
    return pl.pallas_call(
        _combine_kernel,
        grid=(n // tm,),
        in_specs=[
            pl.BlockSpec((1, 2, tm), lambda i: (i, 0, 0), memory_space=pltpu.SMEM),
            pl.BlockSpec((tm, d), lambda i: (i, 0)),
            pl.BlockSpec((tm, LANES), lambda i: (i, 0)),
            pl.BlockSpec((1, d), lambda i: (0, 0)),
            pl.BlockSpec(memory_space=pl.ANY),
        ],
        out_specs=pl.BlockSpec((tm, d), lambda i: (i, 0)),
        out_shape=jax.ShapeDtypeStruct((n, d), F32),
        scratch_shapes=[pltpu.VMEM((2, tm, d // 2), U32), pltpu.SemaphoreType.DMA(())],
        compiler_params=_cparams(1),
        name="combine",
    )(dest3, x1, gate_rows, ln_final, ys)


def kernel(x, positions, ln_mix, w_in, pe_cmp_k, w_cmp_k1, w_cmp_k2, pe_cmp_v, w_cmp_v1, w_cmp_v2, w_pool, b_pool, pool_scale, gn_nsa, gn_pool, w_out, ln_moe, w_router_group, b_router_group, w_router_expert, b_router_expert, w_gate, w_up, w_down, ln_final):
    batch, seq, d = x.shape
    n = batch * seq
    depth = w_in.shape[0]
    nsa_w = N_HEADS * HEAD_DIM
    kv_w = N_KV * HEAD_DIM
    cut_gates = nsa_w + 6 * kv_w
    cut_pool = cut_gates + N_GATES

    inv_freq = ROPE_THETA ** (-jnp.arange(0, ROT_DIM, 2, dtype=F32) / ROT_DIM)
    ang = positions.astype(F32).reshape(n, 1) * inv_freq[None, :]
    ang = jnp.concatenate([-ang, ang, jnp.zeros((n, HEAD_DIM - ROT_DIM), F32)], axis=1)
    angt = ang.T
    kblk = (jnp.arange(seq, dtype=I32)[:, None] // SEL_LEN == jnp.arange(LANES, dtype=I32)[None, :]).astype(BF16)

    x2d = x.reshape(n, d)
    assert depth == 1, "single-layer operation"
    for l in range(depth):
        wi = w_in[l]
        o_kc, o_ks = nsa_w, nsa_w + 2 * kv_w
        o_vs, o_kw, o_vw = o_ks + kv_w, o_ks + 2 * kv_w, o_ks + 3 * kv_w
        gate_rows_pad = -(-N_GATES // SUBLANES) * SUBLANES
        wqt = wi[:, :nsa_w].T.astype(BF16)
        wvt = jnp.concatenate([wi[:, o_vs:o_kw], wi[:, o_vw:cut_gates]], axis=1).T.astype(BF16)
        wgt = jnp.pad(wi[:, cut_gates:cut_pool], ((0, 0), (0, gate_rows_pad - N_GATES))).T.astype(BF16)
        wkc = wi[:, o_kc:o_ks].astype(BF16)
        wk = jnp.concatenate([wi[:, o_ks:o_vs], wi[:, o_kw:o_vw]], axis=1).astype(BF16)
        wu = wi[:, cut_pool:].astype(BF16)
        qnt, qrt, kvc, kk, vt, gt, u = _inproj(x2d, ln_mix[l][None], ang, angt, wqt, wvt, wgt, wkc, wk, wu,
                                               batch, seq)

        half = CMP_LEN * HEAD_DIM // 2
        def stack_w1(w):
            return jnp.concatenate([w[:half], w[half:]], axis=1)
        w1 = jnp.stack([stack_w1(w_cmp_k1[l]), stack_w1(w_cmp_v1[l])]).astype(BF16)
        w2 = jnp.stack([w_cmp_k2[l], w_cmp_v2[l]]).astype(BF16)
        def pe_rows(pe):
            return jnp.pad(pe.reshape(2, half), ((0, SUBLANES - 2), (0, 0)))
        pe2 = jnp.stack([pe_rows(pe_cmp_k[l]), pe_rows(pe_cmp_v[l])]).astype(BF16)
        kvc2 = kvc.reshape(batch, 4, seq // CMP_STRIDE, CMP_STRIDE * HEAD_DIM)
        cmp = _compress(kvc2, pe2, w1, w2)

        mix_nsa = _nsa(qnt, qrt, cmp, kk, vt, kblk, gt, gn_nsa[l][None])
        mix_pool = _pool(u.reshape(batch, seq, -1), w_pool[l].astype(BF16), b_pool[l][None],
                         pool_scale[l][None], gn_pool[l][None])

        ng, ne = N_EXPERT_GROUPS, EXPERTS_PER_GROUP
        w_r = jnp.concatenate([
            w_router_group[l].T, jnp.zeros((ne - ng, d), F32),
            jnp.transpose(w_router_expert[l], (0, 2, 1)).reshape(ng * ne, d),
            jnp.zeros((LANES - ne - ng * ne, d), F32)], axis=0).astype(BF16)
        b_r = jnp.concatenate([
            b_router_group[l], jnp.zeros((ne - ng,), F32), b_router_expert[l].reshape(-1),
            jnp.zeros((LANES - ne - ng * ne,), F32)])
        b_r = jnp.broadcast_to(b_r[:, None], (LANES, LANES))
        x1, h2, eid, gate = _outproj(mix_nsa, mix_pool, x2d, w_out[l].astype(BF16), ln_moe[l][None], w_r, b_r)

        dest, meta = _dispatch(eid)
        cap = (2 * n // TM_MOE + N_EXPERTS) * TM_MOE
        def per_tile(tm):
            return jnp.transpose(dest[:2].reshape(2, n // tm, tm), (1, 0, 2))
        xs = _scatter(per_tile(TS_SCATTER), h2, jnp.zeros((cap, d // 2), U32))
        ys = _experts(meta, xs, w_gate[l], w_up[l], w_down[l])
        dest_tok = per_tile(TM_COMBINE)
        gate_rows = jnp.pad(gate[:2].T, ((0, 0), (0, LANES - 2)))
        x2d = _combine(dest_tok, x1, gate_rows, ln_final[None], ys)
    return x2d.reshape(batch, seq, d)
```

```python
import functools

import jax
import jax.numpy as jnp
from jax import lax
from jax.experimental import pallas as pl
from jax.experimental.pallas import tpu as pltpu

F32 = jnp.float32
BF16 = jnp.bfloat16
I32 = jnp.int32

HEAD_DIM = 128
N_HEADS = 8
N_KV = 2
GQA = N_HEADS // N_KV
N_GATES = 3 * N_HEADS
ROT_DIM = HEAD_DIM // 4
ROPE_THETA = 500000.0
CMP_LEN = 32
CMP_STRIDE = 16
CMP_HIDDEN = 2 * HEAD_DIM
SEL_LEN = 64
SEL_TOPK = 16
N_LOCAL = 2
WINDOW = 512
POOL_SIZES = (2, 4, 8, 16)
N_EXPERT_GROUPS = 4
EXPERTS_PER_GROUP = 8
N_EXPERTS = N_EXPERT_GROUPS * EXPERTS_PER_GROUP
EPS = 1e-6
NEG = -1e30
BIG = 1e30
SCALE = HEAD_DIM ** -0.5

LANES = 128
SUBLANES = 8
VMEM_LIMIT_BYTES = 56 * 1024 * 1024

TM_PROJ = 256
TQ = 256
TK = 256
SEL_MASK_BIAS = 32768.0
TM_MOE = 256
TS_SCATTER = 256
TM_COMBINE = 128
DMA_ISSUE_UNROLL = 8
T_RANK = 512


def _cparams(n_axes, flags=None):
    return pltpu.CompilerParams(dimension_semantics=("arbitrary",) * n_axes,
                                vmem_limit_bytes=VMEM_LIMIT_BYTES, flags=flags)


def _dot(a, b):
    return jnp.dot(a, b, preferred_element_type=F32)


def _dot_nt(a, b):
    return lax.dot_general(a, b, (((1,), (1,)), ((), ())), preferred_element_type=F32)


U32 = jnp.uint32
_HI16 = 0xFFFF0000


def _pack_bf16_halves(x):
    c = x.shape[1] // 2
    bits = lambda t: lax.bitcast_convert_type(t.astype(BF16).astype(F32), U32)
    return lax.shift_right_logical(bits(x[:, :c]), U32(16)) | (bits(x[:, c:]) & U32(_HI16))


def _unpack_bf16_halves(w):
    lo = lax.bitcast_convert_type(lax.shift_left(w, U32(16)), F32)
    hi = lax.bitcast_convert_type(w & U32(_HI16), F32)
    return lo, hi


def _rope(t, cos, sin, axis):
    idx = lax.broadcasted_iota(I32, t.shape, axis)
    partner = jnp.where(idx < ROT_DIM // 2,
                        pltpu.roll(t, HEAD_DIM - ROT_DIM // 2, axis),
                        pltpu.roll(t, ROT_DIM // 2, axis))
    return t * cos + partner * sin


def _inproj_kernel(x_ref, ln_ref, ang_ref, angt_ref, wqt_ref, wvt_ref, wgt_ref, wkc_ref, wk_ref, wu_ref,
                   qnt_ref, qrt_ref, kvc_ref, kk_ref, vt_ref, gt_ref, u_ref):
    x = x_ref[...]
    ms = jnp.mean(x * x, axis=-1, keepdims=True)
    hb = (x * lax.rsqrt(ms + EPS) * ln_ref[...]).astype(BF16)
    chunk = 4 * HEAD_DIM

    angt = angt_ref[...]
    cost, sint = jnp.cos(angt), jnp.sin(angt)
    for ch in range(2):
        rt = _dot_nt(wqt_ref[ch * chunk:(ch + 1) * chunk, :], hb)
        for j in range(4):
            t = rt[j * HEAD_DIM:(j + 1) * HEAD_DIM, :]
            qnt_ref[0, ch * 4 + j] = (t * SCALE).astype(BF16)
            qrt_ref[0, ch * 4 + j] = (_rope(t, cost, sint, 0) * SCALE).astype(BF16)
    rt = _dot_nt(wvt_ref[...], hb)
    for j in range(4):
        vt_ref[0, j, 0] = rt[j * HEAD_DIM:(j + 1) * HEAD_DIM, :].astype(BF16)
    gt_ref[...] = _dot_nt(wgt_ref[...], hb)

    r = _dot(hb, wkc_ref[...])
    for j in range(4):
        kvc_ref[0, j] = r[:, j * HEAD_DIM:(j + 1) * HEAD_DIM].astype(BF16)
    ang = ang_ref[...]
    cos, sin = jnp.cos(ang), jnp.sin(ang)
    r = _dot(hb, wk_ref[...])
    for j in range(4):
        kk_ref[0, j] = _rope(r[:, j * HEAD_DIM:(j + 1) * HEAD_DIM], cos, sin, 1).astype(BF16)
    for ch in range(2):
        u_ref[:, ch * chunk:(ch + 1) * chunk] = _dot(hb, wu_ref[:, ch * chunk:(ch + 1) * chunk])


def _inproj(x2d, ln, ang, angt, wqt, wvt, wgt, wkc, wk, wu, batch, seq):
    n, d = x2d.shape
    tm = TM_PROJ
    assert tm == TK
    spb = seq // tm
    const = lambda i: (0, 0)
    resident = lambda w: pl.BlockSpec(w.shape, const, pipeline_mode=pl.Buffered(1))
    row_heads = lambda i: (i // spb, 0, i % spb, 0)
    col_heads = lambda i: (i // spb, 0, 0, i % spb)
    return pl.pallas_call(
        _inproj_kernel,
        grid=(n // tm,),
        in_specs=[
            pl.BlockSpec((tm, d), lambda i: (i, 0)),
            pl.BlockSpec((1, d), const),
            pl.BlockSpec((tm, HEAD_DIM), lambda i: (i, 0)),
            pl.BlockSpec((HEAD_DIM, tm), lambda i: (0, i)),
            resident(wqt), resident(wvt), resident(wgt), resident(wkc), resident(wk), resident(wu),
        ],
        out_specs=[
            pl.BlockSpec((1, N_HEADS, HEAD_DIM, tm), col_heads),
            pl.BlockSpec((1, N_HEADS, HEAD_DIM, tm), col_heads),
            pl.BlockSpec((1, 4, tm, HEAD_DIM), row_heads),
            pl.BlockSpec((1, 4, tm, HEAD_DIM), row_heads),
            pl.BlockSpec((1, 4, 1, HEAD_DIM, tm), lambda i: (i // spb, 0, i % spb, 0, 0)),
            pl.BlockSpec((wgt.shape[0], tm), lambda i: (0, i)),
            pl.BlockSpec((tm, wu.shape[1]), lambda i: (i, 0)),
        ],
        out_shape=[
            jax.ShapeDtypeStruct((batch, N_HEADS, HEAD_DIM, seq), BF16),
            jax.ShapeDtypeStruct((batch, N_HEADS, HEAD_DIM, seq), BF16),
            jax.ShapeDtypeStruct((batch, 4, seq, HEAD_DIM), BF16),
            jax.ShapeDtypeStruct((batch, 4, seq, HEAD_DIM), BF16),
            jax.ShapeDtypeStruct((batch, 4, spb, HEAD_DIM, tm), BF16),
            jax.ShapeDtypeStruct((wgt.shape[0], n), F32),
            jax.ShapeDtypeStruct((n, wu.shape[1]), F32),
        ],
        compiler_params=_cparams(1),
        name="inproj",
    )(x2d, ln, ang, angt, wqt, wvt, wgt, wkc, wk, wu)


def _compress_kernel(t_ref, pe_ref, w1_ref, w2_ref, o_ref):
    t2 = t_ref[0, 0]
    w1 = w1_ref[0]
    a = _dot(t2, w1)
    pb = _dot(pe_ref[0], w1)
    bias = pb[0:1, :CMP_HIDDEN] + pb[1:2, CMP_HIDDEN:]
    nrow = a.shape[0]
    nxt = pltpu.roll(a[:, CMP_HIDDEN:], nrow - 1, 0)
    hid = jax.nn.gelu(a[:, :CMP_HIDDEN] + nxt + bias)
    out = _dot(hid.astype(BF16), w2_ref[0])
    row = lax.broadcasted_iota(I32, out.shape, 0)
    out = jnp.where(row < nrow - 1, out, 0.0)
    is_value = pl.program_id(1) >= N_KV
    o_ref[0, 0] = jnp.where(is_value, out.T, out).astype(BF16)


def _compress(kvc2, pe2, w1, w2):
    batch, four, nrow, width = kvc2.shape
    return pl.pallas_call(
        _compress_kernel,
        grid=(batch, four),
        in_specs=[
            pl.BlockSpec((1, 1, nrow, width), lambda b, j: (b, j, 0, 0)),
            pl.BlockSpec((1, SUBLANES, width), lambda b, j: (j // N_KV, 0, 0)),
            pl.BlockSpec((1, width, 2 * CMP_HIDDEN), lambda b, j: (j // N_KV, 0, 0)),
            pl.BlockSpec((1, CMP_HIDDEN, HEAD_DIM), lambda b, j: (j // N_KV, 0, 0)),
        ],
        out_specs=pl.BlockSpec((1, 1, nrow, HEAD_DIM), lambda b, j: (b, j, 0, 0)),
        out_shape=jax.ShapeDtypeStruct((batch, four, nrow, HEAD_DIM), BF16),
        compiler_params=_cparams(2),
        name="compress",
    )(kvc2, pe2, w1, w2)


def _nsa_kernel(qnt_ref, qrt_ref, cmp_ref, kk_ref, vt_ref, kblk_ref, gt_ref, gn_ref, o_ref,
                m_scr, l_scr, acc_scr, qa_scr, ocmp_scr, s_scr, o_scr):
    qi = pl.program_id(1)
    tq = qnt_ref.shape[3]
    rows = GQA * tq
    n_cmp_pad = cmp_ref.shape[2]
    n_sel = kk_ref.shape[2] // SEL_LEN
    q0 = qi * tq

    m_scr[...] = jnp.full(m_scr.shape, NEG, F32)
    l_scr[...] = jnp.zeros(l_scr.shape, F32)
    acc_scr[...] = jnp.zeros(acc_scr.shape, F32)

    def scores(item, slot):
        c, g, q_rows, k_tile, v_t, bias = item
        s_scr[slot] = _dot(k_tile, qa_scr[g, :q_rows, :])

    def softmax_pv(item, slot):
        c, g, q_rows, k_tile, v_t, bias = item
        s = s_scr[slot]
        if bias is not None:
            s = s + bias
        m_old = m_scr[c]
        m_new = jnp.maximum(m_old, jnp.max(s, axis=0, keepdims=True))
        alpha = jnp.exp(m_old - m_new)
        p = jnp.exp(s - m_new)
        l_scr[c] = alpha * l_scr[c] + jnp.sum(p, axis=0, keepdims=True)
        acc_scr[c] = alpha * acc_scr[c] + _dot(v_t, p.astype(BF16))
        m_scr[c] = m_new

    def flash_out(c):
        return acc_scr[c] / l_scr[c]

    def tile_heads(a):
        return jnp.concatenate([a] * GQA, axis=1)

    kk_i = lax.broadcasted_iota(I32, (TK, tq), 0)
    qq_i = lax.broadcasted_iota(I32, (TK, tq), 1)
    causal_bias = tile_heads(jnp.where(kk_i <= qq_i, 0.0, NEG))
    band_bias = tile_heads(jnp.where(kk_i > qq_i, 0.0, NEG))
    n_win = WINDOW // TK

    for g in range(N_KV):
        qn4 = jnp.concatenate([qnt_ref[0, h] for h in range(g * GQA, (g + 1) * GQA)], axis=1)
        s = _dot(cmp_ref[0, g], qn4)
        n_idx = lax.broadcasted_iota(I32, (n_cmp_pad, rows), 0)
        t_idx = q0 + jnp.bitwise_and(lax.broadcasted_iota(I32, (n_cmp_pad, rows), 1), tq - 1)
        cmp_ok = n_idx * CMP_STRIDE + (CMP_LEN - 1) <= t_idx
        s = jnp.where(cmp_ok, s, NEG)
        m = jnp.max(s, axis=0, keepdims=True)
        e = jnp.where(cmp_ok, jnp.exp(s - m), 0.0)
        l = jnp.sum(e, axis=0, keepdims=True)
        p = e * jnp.where(l > 0.0, 1.0 / l, 0.0)
        pb = p.astype(BF16)
        ocmp_scr[g] = _dot(cmp_ref[0, N_KV + g], pb)

        jn = lax.broadcasted_iota(I32, (n_sel, n_cmp_pad), 0) * SEL_LEN
        cn = lax.broadcasted_iota(I32, (n_sel, n_cmp_pad), 1) * CMP_STRIDE
        ov = jnp.clip(jnp.minimum(cn + CMP_LEN, jn + SEL_LEN) - jnp.maximum(cn, jn), 0, None)
        w_t = (ov.astype(F32) * (1.0 / CMP_LEN)).astype(BF16)
        imp4 = _dot(w_t, pb)
        imp = imp4[:, 0:tq]
        for r in range(1, GQA):
            imp = imp + imp4[:, r * tq:(r + 1) * tq]
        j_blk = lax.broadcasted_iota(I32, (n_sel, tq), 0)
        t_q = q0 + lax.broadcasted_iota(I32, (n_sel, tq), 1)
        sel_ok = j_blk * SEL_LEN <= t_q
        back = t_q // SEL_LEN - j_blk
        forced = (j_blk == 0) | ((back >= 0) & (back < N_LOCAL))
        val = jnp.where(sel_ok & forced, BIG, jnp.where(sel_ok, imp, -BIG))
        rank = jnp.zeros((n_sel, tq), F32)
        for i in range(n_sel):
            vi = val[i:i + 1, :]
            beats = (vi > val) | ((vi == val) & (j_blk > i))
            rank = rank + beats.astype(F32)
        keep = (rank < float(min(SEL_TOPK, n_sel))) & sel_ok
        bias_t = jnp.where(keep, 0.0, -SEL_MASK_BIAS)
        bias_t = jnp.concatenate([bias_t, jnp.zeros((HEAD_DIM - n_sel, tq), F32)], axis=0)
        qa_scr[g, :HEAD_DIM, :] = jnp.concatenate(
            [qrt_ref[0, h] for h in range(g * GQA, (g + 1) * GQA)], axis=1)
        qa_scr[g, HEAD_DIM:, :] = tile_heads(bias_t.astype(BF16))

    def sel_item(g, kj, bias):
        st = pl.multiple_of(kj * TK, TK)
        k_aug = jnp.concatenate([kk_ref[0, g, pl.ds(st, TK), :], kblk_ref[pl.ds(st, TK), :]], axis=1)
        return (g, g, 2 * HEAD_DIM, k_aug, vt_ref[0, g, kj], bias)

    def win_item(g, back):
        kj = jnp.maximum(qi - back, 0)
        st = pl.multiple_of(kj * TK, TK)
        off = jnp.where(qi >= back, 0.0, NEG)
        bias = causal_bias if back == 0 else (band_bias + off if back == n_win else off)
        return (N_KV + g, g, HEAD_DIM, kk_ref[0, N_KV + g, pl.ds(st, TK), :], vt_ref[0, N_KV + g, kj], bias)

    assert N_KV == 2
    scores(sel_item(0, 0, None), 0)

    def sel_body(kj, carry):
        scores(sel_item(1, kj, None), 1)
        softmax_pv(sel_item(0, kj, None), 0)
        scores(sel_item(0, kj + 1, None), 0)
        softmax_pv(sel_item(1, kj, None), 1)
        return carry

    lax.fori_loop(0, qi, sel_body, 0)
    tail = ([sel_item(g, qi, causal_bias) for g in range(N_KV)]
            + [win_item(g, back) for back in range(n_win, -1, -1) for g in range(N_KV)])
    for i, item in enumerate(tail):
        if i + 1 < len(tail):
            scores(tail[i + 1], (i + 1) % 2)
        softmax_pv(item, i % 2)

    gsig = jax.nn.sigmoid(gt_ref[...])
    for g in range(N_KV):
        o_cmp, o_sel, o_win = ocmp_scr[g], flash_out(g), flash_out(N_KV + g)
        for r in range(GQA):
            h = g * GQA + r
            sl = slice(r * tq, (r + 1) * tq)
            o_h = (gsig[3 * h:3 * h + 1, :] * o_cmp[:, sl]
                   + gsig[3 * h + 1:3 * h + 2, :] * o_sel[:, sl]
                   + gsig[3 * h + 2:3 * h + 3, :] * o_win[:, sl])
            o_scr[:, h * HEAD_DIM:(h + 1) * HEAD_DIM] = o_h.T

    o = o_scr[...]
    ms = jnp.mean(o * o, axis=-1, keepdims=True)
    o_ref[...] = (o * lax.rsqrt(ms + EPS) * gn_ref[...]).astype(BF16)


def _nsa(qnt, qrt, cmp, kk, vt, kblk, gt, gn):
    batch, _, _, seq = qnt.shape
    assert TQ == TK and WINDOW % TK == 0 and TQ & (TQ - 1) == 0
    nq = seq // TQ
    width = N_HEADS * HEAD_DIM
    return pl.pallas_call(
        _nsa_kernel,
        grid=(batch, nq),
        in_specs=[
            pl.BlockSpec((1, N_HEADS, HEAD_DIM, TQ), lambda b, i: (b, 0, 0, i)),
            pl.BlockSpec((1, N_HEADS, HEAD_DIM, TQ), lambda b, i: (b, 0, 0, i)),
            pl.BlockSpec((1,) + cmp.shape[1:], lambda b, i: (b, 0, 0, 0)),
            pl.BlockSpec((1,) + kk.shape[1:], lambda b, i: (b, 0, 0, 0)),
            pl.BlockSpec((1,) + vt.shape[1:], lambda b, i: (b, 0, 0, 0, 0)),
            pl.BlockSpec((seq, LANES), lambda b, i: (0, 0)),
            pl.BlockSpec((gt.shape[0], TQ), lambda b, i: (0, b * nq + i)),
            pl.BlockSpec((1, width), lambda b, i: (0, 0)),
        ],
        out_specs=pl.BlockSpec((TQ, width), lambda b, i: (b * nq + i, 0)),
        out_shape=jax.ShapeDtypeStruct((batch * seq, width), BF16),
        scratch_shapes=[
            pltpu.VMEM((2 * N_KV, 1, GQA * TQ), F32),
            pltpu.VMEM((2 * N_KV, 1, GQA * TQ), F32),
            pltpu.VMEM((2 * N_KV, HEAD_DIM, GQA * TQ), F32),
            pltpu.VMEM((N_KV, 2 * HEAD_DIM, GQA * TQ), BF16),
            pltpu.VMEM((N_KV, HEAD_DIM, GQA * TQ), F32),
            pltpu.VMEM((2, TK, GQA * TQ), F32),
            pltpu.VMEM((TQ, width), F32),
        ],
        compiler_params=_cparams(2),
        name="nsa",
    )(qnt, qrt, cmp, kk, vt, kblk, gt, gn)


def _pool_kernel(u_ref, up_ref, w_ref, b_ref, sc_ref, gn_ref, o_ref, y_scr):
    si = pl.program_id(1)
    tm = u_ref.shape[1]
    halo = max(POOL_SIZES)
    cur = u_ref[0]
    prev = jnp.where(si > 0, up_ref[0, tm - halo:, :], 0.0)
    ext = jnp.concatenate([prev, cur], axis=0)
    t1 = (si * tm + 1 + lax.broadcasted_iota(I32, (tm, 1), 0)).astype(F32)
    cg = cur.shape[1] // len(POOL_SIZES)
    for gi, w in enumerate(POOL_SIZES):
        sl = slice(gi * cg, (gi + 1) * cg)
        acc = ext[:, sl]
        span = 1
        while span < w:
            acc = acc + jnp.concatenate([jnp.zeros((span, cg), F32), acc[:-span]], axis=0)
            span *= 2
        mean = acc[halo:] / jnp.minimum(t1, float(w))
        d = (mean - cur[:, sl]).astype(BF16)
        y = _dot(d, w_ref[gi]) + b_ref[:, sl]
        y_scr[:, sl] = y * sc_ref[:, sl]
    y = y_scr[...]
    ms = jnp.mean(y * y, axis=-1, keepdims=True)
    o_ref[...] = (y * lax.rsqrt(ms + EPS) * gn_ref[...]).astype(BF16)


def _pool(u3, w_pool, b_pool, pool_scale, gn_pool):
    batch, seq, c = u3.shape
    tm = TM_PROJ
    ns = seq // tm
    vec = lambda b, i: (0, 0)
    return pl.pallas_call(
        _pool_kernel,
        grid=(batch, ns),
        in_specs=[
            pl.BlockSpec((1, tm, c), lambda b, i: (b, i, 0)),
            pl.BlockSpec((1, tm, c), lambda b, i: (b, jnp.maximum(i - 1, 0), 0)),
            pl.BlockSpec(w_pool.shape, lambda b, i: (0, 0, 0)),
            pl.BlockSpec((1, c), vec),
            pl.BlockSpec((1, c), vec),
            pl.BlockSpec((1, c), vec),
        ],
        out_specs=pl.BlockSpec((tm, c), lambda b, i: (b * ns + i, 0)),
        out_shape=jax.ShapeDtypeStruct((batch * seq, c), BF16),
        scratch_shapes=[pltpu.VMEM((tm, c), F32)],
        compiler_params=_cparams(2),
        name="pool",
    )(u3, u3, w_pool, b_pool, pool_scale, gn_pool)


def _outproj_kernel(mn_ref, mp_ref, x_ref, w_ref, ln_ref, wr_ref, br_ref,
                    x1_ref, h2_ref, eid_ref, gate_ref):
    tm, d = x_ref.shape
    half = mn_ref.shape[1]
    chunk = 512
    mn = mn_ref[...]
    mp = mp_ref[...]
    ssq = jnp.zeros((tm, 1), F32)
    for c in range(d // chunk):
        sl = slice(c * chunk, (c + 1) * chunk)
        x1 = x_ref[:, sl] + (_dot(mn, w_ref[:half, sl]) + _dot(mp, w_ref[half:, sl]))
        x1_ref[:, sl] = x1
        ssq = ssq + jnp.sum(x1 * x1, axis=-1, keepdims=True)
    h2 = x1_ref[...] * lax.rsqrt(ssq * (1.0 / d) + EPS) * ln_ref[...]
    h2_ref[...] = _pack_bf16_halves(h2)

    logits = _dot_nt(wr_ref[...], h2.astype(BF16)) + br_ref[:, 0:1]
    ng, ne = N_EXPERT_GROUPS, EXPERTS_PER_GROUP
    row = lax.broadcasted_iota(I32, (ne, tm), 0)
    lg = jnp.where(row < ng, logits[0:ne], NEG)
    mg = jnp.max(lg, axis=0, keepdims=True)
    g_sel = jnp.min(jnp.where(lg == mg, row, ne), axis=0, keepdims=True)
    p_g = 1.0 / jnp.sum(jnp.exp(lg - mg), axis=0, keepdims=True)
    le = jnp.zeros((ne, tm), F32)
    for g in range(ng):
        le = jnp.where(g_sel == g, logits[ne * (g + 1):ne * (g + 2)], le)
    ex = jnp.exp(le - jnp.max(le, axis=0, keepdims=True))
    pe = ex / jnp.sum(ex, axis=0, keepdims=True)
    p1 = jnp.max(pe, axis=0, keepdims=True)
    i1 = jnp.min(jnp.where(pe == p1, row, ne), axis=0, keepdims=True)
    rest = jnp.where(row == i1, -1.0, pe)
    p2 = jnp.max(rest, axis=0, keepdims=True)
    i2 = jnp.min(jnp.where(rest == p2, row, ne), axis=0, keepdims=True)
    psum = p1 + p2
    zero_i = jnp.zeros((ne, tm), I32)
    eid = jnp.where(row == 0, g_sel * ne + i1, jnp.where(row == 1, g_sel * ne + i2, zero_i))
    gate = jnp.where(row == 0, p_g * p1 / psum, jnp.where(row == 1, p_g * p2 / psum, 0.0))
    eid_ref[...] = eid
    gate_ref[...] = gate


def _outproj(mix_nsa, mix_pool, x2d, w_out, ln_moe, w_router_t, b_router):
    n, d = x2d.shape
    tm = TM_PROJ
    half = mix_nsa.shape[1]
    const = lambda i: (0, 0)
    return pl.pallas_call(
        _outproj_kernel,
        grid=(n // tm,),
        in_specs=[
            pl.BlockSpec((tm, half), lambda i: (i, 0)),
            pl.BlockSpec((tm, half), lambda i: (i, 0)),
            pl.BlockSpec((tm, d), lambda i: (i, 0)),
            pl.BlockSpec(w_out.shape, const, pipeline_mode=pl.Buffered(1)),
            pl.BlockSpec((1, d), const),
            pl.BlockSpec(w_router_t.shape, const),
            pl.BlockSpec(b_router.shape, const),
        ],
        out_specs=[
            pl.BlockSpec((tm, d), lambda i: (i, 0)),
            pl.BlockSpec((tm, d // 2), lambda i: (i, 0)),
            pl.BlockSpec((SUBLANES, tm), lambda i: (0, i)),
            pl.BlockSpec((SUBLANES, tm), lambda i: (0, i)),
        ],
        out_shape=[
            jax.ShapeDtypeStruct((n, d), F32),
            jax.ShapeDtypeStruct((n, d // 2), U32),
            jax.ShapeDtypeStruct((SUBLANES, n), I32),
            jax.ShapeDtypeStruct((SUBLANES, n), F32),
        ],
        compiler_params=_cparams(1),
        name="outproj",
    )(mix_nsa, mix_pool, x2d, w_out, ln_moe, w_router_t, b_router)


def _dispatch_kernel(eid_ref, dest_ref, meta_ref, rank_scr):
    n = eid_ref.shape[1]
    t = T_RANK
    row = lax.broadcasted_iota(I32, (N_EXPERTS, t), 0)
    before = (lax.broadcasted_iota(I32, (t, t), 0) < lax.broadcasted_iota(I32, (t, t), 1)).astype(BF16)
    carry = jnp.zeros((N_EXPERTS, 1), F32)
    for k in range(2):
        for j in range(n // t):
            sl = slice(j * t, (j + 1) * t)
            oh = row == eid_ref[k:k + 1, sl]
            ohf = oh.astype(F32)
            prior = _dot(ohf.astype(BF16), before) + carry
            rank_scr[k:k + 1, sl] = jnp.sum(jnp.where(oh, prior, 0.0), axis=0, keepdims=True)
            carry = carry + jnp.sum(ohf, axis=1, keepdims=True)
    counts = jnp.broadcast_to(carry, (N_EXPERTS, LANES))
    padded = jnp.floor((counts + (TM_MOE - 1.0)) * (1.0 / TM_MOE)) * TM_MOE
    ends = padded
    r_idx = lax.broadcasted_iota(I32, (N_EXPERTS, LANES), 0)
    sh = 1
    while sh < N_EXPERTS:
        ends = ends + jnp.where(r_idx >= sh, pltpu.roll(ends, sh, 0), 0.0)
        sh *= 2
    starts = (ends - padded)[:, 0:1]
    for k in range(2):
        for j in range(n // t):
            sl = slice(j * t, (j + 1) * t)
            oh = row == eid_ref[k:k + 1, sl]
            base = jnp.sum(jnp.where(oh, starts, 0.0), axis=0, keepdims=True)
            dest_ref[k:k + 1, sl] = (rank_scr[k:k + 1, sl] + base).astype(I32)
    for k in range(2, SUBLANES):
        dest_ref[k:k + 1, :] = jnp.zeros((1, n), I32)
    blk_start = lax.broadcasted_iota(I32, (N_EXPERTS, LANES), 1).astype(F32) * TM_MOE
    blk_expert = jnp.sum((ends <= blk_start).astype(F32), axis=0, keepdims=True)
    blk_expert = jnp.minimum(blk_expert, N_EXPERTS - 1.0)
    n_used = ends[N_EXPERTS - 1:N_EXPERTS, :] * (1.0 / TM_MOE)
    r8 = lax.broadcasted_iota(I32, (SUBLANES, LANES), 0)
    meta = jnp.where(r8 == 0, blk_expert, jnp.where(r8 == 1, n_used, 0.0))
    meta_ref[...] = meta.astype(I32)


def _dispatch(eid):
    n = eid.shape[1]
    return pl.pallas_call(
        _dispatch_kernel,
        out_shape=[jax.ShapeDtypeStruct((SUBLANES, n), I32),
                   jax.ShapeDtypeStruct((SUBLANES, LANES), I32)],
        scratch_shapes=[pltpu.VMEM((SUBLANES, n), F32)],
        compiler_params=pltpu.CompilerParams(vmem_limit_bytes=VMEM_LIMIT_BYTES),
        name="dispatch",
    )(eid)


def _row_copy(src_ref, src_row, dst_ref, dst_row, sem):
    return pltpu.make_async_copy(src_ref.at[pl.ds(src_row, 1)], dst_ref.at[pl.ds(dst_row, 1)], sem)


def _scatter_kernel(dest_ref, h_ref, init_ref, xs_ref, sem):
    del init_ref
    tm = h_ref.shape[0]

    def start(r, c):
        for k in range(2):
            _row_copy(h_ref, r, xs_ref, dest_ref[0, k, r], sem).start()
        return c

    lax.fori_loop(0, tm, start, 0, unroll=DMA_ISSUE_UNROLL)
    for k in range(2):
        pltpu.make_async_copy(h_ref, xs_ref.at[pl.ds(0, tm)], sem).wait()


def _scatter(dest3, h2, xs_init):
    nsteps, _, tm = dest3.shape
    d = h2.shape[1]
    return pl.pallas_call(
        _scatter_kernel,
        grid=(nsteps,),
        in_specs=[
            pl.BlockSpec((1, 2, tm), lambda i: (i, 0, 0), memory_space=pltpu.SMEM),
            pl.BlockSpec((tm, d), lambda i: (i, 0)),
            pl.BlockSpec(memory_space=pl.ANY),
        ],
        out_specs=pl.BlockSpec(memory_space=pl.ANY),
        out_shape=jax.ShapeDtypeStruct(xs_init.shape, xs_init.dtype),
        scratch_shapes=[pltpu.SemaphoreType.DMA(())],
        input_output_aliases={2: 0},
        compiler_params=_cparams(1),
        name="scatter",
    )(dest3, h2, xs_init)


def _experts_kernel(meta_ref, xs_ref, wg_ref, wu_ref, wd_ref, ys_ref):
    c = pl.program_id(0)

    @pl.when(c < meta_ref[1, 0])
    def _():
        lo, hi = _unpack_bf16_halves(xs_ref[...])
        lo, hi = lo.astype(BF16), hi.astype(BF16)
        half = lo.shape[1]

        def proj(w_ref):
            return _dot(lo, w_ref[0, :half, :].astype(BF16)) + _dot(hi, w_ref[0, half:, :].astype(BF16))

        act = (jax.nn.silu(proj(wg_ref)) * proj(wu_ref)).astype(BF16)
        ys_ref[...] = _pack_bf16_halves(_dot(act, wd_ref[0].astype(BF16)))

    @pl.when(c >= meta_ref[1, 0])
    def _():
        ys_ref[...] = jnp.zeros(ys_ref.shape, U32)


def _experts(meta, xs, w_gate, w_up, w_down):
    cap, dw = xs.shape
    nblk = cap // TM_MOE
    d, ff = w_gate.shape[1:]

    def row_map(c, meta):
        return (jnp.minimum(c, meta[1, 0] - 1), 0)

    def w_map(c, meta):
        return (meta[0, jnp.minimum(c, meta[1, 0] - 1)], 0, 0)

    return pl.pallas_call(
        _experts_kernel,
        grid_spec=pltpu.PrefetchScalarGridSpec(
            num_scalar_prefetch=1,
            grid=(nblk,),
            in_specs=[
                pl.BlockSpec((TM_MOE, dw), row_map),
                pl.BlockSpec((1, d, ff), w_map),
                pl.BlockSpec((1, d, ff), w_map),
                pl.BlockSpec((1, ff, d), w_map),
            ],
            out_specs=pl.BlockSpec((TM_MOE, dw), lambda c, meta: (c, 0)),
        ),
        out_shape=jax.ShapeDtypeStruct((cap, dw), U32),
        compiler_params=_cparams(1),
        name="experts",
    )(meta, xs, w_gate, w_up, w_down)


def _combine_kernel(dest_ref, x1_ref, gate_ref, ln_ref, ys_ref, o_ref, ybuf, sem):
    tm = x1_ref.shape[0]

    def start(r, c):
        for k in range(2):
            _row_copy(ys_ref, dest_ref[0, k, r], ybuf.at[k], r, sem).start()
        return c

    lax.fori_loop(0, tm, start, 0, unroll=DMA_ISSUE_UNROLL)
    for k in range(2):
        pltpu.make_async_copy(ys_ref.at[pl.ds(0, tm)], ybuf.at[k], sem).wait()
    gate = gate_ref[...]
    lo0, hi0 = _unpack_bf16_halves(ybuf[0])
    lo1, hi1 = _unpack_bf16_halves(ybuf[1])
    g0, g1 = gate[:, 0:1], gate[:, 1:2]
    moe = jnp.concatenate([lo0 * g0 + lo1 * g1, hi0 * g0 + hi1 * g1], axis=1)
    x2 = x1_ref[...] + moe
    ms = jnp.mean(x2 * x2, axis=-1, keepdims=True)
    o_ref[...] = x2 * lax.rsqrt(ms + EPS) * ln_ref[...]


def _combine(dest3, x1, gate_rows, ln_final, ys):
    n, d = x1.shape
    tm = TM_COMBINE
    return pl.pallas_call(
        _combine_kernel,
        grid=(n // tm,),
        in_specs=[
            pl.BlockSpec((1, 2, tm), lambda i: (i, 0, 0), memory_space=pltpu.SMEM),
            pl.BlockSpec((tm, d), lambda i: (i, 0)),
            pl.BlockSpec((tm, LANES), lambda i: (i, 0)),
            pl.BlockSpec((1, d), lambda i: (0, 0)),
            pl.BlockSpec(memory_space=pl.ANY),
        ],
        out_specs=pl.BlockSpec((tm, d), lambda i: (i, 0)),
        out_shape=jax.ShapeDtypeStruct((n, d), F32),
        scratch_shapes=[pltpu.VMEM((2, tm, d // 2), U32), pltpu.SemaphoreType.DMA(())],
        compiler_params=_cparams(1),
        name="combine",
    )(dest3, x1, gate_rows, ln_final, ys)


def kernel(x, positions, ln_mix, w_in, pe_cmp_k, w_cmp_k1, w_cmp_k2, pe_cmp_v, w_cmp_v1, w_cmp_v2, w_pool, b_pool, pool_scale, gn_nsa, gn_pool, w_out, ln_moe, w_router_group, b_router_group, w_router_expert, b_router_expert, w_gate, w_up, w_down, ln_final):
    batch, seq, d = x.shape
    n = batch * seq
    depth = w_in.shape[0]
    nsa_w = N_HEADS * HEAD_DIM
    kv_w = N_KV * HEAD_DIM
    cut_gates = nsa_w + 6 * kv_w
    cut_pool = cut_gates + N_GATES

    inv_freq = ROPE_THETA ** (-jnp.arange(0, ROT_DIM, 2, dtype=F32) / ROT_DIM)
    ang = positions.astype(F32).reshape(n, 1) * inv_freq[None, :]
    ang = jnp.concatenate([-ang, ang, jnp.zeros((n, HEAD_DIM - ROT_DIM), F32)], axis=1)
    angt = ang.T
    kblk = (jnp.arange(seq, dtype=I32)[:, None] // SEL_LEN == jnp.arange(LANES, dtype=I32)[None, :]).astype(BF16)

    x2d = x.reshape(n, d)
    assert depth == 1, "single-layer operation"
    for l in range(depth):
        wi = w_in[l]
        o_kc, o_ks = nsa_w, nsa_w + 2 * kv_w
        o_vs, o_kw, o_vw = o_ks + kv_w, o_ks + 2 * kv_w, o_ks + 3 * kv_w
        gate_rows_pad = -(-N_GATES // SUBLANES) * SUBLANES
        wqt = wi[:, :nsa_w].T.astype(BF16)
        wvt = jnp.concatenate([wi[:, o_vs:o_kw], wi[:, o_vw:cut_gates]], axis=1).T.astype(BF16)
        wgt = jnp.pad(wi[:, cut_gates:cut_pool], ((0, 0), (0, gate_rows_pad - N_GATES))).T.astype(BF16)
        wkc = wi[:, o_kc:o_ks].astype(BF16)
        wk = jnp.concatenate([wi[:, o_ks:o_vs], wi[:, o_kw:o_vw]], axis=1).astype(BF16)
        wu = wi[:, cut_pool:].astype(BF16)
        qnt, qrt, kvc, kk, vt, gt, u = _inproj(x2d, ln_mix[l][None], ang, angt, wqt, wvt, wgt, wkc, wk, wu,
                                               batch, seq)

        half = CMP_LEN * HEAD_DIM // 2
        def stack_w1(w):
            return jnp.concatenate([w[:half], w[half:]], axis=1)
        w1 = jnp.stack([stack_w1(w_cmp_k1[l]), stack_w1(w_cmp_v1[l])]).astype(BF16)
        w2 = jnp.stack([w_cmp_k2[l], w_cmp_v2[l]]).astype(BF16)
        def pe_rows(pe):
            return jnp.pad(pe.reshape(2, half), ((0, SUBLANES - 2), (0, 0)))
        pe2 = jnp.stack([pe_rows(pe_cmp_k[l]), pe_rows(pe_cmp_v[l])]).astype(BF16)
        kvc2 = kvc.reshape(batch, 4, seq // CMP_STRIDE, CMP_STRIDE * HEAD_DIM)
        cmp = _compress(kvc2, pe2, w1, w2)

        mix_nsa = _nsa(qnt, qrt, cmp, kk, vt, kblk, gt, gn_nsa[l][None])
        mix_pool = _pool(u.reshape(batch, seq, -1), w_pool[l].astype(BF16), b_pool[l][None],
                         pool_scale[l][None], gn_pool[l][None])

        ng, ne = N_EXPERT_GROUPS, EXPERTS_PER_GROUP
        w_r = jnp.concatenate([
            w_router_group[l].T, jnp.zeros((ne - ng, d), F32),
            jnp.transpose(w_router_expert[l], (0, 2, 1)).reshape(ng * ne, d),
            jnp.zeros((LANES - ne - ng * ne, d), F32)], axis=0).astype(BF16)
        b_r = jnp.concatenate([
            b_router_group[l], jnp.zeros((ne - ng,), F32), b_router_expert[l].reshape(-1),
            jnp.zeros((LANES - ne - ng * ne,), F32)])
        b_r = jnp.broadcast_to(b_r[:, None], (LANES, LANES))
        x1, h2, eid, gate = _outproj(mix_nsa, mix_pool, x2d, w_out[l].astype(BF16), ln_moe[l][None], w_r, b_r)

        dest, meta = _dispatch(eid)
        cap = (2 * n // TM_MOE + N_EXPERTS) * TM_MOE
        def per_tile(tm):
            return jnp.transpose(dest[:2].reshape(2, n // tm, tm), (1, 0, 2))
        xs = _scatter(per_tile(TS_SCATTER), h2, jnp.zeros((cap, d // 2), U32))
        ys = _experts(meta, xs, w_gate[l], w_up[l], w_down[l])
        dest_tok = per_tile(TM_COMBINE)
        gate_rows = jnp.pad(gate[:2].T, ((0, 0), (0, LANES - 2)))
        x2d = _combine(dest_tok, x1, gate_rows, ln_final[None], ys)
    return x2d.reshape(batch, seq, d)
```

```python
import functools

import jax
import jax.numpy as jnp
from jax import lax
from jax.experimental import pallas as pl
from jax.experimental.pallas import tpu as pltpu

F32 = jnp.float32
BF16 = jnp.bfloat16
I32 = jnp.int32

HEAD_DIM = 128
N_HEADS = 8
N_KV = 2
GQA = N_HEADS // N_KV
N_GATES = 3 * N_HEADS
ROT_DIM = HEAD_DIM // 4
ROPE_THETA = 500000.0
CMP_LEN = 32
CMP_STRIDE = 16
CMP_HIDDEN = 2 * HEAD_DIM
SEL_LEN = 64
SEL_TOPK = 16
N_LOCAL = 2
WINDOW = 512
POOL_SIZES = (2, 4, 8, 16)
N_EXPERT_GROUPS = 4
EXPERTS_PER_GROUP = 8
N_EXPERTS = N_EXPERT_GROUPS * EXPERTS_PER_GROUP
EPS = 1e-6
NEG = -1e30
BIG = 1e30
SCALE = HEAD_DIM ** -0.5

LANES = 128
SUBLANES = 8
VMEM_LIMIT_BYTES = 56 * 1024 * 1024

TM_PROJ = 256
TQ = 256
TK = 256
SEL_MASK_BIAS = 32768.0
TM_MOE = 256
TS_SCATTER = 512
TM_COMBINE = 128
DMA_ISSUE_UNROLL = 8
T_RANK = 512


def _cparams(n_axes, flags=None):
    return pltpu.CompilerParams(dimension_semantics=("arbitrary",) * n_axes,
                                vmem_limit_bytes=VMEM_LIMIT_BYTES, flags=flags)


def _dot(a, b):
    return jnp.dot(a, b, preferred_element_type=F32)


def _dot_nt(a, b):
    return lax.dot_general(a, b, (((1,), (1,)), ((), ())), preferred_element_type=F32)


U32 = jnp.uint32
_HI16 = 0xFFFF0000


def _pack_bf16_halves(x):
    c = x.shape[1] // 2
    bits = lambda t: lax.bitcast_convert_type(t.astype(BF16).astype(F32), U32)
    return lax.shift_right_logical(bits(x[:, :c]), U32(16)) | (bits(x[:, c:]) & U32(_HI16))


def _unpack_bf16_halves(w):
    lo = lax.bitcast_convert_type(lax.shift_left(w, U32(16)), F32)
    hi = lax.bitcast_convert_type(w & U32(_HI16), F32)
    return lo, hi


def _rope(t, cos, sin, axis):
    idx = lax.broadcasted_iota(I32, t.shape, axis)
    partner = jnp.where(idx < ROT_DIM // 2,
                        pltpu.roll(t, HEAD_DIM - ROT_DIM // 2, axis),
                        pltpu.roll(t, ROT_DIM // 2, axis))
    return t * cos + partner * sin


def _inproj_kernel(x_ref, ln_ref, ang_ref, angt_ref, wqt_ref, wvt_ref, wgt_ref, wkc_ref, wk_ref, wu_ref,
                   qnt_ref, qrt_ref, kvc_ref, kk_ref, vt_ref, gt_ref, u_ref):
    x = x_ref[...]
    ms = jnp.mean(x * x, axis=-1, keepdims=True)
    hb = (x * lax.rsqrt(ms + EPS) * ln_ref[...]).astype(BF16)
    chunk = 4 * HEAD_DIM

    angt = angt_ref[...]
    cost, sint = jnp.cos(angt), jnp.sin(angt)
    for ch in range(2):
        rt = _dot_nt(wqt_ref[ch * chunk:(ch + 1) * chunk, :], hb)
        for j in range(4):
            t = rt[j * HEAD_DIM:(j + 1) * HEAD_DIM, :]
            qnt_ref[0, ch * 4 + j] = (t * SCALE).astype(BF16)
            qrt_ref[0, ch * 4 + j] = (_rope(t, cost, sint, 0) * SCALE).astype(BF16)
    rt = _dot_nt(wvt_ref[...], hb)
    for j in range(4):
        vt_ref[0, j, 0] = rt[j * HEAD_DIM:(j + 1) * HEAD_DIM, :].astype(BF16)
    gt_ref[...] = _dot_nt(wgt_ref[...], hb)

    r = _dot(hb, wkc_ref[...])
    for j in range(4):
        kvc_ref[0, j] = r[:, j * HEAD_DIM:(j + 1) * HEAD_DIM].astype(BF16)
    ang = ang_ref[...]
    cos, sin = jnp.cos(ang), jnp.sin(ang)
    r = _dot(hb, wk_ref[...])
    for j in range(4):
        kk_ref[0, j] = _rope(r[:, j * HEAD_DIM:(j + 1) * HEAD_DIM], cos, sin, 1).astype(BF16)
    for ch in range(2):
        u_ref[:, ch * chunk:(ch + 1) * chunk] = _dot(hb, wu_ref[:, ch * chunk:(ch + 1) * chunk])


def _inproj(x2d, ln, ang, angt, wqt, wvt, wgt, wkc, wk, wu, batch, seq):
    n, d = x2d.shape
    tm = TM_PROJ
    assert tm == TK
    spb = seq // tm
    const = lambda i: (0, 0)
    resident = lambda w: pl.BlockSpec(w.shape, const, pipeline_mode=pl.Buffered(1))
    row_heads = lambda i: (i // spb, 0, i % spb, 0)
    col_heads = lambda i: (i // spb, 0, 0, i % spb)
    return pl.pallas_call(
        _inproj_kernel,
        grid=(n // tm,),
        in_specs=[
            pl.BlockSpec((tm, d), lambda i: (i, 0)),
            pl.BlockSpec((1, d), const),
            pl.BlockSpec((tm, HEAD_DIM), lambda i: (i, 0)),
            pl.BlockSpec((HEAD_DIM, tm), lambda i: (0, i)),
            resident(wqt), resident(wvt), resident(wgt), resident(wkc), resident(wk), resident(wu),
        ],
        out_specs=[
            pl.BlockSpec((1, N_HEADS, HEAD_DIM, tm), col_heads),
            pl.BlockSpec((1, N_HEADS, HEAD_DIM, tm), col_heads),
            pl.BlockSpec((1, 4, tm, HEAD_DIM), row_heads),
            pl.BlockSpec((1, 4, tm, HEAD_DIM), row_heads),
            pl.BlockSpec((1, 4, 1, HEAD_DIM, tm), lambda i: (i // spb, 0, i % spb, 0, 0)),
            pl.BlockSpec((wgt.shape[0], tm), lambda i: (0, i)),
            pl.BlockSpec((tm, wu.shape[1]), lambda i: (i, 0)),
        ],
        out_shape=[
            jax.ShapeDtypeStruct((batch, N_HEADS, HEAD_DIM, seq), BF16),
            jax.ShapeDtypeStruct((batch, N_HEADS, HEAD_DIM, seq), BF16),
            jax.ShapeDtypeStruct((batch, 4, seq, HEAD_DIM), BF16),
            jax.ShapeDtypeStruct((batch, 4, seq, HEAD_DIM), BF16),
            jax.ShapeDtypeStruct((batch, 4, spb, HEAD_DIM, tm), BF16),
            jax.ShapeDtypeStruct((wgt.shape[0], n), F32),
            jax.ShapeDtypeStruct((n, wu.shape[1]), F32),
        ],
        compiler_params=_cparams(1),
        name="inproj",
    )(x2d, ln, ang, angt, wqt, wvt, wgt, wkc, wk, wu)


def _compress_kernel(t_ref, pe_ref, w1_ref, w2_ref, o_ref):
    t2 = t_ref[0, 0]
    w1 = w1_ref[0]
    a = _dot(t2, w1)
    pb = _dot(pe_ref[0], w1)
    bias = pb[0:1, :CMP_HIDDEN] + pb[1:2, CMP_HIDDEN:]
    nrow = a.shape[0]
    nxt = pltpu.roll(a[:, CMP_HIDDEN:], nrow - 1, 0)
    hid = jax.nn.gelu(a[:, :CMP_HIDDEN] + nxt + bias)
    out = _dot(hid.astype(BF16), w2_ref[0])
    row = lax.broadcasted_iota(I32, out.shape, 0)
    out = jnp.where(row < nrow - 1, out, 0.0)
    is_value = pl.program_id(1) >= N_KV
    o_ref[0, 0] = jnp.where(is_value, out.T, out).astype(BF16)


def _compress(kvc2, pe2, w1, w2):
    batch, four, nrow, width = kvc2.shape
    return pl.pallas_call(
        _compress_kernel,
        grid=(batch, four),
        in_specs=[
            pl.BlockSpec((1, 1, nrow, width), lambda b, j: (b, j, 0, 0)),
            pl.BlockSpec((1, SUBLANES, width), lambda b, j: (j // N_KV, 0, 0)),
            pl.BlockSpec((1, width, 2 * CMP_HIDDEN), lambda b, j: (j // N_KV, 0, 0)),
            pl.BlockSpec((1, CMP_HIDDEN, HEAD_DIM), lambda b, j: (j // N_KV, 0, 0)),
        ],
        out_specs=pl.BlockSpec((1, 1, nrow, HEAD_DIM), lambda b, j: (b, j, 0, 0)),
        out_shape=jax.ShapeDtypeStruct((batch, four, nrow, HEAD_DIM), BF16),
        compiler_params=_cparams(2),
        name="compress",
    )(kvc2, pe2, w1, w2)


def _nsa_kernel(qnt_ref, qrt_ref, cmp_ref, kk_ref, vt_ref, kblk_ref, gt_ref, gn_ref, o_ref,
                m_scr, l_scr, acc_scr, qa_scr, ocmp_scr, s_scr, o_scr):
    qi = pl.program_id(1)
    tq = qnt_ref.shape[3]
    rows = GQA * tq
    n_cmp_pad = cmp_ref.shape[2]
    n_sel = kk_ref.shape[2] // SEL_LEN
    q0 = qi * tq

    m_scr[...] = jnp.full(m_scr.shape, NEG, F32)
    l_scr[...] = jnp.zeros(l_scr.shape, F32)
    acc_scr[...] = jnp.zeros(acc_scr.shape, F32)

    def scores(item, slot):
        c, g, q_rows, k_tile, v_t, bias = item
        s_scr[slot] = _dot(k_tile, qa_scr[g, :q_rows, :])

    def softmax_pv(item, slot):
        c, g, q_rows, k_tile, v_t, bias = item
        s = s_scr[slot]
        if bias is not None:
            s = s + bias
        m_old = m_scr[c]
        m_new = jnp.maximum(m_old, jnp.max(s, axis=0, keepdims=True))
        alpha = jnp.exp(m_old - m_new)
        p = jnp.exp(s - m_new)
        l_scr[c] = alpha * l_scr[c] + jnp.sum(p, axis=0, keepdims=True)
        acc_scr[c] = alpha * acc_scr[c] + _dot(v_t, p.astype(BF16))
        m_scr[c] = m_new

    def flash_out(c):
        return acc_scr[c] / l_scr[c]

    def tile_heads(a):
        return jnp.concatenate([a] * GQA, axis=1)

    kk_i = lax.broadcasted_iota(I32, (TK, tq), 0)
    qq_i = lax.broadcasted_iota(I32, (TK, tq), 1)
    causal_bias = tile_heads(jnp.where(kk_i <= qq_i, 0.0, NEG))
    band_bias = tile_heads(jnp.where(kk_i > qq_i, 0.0, NEG))
    n_win = WINDOW // TK

    for g in range(N_KV):
        qn4 = jnp.concatenate([qnt_ref[0, h] for h in range(g * GQA, (g + 1) * GQA)], axis=1)
        s = _dot(cmp_ref[0, g], qn4)
        n_idx = lax.broadcasted_iota(I32, (n_cmp_pad, rows), 0)
        t_idx = q0 + jnp.bitwise_and(lax.broadcasted_iota(I32, (n_cmp_pad, rows), 1), tq - 1)
        cmp_ok = n_idx * CMP_STRIDE + (CMP_LEN - 1) <= t_idx
        s = jnp.where(cmp_ok, s, NEG)
        m = jnp.max(s, axis=0, keepdims=True)
        e = jnp.where(cmp_ok, jnp.exp(s - m), 0.0)
        l = jnp.sum(e, axis=0, keepdims=True)
        p = e * jnp.where(l > 0.0, 1.0 / l, 0.0)
        pb = p.astype(BF16)
        ocmp_scr[g] = _dot(cmp_ref[0, N_KV + g], pb)

        jn = lax.broadcasted_iota(I32, (n_sel, n_cmp_pad), 0) * SEL_LEN
        cn = lax.broadcasted_iota(I32, (n_sel, n_cmp_pad), 1) * CMP_STRIDE
        ov = jnp.clip(jnp.minimum(cn + CMP_LEN, jn + SEL_LEN) - jnp.maximum(cn, jn), 0, None)
        w_t = (ov.astype(F32) * (1.0 / CMP_LEN)).astype(BF16)
        imp4 = _dot(w_t, pb)
        imp = imp4[:, 0:tq]
        for r in range(1, GQA):
            imp = imp + imp4[:, r * tq:(r + 1) * tq]
        j_blk = lax.broadcasted_iota(I32, (n_sel, tq), 0)
        t_q = q0 + lax.broadcasted_iota(I32, (n_sel, tq), 1)
        sel_ok = j_blk * SEL_LEN <= t_q
        back = t_q // SEL_LEN - j_blk
        forced = (j_blk == 0) | ((back >= 0) & (back < N_LOCAL))
        val = jnp.where(sel_ok & forced, BIG, jnp.where(sel_ok, imp, -BIG))
        rank = jnp.zeros((n_sel, tq), F32)
        for i in range(n_sel):
            vi = val[i:i + 1, :]
            beats = (vi > val) | ((vi == val) & (j_blk > i))
            rank = rank + beats.astype(F32)
        keep = (rank < float(min(SEL_TOPK, n_sel))) & sel_ok
        bias_t = jnp.where(keep, 0.0, -SEL_MASK_BIAS)
        bias_t = jnp.concatenate([bias_t, jnp.zeros((HEAD_DIM - n_sel, tq), F32)], axis=0)
        qa_scr[g, :HEAD_DIM, :] = jnp.concatenate(
            [qrt_ref[0, h] for h in range(g * GQA, (g + 1) * GQA)], axis=1)
        qa_scr[g, HEAD_DIM:, :] = tile_heads(bias_t.astype(BF16))

    def sel_item(g, kj, bias):
        st = pl.multiple_of(kj * TK, TK)
        k_aug = jnp.concatenate([kk_ref[0, g, pl.ds(st, TK), :], kblk_ref[pl.ds(st, TK), :]], axis=1)
        return (g, g, 2 * HEAD_DIM, k_aug, vt_ref[0, g, kj], bias)

    def win_item(g, back):
        kj = jnp.maximum(qi - back, 0)
        st = pl.multiple_of(kj * TK, TK)
        off = jnp.where(qi >= back, 0.0, NEG)
        bias = causal_bias if back == 0 else (band_bias + off if back == n_win else off)
        return (N_KV + g, g, HEAD_DIM, kk_ref[0, N_KV + g, pl.ds(st, TK), :], vt_ref[0, N_KV + g, kj], bias)

    assert N_KV == 2
    scores(sel_item(0, 0, None), 0)

    def sel_body(kj, carry):
        scores(sel_item(1, kj, None), 1)
        softmax_pv(sel_item(0, kj, None), 0)
        scores(sel_item(0, kj + 1, None), 0)
        softmax_pv(sel_item(1, kj, None), 1)
        return carry

    lax.fori_loop(0, qi, sel_body, 0)
    tail = ([sel_item(g, qi, causal_bias) for g in range(N_KV)]
            + [win_item(g, back) for back in range(n_win, -1, -1) for g in range(N_KV)])
    for i, item in enumerate(tail):
        if i + 1 < len(tail):
            scores(tail[i + 1], (i + 1) % 2)
        softmax_pv(item, i % 2)

    gsig = jax.nn.sigmoid(gt_ref[...])
    for g in range(N_KV):
        o_cmp, o_sel, o_win = ocmp_scr[g], flash_out(g), flash_out(N_KV + g)
        for r in range(GQA):
            h = g * GQA + r
            sl = slice(r * tq, (r + 1) * tq)
            o_h = (gsig[3 * h:3 * h + 1, :] * o_cmp[:, sl]
                   + gsig[3 * h + 1:3 * h + 2, :] * o_sel[:, sl]
                   + gsig[3 * h + 2:3 * h + 3, :] * o_win[:, sl])
            o_scr[:, h * HEAD_DIM:(h + 1) * HEAD_DIM] = o_h.T

    o = o_scr[...]
    ms = jnp.mean(o * o, axis=-1, keepdims=True)
    o_ref[...] = (o * lax.rsqrt(ms + EPS) * gn_ref[...]).astype(BF16)


def _nsa(qnt, qrt, cmp, kk, vt, kblk, gt, gn):
    batch, _, _, seq = qnt.shape
    assert TQ == TK and WINDOW % TK == 0 and TQ & (TQ - 1) == 0
    nq = seq // TQ
    width = N_HEADS * HEAD_DIM
    return pl.pallas_call(
        _nsa_kernel,
        grid=(batch, nq),
        in_specs=[
            pl.BlockSpec((1, N_HEADS, HEAD_DIM, TQ), lambda b, i: (b, 0, 0, i)),
            pl.BlockSpec((1, N_HEADS, HEAD_DIM, TQ), lambda b, i: (b, 0, 0, i)),
            pl.BlockSpec((1,) + cmp.shape[1:], lambda b, i: (b, 0, 0, 0)),
            pl.BlockSpec((1,) + kk.shape[1:], lambda b, i: (b, 0, 0, 0)),
            pl.BlockSpec((1,) + vt.shape[1:], lambda b, i: (b, 0, 0, 0, 0)),
            pl.BlockSpec((seq, LANES), lambda b, i: (0, 0)),
            pl.BlockSpec((gt.shape[0], TQ), lambda b, i: (0, b * nq + i)),
            pl.BlockSpec((1, width), lambda b, i: (0, 0)),
        ],
        out_specs=pl.BlockSpec((TQ, width), lambda b, i: (b * nq + i, 0)),
        out_shape=jax.ShapeDtypeStruct((batch * seq, width), BF16),
        scratch_shapes=[
            pltpu.VMEM((2 * N_KV, 1, GQA * TQ), F32),
            pltpu.VMEM((2 * N_KV, 1, GQA * TQ), F32),
            pltpu.VMEM((2 * N_KV, HEAD_DIM, GQA * TQ), F32),
            pltpu.VMEM((N_KV, 2 * HEAD_DIM, GQA * TQ), BF16),
            pltpu.VMEM((N_KV, HEAD_DIM, GQA * TQ), F32),
            pltpu.VMEM((2, TK, GQA * TQ), F32),
            pltpu.VMEM((TQ, width), F32),
        ],
        compiler_params=_cparams(2),
        name="nsa",
    )(qnt, qrt, cmp, kk, vt, kblk, gt, gn)


def _pool_kernel(u_ref, up_ref, w_ref, b_ref, sc_ref, gn_ref, o_ref, y_scr):
    si = pl.program_id(1)
    tm = u_ref.shape[1]
    halo = max(POOL_SIZES)
    cur = u_ref[0]
    prev = jnp.where(si > 0, up_ref[0, tm - halo:, :], 0.0)
    ext = jnp.concatenate([prev, cur], axis=0)
    t1 = (si * tm + 1 + lax.broadcasted_iota(I32, (tm, 1), 0)).astype(F32)
    cg = cur.shape[1] // len(POOL_SIZES)
    for gi, w in enumerate(POOL_SIZES):
        sl = slice(gi * cg, (gi + 1) * cg)
        acc = ext[:, sl]
        span = 1
        while span < w:
            acc = acc + jnp.concatenate([jnp.zeros((span, cg), F32), acc[:-span]], axis=0)
            span *= 2
        mean = acc[halo:] / jnp.minimum(t1, float(w))
        d = (mean - cur[:, sl]).astype(BF16)
        y = _dot(d, w_ref[gi]) + b_ref[:, sl]
        y_scr[:, sl] = y * sc_ref[:, sl]
    y = y_scr[...]
    ms = jnp.mean(y * y, axis=-1, keepdims=True)
    o_ref[...] = (y * lax.rsqrt(ms + EPS) * gn_ref[...]).astype(BF16)


def _pool(u3, w_pool, b_pool, pool_scale, gn_pool):
    batch, seq, c = u3.shape
    tm = TM_PROJ
    ns = seq // tm
    vec = lambda b, i: (0, 0)
    return pl.pallas_call(
        _pool_kernel,
        grid=(batch, ns),
        in_specs=[
            pl.BlockSpec((1, tm, c), lambda b, i: (b, i, 0)),
            pl.BlockSpec((1, tm, c), lambda b, i: (b, jnp.maximum(i - 1, 0), 0)),
            pl.BlockSpec(w_pool.shape, lambda b, i: (0, 0, 0)),
            pl.BlockSpec((1, c), vec),
            pl.BlockSpec((1, c), vec),
            pl.BlockSpec((1, c), vec),
        ],
        out_specs=pl.BlockSpec((tm, c), lambda b, i: (b * ns + i, 0)),
        out_shape=jax.ShapeDtypeStruct((batch * seq, c), BF16),
        scratch_shapes=[pltpu.VMEM((tm, c), F32)],
        compiler_params=_cparams(2),
        name="pool",
    )(u3, u3, w_pool, b_pool, pool_scale, gn_pool)


def _outproj_kernel(mn_ref, mp_ref, x_ref, w_ref, ln_ref, wr_ref, br_ref,
                    x1_ref, h2_ref, eid_ref, gate_ref):
    tm, d = x_ref.shape
    half = mn_ref.shape[1]
    chunk = 512
    mn = mn_ref[...]
    mp = mp_ref[...]
    ssq = jnp.zeros((tm, 1), F32)
    for c in range(d // chunk):
        sl = slice(c * chunk, (c + 1) * chunk)
        x1 = x_ref[:, sl] + (_dot(mn, w_ref[:half, sl]) + _dot(mp, w_ref[half:, sl]))
        x1_ref[:, sl] = x1
        ssq = ssq + jnp.sum(x1 * x1, axis=-1, keepdims=True)
    h2 = x1_ref[...] * lax.rsqrt(ssq * (1.0 / d) + EPS) * ln_ref[...]
    h2_ref[...] = _pack_bf16_halves(h2)

    logits = _dot_nt(wr_ref[...], h2.astype(BF16)) + br_ref[:, 0:1]
    ng, ne = N_EXPERT_GROUPS, EXPERTS_PER_GROUP
    row = lax.broadcasted_iota(I32, (ne, tm), 0)
    lg = jnp.where(row < ng, logits[0:ne], NEG)
    mg = jnp.max(lg, axis=0, keepdims=True)
    g_sel = jnp.min(jnp.where(lg == mg, row, ne), axis=0, keepdims=True)
    p_g = 1.0 / jnp.sum(jnp.exp(lg - mg), axis=0, keepdims=True)
    le = jnp.zeros((ne, tm), F32)
    for g in range(ng):
        le = jnp.where(g_sel == g, logits[ne * (g + 1):ne * (g + 2)], le)
    ex = jnp.exp(le - jnp.max(le, axis=0, keepdims=True))
    pe = ex / jnp.sum(ex, axis=0, keepdims=True)
    p1 = jnp.max(pe, axis=0, keepdims=True)
    i1 = jnp.min(jnp.where(pe == p1, row, ne), axis=0, keepdims=True)
    rest = jnp.where(row == i1, -1.0, pe)
    p2 = jnp.max(rest, axis=0, keepdims=True)
    i2 = jnp.min(jnp.where(rest == p2, row, ne), axis=0, keepdims=True)
    psum = p1 + p2
    zero_i = jnp.zeros((ne, tm), I32)
    eid = jnp.where(row == 0, g_sel * ne + i1, jnp.where(row == 1, g_sel * ne + i2, zero_i))
    gate = jnp.where(row == 0, p_g * p1 / psum, jnp.where(row == 1, p_g * p2 / psum, 0.0))
    eid_ref[...] = eid
    gate_ref[...] = gate


def _outproj(mix_nsa, mix_pool, x2d, w_out, ln_moe, w_router_t, b_router):
    n, d = x2d.shape
    tm = TM_PROJ
    half = mix_nsa.shape[1]
    const = lambda i: (0, 0)
    return pl.pallas_call(
        _outproj_kernel,
        grid=(n // tm,),
        in_specs=[
            pl.BlockSpec((tm, half), lambda i: (i, 0)),
            pl.BlockSpec((tm, half), lambda i: (i, 0)),
            pl.BlockSpec((tm, d), lambda i: (i, 0)),
            pl.BlockSpec(w_out.shape, const, pipeline_mode=pl.Buffered(1)),
            pl.BlockSpec((1, d), const),
            pl.BlockSpec(w_router_t.shape, const),
            pl.BlockSpec(b_router.shape, const),
        ],
        out_specs=[
            pl.BlockSpec((tm, d), lambda i: (i, 0)),
            pl.BlockSpec((tm, d // 2), lambda i: (i, 0)),
            pl.BlockSpec((SUBLANES, tm), lambda i: (0, i)),
            pl.BlockSpec((SUBLANES, tm), lambda i: (0, i)),
        ],
        out_shape=[
            jax.ShapeDtypeStruct((n, d), F32),
            jax.ShapeDtypeStruct((n, d // 2), U32),
            jax.ShapeDtypeStruct((SUBLANES, n), I32),
            jax.ShapeDtypeStruct((SUBLANES, n), F32),
        ],
        compiler_params=_cparams(1),
        name="outproj",
    )(mix_nsa, mix_pool, x2d, w_out, ln_moe, w_router_t, b_router)


def _dispatch_kernel(eid_ref, dest_ref, meta_ref, rank_scr):
    n = eid_ref.shape[1]
    t = T_RANK
    row = lax.broadcasted_iota(I32, (N_EXPERTS, t), 0)
    before = (lax.broadcasted_iota(I32, (t, t), 0) < lax.broadcasted_iota(I32, (t, t), 1)).astype(BF16)
    carry = jnp.zeros((N_EXPERTS, 1), F32)
    for k in range(2):
        for j in range(n // t):
            sl = slice(j * t, (j + 1) * t)
            oh = row == eid_ref[k:k + 1, sl]
            ohf = oh.astype(F32)
            prior = _dot(ohf.astype(BF16), before) + carry
            rank_scr[k:k + 1, sl] = jnp.sum(jnp.where(oh, prior, 0.0), axis=0, keepdims=True)
            carry = carry + jnp.sum(ohf, axis=1, keepdims=True)
    counts = jnp.broadcast_to(carry, (N_EXPERTS, LANES))
    padded = jnp.floor((counts + (TM_MOE - 1.0)) * (1.0 / TM_MOE)) * TM_MOE
    ends = padded
    r_idx = lax.broadcasted_iota(I32, (N_EXPERTS, LANES), 0)
    sh = 1
    while sh < N_EXPERTS:
        ends = ends + jnp.where(r_idx >= sh, pltpu.roll(ends, sh, 0), 0.0)
        sh *= 2
    starts = (ends - padded)[:, 0:1]
    for k in range(2):
        for j in range(n // t):
            sl = slice(j * t, (j + 1) * t)
            oh = row == eid_ref[k:k + 1, sl]
            base = jnp.sum(jnp.where(oh, starts, 0.0), axis=0, keepdims=True)
            dest_ref[k:k + 1, sl] = (rank_scr[k:k + 1, sl] + base).astype(I32)
    for k in range(2, SUBLANES):
        dest_ref[k:k + 1, :] = jnp.zeros((1, n), I32)
    blk_start = lax.broadcasted_iota(I32, (N_EXPERTS, LANES), 1).astype(F32) * TM_MOE
    blk_expert = jnp.sum((ends <= blk_start).astype(F32), axis=0, keepdims=True)
    blk_expert = jnp.minimum(blk_expert, N_EXPERTS - 1.0)
    n_used = ends[N_EXPERTS - 1:N_EXPERTS, :] * (1.0 / TM_MOE)
    r8 = lax.broadcasted_iota(I32, (SUBLANES, LANES), 0)
    meta = jnp.where(r8 == 0, blk_expert, jnp.where(r8 == 1, n_used, 0.0))
    meta_ref[...] = meta.astype(I32)


def _dispatch(eid):
    n = eid.shape[1]
    return pl.pallas_call(
        _dispatch_kernel,
        out_shape=[jax.ShapeDtypeStruct((SUBLANES, n), I32),
                   jax.ShapeDtypeStruct((SUBLANES, LANES), I32)],
        scratch_shapes=[pltpu.VMEM((SUBLANES, n), F32)],
        compiler_params=pltpu.CompilerParams(vmem_limit_bytes=VMEM_LIMIT_BYTES),
        name="dispatch",
    )(eid)


def _row_copy(src_ref, src_row, dst_ref, dst_row, sem):
    return pltpu.make_async_copy(src_ref.at[pl.ds(src_row, 1)], dst_ref.at[pl.ds(dst_row, 1)], sem)


def _scatter_kernel(dest_ref, h_ref, init_ref, xs_ref, sem):
    del init_ref
    tm = h_ref.shape[0]

    def start(r, c):
        for k in range(2):
            _row_copy(h_ref, r, xs_ref, dest_ref[0, k, r], sem).start()
        return c

    lax.fori_loop(0, tm, start, 0, unroll=DMA_ISSUE_UNROLL)
    for k in range(2):
        pltpu.make_async_copy(h_ref, xs_ref.at[pl.ds(0, tm)], sem).wait()


def _scatter(dest3, h2, xs_init):
    nsteps, _, tm = dest3.shape
    d = h2.shape[1]
    return pl.pallas_call(
        _scatter_kernel,
        grid=(nsteps,),
        in_specs=[
            pl.BlockSpec((1, 2, tm), lambda i: (i, 0, 0), memory_space=pltpu.SMEM),
            pl.BlockSpec((tm, d), lambda i: (i, 0)),
            pl.BlockSpec(memory_space=pl.ANY),
        ],
        out_specs=pl.BlockSpec(memory_space=pl.ANY),
        out_shape=jax.ShapeDtypeStruct(xs_init.shape, xs_init.dtype),
        scratch_shapes=[pltpu.SemaphoreType.DMA(())],
        input_output_aliases={2: 0},
        compiler_params=_cparams(1),
        name="scatter",
    )(dest3, h2, xs_init)


def _experts_kernel(meta_ref, xs_ref, wg_hbm, wu_hbm, wd_hbm, ys_ref,
                    wg_buf, wu_buf, wd_buf, wg16, wu16, wd16, ord_ref, sem):
    c = pl.program_id(0)
    n_used = meta_ref[1, 0]
    e = meta_ref[0, c]
    first = (c == 0) | (e != meta_ref[0, jnp.maximum(c - 1, 0)])

    def fetch(expert, slot):
        return [pltpu.make_async_copy(w.at[expert], buf.at[slot], sem.at[slot, i])
                for i, (w, buf) in enumerate(((wg_hbm, wg_buf), (wu_hbm, wu_buf), (wd_hbm, wd_buf)))]

    @pl.when(c == 0)
    def _():
        ord_ref[0] = 0
        for cp in fetch(e, 0):
            cp.start()

    @pl.when(first & (c < n_used))
    def _():
        @pl.when(c > 0)
        def _():
            ord_ref[0] = ord_ref[0] + 1
        slot = ord_ref[0] % 2
        for cp in fetch(e, slot):
            cp.wait()
        nxt = lax.while_loop(lambda j: (j < n_used) & (meta_ref[0, jnp.minimum(j, n_used - 1)] == e),
                             lambda j: j + 1, c + 1)

        @pl.when(nxt < n_used)
        def _():
            for cp in fetch(meta_ref[0, jnp.minimum(nxt, n_used - 1)], 1 - slot):
                cp.start()

        wg16[...] = wg_buf[slot].astype(BF16)
        wu16[...] = wu_buf[slot].astype(BF16)
        wd16[...] = wd_buf[slot].astype(BF16)

    @pl.when(c < n_used)
    def _():
        lo, hi = _unpack_bf16_halves(xs_ref[...])
        lo, hi = lo.astype(BF16), hi.astype(BF16)
        half = lo.shape[1]

        def proj(w_ref):
            return _dot(lo, w_ref[:half, :]) + _dot(hi, w_ref[half:, :])

        act = (jax.nn.silu(proj(wg16)) * proj(wu16)).astype(BF16)
        ys_ref[...] = _pack_bf16_halves(_dot(act, wd16[...]))

    @pl.when(c >= n_used)
    def _():
        ys_ref[...] = jnp.zeros(ys_ref.shape, U32)


def _experts(meta, xs, w_gate, w_up, w_down):
    cap, dw = xs.shape
    nblk = cap // TM_MOE
    d, ff = w_gate.shape[1:]

    def row_map(c, meta):
        return (jnp.minimum(c, meta[1, 0] - 1), 0)

    hbm = pl.BlockSpec(memory_space=pl.ANY)
    return pl.pallas_call(
        _experts_kernel,
        grid_spec=pltpu.PrefetchScalarGridSpec(
            num_scalar_prefetch=1,
            grid=(nblk,),
            in_specs=[pl.BlockSpec((TM_MOE, dw), row_map), hbm, hbm, hbm],
            out_specs=pl.BlockSpec((TM_MOE, dw), lambda c, meta: (c, 0)),
            scratch_shapes=[
                pltpu.VMEM((2, d, ff), F32), pltpu.VMEM((2, d, ff), F32), pltpu.VMEM((2, ff, d), F32),
                pltpu.VMEM((d, ff), BF16), pltpu.VMEM((d, ff), BF16), pltpu.VMEM((ff, d), BF16),
                pltpu.SMEM((1,), I32),
                pltpu.SemaphoreType.DMA((2, 3)),
            ],
        ),
        out_shape=jax.ShapeDtypeStruct((cap, dw), U32),
        compiler_params=_cparams(1),
        name="experts",
    )(meta, xs, w_gate, w_up, w_down)


def _combine_kernel(dest_ref, dest_next_ref, x1_ref, gate_ref, ln_ref, ys_ref, o_ref, ybuf, sem):
    i = pl.program_id(0)
    tm = x1_ref.shape[0]
    slot = i % 2

    def issue(d_ref, s):
        def start(r, c):
            for k in range(2):
                _row_copy(ys_ref, d_ref[0, k, r], ybuf.at[s, k], r, sem.at[s]).start()
            return c

        lax.fori_loop(0, tm, start, 0, unroll=DMA_ISSUE_UNROLL)

    @pl.when(i == 0)
    def _():
        issue(dest_ref, 0)

    @pl.when(i + 1 < pl.num_programs(0))
    def _():
        issue(dest_next_ref, 1 - slot)

    for k in range(2):
        pltpu.make_async_copy(ys_ref.at[pl.ds(0, tm)], ybuf.at[slot, k], sem.at[slot]).wait()
    gate = gate_ref[...]
    lo0, hi0 = _unpack_bf16_halves(ybuf[slot, 0])
    lo1, hi1 = _unpack_bf16_halves(ybuf[slot, 1])
    g0, g1 = gate[:, 0:1], gate[:, 1:2]
    moe = jnp.concatenate([lo0 * g0 + lo1 * g1, hi0 * g0 + hi1 * g1], axis=1)
    x2 = x1_ref[...] + moe
    ms = jnp.mean(x2 * x2, axis=-1, keepdims=True)
    o_ref[...] = x2 * lax.rsqrt(ms + EPS) * ln_ref[...]


def _combine(dest3, x1, gate_rows, ln_final, ys):
    n, d = x1.shape
    tm = TM_COMBINE
    last = n // tm - 1
    return pl.pallas_call(
        _combine_kernel,
        grid=(n // tm,),
        in_specs=[
            pl.BlockSpec((1, 2, tm), lambda i: (i, 0, 0), memory_space=pltpu.SMEM),
            pl.BlockSpec((1, 2, tm), lambda i: (jnp.minimum(i + 1, last), 0, 0), memory_space=pltpu.SMEM),
            pl.BlockSpec((tm, d), lambda i: (i, 0)),
            pl.BlockSpec((tm, LANES), lambda i: (i, 0)),
            pl.BlockSpec((1, d), lambda i: (0, 0)),
            pl.BlockSpec(memory_space=pl.ANY),
        ],
        out_specs=pl.BlockSpec((tm, d), lambda i: (i, 0)),
        out_shape=jax.ShapeDtypeStruct((n, d), F32),
        scratch_shapes=[pltpu.VMEM((2, 2, tm, d // 2), U32), pltpu.SemaphoreType.DMA((2,))],
        compiler_params=_cparams(1),
        name="combine",
    )(dest3, dest3, x1, gate_rows, ln_final, ys)


def kernel(x, positions, ln_mix, w_in, pe_cmp_k, w_cmp_k1, w_cmp_k2, pe_cmp_v, w_cmp_v1, w_cmp_v2, w_pool, b_pool, pool_scale, gn_nsa, gn_pool, w_out, ln_moe, w_router_group, b_router_group, w_router_expert, b_router_expert, w_gate, w_up, w_down, ln_final):
    batch, seq, d = x.shape
    n = batch * seq
    depth = w_in.shape[0]
    nsa_w = N_HEADS * HEAD_DIM
    kv_w = N_KV * HEAD_DIM
    cut_gates = nsa_w + 6 * kv_w
    cut_pool = cut_gates + N_GATES

    inv_freq = ROPE_THETA ** (-jnp.arange(0, ROT_DIM, 2, dtype=F32) / ROT_DIM)
    ang = positions.astype(F32).reshape(n, 1) * inv_freq[None, :]
    ang = jnp.concatenate([-ang, ang, jnp.zeros((n, HEAD_DIM - ROT_DIM), F32)], axis=1)
    angt = ang.T
    kblk = (jnp.arange(seq, dtype=I32)[:, None] // SEL_LEN == jnp.arange(LANES, dtype=I32)[None, :]).astype(BF16)

    x2d = x.reshape(n, d)
    assert depth == 1, "single-layer operation"
    for l in range(depth):
        wi = w_in[l]
        o_kc, o_ks = nsa_w, nsa_w + 2 * kv_w
        o_vs, o_kw, o_vw = o_ks + kv_w, o_ks + 2 * kv_w, o_ks + 3 * kv_w
        gate_rows_pad = -(-N_GATES // SUBLANES) * SUBLANES
        wqt = wi[:, :nsa_w].T.astype(BF16)
        wvt = jnp.concatenate([wi[:, o_vs:o_kw], wi[:, o_vw:cut_gates]], axis=1).T.astype(BF16)
        wgt = jnp.pad(wi[:, cut_gates:cut_pool], ((0, 0), (0, gate_rows_pad - N_GATES))).T.astype(BF16)
        wkc = wi[:, o_kc:o_ks].astype(BF16)
        wk = jnp.concatenate([wi[:, o_ks:o_vs], wi[:, o_kw:o_vw]], axis=1).astype(BF16)
        wu = wi[:, cut_pool:].astype(BF16)
        qnt, qrt, kvc, kk, vt, gt, u = _inproj(x2d, ln_mix[l][None], ang, angt, wqt, wvt, wgt, wkc, wk, wu,
                                               batch, seq)

        half = CMP_LEN * HEAD_DIM // 2
        def stack_w1(w):
            return jnp.concatenate([w[:half], w[half:]], axis=1)
        w1 = jnp.stack([stack_w1(w_cmp_k1[l]), stack_w1(w_cmp_v1[l])]).astype(BF16)
        w2 = jnp.stack([w_cmp_k2[l], w_cmp_v2[l]]).astype(BF16)
        def pe_rows(pe):
            return jnp.pad(pe.reshape(2, half), ((0, SUBLANES - 2), (0, 0)))
        pe2 = jnp.stack([pe_rows(pe_cmp_k[l]), pe_rows(pe_cmp_v[l])]).astype(BF16)
        kvc2 = kvc.reshape(batch, 4, seq // CMP_STRIDE, CMP_STRIDE * HEAD_DIM)
        cmp = _compress(kvc2, pe2, w1, w2)

        mix_nsa = _nsa(qnt, qrt, cmp, kk, vt, kblk, gt, gn_nsa[l][None])
        mix_pool = _pool(u.reshape(batch, seq, -1), w_pool[l].astype(BF16), b_pool[l][None],
                         pool_scale[l][None], gn_pool[l][None])

        ng, ne = N_EXPERT_GROUPS, EXPERTS_PER_GROUP
        w_r = jnp.concatenate([
            w_router_group[l].T, jnp.zeros((ne - ng, d), F32),
            jnp.transpose(w_router_expert[l], (0, 2, 1)).reshape(ng * ne, d),
            jnp.zeros((LANES - ne - ng * ne, d), F32)], axis=0).astype(BF16)
        b_r = jnp.concatenate([
            b_router_group[l], jnp.zeros((ne - ng,), F32), b_router_expert[l].reshape(-1),
            jnp.zeros((LANES - ne - ng * ne,), F32)])
        b_r = jnp.broadcast_to(b_r[:, None], (LANES, LANES))
        x1, h2, eid, gate = _outproj(mix_nsa, mix_pool, x2d, w_out[l].astype(BF16), ln_moe[l][None], w_r, b_r)

        dest, meta = _dispatch(eid)
        cap = (2 * n // TM_MOE + N_EXPERTS) * TM_MOE
        def per_tile(tm):
            return jnp.transpose(dest[:2].reshape(2, n // tm, tm), (1, 0, 2))
        xs = _scatter(per_tile(TS_SCATTER), h2, jnp.zeros((cap, d // 2), U32))
        ys = _experts(meta, xs, w_gate[l], w_up[l], w_down[l])
        dest_tok = per_tile(TM_COMBINE)
        gate_rows = jnp.pad(gate[:2].T, ((0, 0), (0, LANES - 2)))
        x2d = _combine(dest_tok, x1, gate_rows, ln_final[None], ys)
    return x2d.reshape(batch, seq, d)
```

```python
import functools

import jax
import jax.numpy as jnp
from jax import lax
from jax.experimental import pallas as pl
from jax.experimental.pallas import tpu as pltpu

F32 = jnp.float32
BF16 = jnp.bfloat16
I32 = jnp.int32

HEAD_DIM = 128
N_HEADS = 8
N_KV = 2
GQA = N_HEADS // N_KV
N_GATES = 3 * N_HEADS
ROT_DIM = HEAD_DIM // 4
ROPE_THETA = 500000.0
CMP_LEN = 32
CMP_STRIDE = 16
CMP_HIDDEN = 2 * HEAD_DIM
SEL_LEN = 64
SEL_TOPK = 16
N_LOCAL = 2
WINDOW = 512
POOL_SIZES = (2, 4, 8, 16)
N_EXPERT_GROUPS = 4
EXPERTS_PER_GROUP = 8
N_EXPERTS = N_EXPERT_GROUPS * EXPERTS_PER_GROUP
EPS = 1e-6
NEG = -1e30
BIG = 1e30
SCALE = HEAD_DIM ** -0.5
LOG2E = 1.4426950408889634
Q_SCALE = SCALE * LOG2E

LANES = 128
SUBLANES = 8
VMEM_LIMIT_BYTES = 56 * 1024 * 1024

TM_PROJ = 256
TQ = 256
TK = 256
SEL_MASK_BIAS = 32768.0
V_ONES_ROWS = 16
TM_MOE = 256
TS_SCATTER = 512
TM_COMBINE = 128
DMA_ISSUE_UNROLL = 8
T_RANK = 512


def _cparams(n_axes, flags=None):
    return pltpu.CompilerParams(dimension_semantics=("arbitrary",) * n_axes,
                                vmem_limit_bytes=VMEM_LIMIT_BYTES, flags=flags)


def _dot(a, b):
    return jnp.dot(a, b, preferred_element_type=F32)


def _dot_nt(a, b):
    return lax.dot_general(a, b, (((1,), (1,)), ((), ())), preferred_element_type=F32)


U32 = jnp.uint32
_HI16 = 0xFFFF0000


def _pack_bf16_halves(x):
    c = x.shape[1] // 2
    bits = lambda t: lax.bitcast_convert_type(t.astype(BF16).astype(F32), U32)
    return lax.shift_right_logical(bits(x[:, :c]), U32(16)) | (bits(x[:, c:]) & U32(_HI16))


def _unpack_bf16_halves(w):
    lo = lax.bitcast_convert_type(lax.shift_left(w, U32(16)), F32)
    hi = lax.bitcast_convert_type(w & U32(_HI16), F32)
    return lo, hi


def _rope(t, cos, sin, axis):
    idx = lax.broadcasted_iota(I32, t.shape, axis)
    partner = jnp.where(idx < ROT_DIM // 2,
                        pltpu.roll(t, HEAD_DIM - ROT_DIM // 2, axis),
                        pltpu.roll(t, ROT_DIM // 2, axis))
    return t * cos + partner * sin


def _inproj_kernel(x_ref, ln_ref, trig_ref, trigt_ref, wqt_ref, wvt_ref, wgt_ref, wkc_ref, wk_ref, wu_ref,
                   qnt_ref, qrt_ref, kvc_ref, kk_ref, vt_ref, gt_ref, u_ref):
    x = x_ref[...]
    ms = jnp.mean(x * x, axis=-1, keepdims=True)
    hb = (x * lax.rsqrt(ms + EPS) * ln_ref[...]).astype(BF16)
    chunk = 4 * HEAD_DIM

    cost, sint = trigt_ref[0], trigt_ref[1]
    for ch in range(2):
        rt = _dot_nt(wqt_ref[ch * chunk:(ch + 1) * chunk, :], hb)
        for j in range(4):
            t = rt[j * HEAD_DIM:(j + 1) * HEAD_DIM, :]
            qnt_ref[0, ch * 4 + j] = (t * Q_SCALE).astype(BF16)
            qrt_ref[0, ch * 4 + j] = (_rope(t, cost, sint, 0) * Q_SCALE).astype(BF16)
    rt = _dot_nt(wvt_ref[...], hb)
    ones = jnp.ones((V_ONES_ROWS, rt.shape[1]), BF16)
    for j in range(4):
        vt_ref[0, j, 0] = jnp.concatenate([rt[j * HEAD_DIM:(j + 1) * HEAD_DIM, :].astype(BF16), ones], axis=0)
    gt_ref[...] = _dot_nt(wgt_ref[...], hb)

    r = _dot(hb, wkc_ref[...])
    for j in range(4):
        kvc_ref[0, j] = r[:, j * HEAD_DIM:(j + 1) * HEAD_DIM].astype(BF16)
    cos, sin = trig_ref[0], trig_ref[1]
    r = _dot(hb, wk_ref[...])
    for j in range(4):
        kk_ref[0, j] = _rope(r[:, j * HEAD_DIM:(j + 1) * HEAD_DIM], cos, sin, 1).astype(BF16)
    for ch in range(2):
        u_ref[:, ch * chunk:(ch + 1) * chunk] = _dot(hb, wu_ref[:, ch * chunk:(ch + 1) * chunk])


def _inproj(x2d, ln, trig, trigt, wqt, wvt, wgt, wkc, wk, wu, batch, seq):
    n, d = x2d.shape
    tm = TM_PROJ
    assert tm == TK
    spb = seq // tm
    const = lambda i: (0, 0)
    resident = lambda w: pl.BlockSpec(w.shape, const, pipeline_mode=pl.Buffered(1))
    row_heads = lambda i: (i // spb, 0, i % spb, 0)
    col_heads = lambda i: (i // spb, 0, 0, i % spb)
    return pl.pallas_call(
        _inproj_kernel,
        grid=(n // tm,),
        in_specs=[
            pl.BlockSpec((tm, d), lambda i: (i, 0)),
            pl.BlockSpec((1, d), const),
            pl.BlockSpec((2, tm, HEAD_DIM), lambda i: (0, i, 0)),
            pl.BlockSpec((2, HEAD_DIM, tm), lambda i: (0, 0, i)),
            resident(wqt), resident(wvt), resident(wgt), resident(wkc), resident(wk), resident(wu),
        ],
        out_specs=[
            pl.BlockSpec((1, N_HEADS, HEAD_DIM, tm), col_heads),
            pl.BlockSpec((1, N_HEADS, HEAD_DIM, tm), col_heads),
            pl.BlockSpec((1, 4, tm, HEAD_DIM), row_heads),
            pl.BlockSpec((1, 4, tm, HEAD_DIM), row_heads),
            pl.BlockSpec((1, 4, 1, HEAD_DIM + V_ONES_ROWS, tm), lambda i: (i // spb, 0, i % spb, 0, 0)),
            pl.BlockSpec((wgt.shape[0], tm), lambda i: (0, i)),
            pl.BlockSpec((tm, wu.shape[1]), lambda i: (i, 0)),
        ],
        out_shape=[
            jax.ShapeDtypeStruct((batch, N_HEADS, HEAD_DIM, seq), BF16),
            jax.ShapeDtypeStruct((batch, N_HEADS, HEAD_DIM, seq), BF16),
            jax.ShapeDtypeStruct((batch, 4, seq, HEAD_DIM), BF16),
            jax.ShapeDtypeStruct((batch, 4, seq, HEAD_DIM), BF16),
            jax.ShapeDtypeStruct((batch, 4, spb, HEAD_DIM + V_ONES_ROWS, tm), BF16),
            jax.ShapeDtypeStruct((wgt.shape[0], n), F32),
            jax.ShapeDtypeStruct((n, wu.shape[1]), F32),
        ],
        compiler_params=_cparams(1),
        name="inproj",
    )(x2d, ln, trig, trigt, wqt, wvt, wgt, wkc, wk, wu)


def _compress_kernel(t_ref, pe_ref, w1_ref, w2_ref, o_ref):
    t2 = t_ref[0, 0]
    w1 = w1_ref[0]
    a = _dot(t2, w1)
    pb = _dot(pe_ref[0], w1)
    bias = pb[0:1, :CMP_HIDDEN] + pb[1:2, CMP_HIDDEN:]
    nrow = a.shape[0]
    nxt = pltpu.roll(a[:, CMP_HIDDEN:], nrow - 1, 0)
    hid = jax.nn.gelu(a[:, :CMP_HIDDEN] + nxt + bias)
    out = _dot(hid.astype(BF16), w2_ref[0])
    row = lax.broadcasted_iota(I32, out.shape, 0)
    out = jnp.where(row < nrow - 1, out, 0.0)
    is_value = pl.program_id(1) >= N_KV
    o_ref[0, 0] = jnp.where(is_value, out.T, out).astype(BF16)


def _compress(kvc2, pe2, w1, w2):
    batch, four, nrow, width = kvc2.shape
    return pl.pallas_call(
        _compress_kernel,
        grid=(batch, four),
        in_specs=[
            pl.BlockSpec((1, 1, nrow, width), lambda b, j: (b, j, 0, 0)),
            pl.BlockSpec((1, SUBLANES, width), lambda b, j: (j // N_KV, 0, 0)),
            pl.BlockSpec((1, width, 2 * CMP_HIDDEN), lambda b, j: (j // N_KV, 0, 0)),
            pl.BlockSpec((1, CMP_HIDDEN, HEAD_DIM), lambda b, j: (j // N_KV, 0, 0)),
        ],
        out_specs=pl.BlockSpec((1, 1, nrow, HEAD_DIM), lambda b, j: (b, j, 0, 0)),
        out_shape=jax.ShapeDtypeStruct((batch, four, nrow, HEAD_DIM), BF16),
        compiler_params=_cparams(2),
        name="compress",
    )(kvc2, pe2, w1, w2)


def _nsa_kernel(qnt_ref, qrt_ref, cmp_ref, kk_ref, vt_ref, kblk_ref, gt_ref, gn_ref, o_ref,
                m_scr, acc_scr, qa_scr, ocmp_scr, s_scr, o_scr):
    qi = pl.program_id(1)
    tq = qnt_ref.shape[3]
    rows = GQA * tq
    n_cmp_pad = cmp_ref.shape[2]
    n_sel = kk_ref.shape[2] // SEL_LEN
    q0 = qi * tq

    m_scr[...] = jnp.full(m_scr.shape, NEG, F32)
    acc_scr[...] = jnp.zeros(acc_scr.shape, F32)

    def scores(item, slot):
        c, g, q_rows, k_tile, v_t, bias = item
        s_scr[slot] = _dot(k_tile, qa_scr[g, :q_rows, :])

    def softmax_pv(item, slot):
        c, g, q_rows, k_tile, v_t, bias = item
        s = s_scr[slot]
        if bias is not None:
            s = s + bias
        m_old = m_scr[c]
        m_new = jnp.maximum(m_old, jnp.max(s, axis=0, keepdims=True))
        alpha = jnp.exp2(m_old - m_new)
        p = jnp.exp2(s - m_new)
        acc_scr[c] = alpha * acc_scr[c] + _dot(v_t, p.astype(BF16))
        m_scr[c] = m_new

    def flash_out(c):
        return acc_scr[c, :HEAD_DIM, :] / acc_scr[c, HEAD_DIM:HEAD_DIM + 1, :]

    def tile_heads(a):
        return jnp.concatenate([a] * GQA, axis=1)

    kk_i = lax.broadcasted_iota(I32, (TK, tq), 0)
    qq_i = lax.broadcasted_iota(I32, (TK, tq), 1)
    causal_bias = tile_heads(jnp.where(kk_i <= qq_i, 0.0, NEG))
    band_bias = tile_heads(jnp.where(kk_i > qq_i, 0.0, NEG))
    n_win = WINDOW // TK

    for g in range(N_KV):
        qn4 = jnp.concatenate([qnt_ref[0, h] for h in range(g * GQA, (g + 1) * GQA)], axis=1)
        s = _dot(cmp_ref[0, g], qn4)
        n_idx = lax.broadcasted_iota(I32, (n_cmp_pad, rows), 0)
        t_idx = q0 + jnp.bitwise_and(lax.broadcasted_iota(I32, (n_cmp_pad, rows), 1), tq - 1)
        cmp_ok = n_idx * CMP_STRIDE + (CMP_LEN - 1) <= t_idx
        s = jnp.where(cmp_ok, s, NEG)
        m = jnp.max(s, axis=0, keepdims=True)
        e = jnp.where(cmp_ok, jnp.exp2(s - m), 0.0)
        l = jnp.sum(e, axis=0, keepdims=True)
        p = e * jnp.where(l > 0.0, 1.0 / l, 0.0)
        pb = p.astype(BF16)
        ocmp_scr[g] = _dot(cmp_ref[0, N_KV + g], pb)

        jn = lax.broadcasted_iota(I32, (n_sel, n_cmp_pad), 0) * SEL_LEN
        cn = lax.broadcasted_iota(I32, (n_sel, n_cmp_pad), 1) * CMP_STRIDE
        ov = jnp.clip(jnp.minimum(cn + CMP_LEN, jn + SEL_LEN) - jnp.maximum(cn, jn), 0, None)
        w_t = (ov.astype(F32) * (1.0 / CMP_LEN)).astype(BF16)
        imp4 = _dot(w_t, pb)
        imp = imp4[:, 0:tq]
        for r in range(1, GQA):
            imp = imp + imp4[:, r * tq:(r + 1) * tq]
        j_blk = lax.broadcasted_iota(I32, (n_sel, tq), 0)
        t_q = q0 + lax.broadcasted_iota(I32, (n_sel, tq), 1)
        sel_ok = j_blk * SEL_LEN <= t_q
        back = t_q // SEL_LEN - j_blk
        forced = (j_blk == 0) | ((back >= 0) & (back < N_LOCAL))
        val = jnp.where(sel_ok & forced, BIG, jnp.where(sel_ok, imp, -BIG))
        rank = jnp.zeros((n_sel, tq), F32)
        for i in range(n_sel):
            vi = val[i:i + 1, :]
            beats = (vi > val) | ((vi == val) & (j_blk > i))
            rank = rank + beats.astype(F32)
        keep = (rank < float(min(SEL_TOPK, n_sel))) & sel_ok
        bias_t = jnp.where(keep, 0.0, -SEL_MASK_BIAS)
        bias_t = jnp.concatenate([bias_t, jnp.zeros((HEAD_DIM - n_sel, tq), F32)], axis=0)
        qa_scr[g, :HEAD_DIM, :] = jnp.concatenate(
            [qrt_ref[0, h] for h in range(g * GQA, (g + 1) * GQA)], axis=1)
        qa_scr[g, HEAD_DIM:, :] = tile_heads(bias_t.astype(BF16))

    def sel_item(g, kj, bias):
        st = pl.multiple_of(kj * TK, TK)
        k_aug = jnp.concatenate([kk_ref[0, g, pl.ds(st, TK), :], kblk_ref[pl.ds(st, TK), :]], axis=1)
        return (g, g, 2 * HEAD_DIM, k_aug, vt_ref[0, g, kj], bias)

    def win_item(g, back):
        kj = jnp.maximum(qi - back, 0)
        st = pl.multiple_of(kj * TK, TK)
        off = jnp.where(qi >= back, 0.0, NEG)
        bias = causal_bias if back == 0 else (band_bias + off if back == n_win else off)
        return (N_KV + g, g, HEAD_DIM, kk_ref[0, N_KV + g, pl.ds(st, TK), :], vt_ref[0, N_KV + g, kj], bias)

    assert N_KV == 2
    scores(sel_item(0, 0, None), 0)

    def sel_body(kj, carry):
        scores(sel_item(1, kj, None), 1)
        softmax_pv(sel_item(0, kj, None), 0)
        scores(sel_item(0, kj + 1, None), 0)
        softmax_pv(sel_item(1, kj, None), 1)
        return carry

    lax.fori_loop(0, qi, sel_body, 0)
    tail = ([sel_item(g, qi, causal_bias) for g in range(N_KV)]
            + [win_item(g, back) for back in range(n_win, -1, -1) for g in range(N_KV)])
    for i, item in enumerate(tail):
        if i + 1 < len(tail):
            scores(tail[i + 1], (i + 1) % 2)
        softmax_pv(item, i % 2)

    gsig = jax.nn.sigmoid(gt_ref[...])
    for g in range(N_KV):
        o_cmp, o_sel, o_win = ocmp_scr[g], flash_out(g), flash_out(N_KV + g)
        for r in range(GQA):
            h = g * GQA + r
            sl = slice(r * tq, (r + 1) * tq)
            o_h = (gsig[3 * h:3 * h + 1, :] * o_cmp[:, sl]
                   + gsig[3 * h + 1:3 * h + 2, :] * o_sel[:, sl]
                   + gsig[3 * h + 2:3 * h + 3, :] * o_win[:, sl])
            o_scr[:, h * HEAD_DIM:(h + 1) * HEAD_DIM] = o_h.T

    o = o_scr[...]
    ms = jnp.mean(o * o, axis=-1, keepdims=True)
    o_ref[...] = (o * lax.rsqrt(ms + EPS) * gn_ref[...]).astype(BF16)


def _nsa(qnt, qrt, cmp, kk, vt, kblk, gt, gn):
    batch, _, _, seq = qnt.shape
    assert TQ == TK and WINDOW % TK == 0 and TQ & (TQ - 1) == 0
    nq = seq // TQ
    width = N_HEADS * HEAD_DIM
    return pl.pallas_call(
        _nsa_kernel,
        grid=(batch, nq),
        in_specs=[
            pl.BlockSpec((1, N_HEADS, HEAD_DIM, TQ), lambda b, i: (b, 0, 0, i)),
            pl.BlockSpec((1, N_HEADS, HEAD_DIM, TQ), lambda b, i: (b, 0, 0, i)),
            pl.BlockSpec((1,) + cmp.shape[1:], lambda b, i: (b, 0, 0, 0)),
            pl.BlockSpec((1,) + kk.shape[1:], lambda b, i: (b, 0, 0, 0)),
            pl.BlockSpec((1,) + vt.shape[1:], lambda b, i: (b, 0, 0, 0, 0)),
            pl.BlockSpec((seq, LANES), lambda b, i: (0, 0)),
            pl.BlockSpec((gt.shape[0], TQ), lambda b, i: (0, b * nq + i)),
            pl.BlockSpec((1, width), lambda b, i: (0, 0)),
        ],
        out_specs=pl.BlockSpec((TQ, width), lambda b, i: (b * nq + i, 0)),
        out_shape=jax.ShapeDtypeStruct((batch * seq, width), BF16),
        scratch_shapes=[
            pltpu.VMEM((2 * N_KV, 1, GQA * TQ), F32),
            pltpu.VMEM((2 * N_KV, vt.shape[3], GQA * TQ), F32),
            pltpu.VMEM((N_KV, 2 * HEAD_DIM, GQA * TQ), BF16),
            pltpu.VMEM((N_KV, HEAD_DIM, GQA * TQ), F32),
            pltpu.VMEM((2, TK, GQA * TQ), F32),
            pltpu.VMEM((TQ, width), F32),
        ],
        compiler_params=_cparams(2),
        name="nsa",
    )(qnt, qrt, cmp, kk, vt, kblk, gt, gn)


def _pool_kernel(u_ref, up_ref, w_ref, b_ref, sc_ref, gn_ref, o_ref, y_scr):
    si = pl.program_id(1)
    tm = u_ref.shape[1]
    halo = max(POOL_SIZES)
    cur = u_ref[0]
    prev = jnp.where(si > 0, up_ref[0, tm - halo:, :], 0.0)
    ext = jnp.concatenate([prev, cur], axis=0)
    t1 = (si * tm + 1 + lax.broadcasted_iota(I32, (tm, 1), 0)).astype(F32)
    cg = cur.shape[1] // len(POOL_SIZES)
    for gi, w in enumerate(POOL_SIZES):
        sl = slice(gi * cg, (gi + 1) * cg)
        acc = ext[:, sl]
        span = 1
        while span < w:
            acc = acc + jnp.concatenate([jnp.zeros((span, cg), F32), acc[:-span]], axis=0)
            span *= 2
        mean = acc[halo:] / jnp.minimum(t1, float(w))
        d = (mean - cur[:, sl]).astype(BF16)
        y = _dot(d, w_ref[gi]) + b_ref[:, sl]
        y_scr[:, sl] = y * sc_ref[:, sl]
    y = y_scr[...]
    ms = jnp.mean(y * y, axis=-1, keepdims=True)
    o_ref[...] = (y * lax.rsqrt(ms + EPS) * gn_ref[...]).astype(BF16)


def _pool(u3, w_pool, b_pool, pool_scale, gn_pool):
    batch, seq, c = u3.shape
    tm = TM_PROJ
    ns = seq // tm
    vec = lambda b, i: (0, 0)
    return pl.pallas_call(
        _pool_kernel,
        grid=(batch, ns),
        in_specs=[
            pl.BlockSpec((1, tm, c), lambda b, i: (b, i, 0)),
            pl.BlockSpec((1, tm, c), lambda b, i: (b, jnp.maximum(i - 1, 0), 0)),
            pl.BlockSpec(w_pool.shape, lambda b, i: (0, 0, 0)),
            pl.BlockSpec((1, c), vec),
            pl.BlockSpec((1, c), vec),
            pl.BlockSpec((1, c), vec),
        ],
        out_specs=pl.BlockSpec((tm, c), lambda b, i: (b * ns + i, 0)),
        out_shape=jax.ShapeDtypeStruct((batch * seq, c), BF16),
        scratch_shapes=[pltpu.VMEM((tm, c), F32)],
        compiler_params=_cparams(2),
        name="pool",
    )(u3, u3, w_pool, b_pool, pool_scale, gn_pool)


def _outproj_kernel(mn_ref, mp_ref, x_ref, w_ref, ln_ref, wr_ref, br_ref,
                    x1_ref, h2_ref, eid_ref, gate_ref):
    tm, d = x_ref.shape
    half = mn_ref.shape[1]
    chunk = 512
    mn = mn_ref[...]
    mp = mp_ref[...]
    ssq = jnp.zeros((tm, 1), F32)
    for c in range(d // chunk):
        sl = slice(c * chunk, (c + 1) * chunk)
        x1 = x_ref[:, sl] + (_dot(mn, w_ref[:half, sl]) + _dot(mp, w_ref[half:, sl]))
        x1_ref[:, sl] = x1
        ssq = ssq + jnp.sum(x1 * x1, axis=-1, keepdims=True)
    h2 = x1_ref[...] * lax.rsqrt(ssq * (1.0 / d) + EPS) * ln_ref[...]
    h2_ref[...] = _pack_bf16_halves(h2)

    logits = _dot_nt(wr_ref[...], h2.astype(BF16)) + br_ref[:, 0:1]
    ng, ne = N_EXPERT_GROUPS, EXPERTS_PER_GROUP
    row = lax.broadcasted_iota(I32, (ne, tm), 0)
    lg = jnp.where(row < ng, logits[0:ne], NEG)
    mg = jnp.max(lg, axis=0, keepdims=True)
    g_sel = jnp.min(jnp.where(lg == mg, row, ne), axis=0, keepdims=True)
    p_g = 1.0 / jnp.sum(jnp.exp(lg - mg), axis=0, keepdims=True)
    le = jnp.zeros((ne, tm), F32)
    for g in range(ng):
        le = jnp.where(g_sel == g, logits[ne * (g + 1):ne * (g + 2)], le)
    ex = jnp.exp(le - jnp.max(le, axis=0, keepdims=True))
    pe = ex / jnp.sum(ex, axis=0, keepdims=True)
    p1 = jnp.max(pe, axis=0, keepdims=True)
    i1 = jnp.min(jnp.where(pe == p1, row, ne), axis=0, keepdims=True)
    rest = jnp.where(row == i1, -1.0, pe)
    p2 = jnp.max(rest, axis=0, keepdims=True)
    i2 = jnp.min(jnp.where(rest == p2, row, ne), axis=0, keepdims=True)
    psum = p1 + p2
    zero_i = jnp.zeros((ne, tm), I32)
    eid = jnp.where(row == 0, g_sel * ne + i1, jnp.where(row == 1, g_sel * ne + i2, zero_i))
    gate = jnp.where(row == 0, p_g * p1 / psum, jnp.where(row == 1, p_g * p2 / psum, 0.0))
    eid_ref[...] = eid
    gate_ref[...] = gate


def _outproj(mix_nsa, mix_pool, x2d, w_out, ln_moe, w_router_t, b_router):
    n, d = x2d.shape
    tm = TM_PROJ
    half = mix_nsa.shape[1]
    const = lambda i: (0, 0)
    return pl.pallas_call(
        _outproj_kernel,
        grid=(n // tm,),
        in_specs=[
            pl.BlockSpec((tm, half), lambda i: (i, 0)),
            pl.BlockSpec((tm, half), lambda i: (i, 0)),
            pl.BlockSpec((tm, d), lambda i: (i, 0)),
            pl.BlockSpec(w_out.shape, const, pipeline_mode=pl.Buffered(1)),
            pl.BlockSpec((1, d), const),
            pl.BlockSpec(w_router_t.shape, const),
            pl.BlockSpec(b_router.shape, const),
        ],
        out_specs=[
            pl.BlockSpec((tm, d), lambda i: (i, 0)),
            pl.BlockSpec((tm, d // 2), lambda i: (i, 0)),
            pl.BlockSpec((SUBLANES, tm), lambda i: (0, i)),
            pl.BlockSpec((SUBLANES, tm), lambda i: (0, i)),
        ],
        out_shape=[
            jax.ShapeDtypeStruct((n, d), F32),
            jax.ShapeDtypeStruct((n, d // 2), U32),
            jax.ShapeDtypeStruct((SUBLANES, n), I32),
            jax.ShapeDtypeStruct((SUBLANES, n), F32),
        ],
        compiler_params=_cparams(1),
        name="outproj",
    )(mix_nsa, mix_pool, x2d, w_out, ln_moe, w_router_t, b_router)


def _dispatch_kernel(eid_ref, dest_ref, meta_ref, rank_scr):
    n = eid_ref.shape[1]
    t = T_RANK
    row = lax.broadcasted_iota(I32, (N_EXPERTS, t), 0)
    before = (lax.broadcasted_iota(I32, (t, t), 0) < lax.broadcasted_iota(I32, (t, t), 1)).astype(BF16)
    carry = jnp.zeros((N_EXPERTS, 1), F32)
    for k in range(2):
        for j in range(n // t):
            sl = slice(j * t, (j + 1) * t)
            oh = row == eid_ref[k:k + 1, sl]
            ohf = oh.astype(F32)
            prior = _dot(ohf.astype(BF16), before) + carry
            rank_scr[k:k + 1, sl] = jnp.sum(jnp.where(oh, prior, 0.0), axis=0, keepdims=True)
            carry = carry + jnp.sum(ohf, axis=1, keepdims=True)
    counts = jnp.broadcast_to(carry, (N_EXPERTS, LANES))
    padded = jnp.floor((counts + (TM_MOE - 1.0)) * (1.0 / TM_MOE)) * TM_MOE
    ends = padded
    r_idx = lax.broadcasted_iota(I32, (N_EXPERTS, LANES), 0)
    sh = 1
    while sh < N_EXPERTS:
        ends = ends + jnp.where(r_idx >= sh, pltpu.roll(ends, sh, 0), 0.0)
        sh *= 2
    starts = (ends - padded)[:, 0:1]
    for k in range(2):
        for j in range(n // t):
            sl = slice(j * t, (j + 1) * t)
            oh = row == eid_ref[k:k + 1, sl]
            base = jnp.sum(jnp.where(oh, starts, 0.0), axis=0, keepdims=True)
            dest_ref[k:k + 1, sl] = (rank_scr[k:k + 1, sl] + base).astype(I32)
    for k in range(2, SUBLANES):
        dest_ref[k:k + 1, :] = jnp.zeros((1, n), I32)
    blk_start = lax.broadcasted_iota(I32, (N_EXPERTS, LANES), 1).astype(F32) * TM_MOE
    blk_expert = jnp.sum((ends <= blk_start).astype(F32), axis=0, keepdims=True)
    blk_expert = jnp.minimum(blk_expert, N_EXPERTS - 1.0)
    n_used = ends[N_EXPERTS - 1:N_EXPERTS, :] * (1.0 / TM_MOE)
    r8 = lax.broadcasted_iota(I32, (SUBLANES, LANES), 0)
    meta = jnp.where(r8 == 0, blk_expert, jnp.where(r8 == 1, n_used, 0.0))
    meta_ref[...] = meta.astype(I32)


def _dispatch(eid):
    n = eid.shape[1]
    return pl.pallas_call(
        _dispatch_kernel,
        out_shape=[jax.ShapeDtypeStruct((SUBLANES, n), I32),
                   jax.ShapeDtypeStruct((SUBLANES, LANES), I32)],
        scratch_shapes=[pltpu.VMEM((SUBLANES, n), F32)],
        compiler_params=pltpu.CompilerParams(vmem_limit_bytes=VMEM_LIMIT_BYTES),
        name="dispatch",
    )(eid)


def _row_copy(src_ref, src_row, dst_ref, dst_row, sem):
    return pltpu.make_async_copy(src_ref.at[pl.ds(src_row, 1)], dst_ref.at[pl.ds(dst_row, 1)], sem)


def _scatter_kernel(dest_ref, h_ref, init_ref, xs_ref, sem):
    del init_ref
    tm = h_ref.shape[0]

    def start(r, c):
        for k in range(2):
            _row_copy(h_ref, r, xs_ref, dest_ref[0, k, r], sem).start()
        return c

    lax.fori_loop(0, tm, start, 0, unroll=DMA_ISSUE_UNROLL)
    for k in range(2):
        pltpu.make_async_copy(h_ref, xs_ref.at[pl.ds(0, tm)], sem).wait()


def _scatter(dest3, h2, xs_init):
    nsteps, _, tm = dest3.shape
    d = h2.shape[1]
    return pl.pallas_call(
        _scatter_kernel,
        grid=(nsteps,),
        in_specs=[
            pl.BlockSpec((1, 2, tm), lambda i: (i, 0, 0), memory_space=pltpu.SMEM),
            pl.BlockSpec((tm, d), lambda i: (i, 0)),
            pl.BlockSpec(memory_space=pl.ANY),
        ],
        out_specs=pl.BlockSpec(memory_space=pl.ANY),
        out_shape=jax.ShapeDtypeStruct(xs_init.shape, xs_init.dtype),
        scratch_shapes=[pltpu.SemaphoreType.DMA(())],
        input_output_aliases={2: 0},
        compiler_params=_cparams(1),
        name="scatter",
    )(dest3, h2, xs_init)


def _experts_kernel(meta_ref, xs_ref, wg_hbm, wu_hbm, wd_hbm, ys_ref,
                    wg_buf, wu_buf, wd_buf, wg16, wu16, wd16, ord_ref, sem):
    c = pl.program_id(0)
    n_used = meta_ref[1, 0]
    e = meta_ref[0, c]
    first = (c == 0) | (e != meta_ref[0, jnp.maximum(c - 1, 0)])

    def fetch(expert, slot):
        return [pltpu.make_async_copy(w.at[expert], buf.at[slot], sem.at[slot, i])
                for i, (w, buf) in enumerate(((wg_hbm, wg_buf), (wu_hbm, wu_buf), (wd_hbm, wd_buf)))]

    @pl.when(c == 0)
    def _():
        ord_ref[0] = 0
        for cp in fetch(e, 0):
            cp.start()

    @pl.when(first & (c < n_used))
    def _():
        @pl.when(c > 0)
        def _():
            ord_ref[0] = ord_ref[0] + 1
        slot = ord_ref[0] % 2
        for cp in fetch(e, slot):
            cp.wait()
        nxt = lax.while_loop(lambda j: (j < n_used) & (meta_ref[0, jnp.minimum(j, n_used - 1)] == e),
                             lambda j: j + 1, c + 1)

        @pl.when(nxt < n_used)
        def _():
            for cp in fetch(meta_ref[0, jnp.minimum(nxt, n_used - 1)], 1 - slot):
                cp.start()

        wg16[...] = wg_buf[slot].astype(BF16)
        wu16[...] = wu_buf[slot].astype(BF16)
        wd16[...] = wd_buf[slot].astype(BF16)

    @pl.when(c < n_used)
    def _():
        lo, hi = _unpack_bf16_halves(xs_ref[...])
        lo, hi = lo.astype(BF16), hi.astype(BF16)
        half = lo.shape[1]

        def proj(w_ref):
            return _dot(lo, w_ref[:half, :]) + _dot(hi, w_ref[half:, :])

        act = (jax.nn.silu(proj(wg16)) * proj(wu16)).astype(BF16)
        ys_ref[...] = _pack_bf16_halves(_dot(act, wd16[...]))

    @pl.when(c >= n_used)
    def _():
        ys_ref[...] = jnp.zeros(ys_ref.shape, U32)


def _experts(meta, xs, w_gate, w_up, w_down):
    cap, dw = xs.shape
    nblk = cap // TM_MOE
    d, ff = w_gate.shape[1:]

    def row_map(c, meta):
        return (jnp.minimum(c, meta[1, 0] - 1), 0)

    hbm = pl.BlockSpec(memory_space=pl.ANY)
    return pl.pallas_call(
        _experts_kernel,
        grid_spec=pltpu.PrefetchScalarGridSpec(
            num_scalar_prefetch=1,
            grid=(nblk,),
            in_specs=[pl.BlockSpec((TM_MOE, dw), row_map), hbm, hbm, hbm],
            out_specs=pl.BlockSpec((TM_MOE, dw), lambda c, meta: (c, 0)),
            scratch_shapes=[
                pltpu.VMEM((2, d, ff), F32), pltpu.VMEM((2, d, ff), F32), pltpu.VMEM((2, ff, d), F32),
                pltpu.VMEM((d, ff), BF16), pltpu.VMEM((d, ff), BF16), pltpu.VMEM((ff, d), BF16),
                pltpu.SMEM((1,), I32),
                pltpu.SemaphoreType.DMA((2, 3)),
            ],
        ),
        out_shape=jax.ShapeDtypeStruct((cap, dw), U32),
        compiler_params=_cparams(1),
        name="experts",
    )(meta, xs, w_gate, w_up, w_down)


def _combine_kernel(dest_ref, dest_next_ref, x1_ref, gate_ref, ln_ref, ys_ref, o_ref, ybuf, sem):
    i = pl.program_id(0)
    tm = x1_ref.shape[0]
    slot = i % 2

    def issue(d_ref, s):
        def start(r, c):
            for k in range(2):
                _row_copy(ys_ref, d_ref[0, k, r], ybuf.at[s, k], r, sem.at[s]).start()
            return c

        lax.fori_loop(0, tm, start, 0, unroll=DMA_ISSUE_UNROLL)

    @pl.when(i == 0)
    def _():
        issue(dest_ref, 0)

    @pl.when(i + 1 < pl.num_programs(0))
    def _():
        issue(dest_next_ref, 1 - slot)

    for k in range(2):
        pltpu.make_async_copy(ys_ref.at[pl.ds(0, tm)], ybuf.at[slot, k], sem.at[slot]).wait()
    gate = gate_ref[...]
    lo0, hi0 = _unpack_bf16_halves(ybuf[slot, 0])
    lo1, hi1 = _unpack_bf16_halves(ybuf[slot, 1])
    g0, g1 = gate[:, 0:1], gate[:, 1:2]
    moe = jnp.concatenate([lo0 * g0 + lo1 * g1, hi0 * g0 + hi1 * g1], axis=1)
    x2 = x1_ref[...] + moe
    ms = jnp.mean(x2 * x2, axis=-1, keepdims=True)
    o_ref[...] = x2 * lax.rsqrt(ms + EPS) * ln_ref[...]


def _combine(dest3, x1, gate_rows, ln_final, ys):
    n, d = x1.shape
    tm = TM_COMBINE
    last = n // tm - 1
    return pl.pallas_call(
        _combine_kernel,
        grid=(n // tm,),
        in_specs=[
            pl.BlockSpec((1, 2, tm), lambda i: (i, 0, 0), memory_space=pltpu.SMEM),
            pl.BlockSpec((1, 2, tm), lambda i: (jnp.minimum(i + 1, last), 0, 0), memory_space=pltpu.SMEM),
            pl.BlockSpec((tm, d), lambda i: (i, 0)),
            pl.BlockSpec((tm, LANES), lambda i: (i, 0)),
            pl.BlockSpec((1, d), lambda i: (0, 0)),
            pl.BlockSpec(memory_space=pl.ANY),
        ],
        out_specs=pl.BlockSpec((tm, d), lambda i: (i, 0)),
        out_shape=jax.ShapeDtypeStruct((n, d), F32),
        scratch_shapes=[pltpu.VMEM((2, 2, tm, d // 2), U32), pltpu.SemaphoreType.DMA((2,))],
        compiler_params=_cparams(1),
        name="combine",
    )(dest3, dest3, x1, gate_rows, ln_final, ys)


def kernel(x, positions, ln_mix, w_in, pe_cmp_k, w_cmp_k1, w_cmp_k2, pe_cmp_v, w_cmp_v1, w_cmp_v2, w_pool, b_pool, pool_scale, gn_nsa, gn_pool, w_out, ln_moe, w_router_group, b_router_group, w_router_expert, b_router_expert, w_gate, w_up, w_down, ln_final):
    batch, seq, d = x.shape
    n = batch * seq
    depth = w_in.shape[0]
    nsa_w = N_HEADS * HEAD_DIM
    kv_w = N_KV * HEAD_DIM
    cut_gates = nsa_w + 6 * kv_w
    cut_pool = cut_gates + N_GATES

    inv_freq = ROPE_THETA ** (-jnp.arange(0, ROT_DIM, 2, dtype=F32) / ROT_DIM)
    ang = positions.astype(F32).reshape(n, 1) * inv_freq[None, :]
    ang = jnp.concatenate([-ang, ang, jnp.zeros((n, HEAD_DIM - ROT_DIM), F32)], axis=1)
    trig = jnp.stack([jnp.cos(ang), jnp.sin(ang)])
    trigt = jnp.transpose(trig, (0, 2, 1))
    kblk = (jnp.arange(seq, dtype=I32)[:, None] // SEL_LEN == jnp.arange(LANES, dtype=I32)[None, :]).astype(BF16)

    x2d = x.reshape(n, d)
    assert depth == 1, "single-layer operation"
    for l in range(depth):
        wi = w_in[l]
        o_kc, o_ks = nsa_w, nsa_w + 2 * kv_w
        o_vs, o_kw, o_vw = o_ks + kv_w, o_ks + 2 * kv_w, o_ks + 3 * kv_w
        gate_rows_pad = -(-N_GATES // SUBLANES) * SUBLANES
        wqt = wi[:, :nsa_w].T.astype(BF16)
        wvt = jnp.concatenate([wi[:, o_vs:o_kw], wi[:, o_vw:cut_gates]], axis=1).T.astype(BF16)
        wgt = jnp.pad(wi[:, cut_gates:cut_pool], ((0, 0), (0, gate_rows_pad - N_GATES))).T.astype(BF16)
        wkc = wi[:, o_kc:o_ks].astype(BF16)
        wk = jnp.concatenate([wi[:, o_ks:o_vs], wi[:, o_kw:o_vw]], axis=1).astype(BF16)
        wu = wi[:, cut_pool:].astype(BF16)
        qnt, qrt, kvc, kk, vt, gt, u = _inproj(x2d, ln_mix[l][None], trig, trigt, wqt, wvt, wgt, wkc, wk, wu,
                                               batch, seq)

        half = CMP_LEN * HEAD_DIM // 2
        def stack_w1(w):
            return jnp.concatenate([w[:half], w[half:]], axis=1)
        w1 = jnp.stack([stack_w1(w_cmp_k1[l]), stack_w1(w_cmp_v1[l])]).astype(BF16)
        w2 = jnp.stack([w_cmp_k2[l], w_cmp_v2[l]]).astype(BF16)
        def pe_rows(pe):
            return jnp.pad(pe.reshape(2, half), ((0, SUBLANES - 2), (0, 0)))
        pe2 = jnp.stack([pe_rows(pe_cmp_k[l]), pe_rows(pe_cmp_v[l])]).astype(BF16)
        kvc2 = kvc.reshape(batch, 4, seq // CMP_STRIDE, CMP_STRIDE * HEAD_DIM)
        cmp = _compress(kvc2, pe2, w1, w2)

        mix_nsa = _nsa(qnt, qrt, cmp, kk, vt, kblk, gt, gn_nsa[l][None])
        mix_pool = _pool(u.reshape(batch, seq, -1), w_pool[l].astype(BF16), b_pool[l][None],
                         pool_scale[l][None], gn_pool[l][None])

        ng, ne = N_EXPERT_GROUPS, EXPERTS_PER_GROUP
        w_r = jnp.concatenate([
            w_router_group[l].T, jnp.zeros((ne - ng, d), F32),
            jnp.transpose(w_router_expert[l], (0, 2, 1)).reshape(ng * ne, d),
            jnp.zeros((LANES - ne - ng * ne, d), F32)], axis=0).astype(BF16)
        b_r = jnp.concatenate([
            b_router_group[l], jnp.zeros((ne - ng,), F32), b_router_expert[l].reshape(-1),
            jnp.zeros((LANES - ne - ng * ne,), F32)])
        b_r = jnp.broadcast_to(b_r[:, None], (LANES, LANES))
        x1, h2, eid, gate = _outproj(mix_nsa, mix_pool, x2d, w_out[l].astype(BF16), ln_moe[l][None], w_r, b_r)

        dest, meta = _dispatch(eid)
        cap = (2 * n // TM_MOE + N_EXPERTS) * TM_MOE
        def per_tile(tm):
            return jnp.transpose(dest[:2].reshape(2, n // tm, tm), (1, 0, 2))
        xs = _scatter(per_tile(TS_SCATTER), h2, jnp.zeros((cap, d // 2), U32))
        ys = _experts(meta, xs, w_gate[l], w_up[l], w_down[l])
        dest_tok = per_tile(TM_COMBINE)
        gate_rows = jnp.pad(gate[:2].T, ((0, 0), (0, LANES - 2)))
        x2d = _combine(dest_tok, x1, gate_rows, ln_final[None], ys)
    return x2d.reshape(batch, seq, d)
```

```python
import functools

import jax
import jax.numpy as jnp
from jax import lax
from jax.experimental import pallas as pl
from jax.experimental.pallas import tpu as pltpu

F32 = jnp.float32
BF16 = jnp.bfloat16
I32 = jnp.int32

HEAD_DIM = 128
N_HEADS = 8
N_KV = 2
GQA = N_HEADS // N_KV
N_GATES = 3 * N_HEADS
ROT_DIM = HEAD_DIM // 4
ROPE_THETA = 500000.0
CMP_LEN = 32
CMP_STRIDE = 16
CMP_HIDDEN = 2 * HEAD_DIM
SEL_LEN = 64
SEL_TOPK = 16
N_LOCAL = 2
WINDOW = 512
POOL_SIZES = (2, 4, 8, 16)
N_EXPERT_GROUPS = 4
EXPERTS_PER_GROUP = 8
N_EXPERTS = N_EXPERT_GROUPS * EXPERTS_PER_GROUP
EPS = 1e-6
NEG = -1e30
BIG = 1e30
SCALE = HEAD_DIM ** -0.5
LOG2E = 1.4426950408889634
Q_SCALE = SCALE * LOG2E

LANES = 128
SUBLANES = 8
VMEM_LIMIT_BYTES = 56 * 1024 * 1024

TM_PROJ = 256
TQ = 256
TK = 256
SEL_MASK_BIAS = 32768.0
V_ONES_ROWS = 16
TM_MOE = 256
TS_SCATTER = 512
TM_COMBINE = 128
DMA_ISSUE_UNROLL = 8
T_RANK = 512


def _cparams(n_axes, flags=None):
    return pltpu.CompilerParams(dimension_semantics=("arbitrary",) * n_axes,
                                vmem_limit_bytes=VMEM_LIMIT_BYTES, flags=flags)


def _dot(a, b):
    return jnp.dot(a, b, preferred_element_type=F32)


def _dot_nt(a, b):
    return lax.dot_general(a, b, (((1,), (1,)), ((), ())), preferred_element_type=F32)


U32 = jnp.uint32
_HI16 = 0xFFFF0000


def _pack_bf16_halves(x):
    c = x.shape[1] // 2
    bits = lambda t: lax.bitcast_convert_type(t.astype(BF16).astype(F32), U32)
    return lax.shift_right_logical(bits(x[:, :c]), U32(16)) | (bits(x[:, c:]) & U32(_HI16))


def _unpack_bf16_halves(w):
    lo = lax.bitcast_convert_type(lax.shift_left(w, U32(16)), F32)
    hi = lax.bitcast_convert_type(w & U32(_HI16), F32)
    return lo, hi


def _rope(t, cos, sin, axis):
    idx = lax.broadcasted_iota(I32, t.shape, axis)
    partner = jnp.where(idx < ROT_DIM // 2,
                        pltpu.roll(t, HEAD_DIM - ROT_DIM // 2, axis),
                        pltpu.roll(t, ROT_DIM // 2, axis))
    return t * cos + partner * sin


def _inproj_kernel(x_ref, ln_ref, trig_ref, trigt_ref, wqt_ref, wvt_ref, wgt_ref, wkc_ref, wk_ref, wu_ref,
                   qnt_ref, qrt_ref, kvc_ref, kk_ref, vt_ref, gt_ref, u_ref, kc_scr):
    x = x_ref[...]
    ms = jnp.mean(x * x, axis=-1, keepdims=True)
    hb = (x * lax.rsqrt(ms + EPS) * ln_ref[...]).astype(BF16)
    chunk = 4 * HEAD_DIM

    cost, sint = trigt_ref[0], trigt_ref[1]
    for ch in range(2):
        rt = _dot_nt(wqt_ref[ch * chunk:(ch + 1) * chunk, :], hb)
        for j in range(4):
            t = rt[j * HEAD_DIM:(j + 1) * HEAD_DIM, :]
            qnt_ref[0, ch * 4 + j] = (t * Q_SCALE).astype(BF16)
            qrt_ref[0, ch * 4 + j] = (_rope(t, cost, sint, 0) * Q_SCALE).astype(BF16)
    rt = _dot_nt(wvt_ref[...], hb)
    ones = jnp.ones((V_ONES_ROWS, rt.shape[1]), BF16)
    for j in range(4):
        vt_ref[0, j, 0] = jnp.concatenate([rt[j * HEAD_DIM:(j + 1) * HEAD_DIM, :].astype(BF16), ones], axis=0)
    gt_ref[...] = _dot_nt(wgt_ref[...], hb)

    r = _dot(hb, wkc_ref[...])
    groups = kvc_ref.shape[2]
    for j in range(4):
        kc_scr[j] = r[:, j * HEAD_DIM:(j + 1) * HEAD_DIM]
        for i in range(CMP_STRIDE):
            kvc_ref[0, j, :, i * HEAD_DIM:(i + 1) * HEAD_DIM] = (
                kc_scr[j, pl.ds(i, groups, stride=CMP_STRIDE), :].astype(BF16))
    cos, sin = trig_ref[0], trig_ref[1]
    r = _dot(hb, wk_ref[...])
    for j in range(4):
        kk_ref[0, j] = _rope(r[:, j * HEAD_DIM:(j + 1) * HEAD_DIM], cos, sin, 1).astype(BF16)
    for ch in range(2):
        u_ref[:, ch * chunk:(ch + 1) * chunk] = _dot(hb, wu_ref[:, ch * chunk:(ch + 1) * chunk])


def _inproj(x2d, ln, trig, trigt, wqt, wvt, wgt, wkc, wk, wu, batch, seq):
    n, d = x2d.shape
    tm = TM_PROJ
    assert tm == TK
    spb = seq // tm
    const = lambda i: (0, 0)
    resident = lambda w: pl.BlockSpec(w.shape, const, pipeline_mode=pl.Buffered(1))
    row_heads = lambda i: (i // spb, 0, i % spb, 0)
    col_heads = lambda i: (i // spb, 0, 0, i % spb)
    return pl.pallas_call(
        _inproj_kernel,
        grid=(n // tm,),
        in_specs=[
            pl.BlockSpec((tm, d), lambda i: (i, 0)),
            pl.BlockSpec((1, d), const),
            pl.BlockSpec((2, tm, HEAD_DIM), lambda i: (0, i, 0)),
            pl.BlockSpec((2, HEAD_DIM, tm), lambda i: (0, 0, i)),
            resident(wqt), resident(wvt), resident(wgt), resident(wkc), resident(wk), resident(wu),
        ],
        out_specs=[
            pl.BlockSpec((1, N_HEADS, HEAD_DIM, tm), col_heads),
            pl.BlockSpec((1, N_HEADS, HEAD_DIM, tm), col_heads),
            pl.BlockSpec((1, 4, tm // CMP_STRIDE, CMP_STRIDE * HEAD_DIM), row_heads),
            pl.BlockSpec((1, 4, tm, HEAD_DIM), row_heads),
            pl.BlockSpec((1, 4, 1, HEAD_DIM + V_ONES_ROWS, tm), lambda i: (i // spb, 0, i % spb, 0, 0)),
            pl.BlockSpec((wgt.shape[0], tm), lambda i: (0, i)),
            pl.BlockSpec((tm, wu.shape[1]), lambda i: (i, 0)),
        ],
        out_shape=[
            jax.ShapeDtypeStruct((batch, N_HEADS, HEAD_DIM, seq), BF16),
            jax.ShapeDtypeStruct((batch, N_HEADS, HEAD_DIM, seq), BF16),
            jax.ShapeDtypeStruct((batch, 4, seq // CMP_STRIDE, CMP_STRIDE * HEAD_DIM), BF16),
            jax.ShapeDtypeStruct((batch, 4, seq, HEAD_DIM), BF16),
            jax.ShapeDtypeStruct((batch, 4, spb, HEAD_DIM + V_ONES_ROWS, tm), BF16),
            jax.ShapeDtypeStruct((wgt.shape[0], n), F32),
            jax.ShapeDtypeStruct((n, wu.shape[1]), F32),
        ],
        scratch_shapes=[pltpu.VMEM((4, tm, HEAD_DIM), F32)],
        compiler_params=_cparams(1),
        name="inproj",
    )(x2d, ln, trig, trigt, wqt, wvt, wgt, wkc, wk, wu)


def _compress_kernel(t_ref, pe_ref, w1_ref, w2_ref, o_ref):
    t2 = t_ref[0, 0]
    w1 = w1_ref[0]
    a = _dot(t2, w1)
    pb = _dot(pe_ref[0], w1)
    bias = pb[0:1, :CMP_HIDDEN] + pb[1:2, CMP_HIDDEN:]
    nrow = a.shape[0]
    nxt = pltpu.roll(a[:, CMP_HIDDEN:], nrow - 1, 0)
    hid = jax.nn.gelu(a[:, :CMP_HIDDEN] + nxt + bias)
    out = _dot(hid.astype(BF16), w2_ref[0])
    row = lax.broadcasted_iota(I32, out.shape, 0)
    out = jnp.where(row < nrow - 1, out, 0.0)
    is_value = pl.program_id(1) >= N_KV
    o_ref[0, 0] = jnp.where(is_value, out.T, out).astype(BF16)


def _compress(kvc2, pe2, w1, w2):
    batch, four, nrow, width = kvc2.shape
    return pl.pallas_call(
        _compress_kernel,
        grid=(batch, four),
        in_specs=[
            pl.BlockSpec((1, 1, nrow, width), lambda b, j: (b, j, 0, 0)),
            pl.BlockSpec((1, SUBLANES, width), lambda b, j: (j // N_KV, 0, 0)),
            pl.BlockSpec((1, width, 2 * CMP_HIDDEN), lambda b, j: (j // N_KV, 0, 0)),
            pl.BlockSpec((1, CMP_HIDDEN, HEAD_DIM), lambda b, j: (j // N_KV, 0, 0)),
        ],
        out_specs=pl.BlockSpec((1, 1, nrow, HEAD_DIM), lambda b, j: (b, j, 0, 0)),
        out_shape=jax.ShapeDtypeStruct((batch, four, nrow, HEAD_DIM), BF16),
        compiler_params=_cparams(2),
        name="compress",
    )(kvc2, pe2, w1, w2)


def _nsa_kernel(qnt_ref, qrt_ref, cmp_ref, kk_ref, vt_ref, kblk_ref, gt_ref, gn_ref, o_ref,
                m_scr, acc_scr, qa_scr, ocmp_scr, s_scr, o_scr):
    qi = pl.program_id(1)
    tq = qnt_ref.shape[3]
    rows = GQA * tq
    n_cmp_pad = cmp_ref.shape[2]
    n_sel = kk_ref.shape[2] // SEL_LEN
    q0 = qi * tq

    m_scr[...] = jnp.full(m_scr.shape, NEG, F32)
    acc_scr[...] = jnp.zeros(acc_scr.shape, F32)

    def scores(item, slot):
        c, g, q_rows, k_tile, v_t, bias = item
        s_scr[slot] = _dot(k_tile, qa_scr[g, :q_rows, :])

    def softmax_pv(item, slot):
        c, g, q_rows, k_tile, v_t, bias = item
        s = s_scr[slot]
        if bias is not None:
            s = s + bias
        m_old = m_scr[c]
        m_new = jnp.maximum(m_old, jnp.max(s, axis=0, keepdims=True))
        alpha = jnp.exp2(m_old - m_new)
        p = jnp.exp2(s - m_new)
        acc_scr[c] = alpha * acc_scr[c] + _dot(v_t, p.astype(BF16))
        m_scr[c] = m_new

    def flash_out(c):
        return acc_scr[c, :HEAD_DIM, :] / acc_scr[c, HEAD_DIM:HEAD_DIM + 1, :]

    def tile_heads(a):
        return jnp.concatenate([a] * GQA, axis=1)

    kk_i = lax.broadcasted_iota(I32, (TK, tq), 0)
    qq_i = lax.broadcasted_iota(I32, (TK, tq), 1)
    causal_bias = tile_heads(jnp.where(kk_i <= qq_i, 0.0, NEG))
    band_bias = tile_heads(jnp.where(kk_i > qq_i, 0.0, NEG))
    n_win = WINDOW // TK

    for g in range(N_KV):
        qn4 = jnp.concatenate([qnt_ref[0, h] for h in range(g * GQA, (g + 1) * GQA)], axis=1)
        s = _dot(cmp_ref[0, g], qn4)
        n_idx = lax.broadcasted_iota(I32, (n_cmp_pad, rows), 0)
        t_idx = q0 + jnp.bitwise_and(lax.broadcasted_iota(I32, (n_cmp_pad, rows), 1), tq - 1)
        cmp_ok = n_idx * CMP_STRIDE + (CMP_LEN - 1) <= t_idx
        s = jnp.where(cmp_ok, s, NEG)
        m = jnp.max(s, axis=0, keepdims=True)
        e = jnp.where(cmp_ok, jnp.exp2(s - m), 0.0)
        l = jnp.sum(e, axis=0, keepdims=True)
        p = e * jnp.where(l > 0.0, 1.0 / l, 0.0)
        pb = p.astype(BF16)
        ocmp_scr[g] = _dot(cmp_ref[0, N_KV + g], pb)

        jn = lax.broadcasted_iota(I32, (n_sel, n_cmp_pad), 0) * SEL_LEN
        cn = lax.broadcasted_iota(I32, (n_sel, n_cmp_pad), 1) * CMP_STRIDE
        ov = jnp.clip(jnp.minimum(cn + CMP_LEN, jn + SEL_LEN) - jnp.maximum(cn, jn), 0, None)
        w_t = (ov.astype(F32) * (1.0 / CMP_LEN)).astype(BF16)
        imp4 = _dot(w_t, pb)
        imp = imp4[:, 0:tq]
        for r in range(1, GQA):
            imp = imp + imp4[:, r * tq:(r + 1) * tq]
        j_blk = lax.broadcasted_iota(I32, (n_sel, tq), 0)
        t_q = q0 + lax.broadcasted_iota(I32, (n_sel, tq), 1)
        sel_ok = j_blk * SEL_LEN <= t_q
        back = t_q // SEL_LEN - j_blk
        forced = (j_blk == 0) | ((back >= 0) & (back < N_LOCAL))
        val = jnp.where(sel_ok & forced, BIG, jnp.where(sel_ok, imp, -BIG))
        rank = jnp.zeros((n_sel, tq), F32)
        for i in range(n_sel):
            vi = val[i:i + 1, :]
            beats = (vi > val) | ((vi == val) & (j_blk > i))
            rank = rank + beats.astype(F32)
        keep = (rank < float(min(SEL_TOPK, n_sel))) & sel_ok
        bias_t = jnp.where(keep, 0.0, -SEL_MASK_BIAS)
        bias_t = jnp.concatenate([bias_t, jnp.zeros((HEAD_DIM - n_sel, tq), F32)], axis=0)
        qa_scr[g, :HEAD_DIM, :] = jnp.concatenate(
            [qrt_ref[0, h] for h in range(g * GQA, (g + 1) * GQA)], axis=1)
        qa_scr[g, HEAD_DIM:, :] = tile_heads(bias_t.astype(BF16))

    def sel_item(g, kj, bias):
        st = pl.multiple_of(kj * TK, TK)
        k_aug = jnp.concatenate([kk_ref[0, g, pl.ds(st, TK), :], kblk_ref[pl.ds(st, TK), :]], axis=1)
        return (g, g, 2 * HEAD_DIM, k_aug, vt_ref[0, g, kj], bias)

    def win_item(g, back):
        kj = jnp.maximum(qi - back, 0)
        st = pl.multiple_of(kj * TK, TK)
        off = jnp.where(qi >= back, 0.0, NEG)
        bias = causal_bias if back == 0 else (band_bias + off if back == n_win else off)
        return (N_KV + g, g, HEAD_DIM, kk_ref[0, N_KV + g, pl.ds(st, TK), :], vt_ref[0, N_KV + g, kj], bias)

    assert N_KV == 2
    scores(sel_item(0, 0, None), 0)

    def sel_body(kj, carry):
        scores(sel_item(1, kj, None), 1)
        softmax_pv(sel_item(0, kj, None), 0)
        scores(sel_item(0, kj + 1, None), 0)
        softmax_pv(sel_item(1, kj, None), 1)
        return carry

    lax.fori_loop(0, qi, sel_body, 0)
    tail = ([sel_item(g, qi, causal_bias) for g in range(N_KV)]
            + [win_item(g, back) for back in range(n_win, -1, -1) for g in range(N_KV)])
    for i, item in enumerate(tail):
        if i + 1 < len(tail):
            scores(tail[i + 1], (i + 1) % 2)
        softmax_pv(item, i % 2)

    gsig = jax.nn.sigmoid(gt_ref[...])
    for g in range(N_KV):
        o_cmp, o_sel, o_win = ocmp_scr[g], flash_out(g), flash_out(N_KV + g)
        for r in range(GQA):
            h = g * GQA + r
            sl = slice(r * tq, (r + 1) * tq)
            o_h = (gsig[3 * h:3 * h + 1, :] * o_cmp[:, sl]
                   + gsig[3 * h + 1:3 * h + 2, :] * o_sel[:, sl]
                   + gsig[3 * h + 2:3 * h + 3, :] * o_win[:, sl])
            o_scr[:, h * HEAD_DIM:(h + 1) * HEAD_DIM] = o_h.T

    o = o_scr[...]
    ms = jnp.mean(o * o, axis=-1, keepdims=True)
    o_ref[...] = (o * lax.rsqrt(ms + EPS) * gn_ref[...]).astype(BF16)


def _nsa(qnt, qrt, cmp, kk, vt, kblk, gt, gn):
    batch, _, _, seq = qnt.shape
    assert TQ == TK and WINDOW % TK == 0 and TQ & (TQ - 1) == 0
    nq = seq // TQ
    width = N_HEADS * HEAD_DIM
    return pl.pallas_call(
        _nsa_kernel,
        grid=(batch, nq),
        in_specs=[
            pl.BlockSpec((1, N_HEADS, HEAD_DIM, TQ), lambda b, i: (b, 0, 0, i)),
            pl.BlockSpec((1, N_HEADS, HEAD_DIM, TQ), lambda b, i: (b, 0, 0, i)),
            pl.BlockSpec((1,) + cmp.shape[1:], lambda b, i: (b, 0, 0, 0)),
            pl.BlockSpec((1,) + kk.shape[1:], lambda b, i: (b, 0, 0, 0)),
            pl.BlockSpec((1,) + vt.shape[1:], lambda b, i: (b, 0, 0, 0, 0)),
            pl.BlockSpec((seq, LANES), lambda b, i: (0, 0)),
            pl.BlockSpec((gt.shape[0], TQ), lambda b, i: (0, b * nq + i)),
            pl.BlockSpec((1, width), lambda b, i: (0, 0)),
        ],
        out_specs=pl.BlockSpec((TQ, width), lambda b, i: (b * nq + i, 0)),
        out_shape=jax.ShapeDtypeStruct((batch * seq, width), BF16),
        scratch_shapes=[
            pltpu.VMEM((2 * N_KV, 1, GQA * TQ), F32),
            pltpu.VMEM((2 * N_KV, vt.shape[3], GQA * TQ), F32),
            pltpu.VMEM((N_KV, 2 * HEAD_DIM, GQA * TQ), BF16),
            pltpu.VMEM((N_KV, HEAD_DIM, GQA * TQ), F32),
            pltpu.VMEM((2, TK, GQA * TQ), F32),
            pltpu.VMEM((TQ, width), F32),
        ],
        compiler_params=_cparams(2),
        name="nsa",
    )(qnt, qrt, cmp, kk, vt, kblk, gt, gn)


def _pool_kernel(u_ref, up_ref, w_ref, b_ref, sc_ref, gn_ref, o_ref, y_scr):
    si = pl.program_id(1)
    tm = u_ref.shape[1]
    halo = max(POOL_SIZES)
    cur = u_ref[0]
    prev = jnp.where(si > 0, up_ref[0, tm - halo:, :], 0.0)
    ext = jnp.concatenate([prev, cur], axis=0)
    t1 = (si * tm + 1 + lax.broadcasted_iota(I32, (tm, 1), 0)).astype(F32)
    cg = cur.shape[1] // len(POOL_SIZES)
    for gi, w in enumerate(POOL_SIZES):
        sl = slice(gi * cg, (gi + 1) * cg)
        acc = ext[:, sl]
        span = 1
        while span < w:
            acc = acc + jnp.concatenate([jnp.zeros((span, cg), F32), acc[:-span]], axis=0)
            span *= 2
        mean = acc[halo:] / jnp.minimum(t1, float(w))
        d = (mean - cur[:, sl]).astype(BF16)
        y = _dot(d, w_ref[gi]) + b_ref[:, sl]
        y_scr[:, sl] = y * sc_ref[:, sl]
    y = y_scr[...]
    ms = jnp.mean(y * y, axis=-1, keepdims=True)
    o_ref[...] = (y * lax.rsqrt(ms + EPS) * gn_ref[...]).astype(BF16)


def _pool(u3, w_pool, b_pool, pool_scale, gn_pool):
    batch, seq, c = u3.shape
    tm = TM_PROJ
    ns = seq // tm
    vec = lambda b, i: (0, 0)
    return pl.pallas_call(
        _pool_kernel,
        grid=(batch, ns),
        in_specs=[
            pl.BlockSpec((1, tm, c), lambda b, i: (b, i, 0)),
            pl.BlockSpec((1, tm, c), lambda b, i: (b, jnp.maximum(i - 1, 0), 0)),
            pl.BlockSpec(w_pool.shape, lambda b, i: (0, 0, 0)),
            pl.BlockSpec((1, c), vec),
            pl.BlockSpec((1, c), vec),
            pl.BlockSpec((1, c), vec),
        ],
        out_specs=pl.BlockSpec((tm, c), lambda b, i: (b * ns + i, 0)),
        out_shape=jax.ShapeDtypeStruct((batch * seq, c), BF16),
        scratch_shapes=[pltpu.VMEM((tm, c), F32)],
        compiler_params=_cparams(2),
        name="pool",
    )(u3, u3, w_pool, b_pool, pool_scale, gn_pool)


def _outproj_kernel(mn_ref, mp_ref, x_ref, w_ref, ln_ref, wr_ref, br_ref,
                    x1_ref, h2_ref, eid_ref, gate_ref):
    tm, d = x_ref.shape
    half = mn_ref.shape[1]
    chunk = 512
    mn = mn_ref[...]
    mp = mp_ref[...]
    ssq = jnp.zeros((tm, 1), F32)
    for c in range(d // chunk):
        sl = slice(c * chunk, (c + 1) * chunk)
        x1 = x_ref[:, sl] + (_dot(mn, w_ref[:half, sl]) + _dot(mp, w_ref[half:, sl]))
        x1_ref[:, sl] = x1
        ssq = ssq + jnp.sum(x1 * x1, axis=-1, keepdims=True)
    h2 = x1_ref[...] * lax.rsqrt(ssq * (1.0 / d) + EPS) * ln_ref[...]
    h2_ref[...] = _pack_bf16_halves(h2)

    logits = _dot_nt(wr_ref[...], h2.astype(BF16)) + br_ref[:, 0:1]
    ng, ne = N_EXPERT_GROUPS, EXPERTS_PER_GROUP
    row = lax.broadcasted_iota(I32, (ne, tm), 0)
    lg = jnp.where(row < ng, logits[0:ne], NEG)
    mg = jnp.max(lg, axis=0, keepdims=True)
    g_sel = jnp.min(jnp.where(lg == mg, row, ne), axis=0, keepdims=True)
    p_g = 1.0 / jnp.sum(jnp.exp(lg - mg), axis=0, keepdims=True)
    le = jnp.zeros((ne, tm), F32)
    for g in range(ng):
        le = jnp.where(g_sel == g, logits[ne * (g + 1):ne * (g + 2)], le)
    ex = jnp.exp(le - jnp.max(le, axis=0, keepdims=True))
    pe = ex / jnp.sum(ex, axis=0, keepdims=True)
    p1 = jnp.max(pe, axis=0, keepdims=True)
    i1 = jnp.min(jnp.where(pe == p1, row, ne), axis=0, keepdims=True)
    rest = jnp.where(row == i1, -1.0, pe)
    p2 = jnp.max(rest, axis=0, keepdims=True)
    i2 = jnp.min(jnp.where(rest == p2, row, ne), axis=0, keepdims=True)
    psum = p1 + p2
    zero_i = jnp.zeros((ne, tm), I32)
    eid = jnp.where(row == 0, g_sel * ne + i1, jnp.where(row == 1, g_sel * ne + i2, zero_i))
    gate = jnp.where(row == 0, p_g * p1 / psum, jnp.where(row == 1, p_g * p2 / psum, 0.0))
    eid_ref[...] = eid
    gate_ref[...] = gate


def _outproj(mix_nsa, mix_pool, x2d, w_out, ln_moe, w_router_t, b_router):
    n, d = x2d.shape
    tm = TM_PROJ
    half = mix_nsa.shape[1]
    const = lambda i: (0, 0)
    return pl.pallas_call(
        _outproj_kernel,
        grid=(n // tm,),
        in_specs=[
            pl.BlockSpec((tm, half), lambda i: (i, 0)),
            pl.BlockSpec((tm, half), lambda i: (i, 0)),
            pl.BlockSpec((tm, d), lambda i: (i, 0)),
            pl.BlockSpec(w_out.shape, const, pipeline_mode=pl.Buffered(1)),
            pl.BlockSpec((1, d), const),
            pl.BlockSpec(w_router_t.shape, const),
            pl.BlockSpec(b_router.shape, const),
        ],
        out_specs=[
            pl.BlockSpec((tm, d), lambda i: (i, 0)),
            pl.BlockSpec((tm, d // 2), lambda i: (i, 0)),
            pl.BlockSpec((SUBLANES, tm), lambda i: (0, i)),
            pl.BlockSpec((SUBLANES, tm), lambda i: (0, i)),
        ],
        out_shape=[
            jax.ShapeDtypeStruct((n, d), F32),
            jax.ShapeDtypeStruct((n, d // 2), U32),
            jax.ShapeDtypeStruct((SUBLANES, n), I32),
            jax.ShapeDtypeStruct((SUBLANES, n), F32),
        ],
        compiler_params=_cparams(1),
        name="outproj",
    )(mix_nsa, mix_pool, x2d, w_out, ln_moe, w_router_t, b_router)


def _dispatch_kernel(eid_ref, dest_ref, meta_ref, rank_scr):
    n = eid_ref.shape[1]
    t = T_RANK
    row = lax.broadcasted_iota(I32, (N_EXPERTS, t), 0)
    before = (lax.broadcasted_iota(I32, (t, t), 0) < lax.broadcasted_iota(I32, (t, t), 1)).astype(BF16)
    carry = jnp.zeros((N_EXPERTS, 1), F32)
    for k in range(2):
        for j in range(n // t):
            sl = slice(j * t, (j + 1) * t)
            oh = row == eid_ref[k:k + 1, sl]
            ohf = oh.astype(F32)
            prior = _dot(ohf.astype(BF16), before) + carry
            rank_scr[k:k + 1, sl] = jnp.sum(jnp.where(oh, prior, 0.0), axis=0, keepdims=True)
            carry = carry + jnp.sum(ohf, axis=1, keepdims=True)
    counts = jnp.broadcast_to(carry, (N_EXPERTS, LANES))
    padded = jnp.floor((counts + (TM_MOE - 1.0)) * (1.0 / TM_MOE)) * TM_MOE
    ends = padded
    r_idx = lax.broadcasted_iota(I32, (N_EXPERTS, LANES), 0)
    sh = 1
    while sh < N_EXPERTS:
        ends = ends + jnp.where(r_idx >= sh, pltpu.roll(ends, sh, 0), 0.0)
        sh *= 2
    starts = (ends - padded)[:, 0:1]
    for k in range(2):
        for j in range(n // t):
            sl = slice(j * t, (j + 1) * t)
            oh = row == eid_ref[k:k + 1, sl]
            base = jnp.sum(jnp.where(oh, starts, 0.0), axis=0, keepdims=True)
            dest_ref[k:k + 1, sl] = (rank_scr[k:k + 1, sl] + base).astype(I32)
    for k in range(2, SUBLANES):
        dest_ref[k:k + 1, :] = jnp.zeros((1, n), I32)
    blk_start = lax.broadcasted_iota(I32, (N_EXPERTS, LANES), 1).astype(F32) * TM_MOE
    blk_expert = jnp.sum((ends <= blk_start).astype(F32), axis=0, keepdims=True)
    blk_expert = jnp.minimum(blk_expert, N_EXPERTS - 1.0)
    n_used = ends[N_EXPERTS - 1:N_EXPERTS, :] * (1.0 / TM_MOE)
    r8 = lax.broadcasted_iota(I32, (SUBLANES, LANES), 0)
    meta = jnp.where(r8 == 0, blk_expert, jnp.where(r8 == 1, n_used, 0.0))
    meta_ref[...] = meta.astype(I32)


def _dispatch(eid):
    n = eid.shape[1]
    return pl.pallas_call(
        _dispatch_kernel,
        out_shape=[jax.ShapeDtypeStruct((SUBLANES, n), I32),
                   jax.ShapeDtypeStruct((SUBLANES, LANES), I32)],
        scratch_shapes=[pltpu.VMEM((SUBLANES, n), F32)],
        compiler_params=pltpu.CompilerParams(vmem_limit_bytes=VMEM_LIMIT_BYTES),
        name="dispatch",
    )(eid)


def _row_copy(src_ref, src_row, dst_ref, dst_row, sem):
    return pltpu.make_async_copy(src_ref.at[pl.ds(src_row, 1)], dst_ref.at[pl.ds(dst_row, 1)], sem)


def _scatter_kernel(dest_ref, h_ref, init_ref, xs_ref, sem):
    del init_ref
    tm = h_ref.shape[0]

    def start(r, c):
        for k in range(2):
            _row_copy(h_ref, r, xs_ref, dest_ref[0, k, r], sem).start()
        return c

    lax.fori_loop(0, tm, start, 0, unroll=DMA_ISSUE_UNROLL)
    for k in range(2):
        pltpu.make_async_copy(h_ref, xs_ref.at[pl.ds(0, tm)], sem).wait()


def _scatter(dest3, h2, xs_init):
    nsteps, _, tm = dest3.shape
    d = h2.shape[1]
    return pl.pallas_call(
        _scatter_kernel,
        grid=(nsteps,),
        in_specs=[
            pl.BlockSpec((1, 2, tm), lambda i: (i, 0, 0), memory_space=pltpu.SMEM),
            pl.BlockSpec((tm, d), lambda i: (i, 0)),
            pl.BlockSpec(memory_space=pl.ANY),
        ],
        out_specs=pl.BlockSpec(memory_space=pl.ANY),
        out_shape=jax.ShapeDtypeStruct(xs_init.shape, xs_init.dtype),
        scratch_shapes=[pltpu.SemaphoreType.DMA(())],
        input_output_aliases={2: 0},
        compiler_params=_cparams(1),
        name="scatter",
    )(dest3, h2, xs_init)


def _experts_kernel(meta_ref, xs_ref, wg_hbm, wu_hbm, wd_hbm, ys_ref,
                    wg_buf, wu_buf, wd_buf, wg16, wu16, wd16, ord_ref, sem):
    c = pl.program_id(0)
    n_used = meta_ref[1, 0]
    e = meta_ref[0, c]
    first = (c == 0) | (e != meta_ref[0, jnp.maximum(c - 1, 0)])

    def fetch(expert, slot):
        return [pltpu.make_async_copy(w.at[expert], buf.at[slot], sem.at[slot, i])
                for i, (w, buf) in enumerate(((wg_hbm, wg_buf), (wu_hbm, wu_buf), (wd_hbm, wd_buf)))]

    @pl.when(c == 0)
    def _():
        ord_ref[0] = 0
        for cp in fetch(e, 0):
            cp.start()

    @pl.when(first & (c < n_used))
    def _():
        @pl.when(c > 0)
        def _():
            ord_ref[0] = ord_ref[0] + 1
        slot = ord_ref[0] % 2
        for cp in fetch(e, slot):
            cp.wait()
        nxt = lax.while_loop(lambda j: (j < n_used) & (meta_ref[0, jnp.minimum(j, n_used - 1)] == e),
                             lambda j: j + 1, c + 1)

        @pl.when(nxt < n_used)
        def _():
            for cp in fetch(meta_ref[0, jnp.minimum(nxt, n_used - 1)], 1 - slot):
                cp.start()

    def mlp(weights):
        lo, hi = _unpack_bf16_halves(xs_ref[...])
        lo, hi = lo.astype(BF16), hi.astype(BF16)
        half = lo.shape[1]
        wg, wu, wd = weights

        def proj(w):
            return _dot(lo, w[:half, :]) + _dot(hi, w[half:, :])

        act = (jax.nn.silu(proj(wg)) * proj(wu)).astype(BF16)
        ys_ref[...] = _pack_bf16_halves(_dot(act, wd))

    @pl.when(first & (c < n_used))
    def _():
        slot = ord_ref[0] % 2
        cast = [buf[slot].astype(BF16) for buf in (wg_buf, wu_buf, wd_buf)]
        for ref16, w in zip((wg16, wu16, wd16), cast):
            ref16[...] = w
        mlp(cast)

    @pl.when(jnp.logical_not(first) & (c < n_used))
    def _():
        mlp((wg16[...], wu16[...], wd16[...]))

    @pl.when(c >= n_used)
    def _():
        ys_ref[...] = jnp.zeros(ys_ref.shape, U32)


def _experts(meta, xs, w_gate, w_up, w_down):
    cap, dw = xs.shape
    nblk = cap // TM_MOE
    d, ff = w_gate.shape[1:]

    def row_map(c, meta):
        return (jnp.minimum(c, meta[1, 0] - 1), 0)

    hbm = pl.BlockSpec(memory_space=pl.ANY)
    return pl.pallas_call(
        _experts_kernel,
        grid_spec=pltpu.PrefetchScalarGridSpec(
            num_scalar_prefetch=1,
            grid=(nblk,),
            in_specs=[pl.BlockSpec((TM_MOE, dw), row_map), hbm, hbm, hbm],
            out_specs=pl.BlockSpec((TM_MOE, dw), lambda c, meta: (c, 0)),
            scratch_shapes=[
                pltpu.VMEM((2, d, ff), F32), pltpu.VMEM((2, d, ff), F32), pltpu.VMEM((2, ff, d), F32),
                pltpu.VMEM((d, ff), BF16), pltpu.VMEM((d, ff), BF16), pltpu.VMEM((ff, d), BF16),
                pltpu.SMEM((1,), I32),
                pltpu.SemaphoreType.DMA((2, 3)),
            ],
        ),
        out_shape=jax.ShapeDtypeStruct((cap, dw), U32),
        compiler_params=_cparams(1),
        name="experts",
    )(meta, xs, w_gate, w_up, w_down)


def _combine_kernel(dest_ref, dest_next_ref, x1_ref, gate_ref, ln_ref, ys_ref, o_ref, ybuf, sem):
    i = pl.program_id(0)
    tm = x1_ref.shape[0]
    slot = i % 2

    def issue(d_ref, s):
        def start(r, c):
            for k in range(2):
                _row_copy(ys_ref, d_ref[0, k, r], ybuf.at[s, k], r, sem.at[s]).start()
            return c

        lax.fori_loop(0, tm, start, 0, unroll=DMA_ISSUE_UNROLL)

    @pl.when(i == 0)
    def _():
        issue(dest_ref, 0)

    @pl.when(i + 1 < pl.num_programs(0))
    def _():
        issue(dest_next_ref, 1 - slot)

    for k in range(2):
        pltpu.make_async_copy(ys_ref.at[pl.ds(0, tm)], ybuf.at[slot, k], sem.at[slot]).wait()
    gate = gate_ref[...]
    lo0, hi0 = _unpack_bf16_halves(ybuf[slot, 0])
    lo1, hi1 = _unpack_bf16_halves(ybuf[slot, 1])
    g0, g1 = gate[:, 0:1], gate[:, 1:2]
    moe = jnp.concatenate([lo0 * g0 + lo1 * g1, hi0 * g0 + hi1 * g1], axis=1)
    x2 = x1_ref[...] + moe
    ms = jnp.mean(x2 * x2, axis=-1, keepdims=True)
    o_ref[...] = x2 * lax.rsqrt(ms + EPS) * ln_ref[...]


def _combine(dest3, x1, gate_rows, ln_final, ys):
    n, d = x1.shape
    tm = TM_COMBINE
    last = n // tm - 1
    return pl.pallas_call(
        _combine_kernel,
        grid=(n // tm,),
        in_specs=[
            pl.BlockSpec((1, 2, tm), lambda i: (i, 0, 0), memory_space=pltpu.SMEM),
            pl.BlockSpec((1, 2, tm), lambda i: (jnp.minimum(i + 1, last), 0, 0), memory_space=pltpu.SMEM),
            pl.BlockSpec((tm, d), lambda i: (i, 0)),
            pl.BlockSpec((tm, LANES), lambda i: (i, 0)),
            pl.BlockSpec((1, d), lambda i: (0, 0)),
            pl.BlockSpec(memory_space=pl.ANY),
        ],
        out_specs=pl.BlockSpec((tm, d), lambda i: (i, 0)),
        out_shape=jax.ShapeDtypeStruct((n, d), F32),
        scratch_shapes=[pltpu.VMEM((2, 2, tm, d // 2), U32), pltpu.SemaphoreType.DMA((2,))],
        compiler_params=_cparams(1),
        name="combine",
    )(dest3, dest3, x1, gate_rows, ln_final, ys)


def kernel(x, positions, ln_mix, w_in, pe_cmp_k, w_cmp_k1, w_cmp_k2, pe_cmp_v, w_cmp_v1, w_cmp_v2, w_pool, b_pool, pool_scale, gn_nsa, gn_pool, w_out, ln_moe, w_router_group, b_router_group, w_router_expert, b_router_expert, w_gate, w_up, w_down, ln_final):
    batch, seq, d = x.shape
    n = batch * seq
    depth = w_in.shape[0]
    nsa_w = N_HEADS * HEAD_DIM
    kv_w = N_KV * HEAD_DIM
    cut_gates = nsa_w + 6 * kv_w
    cut_pool = cut_gates + N_GATES

    inv_freq = ROPE_THETA ** (-jnp.arange(0, ROT_DIM, 2, dtype=F32) / ROT_DIM)
    ang = positions.astype(F32).reshape(n, 1) * inv_freq[None, :]
    cos, sin = jnp.cos(ang), jnp.sin(ang)
    rest = HEAD_DIM - ROT_DIM
    trig = jnp.stack([jnp.concatenate([cos, cos, jnp.ones((n, rest), F32)], axis=1),
                      jnp.concatenate([-sin, sin, jnp.zeros((n, rest), F32)], axis=1)])
    trigt = jnp.transpose(trig, (0, 2, 1))
    kblk = (jnp.arange(seq, dtype=I32)[:, None] // SEL_LEN == jnp.arange(LANES, dtype=I32)[None, :]).astype(BF16)

    x2d = x.reshape(n, d)
    assert depth == 1, "single-layer operation"
    for l in range(depth):
        wi = w_in[l]
        o_kc, o_ks = nsa_w, nsa_w + 2 * kv_w
        o_vs, o_kw, o_vw = o_ks + kv_w, o_ks + 2 * kv_w, o_ks + 3 * kv_w
        gate_rows_pad = -(-N_GATES // SUBLANES) * SUBLANES
        wqt = wi[:, :nsa_w].T.astype(BF16)
        wvt = jnp.concatenate([wi[:, o_vs:o_kw], wi[:, o_vw:cut_gates]], axis=1).T.astype(BF16)
        wgt = jnp.pad(wi[:, cut_gates:cut_pool], ((0, 0), (0, gate_rows_pad - N_GATES))).T.astype(BF16)
        wkc = wi[:, o_kc:o_ks].astype(BF16)
        wk = jnp.concatenate([wi[:, o_ks:o_vs], wi[:, o_kw:o_vw]], axis=1).astype(BF16)
        wu = wi[:, cut_pool:].astype(BF16)
        qnt, qrt, kvc, kk, vt, gt, u = _inproj(x2d, ln_mix[l][None], trig, trigt, wqt, wvt, wgt, wkc, wk, wu,
                                               batch, seq)

        half = CMP_LEN * HEAD_DIM // 2
        def stack_w1(w):
            return jnp.concatenate([w[:half], w[half:]], axis=1)
        w1 = jnp.stack([stack_w1(w_cmp_k1[l]), stack_w1(w_cmp_v1[l])]).astype(BF16)
        w2 = jnp.stack([w_cmp_k2[l], w_cmp_v2[l]]).astype(BF16)
        def pe_rows(pe):
            return jnp.pad(pe.reshape(2, half), ((0, SUBLANES - 2), (0, 0)))
        pe2 = jnp.stack([pe_rows(pe_cmp_k[l]), pe_rows(pe_cmp_v[l])]).astype(BF16)
        cmp = _compress(kvc, pe2, w1, w2)

        mix_nsa = _nsa(qnt, qrt, cmp, kk, vt, kblk, gt, gn_nsa[l][None])
        mix_pool = _pool(u.reshape(batch, seq, -1), w_pool[l].astype(BF16), b_pool[l][None],
                         pool_scale[l][None], gn_pool[l][None])

        ng, ne = N_EXPERT_GROUPS, EXPERTS_PER_GROUP
        w_r = jnp.concatenate([
            w_router_group[l].T, jnp.zeros((ne - ng, d), F32),
            jnp.transpose(w_router_expert[l], (0, 2, 1)).reshape(ng * ne, d),
            jnp.zeros((LANES - ne - ng * ne, d), F32)], axis=0).astype(BF16)
        b_r = jnp.concatenate([
            b_router_group[l], jnp.zeros((ne - ng,), F32), b_router_expert[l].reshape(-1),
            jnp.zeros((LANES - ne - ng * ne,), F32)])
        b_r = jnp.broadcast_to(b_r[:, None], (LANES, LANES))
        x1, h2, eid, gate = _outproj(mix_nsa, mix_pool, x2d, w_out[l].astype(BF16), ln_moe[l][None], w_r, b_r)

        dest, meta = _dispatch(eid)
        cap = (2 * n // TM_MOE + N_EXPERTS) * TM_MOE
        def per_tile(tm):
            return jnp.transpose(dest[:2].reshape(2, n // tm, tm), (1, 0, 2))
        xs = _scatter(per_tile(TS_SCATTER), h2, jnp.zeros((cap, d // 2), U32))
        ys = _experts(meta, xs, w_gate[l], w_up[l], w_down[l])
        dest_tok = per_tile(TM_COMBINE)
        gate_rows = jnp.pad(gate[:2].T, ((0, 0), (0, LANES - 2)))
        x2d = _combine(dest_tok, x1, gate_rows, ln_final[None], ys)
    return x2d.reshape(batch, seq, d)
```

```python
import functools

import jax
import jax.numpy as jnp
from jax import lax
from jax.experimental import pallas as pl
from jax.experimental.pallas import tpu as pltpu

F32 = jnp.float32
BF16 = jnp.bfloat16
I32 = jnp.int32

HEAD_DIM = 128
N_HEADS = 8
N_KV = 2
GQA = N_HEADS // N_KV
N_GATES = 3 * N_HEADS
ROT_DIM = HEAD_DIM // 4
ROPE_THETA = 500000.0
CMP_LEN = 32
CMP_STRIDE = 16
CMP_HIDDEN = 2 * HEAD_DIM
SEL_LEN = 64
SEL_TOPK = 16
N_LOCAL = 2
WINDOW = 512
POOL_SIZES = (2, 4, 8, 16)
N_EXPERT_GROUPS = 4
EXPERTS_PER_GROUP = 8
N_EXPERTS = N_EXPERT_GROUPS * EXPERTS_PER_GROUP
EPS = 1e-6
NEG = -1e30
BIG = 1e30
SCALE = HEAD_DIM ** -0.5
LOG2E = 1.4426950408889634
Q_SCALE = SCALE * LOG2E

LANES = 128
SUBLANES = 8
VMEM_LIMIT_BYTES = 56 * 1024 * 1024

TM_PROJ = 256
TQ = 256
TK = 256
SEL_MASK_BIAS = 32768.0
V_ONES_ROWS = 16
TM_MOE = 256
TS_SCATTER = 512
TM_COMBINE = 128
W_SLOTS = 3
DMA_ISSUE_UNROLL = 8
T_RANK = 512


def _cparams(n_axes, flags=None):
    return pltpu.CompilerParams(dimension_semantics=("arbitrary",) * n_axes,
                                vmem_limit_bytes=VMEM_LIMIT_BYTES, flags=flags)


def _dot(a, b):
    return jnp.dot(a, b, preferred_element_type=F32)


def _dot_nt(a, b):
    return lax.dot_general(a, b, (((1,), (1,)), ((), ())), preferred_element_type=F32)


U32 = jnp.uint32
_HI16 = 0xFFFF0000


def _pack_bf16_halves(x):
    c = x.shape[1] // 2
    bits = lambda t: lax.bitcast_convert_type(t.astype(BF16).astype(F32), U32)
    return lax.shift_right_logical(bits(x[:, :c]), U32(16)) | (bits(x[:, c:]) & U32(_HI16))


def _unpack_bf16_halves(w):
    lo = lax.bitcast_convert_type(lax.shift_left(w, U32(16)), F32)
    hi = lax.bitcast_convert_type(w & U32(_HI16), F32)
    return lo, hi


def _rope(t, cos, sin, axis):
    idx = lax.broadcasted_iota(I32, t.shape, axis)
    partner = jnp.where(idx < ROT_DIM // 2,
                        pltpu.roll(t, HEAD_DIM - ROT_DIM // 2, axis),
                        pltpu.roll(t, ROT_DIM // 2, axis))
    return t * cos + partner * sin


def _inproj_kernel(x_ref, ln_ref, trig_ref, trigt_ref, wq_ref, wv_ref, wg_ref, wkc_ref, wk_ref, wu_ref,
                   qnt_ref, qrt_ref, kvc_ref, kk_ref, vt_ref, gt_ref, u_ref, kc_scr):
    x = x_ref[...]
    ms = jnp.mean(x * x, axis=-1, keepdims=True)
    hb = (x * lax.rsqrt(ms + EPS) * ln_ref[...]).astype(BF16)
    chunk = 4 * HEAD_DIM

    cost, sint = trigt_ref[0], trigt_ref[1]
    for ch in range(2):
        r = _dot(hb, wq_ref[:, ch * chunk:(ch + 1) * chunk])
        for j in range(4):
            t = r[:, j * HEAD_DIM:(j + 1) * HEAD_DIM].T
            qnt_ref[0, ch * 4 + j] = (t * Q_SCALE).astype(BF16)
            qrt_ref[0, ch * 4 + j] = (_rope(t, cost, sint, 0) * Q_SCALE).astype(BF16)
    r = _dot(hb, wv_ref[...])
    ones = jnp.ones((V_ONES_ROWS, r.shape[0]), BF16)
    for j in range(4):
        vt_ref[0, j, 0] = jnp.concatenate([r[:, j * HEAD_DIM:(j + 1) * HEAD_DIM].T.astype(BF16), ones], axis=0)
    gt_ref[...] = _dot(hb, wg_ref[...]).T[:gt_ref.shape[0], :]

    r = _dot(hb, wkc_ref[...])
    groups = kvc_ref.shape[2]
    for j in range(4):
        kc_scr[j] = r[:, j * HEAD_DIM:(j + 1) * HEAD_DIM]
        for i in range(CMP_STRIDE):
            kvc_ref[0, j, :, i * HEAD_DIM:(i + 1) * HEAD_DIM] = (
                kc_scr[j, pl.ds(i, groups, stride=CMP_STRIDE), :].astype(BF16))
    cos, sin = trig_ref[0], trig_ref[1]
    r = _dot(hb, wk_ref[...])
    for j in range(4):
        kk_ref[0, j] = _rope(r[:, j * HEAD_DIM:(j + 1) * HEAD_DIM], cos, sin, 1).astype(BF16)
    for ch in range(2):
        u_ref[:, ch * chunk:(ch + 1) * chunk] = _dot(hb, wu_ref[:, ch * chunk:(ch + 1) * chunk])


def _inproj(x2d, ln, trig, trigt, wq, wv, wg, wkc, wk, wu, gate_rows, batch, seq):
    n, d = x2d.shape
    tm = TM_PROJ
    assert tm == TK
    spb = seq // tm
    const = lambda i: (0, 0)
    resident = lambda w: pl.BlockSpec(w.shape, const, pipeline_mode=pl.Buffered(1))
    row_heads = lambda i: (i // spb, 0, i % spb, 0)
    col_heads = lambda i: (i // spb, 0, 0, i % spb)
    return pl.pallas_call(
        _inproj_kernel,
        grid=(n // tm,),
        in_specs=[
            pl.BlockSpec((tm, d), lambda i: (i, 0)),
            pl.BlockSpec((1, d), const),
            pl.BlockSpec((2, tm, HEAD_DIM), lambda i: (0, i, 0)),
            pl.BlockSpec((2, HEAD_DIM, tm), lambda i: (0, 0, i)),
            resident(wq), resident(wv), resident(wg), resident(wkc), resident(wk), resident(wu),
        ],
        out_specs=[
            pl.BlockSpec((1, N_HEADS, HEAD_DIM, tm), col_heads),
            pl.BlockSpec((1, N_HEADS, HEAD_DIM, tm), col_heads),
            pl.BlockSpec((1, 4, tm // CMP_STRIDE, CMP_STRIDE * HEAD_DIM), row_heads),
            pl.BlockSpec((1, 4, tm, HEAD_DIM), row_heads),
            pl.BlockSpec((1, 4, 1, HEAD_DIM + V_ONES_ROWS, tm), lambda i: (i // spb, 0, i % spb, 0, 0)),
            pl.BlockSpec((gate_rows, tm), lambda i: (0, i)),
            pl.BlockSpec((tm, wu.shape[1]), lambda i: (i, 0)),
        ],
        out_shape=[
            jax.ShapeDtypeStruct((batch, N_HEADS, HEAD_DIM, seq), BF16),
            jax.ShapeDtypeStruct((batch, N_HEADS, HEAD_DIM, seq), BF16),
            jax.ShapeDtypeStruct((batch, 4, seq // CMP_STRIDE, CMP_STRIDE * HEAD_DIM), BF16),
            jax.ShapeDtypeStruct((batch, 4, seq, HEAD_DIM), BF16),
            jax.ShapeDtypeStruct((batch, 4, spb, HEAD_DIM + V_ONES_ROWS, tm), BF16),
            jax.ShapeDtypeStruct((gate_rows, n), F32),
            jax.ShapeDtypeStruct((n, wu.shape[1]), F32),
        ],
        scratch_shapes=[pltpu.VMEM((4, tm, HEAD_DIM), F32)],
        compiler_params=_cparams(1),
        name="inproj",
    )(x2d, ln, trig, trigt, wq, wv, wg, wkc, wk, wu)


def _compress_kernel(t_ref, pe_ref, w1_ref, w2_ref, o_ref):
    t2 = t_ref[0, 0]
    w1 = w1_ref[0]
    a = _dot(t2, w1)
    pb = _dot(pe_ref[0], w1)
    bias = pb[0:1, :CMP_HIDDEN] + pb[1:2, CMP_HIDDEN:]
    nrow = a.shape[0]
    nxt = pltpu.roll(a[:, CMP_HIDDEN:], nrow - 1, 0)
    hid = jax.nn.gelu(a[:, :CMP_HIDDEN] + nxt + bias)
    out = _dot(hid.astype(BF16), w2_ref[0])
    row = lax.broadcasted_iota(I32, out.shape, 0)
    out = jnp.where(row < nrow - 1, out, 0.0)
    is_value = pl.program_id(1) >= N_KV
    o_ref[0, 0] = jnp.where(is_value, out.T, out).astype(BF16)


def _compress(kvc2, pe2, w1, w2):
    batch, four, nrow, width = kvc2.shape
    return pl.pallas_call(
        _compress_kernel,
        grid=(batch, four),
        in_specs=[
            pl.BlockSpec((1, 1, nrow, width), lambda b, j: (b, j, 0, 0)),
            pl.BlockSpec((1, SUBLANES, width), lambda b, j: (j // N_KV, 0, 0)),
            pl.BlockSpec((1, width, 2 * CMP_HIDDEN), lambda b, j: (j // N_KV, 0, 0)),
            pl.BlockSpec((1, CMP_HIDDEN, HEAD_DIM), lambda b, j: (j // N_KV, 0, 0)),
        ],
        out_specs=pl.BlockSpec((1, 1, nrow, HEAD_DIM), lambda b, j: (b, j, 0, 0)),
        out_shape=jax.ShapeDtypeStruct((batch, four, nrow, HEAD_DIM), BF16),
        compiler_params=_cparams(2),
        name="compress",
    )(kvc2, pe2, w1, w2)


def _nsa_kernel(qnt_ref, qrt_ref, cmp_ref, kk_ref, vt_ref, kblk_ref, gt_ref, gn_ref, o_ref,
                m_scr, acc_scr, qa_scr, ocmp_scr, s_scr, o_scr):
    qi = pl.program_id(1)
    tq = qnt_ref.shape[3]
    rows = GQA * tq
    n_cmp_pad = cmp_ref.shape[2]
    n_sel = kk_ref.shape[2] // SEL_LEN
    q0 = qi * tq

    m_scr[...] = jnp.full(m_scr.shape, NEG, F32)
    acc_scr[...] = jnp.zeros(acc_scr.shape, F32)

    def scores(item, slot):
        c, g, q_rows, k_tile, v_t, bias = item
        s_scr[slot] = _dot(k_tile, qa_scr[g, :q_rows, :])

    def softmax_pv(item, slot):
        c, g, q_rows, k_tile, v_t, bias = item
        s = s_scr[slot]
        if bias is not None:
            s = s + bias
        m_old = m_scr[c]
        m_new = jnp.maximum(m_old, jnp.max(s, axis=0, keepdims=True))
        alpha = jnp.exp2(m_old - m_new)
        p = jnp.exp2(s - m_new)
        acc_scr[c] = alpha * acc_scr[c] + _dot(v_t, p.astype(BF16))
        m_scr[c] = m_new

    def flash_out(c):
        return acc_scr[c, :HEAD_DIM, :] / acc_scr[c, HEAD_DIM:HEAD_DIM + 1, :]

    def tile_heads(a):
        return jnp.concatenate([a] * GQA, axis=1)

    kk_i = lax.broadcasted_iota(I32, (TK, tq), 0)
    qq_i = lax.broadcasted_iota(I32, (TK, tq), 1)
    causal_bias = tile_heads(jnp.where(kk_i <= qq_i, 0.0, NEG))
    band_bias = tile_heads(jnp.where(kk_i > qq_i, 0.0, NEG))
    n_win = WINDOW // TK

    for g in range(N_KV):
        qn4 = jnp.concatenate([qnt_ref[0, h] for h in range(g * GQA, (g + 1) * GQA)], axis=1)
        s = _dot(cmp_ref[0, g], qn4)
        n_idx = lax.broadcasted_iota(I32, (n_cmp_pad, rows), 0)
        t_idx = q0 + jnp.bitwise_and(lax.broadcasted_iota(I32, (n_cmp_pad, rows), 1), tq - 1)
        cmp_ok = n_idx * CMP_STRIDE + (CMP_LEN - 1) <= t_idx
        s = jnp.where(cmp_ok, s, NEG)
        m = jnp.max(s, axis=0, keepdims=True)
        e = jnp.where(cmp_ok, jnp.exp2(s - m), 0.0)
        l = jnp.sum(e, axis=0, keepdims=True)
        p = e * jnp.where(l > 0.0, 1.0 / l, 0.0)
        pb = p.astype(BF16)
        ocmp_scr[g] = _dot(cmp_ref[0, N_KV + g], pb)

        jn = lax.broadcasted_iota(I32, (n_sel, n_cmp_pad), 0) * SEL_LEN
        cn = lax.broadcasted_iota(I32, (n_sel, n_cmp_pad), 1) * CMP_STRIDE
        ov = jnp.clip(jnp.minimum(cn + CMP_LEN, jn + SEL_LEN) - jnp.maximum(cn, jn), 0, None)
        w_t = (ov.astype(F32) * (1.0 / CMP_LEN)).astype(BF16)
        imp4 = _dot(w_t, pb)
        imp = imp4[:, 0:tq]
        for r in range(1, GQA):
            imp = imp + imp4[:, r * tq:(r + 1) * tq]
        j_blk = lax.broadcasted_iota(I32, (n_sel, tq), 0)
        t_q = q0 + lax.broadcasted_iota(I32, (n_sel, tq), 1)
        sel_ok = j_blk * SEL_LEN <= t_q
        back = t_q // SEL_LEN - j_blk
        forced = (j_blk == 0) | ((back >= 0) & (back < N_LOCAL))
        val = jnp.where(sel_ok & forced, BIG, jnp.where(sel_ok, imp, -BIG))
        rank = jnp.zeros((n_sel, tq), F32)
        for i in range(n_sel):
            vi = val[i:i + 1, :]
            beats = (vi > val) | ((vi == val) & (j_blk > i))
            rank = rank + beats.astype(F32)
        keep = (rank < float(min(SEL_TOPK, n_sel))) & sel_ok
        bias_t = jnp.where(keep, 0.0, -SEL_MASK_BIAS)
        bias_t = jnp.concatenate([bias_t, jnp.zeros((HEAD_DIM - n_sel, tq), F32)], axis=0)
        qa_scr[g, :HEAD_DIM, :] = jnp.concatenate(
            [qrt_ref[0, h] for h in range(g * GQA, (g + 1) * GQA)], axis=1)
        qa_scr[g, HEAD_DIM:, :] = tile_heads(bias_t.astype(BF16))

    def sel_item(g, kj, bias):
        st = pl.multiple_of(kj * TK, TK)
        k_aug = jnp.concatenate([kk_ref[0, g, pl.ds(st, TK), :], kblk_ref[pl.ds(st, TK), :]], axis=1)
        return (g, g, 2 * HEAD_DIM, k_aug, vt_ref[0, g, kj], bias)

    def win_item(g, back):
        kj = jnp.maximum(qi - back, 0)
        st = pl.multiple_of(kj * TK, TK)
        off = jnp.where(qi >= back, 0.0, NEG)
        bias = causal_bias if back == 0 else (band_bias + off if back == n_win else off)
        return (N_KV + g, g, HEAD_DIM, kk_ref[0, N_KV + g, pl.ds(st, TK), :], vt_ref[0, N_KV + g, kj], bias)

    assert N_KV == 2
    scores(sel_item(0, 0, None), 0)

    def sel_body(kj, carry):
        scores(sel_item(1, kj, None), 1)
        softmax_pv(sel_item(0, kj, None), 0)
        scores(sel_item(0, kj + 1, None), 0)
        softmax_pv(sel_item(1, kj, None), 1)
        return carry

    lax.fori_loop(0, qi, sel_body, 0)
    tail = ([sel_item(g, qi, causal_bias) for g in range(N_KV)]
            + [win_item(g, back) for back in range(n_win, -1, -1) for g in range(N_KV)])
    for i, item in enumerate(tail):
        if i + 1 < len(tail):
            scores(tail[i + 1], (i + 1) % 2)
        softmax_pv(item, i % 2)

    gsig = jax.nn.sigmoid(gt_ref[...])
    for g in range(N_KV):
        o_cmp, o_sel, o_win = ocmp_scr[g], flash_out(g), flash_out(N_KV + g)
        for r in range(GQA):
            h = g * GQA + r
            sl = slice(r * tq, (r + 1) * tq)
            o_h = (gsig[3 * h:3 * h + 1, :] * o_cmp[:, sl]
                   + gsig[3 * h + 1:3 * h + 2, :] * o_sel[:, sl]
                   + gsig[3 * h + 2:3 * h + 3, :] * o_win[:, sl])
            o_scr[:, h * HEAD_DIM:(h + 1) * HEAD_DIM] = o_h.T

    o = o_scr[...]
    ms = jnp.mean(o * o, axis=-1, keepdims=True)
    o_ref[...] = (o * lax.rsqrt(ms + EPS) * gn_ref[...]).astype(BF16)


def _nsa(qnt, qrt, cmp, kk, vt, kblk, gt, gn):
    batch, _, _, seq = qnt.shape
    assert TQ == TK and WINDOW % TK == 0 and TQ & (TQ - 1) == 0
    nq = seq // TQ
    width = N_HEADS * HEAD_DIM
    return pl.pallas_call(
        _nsa_kernel,
        grid=(batch, nq),
        in_specs=[
            pl.BlockSpec((1, N_HEADS, HEAD_DIM, TQ), lambda b, i: (b, 0, 0, i)),
            pl.BlockSpec((1, N_HEADS, HEAD_DIM, TQ), lambda b, i: (b, 0, 0, i)),
            pl.BlockSpec((1,) + cmp.shape[1:], lambda b, i: (b, 0, 0, 0)),
            pl.BlockSpec((1,) + kk.shape[1:], lambda b, i: (b, 0, 0, 0)),
            pl.BlockSpec((1,) + vt.shape[1:], lambda b, i: (b, 0, 0, 0, 0)),
            pl.BlockSpec((seq, LANES), lambda b, i: (0, 0)),
            pl.BlockSpec((gt.shape[0], TQ), lambda b, i: (0, b * nq + i)),
            pl.BlockSpec((1, width), lambda b, i: (0, 0)),
        ],
        out_specs=pl.BlockSpec((TQ, width), lambda b, i: (b * nq + i, 0)),
        out_shape=jax.ShapeDtypeStruct((batch * seq, width), BF16),
        scratch_shapes=[
            pltpu.VMEM((2 * N_KV, 1, GQA * TQ), F32),
            pltpu.VMEM((2 * N_KV, vt.shape[3], GQA * TQ), F32),
            pltpu.VMEM((N_KV, 2 * HEAD_DIM, GQA * TQ), BF16),
            pltpu.VMEM((N_KV, HEAD_DIM, GQA * TQ), F32),
            pltpu.VMEM((2, TK, GQA * TQ), F32),
            pltpu.VMEM((TQ, width), F32),
        ],
        compiler_params=_cparams(2),
        name="nsa",
    )(qnt, qrt, cmp, kk, vt, kblk, gt, gn)


def _pool_kernel(u_ref, up_ref, w_ref, b_ref, sc_ref, gn_ref, o_ref, y_scr):
    si = pl.program_id(1)
    tm = u_ref.shape[1]
    halo = max(POOL_SIZES)
    cur = u_ref[0]
    prev = jnp.where(si > 0, up_ref[0, tm - halo:, :], 0.0)
    ext = jnp.concatenate([prev, cur], axis=0)
    t1 = (si * tm + 1 + lax.broadcasted_iota(I32, (tm, 1), 0)).astype(F32)
    cg = cur.shape[1] // len(POOL_SIZES)
    for gi, w in enumerate(POOL_SIZES):
        sl = slice(gi * cg, (gi + 1) * cg)
        acc = ext[:, sl]
        span = 1
        while span < w:
            acc = acc + jnp.concatenate([jnp.zeros((span, cg), F32), acc[:-span]], axis=0)
            span *= 2
        mean = acc[halo:] / jnp.minimum(t1, float(w))
        d = (mean - cur[:, sl]).astype(BF16)
        y = _dot(d, w_ref[gi]) + b_ref[:, sl]
        y_scr[:, sl] = y * sc_ref[:, sl]
    y = y_scr[...]
    ms = jnp.mean(y * y, axis=-1, keepdims=True)
    o_ref[...] = (y * lax.rsqrt(ms + EPS) * gn_ref[...]).astype(BF16)


def _pool(u3, w_pool, b_pool, pool_scale, gn_pool):
    batch, seq, c = u3.shape
    tm = TM_PROJ
    ns = seq // tm
    vec = lambda b, i: (0, 0)
    return pl.pallas_call(
        _pool_kernel,
        grid=(batch, ns),
        in_specs=[
            pl.BlockSpec((1, tm, c), lambda b, i: (b, i, 0)),
            pl.BlockSpec((1, tm, c), lambda b, i: (b, jnp.maximum(i - 1, 0), 0)),
            pl.BlockSpec(w_pool.shape, lambda b, i: (0, 0, 0)),
            pl.BlockSpec((1, c), vec),
            pl.BlockSpec((1, c), vec),
            pl.BlockSpec((1, c), vec),
        ],
        out_specs=pl.BlockSpec((tm, c), lambda b, i: (b * ns + i, 0)),
        out_shape=jax.ShapeDtypeStruct((batch * seq, c), BF16),
        scratch_shapes=[pltpu.VMEM((tm, c), F32)],
        compiler_params=_cparams(2),
        name="pool",
    )(u3, u3, w_pool, b_pool, pool_scale, gn_pool)


def _outproj_kernel(mn_ref, mp_ref, x_ref, w_ref, ln_ref, wr_ref, br_ref,
                    x1_ref, h2_ref, eid_ref, gate_ref):
    tm, d = x_ref.shape
    half = mn_ref.shape[1]
    chunk = 512
    mn = mn_ref[...]
    mp = mp_ref[...]
    ssq = jnp.zeros((tm, 1), F32)
    for c in range(d // chunk):
        sl = slice(c * chunk, (c + 1) * chunk)
        x1 = x_ref[:, sl] + (_dot(mn, w_ref[:half, sl]) + _dot(mp, w_ref[half:, sl]))
        x1_ref[:, sl] = x1
        ssq = ssq + jnp.sum(x1 * x1, axis=-1, keepdims=True)
    h2 = x1_ref[...] * lax.rsqrt(ssq * (1.0 / d) + EPS) * ln_ref[...]
    h2_ref[...] = _pack_bf16_halves(h2)

    logits = _dot_nt(wr_ref[...], h2.astype(BF16)) + br_ref[:, 0:1]
    ng, ne = N_EXPERT_GROUPS, EXPERTS_PER_GROUP
    row = lax.broadcasted_iota(I32, (ne, tm), 0)
    lg = jnp.where(row < ng, logits[0:ne], NEG)
    mg = jnp.max(lg, axis=0, keepdims=True)
    g_sel = jnp.min(jnp.where(lg == mg, row, ne), axis=0, keepdims=True)
    p_g = 1.0 / jnp.sum(jnp.exp(lg - mg), axis=0, keepdims=True)
    le = jnp.zeros((ne, tm), F32)
    for g in range(ng):
        le = jnp.where(g_sel == g, logits[ne * (g + 1):ne * (g + 2)], le)
    ex = jnp.exp(le - jnp.max(le, axis=0, keepdims=True))
    pe = ex / jnp.sum(ex, axis=0, keepdims=True)
    p1 = jnp.max(pe, axis=0, keepdims=True)
    i1 = jnp.min(jnp.where(pe == p1, row, ne), axis=0, keepdims=True)
    rest = jnp.where(row == i1, -1.0, pe)
    p2 = jnp.max(rest, axis=0, keepdims=True)
    i2 = jnp.min(jnp.where(rest == p2, row, ne), axis=0, keepdims=True)
    psum = p1 + p2
    zero_i = jnp.zeros((ne, tm), I32)
    eid = jnp.where(row == 0, g_sel * ne + i1, jnp.where(row == 1, g_sel * ne + i2, zero_i))
    gate = jnp.where(row == 0, p_g * p1 / psum, jnp.where(row == 1, p_g * p2 / psum, 0.0))
    eid_ref[...] = eid
    gate_ref[...] = gate


def _outproj(mix_nsa, mix_pool, x2d, w_out, ln_moe, w_router_t, b_router):
    n, d = x2d.shape
    tm = TM_PROJ
    half = mix_nsa.shape[1]
    const = lambda i: (0, 0)
    return pl.pallas_call(
        _outproj_kernel,
        grid=(n // tm,),
        in_specs=[
            pl.BlockSpec((tm, half), lambda i: (i, 0)),
            pl.BlockSpec((tm, half), lambda i: (i, 0)),
            pl.BlockSpec((tm, d), lambda i: (i, 0)),
            pl.BlockSpec(w_out.shape, const, pipeline_mode=pl.Buffered(1)),
            pl.BlockSpec((1, d), const),
            pl.BlockSpec(w_router_t.shape, const),
            pl.BlockSpec(b_router.shape, const),
        ],
        out_specs=[
            pl.BlockSpec((tm, d), lambda i: (i, 0)),
            pl.BlockSpec((tm, d // 2), lambda i: (i, 0)),
            pl.BlockSpec((SUBLANES, tm), lambda i: (0, i)),
            pl.BlockSpec((SUBLANES, tm), lambda i: (0, i)),
        ],
        out_shape=[
            jax.ShapeDtypeStruct((n, d), F32),
            jax.ShapeDtypeStruct((n, d // 2), U32),
            jax.ShapeDtypeStruct((SUBLANES, n), I32),
            jax.ShapeDtypeStruct((SUBLANES, n), F32),
        ],
        compiler_params=_cparams(1),
        name="outproj",
    )(mix_nsa, mix_pool, x2d, w_out, ln_moe, w_router_t, b_router)


def _dispatch_kernel(eid_ref, dest_ref, meta_ref, rank_scr):
    n = eid_ref.shape[1]
    t = T_RANK
    row = lax.broadcasted_iota(I32, (N_EXPERTS, t), 0)
    before = (lax.broadcasted_iota(I32, (t, t), 0) < lax.broadcasted_iota(I32, (t, t), 1)).astype(BF16)
    carry = jnp.zeros((N_EXPERTS, 1), F32)
    for k in range(2):
        for j in range(n // t):
            sl = slice(j * t, (j + 1) * t)
            oh = row == eid_ref[k:k + 1, sl]
            ohf = oh.astype(F32)
            prior = _dot(ohf.astype(BF16), before) + carry
            rank_scr[k:k + 1, sl] = jnp.sum(jnp.where(oh, prior, 0.0), axis=0, keepdims=True)
            carry = carry + jnp.sum(ohf, axis=1, keepdims=True)
    counts = jnp.broadcast_to(carry, (N_EXPERTS, LANES))
    padded = jnp.floor((counts + (TM_MOE - 1.0)) * (1.0 / TM_MOE)) * TM_MOE
    ends = padded
    r_idx = lax.broadcasted_iota(I32, (N_EXPERTS, LANES), 0)
    sh = 1
    while sh < N_EXPERTS:
        ends = ends + jnp.where(r_idx >= sh, pltpu.roll(ends, sh, 0), 0.0)
        sh *= 2
    starts = (ends - padded)[:, 0:1]
    for k in range(2):
        for j in range(n // t):
            sl = slice(j * t, (j + 1) * t)
            oh = row == eid_ref[k:k + 1, sl]
            base = jnp.sum(jnp.where(oh, starts, 0.0), axis=0, keepdims=True)
            dest_ref[k:k + 1, sl] = (rank_scr[k:k + 1, sl] + base).astype(I32)
    for k in range(2, SUBLANES):
        dest_ref[k:k + 1, :] = jnp.zeros((1, n), I32)
    blk_start = lax.broadcasted_iota(I32, (N_EXPERTS, LANES), 1).astype(F32) * TM_MOE
    blk_expert = jnp.sum((ends <= blk_start).astype(F32), axis=0, keepdims=True)
    blk_expert = jnp.minimum(blk_expert, N_EXPERTS - 1.0)
    n_used = ends[N_EXPERTS - 1:N_EXPERTS, :] * (1.0 / TM_MOE)
    on_diag = r_idx == lax.broadcasted_iota(I32, (N_EXPERTS, LANES), 1)
    end_blk = jnp.sum(jnp.where(on_diag, ends, 0.0), axis=0, keepdims=True) * (1.0 / TM_MOE)
    n_blk = jnp.sum(jnp.where(on_diag, padded, 0.0), axis=0, keepdims=True) * (1.0 / TM_MOE)
    r8 = lax.broadcasted_iota(I32, (SUBLANES, LANES), 0)
    meta = jnp.where(r8 == 0, blk_expert,
                     jnp.where(r8 == 1, n_used, jnp.where(r8 == 2, end_blk, jnp.where(r8 == 3, n_blk, 0.0))))
    meta_ref[...] = meta.astype(I32)


def _dispatch(eid):
    n = eid.shape[1]
    return pl.pallas_call(
        _dispatch_kernel,
        out_shape=[jax.ShapeDtypeStruct((SUBLANES, n), I32),
                   jax.ShapeDtypeStruct((SUBLANES, LANES), I32)],
        scratch_shapes=[pltpu.VMEM((SUBLANES, n), F32)],
        compiler_params=pltpu.CompilerParams(vmem_limit_bytes=VMEM_LIMIT_BYTES),
        name="dispatch",
    )(eid)


def _row_copy(src_ref, src_row, dst_ref, dst_row, sem):
    return pltpu.make_async_copy(src_ref.at[pl.ds(src_row, 1)], dst_ref.at[pl.ds(dst_row, 1)], sem)


def _scatter_kernel(dest_ref, meta_ref, h_ref, xs_ref, zero_buf, sem, zsem):
    tm = h_ref.shape[0]
    blk = zero_buf.shape[0]

    @pl.when(pl.program_id(0) == 0)
    def _():
        zero_buf[...] = jnp.zeros(zero_buf.shape, zero_buf.dtype)
        n_used = meta_ref[1, 0]

        def zero_block(b):
            return pltpu.make_async_copy(zero_buf, xs_ref.at[pl.ds(pl.multiple_of(b * blk, blk), blk)], zsem)

        def sweep(act):
            def per_expert(e, c):
                @pl.when(meta_ref[3, e] > 0)
                def _():
                    act(zero_block(meta_ref[2, e] - 1))
                return c

            def per_tail(b, c):
                @pl.when(b >= n_used)
                def _():
                    act(zero_block(b))
                return c

            lax.fori_loop(0, N_EXPERTS, per_expert, 0)
            lax.fori_loop(0, xs_ref.shape[0] // blk, per_tail, 0)

        sweep(lambda cp: cp.start())
        sweep(lambda cp: cp.wait())

    def start(r, c):
        for k in range(2):
            _row_copy(h_ref, r, xs_ref, dest_ref[0, k, r], sem).start()
        return c

    lax.fori_loop(0, tm, start, 0, unroll=DMA_ISSUE_UNROLL)
    for k in range(2):
        pltpu.make_async_copy(h_ref, xs_ref.at[pl.ds(0, tm)], sem).wait()


def _scatter(dest3, meta, h2, cap):
    nsteps, _, tm = dest3.shape
    d = h2.shape[1]
    return pl.pallas_call(
        _scatter_kernel,
        grid=(nsteps,),
        in_specs=[
            pl.BlockSpec((1, 2, tm), lambda i: (i, 0, 0), memory_space=pltpu.SMEM),
            pl.BlockSpec(memory_space=pltpu.SMEM),
            pl.BlockSpec((tm, d), lambda i: (i, 0)),
        ],
        out_specs=pl.BlockSpec(memory_space=pl.ANY),
        out_shape=jax.ShapeDtypeStruct((cap, d), h2.dtype),
        scratch_shapes=[pltpu.VMEM((TM_MOE, d), h2.dtype), pltpu.SemaphoreType.DMA(()),
                        pltpu.SemaphoreType.DMA(())],
        compiler_params=_cparams(1),
        name="scatter",
    )(dest3, meta, h2)


def _experts_kernel(meta_ref, xs_ref, wg_hbm, wu_hbm, wd_hbm, ys_ref,
                    wg_buf, wu_buf, wd_buf, wg16, wu16, wd16, elist_ref, ord_ref, sem):
    c = pl.program_id(0)
    n_used = meta_ref[1, 0]
    e = meta_ref[0, c]
    first = (c == 0) | (e != meta_ref[0, jnp.maximum(c - 1, 0)])

    def fetch(expert, slot):
        return [pltpu.make_async_copy(w.at[expert], buf.at[slot], sem.at[slot, i])
                for i, (w, buf) in enumerate(((wg_hbm, wg_buf), (wu_hbm, wu_buf), (wd_hbm, wd_buf)))]

    @pl.when(c == 0)
    def _():
        def scan(b, cnt):
            eb = meta_ref[0, b]
            take = ((b == 0) | (eb != meta_ref[0, jnp.maximum(b - 1, 0)])) & (b < n_used)

            @pl.when(take)
            def _():
                elist_ref[cnt] = eb

            return cnt + take.astype(I32)

        cnt = lax.fori_loop(0, pl.num_programs(0), scan, 0)
        ord_ref[0] = 0
        ord_ref[1] = cnt
        for k in range(W_SLOTS - 1):
            @pl.when(k < cnt)
            def _(k=k):
                for cp in fetch(elist_ref[k], k):
                    cp.start()

    @pl.when(first & (c < n_used))
    def _():
        @pl.when(c > 0)
        def _():
            ord_ref[0] = ord_ref[0] + 1
        k = ord_ref[0]
        for cp in fetch(e, k % W_SLOTS):
            cp.wait()
        ahead = k + (W_SLOTS - 1)

        @pl.when(ahead < ord_ref[1])
        def _():
            for cp in fetch(elist_ref[jnp.minimum(ahead, N_EXPERTS - 1)], ahead % W_SLOTS):
                cp.start()

    def mlp(weights):
        lo, hi = _unpack_bf16_halves(xs_ref[...])
        lo, hi = lo.astype(BF16), hi.astype(BF16)
        half = lo.shape[1]
        wg, wu, wd = weights

        def proj(w):
            return _dot(lo, w[:half, :]) + _dot(hi, w[half:, :])

        act = (jax.nn.silu(proj(wg)) * proj(wu)).astype(BF16)
        ys_ref[...] = _pack_bf16_halves(_dot(act, wd))

    @pl.when(first & (c < n_used))
    def _():
        slot = ord_ref[0] % W_SLOTS
        cast = [buf[slot].astype(BF16) for buf in (wg_buf, wu_buf, wd_buf)]
        for ref16, w in zip((wg16, wu16, wd16), cast):
            ref16[...] = w
        mlp(cast)

    @pl.when(jnp.logical_not(first) & (c < n_used))
    def _():
        mlp((wg16[...], wu16[...], wd16[...]))

    @pl.when(c >= n_used)
    def _():
        ys_ref[...] = jnp.zeros(ys_ref.shape, U32)


def _experts(meta, xs, w_gate, w_up, w_down):
    cap, dw = xs.shape
    nblk = cap // TM_MOE
    d, ff = w_gate.shape[1:]

    def row_map(c, meta):
        return (jnp.minimum(c, meta[1, 0] - 1), 0)

    hbm = pl.BlockSpec(memory_space=pl.ANY)
    return pl.pallas_call(
        _experts_kernel,
        grid_spec=pltpu.PrefetchScalarGridSpec(
            num_scalar_prefetch=1,
            grid=(nblk,),
            in_specs=[pl.BlockSpec((TM_MOE, dw), row_map), hbm, hbm, hbm],
            out_specs=pl.BlockSpec((TM_MOE, dw), lambda c, meta: (c, 0)),
            scratch_shapes=[
                pltpu.VMEM((W_SLOTS, d, ff), F32), pltpu.VMEM((W_SLOTS, d, ff), F32),
                pltpu.VMEM((W_SLOTS, ff, d), F32),
                pltpu.VMEM((d, ff), BF16), pltpu.VMEM((d, ff), BF16), pltpu.VMEM((ff, d), BF16),
                pltpu.SMEM((N_EXPERTS,), I32),
                pltpu.SMEM((2,), I32),
                pltpu.SemaphoreType.DMA((W_SLOTS, 3)),
            ],
        ),
        out_shape=jax.ShapeDtypeStruct((cap, dw), U32),
        compiler_params=_cparams(1),
        name="experts",
    )(meta, xs, w_gate, w_up, w_down)


def _combine_kernel(dest_ref, dest_next_ref, x1_ref, gate_ref, ln_ref, ys_ref, o_ref, ybuf, sem):
    i = pl.program_id(0)
    tm = x1_ref.shape[0]
    slot = i % 2

    def issue(d_ref, s):
        def start(r, c):
            for k in range(2):
                _row_copy(ys_ref, d_ref[0, k, r], ybuf.at[s, k], r, sem.at[s]).start()
            return c

        lax.fori_loop(0, tm, start, 0, unroll=DMA_ISSUE_UNROLL)

    @pl.when(i == 0)
    def _():
        issue(dest_ref, 0)

    @pl.when(i + 1 < pl.num_programs(0))
    def _():
        issue(dest_next_ref, 1 - slot)

    for k in range(2):
        pltpu.make_async_copy(ys_ref.at[pl.ds(0, tm)], ybuf.at[slot, k], sem.at[slot]).wait()
    gate = gate_ref[...]
    lo0, hi0 = _unpack_bf16_halves(ybuf[slot, 0])
    lo1, hi1 = _unpack_bf16_halves(ybuf[slot, 1])
    g0, g1 = gate[:, 0:1], gate[:, 1:2]
    moe = jnp.concatenate([lo0 * g0 + lo1 * g1, hi0 * g0 + hi1 * g1], axis=1)
    x2 = x1_ref[...] + moe
    ms = jnp.mean(x2 * x2, axis=-1, keepdims=True)
    o_ref[...] = x2 * lax.rsqrt(ms + EPS) * ln_ref[...]


def _combine(dest3, x1, gate_rows, ln_final, ys):
    n, d = x1.shape
    tm = TM_COMBINE
    last = n // tm - 1
    return pl.pallas_call(
        _combine_kernel,
        grid=(n // tm,),
        in_specs=[
            pl.BlockSpec((1, 2, tm), lambda i: (i, 0, 0), memory_space=pltpu.SMEM),
            pl.BlockSpec((1, 2, tm), lambda i: (jnp.minimum(i + 1, last), 0, 0), memory_space=pltpu.SMEM),
            pl.BlockSpec((tm, d), lambda i: (i, 0)),
            pl.BlockSpec((tm, LANES), lambda i: (i, 0)),
            pl.BlockSpec((1, d), lambda i: (0, 0)),
            pl.BlockSpec(memory_space=pl.ANY),
        ],
        out_specs=pl.BlockSpec((tm, d), lambda i: (i, 0)),
        out_shape=jax.ShapeDtypeStruct((n, d), F32),
        scratch_shapes=[pltpu.VMEM((2, 2, tm, d // 2), U32), pltpu.SemaphoreType.DMA((2,))],
        compiler_params=_cparams(1),
        name="combine",
    )(dest3, dest3, x1, gate_rows, ln_final, ys)


def kernel(x, positions, ln_mix, w_in, pe_cmp_k, w_cmp_k1, w_cmp_k2, pe_cmp_v, w_cmp_v1, w_cmp_v2, w_pool, b_pool, pool_scale, gn_nsa, gn_pool, w_out, ln_moe, w_router_group, b_router_group, w_router_expert, b_router_expert, w_gate, w_up, w_down, ln_final):
    batch, seq, d = x.shape
    n = batch * seq
    depth = w_in.shape[0]
    nsa_w = N_HEADS * HEAD_DIM
    kv_w = N_KV * HEAD_DIM
    cut_gates = nsa_w + 6 * kv_w
    cut_pool = cut_gates + N_GATES

    inv_freq = ROPE_THETA ** (-jnp.arange(0, ROT_DIM, 2, dtype=F32) / ROT_DIM)
    ang = positions.astype(F32).reshape(n, 1) * inv_freq[None, :]
    cos, sin = jnp.cos(ang), jnp.sin(ang)
    rest = HEAD_DIM - ROT_DIM
    trig = jnp.stack([jnp.concatenate([cos, cos, jnp.ones((n, rest), F32)], axis=1),
                      jnp.concatenate([-sin, sin, jnp.zeros((n, rest), F32)], axis=1)])
    trigt = jnp.transpose(trig, (0, 2, 1))
    kblk = (jnp.arange(seq, dtype=I32)[:, None] // SEL_LEN == jnp.arange(LANES, dtype=I32)[None, :]).astype(BF16)

    x2d = x.reshape(n, d)
    assert depth == 1, "single-layer operation"
    for l in range(depth):
        wi = w_in[l]
        o_kc, o_ks = nsa_w, nsa_w + 2 * kv_w
        o_vs, o_kw, o_vw = o_ks + kv_w, o_ks + 2 * kv_w, o_ks + 3 * kv_w
        gate_rows_pad = -(-N_GATES // SUBLANES) * SUBLANES
        wq = wi[:, :nsa_w].astype(BF16)
        wv = jnp.concatenate([wi[:, o_vs:o_kw], wi[:, o_vw:cut_gates]], axis=1).astype(BF16)
        wg = jnp.pad(wi[:, cut_gates:cut_pool], ((0, 0), (0, LANES - N_GATES))).astype(BF16)
        wkc = wi[:, o_kc:o_ks].astype(BF16)
        wk = jnp.concatenate([wi[:, o_ks:o_vs], wi[:, o_kw:o_vw]], axis=1).astype(BF16)
        wu = wi[:, cut_pool:].astype(BF16)
        qnt, qrt, kvc, kk, vt, gt, u = _inproj(x2d, ln_mix[l][None], trig, trigt, wq, wv, wg, wkc, wk, wu,
                                               gate_rows_pad, batch, seq)

        half = CMP_LEN * HEAD_DIM // 2
        def stack_w1(w):
            return jnp.concatenate([w[:half], w[half:]], axis=1)
        w1 = jnp.stack([stack_w1(w_cmp_k1[l]), stack_w1(w_cmp_v1[l])]).astype(BF16)
        w2 = jnp.stack([w_cmp_k2[l], w_cmp_v2[l]]).astype(BF16)
        def pe_rows(pe):
            return jnp.pad(pe.reshape(2, half), ((0, SUBLANES - 2), (0, 0)))
        pe2 = jnp.stack([pe_rows(pe_cmp_k[l]), pe_rows(pe_cmp_v[l])]).astype(BF16)
        cmp = _compress(kvc, pe2, w1, w2)

        mix_nsa = _nsa(qnt, qrt, cmp, kk, vt, kblk, gt, gn_nsa[l][None])
        mix_pool = _pool(u.reshape(batch, seq, -1), w_pool[l].astype(BF16), b_pool[l][None],
                         pool_scale[l][None], gn_pool[l][None])

        ng, ne = N_EXPERT_GROUPS, EXPERTS_PER_GROUP
        w_r = jnp.concatenate([
            w_router_group[l].T, jnp.zeros((ne - ng, d), F32),
            jnp.transpose(w_router_expert[l], (0, 2, 1)).reshape(ng * ne, d),
            jnp.zeros((LANES - ne - ng * ne, d), F32)], axis=0).astype(BF16)
        b_r = jnp.concatenate([
            b_router_group[l], jnp.zeros((ne - ng,), F32), b_router_expert[l].reshape(-1),
            jnp.zeros((LANES - ne - ng * ne,), F32)])
        b_r = jnp.broadcast_to(b_r[:, None], (LANES, LANES))
        x1, h2, eid, gate = _outproj(mix_nsa, mix_pool, x2d, w_out[l].astype(BF16), ln_moe[l][None], w_r, b_r)

        dest, meta = _dispatch(eid)
        cap = (2 * n // TM_MOE + N_EXPERTS) * TM_MOE
        def per_tile(tm):
            return jnp.transpose(dest[:2].reshape(2, n // tm, tm), (1, 0, 2))
        xs = _scatter(per_tile(TS_SCATTER), meta, h2, cap)
        ys = _experts(meta, xs, w_gate[l], w_up[l], w_down[l])
        dest_tok = per_tile(TM_COMBINE)
        gate_rows = jnp.pad(gate[:2].T, ((0, 0), (0, LANES - 2)))
        x2d = _combine(dest_tok, x1, gate_rows, ln_final[None], ys)
    return x2d.reshape(batch, seq, d)
```

```python
import functools

import jax
import jax.numpy as jnp
from jax import lax
from jax.experimental import pallas as pl
from jax.experimental.pallas import tpu as pltpu

F32 = jnp.float32
BF16 = jnp.bfloat16
I32 = jnp.int32

HEAD_DIM = 128
N_HEADS = 8
N_KV = 2
GQA = N_HEADS // N_KV
N_GATES = 3 * N_HEADS
ROT_DIM = HEAD_DIM // 4
ROPE_THETA = 500000.0
CMP_LEN = 32
CMP_STRIDE = 16
CMP_HIDDEN = 2 * HEAD_DIM
SEL_LEN = 64
SEL_TOPK = 16
N_LOCAL = 2
WINDOW = 512
POOL_SIZES = (2, 4, 8, 16)
N_EXPERT_GROUPS = 4
EXPERTS_PER_GROUP = 8
N_EXPERTS = N_EXPERT_GROUPS * EXPERTS_PER_GROUP
EPS = 1e-6
NEG = -1e30
BIG = 1e30
SCALE = HEAD_DIM ** -0.5
LOG2E = 1.4426950408889634
Q_SCALE = SCALE * LOG2E

LANES = 128
SUBLANES = 8
VMEM_LIMIT_BYTES = 56 * 1024 * 1024

TM_PROJ = 256
TQ = 256
TK = 256
SEL_MASK_BIAS = 32768.0
V_ONES_ROWS = 16
TM_MOE = 256
TS_SCATTER = 512
TM_COMBINE = 128
W_SLOTS = 3
DMA_ISSUE_UNROLL = 8
T_RANK = 512


def _cparams(n_axes, flags=None):
    return pltpu.CompilerParams(dimension_semantics=("arbitrary",) * n_axes,
                                vmem_limit_bytes=VMEM_LIMIT_BYTES, flags=flags)


def _dot(a, b):
    return jnp.dot(a, b, preferred_element_type=F32)


def _dot_nt(a, b):
    return lax.dot_general(a, b, (((1,), (1,)), ((), ())), preferred_element_type=F32)


U32 = jnp.uint32
_HI16 = 0xFFFF0000


def _pack_bf16_halves(x):
    c = x.shape[1] // 2
    bits = lambda t: lax.bitcast_convert_type(t.astype(BF16).astype(F32), U32)
    return lax.shift_right_logical(bits(x[:, :c]), U32(16)) | (bits(x[:, c:]) & U32(_HI16))


def _store_token_tiles(ref, words):
    m, width = words.shape
    assert width == SUBLANES * LANES
    for j in range(SUBLANES):
        ref[pl.ds(j, m, stride=SUBLANES), :] = words[:, j * LANES:(j + 1) * LANES]


def _load_token_tiles(ref):
    m = ref.shape[0] // SUBLANES
    return jnp.concatenate([ref[pl.ds(j, m, stride=SUBLANES), :] for j in range(SUBLANES)], axis=1)


def _unpack_bf16_halves(w):
    lo = lax.bitcast_convert_type(lax.shift_left(w, U32(16)), F32)
    hi = lax.bitcast_convert_type(w & U32(_HI16), F32)
    return lo, hi


def _rope(t, cos, sin, axis):
    idx = lax.broadcasted_iota(I32, t.shape, axis)
    partner = jnp.where(idx < ROT_DIM // 2,
                        pltpu.roll(t, HEAD_DIM - ROT_DIM // 2, axis),
                        pltpu.roll(t, ROT_DIM // 2, axis))
    return t * cos + partner * sin


def _inproj_kernel(x_ref, ln_ref, trig_ref, trigt_ref, wq_ref, wv_ref, wg_ref, wkc_ref, wk_ref, wu_ref,
                   qnt_ref, qrt_ref, kvc_ref, kk_ref, vt_ref, gt_ref, u_ref, kc_scr):
    x = x_ref[...]
    ms = jnp.mean(x * x, axis=-1, keepdims=True)
    hb = (x * lax.rsqrt(ms + EPS) * ln_ref[...]).astype(BF16)
    chunk = 4 * HEAD_DIM

    cost, sint = trigt_ref[0], trigt_ref[1]
    for ch in range(2):
        r = _dot(hb, wq_ref[:, ch * chunk:(ch + 1) * chunk])
        for j in range(4):
            t = r[:, j * HEAD_DIM:(j + 1) * HEAD_DIM].T
            qnt_ref[0, ch * 4 + j] = (t * Q_SCALE).astype(BF16)
            qrt_ref[0, ch * 4 + j] = (_rope(t, cost, sint, 0) * Q_SCALE).astype(BF16)
    r = _dot(hb, wv_ref[...])
    ones = jnp.ones((V_ONES_ROWS, r.shape[0]), BF16)
    for j in range(4):
        vt_ref[0, j, 0] = jnp.concatenate([r[:, j * HEAD_DIM:(j + 1) * HEAD_DIM].T.astype(BF16), ones], axis=0)
    gt_ref[...] = _dot(hb, wg_ref[...]).T[:gt_ref.shape[0], :]

    r = _dot(hb, wkc_ref[...])
    groups = kvc_ref.shape[2]
    for j in range(4):
        kc_scr[j] = r[:, j * HEAD_DIM:(j + 1) * HEAD_DIM]
        for i in range(CMP_STRIDE):
            kvc_ref[0, j, :, i * HEAD_DIM:(i + 1) * HEAD_DIM] = (
                kc_scr[j, pl.ds(i, groups, stride=CMP_STRIDE), :].astype(BF16))
    cos, sin = trig_ref[0], trig_ref[1]
    r = _dot(hb, wk_ref[...])
    for j in range(4):
        kk_ref[0, j] = _rope(r[:, j * HEAD_DIM:(j + 1) * HEAD_DIM], cos, sin, 1).astype(BF16)
    for ch in range(2):
        u_ref[:, ch * chunk:(ch + 1) * chunk] = _dot(hb, wu_ref[:, ch * chunk:(ch + 1) * chunk])


def _inproj(x2d, ln, trig, trigt, wq, wv, wg, wkc, wk, wu, gate_rows, batch, seq):
    n, d = x2d.shape
    tm = TM_PROJ
    assert tm == TK
    spb = seq // tm
    const = lambda i: (0, 0)
    resident = lambda w: pl.BlockSpec(w.shape, const, pipeline_mode=pl.Buffered(1))
    row_heads = lambda i: (i // spb, 0, i % spb, 0)
    col_heads = lambda i: (i // spb, 0, 0, i % spb)
    return pl.pallas_call(
        _inproj_kernel,
        grid=(n // tm,),
        in_specs=[
            pl.BlockSpec((tm, d), lambda i: (i, 0)),
            pl.BlockSpec((1, d), const),
            pl.BlockSpec((2, tm, HEAD_DIM), lambda i: (0, i, 0)),
            pl.BlockSpec((2, HEAD_DIM, tm), lambda i: (0, 0, i)),
            resident(wq), resident(wv), resident(wg), resident(wkc), resident(wk), resident(wu),
        ],
        out_specs=[
            pl.BlockSpec((1, N_HEADS, HEAD_DIM, tm), col_heads),
            pl.BlockSpec((1, N_HEADS, HEAD_DIM, tm), col_heads),
            pl.BlockSpec((1, 4, tm // CMP_STRIDE, CMP_STRIDE * HEAD_DIM), row_heads),
            pl.BlockSpec((1, 4, tm, HEAD_DIM), row_heads),
            pl.BlockSpec((1, 4, 1, HEAD_DIM + V_ONES_ROWS, tm), lambda i: (i // spb, 0, i % spb, 0, 0)),
            pl.BlockSpec((gate_rows, tm), lambda i: (0, i)),
            pl.BlockSpec((tm, wu.shape[1]), lambda i: (i, 0)),
        ],
        out_shape=[
            jax.ShapeDtypeStruct((batch, N_HEADS, HEAD_DIM, seq), BF16),
            jax.ShapeDtypeStruct((batch, N_HEADS, HEAD_DIM, seq), BF16),
            jax.ShapeDtypeStruct((batch, 4, seq // CMP_STRIDE, CMP_STRIDE * HEAD_DIM), BF16),
            jax.ShapeDtypeStruct((batch, 4, seq, HEAD_DIM), BF16),
            jax.ShapeDtypeStruct((batch, 4, spb, HEAD_DIM + V_ONES_ROWS, tm), BF16),
            jax.ShapeDtypeStruct((gate_rows, n), F32),
            jax.ShapeDtypeStruct((n, wu.shape[1]), F32),
        ],
        scratch_shapes=[pltpu.VMEM((4, tm, HEAD_DIM), F32)],
        compiler_params=_cparams(1),
        name="inproj",
    )(x2d, ln, trig, trigt, wq, wv, wg, wkc, wk, wu)


def _compress_kernel(t_ref, pe_ref, w1_ref, w2_ref, o_ref):
    t2 = t_ref[0, 0]
    w1 = w1_ref[0]
    a = _dot(t2, w1)
    pb = _dot(pe_ref[0], w1)
    bias = pb[0:1, :CMP_HIDDEN] + pb[1:2, CMP_HIDDEN:]
    nrow = a.shape[0]
    nxt = pltpu.roll(a[:, CMP_HIDDEN:], nrow - 1, 0)
    hid = jax.nn.gelu(a[:, :CMP_HIDDEN] + nxt + bias)
    out = _dot(hid.astype(BF16), w2_ref[0])
    row = lax.broadcasted_iota(I32, out.shape, 0)
    out = jnp.where(row < nrow - 1, out, 0.0)
    is_value = pl.program_id(1) >= N_KV
    o_ref[0, 0] = jnp.where(is_value, out.T, out).astype(BF16)


def _compress(kvc2, pe2, w1, w2):
    batch, four, nrow, width = kvc2.shape
    return pl.pallas_call(
        _compress_kernel,
        grid=(batch, four),
        in_specs=[
            pl.BlockSpec((1, 1, nrow, width), lambda b, j: (b, j, 0, 0)),
            pl.BlockSpec((1, SUBLANES, width), lambda b, j: (j // N_KV, 0, 0)),
            pl.BlockSpec((1, width, 2 * CMP_HIDDEN), lambda b, j: (j // N_KV, 0, 0)),
            pl.BlockSpec((1, CMP_HIDDEN, HEAD_DIM), lambda b, j: (j // N_KV, 0, 0)),
        ],
        out_specs=pl.BlockSpec((1, 1, nrow, HEAD_DIM), lambda b, j: (b, j, 0, 0)),
        out_shape=jax.ShapeDtypeStruct((batch, four, nrow, HEAD_DIM), BF16),
        compiler_params=_cparams(2),
        name="compress",
    )(kvc2, pe2, w1, w2)


def _nsa_kernel(qnt_ref, qrt_ref, cmp_ref, kk_ref, vt_ref, kblk_ref, gt_ref, gn_ref, o_ref,
                m_scr, acc_scr, qa_scr, ocmp_scr, s_scr, o_scr):
    qi = pl.program_id(1)
    tq = qnt_ref.shape[3]
    rows = GQA * tq
    n_cmp_pad = cmp_ref.shape[2]
    n_sel = kk_ref.shape[2] // SEL_LEN
    q0 = qi * tq

    m_scr[...] = jnp.full(m_scr.shape, NEG, F32)
    acc_scr[...] = jnp.zeros(acc_scr.shape, F32)

    def scores(item, slot):
        c, g, q_rows, k_tile, v_t, bias = item
        s_scr[slot] = _dot(k_tile, qa_scr[g, :q_rows, :])

    def softmax_pv(item, slot):
        c, g, q_rows, k_tile, v_t, bias = item
        s = s_scr[slot]
        if bias is not None:
            s = s + bias
        m_old = m_scr[c]
        m_new = jnp.maximum(m_old, jnp.max(s, axis=0, keepdims=True))
        alpha = jnp.exp2(m_old - m_new)
        p = jnp.exp2(s - m_new)
        acc_scr[c] = alpha * acc_scr[c] + _dot(v_t, p.astype(BF16))
        m_scr[c] = m_new

    def flash_out(c):
        return acc_scr[c, :HEAD_DIM, :] / acc_scr[c, HEAD_DIM:HEAD_DIM + 1, :]

    def tile_heads(a):
        return jnp.concatenate([a] * GQA, axis=1)

    kk_i = lax.broadcasted_iota(I32, (TK, tq), 0)
    qq_i = lax.broadcasted_iota(I32, (TK, tq), 1)
    causal_bias = tile_heads(jnp.where(kk_i <= qq_i, 0.0, NEG))
    band_bias = tile_heads(jnp.where(kk_i > qq_i, 0.0, NEG))
    n_win = WINDOW // TK

    for g in range(N_KV):
        qn4 = jnp.concatenate([qnt_ref[0, h] for h in range(g * GQA, (g + 1) * GQA)], axis=1)
        s = _dot(cmp_ref[0, g], qn4)
        n_idx = lax.broadcasted_iota(I32, (n_cmp_pad, rows), 0)
        t_idx = q0 + jnp.bitwise_and(lax.broadcasted_iota(I32, (n_cmp_pad, rows), 1), tq - 1)
        cmp_ok = n_idx * CMP_STRIDE + (CMP_LEN - 1) <= t_idx
        s = jnp.where(cmp_ok, s, NEG)
        m = jnp.max(s, axis=0, keepdims=True)
        e = jnp.where(cmp_ok, jnp.exp2(s - m), 0.0)
        l = jnp.sum(e, axis=0, keepdims=True)
        p = e * jnp.where(l > 0.0, 1.0 / l, 0.0)
        pb = p.astype(BF16)
        ocmp_scr[g] = _dot(cmp_ref[0, N_KV + g], pb)

        jn = lax.broadcasted_iota(I32, (n_sel, n_cmp_pad), 0) * SEL_LEN
        cn = lax.broadcasted_iota(I32, (n_sel, n_cmp_pad), 1) * CMP_STRIDE
        ov = jnp.clip(jnp.minimum(cn + CMP_LEN, jn + SEL_LEN) - jnp.maximum(cn, jn), 0, None)
        w_t = (ov.astype(F32) * (1.0 / CMP_LEN)).astype(BF16)
        imp4 = _dot(w_t, pb)
        imp = imp4[:, 0:tq]
        for r in range(1, GQA):
            imp = imp + imp4[:, r * tq:(r + 1) * tq]
        j_blk = lax.broadcasted_iota(I32, (n_sel, tq), 0)
        t_q = q0 + lax.broadcasted_iota(I32, (n_sel, tq), 1)
        sel_ok = j_blk * SEL_LEN <= t_q
        back = t_q // SEL_LEN - j_blk
        forced = (j_blk == 0) | ((back >= 0) & (back < N_LOCAL))
        val = jnp.where(sel_ok & forced, BIG, jnp.where(sel_ok, imp, -BIG))
        rank = jnp.zeros((n_sel, tq), F32)
        for i in range(n_sel):
            vi = val[i:i + 1, :]
            beats = (vi > val) | ((vi == val) & (j_blk > i))
            rank = rank + beats.astype(F32)
        keep = (rank < float(min(SEL_TOPK, n_sel))) & sel_ok
        bias_t = jnp.where(keep, 0.0, -SEL_MASK_BIAS)
        bias_t = jnp.concatenate([bias_t, jnp.zeros((HEAD_DIM - n_sel, tq), F32)], axis=0)
        qa_scr[g, :HEAD_DIM, :] = jnp.concatenate(
            [qrt_ref[0, h] for h in range(g * GQA, (g + 1) * GQA)], axis=1)
        qa_scr[g, HEAD_DIM:, :] = tile_heads(bias_t.astype(BF16))

    def sel_item(g, kj, bias):
        st = pl.multiple_of(kj * TK, TK)
        k_aug = jnp.concatenate([kk_ref[0, g, pl.ds(st, TK), :], kblk_ref[pl.ds(st, TK), :]], axis=1)
        return (g, g, 2 * HEAD_DIM, k_aug, vt_ref[0, g, kj], bias)

    def win_item(g, back):
        kj = jnp.maximum(qi - back, 0)
        st = pl.multiple_of(kj * TK, TK)
        off = jnp.where(qi >= back, 0.0, NEG)
        bias = causal_bias if back == 0 else (band_bias + off if back == n_win else off)
        return (N_KV + g, g, HEAD_DIM, kk_ref[0, N_KV + g, pl.ds(st, TK), :], vt_ref[0, N_KV + g, kj], bias)

    assert N_KV == 2
    scores(sel_item(0, 0, None), 0)

    def sel_body(kj, carry):
        scores(sel_item(1, kj, None), 1)
        softmax_pv(sel_item(0, kj, None), 0)
        scores(sel_item(0, kj + 1, None), 0)
        softmax_pv(sel_item(1, kj, None), 1)
        return carry

    lax.fori_loop(0, qi, sel_body, 0)
    tail = ([sel_item(g, qi, causal_bias) for g in range(N_KV)]
            + [win_item(g, back) for back in range(n_win, -1, -1) for g in range(N_KV)])
    for i, item in enumerate(tail):
        if i + 1 < len(tail):
            scores(tail[i + 1], (i + 1) % 2)
        softmax_pv(item, i % 2)

    gsig = jax.nn.sigmoid(gt_ref[...])
    for g in range(N_KV):
        o_cmp, o_sel, o_win = ocmp_scr[g], flash_out(g), flash_out(N_KV + g)
        for r in range(GQA):
            h = g * GQA + r
            sl = slice(r * tq, (r + 1) * tq)
            o_h = (gsig[3 * h:3 * h + 1, :] * o_cmp[:, sl]
                   + gsig[3 * h + 1:3 * h + 2, :] * o_sel[:, sl]
                   + gsig[3 * h + 2:3 * h + 3, :] * o_win[:, sl])
            o_scr[:, h * HEAD_DIM:(h + 1) * HEAD_DIM] = o_h.T

    o = o_scr[...]
    ms = jnp.mean(o * o, axis=-1, keepdims=True)
    o_ref[...] = (o * lax.rsqrt(ms + EPS) * gn_ref[...]).astype(BF16)


def _nsa(qnt, qrt, cmp, kk, vt, kblk, gt, gn):
    batch, _, _, seq = qnt.shape
    assert TQ == TK and WINDOW % TK == 0 and TQ & (TQ - 1) == 0
    nq = seq // TQ
    width = N_HEADS * HEAD_DIM
    return pl.pallas_call(
        _nsa_kernel,
        grid=(batch, nq),
        in_specs=[
            pl.BlockSpec((1, N_HEADS, HEAD_DIM, TQ), lambda b, i: (b, 0, 0, i)),
            pl.BlockSpec((1, N_HEADS, HEAD_DIM, TQ), lambda b, i: (b, 0, 0, i)),
            pl.BlockSpec((1,) + cmp.shape[1:], lambda b, i: (b, 0, 0, 0)),
            pl.BlockSpec((1,) + kk.shape[1:], lambda b, i: (b, 0, 0, 0)),
            pl.BlockSpec((1,) + vt.shape[1:], lambda b, i: (b, 0, 0, 0, 0)),
            pl.BlockSpec((seq, LANES), lambda b, i: (0, 0)),
            pl.BlockSpec((gt.shape[0], TQ), lambda b, i: (0, b * nq + i)),
            pl.BlockSpec((1, width), lambda b, i: (0, 0)),
        ],
        out_specs=pl.BlockSpec((TQ, width), lambda b, i: (b * nq + i, 0)),
        out_shape=jax.ShapeDtypeStruct((batch * seq, width), BF16),
        scratch_shapes=[
            pltpu.VMEM((2 * N_KV, 1, GQA * TQ), F32),
            pltpu.VMEM((2 * N_KV, vt.shape[3], GQA * TQ), F32),
            pltpu.VMEM((N_KV, 2 * HEAD_DIM, GQA * TQ), BF16),
            pltpu.VMEM((N_KV, HEAD_DIM, GQA * TQ), F32),
            pltpu.VMEM((2, TK, GQA * TQ), F32),
            pltpu.VMEM((TQ, width), F32),
        ],
        compiler_params=_cparams(2),
        name="nsa",
    )(qnt, qrt, cmp, kk, vt, kblk, gt, gn)


def _pool_kernel(u_ref, up_ref, w_ref, b_ref, sc_ref, gn_ref, o_ref, y_scr):
    si = pl.program_id(1)
    tm = u_ref.shape[1]
    halo = max(POOL_SIZES)
    cur = u_ref[0]
    prev = jnp.where(si > 0, up_ref[0, tm - halo:, :], 0.0)
    ext = jnp.concatenate([prev, cur], axis=0)
    t1 = (si * tm + 1 + lax.broadcasted_iota(I32, (tm, 1), 0)).astype(F32)
    cg = cur.shape[1] // len(POOL_SIZES)
    for gi, w in enumerate(POOL_SIZES):
        sl = slice(gi * cg, (gi + 1) * cg)
        acc = ext[:, sl]
        span = 1
        while span < w:
            acc = acc + jnp.concatenate([jnp.zeros((span, cg), F32), acc[:-span]], axis=0)
            span *= 2
        mean = acc[halo:] / jnp.minimum(t1, float(w))
        d = (mean - cur[:, sl]).astype(BF16)
        y = _dot(d, w_ref[gi]) + b_ref[:, sl]
        y_scr[:, sl] = y * sc_ref[:, sl]
    y = y_scr[...]
    ms = jnp.mean(y * y, axis=-1, keepdims=True)
    o_ref[...] = (y * lax.rsqrt(ms + EPS) * gn_ref[...]).astype(BF16)


def _pool(u3, w_pool, b_pool, pool_scale, gn_pool):
    batch, seq, c = u3.shape
    tm = TM_PROJ
    ns = seq // tm
    vec = lambda b, i: (0, 0)
    return pl.pallas_call(
        _pool_kernel,
        grid=(batch, ns),
        in_specs=[
            pl.BlockSpec((1, tm, c), lambda b, i: (b, i, 0)),
            pl.BlockSpec((1, tm, c), lambda b, i: (b, jnp.maximum(i - 1, 0), 0)),
            pl.BlockSpec(w_pool.shape, lambda b, i: (0, 0, 0)),
            pl.BlockSpec((1, c), vec),
            pl.BlockSpec((1, c), vec),
            pl.BlockSpec((1, c), vec),
        ],
        out_specs=pl.BlockSpec((tm, c), lambda b, i: (b * ns + i, 0)),
        out_shape=jax.ShapeDtypeStruct((batch * seq, c), BF16),
        scratch_shapes=[pltpu.VMEM((tm, c), F32)],
        compiler_params=_cparams(2),
        name="pool",
    )(u3, u3, w_pool, b_pool, pool_scale, gn_pool)


def _outproj_kernel(mn_ref, mp_ref, x_ref, w_ref, ln_ref, wr_ref, br_ref,
                    x1_ref, h2_ref, eid_ref, gate_ref):
    tm, d = x_ref.shape
    half = mn_ref.shape[1]
    chunk = 512
    mn = mn_ref[...]
    mp = mp_ref[...]
    ssq = jnp.zeros((tm, 1), F32)
    for c in range(d // chunk):
        sl = slice(c * chunk, (c + 1) * chunk)
        x1 = x_ref[:, sl] + (_dot(mn, w_ref[:half, sl]) + _dot(mp, w_ref[half:, sl]))
        x1_ref[:, sl] = x1
        ssq = ssq + jnp.sum(x1 * x1, axis=-1, keepdims=True)
    h2 = x1_ref[...] * lax.rsqrt(ssq * (1.0 / d) + EPS) * ln_ref[...]
    _store_token_tiles(h2_ref, _pack_bf16_halves(h2))

    logits = _dot_nt(wr_ref[...], h2.astype(BF16)) + br_ref[:, 0:1]
    ng, ne = N_EXPERT_GROUPS, EXPERTS_PER_GROUP
    row = lax.broadcasted_iota(I32, (ne, tm), 0)
    lg = jnp.where(row < ng, logits[0:ne], NEG)
    mg = jnp.max(lg, axis=0, keepdims=True)
    g_sel = jnp.min(jnp.where(lg == mg, row, ne), axis=0, keepdims=True)
    p_g = 1.0 / jnp.sum(jnp.exp(lg - mg), axis=0, keepdims=True)
    le = jnp.zeros((ne, tm), F32)
    for g in range(ng):
        le = jnp.where(g_sel == g, logits[ne * (g + 1):ne * (g + 2)], le)
    ex = jnp.exp(le - jnp.max(le, axis=0, keepdims=True))
    pe = ex / jnp.sum(ex, axis=0, keepdims=True)
    p1 = jnp.max(pe, axis=0, keepdims=True)
    i1 = jnp.min(jnp.where(pe == p1, row, ne), axis=0, keepdims=True)
    rest = jnp.where(row == i1, -1.0, pe)
    p2 = jnp.max(rest, axis=0, keepdims=True)
    i2 = jnp.min(jnp.where(rest == p2, row, ne), axis=0, keepdims=True)
    psum = p1 + p2
    zero_i = jnp.zeros((ne, tm), I32)
    eid = jnp.where(row == 0, g_sel * ne + i1, jnp.where(row == 1, g_sel * ne + i2, zero_i))
    gate = jnp.where(row == 0, p_g * p1 / psum, jnp.where(row == 1, p_g * p2 / psum, 0.0))
    eid_ref[...] = eid
    gate_ref[...] = gate


def _outproj(mix_nsa, mix_pool, x2d, w_out, ln_moe, w_router_t, b_router):
    n, d = x2d.shape
    tm = TM_PROJ
    half = mix_nsa.shape[1]
    const = lambda i: (0, 0)
    return pl.pallas_call(
        _outproj_kernel,
        grid=(n // tm,),
        in_specs=[
            pl.BlockSpec((tm, half), lambda i: (i, 0)),
            pl.BlockSpec((tm, half), lambda i: (i, 0)),
            pl.BlockSpec((tm, d), lambda i: (i, 0)),
            pl.BlockSpec(w_out.shape, const, pipeline_mode=pl.Buffered(1)),
            pl.BlockSpec((1, d), const),
            pl.BlockSpec(w_router_t.shape, const),
            pl.BlockSpec(b_router.shape, const),
        ],
        out_specs=[
            pl.BlockSpec((tm, d), lambda i: (i, 0)),
            pl.BlockSpec((tm * SUBLANES, LANES), lambda i: (i, 0)),
            pl.BlockSpec((SUBLANES, tm), lambda i: (0, i)),
            pl.BlockSpec((SUBLANES, tm), lambda i: (0, i)),
        ],
        out_shape=[
            jax.ShapeDtypeStruct((n, d), F32),
            jax.ShapeDtypeStruct((n * SUBLANES, LANES), U32),
            jax.ShapeDtypeStruct((SUBLANES, n), I32),
            jax.ShapeDtypeStruct((SUBLANES, n), F32),
        ],
        compiler_params=_cparams(1),
        name="outproj",
    )(mix_nsa, mix_pool, x2d, w_out, ln_moe, w_router_t, b_router)


def _dispatch_kernel(eid_ref, dest_ref, meta_ref, rank_scr):
    n = eid_ref.shape[1]
    t = T_RANK
    row = lax.broadcasted_iota(I32, (N_EXPERTS, t), 0)
    before = (lax.broadcasted_iota(I32, (t, t), 0) < lax.broadcasted_iota(I32, (t, t), 1)).astype(BF16)
    carry = jnp.zeros((N_EXPERTS, 1), F32)
    for k in range(2):
        for j in range(n // t):
            sl = slice(j * t, (j + 1) * t)
            oh = row == eid_ref[k:k + 1, sl]
            ohf = oh.astype(F32)
            prior = _dot(ohf.astype(BF16), before) + carry
            rank_scr[k:k + 1, sl] = jnp.sum(jnp.where(oh, prior, 0.0), axis=0, keepdims=True)
            carry = carry + jnp.sum(ohf, axis=1, keepdims=True)
    counts = jnp.broadcast_to(carry, (N_EXPERTS, LANES))
    padded = jnp.floor((counts + (TM_MOE - 1.0)) * (1.0 / TM_MOE)) * TM_MOE
    ends = padded
    r_idx = lax.broadcasted_iota(I32, (N_EXPERTS, LANES), 0)
    sh = 1
    while sh < N_EXPERTS:
        ends = ends + jnp.where(r_idx >= sh, pltpu.roll(ends, sh, 0), 0.0)
        sh *= 2
    starts = (ends - padded)[:, 0:1]
    for k in range(2):
        for j in range(n // t):
            sl = slice(j * t, (j + 1) * t)
            oh = row == eid_ref[k:k + 1, sl]
            base = jnp.sum(jnp.where(oh, starts, 0.0), axis=0, keepdims=True)
            dest_ref[k:k + 1, sl] = (rank_scr[k:k + 1, sl] + base).astype(I32)
    for k in range(2, SUBLANES):
        dest_ref[k:k + 1, :] = jnp.zeros((1, n), I32)
    blk_start = lax.broadcasted_iota(I32, (N_EXPERTS, LANES), 1).astype(F32) * TM_MOE
    blk_expert = jnp.sum((ends <= blk_start).astype(F32), axis=0, keepdims=True)
    blk_expert = jnp.minimum(blk_expert, N_EXPERTS - 1.0)
    n_used = ends[N_EXPERTS - 1:N_EXPERTS, :] * (1.0 / TM_MOE)
    on_diag = r_idx == lax.broadcasted_iota(I32, (N_EXPERTS, LANES), 1)
    end_blk = jnp.sum(jnp.where(on_diag, ends, 0.0), axis=0, keepdims=True) * (1.0 / TM_MOE)
    n_blk = jnp.sum(jnp.where(on_diag, padded, 0.0), axis=0, keepdims=True) * (1.0 / TM_MOE)
    r8 = lax.broadcasted_iota(I32, (SUBLANES, LANES), 0)
    meta = jnp.where(r8 == 0, blk_expert,
                     jnp.where(r8 == 1, n_used, jnp.where(r8 == 2, end_blk, jnp.where(r8 == 3, n_blk, 0.0))))
    meta_ref[...] = meta.astype(I32)


def _dispatch(eid):
    n = eid.shape[1]
    return pl.pallas_call(
        _dispatch_kernel,
        out_shape=[jax.ShapeDtypeStruct((SUBLANES, n), I32),
                   jax.ShapeDtypeStruct((SUBLANES, LANES), I32)],
        scratch_shapes=[pltpu.VMEM((SUBLANES, n), F32)],
        compiler_params=pltpu.CompilerParams(vmem_limit_bytes=VMEM_LIMIT_BYTES),
        name="dispatch",
    )(eid)


def _row_copy(src_ref, src_row, dst_ref, dst_row, sem):
    tile = lambda ref, row: ref.at[pl.ds(pl.multiple_of(row * SUBLANES, SUBLANES), SUBLANES)]
    return pltpu.make_async_copy(tile(src_ref, src_row), tile(dst_ref, dst_row), sem)


def _scatter_kernel(dest_ref, meta_ref, h_ref, xs_ref, zero_buf, sem, zsem):
    tm = h_ref.shape[0] // SUBLANES
    blk = zero_buf.shape[0]

    @pl.when(pl.program_id(0) == 0)
    def _():
        zero_buf[...] = jnp.zeros(zero_buf.shape, zero_buf.dtype)
        n_used = meta_ref[1, 0]

        def zero_block(b):
            return pltpu.make_async_copy(zero_buf, xs_ref.at[pl.ds(pl.multiple_of(b * blk, blk), blk)], zsem)

        def sweep(act):
            def per_expert(e, c):
                @pl.when(meta_ref[3, e] > 0)
                def _():
                    act(zero_block(meta_ref[2, e] - 1))
                return c

            def per_tail(b, c):
                @pl.when(b >= n_used)
                def _():
                    act(zero_block(b))
                return c

            lax.fori_loop(0, N_EXPERTS, per_expert, 0)
            lax.fori_loop(0, xs_ref.shape[0] // blk, per_tail, 0)

        sweep(lambda cp: cp.start())
        sweep(lambda cp: cp.wait())

    def start(r, c):
        for k in range(2):
            _row_copy(h_ref, r, xs_ref, dest_ref[0, k, r], sem).start()
        return c

    lax.fori_loop(0, tm, start, 0, unroll=DMA_ISSUE_UNROLL)
    for k in range(2):
        pltpu.make_async_copy(h_ref, xs_ref.at[pl.ds(0, tm * SUBLANES)], sem).wait()


def _scatter(dest3, meta, h2, cap):
    nsteps, _, tm = dest3.shape
    return pl.pallas_call(
        _scatter_kernel,
        grid=(nsteps,),
        in_specs=[
            pl.BlockSpec((1, 2, tm), lambda i: (i, 0, 0), memory_space=pltpu.SMEM),
            pl.BlockSpec(memory_space=pltpu.SMEM),
            pl.BlockSpec((tm * SUBLANES, LANES), lambda i: (i, 0)),
        ],
        out_specs=pl.BlockSpec(memory_space=pl.ANY),
        out_shape=jax.ShapeDtypeStruct((cap * SUBLANES, LANES), h2.dtype),
        scratch_shapes=[pltpu.VMEM((TM_MOE * SUBLANES, LANES), h2.dtype), pltpu.SemaphoreType.DMA(()),
                        pltpu.SemaphoreType.DMA(())],
        compiler_params=_cparams(1),
        name="scatter",
    )(dest3, meta, h2)


def _experts_kernel(meta_ref, xs_ref, wg_hbm, wu_hbm, wd_hbm, ys_ref,
                    wg_buf, wu_buf, wd_buf, wg16, wu16, wd16, elist_ref, ord_ref, sem):
    c = pl.program_id(0)
    n_used = meta_ref[1, 0]
    e = meta_ref[0, c]
    first = (c == 0) | (e != meta_ref[0, jnp.maximum(c - 1, 0)])

    def fetch(expert, slot):
        return [pltpu.make_async_copy(w.at[expert], buf.at[slot], sem.at[slot, i])
                for i, (w, buf) in enumerate(((wg_hbm, wg_buf), (wu_hbm, wu_buf), (wd_hbm, wd_buf)))]

    @pl.when(c == 0)
    def _():
        def scan(b, cnt):
            eb = meta_ref[0, b]
            take = ((b == 0) | (eb != meta_ref[0, jnp.maximum(b - 1, 0)])) & (b < n_used)

            @pl.when(take)
            def _():
                elist_ref[cnt] = eb

            return cnt + take.astype(I32)

        cnt = lax.fori_loop(0, pl.num_programs(0), scan, 0)
        ord_ref[0] = 0
        ord_ref[1] = cnt
        for k in range(W_SLOTS - 1):
            @pl.when(k < cnt)
            def _(k=k):
                for cp in fetch(elist_ref[k], k):
                    cp.start()

    @pl.when(first & (c < n_used))
    def _():
        @pl.when(c > 0)
        def _():
            ord_ref[0] = ord_ref[0] + 1
        k = ord_ref[0]
        for cp in fetch(e, k % W_SLOTS):
            cp.wait()
        ahead = k + (W_SLOTS - 1)

        @pl.when(ahead < ord_ref[1])
        def _():
            for cp in fetch(elist_ref[jnp.minimum(ahead, N_EXPERTS - 1)], ahead % W_SLOTS):
                cp.start()

    def mlp(weights):
        lo, hi = _unpack_bf16_halves(_load_token_tiles(xs_ref))
        lo, hi = lo.astype(BF16), hi.astype(BF16)
        half = lo.shape[1]
        wg, wu, wd = weights

        def proj(w):
            return _dot(lo, w[:half, :]) + _dot(hi, w[half:, :])

        act = (jax.nn.silu(proj(wg)) * proj(wu)).astype(BF16)
        _store_token_tiles(ys_ref, _pack_bf16_halves(_dot(act, wd)))

    @pl.when(first & (c < n_used))
    def _():
        slot = ord_ref[0] % W_SLOTS
        cast = [buf[slot].astype(BF16) for buf in (wg_buf, wu_buf, wd_buf)]
        for ref16, w in zip((wg16, wu16, wd16), cast):
            ref16[...] = w
        mlp(cast)

    @pl.when(jnp.logical_not(first) & (c < n_used))
    def _():
        mlp((wg16[...], wu16[...], wd16[...]))

    @pl.when(c >= n_used)
    def _():
        ys_ref[...] = jnp.zeros(ys_ref.shape, U32)


def _experts(meta, xs, w_gate, w_up, w_down):
    rows = TM_MOE * SUBLANES
    nblk = xs.shape[0] // rows
    d, ff = w_gate.shape[1:]

    def row_map(c, meta):
        return (jnp.minimum(c, meta[1, 0] - 1), 0)

    hbm = pl.BlockSpec(memory_space=pl.ANY)
    return pl.pallas_call(
        _experts_kernel,
        grid_spec=pltpu.PrefetchScalarGridSpec(
            num_scalar_prefetch=1,
            grid=(nblk,),
            in_specs=[pl.BlockSpec((rows, LANES), row_map), hbm, hbm, hbm],
            out_specs=pl.BlockSpec((rows, LANES), lambda c, meta: (c, 0)),
            scratch_shapes=[
                pltpu.VMEM((W_SLOTS, d, ff), F32), pltpu.VMEM((W_SLOTS, d, ff), F32),
                pltpu.VMEM((W_SLOTS, ff, d), F32),
                pltpu.VMEM((d, ff), BF16), pltpu.VMEM((d, ff), BF16), pltpu.VMEM((ff, d), BF16),
                pltpu.SMEM((N_EXPERTS,), I32),
                pltpu.SMEM((2,), I32),
                pltpu.SemaphoreType.DMA((W_SLOTS, 3)),
            ],
        ),
        out_shape=jax.ShapeDtypeStruct(xs.shape, U32),
        compiler_params=_cparams(1),
        name="experts",
    )(meta, xs, w_gate, w_up, w_down)


def _combine_kernel(dest_ref, dest_next_ref, x1_ref, gate_ref, ln_ref, ys_ref, o_ref, ybuf, sem):
    i = pl.program_id(0)
    tm = x1_ref.shape[0]
    slot = i % 2

    def issue(d_ref, s):
        def start(r, c):
            for k in range(2):
                _row_copy(ys_ref, d_ref[0, k, r], ybuf.at[s, k], r, sem.at[s]).start()
            return c

        lax.fori_loop(0, tm, start, 0, unroll=DMA_ISSUE_UNROLL)

    @pl.when(i == 0)
    def _():
        issue(dest_ref, 0)

    @pl.when(i + 1 < pl.num_programs(0))
    def _():
        issue(dest_next_ref, 1 - slot)

    for k in range(2):
        pltpu.make_async_copy(ys_ref.at[pl.ds(0, tm * SUBLANES)], ybuf.at[slot, k], sem.at[slot]).wait()
    gate = gate_ref[...]
    lo0, hi0 = _unpack_bf16_halves(_load_token_tiles(ybuf.at[slot, 0]))
    lo1, hi1 = _unpack_bf16_halves(_load_token_tiles(ybuf.at[slot, 1]))
    g0, g1 = gate[:, 0:1], gate[:, 1:2]
    moe = jnp.concatenate([lo0 * g0 + lo1 * g1, hi0 * g0 + hi1 * g1], axis=1)
    x2 = x1_ref[...] + moe
    ms = jnp.mean(x2 * x2, axis=-1, keepdims=True)
    o_ref[...] = x2 * lax.rsqrt(ms + EPS) * ln_ref[...]


def _combine(dest3, x1, gate_rows, ln_final, ys):
    n, d = x1.shape
    tm = TM_COMBINE
    last = n // tm - 1
    return pl.pallas_call(
        _combine_kernel,
        grid=(n // tm,),
        in_specs=[
            pl.BlockSpec((1, 2, tm), lambda i: (i, 0, 0), memory_space=pltpu.SMEM),
            pl.BlockSpec((1, 2, tm), lambda i: (jnp.minimum(i + 1, last), 0, 0), memory_space=pltpu.SMEM),
            pl.BlockSpec((tm, d), lambda i: (i, 0)),
            pl.BlockSpec((tm, LANES), lambda i: (i, 0)),
            pl.BlockSpec((1, d), lambda i: (0, 0)),
            pl.BlockSpec(memory_space=pl.ANY),
        ],
        out_specs=pl.BlockSpec((tm, d), lambda i: (i, 0)),
        out_shape=jax.ShapeDtypeStruct((n, d), F32),
        scratch_shapes=[pltpu.VMEM((2, 2, tm * SUBLANES, LANES), U32), pltpu.SemaphoreType.DMA((2,))],
        compiler_params=_cparams(1),
        name="combine",
    )(dest3, dest3, x1, gate_rows, ln_final, ys)


def kernel(x, positions, ln_mix, w_in, pe_cmp_k, w_cmp_k1, w_cmp_k2, pe_cmp_v, w_cmp_v1, w_cmp_v2, w_pool, b_pool, pool_scale, gn_nsa, gn_pool, w_out, ln_moe, w_router_group, b_router_group, w_router_expert, b_router_expert, w_gate, w_up, w_down, ln_final):
    batch, seq, d = x.shape
    n = batch * seq
    depth = w_in.shape[0]
    nsa_w = N_HEADS * HEAD_DIM
    kv_w = N_KV * HEAD_DIM
    cut_gates = nsa_w + 6 * kv_w
    cut_pool = cut_gates + N_GATES

    inv_freq = ROPE_THETA ** (-jnp.arange(0, ROT_DIM, 2, dtype=F32) / ROT_DIM)
    ang = positions.astype(F32).reshape(n, 1) * inv_freq[None, :]
    cos, sin = jnp.cos(ang), jnp.sin(ang)
    rest = HEAD_DIM - ROT_DIM
    trig = jnp.stack([jnp.concatenate([cos, cos, jnp.ones((n, rest), F32)], axis=1),
                      jnp.concatenate([-sin, sin, jnp.zeros((n, rest), F32)], axis=1)])
    trigt = jnp.transpose(trig, (0, 2, 1))
    kblk = (jnp.arange(seq, dtype=I32)[:, None] // SEL_LEN == jnp.arange(LANES, dtype=I32)[None, :]).astype(BF16)

    x2d = x.reshape(n, d)
    assert depth == 1, "single-layer operation"
    for l in range(depth):
        wi = w_in[l]
        o_kc, o_ks = nsa_w, nsa_w + 2 * kv_w
        o_vs, o_kw, o_vw = o_ks + kv_w, o_ks + 2 * kv_w, o_ks + 3 * kv_w
        gate_rows_pad = -(-N_GATES // SUBLANES) * SUBLANES
        wq = wi[:, :nsa_w].astype(BF16)
        wv = jnp.concatenate([wi[:, o_vs:o_kw], wi[:, o_vw:cut_gates]], axis=1).astype(BF16)
        wg = jnp.pad(wi[:, cut_gates:cut_pool], ((0, 0), (0, LANES - N_GATES))).astype(BF16)
        wkc = wi[:, o_kc:o_ks].astype(BF16)
        wk = jnp.concatenate([wi[:, o_ks:o_vs], wi[:, o_kw:o_vw]], axis=1).astype(BF16)
        wu = wi[:, cut_pool:].astype(BF16)
        qnt, qrt, kvc, kk, vt, gt, u = _inproj(x2d, ln_mix[l][None], trig, trigt, wq, wv, wg, wkc, wk, wu,
                                               gate_rows_pad, batch, seq)

        half = CMP_LEN * HEAD_DIM // 2
        def stack_w1(w):
            return jnp.concatenate([w[:half], w[half:]], axis=1)
        w1 = jnp.stack([stack_w1(w_cmp_k1[l]), stack_w1(w_cmp_v1[l])]).astype(BF16)
        w2 = jnp.stack([w_cmp_k2[l], w_cmp_v2[l]]).astype(BF16)
        def pe_rows(pe):
            return jnp.pad(pe.reshape(2, half), ((0, SUBLANES - 2), (0, 0)))
        pe2 = jnp.stack([pe_rows(pe_cmp_k[l]), pe_rows(pe_cmp_v[l])]).astype(BF16)
        cmp = _compress(kvc, pe2, w1, w2)

        mix_nsa = _nsa(qnt, qrt, cmp, kk, vt, kblk, gt, gn_nsa[l][None])
        mix_pool = _pool(u.reshape(batch, seq, -1), w_pool[l].astype(BF16), b_pool[l][None],
                         pool_scale[l][None], gn_pool[l][None])

        ng, ne = N_EXPERT_GROUPS, EXPERTS_PER_GROUP
        w_r = jnp.concatenate([
            w_router_group[l].T, jnp.zeros((ne - ng, d), F32),
            jnp.transpose(w_router_expert[l], (0, 2, 1)).reshape(ng * ne, d),
            jnp.zeros((LANES - ne - ng * ne, d), F32)], axis=0).astype(BF16)
        b_r = jnp.concatenate([
            b_router_group[l], jnp.zeros((ne - ng,), F32), b_router_expert[l].reshape(-1),
            jnp.zeros((LANES - ne - ng * ne,), F32)])
        b_r = jnp.broadcast_to(b_r[:, None], (LANES, LANES))
        x1, h2, eid, gate = _outproj(mix_nsa, mix_pool, x2d, w_out[l].astype(BF16), ln_moe[l][None], w_r, b_r)

        dest, meta = _dispatch(eid)
        cap = (2 * n // TM_MOE + N_EXPERTS) * TM_MOE
        def per_tile(tm):
            return jnp.transpose(dest[:2].reshape(2, n // tm, tm), (1, 0, 2))
        xs = _scatter(per_tile(TS_SCATTER), meta, h2, cap)
        ys = _experts(meta, xs, w_gate[l], w_up[l], w_down[l])
        dest_tok = per_tile(TM_COMBINE)
        gate_rows = jnp.pad(gate[:2].T, ((0, 0), (0, LANES - 2)))
        x2d = _combine(dest_tok, x1, gate_rows, ln_final[None], ys)
    return x2d.reshape(batch, seq, d)
```

```python
import functools

import jax
import jax.numpy as jnp
from jax import lax
from jax.experimental import pallas as pl
from jax.experimental.pallas import tpu as pltpu

F32 = jnp.float32
BF16 = jnp.bfloat16
I32 = jnp.int32

HEAD_DIM = 128
N_HEADS = 8
N_KV = 2
GQA = N_HEADS // N_KV
N_GATES = 3 * N_HEADS
ROT_DIM = HEAD_DIM // 4
ROPE_THETA = 500000.0
CMP_LEN = 32
CMP_STRIDE = 16
CMP_HIDDEN = 2 * HEAD_DIM
SEL_LEN = 64
SEL_TOPK = 16
N_LOCAL = 2
WINDOW = 512
POOL_SIZES = (2, 4, 8, 16)
N_EXPERT_GROUPS = 4
EXPERTS_PER_GROUP = 8
N_EXPERTS = N_EXPERT_GROUPS * EXPERTS_PER_GROUP
EPS = 1e-6
NEG = -1e30
BIG = 1e30
SCALE = HEAD_DIM ** -0.5
LOG2E = 1.4426950408889634
Q_SCALE = SCALE * LOG2E

LANES = 128
SUBLANES = 8
VMEM_LIMIT_BYTES = 56 * 1024 * 1024

TM_PROJ = 256
TQ = 256
TK = 256
SEL_MASK_BIAS = 32768.0
V_ONES_ROWS = 16
TM_MOE = 256
TS_SCATTER = 512
TM_COMBINE = 128
WEIGHT_DMA_PRIORITY = 1
W_SLOTS = 3
DMA_ISSUE_UNROLL = 8
T_RANK = 512


def _cparams(n_axes, flags=None):
    return pltpu.CompilerParams(dimension_semantics=("arbitrary",) * n_axes,
                                vmem_limit_bytes=VMEM_LIMIT_BYTES, flags=flags)


def _dot(a, b):
    return jnp.dot(a, b, preferred_element_type=F32)


def _dot_nt(a, b):
    return lax.dot_general(a, b, (((1,), (1,)), ((), ())), preferred_element_type=F32)


U32 = jnp.uint32
_HI16 = 0xFFFF0000


def _pack_bf16_halves(x):
    c = x.shape[1] // 2
    bits = lambda t: lax.bitcast_convert_type(t.astype(BF16).astype(F32), U32)
    return lax.shift_right_logical(bits(x[:, :c]), U32(16)) | (bits(x[:, c:]) & U32(_HI16))


def _store_token_tiles(ref, words):
    m, width = words.shape
    assert width == SUBLANES * LANES
    for j in range(SUBLANES):
        ref[pl.ds(j, m, stride=SUBLANES), :] = words[:, j * LANES:(j + 1) * LANES]


def _load_token_tiles(ref):
    m = ref.shape[0] // SUBLANES
    return jnp.concatenate([ref[pl.ds(j, m, stride=SUBLANES), :] for j in range(SUBLANES)], axis=1)


def _unpack_bf16_halves(w):
    lo = lax.bitcast_convert_type(lax.shift_left(w, U32(16)), F32)
    hi = lax.bitcast_convert_type(w & U32(_HI16), F32)
    return lo, hi


def _rope(t, cos, sin, axis):
    idx = lax.broadcasted_iota(I32, t.shape, axis)
    partner = jnp.where(idx < ROT_DIM // 2,
                        pltpu.roll(t, HEAD_DIM - ROT_DIM // 2, axis),
                        pltpu.roll(t, ROT_DIM // 2, axis))
    return t * cos + partner * sin


def _inproj_kernel(x_ref, ln_ref, trig_ref, trigt_ref, wq_ref, wv_ref, wg_ref, wkc_ref, wk_ref, wu_ref,
                   qnt_ref, qrt_ref, kvc_ref, kk_ref, vt_ref, gt_ref, u_ref, kc_scr):
    x = x_ref[...]
    ms = jnp.mean(x * x, axis=-1, keepdims=True)
    hb = (x * lax.rsqrt(ms + EPS) * ln_ref[...]).astype(BF16)
    chunk = 4 * HEAD_DIM

    cost, sint = trigt_ref[0], trigt_ref[1]
    for ch in range(2):
        r = _dot(hb, wq_ref[:, ch * chunk:(ch + 1) * chunk])
        for j in range(4):
            t = r[:, j * HEAD_DIM:(j + 1) * HEAD_DIM].T
            qnt_ref[0, ch * 4 + j] = (t * Q_SCALE).astype(BF16)
            qrt_ref[0, ch * 4 + j] = (_rope(t, cost, sint, 0) * Q_SCALE).astype(BF16)
    r = _dot(hb, wv_ref[...])
    ones = jnp.ones((V_ONES_ROWS, r.shape[0]), BF16)
    for j in range(4):
        vt_ref[0, j, 0] = jnp.concatenate([r[:, j * HEAD_DIM:(j + 1) * HEAD_DIM].T.astype(BF16), ones], axis=0)
    gt_ref[...] = _dot(hb, wg_ref[...]).T[:gt_ref.shape[0], :]

    r = _dot(hb, wkc_ref[...])
    groups = kvc_ref.shape[2]
    for j in range(4):
        kc_scr[j] = r[:, j * HEAD_DIM:(j + 1) * HEAD_DIM]
        for i in range(CMP_STRIDE):
            kvc_ref[0, j, :, i * HEAD_DIM:(i + 1) * HEAD_DIM] = (
                kc_scr[j, pl.ds(i, groups, stride=CMP_STRIDE), :].astype(BF16))
    cos, sin = trig_ref[0], trig_ref[1]
    r = _dot(hb, wk_ref[...])
    for j in range(4):
        kk_ref[0, j] = _rope(r[:, j * HEAD_DIM:(j + 1) * HEAD_DIM], cos, sin, 1).astype(BF16)
    for ch in range(2):
        u_ref[:, ch * chunk:(ch + 1) * chunk] = _dot(hb, wu_ref[:, ch * chunk:(ch + 1) * chunk])


def _inproj(x2d, ln, trig, trigt, wq, wv, wg, wkc, wk, wu, gate_rows, batch, seq):
    n, d = x2d.shape
    tm = TM_PROJ
    assert tm == TK
    spb = seq // tm
    const = lambda i: (0, 0)
    resident = lambda w: pl.BlockSpec(w.shape, const, pipeline_mode=pl.Buffered(1))
    row_heads = lambda i: (i // spb, 0, i % spb, 0)
    col_heads = lambda i: (i // spb, 0, 0, i % spb)
    return pl.pallas_call(
        _inproj_kernel,
        grid=(n // tm,),
        in_specs=[
            pl.BlockSpec((tm, d), lambda i: (i, 0)),
            pl.BlockSpec((1, d), const),
            pl.BlockSpec((2, tm, HEAD_DIM), lambda i: (0, i, 0)),
            pl.BlockSpec((2, HEAD_DIM, tm), lambda i: (0, 0, i)),
            resident(wq), resident(wv), resident(wg), resident(wkc), resident(wk), resident(wu),
        ],
        out_specs=[
            pl.BlockSpec((1, N_HEADS, HEAD_DIM, tm), col_heads),
            pl.BlockSpec((1, N_HEADS, HEAD_DIM, tm), col_heads),
            pl.BlockSpec((1, 4, tm // CMP_STRIDE, CMP_STRIDE * HEAD_DIM), row_heads),
            pl.BlockSpec((1, 4, tm, HEAD_DIM), row_heads),
            pl.BlockSpec((1, 4, 1, HEAD_DIM + V_ONES_ROWS, tm), lambda i: (i // spb, 0, i % spb, 0, 0)),
            pl.BlockSpec((gate_rows, tm), lambda i: (0, i)),
            pl.BlockSpec((tm, wu.shape[1]), lambda i: (i, 0)),
        ],
        out_shape=[
            jax.ShapeDtypeStruct((batch, N_HEADS, HEAD_DIM, seq), BF16),
            jax.ShapeDtypeStruct((batch, N_HEADS, HEAD_DIM, seq), BF16),
            jax.ShapeDtypeStruct((batch, 4, seq // CMP_STRIDE, CMP_STRIDE * HEAD_DIM), BF16),
            jax.ShapeDtypeStruct((batch, 4, seq, HEAD_DIM), BF16),
            jax.ShapeDtypeStruct((batch, 4, spb, HEAD_DIM + V_ONES_ROWS, tm), BF16),
            jax.ShapeDtypeStruct((gate_rows, n), F32),
            jax.ShapeDtypeStruct((n, wu.shape[1]), F32),
        ],
        scratch_shapes=[pltpu.VMEM((4, tm, HEAD_DIM), F32)],
        compiler_params=_cparams(1),
        name="inproj",
    )(x2d, ln, trig, trigt, wq, wv, wg, wkc, wk, wu)


def _compress_kernel(t_ref, pe_ref, w1_ref, w2_ref, o_ref):
    t2 = t_ref[0, 0]
    w1 = w1_ref[0]
    a = _dot(t2, w1)
    pb = _dot(pe_ref[0], w1)
    bias = pb[0:1, :CMP_HIDDEN] + pb[1:2, CMP_HIDDEN:]
    nrow = a.shape[0]
    nxt = pltpu.roll(a[:, CMP_HIDDEN:], nrow - 1, 0)
    hid = jax.nn.gelu(a[:, :CMP_HIDDEN] + nxt + bias)
    out = _dot(hid.astype(BF16), w2_ref[0])
    row = lax.broadcasted_iota(I32, out.shape, 0)
    out = jnp.where(row < nrow - 1, out, 0.0)
    is_value = pl.program_id(1) >= N_KV
    o_ref[0, 0] = jnp.where(is_value, out.T, out).astype(BF16)


def _compress(kvc2, pe2, w1, w2):
    batch, four, nrow, width = kvc2.shape
    return pl.pallas_call(
        _compress_kernel,
        grid=(batch, four),
        in_specs=[
            pl.BlockSpec((1, 1, nrow, width), lambda b, j: (b, j, 0, 0)),
            pl.BlockSpec((1, SUBLANES, width), lambda b, j: (j // N_KV, 0, 0)),
            pl.BlockSpec((1, width, 2 * CMP_HIDDEN), lambda b, j: (j // N_KV, 0, 0)),
            pl.BlockSpec((1, CMP_HIDDEN, HEAD_DIM), lambda b, j: (j // N_KV, 0, 0)),
        ],
        out_specs=pl.BlockSpec((1, 1, nrow, HEAD_DIM), lambda b, j: (b, j, 0, 0)),
        out_shape=jax.ShapeDtypeStruct((batch, four, nrow, HEAD_DIM), BF16),
        compiler_params=_cparams(2),
        name="compress",
    )(kvc2, pe2, w1, w2)


def _nsa_kernel(qnt_ref, qrt_ref, cmp_ref, kk_ref, vt_ref, kblk_ref, gt_ref, gn_ref, o_ref,
                m_scr, acc_scr, qa_scr, ocmp_scr, s_scr, o_scr):
    qi = pl.program_id(1)
    tq = qnt_ref.shape[3]
    rows = GQA * tq
    n_cmp_pad = cmp_ref.shape[2]
    n_sel = kk_ref.shape[2] // SEL_LEN
    q0 = qi * tq

    m_scr[...] = jnp.full(m_scr.shape, NEG, F32)
    acc_scr[...] = jnp.zeros(acc_scr.shape, F32)

    def scores(item, slot):
        c, g, q_rows, k_tile, v_t, bias = item
        s_scr[slot] = _dot(k_tile, qa_scr[g, :q_rows, :])

    def softmax_pv(item, slot):
        c, g, q_rows, k_tile, v_t, bias = item
        s = s_scr[slot]
        if bias is not None:
            s = s + bias
        m_old = m_scr[c]
        m_new = jnp.maximum(m_old, jnp.max(s, axis=0, keepdims=True))
        alpha = jnp.exp2(m_old - m_new)
        p = jnp.exp2(s - m_new)
        acc_scr[c] = alpha * acc_scr[c] + _dot(v_t, p.astype(BF16))
        m_scr[c] = m_new

    def flash_out(c):
        return acc_scr[c, :HEAD_DIM, :] / acc_scr[c, HEAD_DIM:HEAD_DIM + 1, :]

    def tile_heads(a):
        return jnp.concatenate([a] * GQA, axis=1)

    kk_i = lax.broadcasted_iota(I32, (TK, tq), 0)
    qq_i = lax.broadcasted_iota(I32, (TK, tq), 1)
    causal_bias = tile_heads(jnp.where(kk_i <= qq_i, 0.0, NEG))
    band_bias = tile_heads(jnp.where(kk_i > qq_i, 0.0, NEG))
    n_win = WINDOW // TK

    for g in range(N_KV):
        qn4 = jnp.concatenate([qnt_ref[0, h] for h in range(g * GQA, (g + 1) * GQA)], axis=1)
        s = _dot(cmp_ref[0, g], qn4)
        n_idx = lax.broadcasted_iota(I32, (n_cmp_pad, rows), 0)
        t_idx = q0 + jnp.bitwise_and(lax.broadcasted_iota(I32, (n_cmp_pad, rows), 1), tq - 1)
        cmp_ok = n_idx * CMP_STRIDE + (CMP_LEN - 1) <= t_idx
        s = jnp.where(cmp_ok, s, NEG)
        m = jnp.max(s, axis=0, keepdims=True)
        e = jnp.where(cmp_ok, jnp.exp2(s - m), 0.0)
        l = jnp.sum(e, axis=0, keepdims=True)
        p = e * jnp.where(l > 0.0, 1.0 / l, 0.0)
        pb = p.astype(BF16)
        ocmp_scr[g] = _dot(cmp_ref[0, N_KV + g], pb)

        jn = lax.broadcasted_iota(I32, (n_sel, n_cmp_pad), 0) * SEL_LEN
        cn = lax.broadcasted_iota(I32, (n_sel, n_cmp_pad), 1) * CMP_STRIDE
        ov = jnp.clip(jnp.minimum(cn + CMP_LEN, jn + SEL_LEN) - jnp.maximum(cn, jn), 0, None)
        w_t = (ov.astype(F32) * (1.0 / CMP_LEN)).astype(BF16)
        imp4 = _dot(w_t, pb)
        imp = imp4[:, 0:tq]
        for r in range(1, GQA):
            imp = imp + imp4[:, r * tq:(r + 1) * tq]
        j_blk = lax.broadcasted_iota(I32, (n_sel, tq), 0)
        t_q = q0 + lax.broadcasted_iota(I32, (n_sel, tq), 1)
        sel_ok = j_blk * SEL_LEN <= t_q
        back = t_q // SEL_LEN - j_blk
        forced = (j_blk == 0) | ((back >= 0) & (back < N_LOCAL))
        val = jnp.where(sel_ok & forced, BIG, jnp.where(sel_ok, imp, -BIG))
        rank = jnp.zeros((n_sel, tq), F32)
        for i in range(n_sel):
            vi = val[i:i + 1, :]
            beats = (vi > val) | ((vi == val) & (j_blk > i))
            rank = rank + beats.astype(F32)
        keep = (rank < float(min(SEL_TOPK, n_sel))) & sel_ok
        bias_t = jnp.where(keep, 0.0, -SEL_MASK_BIAS)
        bias_t = jnp.concatenate([bias_t, jnp.zeros((HEAD_DIM - n_sel, tq), F32)], axis=0)
        qa_scr[g, :HEAD_DIM, :] = jnp.concatenate(
            [qrt_ref[0, h] for h in range(g * GQA, (g + 1) * GQA)], axis=1)
        qa_scr[g, HEAD_DIM:, :] = tile_heads(bias_t.astype(BF16))

    def sel_item(g, kj, bias):
        st = pl.multiple_of(kj * TK, TK)
        k_aug = jnp.concatenate([kk_ref[0, g, pl.ds(st, TK), :], kblk_ref[pl.ds(st, TK), :]], axis=1)
        return (g, g, 2 * HEAD_DIM, k_aug, vt_ref[0, g, kj], bias)

    def win_item(g, back):
        kj = jnp.maximum(qi - back, 0)
        st = pl.multiple_of(kj * TK, TK)
        off = jnp.where(qi >= back, 0.0, NEG)
        bias = causal_bias if back == 0 else (band_bias + off if back == n_win else off)
        return (N_KV + g, g, HEAD_DIM, kk_ref[0, N_KV + g, pl.ds(st, TK), :], vt_ref[0, N_KV + g, kj], bias)

    assert N_KV == 2
    scores(sel_item(0, 0, None), 0)

    def sel_body(kj, carry):
        scores(sel_item(1, kj, None), 1)
        softmax_pv(sel_item(0, kj, None), 0)
        scores(sel_item(0, kj + 1, None), 0)
        softmax_pv(sel_item(1, kj, None), 1)
        return carry

    lax.fori_loop(0, qi, sel_body, 0)
    tail = ([sel_item(g, qi, causal_bias) for g in range(N_KV)]
            + [win_item(g, back) for back in range(n_win, -1, -1) for g in range(N_KV)])
    for i, item in enumerate(tail):
        if i + 1 < len(tail):
            scores(tail[i + 1], (i + 1) % 2)
        softmax_pv(item, i % 2)

    gsig = jax.nn.sigmoid(gt_ref[...])
    for g in range(N_KV):
        o_cmp, o_sel, o_win = ocmp_scr[g], flash_out(g), flash_out(N_KV + g)
        for r in range(GQA):
            h = g * GQA + r
            sl = slice(r * tq, (r + 1) * tq)
            o_h = (gsig[3 * h:3 * h + 1, :] * o_cmp[:, sl]
                   + gsig[3 * h + 1:3 * h + 2, :] * o_sel[:, sl]
                   + gsig[3 * h + 2:3 * h + 3, :] * o_win[:, sl])
            o_scr[:, h * HEAD_DIM:(h + 1) * HEAD_DIM] = o_h.T

    o = o_scr[...]
    ms = jnp.mean(o * o, axis=-1, keepdims=True)
    o_ref[...] = (o * lax.rsqrt(ms + EPS) * gn_ref[...]).astype(BF16)


def _nsa(qnt, qrt, cmp, kk, vt, kblk, gt, gn):
    batch, _, _, seq = qnt.shape
    assert TQ == TK and WINDOW % TK == 0 and TQ & (TQ - 1) == 0
    nq = seq // TQ
    width = N_HEADS * HEAD_DIM
    return pl.pallas_call(
        _nsa_kernel,
        grid=(batch, nq),
        in_specs=[
            pl.BlockSpec((1, N_HEADS, HEAD_DIM, TQ), lambda b, i: (b, 0, 0, i)),
            pl.BlockSpec((1, N_HEADS, HEAD_DIM, TQ), lambda b, i: (b, 0, 0, i)),
            pl.BlockSpec((1,) + cmp.shape[1:], lambda b, i: (b, 0, 0, 0)),
            pl.BlockSpec((1,) + kk.shape[1:], lambda b, i: (b, 0, 0, 0)),
            pl.BlockSpec((1,) + vt.shape[1:], lambda b, i: (b, 0, 0, 0, 0)),
            pl.BlockSpec((seq, LANES), lambda b, i: (0, 0)),
            pl.BlockSpec((gt.shape[0], TQ), lambda b, i: (0, b * nq + i)),
            pl.BlockSpec((1, width), lambda b, i: (0, 0)),
        ],
        out_specs=pl.BlockSpec((TQ, width), lambda b, i: (b * nq + i, 0)),
        out_shape=jax.ShapeDtypeStruct((batch * seq, width), BF16),
        scratch_shapes=[
            pltpu.VMEM((2 * N_KV, 1, GQA * TQ), F32),
            pltpu.VMEM((2 * N_KV, vt.shape[3], GQA * TQ), F32),
            pltpu.VMEM((N_KV, 2 * HEAD_DIM, GQA * TQ), BF16),
            pltpu.VMEM((N_KV, HEAD_DIM, GQA * TQ), F32),
            pltpu.VMEM((2, TK, GQA * TQ), F32),
            pltpu.VMEM((TQ, width), F32),
        ],
        compiler_params=_cparams(2),
        name="nsa",
    )(qnt, qrt, cmp, kk, vt, kblk, gt, gn)


def _pool_kernel(u_ref, up_ref, w_ref, b_ref, sc_ref, gn_ref, o_ref, y_scr):
    si = pl.program_id(1)
    tm = u_ref.shape[1]
    halo = max(POOL_SIZES)
    cur = u_ref[0]
    prev = jnp.where(si > 0, up_ref[0, tm - halo:, :], 0.0)
    ext = jnp.concatenate([prev, cur], axis=0)
    t1 = (si * tm + 1 + lax.broadcasted_iota(I32, (tm, 1), 0)).astype(F32)
    cg = cur.shape[1] // len(POOL_SIZES)
    for gi, w in enumerate(POOL_SIZES):
        sl = slice(gi * cg, (gi + 1) * cg)
        acc = ext[:, sl]
        span = 1
        while span < w:
            acc = acc + jnp.concatenate([jnp.zeros((span, cg), F32), acc[:-span]], axis=0)
            span *= 2
        mean = acc[halo:] / jnp.minimum(t1, float(w))
        d = (mean - cur[:, sl]).astype(BF16)
        y = _dot(d, w_ref[gi]) + b_ref[:, sl]
        y_scr[:, sl] = y * sc_ref[:, sl]
    y = y_scr[...]
    ms = jnp.mean(y * y, axis=-1, keepdims=True)
    o_ref[...] = (y * lax.rsqrt(ms + EPS) * gn_ref[...]).astype(BF16)


def _pool(u3, w_pool, b_pool, pool_scale, gn_pool):
    batch, seq, c = u3.shape
    tm = TM_PROJ
    ns = seq // tm
    vec = lambda b, i: (0, 0)
    return pl.pallas_call(
        _pool_kernel,
        grid=(batch, ns),
        in_specs=[
            pl.BlockSpec((1, tm, c), lambda b, i: (b, i, 0)),
            pl.BlockSpec((1, tm, c), lambda b, i: (b, jnp.maximum(i - 1, 0), 0)),
            pl.BlockSpec(w_pool.shape, lambda b, i: (0, 0, 0)),
            pl.BlockSpec((1, c), vec),
            pl.BlockSpec((1, c), vec),
            pl.BlockSpec((1, c), vec),
        ],
        out_specs=pl.BlockSpec((tm, c), lambda b, i: (b * ns + i, 0)),
        out_shape=jax.ShapeDtypeStruct((batch * seq, c), BF16),
        scratch_shapes=[pltpu.VMEM((tm, c), F32)],
        compiler_params=_cparams(2),
        name="pool",
    )(u3, u3, w_pool, b_pool, pool_scale, gn_pool)


def _outproj_kernel(mn_ref, mp_ref, x_ref, w_ref, ln_ref, wr_ref, br_ref,
                    x1_ref, h2_ref, eid_ref, gate_ref):
    tm, d = x_ref.shape
    half = mn_ref.shape[1]
    chunk = 512
    mn = mn_ref[...]
    mp = mp_ref[...]
    ssq = jnp.zeros((tm, 1), F32)
    for c in range(d // chunk):
        sl = slice(c * chunk, (c + 1) * chunk)
        x1 = x_ref[:, sl] + (_dot(mn, w_ref[:half, sl]) + _dot(mp, w_ref[half:, sl]))
        x1_ref[:, sl] = x1
        ssq = ssq + jnp.sum(x1 * x1, axis=-1, keepdims=True)
    h2 = x1_ref[...] * lax.rsqrt(ssq * (1.0 / d) + EPS) * ln_ref[...]
    _store_token_tiles(h2_ref, _pack_bf16_halves(h2))

    logits = _dot_nt(wr_ref[...], h2.astype(BF16)) + br_ref[:, 0:1]
    ng, ne = N_EXPERT_GROUPS, EXPERTS_PER_GROUP
    row = lax.broadcasted_iota(I32, (ne, tm), 0)
    lg = jnp.where(row < ng, logits[0:ne], NEG)
    mg = jnp.max(lg, axis=0, keepdims=True)
    g_sel = jnp.min(jnp.where(lg == mg, row, ne), axis=0, keepdims=True)
    p_g = 1.0 / jnp.sum(jnp.exp(lg - mg), axis=0, keepdims=True)
    le = jnp.zeros((ne, tm), F32)
    for g in range(ng):
        le = jnp.where(g_sel == g, logits[ne * (g + 1):ne * (g + 2)], le)
    ex = jnp.exp(le - jnp.max(le, axis=0, keepdims=True))
    pe = ex / jnp.sum(ex, axis=0, keepdims=True)
    p1 = jnp.max(pe, axis=0, keepdims=True)
    i1 = jnp.min(jnp.where(pe == p1, row, ne), axis=0, keepdims=True)
    rest = jnp.where(row == i1, -1.0, pe)
    p2 = jnp.max(rest, axis=0, keepdims=True)
    i2 = jnp.min(jnp.where(rest == p2, row, ne), axis=0, keepdims=True)
    psum = p1 + p2
    zero_i = jnp.zeros((ne, tm), I32)
    eid = jnp.where(row == 0, g_sel * ne + i1, jnp.where(row == 1, g_sel * ne + i2, zero_i))
    gate = jnp.where(row == 0, p_g * p1 / psum, jnp.where(row == 1, p_g * p2 / psum, 0.0))
    eid_ref[...] = eid
    gate_ref[...] = gate


def _outproj(mix_nsa, mix_pool, x2d, w_out, ln_moe, w_router_t, b_router):
    n, d = x2d.shape
    tm = TM_PROJ
    half = mix_nsa.shape[1]
    const = lambda i: (0, 0)
    return pl.pallas_call(
        _outproj_kernel,
        grid=(n // tm,),
        in_specs=[
            pl.BlockSpec((tm, half), lambda i: (i, 0)),
            pl.BlockSpec((tm, half), lambda i: (i, 0)),
            pl.BlockSpec((tm, d), lambda i: (i, 0)),
            pl.BlockSpec(w_out.shape, const, pipeline_mode=pl.Buffered(1)),
            pl.BlockSpec((1, d), const),
            pl.BlockSpec(w_router_t.shape, const),
            pl.BlockSpec(b_router.shape, const),
        ],
        out_specs=[
            pl.BlockSpec((tm, d), lambda i: (i, 0)),
            pl.BlockSpec((tm * SUBLANES, LANES), lambda i: (i, 0)),
            pl.BlockSpec((SUBLANES, tm), lambda i: (0, i)),
            pl.BlockSpec((SUBLANES, tm), lambda i: (0, i)),
        ],
        out_shape=[
            jax.ShapeDtypeStruct((n, d), F32),
            jax.ShapeDtypeStruct((n * SUBLANES, LANES), U32),
            jax.ShapeDtypeStruct((SUBLANES, n), I32),
            jax.ShapeDtypeStruct((SUBLANES, n), F32),
        ],
        compiler_params=_cparams(1),
        name="outproj",
    )(mix_nsa, mix_pool, x2d, w_out, ln_moe, w_router_t, b_router)


def _dispatch_kernel(eid_ref, dest_ref, meta_ref, rank_scr):
    n = eid_ref.shape[1]
    t = T_RANK
    row = lax.broadcasted_iota(I32, (N_EXPERTS, t), 0)
    before = (lax.broadcasted_iota(I32, (t, t), 0) < lax.broadcasted_iota(I32, (t, t), 1)).astype(BF16)
    carry = jnp.zeros((N_EXPERTS, 1), F32)
    for k in range(2):
        for j in range(n // t):
            sl = slice(j * t, (j + 1) * t)
            oh = row == eid_ref[k:k + 1, sl]
            ohf = oh.astype(F32)
            prior = _dot(ohf.astype(BF16), before) + carry
            rank_scr[k:k + 1, sl] = jnp.sum(jnp.where(oh, prior, 0.0), axis=0, keepdims=True)
            carry = carry + jnp.sum(ohf, axis=1, keepdims=True)
    counts = jnp.broadcast_to(carry, (N_EXPERTS, LANES))
    padded = jnp.floor((counts + (TM_MOE - 1.0)) * (1.0 / TM_MOE)) * TM_MOE
    ends = padded
    r_idx = lax.broadcasted_iota(I32, (N_EXPERTS, LANES), 0)
    sh = 1
    while sh < N_EXPERTS:
        ends = ends + jnp.where(r_idx >= sh, pltpu.roll(ends, sh, 0), 0.0)
        sh *= 2
    starts = (ends - padded)[:, 0:1]
    for k in range(2):
        for j in range(n // t):
            sl = slice(j * t, (j + 1) * t)
            oh = row == eid_ref[k:k + 1, sl]
            base = jnp.sum(jnp.where(oh, starts, 0.0), axis=0, keepdims=True)
            dest_ref[k:k + 1, sl] = (rank_scr[k:k + 1, sl] + base).astype(I32)
    for k in range(2, SUBLANES):
        dest_ref[k:k + 1, :] = jnp.zeros((1, n), I32)
    blk_start = lax.broadcasted_iota(I32, (N_EXPERTS, LANES), 1).astype(F32) * TM_MOE
    blk_expert = jnp.sum((ends <= blk_start).astype(F32), axis=0, keepdims=True)
    blk_expert = jnp.minimum(blk_expert, N_EXPERTS - 1.0)
    n_used = ends[N_EXPERTS - 1:N_EXPERTS, :] * (1.0 / TM_MOE)
    on_diag = r_idx == lax.broadcasted_iota(I32, (N_EXPERTS, LANES), 1)
    end_blk = jnp.sum(jnp.where(on_diag, ends, 0.0), axis=0, keepdims=True) * (1.0 / TM_MOE)
    n_blk = jnp.sum(jnp.where(on_diag, padded, 0.0), axis=0, keepdims=True) * (1.0 / TM_MOE)
    r8 = lax.broadcasted_iota(I32, (SUBLANES, LANES), 0)
    meta = jnp.where(r8 == 0, blk_expert,
                     jnp.where(r8 == 1, n_used, jnp.where(r8 == 2, end_blk, jnp.where(r8 == 3, n_blk, 0.0))))
    meta_ref[...] = meta.astype(I32)


def _dispatch(eid):
    n = eid.shape[1]
    return pl.pallas_call(
        _dispatch_kernel,
        out_shape=[jax.ShapeDtypeStruct((SUBLANES, n), I32),
                   jax.ShapeDtypeStruct((SUBLANES, LANES), I32)],
        scratch_shapes=[pltpu.VMEM((SUBLANES, n), F32)],
        compiler_params=pltpu.CompilerParams(vmem_limit_bytes=VMEM_LIMIT_BYTES),
        name="dispatch",
    )(eid)


def _row_copy(src_ref, src_row, dst_ref, dst_row, sem):
    tile = lambda ref, row: ref.at[pl.ds(pl.multiple_of(row * SUBLANES, SUBLANES), SUBLANES)]
    return pltpu.make_async_copy(tile(src_ref, src_row), tile(dst_ref, dst_row), sem)


def _scatter_kernel(dest_ref, meta_ref, h_ref, xs_ref, zero_buf, sem, zsem):
    tm = h_ref.shape[0] // SUBLANES
    blk = zero_buf.shape[0]

    @pl.when(pl.program_id(0) == 0)
    def _():
        zero_buf[...] = jnp.zeros(zero_buf.shape, zero_buf.dtype)
        n_used = meta_ref[1, 0]

        def zero_block(b):
            return pltpu.make_async_copy(zero_buf, xs_ref.at[pl.ds(pl.multiple_of(b * blk, blk), blk)], zsem)

        def sweep(act):
            def per_expert(e, c):
                @pl.when(meta_ref[3, e] > 0)
                def _():
                    act(zero_block(meta_ref[2, e] - 1))
                return c

            def per_tail(b, c):
                @pl.when(b >= n_used)
                def _():
                    act(zero_block(b))
                return c

            lax.fori_loop(0, N_EXPERTS, per_expert, 0)
            lax.fori_loop(0, xs_ref.shape[0] // blk, per_tail, 0)

        sweep(lambda cp: cp.start())
        sweep(lambda cp: cp.wait())

    def start(r, c):
        for k in range(2):
            _row_copy(h_ref, r, xs_ref, dest_ref[0, k, r], sem).start(priority=k)
        return c

    lax.fori_loop(0, tm, start, 0, unroll=DMA_ISSUE_UNROLL)
    for k in range(2):
        pltpu.make_async_copy(h_ref, xs_ref.at[pl.ds(0, tm * SUBLANES)], sem).wait()


def _scatter(dest3, meta, h2, cap):
    nsteps, _, tm = dest3.shape
    return pl.pallas_call(
        _scatter_kernel,
        grid=(nsteps,),
        in_specs=[
            pl.BlockSpec((1, 2, tm), lambda i: (i, 0, 0), memory_space=pltpu.SMEM),
            pl.BlockSpec(memory_space=pltpu.SMEM),
            pl.BlockSpec((tm * SUBLANES, LANES), lambda i: (i, 0)),
        ],
        out_specs=pl.BlockSpec(memory_space=pl.ANY),
        out_shape=jax.ShapeDtypeStruct((cap * SUBLANES, LANES), h2.dtype),
        scratch_shapes=[pltpu.VMEM((TM_MOE * SUBLANES, LANES), h2.dtype), pltpu.SemaphoreType.DMA(()),
                        pltpu.SemaphoreType.DMA(())],
        compiler_params=_cparams(1),
        name="scatter",
    )(dest3, meta, h2)


def _experts_kernel(meta_ref, xs_ref, wg_hbm, wu_hbm, wd_hbm, ys_ref,
                    wg_buf, wu_buf, wd_buf, wg16, wu16, wd16, elist_ref, ord_ref, sem):
    c = pl.program_id(0)
    n_used = meta_ref[1, 0]
    e = meta_ref[0, c]
    first = (c == 0) | (e != meta_ref[0, jnp.maximum(c - 1, 0)])

    def fetch(expert, slot):
        return [pltpu.make_async_copy(w.at[expert], buf.at[slot], sem.at[slot, i])
                for i, (w, buf) in enumerate(((wg_hbm, wg_buf), (wu_hbm, wu_buf), (wd_hbm, wd_buf)))]

    @pl.when(c == 0)
    def _():
        def scan(b, cnt):
            eb = meta_ref[0, b]
            take = ((b == 0) | (eb != meta_ref[0, jnp.maximum(b - 1, 0)])) & (b < n_used)

            @pl.when(take)
            def _():
                elist_ref[cnt] = eb

            return cnt + take.astype(I32)

        cnt = lax.fori_loop(0, pl.num_programs(0), scan, 0)
        ord_ref[0] = 0
        ord_ref[1] = cnt
        for k in range(W_SLOTS - 1):
            @pl.when(k < cnt)
            def _(k=k):
                for cp in fetch(elist_ref[k], k):
                    cp.start(priority=WEIGHT_DMA_PRIORITY)

    @pl.when(first & (c < n_used))
    def _():
        @pl.when(c > 0)
        def _():
            ord_ref[0] = ord_ref[0] + 1
        k = ord_ref[0]
        for cp in fetch(e, k % W_SLOTS):
            cp.wait()
        ahead = k + (W_SLOTS - 1)

        @pl.when(ahead < ord_ref[1])
        def _():
            for cp in fetch(elist_ref[jnp.minimum(ahead, N_EXPERTS - 1)], ahead % W_SLOTS):
                cp.start(priority=WEIGHT_DMA_PRIORITY)

    def mlp(weights):
        lo, hi = _unpack_bf16_halves(_load_token_tiles(xs_ref))
        lo, hi = lo.astype(BF16), hi.astype(BF16)
        half = lo.shape[1]
        wg, wu, wd = weights

        def proj(w):
            return _dot(lo, w[:half, :]) + _dot(hi, w[half:, :])

        act = (jax.nn.silu(proj(wg)) * proj(wu)).astype(BF16)
        _store_token_tiles(ys_ref, _pack_bf16_halves(_dot(act, wd)))

    @pl.when(first & (c < n_used))
    def _():
        slot = ord_ref[0] % W_SLOTS
        cast = [buf[slot].astype(BF16) for buf in (wg_buf, wu_buf, wd_buf)]
        for ref16, w in zip((wg16, wu16, wd16), cast):
            ref16[...] = w
        mlp(cast)

    @pl.when(jnp.logical_not(first) & (c < n_used))
    def _():
        mlp((wg16[...], wu16[...], wd16[...]))

    @pl.when(c >= n_used)
    def _():
        ys_ref[...] = jnp.zeros(ys_ref.shape, U32)


def _experts(meta, xs, w_gate, w_up, w_down):
    rows = TM_MOE * SUBLANES
    nblk = xs.shape[0] // rows
    d, ff = w_gate.shape[1:]

    def row_map(c, meta):
        return (jnp.minimum(c, meta[1, 0] - 1), 0)

    hbm = pl.BlockSpec(memory_space=pl.ANY)
    return pl.pallas_call(
        _experts_kernel,
        grid_spec=pltpu.PrefetchScalarGridSpec(
            num_scalar_prefetch=1,
            grid=(nblk,),
            in_specs=[pl.BlockSpec((rows, LANES), row_map), hbm, hbm, hbm],
            out_specs=pl.BlockSpec((rows, LANES), lambda c, meta: (c, 0)),
            scratch_shapes=[
                pltpu.VMEM((W_SLOTS, d, ff), F32), pltpu.VMEM((W_SLOTS, d, ff), F32),
                pltpu.VMEM((W_SLOTS, ff, d), F32),
                pltpu.VMEM((d, ff), BF16), pltpu.VMEM((d, ff), BF16), pltpu.VMEM((ff, d), BF16),
                pltpu.SMEM((N_EXPERTS,), I32),
                pltpu.SMEM((2,), I32),
                pltpu.SemaphoreType.DMA((W_SLOTS, 3)),
            ],
        ),
        out_shape=jax.ShapeDtypeStruct(xs.shape, U32),
        compiler_params=_cparams(1),
        name="experts",
    )(meta, xs, w_gate, w_up, w_down)


def _combine_kernel(dest_ref, dest_next_ref, x1_ref, gate_ref, ln_ref, ys_ref, o_ref, ybuf, sem):
    i = pl.program_id(0)
    tm = x1_ref.shape[0]
    slot = i % 2

    def issue(d_ref, s):
        def start(r, c):
            for k in range(2):
                _row_copy(ys_ref, d_ref[0, k, r], ybuf.at[s, k], r, sem.at[s]).start(priority=k)
            return c

        lax.fori_loop(0, tm, start, 0, unroll=DMA_ISSUE_UNROLL)

    @pl.when(i == 0)
    def _():
        issue(dest_ref, 0)

    @pl.when(i + 1 < pl.num_programs(0))
    def _():
        issue(dest_next_ref, 1 - slot)

    for k in range(2):
        pltpu.make_async_copy(ys_ref.at[pl.ds(0, tm * SUBLANES)], ybuf.at[slot, k], sem.at[slot]).wait()
    gate = gate_ref[...]
    lo0, hi0 = _unpack_bf16_halves(_load_token_tiles(ybuf.at[slot, 0]))
    lo1, hi1 = _unpack_bf16_halves(_load_token_tiles(ybuf.at[slot, 1]))
    g0, g1 = gate[:, 0:1], gate[:, 1:2]
    moe = jnp.concatenate([lo0 * g0 + lo1 * g1, hi0 * g0 + hi1 * g1], axis=1)
    x2 = x1_ref[...] + moe
    ms = jnp.mean(x2 * x2, axis=-1, keepdims=True)
    o_ref[...] = x2 * lax.rsqrt(ms + EPS) * ln_ref[...]


def _combine(dest3, x1, gate_rows, ln_final, ys):
    n, d = x1.shape
    tm = TM_COMBINE
    last = n // tm - 1
    return pl.pallas_call(
        _combine_kernel,
        grid=(n // tm,),
        in_specs=[
            pl.BlockSpec((1, 2, tm), lambda i: (i, 0, 0), memory_space=pltpu.SMEM),
            pl.BlockSpec((1, 2, tm), lambda i: (jnp.minimum(i + 1, last), 0, 0), memory_space=pltpu.SMEM),
            pl.BlockSpec((tm, d), lambda i: (i, 0)),
            pl.BlockSpec((tm, LANES), lambda i: (i, 0)),
            pl.BlockSpec((1, d), lambda i: (0, 0)),
            pl.BlockSpec(memory_space=pl.ANY),
        ],
        out_specs=pl.BlockSpec((tm, d), lambda i: (i, 0)),
        out_shape=jax.ShapeDtypeStruct((n, d), F32),
        scratch_shapes=[pltpu.VMEM((2, 2, tm * SUBLANES, LANES), U32), pltpu.SemaphoreType.DMA((2,))],
        compiler_params=_cparams(1),
        name="combine",
    )(dest3, dest3, x1, gate_rows, ln_final, ys)


def kernel(x, positions, ln_mix, w_in, pe_cmp_k, w_cmp_k1, w_cmp_k2, pe_cmp_v, w_cmp_v1, w_cmp_v2, w_pool, b_pool, pool_scale, gn_nsa, gn_pool, w_out, ln_moe, w_router_group, b_router_group, w_router_expert, b_router_expert, w_gate, w_up, w_down, ln_final):
    batch, seq, d = x.shape
    n = batch * seq
    depth = w_in.shape[0]
    nsa_w = N_HEADS * HEAD_DIM
    kv_w = N_KV * HEAD_DIM
    cut_gates = nsa_w + 6 * kv_w
    cut_pool = cut_gates + N_GATES

    inv_freq = ROPE_THETA ** (-jnp.arange(0, ROT_DIM, 2, dtype=F32) / ROT_DIM)
    ang = positions.astype(F32).reshape(n, 1) * inv_freq[None, :]
    cos, sin = jnp.cos(ang), jnp.sin(ang)
    rest = HEAD_DIM - ROT_DIM
    trig = jnp.stack([jnp.concatenate([cos, cos, jnp.ones((n, rest), F32)], axis=1),
                      jnp.concatenate([-sin, sin, jnp.zeros((n, rest), F32)], axis=1)])
    trigt = jnp.transpose(trig, (0, 2, 1))
    kblk = (jnp.arange(seq, dtype=I32)[:, None] // SEL_LEN == jnp.arange(LANES, dtype=I32)[None, :]).astype(BF16)

    x2d = x.reshape(n, d)
    assert depth == 1, "single-layer operation"
    for l in range(depth):
        wi = w_in[l]
        o_kc, o_ks = nsa_w, nsa_w + 2 * kv_w
        o_vs, o_kw, o_vw = o_ks + kv_w, o_ks + 2 * kv_w, o_ks + 3 * kv_w
        gate_rows_pad = -(-N_GATES // SUBLANES) * SUBLANES
        wq = wi[:, :nsa_w].astype(BF16)
        wv = jnp.concatenate([wi[:, o_vs:o_kw], wi[:, o_vw:cut_gates]], axis=1).astype(BF16)
        wg = jnp.pad(wi[:, cut_gates:cut_pool], ((0, 0), (0, LANES - N_GATES))).astype(BF16)
        wkc = wi[:, o_kc:o_ks].astype(BF16)
        wk = jnp.concatenate([wi[:, o_ks:o_vs], wi[:, o_kw:o_vw]], axis=1).astype(BF16)
        wu = wi[:, cut_pool:].astype(BF16)
        qnt, qrt, kvc, kk, vt, gt, u = _inproj(x2d, ln_mix[l][None], trig, trigt, wq, wv, wg, wkc, wk, wu,
                                               gate_rows_pad, batch, seq)

        half = CMP_LEN * HEAD_DIM // 2
        def stack_w1(w):
            return jnp.concatenate([w[:half], w[half:]], axis=1)
        w1 = jnp.stack([stack_w1(w_cmp_k1[l]), stack_w1(w_cmp_v1[l])]).astype(BF16)
        w2 = jnp.stack([w_cmp_k2[l], w_cmp_v2[l]]).astype(BF16)
        def pe_rows(pe):
            return jnp.pad(pe.reshape(2, half), ((0, SUBLANES - 2), (0, 0)))
        pe2 = jnp.stack([pe_rows(pe_cmp_k[l]), pe_rows(pe_cmp_v[l])]).astype(BF16)
        cmp = _compress(kvc, pe2, w1, w2)

        mix_nsa = _nsa(qnt, qrt, cmp, kk, vt, kblk, gt, gn_nsa[l][None])
        mix_pool = _pool(u.reshape(batch, seq, -1), w_pool[l].astype(BF16), b_pool[l][None],
                         pool_scale[l][None], gn_pool[l][None])

        ng, ne = N_EXPERT_GROUPS, EXPERTS_PER_GROUP
        w_r = jnp.concatenate([
            w_router_group[l].T, jnp.zeros((ne - ng, d), F32),
            jnp.transpose(w_router_expert[l], (0, 2, 1)).reshape(ng * ne, d),
            jnp.zeros((LANES - ne - ng * ne, d), F32)], axis=0).astype(BF16)
        b_r = jnp.concatenate([
            b_router_group[l], jnp.zeros((ne - ng,), F32), b_router_expert[l].reshape(-1),
            jnp.zeros((LANES - ne - ng * ne,), F32)])
        b_r = jnp.broadcast_to(b_r[:, None], (LANES, LANES))
        x1, h2, eid, gate = _outproj(mix_nsa, mix_pool, x2d, w_out[l].astype(BF16), ln_moe[l][None], w_r, b_r)

        dest, meta = _dispatch(eid)
        cap = (2 * n // TM_MOE + N_EXPERTS) * TM_MOE
        def per_tile(tm):
            return jnp.transpose(dest[:2].reshape(2, n // tm, tm), (1, 0, 2))
        xs = _scatter(per_tile(TS_SCATTER), meta, h2, cap)
        ys = _experts(meta, xs, w_gate[l], w_up[l], w_down[l])
        dest_tok = per_tile(TM_COMBINE)
        gate_rows = jnp.pad(gate[:2].T, ((0, 0), (0, LANES - 2)))
        x2d = _combine(dest_tok, x1, gate_rows, ln_final[None], ys)
    return x2d.reshape(batch, seq, d)
```

```python
import functools

import jax
import jax.numpy as jnp
from jax import lax
from jax.experimental import pallas as pl
from jax.experimental.pallas import tpu as pltpu

F32 = jnp.float32
BF16 = jnp.bfloat16
I32 = jnp.int32

HEAD_DIM = 128
N_HEADS = 8
N_KV = 2
GQA = N_HEADS // N_KV
N_GATES = 3 * N_HEADS
ROT_DIM = HEAD_DIM // 4
ROPE_THETA = 500000.0
CMP_LEN = 32
CMP_STRIDE = 16
CMP_HIDDEN = 2 * HEAD_DIM
SEL_LEN = 64
SEL_TOPK = 16
N_LOCAL = 2
WINDOW = 512
POOL_SIZES = (2, 4, 8, 16)
N_EXPERT_GROUPS = 4
EXPERTS_PER_GROUP = 8
N_EXPERTS = N_EXPERT_GROUPS * EXPERTS_PER_GROUP
EPS = 1e-6
NEG = -1e30
BIG = 1e30
SCALE = HEAD_DIM ** -0.5
LOG2E = 1.4426950408889634
Q_SCALE = SCALE * LOG2E

LANES = 128
SUBLANES = 8
VMEM_LIMIT_BYTES = 56 * 1024 * 1024

TM_PROJ = 256
TQ = 256
TK = 256
SEL_MASK_BIAS = 32768.0
V_ONES_ROWS = 16
TM_MOE = 256
TS_SCATTER = 512
TM_COMBINE = 128
WEIGHT_DMA_PRIORITY = 1
W_SLOTS = 3
DMA_ISSUE_UNROLL = 8
T_RANK = 512


def _cparams(n_axes, flags=None):
    return pltpu.CompilerParams(dimension_semantics=("arbitrary",) * n_axes,
                                vmem_limit_bytes=VMEM_LIMIT_BYTES, flags=flags)


def _dot(a, b):
    return jnp.dot(a, b, preferred_element_type=F32)


def _dot_nt(a, b):
    return lax.dot_general(a, b, (((1,), (1,)), ((), ())), preferred_element_type=F32)


U32 = jnp.uint32
_HI16 = 0xFFFF0000


def _pack_bf16_halves(x):
    c = x.shape[1] // 2
    bits = lambda t: lax.bitcast_convert_type(t.astype(BF16).astype(F32), U32)
    return lax.shift_right_logical(bits(x[:, :c]), U32(16)) | (bits(x[:, c:]) & U32(_HI16))


def _store_token_tiles(ref, words):
    m, width = words.shape
    assert width == SUBLANES * LANES
    for j in range(SUBLANES):
        ref[pl.ds(j, m, stride=SUBLANES), :] = words[:, j * LANES:(j + 1) * LANES]


def _load_token_tiles(ref):
    m = ref.shape[0] // SUBLANES
    return jnp.concatenate([ref[pl.ds(j, m, stride=SUBLANES), :] for j in range(SUBLANES)], axis=1)


def _unpack_bf16_halves(w):
    lo = lax.bitcast_convert_type(lax.shift_left(w, U32(16)), F32)
    hi = lax.bitcast_convert_type(w & U32(_HI16), F32)
    return lo, hi


def _rope(t, cos, sin, axis):
    idx = lax.broadcasted_iota(I32, t.shape, axis)
    partner = jnp.where(idx < ROT_DIM // 2,
                        pltpu.roll(t, HEAD_DIM - ROT_DIM // 2, axis),
                        pltpu.roll(t, ROT_DIM // 2, axis))
    return t * cos + partner * sin


def _inproj_kernel(x_ref, ln_ref, trig_ref, trigt_ref, wq_ref, wv_ref, wg_ref, wkc_ref, wk_ref, wu_ref,
                   qnt_ref, qrt_ref, kvc_ref, kk_ref, vt_ref, gt_ref, u_ref, kc_scr):
    x = x_ref[...]
    ms = jnp.mean(x * x, axis=-1, keepdims=True)
    hb = (x * lax.rsqrt(ms + EPS) * ln_ref[...]).astype(BF16)
    chunk = 4 * HEAD_DIM

    cost, sint = trigt_ref[0], trigt_ref[1]
    for ch in range(2):
        r = _dot(hb, wq_ref[:, ch * chunk:(ch + 1) * chunk])
        for j in range(4):
            t = r[:, j * HEAD_DIM:(j + 1) * HEAD_DIM].T
            qnt_ref[0, ch * 4 + j] = (t * Q_SCALE).astype(BF16)
            qrt_ref[0, ch * 4 + j] = (_rope(t, cost, sint, 0) * Q_SCALE).astype(BF16)
    r = _dot(hb, wv_ref[...])
    ones = jnp.ones((V_ONES_ROWS, r.shape[0]), BF16)
    for j in range(4):
        vt_ref[0, j, 0] = jnp.concatenate([r[:, j * HEAD_DIM:(j + 1) * HEAD_DIM].T.astype(BF16), ones], axis=0)
    gt_ref[...] = _dot(hb, wg_ref[...]).T[:gt_ref.shape[0], :]

    r = _dot(hb, wkc_ref[...])
    groups = kvc_ref.shape[2]
    for j in range(4):
        kc_scr[j] = r[:, j * HEAD_DIM:(j + 1) * HEAD_DIM]
        for i in range(CMP_STRIDE):
            kvc_ref[0, j, :, i * HEAD_DIM:(i + 1) * HEAD_DIM] = (
                kc_scr[j, pl.ds(i, groups, stride=CMP_STRIDE), :].astype(BF16))
    cos, sin = trig_ref[0], trig_ref[1]
    r = _dot(hb, wk_ref[...])
    for j in range(4):
        kk_ref[0, j] = _rope(r[:, j * HEAD_DIM:(j + 1) * HEAD_DIM], cos, sin, 1).astype(BF16)
    for ch in range(2):
        u_ref[:, ch * chunk:(ch + 1) * chunk] = _dot(hb, wu_ref[:, ch * chunk:(ch + 1) * chunk])


def _inproj(x2d, ln, trig, trigt, wq, wv, wg, wkc, wk, wu, gate_rows, batch, seq):
    n, d = x2d.shape
    tm = TM_PROJ
    assert tm == TK
    spb = seq // tm
    const = lambda i: (0, 0)
    resident = lambda w: pl.BlockSpec(w.shape, const, pipeline_mode=pl.Buffered(1))
    row_heads = lambda i: (i // spb, 0, i % spb, 0)
    col_heads = lambda i: (i // spb, 0, 0, i % spb)
    return pl.pallas_call(
        _inproj_kernel,
        grid=(n // tm,),
        in_specs=[
            pl.BlockSpec((tm, d), lambda i: (i, 0)),
            pl.BlockSpec((1, d), const),
            pl.BlockSpec((2, tm, HEAD_DIM), lambda i: (0, i, 0)),
            pl.BlockSpec((2, HEAD_DIM, tm), lambda i: (0, 0, i)),
            resident(wq), resident(wv), resident(wg), resident(wkc), resident(wk), resident(wu),
        ],
        out_specs=[
            pl.BlockSpec((1, N_HEADS, HEAD_DIM, tm), col_heads),
            pl.BlockSpec((1, N_HEADS, HEAD_DIM, tm), col_heads),
            pl.BlockSpec((1, 4, tm // CMP_STRIDE, CMP_STRIDE * HEAD_DIM), row_heads),
            pl.BlockSpec((1, 4, tm, HEAD_DIM), row_heads),
            pl.BlockSpec((1, 4, 1, HEAD_DIM + V_ONES_ROWS, tm), lambda i: (i // spb, 0, i % spb, 0, 0)),
            pl.BlockSpec((gate_rows, tm), lambda i: (0, i)),
            pl.BlockSpec((tm, wu.shape[1]), lambda i: (i, 0)),
        ],
        out_shape=[
            jax.ShapeDtypeStruct((batch, N_HEADS, HEAD_DIM, seq), BF16),
            jax.ShapeDtypeStruct((batch, N_HEADS, HEAD_DIM, seq), BF16),
            jax.ShapeDtypeStruct((batch, 4, seq // CMP_STRIDE, CMP_STRIDE * HEAD_DIM), BF16),
            jax.ShapeDtypeStruct((batch, 4, seq, HEAD_DIM), BF16),
            jax.ShapeDtypeStruct((batch, 4, spb, HEAD_DIM + V_ONES_ROWS, tm), BF16),
            jax.ShapeDtypeStruct((gate_rows, n), F32),
            jax.ShapeDtypeStruct((n, wu.shape[1]), F32),
        ],
        scratch_shapes=[pltpu.VMEM((4, tm, HEAD_DIM), F32)],
        compiler_params=_cparams(1),
        name="inproj",
    )(x2d, ln, trig, trigt, wq, wv, wg, wkc, wk, wu)


def _compress_kernel(t_ref, pe_ref, w1_ref, w2_ref, o_ref):
    t2 = t_ref[0, 0]
    w1 = w1_ref[0]
    a = _dot(t2, w1)
    pb = _dot(pe_ref[0], w1)
    bias = pb[0:1, :CMP_HIDDEN] + pb[1:2, CMP_HIDDEN:]
    nrow = a.shape[0]
    nxt = pltpu.roll(a[:, CMP_HIDDEN:], nrow - 1, 0)
    hid = jax.nn.gelu(a[:, :CMP_HIDDEN] + nxt + bias)
    out = _dot(hid.astype(BF16), w2_ref[0])
    row = lax.broadcasted_iota(I32, out.shape, 0)
    out = jnp.where(row < nrow - 1, out, 0.0)
    is_value = pl.program_id(1) >= N_KV
    o_ref[0, 0] = jnp.where(is_value, out.T, out).astype(BF16)


def _compress(kvc2, pe2, w1, w2):
    batch, four, nrow, width = kvc2.shape
    return pl.pallas_call(
        _compress_kernel,
        grid=(batch, four),
        in_specs=[
            pl.BlockSpec((1, 1, nrow, width), lambda b, j: (b, j, 0, 0)),
            pl.BlockSpec((1, SUBLANES, width), lambda b, j: (j // N_KV, 0, 0)),
            pl.BlockSpec((1, width, 2 * CMP_HIDDEN), lambda b, j: (j // N_KV, 0, 0)),
            pl.BlockSpec((1, CMP_HIDDEN, HEAD_DIM), lambda b, j: (j // N_KV, 0, 0)),
        ],
        out_specs=pl.BlockSpec((1, 1, nrow, HEAD_DIM), lambda b, j: (b, j, 0, 0)),
        out_shape=jax.ShapeDtypeStruct((batch, four, nrow, HEAD_DIM), BF16),
        compiler_params=_cparams(2),
        name="compress",
    )(kvc2, pe2, w1, w2)


def _nsa_kernel(qnt_ref, qrt_ref, cmp_ref, kk_ref, vt_ref, kblk_ref, gt_ref, gn_ref, o_ref,
                m_scr, acc_scr, qa_scr, ocmp_scr, s_scr, o_scr):
    qi = pl.program_id(1)
    tq = qnt_ref.shape[3]
    rows = GQA * tq
    n_cmp_pad = cmp_ref.shape[2]
    n_sel = kk_ref.shape[2] // SEL_LEN
    q0 = qi * tq

    m_scr[...] = jnp.full(m_scr.shape, NEG, F32)
    acc_scr[...] = jnp.zeros(acc_scr.shape, F32)

    def scores(item, slot):
        c, g, q_rows, k_tile, v_t, bias = item
        s_scr[slot] = _dot(k_tile, qa_scr[g, :q_rows, :])

    def softmax_pv(item, slot):
        c, g, q_rows, k_tile, v_t, bias = item
        s = s_scr[slot]
        if bias is not None:
            s = s + bias
        m_old = m_scr[c]
        m_new = jnp.maximum(m_old, jnp.max(s, axis=0, keepdims=True))
        alpha = jnp.exp2(m_old - m_new)
        p = jnp.exp2(s - m_new)
        acc_scr[c] = alpha * acc_scr[c] + _dot(v_t, p.astype(BF16))
        m_scr[c] = m_new

    def flash_out(c):
        return acc_scr[c, :HEAD_DIM, :] / acc_scr[c, HEAD_DIM:HEAD_DIM + 1, :]

    def tile_heads(a):
        return jnp.concatenate([a] * GQA, axis=1)

    kk_i = lax.broadcasted_iota(I32, (TK, tq), 0)
    qq_i = lax.broadcasted_iota(I32, (TK, tq), 1)
    causal_bias = tile_heads(jnp.where(kk_i <= qq_i, 0.0, NEG))
    band_bias = tile_heads(jnp.where(kk_i > qq_i, 0.0, NEG))
    n_win = WINDOW // TK

    for g in range(N_KV):
        qn4 = jnp.concatenate([qnt_ref[0, h] for h in range(g * GQA, (g + 1) * GQA)], axis=1)
        s = _dot(cmp_ref[0, g], qn4)
        n_idx = lax.broadcasted_iota(I32, (n_cmp_pad, rows), 0)
        t_idx = q0 + jnp.bitwise_and(lax.broadcasted_iota(I32, (n_cmp_pad, rows), 1), tq - 1)
        cmp_ok = n_idx * CMP_STRIDE + (CMP_LEN - 1) <= t_idx
        s = jnp.where(cmp_ok, s, NEG)
        m = jnp.max(s, axis=0, keepdims=True)
        e = jnp.where(cmp_ok, jnp.exp2(s - m), 0.0)
        l = jnp.sum(e, axis=0, keepdims=True)
        p = e * jnp.where(l > 0.0, 1.0 / l, 0.0)
        pb = p.astype(BF16)
        ocmp_scr[g] = _dot(cmp_ref[0, N_KV + g], pb)

        jn = lax.broadcasted_iota(I32, (n_sel, n_cmp_pad), 0) * SEL_LEN
        cn = lax.broadcasted_iota(I32, (n_sel, n_cmp_pad), 1) * CMP_STRIDE
        ov = jnp.clip(jnp.minimum(cn + CMP_LEN, jn + SEL_LEN) - jnp.maximum(cn, jn), 0, None)
        w_t = (ov.astype(F32) * (1.0 / CMP_LEN)).astype(BF16)
        imp4 = _dot(w_t, pb)
        imp = imp4[:, 0:tq]
        for r in range(1, GQA):
            imp = imp + imp4[:, r * tq:(r + 1) * tq]
        j_blk = lax.broadcasted_iota(I32, (n_sel, tq), 0)
        t_q = q0 + lax.broadcasted_iota(I32, (n_sel, tq), 1)
        sel_ok = j_blk * SEL_LEN <= t_q
        back = t_q // SEL_LEN - j_blk
        forced = (j_blk == 0) | ((back >= 0) & (back < N_LOCAL))
        val = jnp.where(sel_ok & forced, BIG, jnp.where(sel_ok, imp, -BIG))
        rank = jnp.zeros((n_sel, tq), F32)
        for i in range(n_sel):
            vi = val[i:i + 1, :]
            beats = (vi > val) | ((vi == val) & (j_blk > i))
            rank = rank + beats.astype(F32)
        keep = (rank < float(min(SEL_TOPK, n_sel))) & sel_ok
        bias_t = jnp.where(keep, 0.0, -SEL_MASK_BIAS)
        bias_t = jnp.concatenate([bias_t, jnp.zeros((HEAD_DIM - n_sel, tq), F32)], axis=0)
        qa_scr[g, :HEAD_DIM, :] = jnp.concatenate(
            [qrt_ref[0, h] for h in range(g * GQA, (g + 1) * GQA)], axis=1)
        qa_scr[g, HEAD_DIM:, :] = tile_heads(bias_t.astype(BF16))

    def sel_item(g, kj, bias):
        st = pl.multiple_of(kj * TK, TK)
        k_aug = jnp.concatenate([kk_ref[0, g, pl.ds(st, TK), :], kblk_ref[pl.ds(st, TK), :]], axis=1)
        return (g, g, 2 * HEAD_DIM, k_aug, vt_ref[0, g, kj], bias)

    def win_item(g, back):
        kj = jnp.maximum(qi - back, 0)
        st = pl.multiple_of(kj * TK, TK)
        off = jnp.where(qi >= back, 0.0, NEG)
        bias = causal_bias if back == 0 else (band_bias + off if back == n_win else off)
        return (N_KV + g, g, HEAD_DIM, kk_ref[0, N_KV + g, pl.ds(st, TK), :], vt_ref[0, N_KV + g, kj], bias)

    assert N_KV == 2
    scores(sel_item(0, 0, None), 0)

    def sel_body(kj, carry):
        scores(sel_item(1, kj, None), 1)
        softmax_pv(sel_item(0, kj, None), 0)
        scores(sel_item(0, kj + 1, None), 0)
        softmax_pv(sel_item(1, kj, None), 1)
        return carry

    lax.fori_loop(0, qi, sel_body, 0)
    tail = ([sel_item(g, qi, causal_bias) for g in range(N_KV)]
            + [win_item(g, back) for back in range(n_win, -1, -1) for g in range(N_KV)])
    for i, item in enumerate(tail):
        if i + 1 < len(tail):
            scores(tail[i + 1], (i + 1) % 2)
        softmax_pv(item, i % 2)

    gsig = jax.nn.sigmoid(gt_ref[...])
    for g in range(N_KV):
        o_cmp, o_sel, o_win = ocmp_scr[g], flash_out(g), flash_out(N_KV + g)
        for r in range(GQA):
            h = g * GQA + r
            sl = slice(r * tq, (r + 1) * tq)
            o_h = (gsig[3 * h:3 * h + 1, :] * o_cmp[:, sl]
                   + gsig[3 * h + 1:3 * h + 2, :] * o_sel[:, sl]
                   + gsig[3 * h + 2:3 * h + 3, :] * o_win[:, sl])
            o_scr[:, h * HEAD_DIM:(h + 1) * HEAD_DIM] = o_h.T

    o = o_scr[...]
    ms = jnp.mean(o * o, axis=-1, keepdims=True)
    o_ref[...] = (o * lax.rsqrt(ms + EPS) * gn_ref[...]).astype(BF16)


def _nsa(qnt, qrt, cmp, kk, vt, kblk, gt, gn):
    batch, _, _, seq = qnt.shape
    assert TQ == TK and WINDOW % TK == 0 and TQ & (TQ - 1) == 0
    nq = seq // TQ
    width = N_HEADS * HEAD_DIM
    return pl.pallas_call(
        _nsa_kernel,
        grid=(batch, nq),
        in_specs=[
            pl.BlockSpec((1, N_HEADS, HEAD_DIM, TQ), lambda b, i: (b, 0, 0, i)),
            pl.BlockSpec((1, N_HEADS, HEAD_DIM, TQ), lambda b, i: (b, 0, 0, i)),
            pl.BlockSpec((1,) + cmp.shape[1:], lambda b, i: (b, 0, 0, 0)),
            pl.BlockSpec((1,) + kk.shape[1:], lambda b, i: (b, 0, 0, 0)),
            pl.BlockSpec((1,) + vt.shape[1:], lambda b, i: (b, 0, 0, 0, 0)),
            pl.BlockSpec((seq, LANES), lambda b, i: (0, 0)),
            pl.BlockSpec((gt.shape[0], TQ), lambda b, i: (0, b * nq + i)),
            pl.BlockSpec((1, width), lambda b, i: (0, 0)),
        ],
        out_specs=pl.BlockSpec((TQ, width), lambda b, i: (b * nq + i, 0)),
        out_shape=jax.ShapeDtypeStruct((batch * seq, width), BF16),
        scratch_shapes=[
            pltpu.VMEM((2 * N_KV, 1, GQA * TQ), F32),
            pltpu.VMEM((2 * N_KV, vt.shape[3], GQA * TQ), F32),
            pltpu.VMEM((N_KV, 2 * HEAD_DIM, GQA * TQ), BF16),
            pltpu.VMEM((N_KV, HEAD_DIM, GQA * TQ), F32),
            pltpu.VMEM((2, TK, GQA * TQ), F32),
            pltpu.VMEM((TQ, width), F32),
        ],
        compiler_params=_cparams(2),
        name="nsa",
    )(qnt, qrt, cmp, kk, vt, kblk, gt, gn)


def _pool_kernel(u_ref, w_ref, b_ref, sc_ref, gn_ref, o_ref, y_scr, halo_scr):
    si = pl.program_id(1)
    tm = u_ref.shape[1]
    halo = halo_scr.shape[0]
    cur = u_ref[0]

    @pl.when(si == 0)
    def _():
        halo_scr[...] = jnp.zeros(halo_scr.shape, F32)

    ext = jnp.concatenate([halo_scr[...], cur], axis=0)
    halo_scr[...] = cur[tm - halo:, :]
    t1 = (si * tm + 1 + lax.broadcasted_iota(I32, (tm, 1), 0)).astype(F32)
    cg = cur.shape[1] // len(POOL_SIZES)
    for gi, w in enumerate(POOL_SIZES):
        sl = slice(gi * cg, (gi + 1) * cg)
        acc = ext[:, sl]
        span = 1
        while span < w:
            acc = acc + jnp.concatenate([jnp.zeros((span, cg), F32), acc[:-span]], axis=0)
            span *= 2
        mean = acc[halo:] / jnp.minimum(t1, float(w))
        d = (mean - cur[:, sl]).astype(BF16)
        y = _dot(d, w_ref[gi]) + b_ref[:, sl]
        y_scr[:, sl] = y * sc_ref[:, sl]
    y = y_scr[...]
    ms = jnp.mean(y * y, axis=-1, keepdims=True)
    o_ref[...] = (y * lax.rsqrt(ms + EPS) * gn_ref[...]).astype(BF16)


def _pool(u3, w_pool, b_pool, pool_scale, gn_pool):
    batch, seq, c = u3.shape
    tm = TM_PROJ
    ns = seq // tm
    vec = lambda b, i: (0, 0)
    return pl.pallas_call(
        _pool_kernel,
        grid=(batch, ns),
        in_specs=[
            pl.BlockSpec((1, tm, c), lambda b, i: (b, i, 0)),
            pl.BlockSpec(w_pool.shape, lambda b, i: (0, 0, 0)),
            pl.BlockSpec((1, c), vec),
            pl.BlockSpec((1, c), vec),
            pl.BlockSpec((1, c), vec),
        ],
        out_specs=pl.BlockSpec((tm, c), lambda b, i: (b * ns + i, 0)),
        out_shape=jax.ShapeDtypeStruct((batch * seq, c), BF16),
        scratch_shapes=[pltpu.VMEM((tm, c), F32), pltpu.VMEM((max(POOL_SIZES), c), F32)],
        compiler_params=_cparams(2),
        name="pool",
    )(u3, w_pool, b_pool, pool_scale, gn_pool)


def _outproj_kernel(mn_ref, mp_ref, x_ref, w_ref, ln_ref, wr_ref, br_ref,
                    x1_ref, h2_ref, eid_ref, gate_ref, x1_scr, ssq_scr):
    tm, d = x_ref.shape
    half = mn_ref.shape[1]
    chunk = 512

    @pl.when(pl.program_id(0) == 0)
    def _():
        x1_scr[...] = jnp.zeros(x1_scr.shape, F32)
        ssq_scr[...] = jnp.zeros(ssq_scr.shape, F32)

    h2 = x1_scr[...] * lax.rsqrt(ssq_scr[...] * (1.0 / d) + EPS) * ln_ref[...]
    _store_token_tiles(h2_ref, _pack_bf16_halves(h2))

    logits = _dot_nt(wr_ref[...], h2.astype(BF16)) + br_ref[:, 0:1]
    ng, ne = N_EXPERT_GROUPS, EXPERTS_PER_GROUP
    row = lax.broadcasted_iota(I32, (ne, tm), 0)
    lg = jnp.where(row < ng, logits[0:ne], NEG)
    mg = jnp.max(lg, axis=0, keepdims=True)
    g_sel = jnp.min(jnp.where(lg == mg, row, ne), axis=0, keepdims=True)
    p_g = 1.0 / jnp.sum(jnp.exp(lg - mg), axis=0, keepdims=True)
    le = jnp.zeros((ne, tm), F32)
    for g in range(ng):
        le = jnp.where(g_sel == g, logits[ne * (g + 1):ne * (g + 2)], le)
    ex = jnp.exp(le - jnp.max(le, axis=0, keepdims=True))
    pe = ex / jnp.sum(ex, axis=0, keepdims=True)
    p1 = jnp.max(pe, axis=0, keepdims=True)
    i1 = jnp.min(jnp.where(pe == p1, row, ne), axis=0, keepdims=True)
    rest = jnp.where(row == i1, -1.0, pe)
    p2 = jnp.max(rest, axis=0, keepdims=True)
    i2 = jnp.min(jnp.where(rest == p2, row, ne), axis=0, keepdims=True)
    psum = p1 + p2
    zero_i = jnp.zeros((ne, tm), I32)
    eid = jnp.where(row == 0, g_sel * ne + i1, jnp.where(row == 1, g_sel * ne + i2, zero_i))
    gate = jnp.where(row == 0, p_g * p1 / psum, jnp.where(row == 1, p_g * p2 / psum, 0.0))
    eid_ref[...] = eid
    gate_ref[...] = gate

    mn = mn_ref[...]
    mp = mp_ref[...]
    ssq = jnp.zeros((tm, 1), F32)
    for c in range(d // chunk):
        sl = slice(c * chunk, (c + 1) * chunk)
        x1 = x_ref[:, sl] + (_dot(mn, w_ref[:half, sl]) + _dot(mp, w_ref[half:, sl]))
        x1_ref[:, sl] = x1
        x1_scr[:, sl] = x1
        ssq = ssq + jnp.sum(x1 * x1, axis=-1, keepdims=True)
    ssq_scr[...] = ssq


def _outproj(mix_nsa, mix_pool, x2d, w_out, ln_moe, w_router_t, b_router):
    n, d = x2d.shape
    tm = TM_PROJ
    half = mix_nsa.shape[1]
    const = lambda i: (0, 0)
    last = n // tm - 1
    cur = lambda i: (jnp.minimum(i, last), 0)
    prev = lambda i: (jnp.maximum(i - 1, 0), 0)
    prev_t = lambda i: (0, jnp.maximum(i - 1, 0))
    return pl.pallas_call(
        _outproj_kernel,
        grid=(n // tm + 1,),
        in_specs=[
            pl.BlockSpec((tm, half), cur),
            pl.BlockSpec((tm, half), cur),
            pl.BlockSpec((tm, d), cur),
            pl.BlockSpec(w_out.shape, const, pipeline_mode=pl.Buffered(1)),
            pl.BlockSpec((1, d), const),
            pl.BlockSpec(w_router_t.shape, const),
            pl.BlockSpec(b_router.shape, const),
        ],
        out_specs=[
            pl.BlockSpec((tm, d), cur),
            pl.BlockSpec((tm * SUBLANES, LANES), prev),
            pl.BlockSpec((SUBLANES, tm), prev_t),
            pl.BlockSpec((SUBLANES, tm), prev_t),
        ],
        scratch_shapes=[pltpu.VMEM((tm, d), F32), pltpu.VMEM((tm, 1), F32)],
        out_shape=[
            jax.ShapeDtypeStruct((n, d), F32),
            jax.ShapeDtypeStruct((n * SUBLANES, LANES), U32),
            jax.ShapeDtypeStruct((SUBLANES, n), I32),
            jax.ShapeDtypeStruct((SUBLANES, n), F32),
        ],
        compiler_params=_cparams(1),
        name="outproj",
    )(mix_nsa, mix_pool, x2d, w_out, ln_moe, w_router_t, b_router)


def _dispatch_kernel(eid_ref, dest_ref, meta_ref, rank_scr):
    n = eid_ref.shape[1]
    t = T_RANK
    row = lax.broadcasted_iota(I32, (N_EXPERTS, t), 0)
    before = (lax.broadcasted_iota(I32, (t, t), 0) < lax.broadcasted_iota(I32, (t, t), 1)).astype(BF16)
    carry = jnp.zeros((N_EXPERTS, 1), F32)
    for k in range(2):
        for j in range(n // t):
            sl = slice(j * t, (j + 1) * t)
            oh = row == eid_ref[k:k + 1, sl]
            ohf = oh.astype(F32)
            prior = _dot(ohf.astype(BF16), before) + carry
            rank_scr[k:k + 1, sl] = jnp.sum(jnp.where(oh, prior, 0.0), axis=0, keepdims=True)
            carry = carry + jnp.sum(ohf, axis=1, keepdims=True)
    counts = jnp.broadcast_to(carry, (N_EXPERTS, LANES))
    padded = jnp.floor((counts + (TM_MOE - 1.0)) * (1.0 / TM_MOE)) * TM_MOE
    ends = padded
    r_idx = lax.broadcasted_iota(I32, (N_EXPERTS, LANES), 0)
    sh = 1
    while sh < N_EXPERTS:
        ends = ends + jnp.where(r_idx >= sh, pltpu.roll(ends, sh, 0), 0.0)
        sh *= 2
    starts = (ends - padded)[:, 0:1]
    for k in range(2):
        for j in range(n // t):
            sl = slice(j * t, (j + 1) * t)
            oh = row == eid_ref[k:k + 1, sl]
            base = jnp.sum(jnp.where(oh, starts, 0.0), axis=0, keepdims=True)
            dest_ref[k:k + 1, sl] = (rank_scr[k:k + 1, sl] + base).astype(I32)
    for k in range(2, SUBLANES):
        dest_ref[k:k + 1, :] = jnp.zeros((1, n), I32)
    blk_start = lax.broadcasted_iota(I32, (N_EXPERTS, LANES), 1).astype(F32) * TM_MOE
    blk_expert = jnp.sum((ends <= blk_start).astype(F32), axis=0, keepdims=True)
    blk_expert = jnp.minimum(blk_expert, N_EXPERTS - 1.0)
    n_used = ends[N_EXPERTS - 1:N_EXPERTS, :] * (1.0 / TM_MOE)
    on_diag = r_idx == lax.broadcasted_iota(I32, (N_EXPERTS, LANES), 1)
    end_blk = jnp.sum(jnp.where(on_diag, ends, 0.0), axis=0, keepdims=True) * (1.0 / TM_MOE)
    n_blk = jnp.sum(jnp.where(on_diag, padded, 0.0), axis=0, keepdims=True) * (1.0 / TM_MOE)
    r8 = lax.broadcasted_iota(I32, (SUBLANES, LANES), 0)
    meta = jnp.where(r8 == 0, blk_expert,
                     jnp.where(r8 == 1, n_used, jnp.where(r8 == 2, end_blk, jnp.where(r8 == 3, n_blk, 0.0))))
    meta_ref[...] = meta.astype(I32)


def _dispatch(eid):
    n = eid.shape[1]
    return pl.pallas_call(
        _dispatch_kernel,
        out_shape=[jax.ShapeDtypeStruct((SUBLANES, n), I32),
                   jax.ShapeDtypeStruct((SUBLANES, LANES), I32)],
        scratch_shapes=[pltpu.VMEM((SUBLANES, n), F32)],
        compiler_params=pltpu.CompilerParams(vmem_limit_bytes=VMEM_LIMIT_BYTES),
        name="dispatch",
    )(eid)


def _row_copy(src_ref, src_row, dst_ref, dst_row, sem):
    tile = lambda ref, row: ref.at[pl.ds(pl.multiple_of(row * SUBLANES, SUBLANES), SUBLANES)]
    return pltpu.make_async_copy(tile(src_ref, src_row), tile(dst_ref, dst_row), sem)


def _scatter_kernel(dest_ref, meta_ref, h_ref, xs_ref, zero_buf, sem, zsem):
    tm = h_ref.shape[0] // SUBLANES
    blk = zero_buf.shape[0]

    @pl.when(pl.program_id(0) == 0)
    def _():
        zero_buf[...] = jnp.zeros(zero_buf.shape, zero_buf.dtype)
        n_used = meta_ref[1, 0]

        def zero_block(b):
            return pltpu.make_async_copy(zero_buf, xs_ref.at[pl.ds(pl.multiple_of(b * blk, blk), blk)], zsem)

        def sweep(act):
            def per_expert(e, c):
                @pl.when(meta_ref[3, e] > 0)
                def _():
                    act(zero_block(meta_ref[2, e] - 1))
                return c

            def per_tail(b, c):
                @pl.when(b >= n_used)
                def _():
                    act(zero_block(b))
                return c

            lax.fori_loop(0, N_EXPERTS, per_expert, 0)
            lax.fori_loop(0, xs_ref.shape[0] // blk, per_tail, 0)

        sweep(lambda cp: cp.start())
        sweep(lambda cp: cp.wait())

    def start(r, c):
        for k in range(2):
            _row_copy(h_ref, r, xs_ref, dest_ref[0, k, r], sem).start(priority=k)
        return c

    lax.fori_loop(0, tm, start, 0, unroll=DMA_ISSUE_UNROLL)
    for k in range(2):
        pltpu.make_async_copy(h_ref, xs_ref.at[pl.ds(0, tm * SUBLANES)], sem).wait()


def _scatter(dest3, meta, h2, cap):
    nsteps, _, tm = dest3.shape
    return pl.pallas_call(
        _scatter_kernel,
        grid=(nsteps,),
        in_specs=[
            pl.BlockSpec((1, 2, tm), lambda i: (i, 0, 0), memory_space=pltpu.SMEM),
            pl.BlockSpec(memory_space=pltpu.SMEM),
            pl.BlockSpec((tm * SUBLANES, LANES), lambda i: (i, 0)),
        ],
        out_specs=pl.BlockSpec(memory_space=pl.ANY),
        out_shape=jax.ShapeDtypeStruct((cap * SUBLANES, LANES), h2.dtype),
        scratch_shapes=[pltpu.VMEM((TM_MOE * SUBLANES, LANES), h2.dtype), pltpu.SemaphoreType.DMA(()),
                        pltpu.SemaphoreType.DMA(())],
        compiler_params=_cparams(1),
        name="scatter",
    )(dest3, meta, h2)


def _experts_kernel(meta_ref, xs_ref, wg_hbm, wu_hbm, wd_hbm, ys_ref,
                    wg_buf, wu_buf, wd_buf, wg16, wu16, wd16, elist_ref, ord_ref, sem):
    c = pl.program_id(0)
    n_used = meta_ref[1, 0]
    e = meta_ref[0, c]
    first = (c == 0) | (e != meta_ref[0, jnp.maximum(c - 1, 0)])

    def fetch(expert, slot):
        return [pltpu.make_async_copy(w.at[expert], buf.at[slot], sem.at[slot, i])
                for i, (w, buf) in enumerate(((wg_hbm, wg_buf), (wu_hbm, wu_buf), (wd_hbm, wd_buf)))]

    @pl.when(c == 0)
    def _():
        def scan(b, cnt):
            eb = meta_ref[0, b]
            take = ((b == 0) | (eb != meta_ref[0, jnp.maximum(b - 1, 0)])) & (b < n_used)

            @pl.when(take)
            def _():
                elist_ref[cnt] = eb

            return cnt + take.astype(I32)

        cnt = lax.fori_loop(0, pl.num_programs(0), scan, 0)
        ord_ref[0] = 0
        ord_ref[1] = cnt
        for k in range(W_SLOTS - 1):
            @pl.when(k < cnt)
            def _(k=k):
                for cp in fetch(elist_ref[k], k):
                    cp.start(priority=WEIGHT_DMA_PRIORITY)

    @pl.when(first & (c < n_used))
    def _():
        @pl.when(c > 0)
        def _():
            ord_ref[0] = ord_ref[0] + 1
        k = ord_ref[0]
        for cp in fetch(e, k % W_SLOTS):
            cp.wait()
        ahead = k + (W_SLOTS - 1)

        @pl.when(ahead < ord_ref[1])
        def _():
            for cp in fetch(elist_ref[jnp.minimum(ahead, N_EXPERTS - 1)], ahead % W_SLOTS):
                cp.start(priority=WEIGHT_DMA_PRIORITY)

    def mlp(weights):
        lo, hi = _unpack_bf16_halves(_load_token_tiles(xs_ref))
        lo, hi = lo.astype(BF16), hi.astype(BF16)
        half = lo.shape[1]
        wg, wu, wd = weights

        def proj(w):
            return _dot(lo, w[:half, :]) + _dot(hi, w[half:, :])

        act = (jax.nn.silu(proj(wg)) * proj(wu)).astype(BF16)
        _store_token_tiles(ys_ref, _pack_bf16_halves(_dot(act, wd)))

    @pl.when(first & (c < n_used))
    def _():
        slot = ord_ref[0] % W_SLOTS
        cast = [buf[slot].astype(BF16) for buf in (wg_buf, wu_buf, wd_buf)]
        for ref16, w in zip((wg16, wu16, wd16), cast):
            ref16[...] = w
        mlp(cast)

    @pl.when(jnp.logical_not(first) & (c < n_used))
    def _():
        mlp((wg16[...], wu16[...], wd16[...]))

    @pl.when(c >= n_used)
    def _():
        ys_ref[...] = jnp.zeros(ys_ref.shape, U32)


def _experts(meta, xs, w_gate, w_up, w_down):
    rows = TM_MOE * SUBLANES
    nblk = xs.shape[0] // rows
    d, ff = w_gate.shape[1:]

    def row_map(c, meta):
        return (jnp.minimum(c, meta[1, 0] - 1), 0)

    hbm = pl.BlockSpec(memory_space=pl.ANY)
    return pl.pallas_call(
        _experts_kernel,
        grid_spec=pltpu.PrefetchScalarGridSpec(
            num_scalar_prefetch=1,
            grid=(nblk,),
            in_specs=[pl.BlockSpec((rows, LANES), row_map), hbm, hbm, hbm],
            out_specs=pl.BlockSpec((rows, LANES), lambda c, meta: (c, 0)),
            scratch_shapes=[
                pltpu.VMEM((W_SLOTS, d, ff), F32), pltpu.VMEM((W_SLOTS, d, ff), F32),
                pltpu.VMEM((W_SLOTS, ff, d), F32),
                pltpu.VMEM((d, ff), BF16), pltpu.VMEM((d, ff), BF16), pltpu.VMEM((ff, d), BF16),
                pltpu.SMEM((N_EXPERTS,), I32),
                pltpu.SMEM((2,), I32),
                pltpu.SemaphoreType.DMA((W_SLOTS, 3)),
            ],
        ),
        out_shape=jax.ShapeDtypeStruct(xs.shape, U32),
        compiler_params=_cparams(1),
        name="experts",
    )(meta, xs, w_gate, w_up, w_down)


def _combine_kernel(dest_ref, dest_next_ref, x1_ref, gate_ref, ln_ref, ys_ref, o_ref, ybuf, sem):
    i = pl.program_id(0)
    tm = x1_ref.shape[0]
    slot = i % 2

    def issue(d_ref, s):
        def start(r, c):
            for k in range(2):
                _row_copy(ys_ref, d_ref[0, k, r], ybuf.at[s, k], r, sem.at[s]).start(priority=k)
            return c

        lax.fori_loop(0, tm, start, 0, unroll=DMA_ISSUE_UNROLL)

    @pl.when(i == 0)
    def _():
        issue(dest_ref, 0)

    @pl.when(i + 1 < pl.num_programs(0))
    def _():
        issue(dest_next_ref, 1 - slot)

    for k in range(2):
        pltpu.make_async_copy(ys_ref.at[pl.ds(0, tm * SUBLANES)], ybuf.at[slot, k], sem.at[slot]).wait()
    gate = gate_ref[...]
    lo0, hi0 = _unpack_bf16_halves(_load_token_tiles(ybuf.at[slot, 0]))
    lo1, hi1 = _unpack_bf16_halves(_load_token_tiles(ybuf.at[slot, 1]))
    g0, g1 = gate[:, 0:1], gate[:, 1:2]
    moe = jnp.concatenate([lo0 * g0 + lo1 * g1, hi0 * g0 + hi1 * g1], axis=1)
    x2 = x1_ref[...] + moe
    ms = jnp.mean(x2 * x2, axis=-1, keepdims=True)
    o_ref[...] = x2 * lax.rsqrt(ms + EPS) * ln_ref[...]


def _combine(dest3, x1, gate_rows, ln_final, ys):
    n, d = x1.shape
    tm = TM_COMBINE
    last = n // tm - 1
    return pl.pallas_call(
        _combine_kernel,
        grid=(n // tm,),
        in_specs=[
            pl.BlockSpec((1, 2, tm), lambda i: (i, 0, 0), memory_space=pltpu.SMEM),
            pl.BlockSpec((1, 2, tm), lambda i: (jnp.minimum(i + 1, last), 0, 0), memory_space=pltpu.SMEM),
            pl.BlockSpec((tm, d), lambda i: (i, 0)),
            pl.BlockSpec((tm, LANES), lambda i: (i, 0)),
            pl.BlockSpec((1, d), lambda i: (0, 0)),
            pl.BlockSpec(memory_space=pl.ANY),
        ],
        out_specs=pl.BlockSpec((tm, d), lambda i: (i, 0)),
        out_shape=jax.ShapeDtypeStruct((n, d), F32),
        scratch_shapes=[pltpu.VMEM((2, 2, tm * SUBLANES, LANES), U32), pltpu.SemaphoreType.DMA((2,))],
        compiler_params=_cparams(1),
        name="combine",
    )(dest3, dest3, x1, gate_rows, ln_final, ys)


def kernel(x, positions, ln_mix, w_in, pe_cmp_k, w_cmp_k1, w_cmp_k2, pe_cmp_v, w_cmp_v1, w_cmp_v2, w_pool, b_pool, pool_scale, gn_nsa, gn_pool, w_out, ln_moe, w_router_group, b_router_group, w_router_expert, b_router_expert, w_gate, w_up, w_down, ln_final):
    batch, seq, d = x.shape
    n = batch * seq
    depth = w_in.shape[0]
    nsa_w = N_HEADS * HEAD_DIM
    kv_w = N_KV * HEAD_DIM
    cut_gates = nsa_w + 6 * kv_w
    cut_pool = cut_gates + N_GATES

    inv_freq = ROPE_THETA ** (-jnp.arange(0, ROT_DIM, 2, dtype=F32) / ROT_DIM)
    ang = positions.astype(F32).reshape(n, 1) * inv_freq[None, :]
    cos, sin = jnp.cos(ang), jnp.sin(ang)
    rest = HEAD_DIM - ROT_DIM
    trig = jnp.stack([jnp.concatenate([cos, cos, jnp.ones((n, rest), F32)], axis=1),
                      jnp.concatenate([-sin, sin, jnp.zeros((n, rest), F32)], axis=1)])
    trigt = jnp.transpose(trig, (0, 2, 1))
    kblk = (jnp.arange(seq, dtype=I32)[:, None] // SEL_LEN == jnp.arange(LANES, dtype=I32)[None, :]).astype(BF16)

    x2d = x.reshape(n, d)
    assert depth == 1, "single-layer operation"
    for l in range(depth):
        wi = w_in[l]
        o_kc, o_ks = nsa_w, nsa_w + 2 * kv_w
        o_vs, o_kw, o_vw = o_ks + kv_w, o_ks + 2 * kv_w, o_ks + 3 * kv_w
        gate_rows_pad = -(-N_GATES // SUBLANES) * SUBLANES
        wq = wi[:, :nsa_w].astype(BF16)
        wv = jnp.concatenate([wi[:, o_vs:o_kw], wi[:, o_vw:cut_gates]], axis=1).astype(BF16)
        wg = jnp.pad(wi[:, cut_gates:cut_pool], ((0, 0), (0, LANES - N_GATES))).astype(BF16)
        wkc = wi[:, o_kc:o_ks].astype(BF16)
        wk = jnp.concatenate([wi[:, o_ks:o_vs], wi[:, o_kw:o_vw]], axis=1).astype(BF16)
        wu = wi[:, cut_pool:].astype(BF16)
        qnt, qrt, kvc, kk, vt, gt, u = _inproj(x2d, ln_mix[l][None], trig, trigt, wq, wv, wg, wkc, wk, wu,
                                               gate_rows_pad, batch, seq)

        half = CMP_LEN * HEAD_DIM // 2
        def stack_w1(w):
            return jnp.concatenate([w[:half], w[half:]], axis=1)
        w1 = jnp.stack([stack_w1(w_cmp_k1[l]), stack_w1(w_cmp_v1[l])]).astype(BF16)
        w2 = jnp.stack([w_cmp_k2[l], w_cmp_v2[l]]).astype(BF16)
        def pe_rows(pe):
            return jnp.pad(pe.reshape(2, half), ((0, SUBLANES - 2), (0, 0)))
        pe2 = jnp.stack([pe_rows(pe_cmp_k[l]), pe_rows(pe_cmp_v[l])]).astype(BF16)
        cmp = _compress(kvc, pe2, w1, w2)

        mix_nsa = _nsa(qnt, qrt, cmp, kk, vt, kblk, gt, gn_nsa[l][None])
        mix_pool = _pool(u.reshape(batch, seq, -1), w_pool[l].astype(BF16), b_pool[l][None],
                         pool_scale[l][None], gn_pool[l][None])

        ng, ne = N_EXPERT_GROUPS, EXPERTS_PER_GROUP
        w_r = jnp.concatenate([
            w_router_group[l].T, jnp.zeros((ne - ng, d), F32),
            jnp.transpose(w_router_expert[l], (0, 2, 1)).reshape(ng * ne, d),
            jnp.zeros((LANES - ne - ng * ne, d), F32)], axis=0).astype(BF16)
        b_r = jnp.concatenate([
            b_router_group[l], jnp.zeros((ne - ng,), F32), b_router_expert[l].reshape(-1),
            jnp.zeros((LANES - ne - ng * ne,), F32)])
        b_r = jnp.broadcast_to(b_r[:, None], (LANES, LANES))
        x1, h2, eid, gate = _outproj(mix_nsa, mix_pool, x2d, w_out[l].astype(BF16), ln_moe[l][None], w_r, b_r)

        dest, meta = _dispatch(eid)
        cap = (2 * n // TM_MOE + N_EXPERTS) * TM_MOE
        def per_tile(tm):
            return jnp.transpose(dest[:2].reshape(2, n // tm, tm), (1, 0, 2))
        xs = _scatter(per_tile(TS_SCATTER), meta, h2, cap)
        ys = _experts(meta, xs, w_gate[l], w_up[l], w_down[l])
        dest_tok = per_tile(TM_COMBINE)
        gate_rows = jnp.pad(gate[:2].T, ((0, 0), (0, LANES - 2)))
        x2d = _combine(dest_tok, x1, gate_rows, ln_final[None], ys)
    return x2d.reshape(batch, seq, d)
```

```python
import functools

import jax
import jax.numpy as jnp
from jax import lax
from jax.experimental import pallas as pl
from jax.experimental.pallas import tpu as pltpu

F32 = jnp.float32
BF16 = jnp.bfloat16
I32 = jnp.int32

HEAD_DIM = 128
N_HEADS = 8
N_KV = 2
GQA = N_HEADS // N_KV
N_GATES = 3 * N_HEADS
ROT_DIM = HEAD_DIM // 4
ROPE_THETA = 500000.0
CMP_LEN = 32
CMP_STRIDE = 16
CMP_HIDDEN = 2 * HEAD_DIM
SEL_LEN = 64
SEL_TOPK = 16
N_LOCAL = 2
WINDOW = 512
POOL_SIZES = (2, 4, 8, 16)
N_EXPERT_GROUPS = 4
EXPERTS_PER_GROUP = 8
N_EXPERTS = N_EXPERT_GROUPS * EXPERTS_PER_GROUP
EPS = 1e-6
NEG = -1e30
BIG = 1e30
SCALE = HEAD_DIM ** -0.5
LOG2E = 1.4426950408889634
Q_SCALE = SCALE * LOG2E

LANES = 128
SUBLANES = 8
VMEM_LIMIT_BYTES = 56 * 1024 * 1024

TM_PROJ = 512
TQ = 256
TK = 256
SEL_MASK_BIAS = 32768.0
V_ONES_ROWS = 16
TM_MOE = 256
TS_SCATTER = 1024
TM_COMBINE = 512
WEIGHT_DMA_PRIORITY = 1
W_SLOTS = 3
DMA_ISSUE_UNROLL = 8
T_RANK = 512


def _cparams(n_axes, flags=None):
    return pltpu.CompilerParams(dimension_semantics=("arbitrary",) * n_axes,
                                vmem_limit_bytes=VMEM_LIMIT_BYTES, flags=flags)


def _dot(a, b):
    return jnp.dot(a, b, preferred_element_type=F32)


def _dot_nt(a, b):
    return lax.dot_general(a, b, (((1,), (1,)), ((), ())), preferred_element_type=F32)


U32 = jnp.uint32
_HI16 = 0xFFFF0000


def _pack_bf16_halves(x):
    c = x.shape[1] // 2
    bits = lambda t: lax.bitcast_convert_type(t.astype(BF16).astype(F32), U32)
    return lax.shift_right_logical(bits(x[:, :c]), U32(16)) | (bits(x[:, c:]) & U32(_HI16))


def _store_token_tiles(ref, words):
    m, width = words.shape
    assert width == SUBLANES * LANES
    for j in range(SUBLANES):
        ref[pl.ds(j, m, stride=SUBLANES), :] = words[:, j * LANES:(j + 1) * LANES]


def _load_token_tiles(ref):
    m = ref.shape[0] // SUBLANES
    return jnp.concatenate([ref[pl.ds(j, m, stride=SUBLANES), :] for j in range(SUBLANES)], axis=1)


def _unpack_bf16_halves(w):
    lo = lax.bitcast_convert_type(lax.shift_left(w, U32(16)), F32)
    hi = lax.bitcast_convert_type(w & U32(_HI16), F32)
    return lo, hi


def _rope(t, cos, sin, axis):
    idx = lax.broadcasted_iota(I32, t.shape, axis)
    partner = jnp.where(idx < ROT_DIM // 2,
                        pltpu.roll(t, HEAD_DIM - ROT_DIM // 2, axis),
                        pltpu.roll(t, ROT_DIM // 2, axis))
    return t * cos + partner * sin


def _inproj_kernel(x_ref, ln_ref, trig_ref, trigt_ref, wq_ref, wv_ref, wg_ref, wkc_ref, wk_ref, wu_ref,
                   qnt_ref, qrt_ref, kvc_ref, kk_ref, vt_ref, gt_ref, u_ref, kc_scr):
    x = x_ref[...]
    ms = jnp.mean(x * x, axis=-1, keepdims=True)
    hb = (x * lax.rsqrt(ms + EPS) * ln_ref[...]).astype(BF16)
    chunk = 4 * HEAD_DIM

    cost, sint = trigt_ref[0], trigt_ref[1]
    for ch in range(2):
        r = _dot(hb, wq_ref[:, ch * chunk:(ch + 1) * chunk])
        for j in range(4):
            t = r[:, j * HEAD_DIM:(j + 1) * HEAD_DIM].T
            qnt_ref[0, ch * 4 + j] = (t * Q_SCALE).astype(BF16)
            qrt_ref[0, ch * 4 + j] = (_rope(t, cost, sint, 0) * Q_SCALE).astype(BF16)
    r = _dot(hb, wv_ref[...])
    ones = jnp.ones((V_ONES_ROWS, TK), BF16)
    for j in range(4):
        for s in range(vt_ref.shape[2]):
            vt_ref[0, j, s] = jnp.concatenate(
                [r[s * TK:(s + 1) * TK, j * HEAD_DIM:(j + 1) * HEAD_DIM].T.astype(BF16), ones], axis=0)
    gt_ref[...] = _dot(hb, wg_ref[...]).T[:gt_ref.shape[0], :]

    r = _dot(hb, wkc_ref[...])
    groups = kvc_ref.shape[2]
    for j in range(4):
        kc_scr[j] = r[:, j * HEAD_DIM:(j + 1) * HEAD_DIM]
        for i in range(CMP_STRIDE):
            kvc_ref[0, j, :, i * HEAD_DIM:(i + 1) * HEAD_DIM] = (
                kc_scr[j, pl.ds(i, groups, stride=CMP_STRIDE), :].astype(BF16))
    cos, sin = trig_ref[0], trig_ref[1]
    r = _dot(hb, wk_ref[...])
    for j in range(4):
        kk_ref[0, j] = _rope(r[:, j * HEAD_DIM:(j + 1) * HEAD_DIM], cos, sin, 1).astype(BF16)
    for ch in range(2):
        u_ref[:, ch * chunk:(ch + 1) * chunk] = _dot(hb, wu_ref[:, ch * chunk:(ch + 1) * chunk])


def _inproj(x2d, ln, trig, trigt, wq, wv, wg, wkc, wk, wu, gate_rows, batch, seq):
    n, d = x2d.shape
    tm = TM_PROJ
    assert tm % TK == 0
    spb = seq // tm
    const = lambda i: (0, 0)
    resident = lambda w: pl.BlockSpec(w.shape, const, pipeline_mode=pl.Buffered(1))
    row_heads = lambda i: (i // spb, 0, i % spb, 0)
    col_heads = lambda i: (i // spb, 0, 0, i % spb)
    return pl.pallas_call(
        _inproj_kernel,
        grid=(n // tm,),
        in_specs=[
            pl.BlockSpec((tm, d), lambda i: (i, 0)),
            pl.BlockSpec((1, d), const),
            pl.BlockSpec((2, tm, HEAD_DIM), lambda i: (0, i, 0)),
            pl.BlockSpec((2, HEAD_DIM, tm), lambda i: (0, 0, i)),
            resident(wq), resident(wv), resident(wg), resident(wkc), resident(wk), resident(wu),
        ],
        out_specs=[
            pl.BlockSpec((1, N_HEADS, HEAD_DIM, tm), col_heads),
            pl.BlockSpec((1, N_HEADS, HEAD_DIM, tm), col_heads),
            pl.BlockSpec((1, 4, tm // CMP_STRIDE, CMP_STRIDE * HEAD_DIM), row_heads),
            pl.BlockSpec((1, 4, tm, HEAD_DIM), row_heads),
            pl.BlockSpec((1, 4, tm // TK, HEAD_DIM + V_ONES_ROWS, TK), lambda i: (i // spb, 0, i % spb, 0, 0)),
            pl.BlockSpec((gate_rows, tm), lambda i: (0, i)),
            pl.BlockSpec((tm, wu.shape[1]), lambda i: (i, 0)),
        ],
        out_shape=[
            jax.ShapeDtypeStruct((batch, N_HEADS, HEAD_DIM, seq), BF16),
            jax.ShapeDtypeStruct((batch, N_HEADS, HEAD_DIM, seq), BF16),
            jax.ShapeDtypeStruct((batch, 4, seq // CMP_STRIDE, CMP_STRIDE * HEAD_DIM), BF16),
            jax.ShapeDtypeStruct((batch, 4, seq, HEAD_DIM), BF16),
            jax.ShapeDtypeStruct((batch, 4, seq // TK, HEAD_DIM + V_ONES_ROWS, TK), BF16),
            jax.ShapeDtypeStruct((gate_rows, n), F32),
            jax.ShapeDtypeStruct((n, wu.shape[1]), F32),
        ],
        scratch_shapes=[pltpu.VMEM((4, tm, HEAD_DIM), F32)],
        compiler_params=_cparams(1),
        name="inproj",
    )(x2d, ln, trig, trigt, wq, wv, wg, wkc, wk, wu)


def _compress_kernel(t_ref, pe_ref, w1_ref, w2_ref, o_ref):
    t2 = t_ref[0, 0]
    w1 = w1_ref[0]
    a = _dot(t2, w1)
    pb = _dot(pe_ref[0], w1)
    bias = pb[0:1, :CMP_HIDDEN] + pb[1:2, CMP_HIDDEN:]
    nrow = a.shape[0]
    nxt = pltpu.roll(a[:, CMP_HIDDEN:], nrow - 1, 0)
    hid = jax.nn.gelu(a[:, :CMP_HIDDEN] + nxt + bias)
    out = _dot(hid.astype(BF16), w2_ref[0])
    row = lax.broadcasted_iota(I32, out.shape, 0)
    out = jnp.where(row < nrow - 1, out, 0.0)
    is_value = pl.program_id(1) >= N_KV
    o_ref[0, 0] = jnp.where(is_value, out.T, out).astype(BF16)


def _compress(kvc2, pe2, w1, w2):
    batch, four, nrow, width = kvc2.shape
    return pl.pallas_call(
        _compress_kernel,
        grid=(batch, four),
        in_specs=[
            pl.BlockSpec((1, 1, nrow, width), lambda b, j: (b, j, 0, 0)),
            pl.BlockSpec((1, SUBLANES, width), lambda b, j: (j // N_KV, 0, 0)),
            pl.BlockSpec((1, width, 2 * CMP_HIDDEN), lambda b, j: (j // N_KV, 0, 0)),
            pl.BlockSpec((1, CMP_HIDDEN, HEAD_DIM), lambda b, j: (j // N_KV, 0, 0)),
        ],
        out_specs=pl.BlockSpec((1, 1, nrow, HEAD_DIM), lambda b, j: (b, j, 0, 0)),
        out_shape=jax.ShapeDtypeStruct((batch, four, nrow, HEAD_DIM), BF16),
        compiler_params=_cparams(2),
        name="compress",
    )(kvc2, pe2, w1, w2)


def _nsa_kernel(qnt_ref, qrt_ref, cmp_ref, kk_ref, vt_ref, kblk_ref, gt_ref, gn_ref, o_ref,
                m_scr, acc_scr, qa_scr, ocmp_scr, s_scr, o_scr):
    qi = pl.program_id(1)
    tq = qnt_ref.shape[3]
    rows = GQA * tq
    n_cmp_pad = cmp_ref.shape[2]
    n_sel = kk_ref.shape[2] // SEL_LEN
    q0 = qi * tq

    m_scr[...] = jnp.full(m_scr.shape, NEG, F32)
    acc_scr[...] = jnp.zeros(acc_scr.shape, F32)

    def scores(item, slot):
        c, g, q_rows, k_tile, v_t, bias = item
        s_scr[slot] = _dot(k_tile, qa_scr[g, :q_rows, :])

    def softmax_pv(item, slot):
        c, g, q_rows, k_tile, v_t, bias = item
        s = s_scr[slot]
        if bias is not None:
            s = s + bias
        m_old = m_scr[c]
        m_new = jnp.maximum(m_old, jnp.max(s, axis=0, keepdims=True))
        alpha = jnp.exp2(m_old - m_new)
        p = jnp.exp2(s - m_new)
        acc_scr[c] = alpha * acc_scr[c] + _dot(v_t, p.astype(BF16))
        m_scr[c] = m_new

    def flash_out(c):
        return acc_scr[c, :HEAD_DIM, :] / acc_scr[c, HEAD_DIM:HEAD_DIM + 1, :]

    def tile_heads(a):
        return jnp.concatenate([a] * GQA, axis=1)

    kk_i = lax.broadcasted_iota(I32, (TK, tq), 0)
    qq_i = lax.broadcasted_iota(I32, (TK, tq), 1)
    causal_bias = tile_heads(jnp.where(kk_i <= qq_i, 0.0, NEG))
    band_bias = tile_heads(jnp.where(kk_i > qq_i, 0.0, NEG))
    n_win = WINDOW // TK

    for g in range(N_KV):
        qn4 = jnp.concatenate([qnt_ref[0, h] for h in range(g * GQA, (g + 1) * GQA)], axis=1)
        s = _dot(cmp_ref[0, g], qn4)
        n_idx = lax.broadcasted_iota(I32, (n_cmp_pad, rows), 0)
        t_idx = q0 + jnp.bitwise_and(lax.broadcasted_iota(I32, (n_cmp_pad, rows), 1), tq - 1)
        cmp_ok = n_idx * CMP_STRIDE + (CMP_LEN - 1) <= t_idx
        s = jnp.where(cmp_ok, s, NEG)
        m = jnp.max(s, axis=0, keepdims=True)
        e = jnp.where(cmp_ok, jnp.exp2(s - m), 0.0)
        l = jnp.sum(e, axis=0, keepdims=True)
        p = e * jnp.where(l > 0.0, 1.0 / l, 0.0)
        pb = p.astype(BF16)
        ocmp_scr[g] = _dot(cmp_ref[0, N_KV + g], pb)

        jn = lax.broadcasted_iota(I32, (n_sel, n_cmp_pad), 0) * SEL_LEN
        cn = lax.broadcasted_iota(I32, (n_sel, n_cmp_pad), 1) * CMP_STRIDE
        ov = jnp.clip(jnp.minimum(cn + CMP_LEN, jn + SEL_LEN) - jnp.maximum(cn, jn), 0, None)
        w_t = (ov.astype(F32) * (1.0 / CMP_LEN)).astype(BF16)
        imp4 = _dot(w_t, pb)
        imp = imp4[:, 0:tq]
        for r in range(1, GQA):
            imp = imp + imp4[:, r * tq:(r + 1) * tq]
        j_blk = lax.broadcasted_iota(I32, (n_sel, tq), 0)
        t_q = q0 + lax.broadcasted_iota(I32, (n_sel, tq), 1)
        sel_ok = j_blk * SEL_LEN <= t_q
        back = t_q // SEL_LEN - j_blk
        forced = (j_blk == 0) | ((back >= 0) & (back < N_LOCAL))
        val = jnp.where(sel_ok & forced, BIG, jnp.where(sel_ok, imp, -BIG))
        rank = jnp.zeros((n_sel, tq), F32)
        for i in range(n_sel):
            vi = val[i:i + 1, :]
            beats = (vi > val) | ((vi == val) & (j_blk > i))
            rank = rank + beats.astype(F32)
        keep = (rank < float(min(SEL_TOPK, n_sel))) & sel_ok
        bias_t = jnp.where(keep, 0.0, -SEL_MASK_BIAS)
        bias_t = jnp.concatenate([bias_t, jnp.zeros((HEAD_DIM - n_sel, tq), F32)], axis=0)
        qa_scr[g, :HEAD_DIM, :] = jnp.concatenate(
            [qrt_ref[0, h] for h in range(g * GQA, (g + 1) * GQA)], axis=1)
        qa_scr[g, HEAD_DIM:, :] = tile_heads(bias_t.astype(BF16))

    def sel_item(g, kj, bias):
        st = pl.multiple_of(kj * TK, TK)
        k_aug = jnp.concatenate([kk_ref[0, g, pl.ds(st, TK), :], kblk_ref[pl.ds(st, TK), :]], axis=1)
        return (g, g, 2 * HEAD_DIM, k_aug, vt_ref[0, g, kj], bias)

    def win_item(g, back):
        kj = jnp.maximum(qi - back, 0)
        st = pl.multiple_of(kj * TK, TK)
        off = jnp.where(qi >= back, 0.0, NEG)
        bias = causal_bias if back == 0 else (band_bias + off if back == n_win else off)
        return (N_KV + g, g, HEAD_DIM, kk_ref[0, N_KV + g, pl.ds(st, TK), :], vt_ref[0, N_KV + g, kj], bias)

    assert N_KV == 2
    scores(sel_item(0, 0, None), 0)

    def sel_body(kj, carry):
        scores(sel_item(1, kj, None), 1)
        softmax_pv(sel_item(0, kj, None), 0)
        scores(sel_item(0, kj + 1, None), 0)
        softmax_pv(sel_item(1, kj, None), 1)
        return carry

    lax.fori_loop(0, qi, sel_body, 0)
    tail = ([sel_item(g, qi, causal_bias) for g in range(N_KV)]
            + [win_item(g, back) for back in range(n_win, -1, -1) for g in range(N_KV)])
    for i, item in enumerate(tail):
        if i + 1 < len(tail):
            scores(tail[i + 1], (i + 1) % 2)
        softmax_pv(item, i % 2)

    gsig = jax.nn.sigmoid(gt_ref[...])
    for g in range(N_KV):
        o_cmp, o_sel, o_win = ocmp_scr[g], flash_out(g), flash_out(N_KV + g)
        for r in range(GQA):
            h = g * GQA + r
            sl = slice(r * tq, (r + 1) * tq)
            o_h = (gsig[3 * h:3 * h + 1, :] * o_cmp[:, sl]
                   + gsig[3 * h + 1:3 * h + 2, :] * o_sel[:, sl]
                   + gsig[3 * h + 2:3 * h + 3, :] * o_win[:, sl])
            o_scr[:, h * HEAD_DIM:(h + 1) * HEAD_DIM] = o_h.T

    o = o_scr[...]
    ms = jnp.mean(o * o, axis=-1, keepdims=True)
    o_ref[...] = (o * lax.rsqrt(ms + EPS) * gn_ref[...]).astype(BF16)


def _nsa(qnt, qrt, cmp, kk, vt, kblk, gt, gn):
    batch, _, _, seq = qnt.shape
    assert TQ == TK and WINDOW % TK == 0 and TQ & (TQ - 1) == 0
    nq = seq // TQ
    width = N_HEADS * HEAD_DIM
    return pl.pallas_call(
        _nsa_kernel,
        grid=(batch, nq),
        in_specs=[
            pl.BlockSpec((1, N_HEADS, HEAD_DIM, TQ), lambda b, i: (b, 0, 0, i)),
            pl.BlockSpec((1, N_HEADS, HEAD_DIM, TQ), lambda b, i: (b, 0, 0, i)),
            pl.BlockSpec((1,) + cmp.shape[1:], lambda b, i: (b, 0, 0, 0)),
            pl.BlockSpec((1,) + kk.shape[1:], lambda b, i: (b, 0, 0, 0)),
            pl.BlockSpec((1,) + vt.shape[1:], lambda b, i: (b, 0, 0, 0, 0)),
            pl.BlockSpec((seq, LANES), lambda b, i: (0, 0)),
            pl.BlockSpec((gt.shape[0], TQ), lambda b, i: (0, b * nq + i)),
            pl.BlockSpec((1, width), lambda b, i: (0, 0)),
        ],
        out_specs=pl.BlockSpec((TQ, width), lambda b, i: (b * nq + i, 0)),
        out_shape=jax.ShapeDtypeStruct((batch * seq, width), BF16),
        scratch_shapes=[
            pltpu.VMEM((2 * N_KV, 1, GQA * TQ), F32),
            pltpu.VMEM((2 * N_KV, vt.shape[3], GQA * TQ), F32),
            pltpu.VMEM((N_KV, 2 * HEAD_DIM, GQA * TQ), BF16),
            pltpu.VMEM((N_KV, HEAD_DIM, GQA * TQ), F32),
            pltpu.VMEM((2, TK, GQA * TQ), F32),
            pltpu.VMEM((TQ, width), F32),
        ],
        compiler_params=_cparams(2),
        name="nsa",
    )(qnt, qrt, cmp, kk, vt, kblk, gt, gn)


def _pool_kernel(u_ref, w_ref, b_ref, sc_ref, gn_ref, o_ref, y_scr, halo_scr):
    si = pl.program_id(1)
    tm = u_ref.shape[1]
    halo = halo_scr.shape[0]
    cur = u_ref[0]

    @pl.when(si == 0)
    def _():
        halo_scr[...] = jnp.zeros(halo_scr.shape, F32)

    ext = jnp.concatenate([halo_scr[...], cur], axis=0)
    halo_scr[...] = cur[tm - halo:, :]
    t1 = (si * tm + 1 + lax.broadcasted_iota(I32, (tm, 1), 0)).astype(F32)
    cg = cur.shape[1] // len(POOL_SIZES)
    for gi, w in enumerate(POOL_SIZES):
        sl = slice(gi * cg, (gi + 1) * cg)
        acc = ext[:, sl]
        span = 1
        while span < w:
            acc = acc + jnp.concatenate([jnp.zeros((span, cg), F32), acc[:-span]], axis=0)
            span *= 2
        mean = acc[halo:] / jnp.minimum(t1, float(w))
        d = (mean - cur[:, sl]).astype(BF16)
        y = _dot(d, w_ref[gi]) + b_ref[:, sl]
        y_scr[:, sl] = y * sc_ref[:, sl]
    y = y_scr[...]
    ms = jnp.mean(y * y, axis=-1, keepdims=True)
    o_ref[...] = (y * lax.rsqrt(ms + EPS) * gn_ref[...]).astype(BF16)


def _pool(u3, w_pool, b_pool, pool_scale, gn_pool):
    batch, seq, c = u3.shape
    tm = TM_PROJ
    ns = seq // tm
    vec = lambda b, i: (0, 0)
    return pl.pallas_call(
        _pool_kernel,
        grid=(batch, ns),
        in_specs=[
            pl.BlockSpec((1, tm, c), lambda b, i: (b, i, 0)),
            pl.BlockSpec(w_pool.shape, lambda b, i: (0, 0, 0)),
            pl.BlockSpec((1, c), vec),
            pl.BlockSpec((1, c), vec),
            pl.BlockSpec((1, c), vec),
        ],
        out_specs=pl.BlockSpec((tm, c), lambda b, i: (b * ns + i, 0)),
        out_shape=jax.ShapeDtypeStruct((batch * seq, c), BF16),
        scratch_shapes=[pltpu.VMEM((tm, c), F32), pltpu.VMEM((max(POOL_SIZES), c), F32)],
        compiler_params=_cparams(2),
        name="pool",
    )(u3, w_pool, b_pool, pool_scale, gn_pool)


def _outproj_kernel(mn_ref, mp_ref, x_ref, w_ref, ln_ref, wr_ref, br_ref,
                    x1_ref, h2_ref, eid_ref, gate_ref, x1_scr, ssq_scr):
    tm, d = x_ref.shape
    half = mn_ref.shape[1]
    chunk = 512

    @pl.when(pl.program_id(0) == 0)
    def _():
        x1_scr[...] = jnp.zeros(x1_scr.shape, F32)
        ssq_scr[...] = jnp.zeros(ssq_scr.shape, F32)

    h2 = x1_scr[...] * lax.rsqrt(ssq_scr[...] * (1.0 / d) + EPS) * ln_ref[...]
    _store_token_tiles(h2_ref, _pack_bf16_halves(h2))

    logits = _dot_nt(wr_ref[...], h2.astype(BF16)) + br_ref[:, 0:1]
    ng, ne = N_EXPERT_GROUPS, EXPERTS_PER_GROUP
    row = lax.broadcasted_iota(I32, (ne, tm), 0)
    lg = jnp.where(row < ng, logits[0:ne], NEG)
    mg = jnp.max(lg, axis=0, keepdims=True)
    g_sel = jnp.min(jnp.where(lg == mg, row, ne), axis=0, keepdims=True)
    p_g = 1.0 / jnp.sum(jnp.exp(lg - mg), axis=0, keepdims=True)
    le = jnp.zeros((ne, tm), F32)
    for g in range(ng):
        le = jnp.where(g_sel == g, logits[ne * (g + 1):ne * (g + 2)], le)
    ex = jnp.exp(le - jnp.max(le, axis=0, keepdims=True))
    pe = ex / jnp.sum(ex, axis=0, keepdims=True)
    p1 = jnp.max(pe, axis=0, keepdims=True)
    i1 = jnp.min(jnp.where(pe == p1, row, ne), axis=0, keepdims=True)
    rest = jnp.where(row == i1, -1.0, pe)
    p2 = jnp.max(rest, axis=0, keepdims=True)
    i2 = jnp.min(jnp.where(rest == p2, row, ne), axis=0, keepdims=True)
    psum = p1 + p2
    zero_i = jnp.zeros((ne, tm), I32)
    eid = jnp.where(row == 0, g_sel * ne + i1, jnp.where(row == 1, g_sel * ne + i2, zero_i))
    gate = jnp.where(row == 0, p_g * p1 / psum, jnp.where(row == 1, p_g * p2 / psum, 0.0))
    eid_ref[...] = eid
    gate_ref[...] = gate

    mn = mn_ref[...]
    mp = mp_ref[...]
    ssq = jnp.zeros((tm, 1), F32)
    for c in range(d // chunk):
        sl = slice(c * chunk, (c + 1) * chunk)
        x1 = x_ref[:, sl] + (_dot(mn, w_ref[:half, sl]) + _dot(mp, w_ref[half:, sl]))
        x1_ref[:, sl] = x1
        x1_scr[:, sl] = x1
        ssq = ssq + jnp.sum(x1 * x1, axis=-1, keepdims=True)
    ssq_scr[...] = ssq


def _outproj(mix_nsa, mix_pool, x2d, w_out, ln_moe, w_router_t, b_router):
    n, d = x2d.shape
    tm = TM_PROJ
    half = mix_nsa.shape[1]
    const = lambda i: (0, 0)
    last = n // tm - 1
    cur = lambda i: (jnp.minimum(i, last), 0)
    prev = lambda i: (jnp.maximum(i - 1, 0), 0)
    prev_t = lambda i: (0, jnp.maximum(i - 1, 0))
    return pl.pallas_call(
        _outproj_kernel,
        grid=(n // tm + 1,),
        in_specs=[
            pl.BlockSpec((tm, half), cur),
            pl.BlockSpec((tm, half), cur),
            pl.BlockSpec((tm, d), cur),
            pl.BlockSpec(w_out.shape, const, pipeline_mode=pl.Buffered(1)),
            pl.BlockSpec((1, d), const),
            pl.BlockSpec(w_router_t.shape, const),
            pl.BlockSpec(b_router.shape, const),
        ],
        out_specs=[
            pl.BlockSpec((tm, d), cur),
            pl.BlockSpec((tm * SUBLANES, LANES), prev),
            pl.BlockSpec((SUBLANES, tm), prev_t),
            pl.BlockSpec((SUBLANES, tm), prev_t),
        ],
        scratch_shapes=[pltpu.VMEM((tm, d), F32), pltpu.VMEM((tm, 1), F32)],
        out_shape=[
            jax.ShapeDtypeStruct((n, d), F32),
            jax.ShapeDtypeStruct((n * SUBLANES, LANES), U32),
            jax.ShapeDtypeStruct((SUBLANES, n), I32),
            jax.ShapeDtypeStruct((SUBLANES, n), F32),
        ],
        compiler_params=_cparams(1),
        name="outproj",
    )(mix_nsa, mix_pool, x2d, w_out, ln_moe, w_router_t, b_router)


def _dispatch_kernel(eid_ref, dest_ref, meta_ref, rank_scr):
    n = eid_ref.shape[1]
    t = T_RANK
    row = lax.broadcasted_iota(I32, (N_EXPERTS, t), 0)
    before = (lax.broadcasted_iota(I32, (t, t), 0) < lax.broadcasted_iota(I32, (t, t), 1)).astype(BF16)
    carry = jnp.zeros((N_EXPERTS, 1), F32)
    for k in range(2):
        for j in range(n // t):
            sl = slice(j * t, (j + 1) * t)
            oh = row == eid_ref[k:k + 1, sl]
            ohf = oh.astype(F32)
            prior = _dot(ohf.astype(BF16), before) + carry
            rank_scr[k:k + 1, sl] = jnp.sum(jnp.where(oh, prior, 0.0), axis=0, keepdims=True)
            carry = carry + jnp.sum(ohf, axis=1, keepdims=True)
    counts = jnp.broadcast_to(carry, (N_EXPERTS, LANES))
    padded = jnp.floor((counts + (TM_MOE - 1.0)) * (1.0 / TM_MOE)) * TM_MOE
    ends = padded
    r_idx = lax.broadcasted_iota(I32, (N_EXPERTS, LANES), 0)
    sh = 1
    while sh < N_EXPERTS:
        ends = ends + jnp.where(r_idx >= sh, pltpu.roll(ends, sh, 0), 0.0)
        sh *= 2
    starts = (ends - padded)[:, 0:1]
    for k in range(2):
        for j in range(n // t):
            sl = slice(j * t, (j + 1) * t)
            oh = row == eid_ref[k:k + 1, sl]
            base = jnp.sum(jnp.where(oh, starts, 0.0), axis=0, keepdims=True)
            dest_ref[k:k + 1, sl] = (rank_scr[k:k + 1, sl] + base).astype(I32)
    for k in range(2, SUBLANES):
        dest_ref[k:k + 1, :] = jnp.zeros((1, n), I32)
    blk_start = lax.broadcasted_iota(I32, (N_EXPERTS, LANES), 1).astype(F32) * TM_MOE
    blk_expert = jnp.sum((ends <= blk_start).astype(F32), axis=0, keepdims=True)
    blk_expert = jnp.minimum(blk_expert, N_EXPERTS - 1.0)
    n_used = ends[N_EXPERTS - 1:N_EXPERTS, :] * (1.0 / TM_MOE)
    on_diag = r_idx == lax.broadcasted_iota(I32, (N_EXPERTS, LANES), 1)
    end_blk = jnp.sum(jnp.where(on_diag, ends, 0.0), axis=0, keepdims=True) * (1.0 / TM_MOE)
    n_blk = jnp.sum(jnp.where(on_diag, padded, 0.0), axis=0, keepdims=True) * (1.0 / TM_MOE)
    r8 = lax.broadcasted_iota(I32, (SUBLANES, LANES), 0)
    meta = jnp.where(r8 == 0, blk_expert,
                     jnp.where(r8 == 1, n_used, jnp.where(r8 == 2, end_blk, jnp.where(r8 == 3, n_blk, 0.0))))
    meta_ref[...] = meta.astype(I32)


def _dispatch(eid):
    n = eid.shape[1]
    return pl.pallas_call(
        _dispatch_kernel,
        out_shape=[jax.ShapeDtypeStruct((SUBLANES, n), I32),
                   jax.ShapeDtypeStruct((SUBLANES, LANES), I32)],
        scratch_shapes=[pltpu.VMEM((SUBLANES, n), F32)],
        compiler_params=pltpu.CompilerParams(vmem_limit_bytes=VMEM_LIMIT_BYTES),
        name="dispatch",
    )(eid)


def _row_copy(src_ref, src_row, dst_ref, dst_row, sem):
    tile = lambda ref, row: ref.at[pl.ds(pl.multiple_of(row * SUBLANES, SUBLANES), SUBLANES)]
    return pltpu.make_async_copy(tile(src_ref, src_row), tile(dst_ref, dst_row), sem)


def _scatter_kernel(dest_ref, meta_ref, h_ref, xs_ref, zero_buf, sem, zsem):
    tm = h_ref.shape[0] // SUBLANES
    blk = zero_buf.shape[0]

    @pl.when(pl.program_id(0) == 0)
    def _():
        zero_buf[...] = jnp.zeros(zero_buf.shape, zero_buf.dtype)
        n_used = meta_ref[1, 0]

        def zero_block(b):
            return pltpu.make_async_copy(zero_buf, xs_ref.at[pl.ds(pl.multiple_of(b * blk, blk), blk)], zsem)

        def sweep(act):
            def per_expert(e, c):
                @pl.when(meta_ref[3, e] > 0)
                def _():
                    act(zero_block(meta_ref[2, e] - 1))
                return c

            def per_tail(b, c):
                @pl.when(b >= n_used)
                def _():
                    act(zero_block(b))
                return c

            lax.fori_loop(0, N_EXPERTS, per_expert, 0)
            lax.fori_loop(0, xs_ref.shape[0] // blk, per_tail, 0)

        sweep(lambda cp: cp.start())
        sweep(lambda cp: cp.wait())

    def start(r, c):
        for k in range(2):
            _row_copy(h_ref, r, xs_ref, dest_ref[0, k, r], sem).start(priority=k)
        return c

    lax.fori_loop(0, tm, start, 0, unroll=DMA_ISSUE_UNROLL)
    for k in range(2):
        pltpu.make_async_copy(h_ref, xs_ref.at[pl.ds(0, tm * SUBLANES)], sem).wait()


def _scatter(dest3, meta, h2, cap):
    nsteps, _, tm = dest3.shape
    return pl.pallas_call(
        _scatter_kernel,
        grid=(nsteps,),
        in_specs=[
            pl.BlockSpec((1, 2, tm), lambda i: (i, 0, 0), memory_space=pltpu.SMEM),
            pl.BlockSpec(memory_space=pltpu.SMEM),
            pl.BlockSpec((tm * SUBLANES, LANES), lambda i: (i, 0)),
        ],
        out_specs=pl.BlockSpec(memory_space=pl.ANY),
        out_shape=jax.ShapeDtypeStruct((cap * SUBLANES, LANES), h2.dtype),
        scratch_shapes=[pltpu.VMEM((TM_MOE * SUBLANES, LANES), h2.dtype), pltpu.SemaphoreType.DMA(()),
                        pltpu.SemaphoreType.DMA(())],
        compiler_params=_cparams(1),
        name="scatter",
    )(dest3, meta, h2)


def _experts_kernel(meta_ref, xs_ref, wg_hbm, wu_hbm, wd_hbm, ys_ref,
                    wg_buf, wu_buf, wd_buf, wg16, wu16, wd16, elist_ref, ord_ref, sem):
    c = pl.program_id(0)
    n_used = meta_ref[1, 0]
    e = meta_ref[0, c]
    first = (c == 0) | (e != meta_ref[0, jnp.maximum(c - 1, 0)])

    def fetch(expert, slot):
        return [pltpu.make_async_copy(w.at[expert], buf.at[slot], sem.at[slot, i])
                for i, (w, buf) in enumerate(((wg_hbm, wg_buf), (wu_hbm, wu_buf), (wd_hbm, wd_buf)))]

    @pl.when(c == 0)
    def _():
        def scan(b, cnt):
            eb = meta_ref[0, b]
            take = ((b == 0) | (eb != meta_ref[0, jnp.maximum(b - 1, 0)])) & (b < n_used)

            @pl.when(take)
            def _():
                elist_ref[cnt] = eb

            return cnt + take.astype(I32)

        cnt = lax.fori_loop(0, pl.num_programs(0), scan, 0)
        ord_ref[0] = 0
        ord_ref[1] = cnt
        for k in range(W_SLOTS - 1):
            @pl.when(k < cnt)
            def _(k=k):
                for cp in fetch(elist_ref[k], k):
                    cp.start(priority=WEIGHT_DMA_PRIORITY)

    @pl.when(first & (c < n_used))
    def _():
        @pl.when(c > 0)
        def _():
            ord_ref[0] = ord_ref[0] + 1
        k = ord_ref[0]
        for cp in fetch(e, k % W_SLOTS):
            cp.wait()
        ahead = k + (W_SLOTS - 1)

        @pl.when(ahead < ord_ref[1])
        def _():
            for cp in fetch(elist_ref[jnp.minimum(ahead, N_EXPERTS - 1)], ahead % W_SLOTS):
                cp.start(priority=WEIGHT_DMA_PRIORITY)

    def mlp(weights):
        lo, hi = _unpack_bf16_halves(_load_token_tiles(xs_ref))
        lo, hi = lo.astype(BF16), hi.astype(BF16)
        half = lo.shape[1]
        wg, wu, wd = weights

        def proj(w):
            return _dot(lo, w[:half, :]) + _dot(hi, w[half:, :])

        act = (jax.nn.silu(proj(wg)) * proj(wu)).astype(BF16)
        _store_token_tiles(ys_ref, _pack_bf16_halves(_dot(act, wd)))

    @pl.when(first & (c < n_used))
    def _():
        slot = ord_ref[0] % W_SLOTS
        cast = [buf[slot].astype(BF16) for buf in (wg_buf, wu_buf, wd_buf)]
        for ref16, w in zip((wg16, wu16, wd16), cast):
            ref16[...] = w
        mlp(cast)

    @pl.when(jnp.logical_not(first) & (c < n_used))
    def _():
        mlp((wg16[...], wu16[...], wd16[...]))

    @pl.when(c >= n_used)
    def _():
        ys_ref[...] = jnp.zeros(ys_ref.shape, U32)


def _experts(meta, xs, w_gate, w_up, w_down):
    rows = TM_MOE * SUBLANES
    nblk = xs.shape[0] // rows
    d, ff = w_gate.shape[1:]

    def row_map(c, meta):
        return (jnp.minimum(c, meta[1, 0] - 1), 0)

    hbm = pl.BlockSpec(memory_space=pl.ANY)
    return pl.pallas_call(
        _experts_kernel,
        grid_spec=pltpu.PrefetchScalarGridSpec(
            num_scalar_prefetch=1,
            grid=(nblk,),
            in_specs=[pl.BlockSpec((rows, LANES), row_map), hbm, hbm, hbm],
            out_specs=pl.BlockSpec((rows, LANES), lambda c, meta: (c, 0)),
            scratch_shapes=[
                pltpu.VMEM((W_SLOTS, d, ff), F32), pltpu.VMEM((W_SLOTS, d, ff), F32),
                pltpu.VMEM((W_SLOTS, ff, d), F32),
                pltpu.VMEM((d, ff), BF16), pltpu.VMEM((d, ff), BF16), pltpu.VMEM((ff, d), BF16),
                pltpu.SMEM((N_EXPERTS,), I32),
                pltpu.SMEM((2,), I32),
                pltpu.SemaphoreType.DMA((W_SLOTS, 3)),
            ],
        ),
        out_shape=jax.ShapeDtypeStruct(xs.shape, U32),
        compiler_params=_cparams(1),
        name="experts",
    )(meta, xs, w_gate, w_up, w_down)


def _combine_kernel(dest_ref, dest_next_ref, x1_ref, gate_ref, ln_ref, ys_ref, o_ref, ybuf, sem):
    i = pl.program_id(0)
    tm = x1_ref.shape[0]
    slot = i % 2

    def issue(d_ref, s):
        def start(r, c):
            for k in range(2):
                _row_copy(ys_ref, d_ref[0, k, r], ybuf.at[s, k], r, sem.at[s]).start(priority=k)
            return c

        lax.fori_loop(0, tm, start, 0, unroll=DMA_ISSUE_UNROLL)

    @pl.when(i == 0)
    def _():
        issue(dest_ref, 0)

    @pl.when(i + 1 < pl.num_programs(0))
    def _():
        issue(dest_next_ref, 1 - slot)

    for k in range(2):
        pltpu.make_async_copy(ys_ref.at[pl.ds(0, tm * SUBLANES)], ybuf.at[slot, k], sem.at[slot]).wait()
    gate = gate_ref[...]
    lo0, hi0 = _unpack_bf16_halves(_load_token_tiles(ybuf.at[slot, 0]))
    lo1, hi1 = _unpack_bf16_halves(_load_token_tiles(ybuf.at[slot, 1]))
    g0, g1 = gate[:, 0:1], gate[:, 1:2]
    moe = jnp.concatenate([lo0 * g0 + lo1 * g1, hi0 * g0 + hi1 * g1], axis=1)
    x2 = x1_ref[...] + moe
    ms = jnp.mean(x2 * x2, axis=-1, keepdims=True)
    o_ref[...] = x2 * lax.rsqrt(ms + EPS) * ln_ref[...]


def _combine(dest3, x1, gate_rows, ln_final, ys):
    n, d = x1.shape
    tm = TM_COMBINE
    last = n // tm - 1
    return pl.pallas_call(
        _combine_kernel,
        grid=(n // tm,),
        in_specs=[
            pl.BlockSpec((1, 2, tm), lambda i: (i, 0, 0), memory_space=pltpu.SMEM),
            pl.BlockSpec((1, 2, tm), lambda i: (jnp.minimum(i + 1, last), 0, 0), memory_space=pltpu.SMEM),
            pl.BlockSpec((tm, d), lambda i: (i, 0)),
            pl.BlockSpec((tm, LANES), lambda i: (i, 0)),
            pl.BlockSpec((1, d), lambda i: (0, 0)),
            pl.BlockSpec(memory_space=pl.ANY),
        ],
        out_specs=pl.BlockSpec((tm, d), lambda i: (i, 0)),
        out_shape=jax.ShapeDtypeStruct((n, d), F32),
        scratch_shapes=[pltpu.VMEM((2, 2, tm * SUBLANES, LANES), U32), pltpu.SemaphoreType.DMA((2,))],
        compiler_params=_cparams(1),
        name="combine",
    )(dest3, dest3, x1, gate_rows, ln_final, ys)


def kernel(x, positions, ln_mix, w_in, pe_cmp_k, w_cmp_k1, w_cmp_k2, pe_cmp_v, w_cmp_v1, w_cmp_v2, w_pool, b_pool, pool_scale, gn_nsa, gn_pool, w_out, ln_moe, w_router_group, b_router_group, w_router_expert, b_router_expert, w_gate, w_up, w_down, ln_final):
    batch, seq, d = x.shape
    n = batch * seq
    depth = w_in.shape[0]
    nsa_w = N_HEADS * HEAD_DIM
    kv_w = N_KV * HEAD_DIM
    cut_gates = nsa_w + 6 * kv_w
    cut_pool = cut_gates + N_GATES

    inv_freq = ROPE_THETA ** (-jnp.arange(0, ROT_DIM, 2, dtype=F32) / ROT_DIM)
    ang = positions.astype(F32).reshape(n, 1) * inv_freq[None, :]
    cos, sin = jnp.cos(ang), jnp.sin(ang)
    rest = HEAD_DIM - ROT_DIM
    trig = jnp.stack([jnp.concatenate([cos, cos, jnp.ones((n, rest), F32)], axis=1),
                      jnp.concatenate([-sin, sin, jnp.zeros((n, rest), F32)], axis=1)])
    trigt = jnp.transpose(trig, (0, 2, 1))
    kblk = (jnp.arange(seq, dtype=I32)[:, None] // SEL_LEN == jnp.arange(LANES, dtype=I32)[None, :]).astype(BF16)

    x2d = x.reshape(n, d)
    assert depth == 1, "single-layer operation"
    for l in range(depth):
        wi = w_in[l]
        o_kc, o_ks = nsa_w, nsa_w + 2 * kv_w
        o_vs, o_kw, o_vw = o_ks + kv_w, o_ks + 2 * kv_w, o_ks + 3 * kv_w
        gate_rows_pad = -(-N_GATES // SUBLANES) * SUBLANES
        wq = wi[:, :nsa_w].astype(BF16)
        wv = jnp.concatenate([wi[:, o_vs:o_kw], wi[:, o_vw:cut_gates]], axis=1).astype(BF16)
        wg = jnp.pad(wi[:, cut_gates:cut_pool], ((0, 0), (0, LANES - N_GATES))).astype(BF16)
        wkc = wi[:, o_kc:o_ks].astype(BF16)
        wk = jnp.concatenate([wi[:, o_ks:o_vs], wi[:, o_kw:o_vw]], axis=1).astype(BF16)
        wu = wi[:, cut_pool:].astype(BF16)
        qnt, qrt, kvc, kk, vt, gt, u = _inproj(x2d, ln_mix[l][None], trig, trigt, wq, wv, wg, wkc, wk, wu,
                                               gate_rows_pad, batch, seq)

        half = CMP_LEN * HEAD_DIM // 2
        def stack_w1(w):
            return jnp.concatenate([w[:half], w[half:]], axis=1)
        w1 = jnp.stack([stack_w1(w_cmp_k1[l]), stack_w1(w_cmp_v1[l])]).astype(BF16)
        w2 = jnp.stack([w_cmp_k2[l], w_cmp_v2[l]]).astype(BF16)
        def pe_rows(pe):
            return jnp.pad(pe.reshape(2, half), ((0, SUBLANES - 2), (0, 0)))
        pe2 = jnp.stack([pe_rows(pe_cmp_k[l]), pe_rows(pe_cmp_v[l])]).astype(BF16)
        cmp = _compress(kvc, pe2, w1, w2)

        mix_nsa = _nsa(qnt, qrt, cmp, kk, vt, kblk, gt, gn_nsa[l][None])
        mix_pool = _pool(u.reshape(batch, seq, -1), w_pool[l].astype(BF16), b_pool[l][None],
                         pool_scale[l][None], gn_pool[l][None])

        ng, ne = N_EXPERT_GROUPS, EXPERTS_PER_GROUP
        w_r = jnp.concatenate([
            w_router_group[l].T, jnp.zeros((ne - ng, d), F32),
            jnp.transpose(w_router_expert[l], (0, 2, 1)).reshape(ng * ne, d),
            jnp.zeros((LANES - ne - ng * ne, d), F32)], axis=0).astype(BF16)
        b_r = jnp.concatenate([
            b_router_group[l], jnp.zeros((ne - ng,), F32), b_router_expert[l].reshape(-1),
            jnp.zeros((LANES - ne - ng * ne,), F32)])
        b_r = jnp.broadcast_to(b_r[:, None], (LANES, LANES))
        x1, h2, eid, gate = _outproj(mix_nsa, mix_pool, x2d, w_out[l].astype(BF16), ln_moe[l][None], w_r, b_r)

        dest, meta = _dispatch(eid)
        cap = (2 * n // TM_MOE + N_EXPERTS) * TM_MOE
        def per_tile(tm):
            return jnp.transpose(dest[:2].reshape(2, n // tm, tm), (1, 0, 2))
        xs = _scatter(per_tile(TS_SCATTER), meta, h2, cap)
        ys = _experts(meta, xs, w_gate[l], w_up[l], w_down[l])
        dest_tok = per_tile(TM_COMBINE)
        gate_rows = jnp.pad(gate[:2].T, ((0, 0), (0, LANES - 2)))
        x2d = _combine(dest_tok, x1, gate_rows, ln_final[None], ys)
    return x2d.reshape(batch, seq, d)
```

```python
import functools

import jax
import jax.numpy as jnp
from jax import lax
from jax.experimental import pallas as pl
from jax.experimental.pallas import tpu as pltpu

F32 = jnp.float32
BF16 = jnp.bfloat16
I32 = jnp.int32

HEAD_DIM = 128
N_HEADS = 8
N_KV = 2
GQA = N_HEADS // N_KV
N_GATES = 3 * N_HEADS
ROT_DIM = HEAD_DIM // 4
ROPE_THETA = 500000.0
CMP_LEN = 32
CMP_STRIDE = 16
CMP_HIDDEN = 2 * HEAD_DIM
SEL_LEN = 64
SEL_TOPK = 16
N_LOCAL = 2
WINDOW = 512
POOL_SIZES = (2, 4, 8, 16)
N_EXPERT_GROUPS = 4
EXPERTS_PER_GROUP = 8
N_EXPERTS = N_EXPERT_GROUPS * EXPERTS_PER_GROUP
EPS = 1e-6
NEG = -1e30
BIG = 1e30
SCALE = HEAD_DIM ** -0.5
LOG2E = 1.4426950408889634
Q_SCALE = SCALE * LOG2E

LANES = 128
SUBLANES = 8
VMEM_LIMIT_BYTES = 56 * 1024 * 1024

TM_PROJ = 512
TQ = 256
TK = 256
SEL_MASK_BIAS = 32768.0
V_ONES_ROWS = 16
TM_MOE = 256
TS_SCATTER = 1024
TM_COMBINE = 512
WEIGHT_DMA_PRIORITIES = (0, 1, 1)
W_STAGE_ROWS = 256
W_SLOTS = 3
DMA_ISSUE_UNROLL = 8
T_RANK = 512


def _cparams(n_axes, flags=None):
    return pltpu.CompilerParams(dimension_semantics=("arbitrary",) * n_axes,
                                vmem_limit_bytes=VMEM_LIMIT_BYTES, flags=flags)


def _dot(a, b):
    return jnp.dot(a, b, preferred_element_type=F32)


def _dot_nt(a, b):
    return lax.dot_general(a, b, (((1,), (1,)), ((), ())), preferred_element_type=F32)


U32 = jnp.uint32
_HI16 = 0xFFFF0000


def _pack_bf16_halves(x):
    c = x.shape[1] // 2
    bits = lambda t: lax.bitcast_convert_type(t.astype(BF16).astype(F32), U32)
    return lax.shift_right_logical(bits(x[:, :c]), U32(16)) | (bits(x[:, c:]) & U32(_HI16))


def _store_token_tiles(ref, words):
    m, width = words.shape
    assert width == SUBLANES * LANES
    for j in range(SUBLANES):
        ref[pl.ds(j, m, stride=SUBLANES), :] = words[:, j * LANES:(j + 1) * LANES]


def _load_token_tiles(ref):
    m = ref.shape[0] // SUBLANES
    return jnp.concatenate([ref[pl.ds(j, m, stride=SUBLANES), :] for j in range(SUBLANES)], axis=1)


def _unpack_bf16_halves(w):
    lo = lax.bitcast_convert_type(lax.shift_left(w, U32(16)), F32)
    hi = lax.bitcast_convert_type(w & U32(_HI16), F32)
    return lo, hi


def _rope(t, cos, sin, axis):
    idx = lax.broadcasted_iota(I32, t.shape, axis)
    partner = jnp.where(idx < ROT_DIM // 2,
                        pltpu.roll(t, HEAD_DIM - ROT_DIM // 2, axis),
                        pltpu.roll(t, ROT_DIM // 2, axis))
    return t * cos + partner * sin


def _inproj_kernel(x_ref, ln_ref, trig_ref, win_hbm,
                   qnt_ref, qrt_ref, kvc_ref, kk_ref, vt_ref, gt_ref, u_ref,
                   kc_scr, stage, wq_ref, wv_ref, wg_ref, wkc_ref, wk_ref, wu_ref, wsem):
    @pl.when(pl.program_id(0) == 0)
    def _():
        rc = stage.shape[1]
        n_chunks = win_hbm.shape[0] // rc
        q_w, kv_w = N_HEADS * HEAD_DIM, N_KV * HEAD_DIM
        o_kc, o_ks = q_w, q_w + 2 * kv_w
        o_vs, o_kw, o_vw, o_g = o_ks + kv_w, o_ks + 2 * kv_w, o_ks + 3 * kv_w, o_ks + 4 * kv_w
        o_u = o_g + N_GATES
        gate_lane = lax.broadcasted_iota(I32, (rc, LANES), 1) < N_GATES

        def chunk_copy(c, slot):
            return pltpu.make_async_copy(win_hbm.at[pl.ds(c * rc, rc)], stage.at[slot], wsem.at[slot])

        chunk_copy(0, 0).start()
        for c in range(n_chunks):
            slot = c % 2
            if c + 1 < n_chunks:
                chunk_copy(c + 1, 1 - slot).start()
            chunk_copy(c, slot).wait()
            rows = slice(c * rc, (c + 1) * rc)
            cut = lambda a, b, slot=slot: stage[slot, :, a:b].astype(BF16)
            wq_ref[rows, :] = cut(0, q_w)
            wkc_ref[rows, :] = cut(o_kc, o_ks)
            wk_ref[rows, :kv_w] = cut(o_ks, o_vs)
            wk_ref[rows, kv_w:] = cut(o_kw, o_vw)
            wv_ref[rows, :kv_w] = cut(o_vs, o_kw)
            wv_ref[rows, kv_w:] = cut(o_vw, o_g)
            wg_ref[rows, :] = jnp.where(gate_lane, stage[slot, :, o_g:o_g + LANES], 0.0).astype(BF16)
            wu_ref[rows, :] = cut(o_u, o_u + wu_ref.shape[1])

    x = x_ref[...]
    ms = jnp.mean(x * x, axis=-1, keepdims=True)
    hb = (x * lax.rsqrt(ms + EPS) * ln_ref[...]).astype(BF16)
    chunk = 4 * HEAD_DIM

    cos, sin = trig_ref[0], trig_ref[1]
    cost, sint = cos.T, sin.T
    for ch in range(2):
        r = _dot(hb, wq_ref[:, ch * chunk:(ch + 1) * chunk])
        for j in range(4):
            t = r[:, j * HEAD_DIM:(j + 1) * HEAD_DIM].T
            qnt_ref[0, ch * 4 + j] = (t * Q_SCALE).astype(BF16)
            qrt_ref[0, ch * 4 + j] = (_rope(t, cost, sint, 0) * Q_SCALE).astype(BF16)
    r = _dot(hb, wv_ref[...])
    ones = jnp.ones((V_ONES_ROWS, TK), BF16)
    for j in range(4):
        for s in range(vt_ref.shape[2]):
            vt_ref[0, j, s] = jnp.concatenate(
                [r[s * TK:(s + 1) * TK, j * HEAD_DIM:(j + 1) * HEAD_DIM].T.astype(BF16), ones], axis=0)
    gt_ref[...] = _dot(hb, wg_ref[...]).T[:gt_ref.shape[0], :]

    r = _dot(hb, wkc_ref[...])
    groups = kvc_ref.shape[2]
    for j in range(4):
        kc_scr[j] = r[:, j * HEAD_DIM:(j + 1) * HEAD_DIM]
        for i in range(CMP_STRIDE):
            kvc_ref[0, j, :, i * HEAD_DIM:(i + 1) * HEAD_DIM] = (
                kc_scr[j, pl.ds(i, groups, stride=CMP_STRIDE), :].astype(BF16))
    r = _dot(hb, wk_ref[...])
    for j in range(4):
        kk_ref[0, j] = _rope(r[:, j * HEAD_DIM:(j + 1) * HEAD_DIM], cos, sin, 1).astype(BF16)
    for ch in range(2):
        u_ref[:, ch * chunk:(ch + 1) * chunk] = _dot(hb, wu_ref[:, ch * chunk:(ch + 1) * chunk])


def _inproj(x2d, ln, trig, w_in, gate_rows, batch, seq):
    n, d = x2d.shape
    tm = TM_PROJ
    assert tm % TK == 0
    spb = seq // tm
    const = lambda i: (0, 0)
    q_w, kv_w = N_HEADS * HEAD_DIM, N_KV * HEAD_DIM
    pool_w = w_in.shape[1] - (q_w + 6 * kv_w + N_GATES)
    row_heads = lambda i: (i // spb, 0, i % spb, 0)
    col_heads = lambda i: (i // spb, 0, 0, i % spb)
    return pl.pallas_call(
        _inproj_kernel,
        grid=(n // tm,),
        in_specs=[
            pl.BlockSpec((tm, d), lambda i: (i, 0)),
            pl.BlockSpec((1, d), const),
            pl.BlockSpec((2, tm, HEAD_DIM), lambda i: (0, i, 0)),
            pl.BlockSpec(memory_space=pl.ANY),
        ],
        out_specs=[
            pl.BlockSpec((1, N_HEADS, HEAD_DIM, tm), col_heads),
            pl.BlockSpec((1, N_HEADS, HEAD_DIM, tm), col_heads),
            pl.BlockSpec((1, 4, tm // CMP_STRIDE, CMP_STRIDE * HEAD_DIM), row_heads),
            pl.BlockSpec((1, 4, tm, HEAD_DIM), row_heads),
            pl.BlockSpec((1, 4, tm // TK, HEAD_DIM + V_ONES_ROWS, TK), lambda i: (i // spb, 0, i % spb, 0, 0)),
            pl.BlockSpec((gate_rows, tm), lambda i: (0, i)),
            pl.BlockSpec((tm, pool_w), lambda i: (i, 0)),
        ],
        out_shape=[
            jax.ShapeDtypeStruct((batch, N_HEADS, HEAD_DIM, seq), BF16),
            jax.ShapeDtypeStruct((batch, N_HEADS, HEAD_DIM, seq), BF16),
            jax.ShapeDtypeStruct((batch, 4, seq // CMP_STRIDE, CMP_STRIDE * HEAD_DIM), BF16),
            jax.ShapeDtypeStruct((batch, 4, seq, HEAD_DIM), BF16),
            jax.ShapeDtypeStruct((batch, 4, seq // TK, HEAD_DIM + V_ONES_ROWS, TK), BF16),
            jax.ShapeDtypeStruct((gate_rows, n), F32),
            jax.ShapeDtypeStruct((n, pool_w), F32),
        ],
        scratch_shapes=[
            pltpu.VMEM((4, tm, HEAD_DIM), F32),
            pltpu.VMEM((2, W_STAGE_ROWS, w_in.shape[1]), F32),
            pltpu.VMEM((d, q_w), BF16), pltpu.VMEM((d, 2 * kv_w), BF16), pltpu.VMEM((d, LANES), BF16),
            pltpu.VMEM((d, 2 * kv_w), BF16), pltpu.VMEM((d, 2 * kv_w), BF16), pltpu.VMEM((d, pool_w), BF16),
            pltpu.SemaphoreType.DMA((2,)),
        ],
        compiler_params=_cparams(1),
        name="inproj",
    )(x2d, ln, trig, w_in)


def _compress_kernel(t_ref, pe_ref, w1_ref, w2_ref, o_ref):
    t2 = t_ref[0, 0]
    w1 = w1_ref[0]
    a = _dot(t2, w1)
    pb = _dot(pe_ref[0], w1)
    bias = pb[0:1, :CMP_HIDDEN] + pb[1:2, CMP_HIDDEN:]
    nrow = a.shape[0]
    nxt = pltpu.roll(a[:, CMP_HIDDEN:], nrow - 1, 0)
    hid = jax.nn.gelu(a[:, :CMP_HIDDEN] + nxt + bias)
    out = _dot(hid.astype(BF16), w2_ref[0])
    row = lax.broadcasted_iota(I32, out.shape, 0)
    out = jnp.where(row < nrow - 1, out, 0.0)
    is_value = pl.program_id(1) >= N_KV
    o_ref[0, 0] = jnp.where(is_value, out.T, out).astype(BF16)


def _compress(kvc2, pe2, w1, w2):
    batch, four, nrow, width = kvc2.shape
    return pl.pallas_call(
        _compress_kernel,
        grid=(batch, four),
        in_specs=[
            pl.BlockSpec((1, 1, nrow, width), lambda b, j: (b, j, 0, 0)),
            pl.BlockSpec((1, SUBLANES, width), lambda b, j: (j // N_KV, 0, 0)),
            pl.BlockSpec((1, width, 2 * CMP_HIDDEN), lambda b, j: (j // N_KV, 0, 0)),
            pl.BlockSpec((1, CMP_HIDDEN, HEAD_DIM), lambda b, j: (j // N_KV, 0, 0)),
        ],
        out_specs=pl.BlockSpec((1, 1, nrow, HEAD_DIM), lambda b, j: (b, j, 0, 0)),
        out_shape=jax.ShapeDtypeStruct((batch, four, nrow, HEAD_DIM), BF16),
        compiler_params=_cparams(2),
        name="compress",
    )(kvc2, pe2, w1, w2)


def _nsa_kernel(qnt_ref, qrt_ref, cmp_ref, kk_ref, vt_ref, kblk_ref, gt_ref, gn_ref, o_ref,
                m_scr, acc_scr, qa_scr, ocmp_scr, s_scr, o_scr):
    qi = pl.program_id(1)
    tq = qnt_ref.shape[3]
    rows = GQA * tq
    n_cmp_pad = cmp_ref.shape[2]
    n_sel = kk_ref.shape[2] // SEL_LEN
    q0 = qi * tq

    m_scr[...] = jnp.full(m_scr.shape, NEG, F32)
    acc_scr[...] = jnp.zeros(acc_scr.shape, F32)

    def scores(item, slot):
        c, g, q_rows, k_tile, v_t, bias = item
        s_scr[slot] = _dot(k_tile, qa_scr[g, :q_rows, :])

    def softmax_pv(item, slot):
        c, g, q_rows, k_tile, v_t, bias = item
        s = s_scr[slot]
        if bias is not None:
            s = s + bias
        m_old = m_scr[c]
        m_new = jnp.maximum(m_old, jnp.max(s, axis=0, keepdims=True))
        alpha = jnp.exp2(m_old - m_new)
        p = jnp.exp2(s - m_new)
        acc_scr[c] = alpha * acc_scr[c] + _dot(v_t, p.astype(BF16))
        m_scr[c] = m_new

    def flash_out(c):
        return acc_scr[c, :HEAD_DIM, :] / acc_scr[c, HEAD_DIM:HEAD_DIM + 1, :]

    def tile_heads(a):
        return jnp.concatenate([a] * GQA, axis=1)

    kk_i = lax.broadcasted_iota(I32, (TK, tq), 0)
    qq_i = lax.broadcasted_iota(I32, (TK, tq), 1)
    causal_bias = tile_heads(jnp.where(kk_i <= qq_i, 0.0, NEG))
    band_bias = tile_heads(jnp.where(kk_i > qq_i, 0.0, NEG))
    n_win = WINDOW // TK

    for g in range(N_KV):
        qn4 = jnp.concatenate([qnt_ref[0, h] for h in range(g * GQA, (g + 1) * GQA)], axis=1)
        s = _dot(cmp_ref[0, g], qn4)
        n_idx = lax.broadcasted_iota(I32, (n_cmp_pad, rows), 0)
        t_idx = q0 + jnp.bitwise_and(lax.broadcasted_iota(I32, (n_cmp_pad, rows), 1), tq - 1)
        cmp_ok = n_idx * CMP_STRIDE + (CMP_LEN - 1) <= t_idx
        s = jnp.where(cmp_ok, s, NEG)
        m = jnp.max(s, axis=0, keepdims=True)
        e = jnp.where(cmp_ok, jnp.exp2(s - m), 0.0)
        l = jnp.sum(e, axis=0, keepdims=True)
        p = e * jnp.where(l > 0.0, 1.0 / l, 0.0)
        pb = p.astype(BF16)
        ocmp_scr[g] = _dot(cmp_ref[0, N_KV + g], pb)

        jn = lax.broadcasted_iota(I32, (n_sel, n_cmp_pad), 0) * SEL_LEN
        cn = lax.broadcasted_iota(I32, (n_sel, n_cmp_pad), 1) * CMP_STRIDE
        ov = jnp.clip(jnp.minimum(cn + CMP_LEN, jn + SEL_LEN) - jnp.maximum(cn, jn), 0, None)
        w_t = (ov.astype(F32) * (1.0 / CMP_LEN)).astype(BF16)
        imp4 = _dot(w_t, pb)
        imp = imp4[:, 0:tq]
        for r in range(1, GQA):
            imp = imp + imp4[:, r * tq:(r + 1) * tq]
        j_blk = lax.broadcasted_iota(I32, (n_sel, tq), 0)
        t_q = q0 + lax.broadcasted_iota(I32, (n_sel, tq), 1)
        sel_ok = j_blk * SEL_LEN <= t_q
        back = t_q // SEL_LEN - j_blk
        forced = (j_blk == 0) | ((back >= 0) & (back < N_LOCAL))
        val = jnp.where(sel_ok & forced, BIG, jnp.where(sel_ok, imp, -BIG))
        rank = jnp.zeros((n_sel, tq), F32)
        for i in range(n_sel):
            vi = val[i:i + 1, :]
            beats = (vi > val) | ((vi == val) & (j_blk > i))
            rank = rank + beats.astype(F32)
        keep = (rank < float(min(SEL_TOPK, n_sel))) & sel_ok
        bias_t = jnp.where(keep, 0.0, -SEL_MASK_BIAS)
        bias_t = jnp.concatenate([bias_t, jnp.zeros((HEAD_DIM - n_sel, tq), F32)], axis=0)
        qa_scr[g, :HEAD_DIM, :] = jnp.concatenate(
            [qrt_ref[0, h] for h in range(g * GQA, (g + 1) * GQA)], axis=1)
        qa_scr[g, HEAD_DIM:, :] = tile_heads(bias_t.astype(BF16))

    def sel_item(g, kj, bias):
        st = pl.multiple_of(kj * TK, TK)
        k_aug = jnp.concatenate([kk_ref[0, g, pl.ds(st, TK), :], kblk_ref[pl.ds(st, TK), :]], axis=1)
        return (g, g, 2 * HEAD_DIM, k_aug, vt_ref[0, g, kj], bias)

    def win_item(g, back):
        kj = jnp.maximum(qi - back, 0)
        st = pl.multiple_of(kj * TK, TK)
        off = jnp.where(qi >= back, 0.0, NEG)
        bias = causal_bias if back == 0 else (band_bias + off if back == n_win else off)
        return (N_KV + g, g, HEAD_DIM, kk_ref[0, N_KV + g, pl.ds(st, TK), :], vt_ref[0, N_KV + g, kj], bias)

    assert N_KV == 2
    scores(sel_item(0, 0, None), 0)

    def sel_body(kj, carry):
        scores(sel_item(1, kj, None), 1)
        softmax_pv(sel_item(0, kj, None), 0)
        scores(sel_item(0, kj + 1, None), 0)
        softmax_pv(sel_item(1, kj, None), 1)
        return carry

    lax.fori_loop(0, qi, sel_body, 0)
    tail = ([sel_item(g, qi, causal_bias) for g in range(N_KV)]
            + [win_item(g, back) for back in range(n_win, -1, -1) for g in range(N_KV)])
    for i, item in enumerate(tail):
        if i + 1 < len(tail):
            scores(tail[i + 1], (i + 1) % 2)
        softmax_pv(item, i % 2)

    gsig = jax.nn.sigmoid(gt_ref[...])
    for g in range(N_KV):
        o_cmp, o_sel, o_win = ocmp_scr[g], flash_out(g), flash_out(N_KV + g)
        for r in range(GQA):
            h = g * GQA + r
            sl = slice(r * tq, (r + 1) * tq)
            o_h = (gsig[3 * h:3 * h + 1, :] * o_cmp[:, sl]
                   + gsig[3 * h + 1:3 * h + 2, :] * o_sel[:, sl]
                   + gsig[3 * h + 2:3 * h + 3, :] * o_win[:, sl])
            o_scr[:, h * HEAD_DIM:(h + 1) * HEAD_DIM] = o_h.T

    o = o_scr[...]
    ms = jnp.mean(o * o, axis=-1, keepdims=True)
    o_ref[...] = (o * lax.rsqrt(ms + EPS) * gn_ref[...]).astype(BF16)


def _nsa(qnt, qrt, cmp, kk, vt, kblk, gt, gn):
    batch, _, _, seq = qnt.shape
    assert TQ == TK and WINDOW % TK == 0 and TQ & (TQ - 1) == 0
    nq = seq // TQ
    width = N_HEADS * HEAD_DIM
    return pl.pallas_call(
        _nsa_kernel,
        grid=(batch, nq),
        in_specs=[
            pl.BlockSpec((1, N_HEADS, HEAD_DIM, TQ), lambda b, i: (b, 0, 0, i)),
            pl.BlockSpec((1, N_HEADS, HEAD_DIM, TQ), lambda b, i: (b, 0, 0, i)),
            pl.BlockSpec((1,) + cmp.shape[1:], lambda b, i: (b, 0, 0, 0)),
            pl.BlockSpec((1,) + kk.shape[1:], lambda b, i: (b, 0, 0, 0)),
            pl.BlockSpec((1,) + vt.shape[1:], lambda b, i: (b, 0, 0, 0, 0)),
            pl.BlockSpec((seq, LANES), lambda b, i: (0, 0)),
            pl.BlockSpec((gt.shape[0], TQ), lambda b, i: (0, b * nq + i)),
            pl.BlockSpec((1, width), lambda b, i: (0, 0)),
        ],
        out_specs=pl.BlockSpec((TQ, width), lambda b, i: (b * nq + i, 0)),
        out_shape=jax.ShapeDtypeStruct((batch * seq, width), BF16),
        scratch_shapes=[
            pltpu.VMEM((2 * N_KV, 1, GQA * TQ), F32),
            pltpu.VMEM((2 * N_KV, vt.shape[3], GQA * TQ), F32),
            pltpu.VMEM((N_KV, 2 * HEAD_DIM, GQA * TQ), BF16),
            pltpu.VMEM((N_KV, HEAD_DIM, GQA * TQ), F32),
            pltpu.VMEM((2, TK, GQA * TQ), F32),
            pltpu.VMEM((TQ, width), F32),
        ],
        compiler_params=_cparams(2),
        name="nsa",
    )(qnt, qrt, cmp, kk, vt, kblk, gt, gn)


def _pool_kernel(u_ref, w_ref, b_ref, sc_ref, gn_ref, o_ref, y_scr, halo_scr):
    si = pl.program_id(1)
    tm = u_ref.shape[1]
    halo = halo_scr.shape[0]
    cur = u_ref[0]

    @pl.when(si == 0)
    def _():
        halo_scr[...] = jnp.zeros(halo_scr.shape, F32)

    ext = jnp.concatenate([halo_scr[...], cur], axis=0)
    halo_scr[...] = cur[tm - halo:, :]
    t1 = (si * tm + 1 + lax.broadcasted_iota(I32, (tm, 1), 0)).astype(F32)
    cg = cur.shape[1] // len(POOL_SIZES)
    for gi, w in enumerate(POOL_SIZES):
        sl = slice(gi * cg, (gi + 1) * cg)
        acc = ext[:, sl]
        span = 1
        while span < w:
            acc = acc + jnp.concatenate([jnp.zeros((span, cg), F32), acc[:-span]], axis=0)
            span *= 2
        mean = acc[halo:] / jnp.minimum(t1, float(w))
        d = (mean - cur[:, sl]).astype(BF16)
        y = _dot(d, w_ref[gi]) + b_ref[:, sl]
        y_scr[:, sl] = y * sc_ref[:, sl]
    y = y_scr[...]
    ms = jnp.mean(y * y, axis=-1, keepdims=True)
    o_ref[...] = (y * lax.rsqrt(ms + EPS) * gn_ref[...]).astype(BF16)


def _pool(u3, w_pool, b_pool, pool_scale, gn_pool):
    batch, seq, c = u3.shape
    tm = TM_PROJ
    ns = seq // tm
    vec = lambda b, i: (0, 0)
    return pl.pallas_call(
        _pool_kernel,
        grid=(batch, ns),
        in_specs=[
            pl.BlockSpec((1, tm, c), lambda b, i: (b, i, 0)),
            pl.BlockSpec(w_pool.shape, lambda b, i: (0, 0, 0)),
            pl.BlockSpec((1, c), vec),
            pl.BlockSpec((1, c), vec),
            pl.BlockSpec((1, c), vec),
        ],
        out_specs=pl.BlockSpec((tm, c), lambda b, i: (b * ns + i, 0)),
        out_shape=jax.ShapeDtypeStruct((batch * seq, c), BF16),
        scratch_shapes=[pltpu.VMEM((tm, c), F32), pltpu.VMEM((max(POOL_SIZES), c), F32)],
        compiler_params=_cparams(2),
        name="pool",
    )(u3, w_pool, b_pool, pool_scale, gn_pool)


def _outproj_kernel(mn_ref, mp_ref, x_ref, w_ref, ln_ref, wr_ref, br_ref,
                    x1_ref, h2_ref, eid_ref, gate_ref, x1_scr, ssq_scr):
    tm, d = x_ref.shape
    half = mn_ref.shape[1]
    chunk = 512

    @pl.when(pl.program_id(0) == 0)
    def _():
        x1_scr[...] = jnp.zeros(x1_scr.shape, F32)
        ssq_scr[...] = jnp.zeros(ssq_scr.shape, F32)

    h2 = x1_scr[...] * lax.rsqrt(ssq_scr[...] * (1.0 / d) + EPS) * ln_ref[...]
    _store_token_tiles(h2_ref, _pack_bf16_halves(h2))

    logits = _dot_nt(wr_ref[...], h2.astype(BF16)) + br_ref[:, 0:1]
    ng, ne = N_EXPERT_GROUPS, EXPERTS_PER_GROUP
    row = lax.broadcasted_iota(I32, (ne, tm), 0)
    lg = jnp.where(row < ng, logits[0:ne], NEG)
    mg = jnp.max(lg, axis=0, keepdims=True)
    g_sel = jnp.min(jnp.where(lg == mg, row, ne), axis=0, keepdims=True)
    p_g = 1.0 / jnp.sum(jnp.exp(lg - mg), axis=0, keepdims=True)
    le = jnp.zeros((ne, tm), F32)
    for g in range(ng):
        le = jnp.where(g_sel == g, logits[ne * (g + 1):ne * (g + 2)], le)
    ex = jnp.exp(le - jnp.max(le, axis=0, keepdims=True))
    pe = ex / jnp.sum(ex, axis=0, keepdims=True)
    p1 = jnp.max(pe, axis=0, keepdims=True)
    i1 = jnp.min(jnp.where(pe == p1, row, ne), axis=0, keepdims=True)
    rest = jnp.where(row == i1, -1.0, pe)
    p2 = jnp.max(rest, axis=0, keepdims=True)
    i2 = jnp.min(jnp.where(rest == p2, row, ne), axis=0, keepdims=True)
    psum = p1 + p2
    zero_i = jnp.zeros((ne, tm), I32)
    eid = jnp.where(row == 0, g_sel * ne + i1, jnp.where(row == 1, g_sel * ne + i2, zero_i))
    gate = jnp.where(row == 0, p_g * p1 / psum, jnp.where(row == 1, p_g * p2 / psum, 0.0))
    eid_ref[...] = eid
    gate_ref[...] = gate

    mn = mn_ref[...]
    mp = mp_ref[...]
    ssq = jnp.zeros((tm, 1), F32)
    for c in range(d // chunk):
        sl = slice(c * chunk, (c + 1) * chunk)
        x1 = x_ref[:, sl] + (_dot(mn, w_ref[:half, sl]) + _dot(mp, w_ref[half:, sl]))
        x1_ref[:, sl] = x1
        x1_scr[:, sl] = x1
        ssq = ssq + jnp.sum(x1 * x1, axis=-1, keepdims=True)
    ssq_scr[...] = ssq


def _outproj(mix_nsa, mix_pool, x2d, w_out, ln_moe, w_router_t, b_router):
    n, d = x2d.shape
    tm = TM_PROJ
    half = mix_nsa.shape[1]
    const = lambda i: (0, 0)
    last = n // tm - 1
    cur = lambda i: (jnp.minimum(i, last), 0)
    prev = lambda i: (jnp.maximum(i - 1, 0), 0)
    prev_t = lambda i: (0, jnp.maximum(i - 1, 0))
    return pl.pallas_call(
        _outproj_kernel,
        grid=(n // tm + 1,),
        in_specs=[
            pl.BlockSpec((tm, half), cur),
            pl.BlockSpec((tm, half), cur),
            pl.BlockSpec((tm, d), cur),
            pl.BlockSpec(w_out.shape, const, pipeline_mode=pl.Buffered(1)),
            pl.BlockSpec((1, d), const),
            pl.BlockSpec(w_router_t.shape, const),
            pl.BlockSpec(b_router.shape, const),
        ],
        out_specs=[
            pl.BlockSpec((tm, d), cur),
            pl.BlockSpec((tm * SUBLANES, LANES), prev),
            pl.BlockSpec((SUBLANES, tm), prev_t),
            pl.BlockSpec((SUBLANES, tm), prev_t),
        ],
        scratch_shapes=[pltpu.VMEM((tm, d), F32), pltpu.VMEM((tm, 1), F32)],
        out_shape=[
            jax.ShapeDtypeStruct((n, d), F32),
            jax.ShapeDtypeStruct((n * SUBLANES, LANES), U32),
            jax.ShapeDtypeStruct((SUBLANES, n), I32),
            jax.ShapeDtypeStruct((SUBLANES, n), F32),
        ],
        compiler_params=_cparams(1),
        name="outproj",
    )(mix_nsa, mix_pool, x2d, w_out, ln_moe, w_router_t, b_router)


def _dispatch_kernel(eid_ref, dest_ref, meta_ref, rank_scr):
    n = eid_ref.shape[1]
    t = T_RANK
    row = lax.broadcasted_iota(I32, (N_EXPERTS, t), 0)
    before = (lax.broadcasted_iota(I32, (t, t), 0) < lax.broadcasted_iota(I32, (t, t), 1)).astype(BF16)
    carry = jnp.zeros((N_EXPERTS, 1), F32)
    for k in range(2):
        for j in range(n // t):
            sl = slice(j * t, (j + 1) * t)
            oh = row == eid_ref[k:k + 1, sl]
            ohf = oh.astype(F32)
            prior = _dot(ohf.astype(BF16), before) + carry
            rank_scr[k:k + 1, sl] = jnp.sum(jnp.where(oh, prior, 0.0), axis=0, keepdims=True)
            carry = carry + jnp.sum(ohf, axis=1, keepdims=True)
    counts = jnp.broadcast_to(carry, (N_EXPERTS, LANES))
    padded = jnp.floor((counts + (TM_MOE - 1.0)) * (1.0 / TM_MOE)) * TM_MOE
    ends = padded
    r_idx = lax.broadcasted_iota(I32, (N_EXPERTS, LANES), 0)
    sh = 1
    while sh < N_EXPERTS:
        ends = ends + jnp.where(r_idx >= sh, pltpu.roll(ends, sh, 0), 0.0)
        sh *= 2
    starts = (ends - padded)[:, 0:1]
    for k in range(2):
        for j in range(n // t):
            sl = slice(j * t, (j + 1) * t)
            oh = row == eid_ref[k:k + 1, sl]
            base = jnp.sum(jnp.where(oh, starts, 0.0), axis=0, keepdims=True)
            dest_ref[k:k + 1, sl] = (rank_scr[k:k + 1, sl] + base).astype(I32)
    for k in range(2, SUBLANES):
        dest_ref[k:k + 1, :] = jnp.zeros((1, n), I32)
    blk_start = lax.broadcasted_iota(I32, (N_EXPERTS, LANES), 1).astype(F32) * TM_MOE
    blk_expert = jnp.sum((ends <= blk_start).astype(F32), axis=0, keepdims=True)
    blk_expert = jnp.minimum(blk_expert, N_EXPERTS - 1.0)
    n_used = ends[N_EXPERTS - 1:N_EXPERTS, :] * (1.0 / TM_MOE)
    on_diag = r_idx == lax.broadcasted_iota(I32, (N_EXPERTS, LANES), 1)
    end_blk = jnp.sum(jnp.where(on_diag, ends, 0.0), axis=0, keepdims=True) * (1.0 / TM_MOE)
    n_blk = jnp.sum(jnp.where(on_diag, padded, 0.0), axis=0, keepdims=True) * (1.0 / TM_MOE)
    r8 = lax.broadcasted_iota(I32, (SUBLANES, LANES), 0)
    meta = jnp.where(r8 == 0, blk_expert,
                     jnp.where(r8 == 1, n_used, jnp.where(r8 == 2, end_blk, jnp.where(r8 == 3, n_blk, 0.0))))
    meta_ref[...] = meta.astype(I32)


def _dispatch(eid):
    n = eid.shape[1]
    return pl.pallas_call(
        _dispatch_kernel,
        out_shape=[jax.ShapeDtypeStruct((SUBLANES, n), I32),
                   jax.ShapeDtypeStruct((SUBLANES, LANES), I32)],
        scratch_shapes=[pltpu.VMEM((SUBLANES, n), F32)],
        compiler_params=pltpu.CompilerParams(vmem_limit_bytes=VMEM_LIMIT_BYTES),
        name="dispatch",
    )(eid)


def _row_copy(src_ref, src_row, dst_ref, dst_row, sem):
    tile = lambda ref, row: ref.at[pl.ds(pl.multiple_of(row * SUBLANES, SUBLANES), SUBLANES)]
    return pltpu.make_async_copy(tile(src_ref, src_row), tile(dst_ref, dst_row), sem)


def _scatter_kernel(dest_ref, meta_ref, h_ref, xs_ref, zero_buf, sem, zsem):
    tm = h_ref.shape[0] // SUBLANES
    blk = zero_buf.shape[0]

    @pl.when(pl.program_id(0) == 0)
    def _():
        zero_buf[...] = jnp.zeros(zero_buf.shape, zero_buf.dtype)
        n_used = meta_ref[1, 0]

        def zero_block(b):
            return pltpu.make_async_copy(zero_buf, xs_ref.at[pl.ds(pl.multiple_of(b * blk, blk), blk)], zsem)

        def sweep(act):
            def per_expert(e, c):
                @pl.when(meta_ref[3, e] > 0)
                def _():
                    act(zero_block(meta_ref[2, e] - 1))
                return c

            def per_tail(b, c):
                @pl.when(b >= n_used)
                def _():
                    act(zero_block(b))
                return c

            lax.fori_loop(0, N_EXPERTS, per_expert, 0)
            lax.fori_loop(0, xs_ref.shape[0] // blk, per_tail, 0)

        sweep(lambda cp: cp.start())
        sweep(lambda cp: cp.wait())

    def start(r, c):
        for k in range(2):
            _row_copy(h_ref, r, xs_ref, dest_ref[0, k, r], sem).start(priority=k)
        return c

    lax.fori_loop(0, tm, start, 0, unroll=DMA_ISSUE_UNROLL)
    for k in range(2):
        pltpu.make_async_copy(h_ref, xs_ref.at[pl.ds(0, tm * SUBLANES)], sem).wait()


def _scatter(dest3, meta, h2, cap):
    nsteps, _, tm = dest3.shape
    return pl.pallas_call(
        _scatter_kernel,
        grid=(nsteps,),
        in_specs=[
            pl.BlockSpec((1, 2, tm), lambda i: (i, 0, 0), memory_space=pltpu.SMEM),
            pl.BlockSpec(memory_space=pltpu.SMEM),
            pl.BlockSpec((tm * SUBLANES, LANES), lambda i: (i, 0)),
        ],
        out_specs=pl.BlockSpec(memory_space=pl.ANY),
        out_shape=jax.ShapeDtypeStruct((cap * SUBLANES, LANES), h2.dtype),
        scratch_shapes=[pltpu.VMEM((TM_MOE * SUBLANES, LANES), h2.dtype), pltpu.SemaphoreType.DMA(()),
                        pltpu.SemaphoreType.DMA(())],
        compiler_params=_cparams(1),
        name="scatter",
    )(dest3, meta, h2)


def _experts_kernel(meta_ref, xs_ref, wg_hbm, wu_hbm, wd_hbm, ys_ref,
                    wg_buf, wu_buf, wd_buf, wg16, wu16, wd16, elist_ref, ord_ref, sem):
    c = pl.program_id(0)
    n_used = meta_ref[1, 0]
    e = meta_ref[0, c]
    first = (c == 0) | (e != meta_ref[0, jnp.maximum(c - 1, 0)])

    def fetch(expert, slot):
        return [pltpu.make_async_copy(w.at[expert], buf.at[slot], sem.at[slot, i])
                for i, (w, buf) in enumerate(((wg_hbm, wg_buf), (wu_hbm, wu_buf), (wd_hbm, wd_buf)))]

    @pl.when(c == 0)
    def _():
        def scan(b, cnt):
            eb = meta_ref[0, b]
            take = ((b == 0) | (eb != meta_ref[0, jnp.maximum(b - 1, 0)])) & (b < n_used)

            @pl.when(take)
            def _():
                elist_ref[cnt] = eb

            return cnt + take.astype(I32)

        cnt = lax.fori_loop(0, pl.num_programs(0), scan, 0)
        ord_ref[0] = 0
        ord_ref[1] = cnt
        for k in range(W_SLOTS - 1):
            @pl.when(k < cnt)
            def _(k=k):
                for cp, prio in zip(fetch(elist_ref[k], k), WEIGHT_DMA_PRIORITIES):
                    cp.start(priority=prio)

    @pl.when(first & (c < n_used))
    def _():
        @pl.when(c > 0)
        def _():
            ord_ref[0] = ord_ref[0] + 1
        k = ord_ref[0]
        for cp in fetch(e, k % W_SLOTS):
            cp.wait()
        ahead = k + (W_SLOTS - 1)

        @pl.when(ahead < ord_ref[1])
        def _():
            for cp, prio in zip(fetch(elist_ref[jnp.minimum(ahead, N_EXPERTS - 1)], ahead % W_SLOTS),
                                WEIGHT_DMA_PRIORITIES):
                cp.start(priority=prio)

    def mlp(weights):
        lo, hi = _unpack_bf16_halves(_load_token_tiles(xs_ref))
        lo, hi = lo.astype(BF16), hi.astype(BF16)
        half = lo.shape[1]
        wg, wu, wd = weights

        def proj(w):
            return _dot(lo, w[:half, :]) + _dot(hi, w[half:, :])

        act = (jax.nn.silu(proj(wg)) * proj(wu)).astype(BF16)
        _store_token_tiles(ys_ref, _pack_bf16_halves(_dot(act, wd)))

    @pl.when(first & (c < n_used))
    def _():
        slot = ord_ref[0] % W_SLOTS
        cast = [buf[slot].astype(BF16) for buf in (wg_buf, wu_buf, wd_buf)]
        for ref16, w in zip((wg16, wu16, wd16), cast):
            ref16[...] = w
        mlp(cast)

    @pl.when(jnp.logical_not(first) & (c < n_used))
    def _():
        mlp((wg16[...], wu16[...], wd16[...]))

    @pl.when(c >= n_used)
    def _():
        ys_ref[...] = jnp.zeros(ys_ref.shape, U32)


def _experts(meta, xs, w_gate, w_up, w_down):
    rows = TM_MOE * SUBLANES
    nblk = xs.shape[0] // rows
    d, ff = w_gate.shape[1:]

    def row_map(c, meta):
        return (jnp.minimum(c, meta[1, 0] - 1), 0)

    hbm = pl.BlockSpec(memory_space=pl.ANY)
    return pl.pallas_call(
        _experts_kernel,
        grid_spec=pltpu.PrefetchScalarGridSpec(
            num_scalar_prefetch=1,
            grid=(nblk,),
            in_specs=[pl.BlockSpec((rows, LANES), row_map), hbm, hbm, hbm],
            out_specs=pl.BlockSpec((rows, LANES), lambda c, meta: (c, 0)),
            scratch_shapes=[
                pltpu.VMEM((W_SLOTS, d, ff), F32), pltpu.VMEM((W_SLOTS, d, ff), F32),
                pltpu.VMEM((W_SLOTS, ff, d), F32),
                pltpu.VMEM((d, ff), BF16), pltpu.VMEM((d, ff), BF16), pltpu.VMEM((ff, d), BF16),
                pltpu.SMEM((N_EXPERTS,), I32),
                pltpu.SMEM((2,), I32),
                pltpu.SemaphoreType.DMA((W_SLOTS, 3)),
            ],
        ),
        out_shape=jax.ShapeDtypeStruct(xs.shape, U32),
        compiler_params=_cparams(1),
        name="experts",
    )(meta, xs, w_gate, w_up, w_down)


def _combine_kernel(dest_ref, dest_next_ref, x1_ref, gate_ref, ln_ref, ys_ref, o_ref, ybuf, sem):
    i = pl.program_id(0)
    tm = x1_ref.shape[0]
    slot = i % 2

    def issue(d_ref, s):
        def start(r, c):
            for k in range(2):
                _row_copy(ys_ref, d_ref[0, k, r], ybuf.at[s, k], r, sem.at[s]).start(priority=k)
            return c

        lax.fori_loop(0, tm, start, 0, unroll=DMA_ISSUE_UNROLL)

    @pl.when(i == 0)
    def _():
        issue(dest_ref, 0)

    @pl.when(i + 1 < pl.num_programs(0))
    def _():
        issue(dest_next_ref, 1 - slot)

    for k in range(2):
        pltpu.make_async_copy(ys_ref.at[pl.ds(0, tm * SUBLANES)], ybuf.at[slot, k], sem.at[slot]).wait()
    gate = gate_ref[...]
    lo0, hi0 = _unpack_bf16_halves(_load_token_tiles(ybuf.at[slot, 0]))
    lo1, hi1 = _unpack_bf16_halves(_load_token_tiles(ybuf.at[slot, 1]))
    g0, g1 = gate[:, 0:1], gate[:, 1:2]
    moe = jnp.concatenate([lo0 * g0 + lo1 * g1, hi0 * g0 + hi1 * g1], axis=1)
    x2 = x1_ref[...] + moe
    ms = jnp.mean(x2 * x2, axis=-1, keepdims=True)
    o_ref[...] = x2 * lax.rsqrt(ms + EPS) * ln_ref[...]


def _combine(dest3, x1, gate_rows, ln_final, ys):
    n, d = x1.shape
    tm = TM_COMBINE
    last = n // tm - 1
    return pl.pallas_call(
        _combine_kernel,
        grid=(n // tm,),
        in_specs=[
            pl.BlockSpec((1, 2, tm), lambda i: (i, 0, 0), memory_space=pltpu.SMEM),
            pl.BlockSpec((1, 2, tm), lambda i: (jnp.minimum(i + 1, last), 0, 0), memory_space=pltpu.SMEM),
            pl.BlockSpec((tm, d), lambda i: (i, 0)),
            pl.BlockSpec((tm, LANES), lambda i: (i, 0)),
            pl.BlockSpec((1, d), lambda i: (0, 0)),
            pl.BlockSpec(memory_space=pl.ANY),
        ],
        out_specs=pl.BlockSpec((tm, d), lambda i: (i, 0)),
        out_shape=jax.ShapeDtypeStruct((n, d), F32),
        scratch_shapes=[pltpu.VMEM((2, 2, tm * SUBLANES, LANES), U32), pltpu.SemaphoreType.DMA((2,))],
        compiler_params=_cparams(1),
        name="combine",
    )(dest3, dest3, x1, gate_rows, ln_final, ys)


def kernel(x, positions, ln_mix, w_in, pe_cmp_k, w_cmp_k1, w_cmp_k2, pe_cmp_v, w_cmp_v1, w_cmp_v2, w_pool, b_pool, pool_scale, gn_nsa, gn_pool, w_out, ln_moe, w_router_group, b_router_group, w_router_expert, b_router_expert, w_gate, w_up, w_down, ln_final):
    batch, seq, d = x.shape
    n = batch * seq
    depth = w_in.shape[0]
    nsa_w = N_HEADS * HEAD_DIM
    kv_w = N_KV * HEAD_DIM
    cut_gates = nsa_w + 6 * kv_w
    cut_pool = cut_gates + N_GATES

    inv_freq = ROPE_THETA ** (-jnp.arange(0, ROT_DIM, 2, dtype=F32) / ROT_DIM)
    ang = positions.astype(F32).reshape(n, 1) * inv_freq[None, :]
    cos, sin = jnp.cos(ang), jnp.sin(ang)
    rest = HEAD_DIM - ROT_DIM
    trig = jnp.stack([jnp.concatenate([cos, cos, jnp.ones((n, rest), F32)], axis=1),
                      jnp.concatenate([-sin, sin, jnp.zeros((n, rest), F32)], axis=1)])
    kblk = (jnp.arange(seq, dtype=I32)[:, None] // SEL_LEN == jnp.arange(LANES, dtype=I32)[None, :]).astype(BF16)

    x2d = x.reshape(n, d)
    assert depth == 1, "single-layer operation"
    for l in range(depth):
        gate_rows_pad = -(-N_GATES // SUBLANES) * SUBLANES
        qnt, qrt, kvc, kk, vt, gt, u = _inproj(x2d, ln_mix[l][None], trig, w_in[l], gate_rows_pad, batch, seq)

        half = CMP_LEN * HEAD_DIM // 2
        def stack_w1(w):
            return jnp.concatenate([w[:half], w[half:]], axis=1)
        w1 = jnp.stack([stack_w1(w_cmp_k1[l]), stack_w1(w_cmp_v1[l])]).astype(BF16)
        w2 = jnp.stack([w_cmp_k2[l], w_cmp_v2[l]]).astype(BF16)
        def pe_rows(pe):
            return jnp.pad(pe.reshape(2, half), ((0, SUBLANES - 2), (0, 0)))
        pe2 = jnp.stack([pe_rows(pe_cmp_k[l]), pe_rows(pe_cmp_v[l])]).astype(BF16)
        cmp = _compress(kvc, pe2, w1, w2)

        mix_nsa = _nsa(qnt, qrt, cmp, kk, vt, kblk, gt, gn_nsa[l][None])
        mix_pool = _pool(u.reshape(batch, seq, -1), w_pool[l].astype(BF16), b_pool[l][None],
                         pool_scale[l][None], gn_pool[l][None])

        ng, ne = N_EXPERT_GROUPS, EXPERTS_PER_GROUP
        w_r = jnp.concatenate([
            w_router_group[l].T, jnp.zeros((ne - ng, d), F32),
            jnp.transpose(w_router_expert[l], (0, 2, 1)).reshape(ng * ne, d),
            jnp.zeros((LANES - ne - ng * ne, d), F32)], axis=0).astype(BF16)
        b_r = jnp.concatenate([
            b_router_group[l], jnp.zeros((ne - ng,), F32), b_router_expert[l].reshape(-1),
            jnp.zeros((LANES - ne - ng * ne,), F32)])
        b_r = jnp.broadcast_to(b_r[:, None], (LANES, LANES))
        x1, h2, eid, gate = _outproj(mix_nsa, mix_pool, x2d, w_out[l].astype(BF16), ln_moe[l][None], w_r, b_r)

        dest, meta = _dispatch(eid)
        cap = (2 * n // TM_MOE + N_EXPERTS) * TM_MOE
        def per_tile(tm):
            return jnp.transpose(dest[:2].reshape(2, n // tm, tm), (1, 0, 2))
        xs = _scatter(per_tile(TS_SCATTER), meta, h2, cap)
        ys = _experts(meta, xs, w_gate[l], w_up[l], w_down[l])
        dest_tok = per_tile(TM_COMBINE)
        gate_rows = jnp.pad(gate[:2].T, ((0, 0), (0, LANES - 2)))
        x2d = _combine(dest_tok, x1, gate_rows, ln_final[None], ys)
    return x2d.reshape(batch, seq, d)
```

```python
import functools

import jax
import jax.numpy as jnp
from jax import lax
from jax.experimental import pallas as pl
from jax.experimental.pallas import tpu as pltpu

F32 = jnp.float32
BF16 = jnp.bfloat16
I32 = jnp.int32

HEAD_DIM = 128
N_HEADS = 8
N_KV = 2
GQA = N_HEADS // N_KV
N_GATES = 3 * N_HEADS
ROT_DIM = HEAD_DIM // 4
ROPE_THETA = 500000.0
CMP_LEN = 32
CMP_STRIDE = 16
CMP_HIDDEN = 2 * HEAD_DIM
SEL_LEN = 64
SEL_TOPK = 16
N_LOCAL = 2
WINDOW = 512
POOL_SIZES = (2, 4, 8, 16)
N_EXPERT_GROUPS = 4
EXPERTS_PER_GROUP = 8
N_EXPERTS = N_EXPERT_GROUPS * EXPERTS_PER_GROUP
EPS = 1e-6
NEG = -1e30
BIG = 1e30
SCALE = HEAD_DIM ** -0.5
LOG2E = 1.4426950408889634
Q_SCALE = SCALE * LOG2E

LANES = 128
SUBLANES = 8
VMEM_LIMIT_BYTES = 56 * 1024 * 1024

TM_PROJ = 512
TQ = 256
TK = 256
SEL_MASK_BIAS = 32768.0
V_ONES_ROWS = 16
TM_MOE = 256
TS_SCATTER = 1024
TM_COMBINE = 512
WEIGHT_DMA_PRIORITIES = (0, 1, 1)
W_STAGE_ROWS = 256
W_SLOTS = 3
DMA_ISSUE_UNROLL = 8
T_RANK = 512


def _cparams(n_axes, flags=None):
    return pltpu.CompilerParams(dimension_semantics=("arbitrary",) * n_axes,
                                vmem_limit_bytes=VMEM_LIMIT_BYTES, flags=flags)


def _dot(a, b):
    return jnp.dot(a, b, preferred_element_type=F32)


def _dot_nt(a, b):
    return lax.dot_general(a, b, (((1,), (1,)), ((), ())), preferred_element_type=F32)


U32 = jnp.uint32
_HI16 = 0xFFFF0000


def _pack_bf16_halves(x):
    c = x.shape[1] // 2
    bits = lambda t: lax.bitcast_convert_type(t.astype(BF16).astype(F32), U32)
    return lax.shift_right_logical(bits(x[:, :c]), U32(16)) | (bits(x[:, c:]) & U32(_HI16))


def _store_token_tiles(ref, words):
    m, width = words.shape
    assert width == SUBLANES * LANES
    for j in range(SUBLANES):
        ref[pl.ds(j, m, stride=SUBLANES), :] = words[:, j * LANES:(j + 1) * LANES]


def _load_token_tiles(ref):
    m = ref.shape[0] // SUBLANES
    return jnp.concatenate([ref[pl.ds(j, m, stride=SUBLANES), :] for j in range(SUBLANES)], axis=1)


def _unpack_bf16_halves(w):
    lo = lax.bitcast_convert_type(lax.shift_left(w, U32(16)), F32)
    hi = lax.bitcast_convert_type(w & U32(_HI16), F32)
    return lo, hi


def _rope(t, cos, sin, axis):
    idx = lax.broadcasted_iota(I32, t.shape, axis)
    partner = jnp.where(idx < ROT_DIM // 2,
                        pltpu.roll(t, HEAD_DIM - ROT_DIM // 2, axis),
                        pltpu.roll(t, ROT_DIM // 2, axis))
    return t * cos + partner * sin


def _inproj_kernel(x_ref, ln_ref, trig_ref, wt_hbm,
                   qnt_ref, qrt_ref, kvc_ref, kk_ref, vt_ref, gt_ref, u_ref,
                   kc_scr, stage, wq_ref, wv_ref, wg_ref, wkc_ref, wk_ref, wu_ref, wsem):
    @pl.when(pl.program_id(0) == 0)
    def _():
        rc = stage.shape[1]
        q_w, kv_w = N_HEADS * HEAD_DIM, N_KV * HEAD_DIM
        o_kc, o_ks = q_w, q_w + 2 * kv_w
        o_vs, o_kw, o_vw, o_g = o_ks + kv_w, o_ks + 2 * kv_w, o_ks + 3 * kv_w, o_ks + 4 * kv_w
        o_u = o_g + N_GATES
        pieces = ([(r0, rc, wq_ref, r0) for r0 in range(0, q_w, rc)]
                  + [(o_kc + r0, rc, wkc_ref, r0) for r0 in range(0, 2 * kv_w, rc)]
                  + [(o_ks, kv_w, wk_ref, 0), (o_kw, kv_w, wk_ref, kv_w)]
                  + [(o_vs, kv_w, wv_ref, 0), (o_vw, kv_w, wv_ref, kv_w)]
                  + [(o_g, N_GATES, wg_ref, 0)]
                  + [(o_u + r0, rc, wu_ref, r0) for r0 in range(0, wu_ref.shape[0], rc)])
        assert all(n <= rc and src % SUBLANES == 0 for src, n, _, _ in pieces)
        wg_ref[...] = jnp.zeros(wg_ref.shape, BF16)

        def piece_copy(i, slot):
            src, rows, _, _ = pieces[i]
            return pltpu.make_async_copy(wt_hbm.at[0, pl.ds(src, rows)], stage.at[slot, pl.ds(0, rows)],
                                         wsem.at[slot])

        piece_copy(0, 0).start()
        for i, (src, rows, dst, dst_row) in enumerate(pieces):
            slot = i % 2
            if i + 1 < len(pieces):
                piece_copy(i + 1, 1 - slot).start()
            piece_copy(i, slot).wait()
            dst[dst_row:dst_row + rows, :] = stage[slot, :rows, :].astype(BF16)

    x = x_ref[...]
    ms = jnp.mean(x * x, axis=-1, keepdims=True)
    hb = (x * lax.rsqrt(ms + EPS) * ln_ref[...]).astype(BF16)
    chunk = 4 * HEAD_DIM

    cos, sin = trig_ref[0], trig_ref[1]
    cost, sint = cos.T, sin.T
    for ch in range(2):
        rt = _dot_nt(wq_ref[ch * chunk:(ch + 1) * chunk, :], hb)
        for j in range(4):
            t = rt[j * HEAD_DIM:(j + 1) * HEAD_DIM, :]
            qnt_ref[0, ch * 4 + j] = (t * Q_SCALE).astype(BF16)
            qrt_ref[0, ch * 4 + j] = (_rope(t, cost, sint, 0) * Q_SCALE).astype(BF16)
    rt = _dot_nt(wv_ref[...], hb)
    ones = jnp.ones((V_ONES_ROWS, TK), BF16)
    for j in range(4):
        for s in range(vt_ref.shape[2]):
            vt_ref[0, j, s] = jnp.concatenate(
                [rt[j * HEAD_DIM:(j + 1) * HEAD_DIM, s * TK:(s + 1) * TK].astype(BF16), ones], axis=0)
    gt_ref[...] = _dot_nt(wg_ref[...], hb)

    r = _dot_nt(hb, wkc_ref[...])
    groups = kvc_ref.shape[2]
    for j in range(4):
        kc_scr[j] = r[:, j * HEAD_DIM:(j + 1) * HEAD_DIM]
        for i in range(CMP_STRIDE):
            kvc_ref[0, j, :, i * HEAD_DIM:(i + 1) * HEAD_DIM] = (
                kc_scr[j, pl.ds(i, groups, stride=CMP_STRIDE), :].astype(BF16))
    r = _dot_nt(hb, wk_ref[...])
    for j in range(4):
        kk_ref[0, j] = _rope(r[:, j * HEAD_DIM:(j + 1) * HEAD_DIM], cos, sin, 1).astype(BF16)
    for ch in range(u_ref.shape[1] // chunk):
        u_ref[:, ch * chunk:(ch + 1) * chunk] = _dot_nt(hb, wu_ref[ch * chunk:(ch + 1) * chunk, :])


def _inproj(x2d, ln, trig, w_in_t, gate_rows, batch, seq):
    n, d = x2d.shape
    tm = TM_PROJ
    assert tm % TK == 0
    spb = seq // tm
    const = lambda i: (0, 0)
    q_w, kv_w = N_HEADS * HEAD_DIM, N_KV * HEAD_DIM
    pool_w = w_in_t.shape[1] - (q_w + 6 * kv_w + N_GATES)
    row_heads = lambda i: (i // spb, 0, i % spb, 0)
    col_heads = lambda i: (i // spb, 0, 0, i % spb)
    return pl.pallas_call(
        _inproj_kernel,
        grid=(n // tm,),
        in_specs=[
            pl.BlockSpec((tm, d), lambda i: (i, 0)),
            pl.BlockSpec((1, d), const),
            pl.BlockSpec((2, tm, HEAD_DIM), lambda i: (0, i, 0)),
            pl.BlockSpec(memory_space=pl.ANY),
        ],
        out_specs=[
            pl.BlockSpec((1, N_HEADS, HEAD_DIM, tm), col_heads),
            pl.BlockSpec((1, N_HEADS, HEAD_DIM, tm), col_heads),
            pl.BlockSpec((1, 4, tm // CMP_STRIDE, CMP_STRIDE * HEAD_DIM), row_heads),
            pl.BlockSpec((1, 4, tm, HEAD_DIM), row_heads),
            pl.BlockSpec((1, 4, tm // TK, HEAD_DIM + V_ONES_ROWS, TK), lambda i: (i // spb, 0, i % spb, 0, 0)),
            pl.BlockSpec((gate_rows, tm), lambda i: (0, i)),
            pl.BlockSpec((tm, pool_w), lambda i: (i, 0)),
        ],
        out_shape=[
            jax.ShapeDtypeStruct((batch, N_HEADS, HEAD_DIM, seq), BF16),
            jax.ShapeDtypeStruct((batch, N_HEADS, HEAD_DIM, seq), BF16),
            jax.ShapeDtypeStruct((batch, 4, seq // CMP_STRIDE, CMP_STRIDE * HEAD_DIM), BF16),
            jax.ShapeDtypeStruct((batch, 4, seq, HEAD_DIM), BF16),
            jax.ShapeDtypeStruct((batch, 4, seq // TK, HEAD_DIM + V_ONES_ROWS, TK), BF16),
            jax.ShapeDtypeStruct((gate_rows, n), F32),
            jax.ShapeDtypeStruct((n, pool_w), F32),
        ],
        scratch_shapes=[
            pltpu.VMEM((4, tm, HEAD_DIM), F32),
            pltpu.VMEM((2, W_STAGE_ROWS, d), F32),
            pltpu.VMEM((q_w, d), BF16), pltpu.VMEM((2 * kv_w, d), BF16), pltpu.VMEM((gate_rows, d), BF16),
            pltpu.VMEM((2 * kv_w, d), BF16), pltpu.VMEM((2 * kv_w, d), BF16), pltpu.VMEM((pool_w, d), BF16),
            pltpu.SemaphoreType.DMA((2,)),
        ],
        compiler_params=_cparams(1),
        name="inproj",
    )(x2d, ln, trig, w_in_t)


def _compress_kernel(t_ref, pe_ref, w1_ref, w2_ref, o_ref):
    t2 = t_ref[0, 0]
    w1 = w1_ref[0]
    a = _dot(t2, w1)
    pb = _dot(pe_ref[0], w1)
    bias = pb[0:1, :CMP_HIDDEN] + pb[1:2, CMP_HIDDEN:]
    nrow = a.shape[0]
    nxt = pltpu.roll(a[:, CMP_HIDDEN:], nrow - 1, 0)
    hid = jax.nn.gelu(a[:, :CMP_HIDDEN] + nxt + bias)
    out = _dot(hid.astype(BF16), w2_ref[0])
    row = lax.broadcasted_iota(I32, out.shape, 0)
    out = jnp.where(row < nrow - 1, out, 0.0)
    is_value = pl.program_id(1) >= N_KV
    o_ref[0, 0] = jnp.where(is_value, out.T, out).astype(BF16)


def _compress(kvc2, pe2, w1, w2):
    batch, four, nrow, width = kvc2.shape
    return pl.pallas_call(
        _compress_kernel,
        grid=(batch, four),
        in_specs=[
            pl.BlockSpec((1, 1, nrow, width), lambda b, j: (b, j, 0, 0)),
            pl.BlockSpec((1, SUBLANES, width), lambda b, j: (j // N_KV, 0, 0)),
            pl.BlockSpec((1, width, 2 * CMP_HIDDEN), lambda b, j: (j // N_KV, 0, 0)),
            pl.BlockSpec((1, CMP_HIDDEN, HEAD_DIM), lambda b, j: (j // N_KV, 0, 0)),
        ],
        out_specs=pl.BlockSpec((1, 1, nrow, HEAD_DIM), lambda b, j: (b, j, 0, 0)),
        out_shape=jax.ShapeDtypeStruct((batch, four, nrow, HEAD_DIM), BF16),
        compiler_params=_cparams(2),
        name="compress",
    )(kvc2, pe2, w1, w2)


def _nsa_kernel(qnt_ref, qrt_ref, cmp_ref, kk_ref, vt_ref, kblk_ref, gt_ref, gn_ref, o_ref,
                m_scr, acc_scr, qa_scr, ocmp_scr, s_scr, o_scr):
    qi = pl.program_id(1)
    tq = qnt_ref.shape[3]
    rows = GQA * tq
    n_cmp_pad = cmp_ref.shape[2]
    n_sel = kk_ref.shape[2] // SEL_LEN
    q0 = qi * tq

    m_scr[...] = jnp.full(m_scr.shape, NEG, F32)
    acc_scr[...] = jnp.zeros(acc_scr.shape, F32)

    def scores(item, slot):
        c, g, q_rows, k_tile, v_t, bias = item
        s_scr[slot] = _dot(k_tile, qa_scr[g, :q_rows, :])

    def softmax_pv(item, slot):
        c, g, q_rows, k_tile, v_t, bias = item
        s = s_scr[slot]
        if bias is not None:
            s = s + bias
        m_old = m_scr[c]
        m_new = jnp.maximum(m_old, jnp.max(s, axis=0, keepdims=True))
        alpha = jnp.exp2(m_old - m_new)
        p = jnp.exp2(s - m_new)
        acc_scr[c] = alpha * acc_scr[c] + _dot(v_t, p.astype(BF16))
        m_scr[c] = m_new

    def flash_out(c):
        return acc_scr[c, :HEAD_DIM, :] / acc_scr[c, HEAD_DIM:HEAD_DIM + 1, :]

    def tile_heads(a):
        return jnp.concatenate([a] * GQA, axis=1)

    kk_i = lax.broadcasted_iota(I32, (TK, tq), 0)
    qq_i = lax.broadcasted_iota(I32, (TK, tq), 1)
    causal_bias = tile_heads(jnp.where(kk_i <= qq_i, 0.0, NEG))
    band_bias = tile_heads(jnp.where(kk_i > qq_i, 0.0, NEG))
    n_win = WINDOW // TK

    for g in range(N_KV):
        qn4 = jnp.concatenate([qnt_ref[0, h] for h in range(g * GQA, (g + 1) * GQA)], axis=1)
        s = _dot(cmp_ref[0, g], qn4)
        n_idx = lax.broadcasted_iota(I32, (n_cmp_pad, rows), 0)
        t_idx = q0 + jnp.bitwise_and(lax.broadcasted_iota(I32, (n_cmp_pad, rows), 1), tq - 1)
        cmp_ok = n_idx * CMP_STRIDE + (CMP_LEN - 1) <= t_idx
        s = jnp.where(cmp_ok, s, NEG)
        m = jnp.max(s, axis=0, keepdims=True)
        e = jnp.where(cmp_ok, jnp.exp2(s - m), 0.0)
        l = jnp.sum(e, axis=0, keepdims=True)
        p = e * jnp.where(l > 0.0, 1.0 / l, 0.0)
        pb = p.astype(BF16)
        ocmp_scr[g] = _dot(cmp_ref[0, N_KV + g], pb)

        jn = lax.broadcasted_iota(I32, (n_sel, n_cmp_pad), 0) * SEL_LEN
        cn = lax.broadcasted_iota(I32, (n_sel, n_cmp_pad), 1) * CMP_STRIDE
        ov = jnp.clip(jnp.minimum(cn + CMP_LEN, jn + SEL_LEN) - jnp.maximum(cn, jn), 0, None)
        w_t = (ov.astype(F32) * (1.0 / CMP_LEN)).astype(BF16)
        imp4 = _dot(w_t, pb)
        imp = imp4[:, 0:tq]
        for r in range(1, GQA):
            imp = imp + imp4[:, r * tq:(r + 1) * tq]
        j_blk = lax.broadcasted_iota(I32, (n_sel, tq), 0)
        t_q = q0 + lax.broadcasted_iota(I32, (n_sel, tq), 1)
        sel_ok = j_blk * SEL_LEN <= t_q
        back = t_q // SEL_LEN - j_blk
        forced = (j_blk == 0) | ((back >= 0) & (back < N_LOCAL))
        val = jnp.where(sel_ok & forced, BIG, jnp.where(sel_ok, imp, -BIG))
        rank = jnp.zeros((n_sel, tq), F32)
        for i in range(n_sel):
            vi = val[i:i + 1, :]
            beats = (vi > val) | ((vi == val) & (j_blk > i))
            rank = rank + beats.astype(F32)
        keep = (rank < float(min(SEL_TOPK, n_sel))) & sel_ok
        bias_t = jnp.where(keep, 0.0, -SEL_MASK_BIAS)
        bias_t = jnp.concatenate([bias_t, jnp.zeros((HEAD_DIM - n_sel, tq), F32)], axis=0)
        qa_scr[g, :HEAD_DIM, :] = jnp.concatenate(
            [qrt_ref[0, h] for h in range(g * GQA, (g + 1) * GQA)], axis=1)
        qa_scr[g, HEAD_DIM:, :] = tile_heads(bias_t.astype(BF16))

    def sel_item(g, kj, bias):
        st = pl.multiple_of(kj * TK, TK)
        k_aug = jnp.concatenate([kk_ref[0, g, pl.ds(st, TK), :], kblk_ref[pl.ds(st, TK), :]], axis=1)
        return (g, g, 2 * HEAD_DIM, k_aug, vt_ref[0, g, kj], bias)

    def win_item(g, back):
        kj = jnp.maximum(qi - back, 0)
        st = pl.multiple_of(kj * TK, TK)
        off = jnp.where(qi >= back, 0.0, NEG)
        bias = causal_bias if back == 0 else (band_bias + off if back == n_win else off)
        return (N_KV + g, g, HEAD_DIM, kk_ref[0, N_KV + g, pl.ds(st, TK), :], vt_ref[0, N_KV + g, kj], bias)

    assert N_KV == 2
    scores(sel_item(0, 0, None), 0)

    def sel_body(kj, carry):
        scores(sel_item(1, kj, None), 1)
        softmax_pv(sel_item(0, kj, None), 0)
        scores(sel_item(0, kj + 1, None), 0)
        softmax_pv(sel_item(1, kj, None), 1)
        return carry

    lax.fori_loop(0, qi, sel_body, 0)
    tail = ([sel_item(g, qi, causal_bias) for g in range(N_KV)]
            + [win_item(g, back) for back in range(n_win, -1, -1) for g in range(N_KV)])
    for i, item in enumerate(tail):
        if i + 1 < len(tail):
            scores(tail[i + 1], (i + 1) % 2)
        softmax_pv(item, i % 2)

    gsig = jax.nn.sigmoid(gt_ref[...])
    for g in range(N_KV):
        o_cmp, o_sel, o_win = ocmp_scr[g], flash_out(g), flash_out(N_KV + g)
        for r in range(GQA):
            h = g * GQA + r
            sl = slice(r * tq, (r + 1) * tq)
            o_h = (gsig[3 * h:3 * h + 1, :] * o_cmp[:, sl]
                   + gsig[3 * h + 1:3 * h + 2, :] * o_sel[:, sl]
                   + gsig[3 * h + 2:3 * h + 3, :] * o_win[:, sl])
            o_scr[:, h * HEAD_DIM:(h + 1) * HEAD_DIM] = o_h.T

    o = o_scr[...]
    ms = jnp.mean(o * o, axis=-1, keepdims=True)
    o_ref[...] = (o * lax.rsqrt(ms + EPS) * gn_ref[...]).astype(BF16)


def _nsa(qnt, qrt, cmp, kk, vt, kblk, gt, gn):
    batch, _, _, seq = qnt.shape
    assert TQ == TK and WINDOW % TK == 0 and TQ & (TQ - 1) == 0
    nq = seq // TQ
    width = N_HEADS * HEAD_DIM
    return pl.pallas_call(
        _nsa_kernel,
        grid=(batch, nq),
        in_specs=[
            pl.BlockSpec((1, N_HEADS, HEAD_DIM, TQ), lambda b, i: (b, 0, 0, i)),
            pl.BlockSpec((1, N_HEADS, HEAD_DIM, TQ), lambda b, i: (b, 0, 0, i)),
            pl.BlockSpec((1,) + cmp.shape[1:], lambda b, i: (b, 0, 0, 0)),
            pl.BlockSpec((1,) + kk.shape[1:], lambda b, i: (b, 0, 0, 0)),
            pl.BlockSpec((1,) + vt.shape[1:], lambda b, i: (b, 0, 0, 0, 0)),
            pl.BlockSpec((seq, LANES), lambda b, i: (0, 0)),
            pl.BlockSpec((gt.shape[0], TQ), lambda b, i: (0, b * nq + i)),
            pl.BlockSpec((1, width), lambda b, i: (0, 0)),
        ],
        out_specs=pl.BlockSpec((TQ, width), lambda b, i: (b * nq + i, 0)),
        out_shape=jax.ShapeDtypeStruct((batch * seq, width), BF16),
        scratch_shapes=[
            pltpu.VMEM((2 * N_KV, 1, GQA * TQ), F32),
            pltpu.VMEM((2 * N_KV, vt.shape[3], GQA * TQ), F32),
            pltpu.VMEM((N_KV, 2 * HEAD_DIM, GQA * TQ), BF16),
            pltpu.VMEM((N_KV, HEAD_DIM, GQA * TQ), F32),
            pltpu.VMEM((2, TK, GQA * TQ), F32),
            pltpu.VMEM((TQ, width), F32),
        ],
        compiler_params=_cparams(2),
        name="nsa",
    )(qnt, qrt, cmp, kk, vt, kblk, gt, gn)


def _pool_kernel(u_ref, w_ref, b_ref, sc_ref, gn_ref, o_ref, y_scr, halo_scr):
    si = pl.program_id(1)
    tm = u_ref.shape[1]
    halo = halo_scr.shape[0]
    cur = u_ref[0]

    @pl.when(si == 0)
    def _():
        halo_scr[...] = jnp.zeros(halo_scr.shape, F32)

    ext = jnp.concatenate([halo_scr[...], cur], axis=0)
    halo_scr[...] = cur[tm - halo:, :]
    t1 = (si * tm + 1 + lax.broadcasted_iota(I32, (tm, 1), 0)).astype(F32)
    cg = cur.shape[1] // len(POOL_SIZES)
    for gi, w in enumerate(POOL_SIZES):
        sl = slice(gi * cg, (gi + 1) * cg)
        acc = ext[:, sl]
        span = 1
        while span < w:
            acc = acc + jnp.concatenate([jnp.zeros((span, cg), F32), acc[:-span]], axis=0)
            span *= 2
        mean = acc[halo:] / jnp.minimum(t1, float(w))
        d = (mean - cur[:, sl]).astype(BF16)
        y = _dot(d, w_ref[gi]) + b_ref[:, sl]
        y_scr[:, sl] = y * sc_ref[:, sl]
    y = y_scr[...]
    ms = jnp.mean(y * y, axis=-1, keepdims=True)
    o_ref[...] = (y * lax.rsqrt(ms + EPS) * gn_ref[...]).astype(BF16)


def _pool(u3, w_pool, b_pool, pool_scale, gn_pool):
    batch, seq, c = u3.shape
    tm = TM_PROJ
    ns = seq // tm
    vec = lambda b, i: (0, 0)
    return pl.pallas_call(
        _pool_kernel,
        grid=(batch, ns),
        in_specs=[
            pl.BlockSpec((1, tm, c), lambda b, i: (b, i, 0)),
            pl.BlockSpec(w_pool.shape, lambda b, i: (0, 0, 0)),
            pl.BlockSpec((1, c), vec),
            pl.BlockSpec((1, c), vec),
            pl.BlockSpec((1, c), vec),
        ],
        out_specs=pl.BlockSpec((tm, c), lambda b, i: (b * ns + i, 0)),
        out_shape=jax.ShapeDtypeStruct((batch * seq, c), BF16),
        scratch_shapes=[pltpu.VMEM((tm, c), F32), pltpu.VMEM((max(POOL_SIZES), c), F32)],
        compiler_params=_cparams(2),
        name="pool",
    )(u3, w_pool, b_pool, pool_scale, gn_pool)


def _outproj_kernel(mn_ref, mp_ref, x_ref, w_ref, ln_ref, wr_ref, br_ref,
                    x1_ref, h2_ref, eid_ref, gate_ref, x1_scr, ssq_scr):
    tm, d = x_ref.shape
    half = mn_ref.shape[1]
    chunk = 512

    @pl.when(pl.program_id(0) == 0)
    def _():
        x1_scr[...] = jnp.zeros(x1_scr.shape, F32)
        ssq_scr[...] = jnp.zeros(ssq_scr.shape, F32)

    h2 = x1_scr[...] * lax.rsqrt(ssq_scr[...] * (1.0 / d) + EPS) * ln_ref[...]
    _store_token_tiles(h2_ref, _pack_bf16_halves(h2))

    logits = _dot_nt(wr_ref[...], h2.astype(BF16)) + br_ref[:, 0:1]
    ng, ne = N_EXPERT_GROUPS, EXPERTS_PER_GROUP
    row = lax.broadcasted_iota(I32, (ne, tm), 0)
    lg = jnp.where(row < ng, logits[0:ne], NEG)
    mg = jnp.max(lg, axis=0, keepdims=True)
    g_sel = jnp.min(jnp.where(lg == mg, row, ne), axis=0, keepdims=True)
    p_g = 1.0 / jnp.sum(jnp.exp(lg - mg), axis=0, keepdims=True)
    le = jnp.zeros((ne, tm), F32)
    for g in range(ng):
        le = jnp.where(g_sel == g, logits[ne * (g + 1):ne * (g + 2)], le)
    ex = jnp.exp(le - jnp.max(le, axis=0, keepdims=True))
    pe = ex / jnp.sum(ex, axis=0, keepdims=True)
    p1 = jnp.max(pe, axis=0, keepdims=True)
    i1 = jnp.min(jnp.where(pe == p1, row, ne), axis=0, keepdims=True)
    rest = jnp.where(row == i1, -1.0, pe)
    p2 = jnp.max(rest, axis=0, keepdims=True)
    i2 = jnp.min(jnp.where(rest == p2, row, ne), axis=0, keepdims=True)
    psum = p1 + p2
    zero_i = jnp.zeros((ne, tm), I32)
    eid = jnp.where(row == 0, g_sel * ne + i1, jnp.where(row == 1, g_sel * ne + i2, zero_i))
    gate = jnp.where(row == 0, p_g * p1 / psum, jnp.where(row == 1, p_g * p2 / psum, 0.0))
    eid_ref[...] = eid
    gate_ref[...] = gate

    mn = mn_ref[...]
    mp = mp_ref[...]
    ssq = jnp.zeros((tm, 1), F32)
    for c in range(d // chunk):
        sl = slice(c * chunk, (c + 1) * chunk)
        x1 = x_ref[:, sl] + (_dot(mn, w_ref[:half, sl]) + _dot(mp, w_ref[half:, sl]))
        x1_ref[:, sl] = x1
        x1_scr[:, sl] = x1
        ssq = ssq + jnp.sum(x1 * x1, axis=-1, keepdims=True)
    ssq_scr[...] = ssq


def _outproj(mix_nsa, mix_pool, x2d, w_out, ln_moe, w_router_t, b_router):
    n, d = x2d.shape
    tm = TM_PROJ
    half = mix_nsa.shape[1]
    const = lambda i: (0, 0)
    last = n // tm - 1
    cur = lambda i: (jnp.minimum(i, last), 0)
    prev = lambda i: (jnp.maximum(i - 1, 0), 0)
    prev_t = lambda i: (0, jnp.maximum(i - 1, 0))
    return pl.pallas_call(
        _outproj_kernel,
        grid=(n // tm + 1,),
        in_specs=[
            pl.BlockSpec((tm, half), cur),
            pl.BlockSpec((tm, half), cur),
            pl.BlockSpec((tm, d), cur),
            pl.BlockSpec(w_out.shape, const, pipeline_mode=pl.Buffered(1)),
            pl.BlockSpec((1, d), const),
            pl.BlockSpec(w_router_t.shape, const),
            pl.BlockSpec(b_router.shape, const),
        ],
        out_specs=[
            pl.BlockSpec((tm, d), cur),
            pl.BlockSpec((tm * SUBLANES, LANES), prev),
            pl.BlockSpec((SUBLANES, tm), prev_t),
            pl.BlockSpec((SUBLANES, tm), prev_t),
        ],
        scratch_shapes=[pltpu.VMEM((tm, d), F32), pltpu.VMEM((tm, 1), F32)],
        out_shape=[
            jax.ShapeDtypeStruct((n, d), F32),
            jax.ShapeDtypeStruct((n * SUBLANES, LANES), U32),
            jax.ShapeDtypeStruct((SUBLANES, n), I32),
            jax.ShapeDtypeStruct((SUBLANES, n), F32),
        ],
        compiler_params=_cparams(1),
        name="outproj",
    )(mix_nsa, mix_pool, x2d, w_out, ln_moe, w_router_t, b_router)


def _dispatch_kernel(eid_ref, dest_ref, meta_ref, rank_scr):
    n = eid_ref.shape[1]
    t = T_RANK
    row = lax.broadcasted_iota(I32, (N_EXPERTS, t), 0)
    before = (lax.broadcasted_iota(I32, (t, t), 0) < lax.broadcasted_iota(I32, (t, t), 1)).astype(BF16)
    carry = jnp.zeros((N_EXPERTS, 1), F32)
    for k in range(2):
        for j in range(n // t):
            sl = slice(j * t, (j + 1) * t)
            oh = row == eid_ref[k:k + 1, sl]
            ohf = oh.astype(F32)
            prior = _dot(ohf.astype(BF16), before) + carry
            rank_scr[k:k + 1, sl] = jnp.sum(jnp.where(oh, prior, 0.0), axis=0, keepdims=True)
            carry = carry + jnp.sum(ohf, axis=1, keepdims=True)
    counts = jnp.broadcast_to(carry, (N_EXPERTS, LANES))
    padded = jnp.floor((counts + (TM_MOE - 1.0)) * (1.0 / TM_MOE)) * TM_MOE
    ends = padded
    r_idx = lax.broadcasted_iota(I32, (N_EXPERTS, LANES), 0)
    sh = 1
    while sh < N_EXPERTS:
        ends = ends + jnp.where(r_idx >= sh, pltpu.roll(ends, sh, 0), 0.0)
        sh *= 2
    starts = (ends - padded)[:, 0:1]
    for k in range(2):
        for j in range(n // t):
            sl = slice(j * t, (j + 1) * t)
            oh = row == eid_ref[k:k + 1, sl]
            base = jnp.sum(jnp.where(oh, starts, 0.0), axis=0, keepdims=True)
            dest_ref[k:k + 1, sl] = (rank_scr[k:k + 1, sl] + base).astype(I32)
    for k in range(2, SUBLANES):
        dest_ref[k:k + 1, :] = jnp.zeros((1, n), I32)
    blk_start = lax.broadcasted_iota(I32, (N_EXPERTS, LANES), 1).astype(F32) * TM_MOE
    blk_expert = jnp.sum((ends <= blk_start).astype(F32), axis=0, keepdims=True)
    blk_expert = jnp.minimum(blk_expert, N_EXPERTS - 1.0)
    n_used = ends[N_EXPERTS - 1:N_EXPERTS, :] * (1.0 / TM_MOE)
    on_diag = r_idx == lax.broadcasted_iota(I32, (N_EXPERTS, LANES), 1)
    end_blk = jnp.sum(jnp.where(on_diag, ends, 0.0), axis=0, keepdims=True) * (1.0 / TM_MOE)
    n_blk = jnp.sum(jnp.where(on_diag, padded, 0.0), axis=0, keepdims=True) * (1.0 / TM_MOE)
    r8 = lax.broadcasted_iota(I32, (SUBLANES, LANES), 0)
    meta = jnp.where(r8 == 0, blk_expert,
                     jnp.where(r8 == 1, n_used, jnp.where(r8 == 2, end_blk, jnp.where(r8 == 3, n_blk, 0.0))))
    meta_ref[...] = meta.astype(I32)


def _dispatch(eid):
    n = eid.shape[1]
    return pl.pallas_call(
        _dispatch_kernel,
        out_shape=[jax.ShapeDtypeStruct((SUBLANES, n), I32),
                   jax.ShapeDtypeStruct((SUBLANES, LANES), I32)],
        scratch_shapes=[pltpu.VMEM((SUBLANES, n), F32)],
        compiler_params=pltpu.CompilerParams(vmem_limit_bytes=VMEM_LIMIT_BYTES),
        name="dispatch",
    )(eid)


def _row_copy(src_ref, src_row, dst_ref, dst_row, sem):
    tile = lambda ref, row: ref.at[pl.ds(pl.multiple_of(row * SUBLANES, SUBLANES), SUBLANES)]
    return pltpu.make_async_copy(tile(src_ref, src_row), tile(dst_ref, dst_row), sem)


def _scatter_kernel(dest_ref, meta_ref, h_ref, xs_ref, zero_buf, sem, zsem):
    tm = h_ref.shape[0] // SUBLANES
    blk = zero_buf.shape[0]

    @pl.when(pl.program_id(0) == 0)
    def _():
        zero_buf[...] = jnp.zeros(zero_buf.shape, zero_buf.dtype)
        n_used = meta_ref[1, 0]

        def zero_block(b):
            return pltpu.make_async_copy(zero_buf, xs_ref.at[pl.ds(pl.multiple_of(b * blk, blk), blk)], zsem)

        def sweep(act):
            def per_expert(e, c):
                @pl.when(meta_ref[3, e] > 0)
                def _():
                    act(zero_block(meta_ref[2, e] - 1))
                return c

            def per_tail(b, c):
                @pl.when(b >= n_used)
                def _():
                    act(zero_block(b))
                return c

            lax.fori_loop(0, N_EXPERTS, per_expert, 0)
            lax.fori_loop(0, xs_ref.shape[0] // blk, per_tail, 0)

        sweep(lambda cp: cp.start())
        sweep(lambda cp: cp.wait())

    def start(r, c):
        for k in range(2):
            _row_copy(h_ref, r, xs_ref, dest_ref[0, k, r], sem).start(priority=k)
        return c

    lax.fori_loop(0, tm, start, 0, unroll=DMA_ISSUE_UNROLL)
    for k in range(2):
        pltpu.make_async_copy(h_ref, xs_ref.at[pl.ds(0, tm * SUBLANES)], sem).wait()


def _scatter(dest3, meta, h2, cap):
    nsteps, _, tm = dest3.shape
    return pl.pallas_call(
        _scatter_kernel,
        grid=(nsteps,),
        in_specs=[
            pl.BlockSpec((1, 2, tm), lambda i: (i, 0, 0), memory_space=pltpu.SMEM),
            pl.BlockSpec(memory_space=pltpu.SMEM),
            pl.BlockSpec((tm * SUBLANES, LANES), lambda i: (i, 0)),
        ],
        out_specs=pl.BlockSpec(memory_space=pl.ANY),
        out_shape=jax.ShapeDtypeStruct((cap * SUBLANES, LANES), h2.dtype),
        scratch_shapes=[pltpu.VMEM((TM_MOE * SUBLANES, LANES), h2.dtype), pltpu.SemaphoreType.DMA(()),
                        pltpu.SemaphoreType.DMA(())],
        compiler_params=_cparams(1),
        name="scatter",
    )(dest3, meta, h2)


def _experts_kernel(meta_ref, xs_ref, wg_hbm, wu_hbm, wd_hbm, ys_ref,
                    wg_buf, wu_buf, wd_buf, wg16, wu16, wd16, elist_ref, ord_ref, sem):
    c = pl.program_id(0)
    n_used = meta_ref[1, 0]
    e = meta_ref[0, c]
    first = (c == 0) | (e != meta_ref[0, jnp.maximum(c - 1, 0)])

    def fetch(expert, slot):
        return [pltpu.make_async_copy(w.at[expert], buf.at[slot], sem.at[slot, i])
                for i, (w, buf) in enumerate(((wg_hbm, wg_buf), (wu_hbm, wu_buf), (wd_hbm, wd_buf)))]

    @pl.when(c == 0)
    def _():
        def scan(b, cnt):
            eb = meta_ref[0, b]
            take = ((b == 0) | (eb != meta_ref[0, jnp.maximum(b - 1, 0)])) & (b < n_used)

            @pl.when(take)
            def _():
                elist_ref[cnt] = eb

            return cnt + take.astype(I32)

        cnt = lax.fori_loop(0, pl.num_programs(0), scan, 0)
        ord_ref[0] = 0
        ord_ref[1] = cnt
        for k in range(W_SLOTS - 1):
            @pl.when(k < cnt)
            def _(k=k):
                for cp, prio in zip(fetch(elist_ref[k], k), WEIGHT_DMA_PRIORITIES):
                    cp.start(priority=prio)

    @pl.when(first & (c < n_used))
    def _():
        @pl.when(c > 0)
        def _():
            ord_ref[0] = ord_ref[0] + 1
        k = ord_ref[0]
        for cp in fetch(e, k % W_SLOTS):
            cp.wait()
        ahead = k + (W_SLOTS - 1)

        @pl.when(ahead < ord_ref[1])
        def _():
            for cp, prio in zip(fetch(elist_ref[jnp.minimum(ahead, N_EXPERTS - 1)], ahead % W_SLOTS),
                                WEIGHT_DMA_PRIORITIES):
                cp.start(priority=prio)

    def mlp(weights):
        lo, hi = _unpack_bf16_halves(_load_token_tiles(xs_ref))
        lo, hi = lo.astype(BF16), hi.astype(BF16)
        half = lo.shape[1]
        wg, wu, wd = weights

        def proj(w):
            return _dot(lo, w[:half, :]) + _dot(hi, w[half:, :])

        act = (jax.nn.silu(proj(wg)) * proj(wu)).astype(BF16)
        _store_token_tiles(ys_ref, _pack_bf16_halves(_dot(act, wd)))

    @pl.when(first & (c < n_used))
    def _():
        slot = ord_ref[0] % W_SLOTS
        cast = [buf[slot].astype(BF16) for buf in (wg_buf, wu_buf, wd_buf)]
        for ref16, w in zip((wg16, wu16, wd16), cast):
            ref16[...] = w
        mlp(cast)

    @pl.when(jnp.logical_not(first) & (c < n_used))
    def _():
        mlp((wg16[...], wu16[...], wd16[...]))

    @pl.when(c >= n_used)
    def _():
        ys_ref[...] = jnp.zeros(ys_ref.shape, U32)


def _experts(meta, xs, w_gate, w_up, w_down):
    rows = TM_MOE * SUBLANES
    nblk = xs.shape[0] // rows
    d, ff = w_gate.shape[1:]

    def row_map(c, meta):
        return (jnp.minimum(c, meta[1, 0] - 1), 0)

    hbm = pl.BlockSpec(memory_space=pl.ANY)
    return pl.pallas_call(
        _experts_kernel,
        grid_spec=pltpu.PrefetchScalarGridSpec(
            num_scalar_prefetch=1,
            grid=(nblk,),
            in_specs=[pl.BlockSpec((rows, LANES), row_map), hbm, hbm, hbm],
            out_specs=pl.BlockSpec((rows, LANES), lambda c, meta: (c, 0)),
            scratch_shapes=[
                pltpu.VMEM((W_SLOTS, d, ff), F32), pltpu.VMEM((W_SLOTS, d, ff), F32),
                pltpu.VMEM((W_SLOTS, ff, d), F32),
                pltpu.VMEM((d, ff), BF16), pltpu.VMEM((d, ff), BF16), pltpu.VMEM((ff, d), BF16),
                pltpu.SMEM((N_EXPERTS,), I32),
                pltpu.SMEM((2,), I32),
                pltpu.SemaphoreType.DMA((W_SLOTS, 3)),
            ],
        ),
        out_shape=jax.ShapeDtypeStruct(xs.shape, U32),
        compiler_params=_cparams(1),
        name="experts",
    )(meta, xs, w_gate, w_up, w_down)


def _combine_kernel(dest_ref, dest_next_ref, x1_ref, gate_ref, ln_ref, ys_ref, o_ref, ybuf, sem):
    i = pl.program_id(0)
    tm = x1_ref.shape[0]
    slot = i % 2

    def issue(d_ref, s):
        def start(r, c):
            for k in range(2):
                _row_copy(ys_ref, d_ref[0, k, r], ybuf.at[s, k], r, sem.at[s]).start(priority=k)
            return c

        lax.fori_loop(0, tm, start, 0, unroll=DMA_ISSUE_UNROLL)

    @pl.when(i == 0)
    def _():
        issue(dest_ref, 0)

    @pl.when(i + 1 < pl.num_programs(0))
    def _():
        issue(dest_next_ref, 1 - slot)

    for k in range(2):
        pltpu.make_async_copy(ys_ref.at[pl.ds(0, tm * SUBLANES)], ybuf.at[slot, k], sem.at[slot]).wait()
    gate = gate_ref[...]
    lo0, hi0 = _unpack_bf16_halves(_load_token_tiles(ybuf.at[slot, 0]))
    lo1, hi1 = _unpack_bf16_halves(_load_token_tiles(ybuf.at[slot, 1]))
    g0, g1 = gate[:, 0:1], gate[:, 1:2]
    moe = jnp.concatenate([lo0 * g0 + lo1 * g1, hi0 * g0 + hi1 * g1], axis=1)
    x2 = x1_ref[...] + moe
    ms = jnp.mean(x2 * x2, axis=-1, keepdims=True)
    o_ref[...] = x2 * lax.rsqrt(ms + EPS) * ln_ref[...]


def _combine(dest3, x1, gate_rows, ln_final, ys):
    n, d = x1.shape
    tm = TM_COMBINE
    last = n // tm - 1
    return pl.pallas_call(
        _combine_kernel,
        grid=(n // tm,),
        in_specs=[
            pl.BlockSpec((1, 2, tm), lambda i: (i, 0, 0), memory_space=pltpu.SMEM),
            pl.BlockSpec((1, 2, tm), lambda i: (jnp.minimum(i + 1, last), 0, 0), memory_space=pltpu.SMEM),
            pl.BlockSpec((tm, d), lambda i: (i, 0)),
            pl.BlockSpec((tm, LANES), lambda i: (i, 0)),
            pl.BlockSpec((1, d), lambda i: (0, 0)),
            pl.BlockSpec(memory_space=pl.ANY),
        ],
        out_specs=pl.BlockSpec((tm, d), lambda i: (i, 0)),
        out_shape=jax.ShapeDtypeStruct((n, d), F32),
        scratch_shapes=[pltpu.VMEM((2, 2, tm * SUBLANES, LANES), U32), pltpu.SemaphoreType.DMA((2,))],
        compiler_params=_cparams(1),
        name="combine",
    )(dest3, dest3, x1, gate_rows, ln_final, ys)


def kernel(x, positions, ln_mix, w_in, pe_cmp_k, w_cmp_k1, w_cmp_k2, pe_cmp_v, w_cmp_v1, w_cmp_v2, w_pool, b_pool, pool_scale, gn_nsa, gn_pool, w_out, ln_moe, w_router_group, b_router_group, w_router_expert, b_router_expert, w_gate, w_up, w_down, ln_final):
    batch, seq, d = x.shape
    n = batch * seq
    depth = w_in.shape[0]
    nsa_w = N_HEADS * HEAD_DIM
    kv_w = N_KV * HEAD_DIM
    cut_gates = nsa_w + 6 * kv_w
    cut_pool = cut_gates + N_GATES

    inv_freq = ROPE_THETA ** (-jnp.arange(0, ROT_DIM, 2, dtype=F32) / ROT_DIM)
    ang = positions.astype(F32).reshape(n, 1) * inv_freq[None, :]
    cos, sin = jnp.cos(ang), jnp.sin(ang)
    rest = HEAD_DIM - ROT_DIM
    trig = jnp.stack([jnp.concatenate([cos, cos, jnp.ones((n, rest), F32)], axis=1),
                      jnp.concatenate([-sin, sin, jnp.zeros((n, rest), F32)], axis=1)])
    kblk = (jnp.arange(seq, dtype=I32)[:, None] // SEL_LEN == jnp.arange(LANES, dtype=I32)[None, :]).astype(BF16)

    x2d = x.reshape(n, d)
    assert depth == 1, "single-layer operation"
    for l in range(depth):
        gate_rows_pad = -(-N_GATES // SUBLANES) * SUBLANES
        qnt, qrt, kvc, kk, vt, gt, u = _inproj(x2d, ln_mix[l][None], trig, jnp.swapaxes(w_in, 1, 2),
                                               gate_rows_pad, batch, seq)

        half = CMP_LEN * HEAD_DIM // 2
        def stack_w1(w):
            return jnp.concatenate([w[:half], w[half:]], axis=1)
        w1 = jnp.stack([stack_w1(w_cmp_k1[l]), stack_w1(w_cmp_v1[l])]).astype(BF16)
        w2 = jnp.stack([w_cmp_k2[l], w_cmp_v2[l]]).astype(BF16)
        def pe_rows(pe):
            return jnp.pad(pe.reshape(2, half), ((0, SUBLANES - 2), (0, 0)))
        pe2 = jnp.stack([pe_rows(pe_cmp_k[l]), pe_rows(pe_cmp_v[l])]).astype(BF16)
        cmp = _compress(kvc, pe2, w1, w2)

        mix_nsa = _nsa(qnt, qrt, cmp, kk, vt, kblk, gt, gn_nsa[l][None])
        mix_pool = _pool(u.reshape(batch, seq, -1), w_pool[l].astype(BF16), b_pool[l][None],
                         pool_scale[l][None], gn_pool[l][None])

        ng, ne = N_EXPERT_GROUPS, EXPERTS_PER_GROUP
        w_r = jnp.concatenate([
            w_router_group[l].T, jnp.zeros((ne - ng, d), F32),
            jnp.transpose(w_router_expert[l], (0, 2, 1)).reshape(ng * ne, d),
            jnp.zeros((LANES - ne - ng * ne, d), F32)], axis=0).astype(BF16)
        b_r = jnp.concatenate([
            b_router_group[l], jnp.zeros((ne - ng,), F32), b_router_expert[l].reshape(-1),
            jnp.zeros((LANES - ne - ng * ne,), F32)])
        b_r = jnp.broadcast_to(b_r[:, None], (LANES, LANES))
        x1, h2, eid, gate = _outproj(mix_nsa, mix_pool, x2d, w_out[l].astype(BF16), ln_moe[l][None], w_r, b_r)

        dest, meta = _dispatch(eid)
        cap = (2 * n // TM_MOE + N_EXPERTS) * TM_MOE
        def per_tile(tm):
            return jnp.transpose(dest[:2].reshape(2, n // tm, tm), (1, 0, 2))
        xs = _scatter(per_tile(TS_SCATTER), meta, h2, cap)
        ys = _experts(meta, xs, w_gate[l], w_up[l], w_down[l])
        dest_tok = per_tile(TM_COMBINE)
        gate_rows = jnp.pad(gate[:2].T, ((0, 0), (0, LANES - 2)))
        x2d = _combine(dest_tok, x1, gate_rows, ln_final[None], ys)
    return x2d.reshape(batch, seq, d)
```

```python
import functools

import jax
import jax.numpy as jnp
from jax import lax
from jax.experimental import pallas as pl
from jax.experimental.pallas import tpu as pltpu

F32 = jnp.float32
BF16 = jnp.bfloat16
I32 = jnp.int32

HEAD_DIM = 128
N_HEADS = 8
N_KV = 2
GQA = N_HEADS // N_KV
N_GATES = 3 * N_HEADS
ROT_DIM = HEAD_DIM // 4
ROPE_THETA = 500000.0
CMP_LEN = 32
CMP_STRIDE = 16
CMP_HIDDEN = 2 * HEAD_DIM
SEL_LEN = 64
SEL_TOPK = 16
N_LOCAL = 2
WINDOW = 512
POOL_SIZES = (2, 4, 8, 16)
N_EXPERT_GROUPS = 4
EXPERTS_PER_GROUP = 8
N_EXPERTS = N_EXPERT_GROUPS * EXPERTS_PER_GROUP
EPS = 1e-6
NEG = -1e30
BIG = 1e30
SCALE = HEAD_DIM ** -0.5
LOG2E = 1.4426950408889634
Q_SCALE = SCALE * LOG2E

LANES = 128
SUBLANES = 8
VMEM_LIMIT_BYTES = 56 * 1024 * 1024

TM_PROJ = 512
TQ = 256
TK = 256
SEL_MASK_BIAS = 32768.0
V_ONES_ROWS = 16
TM_MOE = 256
TS_SCATTER = 1024
TM_COMBINE = 512
WEIGHT_DMA_PRIORITIES = (0, 1, 1)
W_STAGE_ROWS = 256
W_SLOTS = 3
DMA_ISSUE_UNROLL = 8
T_RANK = 512


def _cparams(n_axes, flags=None):
    return pltpu.CompilerParams(dimension_semantics=("arbitrary",) * n_axes,
                                vmem_limit_bytes=VMEM_LIMIT_BYTES, flags=flags)


def _dot(a, b):
    return jnp.dot(a, b, preferred_element_type=F32)


def _dot_nt(a, b):
    return lax.dot_general(a, b, (((1,), (1,)), ((), ())), preferred_element_type=F32)


U32 = jnp.uint32
_HI16 = 0xFFFF0000


def _pack_bf16_halves(x):
    c = x.shape[1] // 2
    bits = lambda t: lax.bitcast_convert_type(t.astype(BF16).astype(F32), U32)
    return lax.shift_right_logical(bits(x[:, :c]), U32(16)) | (bits(x[:, c:]) & U32(_HI16))


def _store_token_tiles(ref, words):
    m, width = words.shape
    assert width == SUBLANES * LANES
    for j in range(SUBLANES):
        ref[pl.ds(j, m, stride=SUBLANES), :] = words[:, j * LANES:(j + 1) * LANES]


def _load_token_tiles(ref):
    m = ref.shape[0] // SUBLANES
    return jnp.concatenate([ref[pl.ds(j, m, stride=SUBLANES), :] for j in range(SUBLANES)], axis=1)


def _unpack_bf16_halves(w):
    lo = lax.bitcast_convert_type(lax.shift_left(w, U32(16)), F32)
    hi = lax.bitcast_convert_type(w & U32(_HI16), F32)
    return lo, hi


def _rope(t, cos, sin, axis):
    idx = lax.broadcasted_iota(I32, t.shape, axis)
    partner = jnp.where(idx < ROT_DIM // 2,
                        pltpu.roll(t, HEAD_DIM - ROT_DIM // 2, axis),
                        pltpu.roll(t, ROT_DIM // 2, axis))
    return t * cos + partner * sin


def _inproj_kernel(x_ref, ln_ref, ang_ref, wt_hbm,
                   qnt_ref, qrt_ref, kvc_ref, kk_ref, vt_ref, gt_ref, u_ref,
                   kc_scr, stage, wq_ref, wv_ref, wg_ref, wkc_ref, wk_ref, wu_ref, wsem):
    @pl.when(pl.program_id(0) == 0)
    def _():
        rc = stage.shape[1]
        q_w, kv_w = N_HEADS * HEAD_DIM, N_KV * HEAD_DIM
        o_kc, o_ks = q_w, q_w + 2 * kv_w
        o_vs, o_kw, o_vw, o_g = o_ks + kv_w, o_ks + 2 * kv_w, o_ks + 3 * kv_w, o_ks + 4 * kv_w
        o_u = o_g + N_GATES
        pieces = ([(r0, rc, wq_ref, r0) for r0 in range(0, q_w, rc)]
                  + [(o_kc + r0, rc, wkc_ref, r0) for r0 in range(0, 2 * kv_w, rc)]
                  + [(o_ks, kv_w, wk_ref, 0), (o_kw, kv_w, wk_ref, kv_w)]
                  + [(o_vs, kv_w, wv_ref, 0), (o_vw, kv_w, wv_ref, kv_w)]
                  + [(o_g, N_GATES, wg_ref, 0)]
                  + [(o_u + r0, rc, wu_ref, r0) for r0 in range(0, wu_ref.shape[0], rc)])
        assert all(n <= rc and src % SUBLANES == 0 for src, n, _, _ in pieces)
        wg_ref[...] = jnp.zeros(wg_ref.shape, BF16)

        def piece_copy(i, slot):
            src, rows, _, _ = pieces[i]
            return pltpu.make_async_copy(wt_hbm.at[0, pl.ds(src, rows)], stage.at[slot, pl.ds(0, rows)],
                                         wsem.at[slot])

        piece_copy(0, 0).start()
        for i, (src, rows, dst, dst_row) in enumerate(pieces):
            slot = i % 2
            if i + 1 < len(pieces):
                piece_copy(i + 1, 1 - slot).start()
            piece_copy(i, slot).wait()
            dst[dst_row:dst_row + rows, :] = stage[slot, :rows, :].astype(BF16)

    x = x_ref[...]
    ms = jnp.mean(x * x, axis=-1, keepdims=True)
    hb = (x * lax.rsqrt(ms + EPS) * ln_ref[...]).astype(BF16)
    chunk = 4 * HEAD_DIM

    ang = ang_ref[...]
    ca, sa = jnp.cos(ang), jnp.sin(ang)
    rest = (HEAD_DIM - ROT_DIM, ang.shape[1])
    cost = jnp.concatenate([ca, ca, jnp.ones(rest, F32)], axis=0)
    sint = jnp.concatenate([-sa, sa, jnp.zeros(rest, F32)], axis=0)
    cos, sin = cost.T, sint.T
    for ch in range(2):
        rt = _dot_nt(wq_ref[ch * chunk:(ch + 1) * chunk, :], hb)
        for j in range(4):
            t = rt[j * HEAD_DIM:(j + 1) * HEAD_DIM, :]
            qnt_ref[0, ch * 4 + j] = (t * Q_SCALE).astype(BF16)
            qrt_ref[0, ch * 4 + j] = (_rope(t, cost, sint, 0) * Q_SCALE).astype(BF16)
    rt = _dot_nt(wv_ref[...], hb)
    ones = jnp.ones((V_ONES_ROWS, TK), BF16)
    for j in range(4):
        for s in range(vt_ref.shape[2]):
            vt_ref[0, j, s] = jnp.concatenate(
                [rt[j * HEAD_DIM:(j + 1) * HEAD_DIM, s * TK:(s + 1) * TK].astype(BF16), ones], axis=0)
    gt_ref[...] = _dot_nt(wg_ref[...], hb)

    r = _dot_nt(hb, wkc_ref[...])
    groups = kvc_ref.shape[2]
    for j in range(4):
        kc_scr[j] = r[:, j * HEAD_DIM:(j + 1) * HEAD_DIM]
        for i in range(CMP_STRIDE):
            kvc_ref[0, j, :, i * HEAD_DIM:(i + 1) * HEAD_DIM] = (
                kc_scr[j, pl.ds(i, groups, stride=CMP_STRIDE), :].astype(BF16))
    r = _dot_nt(hb, wk_ref[...])
    for j in range(4):
        kk_ref[0, j] = _rope(r[:, j * HEAD_DIM:(j + 1) * HEAD_DIM], cos, sin, 1).astype(BF16)
    for ch in range(u_ref.shape[1] // chunk):
        u_ref[:, ch * chunk:(ch + 1) * chunk] = _dot_nt(hb, wu_ref[ch * chunk:(ch + 1) * chunk, :])


def _inproj(x2d, ln, ang_t, w_in_t, gate_rows, batch, seq):
    n, d = x2d.shape
    tm = TM_PROJ
    assert tm % TK == 0
    spb = seq // tm
    const = lambda i: (0, 0)
    q_w, kv_w = N_HEADS * HEAD_DIM, N_KV * HEAD_DIM
    pool_w = w_in_t.shape[1] - (q_w + 6 * kv_w + N_GATES)
    row_heads = lambda i: (i // spb, 0, i % spb, 0)
    col_heads = lambda i: (i // spb, 0, 0, i % spb)
    return pl.pallas_call(
        _inproj_kernel,
        grid=(n // tm,),
        in_specs=[
            pl.BlockSpec((tm, d), lambda i: (i, 0)),
            pl.BlockSpec((1, d), const),
            pl.BlockSpec((ang_t.shape[0], tm), lambda i: (0, i)),
            pl.BlockSpec(memory_space=pl.ANY),
        ],
        out_specs=[
            pl.BlockSpec((1, N_HEADS, HEAD_DIM, tm), col_heads),
            pl.BlockSpec((1, N_HEADS, HEAD_DIM, tm), col_heads),
            pl.BlockSpec((1, 4, tm // CMP_STRIDE, CMP_STRIDE * HEAD_DIM), row_heads),
            pl.BlockSpec((1, 4, tm, HEAD_DIM), row_heads),
            pl.BlockSpec((1, 4, tm // TK, HEAD_DIM + V_ONES_ROWS, TK), lambda i: (i // spb, 0, i % spb, 0, 0)),
            pl.BlockSpec((gate_rows, tm), lambda i: (0, i)),
            pl.BlockSpec((tm, pool_w), lambda i: (i, 0)),
        ],
        out_shape=[
            jax.ShapeDtypeStruct((batch, N_HEADS, HEAD_DIM, seq), BF16),
            jax.ShapeDtypeStruct((batch, N_HEADS, HEAD_DIM, seq), BF16),
            jax.ShapeDtypeStruct((batch, 4, seq // CMP_STRIDE, CMP_STRIDE * HEAD_DIM), BF16),
            jax.ShapeDtypeStruct((batch, 4, seq, HEAD_DIM), BF16),
            jax.ShapeDtypeStruct((batch, 4, seq // TK, HEAD_DIM + V_ONES_ROWS, TK), BF16),
            jax.ShapeDtypeStruct((gate_rows, n), F32),
            jax.ShapeDtypeStruct((n, pool_w), F32),
        ],
        scratch_shapes=[
            pltpu.VMEM((4, tm, HEAD_DIM), F32),
            pltpu.VMEM((2, W_STAGE_ROWS, d), F32),
            pltpu.VMEM((q_w, d), BF16), pltpu.VMEM((2 * kv_w, d), BF16), pltpu.VMEM((gate_rows, d), BF16),
            pltpu.VMEM((2 * kv_w, d), BF16), pltpu.VMEM((2 * kv_w, d), BF16), pltpu.VMEM((pool_w, d), BF16),
            pltpu.SemaphoreType.DMA((2,)),
        ],
        compiler_params=_cparams(1),
        name="inproj",
    )(x2d, ln, ang_t, w_in_t)


def _compress_kernel(t_ref, pe_ref, w1_ref, w2_ref, o_ref):
    t2 = t_ref[0, 0]
    w1 = w1_ref[0]
    a = _dot(t2, w1)
    pb = _dot(pe_ref[0], w1)
    bias = pb[0:1, :CMP_HIDDEN] + pb[1:2, CMP_HIDDEN:]
    nrow = a.shape[0]
    nxt = pltpu.roll(a[:, CMP_HIDDEN:], nrow - 1, 0)
    hid = jax.nn.gelu(a[:, :CMP_HIDDEN] + nxt + bias)
    out = _dot(hid.astype(BF16), w2_ref[0])
    row = lax.broadcasted_iota(I32, out.shape, 0)
    out = jnp.where(row < nrow - 1, out, 0.0)
    is_value = pl.program_id(1) >= N_KV
    o_ref[0, 0] = jnp.where(is_value, out.T, out).astype(BF16)


def _compress(kvc2, pe2, w1, w2):
    batch, four, nrow, width = kvc2.shape
    return pl.pallas_call(
        _compress_kernel,
        grid=(batch, four),
        in_specs=[
            pl.BlockSpec((1, 1, nrow, width), lambda b, j: (b, j, 0, 0)),
            pl.BlockSpec((1, SUBLANES, width), lambda b, j: (j // N_KV, 0, 0)),
            pl.BlockSpec((1, width, 2 * CMP_HIDDEN), lambda b, j: (j // N_KV, 0, 0)),
            pl.BlockSpec((1, CMP_HIDDEN, HEAD_DIM), lambda b, j: (j // N_KV, 0, 0)),
        ],
        out_specs=pl.BlockSpec((1, 1, nrow, HEAD_DIM), lambda b, j: (b, j, 0, 0)),
        out_shape=jax.ShapeDtypeStruct((batch, four, nrow, HEAD_DIM), BF16),
        compiler_params=_cparams(2),
        name="compress",
    )(kvc2, pe2, w1, w2)


def _nsa_kernel(qnt_ref, qrt_ref, cmp_ref, kk_ref, vt_ref, kblk_ref, gt_ref, gn_ref, o_ref,
                m_scr, acc_scr, qa_scr, ocmp_scr, s_scr, o_scr):
    qi = pl.program_id(1)
    tq = qnt_ref.shape[3]
    rows = GQA * tq
    n_cmp_pad = cmp_ref.shape[2]
    n_sel = kk_ref.shape[2] // SEL_LEN
    q0 = qi * tq

    m_scr[...] = jnp.full(m_scr.shape, NEG, F32)
    acc_scr[...] = jnp.zeros(acc_scr.shape, F32)

    def scores(item, slot):
        c, g, q_rows, k_tile, v_t, bias = item
        s_scr[slot] = _dot(k_tile, qa_scr[g, :q_rows, :])

    def softmax_pv(item, slot):
        c, g, q_rows, k_tile, v_t, bias = item
        s = s_scr[slot]
        if bias is not None:
            s = s + bias
        m_old = m_scr[c]
        m_new = jnp.maximum(m_old, jnp.max(s, axis=0, keepdims=True))
        alpha = jnp.exp2(m_old - m_new)
        p = jnp.exp2(s - m_new)
        acc_scr[c] = alpha * acc_scr[c] + _dot(v_t, p.astype(BF16))
        m_scr[c] = m_new

    def flash_out(c):
        return acc_scr[c, :HEAD_DIM, :] / acc_scr[c, HEAD_DIM:HEAD_DIM + 1, :]

    def tile_heads(a):
        return jnp.concatenate([a] * GQA, axis=1)

    kk_i = lax.broadcasted_iota(I32, (TK, tq), 0)
    qq_i = lax.broadcasted_iota(I32, (TK, tq), 1)
    causal_bias = tile_heads(jnp.where(kk_i <= qq_i, 0.0, NEG))
    band_bias = tile_heads(jnp.where(kk_i > qq_i, 0.0, NEG))
    n_win = WINDOW // TK

    for g in range(N_KV):
        qn4 = jnp.concatenate([qnt_ref[0, h] for h in range(g * GQA, (g + 1) * GQA)], axis=1)
        s = _dot(cmp_ref[0, g], qn4)
        n_idx = lax.broadcasted_iota(I32, (n_cmp_pad, rows), 0)
        t_idx = q0 + jnp.bitwise_and(lax.broadcasted_iota(I32, (n_cmp_pad, rows), 1), tq - 1)
        cmp_ok = n_idx * CMP_STRIDE + (CMP_LEN - 1) <= t_idx
        s = jnp.where(cmp_ok, s, NEG)
        m = jnp.max(s, axis=0, keepdims=True)
        e = jnp.where(cmp_ok, jnp.exp2(s - m), 0.0)
        l = jnp.sum(e, axis=0, keepdims=True)
        p = e * jnp.where(l > 0.0, 1.0 / l, 0.0)
        pb = p.astype(BF16)
        ocmp_scr[g] = _dot(cmp_ref[0, N_KV + g], pb)

        jn = lax.broadcasted_iota(I32, (n_sel, n_cmp_pad), 0) * SEL_LEN
        cn = lax.broadcasted_iota(I32, (n_sel, n_cmp_pad), 1) * CMP_STRIDE
        ov = jnp.clip(jnp.minimum(cn + CMP_LEN, jn + SEL_LEN) - jnp.maximum(cn, jn), 0, None)
        w_t = (ov.astype(F32) * (1.0 / CMP_LEN)).astype(BF16)
        imp4 = _dot(w_t, pb)
        imp = imp4[:, 0:tq]
        for r in range(1, GQA):
            imp = imp + imp4[:, r * tq:(r + 1) * tq]
        j_blk = lax.broadcasted_iota(I32, (n_sel, tq), 0)
        t_q = q0 + lax.broadcasted_iota(I32, (n_sel, tq), 1)
        sel_ok = j_blk * SEL_LEN <= t_q
        back = t_q // SEL_LEN - j_blk
        forced = (j_blk == 0) | ((back >= 0) & (back < N_LOCAL))
        val = jnp.where(sel_ok & forced, BIG, jnp.where(sel_ok, imp, -BIG))
        rank = jnp.zeros((n_sel, tq), F32)
        for i in range(n_sel):
            vi = val[i:i + 1, :]
            beats = (vi > val) | ((vi == val) & (j_blk > i))
            rank = rank + beats.astype(F32)
        keep = (rank < float(min(SEL_TOPK, n_sel))) & sel_ok
        bias_t = jnp.where(keep, 0.0, -SEL_MASK_BIAS)
        bias_t = jnp.concatenate([bias_t, jnp.zeros((HEAD_DIM - n_sel, tq), F32)], axis=0)
        qa_scr[g, :HEAD_DIM, :] = jnp.concatenate(
            [qrt_ref[0, h] for h in range(g * GQA, (g + 1) * GQA)], axis=1)
        qa_scr[g, HEAD_DIM:, :] = tile_heads(bias_t.astype(BF16))

    def sel_item(g, kj, bias):
        st = pl.multiple_of(kj * TK, TK)
        k_aug = jnp.concatenate([kk_ref[0, g, pl.ds(st, TK), :], kblk_ref[pl.ds(st, TK), :]], axis=1)
        return (g, g, 2 * HEAD_DIM, k_aug, vt_ref[0, g, kj], bias)

    def win_item(g, back):
        kj = jnp.maximum(qi - back, 0)
        st = pl.multiple_of(kj * TK, TK)
        off = jnp.where(qi >= back, 0.0, NEG)
        bias = causal_bias if back == 0 else (band_bias + off if back == n_win else off)
        return (N_KV + g, g, HEAD_DIM, kk_ref[0, N_KV + g, pl.ds(st, TK), :], vt_ref[0, N_KV + g, kj], bias)

    assert N_KV == 2
    scores(sel_item(0, 0, None), 0)

    def sel_tile(kj):
        scores(sel_item(1, kj, None), 1)
        softmax_pv(sel_item(0, kj, None), 0)
        scores(sel_item(0, kj + 1, None), 0)
        softmax_pv(sel_item(1, kj, None), 1)

    def sel_pair(j, carry):
        sel_tile(2 * j)
        sel_tile(2 * j + 1)
        return carry

    lax.fori_loop(0, qi // 2, sel_pair, 0)

    @pl.when(qi % 2 == 1)
    def _():
        sel_tile(qi - 1)
    tail = ([sel_item(g, qi, causal_bias) for g in range(N_KV)]
            + [win_item(g, back) for back in range(n_win, -1, -1) for g in range(N_KV)])
    for i, item in enumerate(tail):
        if i + 1 < len(tail):
            scores(tail[i + 1], (i + 1) % 2)
        softmax_pv(item, i % 2)

    gsig = jax.nn.sigmoid(gt_ref[...])
    for g in range(N_KV):
        o_cmp, o_sel, o_win = ocmp_scr[g], flash_out(g), flash_out(N_KV + g)
        for r in range(GQA):
            h = g * GQA + r
            sl = slice(r * tq, (r + 1) * tq)
            o_h = (gsig[3 * h:3 * h + 1, :] * o_cmp[:, sl]
                   + gsig[3 * h + 1:3 * h + 2, :] * o_sel[:, sl]
                   + gsig[3 * h + 2:3 * h + 3, :] * o_win[:, sl])
            o_scr[:, h * HEAD_DIM:(h + 1) * HEAD_DIM] = o_h.T

    o = o_scr[...]
    ms = jnp.mean(o * o, axis=-1, keepdims=True)
    o_ref[...] = (o * lax.rsqrt(ms + EPS) * gn_ref[...]).astype(BF16)


def _nsa(qnt, qrt, cmp, kk, vt, kblk, gt, gn):
    batch, _, _, seq = qnt.shape
    assert TQ == TK and WINDOW % TK == 0 and TQ & (TQ - 1) == 0
    nq = seq // TQ
    width = N_HEADS * HEAD_DIM
    return pl.pallas_call(
        _nsa_kernel,
        grid=(batch, nq),
        in_specs=[
            pl.BlockSpec((1, N_HEADS, HEAD_DIM, TQ), lambda b, i: (b, 0, 0, i)),
            pl.BlockSpec((1, N_HEADS, HEAD_DIM, TQ), lambda b, i: (b, 0, 0, i)),
            pl.BlockSpec((1,) + cmp.shape[1:], lambda b, i: (b, 0, 0, 0)),
            pl.BlockSpec((1,) + kk.shape[1:], lambda b, i: (b, 0, 0, 0)),
            pl.BlockSpec((1,) + vt.shape[1:], lambda b, i: (b, 0, 0, 0, 0)),
            pl.BlockSpec((seq, LANES), lambda b, i: (0, 0)),
            pl.BlockSpec((gt.shape[0], TQ), lambda b, i: (0, b * nq + i)),
            pl.BlockSpec((1, width), lambda b, i: (0, 0)),
        ],
        out_specs=pl.BlockSpec((TQ, width), lambda b, i: (b * nq + i, 0)),
        out_shape=jax.ShapeDtypeStruct((batch * seq, width), BF16),
        scratch_shapes=[
            pltpu.VMEM((2 * N_KV, 1, GQA * TQ), F32),
            pltpu.VMEM((2 * N_KV, vt.shape[3], GQA * TQ), F32),
            pltpu.VMEM((N_KV, 2 * HEAD_DIM, GQA * TQ), BF16),
            pltpu.VMEM((N_KV, HEAD_DIM, GQA * TQ), F32),
            pltpu.VMEM((2, TK, GQA * TQ), F32),
            pltpu.VMEM((TQ, width), F32),
        ],
        compiler_params=_cparams(2),
        name="nsa",
    )(qnt, qrt, cmp, kk, vt, kblk, gt, gn)


def _pool_kernel(u_ref, w_ref, b_ref, sc_ref, gn_ref, o_ref, y_scr, halo_scr):
    si = pl.program_id(1)
    tm = u_ref.shape[1]
    halo = halo_scr.shape[0]
    cur = u_ref[0]

    @pl.when(si == 0)
    def _():
        halo_scr[...] = jnp.zeros(halo_scr.shape, F32)

    ext = jnp.concatenate([halo_scr[...], cur], axis=0)
    halo_scr[...] = cur[tm - halo:, :]
    t1 = (si * tm + 1 + lax.broadcasted_iota(I32, (tm, 1), 0)).astype(F32)
    cg = cur.shape[1] // len(POOL_SIZES)
    for gi, w in enumerate(POOL_SIZES):
        sl = slice(gi * cg, (gi + 1) * cg)
        acc = ext[:, sl]
        span = 1
        while span < w:
            acc = acc + jnp.concatenate([jnp.zeros((span, cg), F32), acc[:-span]], axis=0)
            span *= 2
        mean = acc[halo:] / jnp.minimum(t1, float(w))
        d = (mean - cur[:, sl]).astype(BF16)
        y = _dot(d, w_ref[gi]) + b_ref[:, sl]
        y_scr[:, sl] = y * sc_ref[:, sl]
    y = y_scr[...]
    ms = jnp.mean(y * y, axis=-1, keepdims=True)
    o_ref[...] = (y * lax.rsqrt(ms + EPS) * gn_ref[...]).astype(BF16)


def _pool(u3, w_pool, b_pool, pool_scale, gn_pool):
    batch, seq, c = u3.shape
    tm = TM_PROJ
    ns = seq // tm
    vec = lambda b, i: (0, 0)
    return pl.pallas_call(
        _pool_kernel,
        grid=(batch, ns),
        in_specs=[
            pl.BlockSpec((1, tm, c), lambda b, i: (b, i, 0)),
            pl.BlockSpec(w_pool.shape, lambda b, i: (0, 0, 0)),
            pl.BlockSpec((1, c), vec),
            pl.BlockSpec((1, c), vec),
            pl.BlockSpec((1, c), vec),
        ],
        out_specs=pl.BlockSpec((tm, c), lambda b, i: (b * ns + i, 0)),
        out_shape=jax.ShapeDtypeStruct((batch * seq, c), BF16),
        scratch_shapes=[pltpu.VMEM((tm, c), F32), pltpu.VMEM((max(POOL_SIZES), c), F32)],
        compiler_params=_cparams(2),
        name="pool",
    )(u3, w_pool, b_pool, pool_scale, gn_pool)


def _outproj_kernel(mn_ref, mp_ref, x_ref, w_hbm, ln_ref, wr_ref, br_ref,
                    x1_ref, h2_ref, eid_ref, gate_ref, x1_scr, ssq_scr, stage, w_ref, wsem):
    tm, d = x_ref.shape
    half = mn_ref.shape[1]
    chunk = 512

    @pl.when(pl.program_id(0) == 0)
    def _():
        x1_scr[...] = jnp.zeros(x1_scr.shape, F32)
        ssq_scr[...] = jnp.zeros(ssq_scr.shape, F32)
        rc = stage.shape[1]
        n_chunks = w_ref.shape[0] // rc

        def chunk_copy(c, slot):
            return pltpu.make_async_copy(w_hbm.at[0, pl.ds(c * rc, rc)], stage.at[slot], wsem.at[slot])

        chunk_copy(0, 0).start()
        for c in range(n_chunks):
            slot = c % 2
            if c + 1 < n_chunks:
                chunk_copy(c + 1, 1 - slot).start()
            chunk_copy(c, slot).wait()
            w_ref[c * rc:(c + 1) * rc, :] = stage[slot].astype(BF16)

    h2 = x1_scr[...] * lax.rsqrt(ssq_scr[...] * (1.0 / d) + EPS) * ln_ref[...]
    _store_token_tiles(h2_ref, _pack_bf16_halves(h2))

    logits = _dot_nt(wr_ref[...], h2.astype(BF16)) + br_ref[:, 0:1]
    ng, ne = N_EXPERT_GROUPS, EXPERTS_PER_GROUP
    row = lax.broadcasted_iota(I32, (ne, tm), 0)
    lg = jnp.where(row < ng, logits[0:ne], NEG)
    mg = jnp.max(lg, axis=0, keepdims=True)
    g_sel = jnp.min(jnp.where(lg == mg, row, ne), axis=0, keepdims=True)
    p_g = 1.0 / jnp.sum(jnp.exp(lg - mg), axis=0, keepdims=True)
    le = jnp.zeros((ne, tm), F32)
    for g in range(ng):
        le = jnp.where(g_sel == g, logits[ne * (g + 1):ne * (g + 2)], le)
    ex = jnp.exp(le - jnp.max(le, axis=0, keepdims=True))
    pe = ex / jnp.sum(ex, axis=0, keepdims=True)
    p1 = jnp.max(pe, axis=0, keepdims=True)
    i1 = jnp.min(jnp.where(pe == p1, row, ne), axis=0, keepdims=True)
    rest = jnp.where(row == i1, -1.0, pe)
    p2 = jnp.max(rest, axis=0, keepdims=True)
    i2 = jnp.min(jnp.where(rest == p2, row, ne), axis=0, keepdims=True)
    psum = p1 + p2
    zero_i = jnp.zeros((ne, tm), I32)
    eid = jnp.where(row == 0, g_sel * ne + i1, jnp.where(row == 1, g_sel * ne + i2, zero_i))
    gate = jnp.where(row == 0, p_g * p1 / psum, jnp.where(row == 1, p_g * p2 / psum, 0.0))
    eid_ref[...] = eid
    gate_ref[...] = gate

    mn = mn_ref[...]
    mp = mp_ref[...]
    ssq = jnp.zeros((tm, 1), F32)
    for c in range(d // chunk):
        sl = slice(c * chunk, (c + 1) * chunk)
        x1 = x_ref[:, sl] + (_dot(mn, w_ref[:half, sl]) + _dot(mp, w_ref[half:, sl]))
        x1_ref[:, sl] = x1
        x1_scr[:, sl] = x1
        ssq = ssq + jnp.sum(x1 * x1, axis=-1, keepdims=True)
    ssq_scr[...] = ssq


def _outproj(mix_nsa, mix_pool, x2d, w_out, ln_moe, w_router_t, b_router):
    n, d = x2d.shape
    tm = TM_PROJ
    half = mix_nsa.shape[1]
    const = lambda i: (0, 0)
    last = n // tm - 1
    cur = lambda i: (jnp.minimum(i, last), 0)
    prev = lambda i: (jnp.maximum(i - 1, 0), 0)
    prev_t = lambda i: (0, jnp.maximum(i - 1, 0))
    return pl.pallas_call(
        _outproj_kernel,
        grid=(n // tm + 1,),
        in_specs=[
            pl.BlockSpec((tm, half), cur),
            pl.BlockSpec((tm, half), cur),
            pl.BlockSpec((tm, d), cur),
            pl.BlockSpec(memory_space=pl.ANY),
            pl.BlockSpec((1, d), const),
            pl.BlockSpec(w_router_t.shape, const),
            pl.BlockSpec(b_router.shape, const),
        ],
        out_specs=[
            pl.BlockSpec((tm, d), cur),
            pl.BlockSpec((tm * SUBLANES, LANES), prev),
            pl.BlockSpec((SUBLANES, tm), prev_t),
            pl.BlockSpec((SUBLANES, tm), prev_t),
        ],
        scratch_shapes=[pltpu.VMEM((tm, d), F32), pltpu.VMEM((tm, 1), F32),
                        pltpu.VMEM((2, W_STAGE_ROWS, d), F32), pltpu.VMEM(w_out.shape[1:], BF16),
                        pltpu.SemaphoreType.DMA((2,))],
        out_shape=[
            jax.ShapeDtypeStruct((n, d), F32),
            jax.ShapeDtypeStruct((n * SUBLANES, LANES), U32),
            jax.ShapeDtypeStruct((SUBLANES, n), I32),
            jax.ShapeDtypeStruct((SUBLANES, n), F32),
        ],
        compiler_params=_cparams(1),
        name="outproj",
    )(mix_nsa, mix_pool, x2d, w_out, ln_moe, w_router_t, b_router)


def _dispatch_kernel(eid_ref, dest_ref, meta_ref, rank_scr):
    n = eid_ref.shape[1]
    t = T_RANK
    row = lax.broadcasted_iota(I32, (N_EXPERTS, t), 0)
    before = (lax.broadcasted_iota(I32, (t, t), 0) < lax.broadcasted_iota(I32, (t, t), 1)).astype(BF16)
    carry = jnp.zeros((N_EXPERTS, 1), F32)
    for k in range(2):
        for j in range(n // t):
            sl = slice(j * t, (j + 1) * t)
            oh = row == eid_ref[k:k + 1, sl]
            ohf = oh.astype(F32)
            prior = _dot(ohf.astype(BF16), before) + carry
            rank_scr[k:k + 1, sl] = jnp.sum(jnp.where(oh, prior, 0.0), axis=0, keepdims=True)
            carry = carry + jnp.sum(ohf, axis=1, keepdims=True)
    counts = jnp.broadcast_to(carry, (N_EXPERTS, LANES))
    padded = jnp.floor((counts + (TM_MOE - 1.0)) * (1.0 / TM_MOE)) * TM_MOE
    ends = padded
    r_idx = lax.broadcasted_iota(I32, (N_EXPERTS, LANES), 0)
    sh = 1
    while sh < N_EXPERTS:
        ends = ends + jnp.where(r_idx >= sh, pltpu.roll(ends, sh, 0), 0.0)
        sh *= 2
    starts = (ends - padded)[:, 0:1]
    for k in range(2):
        for j in range(n // t):
            sl = slice(j * t, (j + 1) * t)
            oh = row == eid_ref[k:k + 1, sl]
            base = jnp.sum(jnp.where(oh, starts, 0.0), axis=0, keepdims=True)
            dest_ref[k:k + 1, sl] = (rank_scr[k:k + 1, sl] + base).astype(I32)
    for k in range(2, SUBLANES):
        dest_ref[k:k + 1, :] = jnp.zeros((1, n), I32)
    blk_start = lax.broadcasted_iota(I32, (N_EXPERTS, LANES), 1).astype(F32) * TM_MOE
    blk_expert = jnp.sum((ends <= blk_start).astype(F32), axis=0, keepdims=True)
    blk_expert = jnp.minimum(blk_expert, N_EXPERTS - 1.0)
    n_used = ends[N_EXPERTS - 1:N_EXPERTS, :] * (1.0 / TM_MOE)
    on_diag = r_idx == lax.broadcasted_iota(I32, (N_EXPERTS, LANES), 1)
    end_blk = jnp.sum(jnp.where(on_diag, ends, 0.0), axis=0, keepdims=True) * (1.0 / TM_MOE)
    n_blk = jnp.sum(jnp.where(on_diag, padded, 0.0), axis=0, keepdims=True) * (1.0 / TM_MOE)
    r8 = lax.broadcasted_iota(I32, (SUBLANES, LANES), 0)
    meta = jnp.where(r8 == 0, blk_expert,
                     jnp.where(r8 == 1, n_used, jnp.where(r8 == 2, end_blk, jnp.where(r8 == 3, n_blk, 0.0))))
    meta_ref[...] = meta.astype(I32)


def _dispatch(eid):
    n = eid.shape[1]
    return pl.pallas_call(
        _dispatch_kernel,
        out_shape=[jax.ShapeDtypeStruct((SUBLANES, n), I32),
                   jax.ShapeDtypeStruct((SUBLANES, LANES), I32)],
        scratch_shapes=[pltpu.VMEM((SUBLANES, n), F32)],
        compiler_params=pltpu.CompilerParams(vmem_limit_bytes=VMEM_LIMIT_BYTES),
        name="dispatch",
    )(eid)


def _row_copy(src_ref, src_row, dst_ref, dst_row, sem):
    tile = lambda ref, row: ref.at[pl.ds(pl.multiple_of(row * SUBLANES, SUBLANES), SUBLANES)]
    return pltpu.make_async_copy(tile(src_ref, src_row), tile(dst_ref, dst_row), sem)


def _scatter_kernel(dest_ref, meta_ref, h_ref, xs_ref, zero_buf, sem, zsem):
    tm = h_ref.shape[0] // SUBLANES
    blk = zero_buf.shape[0]

    @pl.when(pl.program_id(0) == 0)
    def _():
        zero_buf[...] = jnp.zeros(zero_buf.shape, zero_buf.dtype)
        n_used = meta_ref[1, 0]

        def zero_block(b):
            return pltpu.make_async_copy(zero_buf, xs_ref.at[pl.ds(pl.multiple_of(b * blk, blk), blk)], zsem)

        def sweep(act):
            def per_expert(e, c):
                @pl.when(meta_ref[3, e] > 0)
                def _():
                    act(zero_block(meta_ref[2, e] - 1))
                return c

            def per_tail(b, c):
                @pl.when(b >= n_used)
                def _():
                    act(zero_block(b))
                return c

            lax.fori_loop(0, N_EXPERTS, per_expert, 0)
            lax.fori_loop(0, xs_ref.shape[0] // blk, per_tail, 0)

        sweep(lambda cp: cp.start())
        sweep(lambda cp: cp.wait())

    def start(r, c):
        for k in range(2):
            _row_copy(h_ref, r, xs_ref, dest_ref[0, k, r], sem).start(priority=k)
        return c

    lax.fori_loop(0, tm, start, 0, unroll=DMA_ISSUE_UNROLL)
    for k in range(2):
        pltpu.make_async_copy(h_ref, xs_ref.at[pl.ds(0, tm * SUBLANES)], sem).wait()


def _scatter(dest3, meta, h2, cap):
    nsteps, _, tm = dest3.shape
    return pl.pallas_call(
        _scatter_kernel,
        grid=(nsteps,),
        in_specs=[
            pl.BlockSpec((1, 2, tm), lambda i: (i, 0, 0), memory_space=pltpu.SMEM),
            pl.BlockSpec(memory_space=pltpu.SMEM),
            pl.BlockSpec((tm * SUBLANES, LANES), lambda i: (i, 0)),
        ],
        out_specs=pl.BlockSpec(memory_space=pl.ANY),
        out_shape=jax.ShapeDtypeStruct((cap * SUBLANES, LANES), h2.dtype),
        scratch_shapes=[pltpu.VMEM((TM_MOE * SUBLANES, LANES), h2.dtype), pltpu.SemaphoreType.DMA(()),
                        pltpu.SemaphoreType.DMA(())],
        compiler_params=_cparams(1),
        name="scatter",
    )(dest3, meta, h2)


def _experts_kernel(meta_ref, xs_ref, wg_hbm, wu_hbm, wd_hbm, ys_ref,
                    wg_buf, wu_buf, wd_buf, wg16, wu16, wd16, elist_ref, ord_ref, sem):
    c = pl.program_id(0)
    n_used = meta_ref[1, 0]
    e = meta_ref[0, c]
    first = (c == 0) | (e != meta_ref[0, jnp.maximum(c - 1, 0)])

    def fetch(expert, slot):
        return [pltpu.make_async_copy(w.at[expert], buf.at[slot], sem.at[slot, i])
                for i, (w, buf) in enumerate(((wg_hbm, wg_buf), (wu_hbm, wu_buf), (wd_hbm, wd_buf)))]

    @pl.when(c == 0)
    def _():
        def scan(b, cnt):
            eb = meta_ref[0, b]
            take = ((b == 0) | (eb != meta_ref[0, jnp.maximum(b - 1, 0)])) & (b < n_used)

            @pl.when(take)
            def _():
                elist_ref[cnt] = eb

            return cnt + take.astype(I32)

        cnt = lax.fori_loop(0, pl.num_programs(0), scan, 0)
        ord_ref[0] = 0
        ord_ref[1] = cnt
        for k in range(W_SLOTS - 1):
            @pl.when(k < cnt)
            def _(k=k):
                for cp, prio in zip(fetch(elist_ref[k], k), WEIGHT_DMA_PRIORITIES):
                    cp.start(priority=prio)

    @pl.when(first & (c < n_used))
    def _():
        @pl.when(c > 0)
        def _():
            ord_ref[0] = ord_ref[0] + 1
        k = ord_ref[0]
        for cp in fetch(e, k % W_SLOTS):
            cp.wait()
        ahead = k + (W_SLOTS - 1)

        @pl.when(ahead < ord_ref[1])
        def _():
            for cp, prio in zip(fetch(elist_ref[jnp.minimum(ahead, N_EXPERTS - 1)], ahead % W_SLOTS),
                                WEIGHT_DMA_PRIORITIES):
                cp.start(priority=prio)

    def mlp(weights):
        lo, hi = _unpack_bf16_halves(_load_token_tiles(xs_ref))
        lo, hi = lo.astype(BF16), hi.astype(BF16)
        half = lo.shape[1]
        wg, wu, wd = weights

        def proj(w):
            return _dot(lo, w[:half, :]) + _dot(hi, w[half:, :])

        act = (jax.nn.silu(proj(wg)) * proj(wu)).astype(BF16)
        _store_token_tiles(ys_ref, _pack_bf16_halves(_dot(act, wd)))

    @pl.when(first & (c < n_used))
    def _():
        slot = ord_ref[0] % W_SLOTS
        cast = [buf[slot].astype(BF16) for buf in (wg_buf, wu_buf, wd_buf)]
        for ref16, w in zip((wg16, wu16, wd16), cast):
            ref16[...] = w
        mlp(cast)

    @pl.when(jnp.logical_not(first) & (c < n_used))
    def _():
        mlp((wg16[...], wu16[...], wd16[...]))

    @pl.when(c >= n_used)
    def _():
        ys_ref[...] = jnp.zeros(ys_ref.shape, U32)


def _experts(meta, xs, w_gate, w_up, w_down):
    rows = TM_MOE * SUBLANES
    nblk = xs.shape[0] // rows
    d, ff = w_gate.shape[1:]

    def row_map(c, meta):
        return (jnp.minimum(c, meta[1, 0] - 1), 0)

    hbm = pl.BlockSpec(memory_space=pl.ANY)
    return pl.pallas_call(
        _experts_kernel,
        grid_spec=pltpu.PrefetchScalarGridSpec(
            num_scalar_prefetch=1,
            grid=(nblk,),
            in_specs=[pl.BlockSpec((rows, LANES), row_map), hbm, hbm, hbm],
            out_specs=pl.BlockSpec((rows, LANES), lambda c, meta: (c, 0)),
            scratch_shapes=[
                pltpu.VMEM((W_SLOTS, d, ff), F32), pltpu.VMEM((W_SLOTS, d, ff), F32),
                pltpu.VMEM((W_SLOTS, ff, d), F32),
                pltpu.VMEM((d, ff), BF16), pltpu.VMEM((d, ff), BF16), pltpu.VMEM((ff, d), BF16),
                pltpu.SMEM((N_EXPERTS,), I32),
                pltpu.SMEM((2,), I32),
                pltpu.SemaphoreType.DMA((W_SLOTS, 3)),
            ],
        ),
        out_shape=jax.ShapeDtypeStruct(xs.shape, U32),
        compiler_params=_cparams(1),
        name="experts",
    )(meta, xs, w_gate, w_up, w_down)


def _combine_kernel(dest_ref, dest_next_ref, x1_ref, gate_ref, ln_ref, ys_ref, o_ref, ybuf, sem):
    i = pl.program_id(0)
    tm = x1_ref.shape[0]
    slot = i % 2

    def issue(d_ref, s):
        def start(r, c):
            for k in range(2):
                _row_copy(ys_ref, d_ref[0, k, r], ybuf.at[s, k], r, sem.at[s]).start(priority=k)
            return c

        lax.fori_loop(0, tm, start, 0, unroll=DMA_ISSUE_UNROLL)

    @pl.when(i == 0)
    def _():
        issue(dest_ref, 0)

    @pl.when(i + 1 < pl.num_programs(0))
    def _():
        issue(dest_next_ref, 1 - slot)

    for k in range(2):
        pltpu.make_async_copy(ys_ref.at[pl.ds(0, tm * SUBLANES)], ybuf.at[slot, k], sem.at[slot]).wait()
    gate = gate_ref[...]
    lo0, hi0 = _unpack_bf16_halves(_load_token_tiles(ybuf.at[slot, 0]))
    lo1, hi1 = _unpack_bf16_halves(_load_token_tiles(ybuf.at[slot, 1]))
    g0, g1 = gate[:, 0:1], gate[:, 1:2]
    moe = jnp.concatenate([lo0 * g0 + lo1 * g1, hi0 * g0 + hi1 * g1], axis=1)
    x2 = x1_ref[...] + moe
    ms = jnp.mean(x2 * x2, axis=-1, keepdims=True)
    o_ref[...] = x2 * lax.rsqrt(ms + EPS) * ln_ref[...]


def _combine(dest3, x1, gate_rows, ln_final, ys):
    n, d = x1.shape
    tm = TM_COMBINE
    last = n // tm - 1
    return pl.pallas_call(
        _combine_kernel,
        grid=(n // tm,),
        in_specs=[
            pl.BlockSpec((1, 2, tm), lambda i: (i, 0, 0), memory_space=pltpu.SMEM),
            pl.BlockSpec((1, 2, tm), lambda i: (jnp.minimum(i + 1, last), 0, 0), memory_space=pltpu.SMEM),
            pl.BlockSpec((tm, d), lambda i: (i, 0)),
            pl.BlockSpec((tm, LANES), lambda i: (i, 0)),
            pl.BlockSpec((1, d), lambda i: (0, 0)),
            pl.BlockSpec(memory_space=pl.ANY),
        ],
        out_specs=pl.BlockSpec((tm, d), lambda i: (i, 0)),
        out_shape=jax.ShapeDtypeStruct((n, d), F32),
        scratch_shapes=[pltpu.VMEM((2, 2, tm * SUBLANES, LANES), U32), pltpu.SemaphoreType.DMA((2,))],
        compiler_params=_cparams(1),
        name="combine",
    )(dest3, dest3, x1, gate_rows, ln_final, ys)


def kernel(x, positions, ln_mix, w_in, pe_cmp_k, w_cmp_k1, w_cmp_k2, pe_cmp_v, w_cmp_v1, w_cmp_v2, w_pool, b_pool, pool_scale, gn_nsa, gn_pool, w_out, ln_moe, w_router_group, b_router_group, w_router_expert, b_router_expert, w_gate, w_up, w_down, ln_final):
    batch, seq, d = x.shape
    n = batch * seq
    depth = w_in.shape[0]
    nsa_w = N_HEADS * HEAD_DIM
    kv_w = N_KV * HEAD_DIM
    cut_gates = nsa_w + 6 * kv_w
    cut_pool = cut_gates + N_GATES

    inv_freq = ROPE_THETA ** (-jnp.arange(0, ROT_DIM, 2, dtype=F32) / ROT_DIM)
    ang_t = inv_freq[:, None] * positions.astype(F32).reshape(1, n)
    kblk = (jnp.arange(seq, dtype=I32)[:, None] // SEL_LEN == jnp.arange(LANES, dtype=I32)[None, :]).astype(BF16)

    x2d = x.reshape(n, d)
    assert depth == 1, "single-layer operation"
    for l in range(depth):
        gate_rows_pad = -(-N_GATES // SUBLANES) * SUBLANES
        qnt, qrt, kvc, kk, vt, gt, u = _inproj(x2d, ln_mix[l][None], ang_t, jnp.swapaxes(w_in, 1, 2),
                                               gate_rows_pad, batch, seq)

        half = CMP_LEN * HEAD_DIM // 2
        def stack_w1(w):
            return jnp.concatenate([w[:half], w[half:]], axis=1)
        w1 = jnp.stack([stack_w1(w_cmp_k1[l]), stack_w1(w_cmp_v1[l])]).astype(BF16)
        w2 = jnp.stack([w_cmp_k2[l], w_cmp_v2[l]]).astype(BF16)
        def pe_rows(pe):
            return jnp.pad(pe.reshape(2, half), ((0, SUBLANES - 2), (0, 0)))
        pe2 = jnp.stack([pe_rows(pe_cmp_k[l]), pe_rows(pe_cmp_v[l])]).astype(BF16)
        cmp = _compress(kvc, pe2, w1, w2)

        mix_nsa = _nsa(qnt, qrt, cmp, kk, vt, kblk, gt, gn_nsa[l][None])
        mix_pool = _pool(u.reshape(batch, seq, -1), w_pool[l].astype(BF16), b_pool[l][None],
                         pool_scale[l][None], gn_pool[l][None])

        ng, ne = N_EXPERT_GROUPS, EXPERTS_PER_GROUP
        w_r = jnp.concatenate([
            w_router_group[l].T, jnp.zeros((ne - ng, d), F32),
            jnp.transpose(w_router_expert[l], (0, 2, 1)).reshape(ng * ne, d),
            jnp.zeros((LANES - ne - ng * ne, d), F32)], axis=0).astype(BF16)
        b_r = jnp.concatenate([
            b_router_group[l], jnp.zeros((ne - ng,), F32), b_router_expert[l].reshape(-1),
            jnp.zeros((LANES - ne - ng * ne,), F32)])
        b_r = jnp.broadcast_to(b_r[:, None], (LANES, LANES))
        x1, h2, eid, gate = _outproj(mix_nsa, mix_pool, x2d, w_out, ln_moe[l][None], w_r, b_r)

        dest, meta = _dispatch(eid)
        cap = (2 * n // TM_MOE + N_EXPERTS) * TM_MOE
        def per_tile(tm):
            return jnp.transpose(dest[:2].reshape(2, n // tm, tm), (1, 0, 2))
        xs = _scatter(per_tile(TS_SCATTER), meta, h2, cap)
        ys = _experts(meta, xs, w_gate[l], w_up[l], w_down[l])
        dest_tok = per_tile(TM_COMBINE)
        gate_rows = jnp.pad(gate[:2].T, ((0, 0), (0, LANES - 2)))
        x2d = _combine(dest_tok, x1, gate_rows, ln_final[None], ys)
    return x2d.reshape(batch, seq, d)
```

```python
import jax
import jax.numpy as jnp
from jax import lax
from jax.experimental import pallas as pl
from jax.experimental.pallas import tpu as pltpu

F32 = jnp.float32
BF16 = jnp.bfloat16
I32 = jnp.int32

HEAD_DIM = 128
N_HEADS = 8
N_KV = 2
GQA = N_HEADS // N_KV
N_GATES = 3 * N_HEADS
ROT_DIM = HEAD_DIM // 4
ROPE_THETA = 500000.0
CMP_LEN = 32
CMP_STRIDE = 16
CMP_HIDDEN = 2 * HEAD_DIM
SEL_LEN = 64
SEL_TOPK = 16
N_LOCAL = 2
WINDOW = 512
POOL_SIZES = (2, 4, 8, 16)
N_EXPERT_GROUPS = 4
EXPERTS_PER_GROUP = 8
N_EXPERTS = N_EXPERT_GROUPS * EXPERTS_PER_GROUP
EPS = 1e-6
NEG = -1e30
BIG = 1e30
SCALE = HEAD_DIM ** -0.5
LOG2E = 1.4426950408889634
Q_SCALE = SCALE * LOG2E

LANES = 128
SUBLANES = 8
VMEM_LIMIT_BYTES = 56 * 1024 * 1024

TM_PROJ = 512
TQ = 256
TK = 256
SEL_MASK_BIAS = 2.0 ** 40
V_ONES_ROWS = 16
TM_MOE = 256
TS_SCATTER = 1024
TM_COMBINE = 512
WEIGHT_DMA_PRIORITIES = (0, 1, 1)
W_STAGE_ROWS = 256
W_SLOTS = 3
DMA_ISSUE_UNROLL = 8
T_RANK = 512


def _cparams(n_axes, flags=None):
    return pltpu.CompilerParams(dimension_semantics=("arbitrary",) * n_axes,
                                vmem_limit_bytes=VMEM_LIMIT_BYTES, flags=flags)


def _dot(a, b):
    return jnp.dot(a, b, preferred_element_type=F32)


def _dot_nt(a, b):
    return lax.dot_general(a, b, (((1,), (1,)), ((), ())), preferred_element_type=F32)


U32 = jnp.uint32
_HI16 = 0xFFFF0000


def _pack_bf16_halves(x):
    c = x.shape[1] // 2
    bits = lambda t: lax.bitcast_convert_type(t.astype(BF16).astype(F32), U32)
    return lax.shift_right_logical(bits(x[:, :c]), U32(16)) | (bits(x[:, c:]) & U32(_HI16))


def _store_token_tiles(ref, words):
    m, width = words.shape
    assert width == SUBLANES * LANES
    for j in range(SUBLANES):
        ref[pl.ds(j, m, stride=SUBLANES), :] = words[:, j * LANES:(j + 1) * LANES]


def _load_token_tiles(ref):
    m = ref.shape[0] // SUBLANES
    return jnp.concatenate([ref[pl.ds(j, m, stride=SUBLANES), :] for j in range(SUBLANES)], axis=1)


def _unpack_bf16_halves(w):
    lo = lax.bitcast_convert_type(lax.shift_left(w, U32(16)), F32)
    hi = lax.bitcast_convert_type(w & U32(_HI16), F32)
    return lo, hi


def _rope(t, cos, sin, axis):
    idx = lax.broadcasted_iota(I32, t.shape, axis)
    partner = jnp.where(idx < ROT_DIM // 2,
                        pltpu.roll(t, HEAD_DIM - ROT_DIM // 2, axis),
                        pltpu.roll(t, ROT_DIM // 2, axis))
    return t * cos + partner * sin


def _inproj_kernel(x_ref, ln_ref, ang_ref, wt_hbm,
                   qnt_ref, qrt_ref, kvc_ref, kk_ref, vt_ref, gt_ref, u_ref,
                   kc_scr, stage, wq_ref, wv_ref, wg_ref, wkc_ref, wk_ref, wu_ref, wsem):
    @pl.when(pl.program_id(0) == 0)
    def _():
        rc = stage.shape[1]
        q_w, kv_w = N_HEADS * HEAD_DIM, N_KV * HEAD_DIM
        o_kc, o_ks = q_w, q_w + 2 * kv_w
        o_vs, o_kw, o_vw, o_g = o_ks + kv_w, o_ks + 2 * kv_w, o_ks + 3 * kv_w, o_ks + 4 * kv_w
        o_u = o_g + N_GATES
        pieces = ([(r0, rc, wq_ref, r0) for r0 in range(0, q_w, rc)]
                  + [(o_kc + r0, rc, wkc_ref, r0) for r0 in range(0, 2 * kv_w, rc)]
                  + [(o_ks, kv_w, wk_ref, 0), (o_kw, kv_w, wk_ref, kv_w)]
                  + [(o_vs, kv_w, wv_ref, 0), (o_vw, kv_w, wv_ref, kv_w)]
                  + [(o_g, N_GATES, wg_ref, 0)]
                  + [(o_u + r0, rc, wu_ref, r0) for r0 in range(0, wu_ref.shape[0], rc)])
        assert all(n <= rc and src % SUBLANES == 0 for src, n, _, _ in pieces)
        wg_ref[...] = jnp.zeros(wg_ref.shape, BF16)

        def piece_copy(i, slot):
            src, rows, _, _ = pieces[i]
            return pltpu.make_async_copy(wt_hbm.at[0, pl.ds(src, rows)], stage.at[slot, pl.ds(0, rows)],
                                         wsem.at[slot])

        piece_copy(0, 0).start()
        for i, (src, rows, dst, dst_row) in enumerate(pieces):
            slot = i % 2
            if i + 1 < len(pieces):
                piece_copy(i + 1, 1 - slot).start()
            piece_copy(i, slot).wait()
            dst[dst_row:dst_row + rows, :] = stage[slot, :rows, :].astype(BF16)

    x = x_ref[...]
    ms = jnp.mean(x * x, axis=-1, keepdims=True)
    hb = (x * lax.rsqrt(ms + EPS) * ln_ref[...]).astype(BF16)
    chunk = 4 * HEAD_DIM

    ang = ang_ref[...]
    ca, sa = jnp.cos(ang), jnp.sin(ang)
    rest = (HEAD_DIM - ROT_DIM, ang.shape[1])
    cost = jnp.concatenate([ca, ca, jnp.ones(rest, F32)], axis=0)
    sint = jnp.concatenate([-sa, sa, jnp.zeros(rest, F32)], axis=0)
    cos, sin = cost.T, sint.T
    for ch in range(2):
        rt = _dot_nt(wq_ref[ch * chunk:(ch + 1) * chunk, :], hb)
        for j in range(4):
            t = rt[j * HEAD_DIM:(j + 1) * HEAD_DIM, :]
            qnt_ref[0, ch * 4 + j] = (t * Q_SCALE).astype(BF16)
            qrt_ref[0, ch * 4 + j] = (_rope(t, cost, sint, 0) * Q_SCALE).astype(BF16)
    rt = _dot_nt(wv_ref[...], hb)
    ones = jnp.ones((V_ONES_ROWS, TK), BF16)
    for j in range(4):
        for s in range(vt_ref.shape[2]):
            vt_ref[0, j, s] = jnp.concatenate(
                [rt[j * HEAD_DIM:(j + 1) * HEAD_DIM, s * TK:(s + 1) * TK].astype(BF16), ones], axis=0)
    gt_ref[...] = _dot_nt(wg_ref[...], hb)

    r = _dot_nt(hb, wkc_ref[...])
    groups = kvc_ref.shape[2]
    for j in range(4):
        kc_scr[j] = r[:, j * HEAD_DIM:(j + 1) * HEAD_DIM]
        for i in range(CMP_STRIDE):
            kvc_ref[0, j, :, i * HEAD_DIM:(i + 1) * HEAD_DIM] = (
                kc_scr[j, pl.ds(i, groups, stride=CMP_STRIDE), :].astype(BF16))
    r = _dot_nt(hb, wk_ref[...])
    for j in range(4):
        kk_ref[0, j] = _rope(r[:, j * HEAD_DIM:(j + 1) * HEAD_DIM], cos, sin, 1).astype(BF16)
    for ch in range(u_ref.shape[1] // chunk):
        u_ref[:, ch * chunk:(ch + 1) * chunk] = _dot_nt(hb, wu_ref[ch * chunk:(ch + 1) * chunk, :])


def _inproj(x2d, ln, ang_t, w_in_t, gate_rows, batch, seq):
    n, d = x2d.shape
    tm = TM_PROJ
    assert tm % TK == 0
    spb = seq // tm
    const = lambda i: (0, 0)
    q_w, kv_w = N_HEADS * HEAD_DIM, N_KV * HEAD_DIM
    pool_w = w_in_t.shape[1] - (q_w + 6 * kv_w + N_GATES)
    row_heads = lambda i: (i // spb, 0, i % spb, 0)
    col_heads = lambda i: (i // spb, 0, 0, i % spb)
    return pl.pallas_call(
        _inproj_kernel,
        grid=(n // tm,),
        in_specs=[
            pl.BlockSpec((tm, d), lambda i: (i, 0)),
            pl.BlockSpec((1, d), const),
            pl.BlockSpec((ang_t.shape[0], tm), lambda i: (0, i)),
            pl.BlockSpec(memory_space=pl.ANY),
        ],
        out_specs=[
            pl.BlockSpec((1, N_HEADS, HEAD_DIM, tm), col_heads),
            pl.BlockSpec((1, N_HEADS, HEAD_DIM, tm), col_heads),
            pl.BlockSpec((1, 4, tm // CMP_STRIDE, CMP_STRIDE * HEAD_DIM), row_heads),
            pl.BlockSpec((1, 4, tm, HEAD_DIM), row_heads),
            pl.BlockSpec((1, 4, tm // TK, HEAD_DIM + V_ONES_ROWS, TK), lambda i: (i // spb, 0, i % spb, 0, 0)),
            pl.BlockSpec((gate_rows, tm), lambda i: (0, i)),
            pl.BlockSpec((tm, pool_w), lambda i: (i, 0)),
        ],
        out_shape=[
            jax.ShapeDtypeStruct((batch, N_HEADS, HEAD_DIM, seq), BF16),
            jax.ShapeDtypeStruct((batch, N_HEADS, HEAD_DIM, seq), BF16),
            jax.ShapeDtypeStruct((batch, 4, seq // CMP_STRIDE, CMP_STRIDE * HEAD_DIM), BF16),
            jax.ShapeDtypeStruct((batch, 4, seq, HEAD_DIM), BF16),
            jax.ShapeDtypeStruct((batch, 4, seq // TK, HEAD_DIM + V_ONES_ROWS, TK), BF16),
            jax.ShapeDtypeStruct((gate_rows, n), F32),
            jax.ShapeDtypeStruct((n, pool_w), F32),
        ],
        scratch_shapes=[
            pltpu.VMEM((4, tm, HEAD_DIM), F32),
            pltpu.VMEM((2, W_STAGE_ROWS, d), F32),
            pltpu.VMEM((q_w, d), BF16), pltpu.VMEM((2 * kv_w, d), BF16), pltpu.VMEM((gate_rows, d), BF16),
            pltpu.VMEM((2 * kv_w, d), BF16), pltpu.VMEM((2 * kv_w, d), BF16), pltpu.VMEM((pool_w, d), BF16),
            pltpu.SemaphoreType.DMA((2,)),
        ],
        compiler_params=_cparams(1),
        name="inproj",
    )(x2d, ln, ang_t, w_in_t)


def _compress_kernel(t_ref, pe_ref, w1_ref, w2_ref, o_ref):
    t2 = t_ref[0, 0]
    w1 = w1_ref[0]
    a = _dot(t2, w1)
    pb = _dot(pe_ref[0], w1)
    bias = pb[0:1, :CMP_HIDDEN] + pb[1:2, CMP_HIDDEN:]
    nrow = a.shape[0]
    nxt = pltpu.roll(a[:, CMP_HIDDEN:], nrow - 1, 0)
    hid = jax.nn.gelu(a[:, :CMP_HIDDEN] + nxt + bias)
    out = _dot(hid.astype(BF16), w2_ref[0])
    row = lax.broadcasted_iota(I32, out.shape, 0)
    out = jnp.where(row < nrow - 1, out, 0.0)
    is_value = pl.program_id(1) >= N_KV
    o_ref[0, 0] = jnp.where(is_value, out.T, out).astype(BF16)


def _compress(kvc2, pe2, w1, w2):
    batch, four, nrow, width = kvc2.shape
    return pl.pallas_call(
        _compress_kernel,
        grid=(batch, four),
        in_specs=[
            pl.BlockSpec((1, 1, nrow, width), lambda b, j: (b, j, 0, 0)),
            pl.BlockSpec((1, SUBLANES, width), lambda b, j: (j // N_KV, 0, 0)),
            pl.BlockSpec((1, width, 2 * CMP_HIDDEN), lambda b, j: (j // N_KV, 0, 0)),
            pl.BlockSpec((1, CMP_HIDDEN, HEAD_DIM), lambda b, j: (j // N_KV, 0, 0)),
        ],
        out_specs=pl.BlockSpec((1, 1, nrow, HEAD_DIM), lambda b, j: (b, j, 0, 0)),
        out_shape=jax.ShapeDtypeStruct((batch, four, nrow, HEAD_DIM), BF16),
        compiler_params=_cparams(2),
        name="compress",
    )(kvc2, pe2, w1, w2)


def _nsa_kernel(qnt_ref, qrt_ref, cmp_ref, kk_ref, vt_ref, kblk_ref, gt_ref, gn_ref, o_ref,
                m_scr, acc_scr, qa_scr, ocmp_scr, s_scr, o_scr):
    qi = pl.program_id(1)
    tq = qnt_ref.shape[3]
    rows = GQA * tq
    n_cmp_pad = cmp_ref.shape[2]
    n_sel = kk_ref.shape[2] // SEL_LEN
    q0 = qi * tq

    m_scr[...] = jnp.full(m_scr.shape, NEG, F32)
    acc_scr[...] = jnp.zeros(acc_scr.shape, F32)

    def scores(item, slot):
        c, g, q_rows, k_tile, v_t, bias = item
        s_scr[slot] = _dot(k_tile, qa_scr[g, :q_rows, :])

    def softmax_pv(item, slot):
        c, g, q_rows, k_tile, v_t, bias = item
        s = s_scr[slot]
        if bias is not None:
            s = s + bias
        m_old = m_scr[c]
        m_new = jnp.maximum(m_old, jnp.max(s, axis=0, keepdims=True))
        alpha = jnp.exp2(m_old - m_new)
        p = jnp.exp2(s - m_new)
        acc_scr[c] = alpha * acc_scr[c] + _dot(v_t, p.astype(BF16))
        m_scr[c] = m_new

    def flash_out(c):
        return acc_scr[c, :HEAD_DIM, :] / acc_scr[c, HEAD_DIM:HEAD_DIM + 1, :]

    def tile_heads(a):
        return jnp.concatenate([a] * GQA, axis=1)

    kk_i = lax.broadcasted_iota(I32, (TK, tq), 0)
    qq_i = lax.broadcasted_iota(I32, (TK, tq), 1)
    causal_bias = tile_heads(jnp.where(kk_i <= qq_i, 0.0, NEG))
    band_bias = tile_heads(jnp.where(kk_i > qq_i, 0.0, NEG))
    n_win = WINDOW // TK

    for g in range(N_KV):
        qn4 = jnp.concatenate([qnt_ref[0, h] for h in range(g * GQA, (g + 1) * GQA)], axis=1)
        s = _dot(cmp_ref[0, g], qn4)
        n_idx = lax.broadcasted_iota(I32, (n_cmp_pad, rows), 0)
        t_idx = q0 + jnp.bitwise_and(lax.broadcasted_iota(I32, (n_cmp_pad, rows), 1), tq - 1)
        cmp_ok = n_idx * CMP_STRIDE + (CMP_LEN - 1) <= t_idx
        s = jnp.where(cmp_ok, s, NEG)
        m = jnp.max(s, axis=0, keepdims=True)
        e = jnp.where(cmp_ok, jnp.exp2(s - m), 0.0)
        l = jnp.sum(e, axis=0, keepdims=True)
        p = e * jnp.where(l > 0.0, 1.0 / l, 0.0)
        pb = p.astype(BF16)
        ocmp_scr[g] = _dot(cmp_ref[0, N_KV + g], pb)

        jn = lax.broadcasted_iota(I32, (n_sel, n_cmp_pad), 0) * SEL_LEN
        cn = lax.broadcasted_iota(I32, (n_sel, n_cmp_pad), 1) * CMP_STRIDE
        ov = jnp.clip(jnp.minimum(cn + CMP_LEN, jn + SEL_LEN) - jnp.maximum(cn, jn), 0, None)
        w_t = (ov.astype(F32) * (1.0 / CMP_LEN)).astype(BF16)
        imp4 = _dot(w_t, pb)
        imp = imp4[:, 0:tq]
        for r in range(1, GQA):
            imp = imp + imp4[:, r * tq:(r + 1) * tq]
        j_blk = lax.broadcasted_iota(I32, (n_sel, tq), 0)
        t_q = q0 + lax.broadcasted_iota(I32, (n_sel, tq), 1)
        sel_ok = j_blk * SEL_LEN <= t_q
        back = t_q // SEL_LEN - j_blk
        forced = (j_blk == 0) | ((back >= 0) & (back < N_LOCAL))
        val = jnp.where(sel_ok & forced, BIG, jnp.where(sel_ok, imp, -BIG))
        rank = jnp.zeros((n_sel, tq), F32)
        for i in range(n_sel):
            vi = val[i:i + 1, :]
            beats = (vi > val) | ((vi == val) & (j_blk > i))
            rank = rank + beats.astype(F32)
        keep = (rank < float(min(SEL_TOPK, n_sel))) & sel_ok
        bias_t = jnp.where(keep, 0.0, -SEL_MASK_BIAS)
        bias_t = jnp.concatenate([bias_t, jnp.zeros((HEAD_DIM - n_sel, tq), F32)], axis=0)
        qa_scr[g, :HEAD_DIM, :] = jnp.concatenate(
            [qrt_ref[0, h] for h in range(g * GQA, (g + 1) * GQA)], axis=1)
        qa_scr[g, HEAD_DIM:, :] = tile_heads(bias_t.astype(BF16))

    def sel_item(g, kj, bias):
        st = pl.multiple_of(kj * TK, TK)
        k_aug = jnp.concatenate([kk_ref[0, g, pl.ds(st, TK), :], kblk_ref[pl.ds(st, TK), :]], axis=1)
        return (g, g, 2 * HEAD_DIM, k_aug, vt_ref[0, g, kj], bias)

    def win_item(g, back):
        kj = jnp.maximum(qi - back, 0)
        st = pl.multiple_of(kj * TK, TK)
        off = jnp.where(qi >= back, 0.0, NEG)
        bias = causal_bias if back == 0 else (band_bias + off if back == n_win else off)
        return (N_KV + g, g, HEAD_DIM, kk_ref[0, N_KV + g, pl.ds(st, TK), :], vt_ref[0, N_KV + g, kj], bias)

    assert N_KV == 2
    scores(sel_item(0, 0, None), 0)

    def sel_tile(kj):
        scores(sel_item(1, kj, None), 1)
        softmax_pv(sel_item(0, kj, None), 0)
        scores(sel_item(0, kj + 1, None), 0)
        softmax_pv(sel_item(1, kj, None), 1)

    def sel_pair(j, carry):
        sel_tile(2 * j)
        sel_tile(2 * j + 1)
        return carry

    lax.fori_loop(0, qi // 2, sel_pair, 0)

    @pl.when(qi % 2 == 1)
    def _():
        sel_tile(qi - 1)
    tail = ([sel_item(g, qi, causal_bias) for g in range(N_KV)]
            + [win_item(g, back) for back in range(n_win, -1, -1) for g in range(N_KV)])
    for i, item in enumerate(tail):
        if i + 1 < len(tail):
            scores(tail[i + 1], (i + 1) % 2)
        softmax_pv(item, i % 2)

    gsig = jax.nn.sigmoid(gt_ref[...])
    for g in range(N_KV):
        o_cmp, o_sel, o_win = ocmp_scr[g], flash_out(g), flash_out(N_KV + g)
        for r in range(GQA):
            h = g * GQA + r
            sl = slice(r * tq, (r + 1) * tq)
            o_h = (gsig[3 * h:3 * h + 1, :] * o_cmp[:, sl]
                   + gsig[3 * h + 1:3 * h + 2, :] * o_sel[:, sl]
                   + gsig[3 * h + 2:3 * h + 3, :] * o_win[:, sl])
            o_scr[:, h * HEAD_DIM:(h + 1) * HEAD_DIM] = o_h.T

    o = o_scr[...]
    ms = jnp.mean(o * o, axis=-1, keepdims=True)
    o_ref[...] = (o * lax.rsqrt(ms + EPS) * gn_ref[...]).astype(BF16)


def _nsa(qnt, qrt, cmp, kk, vt, kblk, gt, gn):
    batch, _, _, seq = qnt.shape
    assert TQ == TK and WINDOW % TK == 0 and TQ & (TQ - 1) == 0
    nq = seq // TQ
    width = N_HEADS * HEAD_DIM
    return pl.pallas_call(
        _nsa_kernel,
        grid=(batch, nq),
        in_specs=[
            pl.BlockSpec((1, N_HEADS, HEAD_DIM, TQ), lambda b, i: (b, 0, 0, i)),
            pl.BlockSpec((1, N_HEADS, HEAD_DIM, TQ), lambda b, i: (b, 0, 0, i)),
            pl.BlockSpec((1,) + cmp.shape[1:], lambda b, i: (b, 0, 0, 0)),
            pl.BlockSpec((1,) + kk.shape[1:], lambda b, i: (b, 0, 0, 0)),
            pl.BlockSpec((1,) + vt.shape[1:], lambda b, i: (b, 0, 0, 0, 0)),
            pl.BlockSpec((seq, LANES), lambda b, i: (0, 0)),
            pl.BlockSpec((gt.shape[0], TQ), lambda b, i: (0, b * nq + i)),
            pl.BlockSpec((1, width), lambda b, i: (0, 0)),
        ],
        out_specs=pl.BlockSpec((TQ, width), lambda b, i: (b * nq + i, 0)),
        out_shape=jax.ShapeDtypeStruct((batch * seq, width), BF16),
        scratch_shapes=[
            pltpu.VMEM((2 * N_KV, 1, GQA * TQ), F32),
            pltpu.VMEM((2 * N_KV, vt.shape[3], GQA * TQ), F32),
            pltpu.VMEM((N_KV, 2 * HEAD_DIM, GQA * TQ), BF16),
            pltpu.VMEM((N_KV, HEAD_DIM, GQA * TQ), F32),
            pltpu.VMEM((2, TK, GQA * TQ), F32),
            pltpu.VMEM((TQ, width), F32),
        ],
        compiler_params=_cparams(2),
        name="nsa",
    )(qnt, qrt, cmp, kk, vt, kblk, gt, gn)


def _pool_group(cur, halo_rows, t1, window, w_mat, bias, scale):
    halo = halo_rows.shape[0]
    acc = jnp.concatenate([halo_rows, cur], axis=0)
    span = 1
    while span < window:
        acc = acc + jnp.concatenate([jnp.zeros((span, acc.shape[1]), F32), acc[:-span]], axis=0)
        span *= 2
    mean = acc[halo:] / jnp.minimum(t1, float(window))
    return (_dot((mean - cur).astype(BF16), w_mat) + bias) * scale


def _pool_kernel(u_ref, w_ref, b_ref, sc_ref, gn_ref, o_ref, y_scr, halo_scr):
    si = pl.program_id(1)
    tm = u_ref.shape[1]
    halo = halo_scr.shape[0]

    @pl.when(si == 0)
    def _():
        halo_scr[...] = jnp.zeros(halo_scr.shape, F32)

    t1 = (si * tm + 1 + lax.broadcasted_iota(I32, (tm, 1), 0)).astype(F32)
    cg = w_ref.shape[1]
    for gi, window in enumerate(POOL_SIZES):
        sl = slice(gi * cg, (gi + 1) * cg)
        cur = u_ref[0, :, sl]
        y_scr[:, sl] = _pool_group(cur, halo_scr[:, sl], t1, window, w_ref[gi], b_ref[:, sl], sc_ref[:, sl])
        halo_scr[:, sl] = cur[tm - halo:, :]
    y = y_scr[...]
    ms = jnp.mean(y * y, axis=-1, keepdims=True)
    o_ref[...] = (y * lax.rsqrt(ms + EPS) * gn_ref[...]).astype(BF16)


def _pool(u3, w_pool, b_pool, pool_scale, gn_pool):
    batch, seq, c = u3.shape
    tm = TM_PROJ
    ns = seq // tm
    vec = lambda b, i: (0, 0)
    return pl.pallas_call(
        _pool_kernel,
        grid=(batch, ns),
        in_specs=[
            pl.BlockSpec((1, tm, c), lambda b, i: (b, i, 0)),
            pl.BlockSpec(w_pool.shape, lambda b, i: (0, 0, 0)),
            pl.BlockSpec((1, c), vec),
            pl.BlockSpec((1, c), vec),
            pl.BlockSpec((1, c), vec),
        ],
        out_specs=pl.BlockSpec((tm, c), lambda b, i: (b * ns + i, 0)),
        out_shape=jax.ShapeDtypeStruct((batch * seq, c), BF16),
        scratch_shapes=[pltpu.VMEM((tm, c), F32), pltpu.VMEM((max(POOL_SIZES), c), F32)],
        compiler_params=_cparams(2),
        name="pool",
    )(u3, w_pool, b_pool, pool_scale, gn_pool)


def _outproj_kernel(mn_ref, mp_ref, x_ref, w_hbm, ln_ref, wr_ref, br_ref,
                    x1_ref, h2_ref, eid_ref, gate_ref, x1_scr, ssq_scr, stage, w_ref, wsem):
    tm, d = x_ref.shape
    half = mn_ref.shape[1]
    chunk = 512

    @pl.when(pl.program_id(0) == 0)
    def _():
        x1_scr[...] = jnp.zeros(x1_scr.shape, F32)
        ssq_scr[...] = jnp.zeros(ssq_scr.shape, F32)
        rc = stage.shape[1]
        n_chunks = w_ref.shape[0] // rc

        def chunk_copy(c, slot):
            return pltpu.make_async_copy(w_hbm.at[0, pl.ds(c * rc, rc)], stage.at[slot], wsem.at[slot])

        chunk_copy(0, 0).start()
        for c in range(n_chunks):
            slot = c % 2
            if c + 1 < n_chunks:
                chunk_copy(c + 1, 1 - slot).start()
            chunk_copy(c, slot).wait()
            w_ref[c * rc:(c + 1) * rc, :] = stage[slot].astype(BF16)

    h2 = x1_scr[...] * lax.rsqrt(ssq_scr[...] * (1.0 / d) + EPS) * ln_ref[...]
    _store_token_tiles(h2_ref, _pack_bf16_halves(h2))

    logits = _dot_nt(wr_ref[...], h2.astype(BF16)) + br_ref[:, 0:1]
    ng, ne = N_EXPERT_GROUPS, EXPERTS_PER_GROUP
    row = lax.broadcasted_iota(I32, (ne, tm), 0)
    lg = jnp.where(row < ng, logits[0:ne], NEG)
    mg = jnp.max(lg, axis=0, keepdims=True)
    g_sel = jnp.min(jnp.where(lg == mg, row, ne), axis=0, keepdims=True)
    p_g = 1.0 / jnp.sum(jnp.exp(lg - mg), axis=0, keepdims=True)
    le = jnp.zeros((ne, tm), F32)
    for g in range(ng):
        le = jnp.where(g_sel == g, logits[ne * (g + 1):ne * (g + 2)], le)
    ex = jnp.exp(le - jnp.max(le, axis=0, keepdims=True))
    pe = ex / jnp.sum(ex, axis=0, keepdims=True)
    p1 = jnp.max(pe, axis=0, keepdims=True)
    i1 = jnp.min(jnp.where(pe == p1, row, ne), axis=0, keepdims=True)
    rest = jnp.where(row == i1, -1.0, pe)
    p2 = jnp.max(rest, axis=0, keepdims=True)
    i2 = jnp.min(jnp.where(rest == p2, row, ne), axis=0, keepdims=True)
    psum = p1 + p2
    zero_i = jnp.zeros((ne, tm), I32)
    eid = jnp.where(row == 0, g_sel * ne + i1, jnp.where(row == 1, g_sel * ne + i2, zero_i))
    gate = jnp.where(row == 0, p_g * p1 / psum, jnp.where(row == 1, p_g * p2 / psum, 0.0))
    eid_ref[...] = eid
    gate_ref[...] = gate

    mn = mn_ref[...]
    mp = mp_ref[...]
    ssq = jnp.zeros((tm, 1), F32)
    for c in range(d // chunk):
        sl = slice(c * chunk, (c + 1) * chunk)
        x1 = x_ref[:, sl] + (_dot(mn, w_ref[:half, sl]) + _dot(mp, w_ref[half:, sl]))
        x1_ref[:, sl] = x1
        x1_scr[:, sl] = x1
        ssq = ssq + jnp.sum(x1 * x1, axis=-1, keepdims=True)
    ssq_scr[...] = ssq


def _outproj(mix_nsa, mix_pool, x2d, w_out, ln_moe, w_router_t, b_router):
    n, d = x2d.shape
    tm = TM_PROJ
    half = mix_nsa.shape[1]
    const = lambda i: (0, 0)
    last = n // tm - 1
    cur = lambda i: (jnp.minimum(i, last), 0)
    prev = lambda i: (jnp.maximum(i - 1, 0), 0)
    prev_t = lambda i: (0, jnp.maximum(i - 1, 0))
    return pl.pallas_call(
        _outproj_kernel,
        grid=(n // tm + 1,),
        in_specs=[
            pl.BlockSpec((tm, half), cur),
            pl.BlockSpec((tm, half), cur),
            pl.BlockSpec((tm, d), cur),
            pl.BlockSpec(memory_space=pl.ANY),
            pl.BlockSpec((1, d), const),
            pl.BlockSpec(w_router_t.shape, const),
            pl.BlockSpec(b_router.shape, const),
        ],
        out_specs=[
            pl.BlockSpec((tm, d), cur),
            pl.BlockSpec((tm * SUBLANES, LANES), prev),
            pl.BlockSpec((SUBLANES, tm), prev_t),
            pl.BlockSpec((SUBLANES, tm), prev_t),
        ],
        scratch_shapes=[pltpu.VMEM((tm, d), F32), pltpu.VMEM((tm, 1), F32),
                        pltpu.VMEM((2, W_STAGE_ROWS, d), F32), pltpu.VMEM(w_out.shape[1:], BF16),
                        pltpu.SemaphoreType.DMA((2,))],
        out_shape=[
            jax.ShapeDtypeStruct((n, d), F32),
            jax.ShapeDtypeStruct((n * SUBLANES, LANES), U32),
            jax.ShapeDtypeStruct((SUBLANES, n), I32),
            jax.ShapeDtypeStruct((SUBLANES, n), F32),
        ],
        compiler_params=_cparams(1),
        name="outproj",
    )(mix_nsa, mix_pool, x2d, w_out, ln_moe, w_router_t, b_router)


def _dispatch_kernel(eid_ref, dest_ref, meta_ref, rank_scr):
    n = eid_ref.shape[1]
    t = T_RANK
    row = lax.broadcasted_iota(I32, (N_EXPERTS, t), 0)
    before = (lax.broadcasted_iota(I32, (t, t), 0) < lax.broadcasted_iota(I32, (t, t), 1)).astype(BF16)
    carry = jnp.zeros((N_EXPERTS, 1), F32)
    for k in range(2):
        for j in range(n // t):
            sl = slice(j * t, (j + 1) * t)
            oh = row == eid_ref[k:k + 1, sl]
            ohf = oh.astype(F32)
            prior = _dot(ohf.astype(BF16), before) + carry
            rank_scr[k:k + 1, sl] = jnp.sum(jnp.where(oh, prior, 0.0), axis=0, keepdims=True)
            carry = carry + jnp.sum(ohf, axis=1, keepdims=True)
    counts = jnp.broadcast_to(carry, (N_EXPERTS, LANES))
    padded = jnp.floor((counts + (TM_MOE - 1.0)) * (1.0 / TM_MOE)) * TM_MOE
    ends = padded
    r_idx = lax.broadcasted_iota(I32, (N_EXPERTS, LANES), 0)
    sh = 1
    while sh < N_EXPERTS:
        ends = ends + jnp.where(r_idx >= sh, pltpu.roll(ends, sh, 0), 0.0)
        sh *= 2
    starts = (ends - padded)[:, 0:1]
    for k in range(2):
        for j in range(n // t):
            sl = slice(j * t, (j + 1) * t)
            oh = row == eid_ref[k:k + 1, sl]
            base = jnp.sum(jnp.where(oh, starts, 0.0), axis=0, keepdims=True)
            dest_ref[k:k + 1, sl] = (rank_scr[k:k + 1, sl] + base).astype(I32)
    for k in range(2, SUBLANES):
        dest_ref[k:k + 1, :] = jnp.zeros((1, n), I32)
    blk_start = lax.broadcasted_iota(I32, (N_EXPERTS, LANES), 1).astype(F32) * TM_MOE
    blk_expert = jnp.sum((ends <= blk_start).astype(F32), axis=0, keepdims=True)
    blk_expert = jnp.minimum(blk_expert, N_EXPERTS - 1.0)
    n_used = ends[N_EXPERTS - 1:N_EXPERTS, :] * (1.0 / TM_MOE)
    on_diag = r_idx == lax.broadcasted_iota(I32, (N_EXPERTS, LANES), 1)
    end_blk = jnp.sum(jnp.where(on_diag, ends, 0.0), axis=0, keepdims=True) * (1.0 / TM_MOE)
    n_blk = jnp.sum(jnp.where(on_diag, padded, 0.0), axis=0, keepdims=True) * (1.0 / TM_MOE)
    r8 = lax.broadcasted_iota(I32, (SUBLANES, LANES), 0)
    meta = jnp.where(r8 == 0, blk_expert,
                     jnp.where(r8 == 1, n_used, jnp.where(r8 == 2, end_blk, jnp.where(r8 == 3, n_blk, 0.0))))
    meta_ref[...] = meta.astype(I32)


def _dispatch(eid):
    n = eid.shape[1]
    return pl.pallas_call(
        _dispatch_kernel,
        out_shape=[jax.ShapeDtypeStruct((SUBLANES, n), I32),
                   jax.ShapeDtypeStruct((SUBLANES, LANES), I32)],
        scratch_shapes=[pltpu.VMEM((SUBLANES, n), F32)],
        compiler_params=pltpu.CompilerParams(vmem_limit_bytes=VMEM_LIMIT_BYTES),
        name="dispatch",
    )(eid)


def _row_copy(src_ref, src_row, dst_ref, dst_row, sem):
    tile = lambda ref, row: ref.at[pl.ds(pl.multiple_of(row * SUBLANES, SUBLANES), SUBLANES)]
    return pltpu.make_async_copy(tile(src_ref, src_row), tile(dst_ref, dst_row), sem)


def _scatter_kernel(dest_ref, meta_ref, h_ref, xs_ref, zero_buf, sem, zsem):
    tm = h_ref.shape[0] // SUBLANES
    blk = zero_buf.shape[0]

    @pl.when(pl.program_id(0) == 0)
    def _():
        zero_buf[...] = jnp.zeros(zero_buf.shape, zero_buf.dtype)
        n_used = meta_ref[1, 0]

        def zero_block(b):
            return pltpu.make_async_copy(zero_buf, xs_ref.at[pl.ds(pl.multiple_of(b * blk, blk), blk)], zsem)

        def sweep(act):
            def per_expert(e, c):
                @pl.when(meta_ref[3, e] > 0)
                def _():
                    act(zero_block(meta_ref[2, e] - 1))
                return c

            def per_tail(b, c):
                @pl.when(b >= n_used)
                def _():
                    act(zero_block(b))
                return c

            lax.fori_loop(0, N_EXPERTS, per_expert, 0)
            lax.fori_loop(0, xs_ref.shape[0] // blk, per_tail, 0)

        sweep(lambda cp: cp.start())
        sweep(lambda cp: cp.wait())

    def start(r, c):
        for k in range(2):
            _row_copy(h_ref, r, xs_ref, dest_ref[0, k, r], sem).start(priority=k)
        return c

    lax.fori_loop(0, tm, start, 0, unroll=DMA_ISSUE_UNROLL)
    for k in range(2):
        pltpu.make_async_copy(h_ref, xs_ref.at[pl.ds(0, tm * SUBLANES)], sem).wait()


def _scatter(dest3, meta, h2, cap):
    nsteps, _, tm = dest3.shape
    return pl.pallas_call(
        _scatter_kernel,
        grid=(nsteps,),
        in_specs=[
            pl.BlockSpec((1, 2, tm), lambda i: (i, 0, 0), memory_space=pltpu.SMEM),
            pl.BlockSpec(memory_space=pltpu.SMEM),
            pl.BlockSpec((tm * SUBLANES, LANES), lambda i: (i, 0)),
        ],
        out_specs=pl.BlockSpec(memory_space=pl.ANY),
        out_shape=jax.ShapeDtypeStruct((cap * SUBLANES, LANES), h2.dtype),
        scratch_shapes=[pltpu.VMEM((TM_MOE * SUBLANES, LANES), h2.dtype), pltpu.SemaphoreType.DMA(()),
                        pltpu.SemaphoreType.DMA(())],
        compiler_params=_cparams(1),
        name="scatter",
    )(dest3, meta, h2)


def _experts_kernel(meta_ref, xs_ref, wg_hbm, wu_hbm, wd_hbm, ys_ref,
                    wg_buf, wu_buf, wd_buf, wg16, wu16, wd16, elist_ref, ord_ref, sem):
    c = pl.program_id(0)
    n_used = meta_ref[1, 0]
    e = meta_ref[0, c]
    first = (c == 0) | (e != meta_ref[0, jnp.maximum(c - 1, 0)])

    def fetch(expert, slot):
        return [pltpu.make_async_copy(w.at[expert], buf.at[slot], sem.at[slot, i])
                for i, (w, buf) in enumerate(((wg_hbm, wg_buf), (wu_hbm, wu_buf), (wd_hbm, wd_buf)))]

    @pl.when(c == 0)
    def _():
        def scan(b, cnt):
            eb = meta_ref[0, b]
            take = ((b == 0) | (eb != meta_ref[0, jnp.maximum(b - 1, 0)])) & (b < n_used)

            @pl.when(take)
            def _():
                elist_ref[cnt] = eb

            return cnt + take.astype(I32)

        cnt = lax.fori_loop(0, pl.num_programs(0), scan, 0)
        ord_ref[0] = 0
        ord_ref[1] = cnt
        for k in range(W_SLOTS - 1):
            @pl.when(k < cnt)
            def _(k=k):
                for cp, prio in zip(fetch(elist_ref[k], k), WEIGHT_DMA_PRIORITIES):
                    cp.start(priority=prio)

    @pl.when(first & (c < n_used))
    def _():
        @pl.when(c > 0)
        def _():
            ord_ref[0] = ord_ref[0] + 1
        k = ord_ref[0]
        for cp in fetch(e, k % W_SLOTS):
            cp.wait()
        ahead = k + (W_SLOTS - 1)

        @pl.when(ahead < ord_ref[1])
        def _():
            for cp, prio in zip(fetch(elist_ref[jnp.minimum(ahead, N_EXPERTS - 1)], ahead % W_SLOTS),
                                WEIGHT_DMA_PRIORITIES):
                cp.start(priority=prio)

    def mlp(weights):
        lo, hi = _unpack_bf16_halves(_load_token_tiles(xs_ref))
        lo, hi = lo.astype(BF16), hi.astype(BF16)
        half = lo.shape[1]
        wg, wu, wd = weights

        def proj(w):
            return _dot(lo, w[:half, :]) + _dot(hi, w[half:, :])

        act = (jax.nn.silu(proj(wg)) * proj(wu)).astype(BF16)
        _store_token_tiles(ys_ref, _pack_bf16_halves(_dot(act, wd)))

    @pl.when(first & (c < n_used))
    def _():
        slot = ord_ref[0] % W_SLOTS
        cast = [buf[slot].astype(BF16) for buf in (wg_buf, wu_buf, wd_buf)]
        for ref16, w in zip((wg16, wu16, wd16), cast):
            ref16[...] = w
        mlp(cast)

    @pl.when(jnp.logical_not(first) & (c < n_used))
    def _():
        mlp((wg16[...], wu16[...], wd16[...]))

    @pl.when(c >= n_used)
    def _():
        ys_ref[...] = jnp.zeros(ys_ref.shape, U32)


def _experts(meta, xs, w_gate, w_up, w_down):
    rows = TM_MOE * SUBLANES
    nblk = xs.shape[0] // rows
    d, ff = w_gate.shape[1:]

    def row_map(c, meta):
        return (jnp.minimum(c, meta[1, 0] - 1), 0)

    hbm = pl.BlockSpec(memory_space=pl.ANY)
    return pl.pallas_call(
        _experts_kernel,
        grid_spec=pltpu.PrefetchScalarGridSpec(
            num_scalar_prefetch=1,
            grid=(nblk,),
            in_specs=[pl.BlockSpec((rows, LANES), row_map), hbm, hbm, hbm],
            out_specs=pl.BlockSpec((rows, LANES), lambda c, meta: (c, 0)),
            scratch_shapes=[
                pltpu.VMEM((W_SLOTS, d, ff), F32), pltpu.VMEM((W_SLOTS, d, ff), F32),
                pltpu.VMEM((W_SLOTS, ff, d), F32),
                pltpu.VMEM((d, ff), BF16), pltpu.VMEM((d, ff), BF16), pltpu.VMEM((ff, d), BF16),
                pltpu.SMEM((N_EXPERTS,), I32),
                pltpu.SMEM((2,), I32),
                pltpu.SemaphoreType.DMA((W_SLOTS, 3)),
            ],
        ),
        out_shape=jax.ShapeDtypeStruct(xs.shape, U32),
        compiler_params=_cparams(1),
        name="experts",
    )(meta, xs, w_gate, w_up, w_down)


def _combine_kernel(dest_ref, dest_next_ref, x1_ref, gate_ref, ln_ref, ys_ref, o_ref, ybuf, sem):
    i = pl.program_id(0)
    tm = x1_ref.shape[0]
    slot = i % 2

    def issue(d_ref, s):
        def start(r, c):
            for k in range(2):
                _row_copy(ys_ref, d_ref[0, k, r], ybuf.at[s, k], r, sem.at[s]).start(priority=k)
            return c

        lax.fori_loop(0, tm, start, 0, unroll=DMA_ISSUE_UNROLL)

    @pl.when(i == 0)
    def _():
        issue(dest_ref, 0)

    @pl.when(i + 1 < pl.num_programs(0))
    def _():
        issue(dest_next_ref, 1 - slot)

    for k in range(2):
        pltpu.make_async_copy(ys_ref.at[pl.ds(0, tm * SUBLANES)], ybuf.at[slot, k], sem.at[slot]).wait()
    gate = gate_ref[...]
    lo0, hi0 = _unpack_bf16_halves(_load_token_tiles(ybuf.at[slot, 0]))
    lo1, hi1 = _unpack_bf16_halves(_load_token_tiles(ybuf.at[slot, 1]))
    g0, g1 = gate[:, 0:1], gate[:, 1:2]
    moe = jnp.concatenate([lo0 * g0 + lo1 * g1, hi0 * g0 + hi1 * g1], axis=1)
    x2 = x1_ref[...] + moe
    ms = jnp.mean(x2 * x2, axis=-1, keepdims=True)
    o_ref[...] = x2 * lax.rsqrt(ms + EPS) * ln_ref[...]


def _combine(dest3, x1, gate_rows, ln_final, ys):
    n, d = x1.shape
    tm = TM_COMBINE
    last = n // tm - 1
    return pl.pallas_call(
        _combine_kernel,
        grid=(n // tm,),
        in_specs=[
            pl.BlockSpec((1, 2, tm), lambda i: (i, 0, 0), memory_space=pltpu.SMEM),
            pl.BlockSpec((1, 2, tm), lambda i: (jnp.minimum(i + 1, last), 0, 0), memory_space=pltpu.SMEM),
            pl.BlockSpec((tm, d), lambda i: (i, 0)),
            pl.BlockSpec((tm, LANES), lambda i: (i, 0)),
            pl.BlockSpec((1, d), lambda i: (0, 0)),
            pl.BlockSpec(memory_space=pl.ANY),
        ],
        out_specs=pl.BlockSpec((tm, d), lambda i: (i, 0)),
        out_shape=jax.ShapeDtypeStruct((n, d), F32),
        scratch_shapes=[pltpu.VMEM((2, 2, tm * SUBLANES, LANES), U32), pltpu.SemaphoreType.DMA((2,))],
        compiler_params=_cparams(1),
        name="combine",
    )(dest3, dest3, x1, gate_rows, ln_final, ys)


def kernel(x, positions, ln_mix, w_in, pe_cmp_k, w_cmp_k1, w_cmp_k2, pe_cmp_v, w_cmp_v1, w_cmp_v2, w_pool, b_pool, pool_scale, gn_nsa, gn_pool, w_out, ln_moe, w_router_group, b_router_group, w_router_expert, b_router_expert, w_gate, w_up, w_down, ln_final):
    batch, seq, d = x.shape
    n = batch * seq
    depth = w_in.shape[0]

    inv_freq = ROPE_THETA ** (-jnp.arange(0, ROT_DIM, 2, dtype=F32) / ROT_DIM)
    ang_t = inv_freq[:, None] * positions.astype(F32).reshape(1, n)
    kblk = (jnp.arange(seq, dtype=I32)[:, None] // SEL_LEN == jnp.arange(LANES, dtype=I32)[None, :]).astype(BF16)

    x2d = x.reshape(n, d)
    assert depth == 1, "single-layer operation"
    for l in range(depth):
        gate_rows_pad = -(-N_GATES // SUBLANES) * SUBLANES
        qnt, qrt, kvc, kk, vt, gt, u = _inproj(x2d, ln_mix[l][None], ang_t, jnp.swapaxes(w_in, 1, 2),
                                               gate_rows_pad, batch, seq)

        half = CMP_LEN * HEAD_DIM // 2
        def stack_w1(w):
            return jnp.concatenate([w[:half], w[half:]], axis=1)
        w1 = jnp.stack([stack_w1(w_cmp_k1[l]), stack_w1(w_cmp_v1[l])]).astype(BF16)
        w2 = jnp.stack([w_cmp_k2[l], w_cmp_v2[l]]).astype(BF16)
        def pe_rows(pe):
            return jnp.pad(pe.reshape(2, half), ((0, SUBLANES - 2), (0, 0)))
        pe2 = jnp.stack([pe_rows(pe_cmp_k[l]), pe_rows(pe_cmp_v[l])]).astype(BF16)
        cmp = _compress(kvc, pe2, w1, w2)

        mix_nsa = _nsa(qnt, qrt, cmp, kk, vt, kblk, gt, gn_nsa[l][None])
        mix_pool = _pool(u.reshape(batch, seq, -1), w_pool[l].astype(BF16), b_pool[l][None],
                         pool_scale[l][None], gn_pool[l][None])

        ng, ne = N_EXPERT_GROUPS, EXPERTS_PER_GROUP
        w_r = jnp.concatenate([
            w_router_group[l].T, jnp.zeros((ne - ng, d), F32),
            jnp.transpose(w_router_expert[l], (0, 2, 1)).reshape(ng * ne, d),
            jnp.zeros((LANES - ne - ng * ne, d), F32)], axis=0).astype(BF16)
        b_r = jnp.concatenate([
            b_router_group[l], jnp.zeros((ne - ng,), F32), b_router_expert[l].reshape(-1),
            jnp.zeros((LANES - ne - ng * ne,), F32)])
        b_r = jnp.broadcast_to(b_r[:, None], (LANES, LANES))
        x1, h2, eid, gate = _outproj(mix_nsa, mix_pool, x2d, w_out, ln_moe[l][None], w_r, b_r)

        dest, meta = _dispatch(eid)
        cap = (2 * n // TM_MOE + N_EXPERTS) * TM_MOE
        def per_tile(tm):
            return jnp.transpose(dest[:2].reshape(2, n // tm, tm), (1, 0, 2))
        xs = _scatter(per_tile(TS_SCATTER), meta, h2, cap)
        ys = _experts(meta, xs, w_gate[l], w_up[l], w_down[l])
        dest_tok = per_tile(TM_COMBINE)
        gate_rows = jnp.pad(gate[:2].T, ((0, 0), (0, LANES - 2)))
        x2d = _combine(dest_tok, x1, gate_rows, ln_final[None], ys)
    return x2d.reshape(batch, seq, d)
```

```python
import jax
import jax.numpy as jnp
from jax import lax
from jax.experimental import pallas as pl
from jax.experimental.pallas import tpu as pltpu

F32 = jnp.float32
BF16 = jnp.bfloat16
I32 = jnp.int32

HEAD_DIM = 128
N_HEADS = 8
N_KV = 2
GQA = N_HEADS // N_KV
N_GATES = 3 * N_HEADS
ROT_DIM = HEAD_DIM // 4
ROPE_THETA = 500000.0
CMP_LEN = 32
CMP_STRIDE = 16
CMP_HIDDEN = 2 * HEAD_DIM
SEL_LEN = 64
SEL_TOPK = 16
N_LOCAL = 2
WINDOW = 512
POOL_SIZES = (2, 4, 8, 16)
N_EXPERT_GROUPS = 4
EXPERTS_PER_GROUP = 8
N_EXPERTS = N_EXPERT_GROUPS * EXPERTS_PER_GROUP
EPS = 1e-6
NEG = -1e30
BIG = 1e30
SCALE = HEAD_DIM ** -0.5
LOG2E = 1.4426950408889634
Q_SCALE = SCALE * LOG2E

LANES = 128
SUBLANES = 8
VMEM_LIMIT_BYTES = 56 * 1024 * 1024

TM_PROJ = 512
TQ = 256
TK = 256
SEL_MASK_BIAS = 2.0 ** 40
V_ONES_ROWS = 16
TM_MOE = 256
TS_SCATTER = 1024
TM_COMBINE = 512
WEIGHT_DMA_PRIORITIES = (0, 1, 1)
W_STAGE_ROWS = 256
W_SLOTS = 3
DMA_ISSUE_UNROLL = 8
T_RANK = 512


def _cparams(n_axes):
    return pltpu.CompilerParams(dimension_semantics=("arbitrary",) * n_axes, vmem_limit_bytes=VMEM_LIMIT_BYTES)


def _dot(a, b):
    return jnp.dot(a, b, preferred_element_type=F32)


def _dot_nt(a, b):
    return lax.dot_general(a, b, (((1,), (1,)), ((), ())), preferred_element_type=F32)


U32 = jnp.uint32
_HI16 = 0xFFFF0000


def _pack_bf16_halves(x):
    c = x.shape[1] // 2
    bits = lambda t: lax.bitcast_convert_type(t.astype(BF16).astype(F32), U32)
    return lax.shift_right_logical(bits(x[:, :c]), U32(16)) | (bits(x[:, c:]) & U32(_HI16))


def _store_token_tiles(ref, words):
    m, width = words.shape
    assert width == SUBLANES * LANES
    for j in range(SUBLANES):
        ref[pl.ds(j, m, stride=SUBLANES), :] = words[:, j * LANES:(j + 1) * LANES]


def _load_token_tiles(ref):
    m = ref.shape[0] // SUBLANES
    return jnp.concatenate([ref[pl.ds(j, m, stride=SUBLANES), :] for j in range(SUBLANES)], axis=1)


def _unpack_bf16_halves(w):
    lo = lax.bitcast_convert_type(lax.shift_left(w, U32(16)), F32)
    hi = lax.bitcast_convert_type(w & U32(_HI16), F32)
    return lo, hi


def _rope(t, cos, sin, axis):
    idx = lax.broadcasted_iota(I32, t.shape, axis)
    partner = jnp.where(idx < ROT_DIM // 2,
                        pltpu.roll(t, HEAD_DIM - ROT_DIM // 2, axis),
                        pltpu.roll(t, ROT_DIM // 2, axis))
    return t * cos + partner * sin


def _inproj_kernel(x_ref, ln_ref, ang_ref, wt_hbm,
                   qnt_ref, qrt_ref, kvc_ref, kk_ref, vt_ref, gt_ref, u_ref,
                   kc_scr, stage, wq_ref, wv_ref, wg_ref, wkc_ref, wk_ref, wu_ref, wsem):
    @pl.when(pl.program_id(0) == 0)
    def _():
        rc = stage.shape[1]
        q_w, kv_w = N_HEADS * HEAD_DIM, N_KV * HEAD_DIM
        o_kc, o_ks = q_w, q_w + 2 * kv_w
        o_vs, o_kw, o_vw, o_g = o_ks + kv_w, o_ks + 2 * kv_w, o_ks + 3 * kv_w, o_ks + 4 * kv_w
        o_u = o_g + N_GATES
        pieces = ([(r0, rc, wq_ref, r0) for r0 in range(0, q_w, rc)]
                  + [(o_kc + r0, rc, wkc_ref, r0) for r0 in range(0, 2 * kv_w, rc)]
                  + [(o_ks, kv_w, wk_ref, 0), (o_kw, kv_w, wk_ref, kv_w)]
                  + [(o_vs, kv_w, wv_ref, 0), (o_vw, kv_w, wv_ref, kv_w)]
                  + [(o_g, N_GATES, wg_ref, 0)]
                  + [(o_u + r0, rc, wu_ref, r0) for r0 in range(0, wu_ref.shape[0], rc)])
        assert all(n <= rc and src % SUBLANES == 0 for src, n, _, _ in pieces)
        wg_ref[...] = jnp.zeros(wg_ref.shape, BF16)

        def piece_copy(i, slot):
            src, rows, _, _ = pieces[i]
            return pltpu.make_async_copy(wt_hbm.at[0, pl.ds(src, rows)], stage.at[slot, pl.ds(0, rows)],
                                         wsem.at[slot])

        piece_copy(0, 0).start()
        for i, (src, rows, dst, dst_row) in enumerate(pieces):
            slot = i % 2
            if i + 1 < len(pieces):
                piece_copy(i + 1, 1 - slot).start()
            piece_copy(i, slot).wait()
            dst[dst_row:dst_row + rows, :] = stage[slot, :rows, :].astype(BF16)

    x = x_ref[...]
    ms = jnp.mean(x * x, axis=-1, keepdims=True)
    hb = (x * lax.rsqrt(ms + EPS) * ln_ref[...]).astype(BF16)
    chunk = 4 * HEAD_DIM

    ang = ang_ref[...]
    ca, sa = jnp.cos(ang), jnp.sin(ang)
    rest = (HEAD_DIM - ROT_DIM, ang.shape[1])
    cost = jnp.concatenate([ca, ca, jnp.ones(rest, F32)], axis=0)
    sint = jnp.concatenate([-sa, sa, jnp.zeros(rest, F32)], axis=0)
    cos, sin = cost.T, sint.T
    for ch in range(2):
        rt = _dot_nt(wq_ref[ch * chunk:(ch + 1) * chunk, :], hb)
        for j in range(4):
            t = rt[j * HEAD_DIM:(j + 1) * HEAD_DIM, :]
            qnt_ref[0, ch * 4 + j] = (t * Q_SCALE).astype(BF16)
            qrt_ref[0, ch * 4 + j] = (_rope(t, cost, sint, 0) * Q_SCALE).astype(BF16)
    rt = _dot_nt(wv_ref[...], hb)
    ones = jnp.ones((V_ONES_ROWS, TK), BF16)
    for j in range(4):
        for s in range(vt_ref.shape[2]):
            vt_ref[0, j, s] = jnp.concatenate(
                [rt[j * HEAD_DIM:(j + 1) * HEAD_DIM, s * TK:(s + 1) * TK].astype(BF16), ones], axis=0)
    gt_ref[...] = _dot_nt(wg_ref[...], hb)

    r = _dot_nt(hb, wkc_ref[...])
    groups = kvc_ref.shape[2]
    for j in range(4):
        kc_scr[j] = r[:, j * HEAD_DIM:(j + 1) * HEAD_DIM]
        for i in range(CMP_STRIDE):
            kvc_ref[0, j, :, i * HEAD_DIM:(i + 1) * HEAD_DIM] = (
                kc_scr[j, pl.ds(i, groups, stride=CMP_STRIDE), :].astype(BF16))
    r = _dot_nt(hb, wk_ref[...])
    for j in range(4):
        kk_ref[0, j] = _rope(r[:, j * HEAD_DIM:(j + 1) * HEAD_DIM], cos, sin, 1).astype(BF16)
    for ch in range(u_ref.shape[1] // chunk):
        u_ref[:, ch * chunk:(ch + 1) * chunk] = _dot_nt(hb, wu_ref[ch * chunk:(ch + 1) * chunk, :])


def _inproj(x2d, ln, ang_t, w_in_t, gate_rows, batch, seq):
    n, d = x2d.shape
    tm = TM_PROJ
    assert tm % TK == 0
    spb = seq // tm
    const = lambda i: (0, 0)
    q_w, kv_w = N_HEADS * HEAD_DIM, N_KV * HEAD_DIM
    pool_w = w_in_t.shape[1] - (q_w + 6 * kv_w + N_GATES)
    row_heads = lambda i: (i // spb, 0, i % spb, 0)
    col_heads = lambda i: (i // spb, 0, 0, i % spb)
    return pl.pallas_call(
        _inproj_kernel,
        grid=(n // tm,),
        in_specs=[
            pl.BlockSpec((tm, d), lambda i: (i, 0)),
            pl.BlockSpec((1, d), const),
            pl.BlockSpec((ang_t.shape[0], tm), lambda i: (0, i)),
            pl.BlockSpec(memory_space=pl.ANY),
        ],
        out_specs=[
            pl.BlockSpec((1, N_HEADS, HEAD_DIM, tm), col_heads),
            pl.BlockSpec((1, N_HEADS, HEAD_DIM, tm), col_heads),
            pl.BlockSpec((1, 4, tm // CMP_STRIDE, CMP_STRIDE * HEAD_DIM), row_heads),
            pl.BlockSpec((1, 4, tm, HEAD_DIM), row_heads),
            pl.BlockSpec((1, 4, tm // TK, HEAD_DIM + V_ONES_ROWS, TK), lambda i: (i // spb, 0, i % spb, 0, 0)),
            pl.BlockSpec((gate_rows, tm), lambda i: (0, i)),
            pl.BlockSpec((tm, pool_w), lambda i: (i, 0)),
        ],
        out_shape=[
            jax.ShapeDtypeStruct((batch, N_HEADS, HEAD_DIM, seq), BF16),
            jax.ShapeDtypeStruct((batch, N_HEADS, HEAD_DIM, seq), BF16),
            jax.ShapeDtypeStruct((batch, 4, seq // CMP_STRIDE, CMP_STRIDE * HEAD_DIM), BF16),
            jax.ShapeDtypeStruct((batch, 4, seq, HEAD_DIM), BF16),
            jax.ShapeDtypeStruct((batch, 4, seq // TK, HEAD_DIM + V_ONES_ROWS, TK), BF16),
            jax.ShapeDtypeStruct((gate_rows, n), F32),
            jax.ShapeDtypeStruct((n, pool_w), F32),
        ],
        scratch_shapes=[
            pltpu.VMEM((4, tm, HEAD_DIM), F32),
            pltpu.VMEM((2, W_STAGE_ROWS, d), F32),
            pltpu.VMEM((q_w, d), BF16), pltpu.VMEM((2 * kv_w, d), BF16), pltpu.VMEM((gate_rows, d), BF16),
            pltpu.VMEM((2 * kv_w, d), BF16), pltpu.VMEM((2 * kv_w, d), BF16), pltpu.VMEM((pool_w, d), BF16),
            pltpu.SemaphoreType.DMA((2,)),
        ],
        compiler_params=_cparams(1),
        name="inproj",
    )(x2d, ln, ang_t, w_in_t)


def _compress_kernel(t_ref, pe_ref, w1_ref, w2_ref, o_ref):
    t2 = t_ref[0, 0]
    w1 = w1_ref[0]
    a = _dot(t2, w1)
    pb = _dot(pe_ref[0], w1)
    bias = pb[0:1, :CMP_HIDDEN] + pb[1:2, CMP_HIDDEN:]
    nrow = a.shape[0]
    nxt = pltpu.roll(a[:, CMP_HIDDEN:], nrow - 1, 0)
    hid = jax.nn.gelu(a[:, :CMP_HIDDEN] + nxt + bias)
    out = _dot(hid.astype(BF16), w2_ref[0])
    row = lax.broadcasted_iota(I32, out.shape, 0)
    out = jnp.where(row < nrow - 1, out, 0.0)
    is_value = pl.program_id(1) >= N_KV
    o_ref[0, 0] = jnp.where(is_value, out.T, out).astype(BF16)


def _compress(kvc2, pe2, w1, w2):
    batch, four, nrow, width = kvc2.shape
    return pl.pallas_call(
        _compress_kernel,
        grid=(batch, four),
        in_specs=[
            pl.BlockSpec((1, 1, nrow, width), lambda b, j: (b, j, 0, 0)),
            pl.BlockSpec((1, SUBLANES, width), lambda b, j: (j // N_KV, 0, 0)),
            pl.BlockSpec((1, width, 2 * CMP_HIDDEN), lambda b, j: (j // N_KV, 0, 0)),
            pl.BlockSpec((1, CMP_HIDDEN, HEAD_DIM), lambda b, j: (j // N_KV, 0, 0)),
        ],
        out_specs=pl.BlockSpec((1, 1, nrow, HEAD_DIM), lambda b, j: (b, j, 0, 0)),
        out_shape=jax.ShapeDtypeStruct((batch, four, nrow, HEAD_DIM), BF16),
        compiler_params=_cparams(2),
        name="compress",
    )(kvc2, pe2, w1, w2)


def _nsa_kernel(qnt_ref, qrt_ref, cmp_ref, kk_ref, vt_ref, kblk_ref, gt_ref, gn_ref, o_ref,
                m_scr, acc_scr, qa_scr, ocmp_scr, s_scr, o_scr):
    qi = pl.program_id(1)
    tq = qnt_ref.shape[3]
    rows = GQA * tq
    n_cmp_pad = cmp_ref.shape[2]
    n_sel = kk_ref.shape[2] // SEL_LEN
    q0 = qi * tq

    m_scr[...] = jnp.full(m_scr.shape, NEG, F32)
    acc_scr[...] = jnp.zeros(acc_scr.shape, F32)

    def scores(item, slot):
        c, g, q_rows, k_tile, v_t, bias = item
        s_scr[slot] = _dot(k_tile, qa_scr[g, :q_rows, :])

    def softmax_pv(item, slot):
        c, g, q_rows, k_tile, v_t, bias = item
        s = s_scr[slot]
        if bias is not None:
            s = s + bias
        m_old = m_scr[c]
        m_new = jnp.maximum(m_old, jnp.max(s, axis=0, keepdims=True))
        alpha = jnp.exp2(m_old - m_new)
        p = jnp.exp2(s - m_new)
        acc_scr[c] = alpha * acc_scr[c] + _dot(v_t, p.astype(BF16))
        m_scr[c] = m_new

    def flash_out(c):
        return acc_scr[c, :HEAD_DIM, :] / acc_scr[c, HEAD_DIM:HEAD_DIM + 1, :]

    def tile_heads(a):
        return jnp.concatenate([a] * GQA, axis=1)

    kk_i = lax.broadcasted_iota(I32, (TK, tq), 0)
    qq_i = lax.broadcasted_iota(I32, (TK, tq), 1)
    causal_bias = tile_heads(jnp.where(kk_i <= qq_i, 0.0, NEG))
    band_bias = tile_heads(jnp.where(kk_i > qq_i, 0.0, NEG))
    n_win = WINDOW // TK

    for g in range(N_KV):
        qn4 = jnp.concatenate([qnt_ref[0, h] for h in range(g * GQA, (g + 1) * GQA)], axis=1)
        s = _dot(cmp_ref[0, g], qn4)
        n_idx = lax.broadcasted_iota(I32, (n_cmp_pad, rows), 0)
        t_idx = q0 + jnp.bitwise_and(lax.broadcasted_iota(I32, (n_cmp_pad, rows), 1), tq - 1)
        cmp_ok = n_idx * CMP_STRIDE + (CMP_LEN - 1) <= t_idx
        s = jnp.where(cmp_ok, s, NEG)
        m = jnp.max(s, axis=0, keepdims=True)
        e = jnp.where(cmp_ok, jnp.exp2(s - m), 0.0)
        l = jnp.sum(e, axis=0, keepdims=True)
        p = e * jnp.where(l > 0.0, 1.0 / l, 0.0)
        pb = p.astype(BF16)
        ocmp_scr[g] = _dot(cmp_ref[0, N_KV + g], pb)

        jn = lax.broadcasted_iota(I32, (n_sel, n_cmp_pad), 0) * SEL_LEN
        cn = lax.broadcasted_iota(I32, (n_sel, n_cmp_pad), 1) * CMP_STRIDE
        ov = jnp.clip(jnp.minimum(cn + CMP_LEN, jn + SEL_LEN) - jnp.maximum(cn, jn), 0, None)
        w_t = (ov.astype(F32) * (1.0 / CMP_LEN)).astype(BF16)
        imp4 = _dot(w_t, pb)
        imp = imp4[:, 0:tq]
        for r in range(1, GQA):
            imp = imp + imp4[:, r * tq:(r + 1) * tq]
        j_blk = lax.broadcasted_iota(I32, (n_sel, tq), 0)
        t_q = q0 + lax.broadcasted_iota(I32, (n_sel, tq), 1)
        sel_ok = j_blk * SEL_LEN <= t_q
        back = t_q // SEL_LEN - j_blk
        forced = (j_blk == 0) | ((back >= 0) & (back < N_LOCAL))
        val = jnp.where(sel_ok & forced, BIG, jnp.where(sel_ok, imp, -BIG))
        rank = jnp.zeros((n_sel, tq), F32)
        for i in range(n_sel):
            vi = val[i:i + 1, :]
            beats = (vi > val) | ((vi == val) & (j_blk > i))
            rank = rank + beats.astype(F32)
        keep = (rank < float(min(SEL_TOPK, n_sel))) & sel_ok
        bias_t = jnp.where(keep, 0.0, -SEL_MASK_BIAS)
        bias_t = jnp.concatenate([bias_t, jnp.zeros((HEAD_DIM - n_sel, tq), F32)], axis=0)
        qa_scr[g, :HEAD_DIM, :] = jnp.concatenate(
            [qrt_ref[0, h] for h in range(g * GQA, (g + 1) * GQA)], axis=1)
        qa_scr[g, HEAD_DIM:, :] = tile_heads(bias_t.astype(BF16))

    def sel_item(g, kj, bias):
        st = pl.multiple_of(kj * TK, TK)
        k_aug = jnp.concatenate([kk_ref[0, g, pl.ds(st, TK), :], kblk_ref[pl.ds(st, TK), :]], axis=1)
        return (g, g, 2 * HEAD_DIM, k_aug, vt_ref[0, g, kj], bias)

    def win_item(g, back):
        kj = jnp.maximum(qi - back, 0)
        st = pl.multiple_of(kj * TK, TK)
        off = jnp.where(qi >= back, 0.0, NEG)
        bias = causal_bias if back == 0 else (band_bias + off if back == n_win else off)
        return (N_KV + g, g, HEAD_DIM, kk_ref[0, N_KV + g, pl.ds(st, TK), :], vt_ref[0, N_KV + g, kj], bias)

    assert N_KV == 2
    scores(sel_item(0, 0, None), 0)

    def sel_tile(kj):
        scores(sel_item(1, kj, None), 1)
        softmax_pv(sel_item(0, kj, None), 0)
        scores(sel_item(0, kj + 1, None), 0)
        softmax_pv(sel_item(1, kj, None), 1)

    def sel_pair(j, carry):
        sel_tile(2 * j)
        sel_tile(2 * j + 1)
        return carry

    lax.fori_loop(0, qi // 2, sel_pair, 0)

    @pl.when(qi % 2 == 1)
    def _():
        sel_tile(qi - 1)
    tail = ([sel_item(g, qi, causal_bias) for g in range(N_KV)]
            + [win_item(g, back) for back in range(n_win, -1, -1) for g in range(N_KV)])
    for i, item in enumerate(tail):
        if i + 1 < len(tail):
            scores(tail[i + 1], (i + 1) % 2)
        softmax_pv(item, i % 2)

    gsig = jax.nn.sigmoid(gt_ref[...])
    for g in range(N_KV):
        o_cmp, o_sel, o_win = ocmp_scr[g], flash_out(g), flash_out(N_KV + g)
        for r in range(GQA):
            h = g * GQA + r
            sl = slice(r * tq, (r + 1) * tq)
            o_h = (gsig[3 * h:3 * h + 1, :] * o_cmp[:, sl]
                   + gsig[3 * h + 1:3 * h + 2, :] * o_sel[:, sl]
                   + gsig[3 * h + 2:3 * h + 3, :] * o_win[:, sl])
            o_scr[:, h * HEAD_DIM:(h + 1) * HEAD_DIM] = o_h.T

    o = o_scr[...]
    ms = jnp.mean(o * o, axis=-1, keepdims=True)
    o_ref[...] = (o * lax.rsqrt(ms + EPS) * gn_ref[...]).astype(BF16)


def _nsa(qnt, qrt, cmp, kk, vt, kblk, gt, gn):
    batch, _, _, seq = qnt.shape
    assert TQ == TK and WINDOW % TK == 0 and TQ & (TQ - 1) == 0
    nq = seq // TQ
    width = N_HEADS * HEAD_DIM
    return pl.pallas_call(
        _nsa_kernel,
        grid=(batch, nq),
        in_specs=[
            pl.BlockSpec((1, N_HEADS, HEAD_DIM, TQ), lambda b, i: (b, 0, 0, i)),
            pl.BlockSpec((1, N_HEADS, HEAD_DIM, TQ), lambda b, i: (b, 0, 0, i)),
            pl.BlockSpec((1,) + cmp.shape[1:], lambda b, i: (b, 0, 0, 0)),
            pl.BlockSpec((1,) + kk.shape[1:], lambda b, i: (b, 0, 0, 0)),
            pl.BlockSpec((1,) + vt.shape[1:], lambda b, i: (b, 0, 0, 0, 0)),
            pl.BlockSpec((seq, LANES), lambda b, i: (0, 0)),
            pl.BlockSpec((gt.shape[0], TQ), lambda b, i: (0, b * nq + i)),
            pl.BlockSpec((1, width), lambda b, i: (0, 0)),
        ],
        out_specs=pl.BlockSpec((TQ, width), lambda b, i: (b * nq + i, 0)),
        out_shape=jax.ShapeDtypeStruct((batch * seq, width), BF16),
        scratch_shapes=[
            pltpu.VMEM((2 * N_KV, 1, GQA * TQ), F32),
            pltpu.VMEM((2 * N_KV, vt.shape[3], GQA * TQ), F32),
            pltpu.VMEM((N_KV, 2 * HEAD_DIM, GQA * TQ), BF16),
            pltpu.VMEM((N_KV, HEAD_DIM, GQA * TQ), F32),
            pltpu.VMEM((2, TK, GQA * TQ), F32),
            pltpu.VMEM((TQ, width), F32),
        ],
        compiler_params=_cparams(2),
        name="nsa",
    )(qnt, qrt, cmp, kk, vt, kblk, gt, gn)


def _pool_group(cur, halo_rows, t1, window, w_mat, bias, scale):
    halo = halo_rows.shape[0]
    acc = jnp.concatenate([halo_rows, cur], axis=0)
    span = 1
    while span < window:
        acc = acc + jnp.concatenate([jnp.zeros((span, acc.shape[1]), F32), acc[:-span]], axis=0)
        span *= 2
    mean = acc[halo:] / jnp.minimum(t1, float(window))
    return (_dot((mean - cur).astype(BF16), w_mat) + bias) * scale


def _pool_kernel(u_ref, w_ref, b_ref, sc_ref, gn_ref, o_ref, y_scr, halo_scr):
    si = pl.program_id(1)
    tm = u_ref.shape[1]
    halo = halo_scr.shape[0]

    @pl.when(si == 0)
    def _():
        halo_scr[...] = jnp.zeros(halo_scr.shape, F32)

    t1 = (si * tm + 1 + lax.broadcasted_iota(I32, (tm, 1), 0)).astype(F32)
    cg = w_ref.shape[1]
    for gi, window in enumerate(POOL_SIZES):
        sl = slice(gi * cg, (gi + 1) * cg)
        cur = u_ref[0, :, sl]
        y_scr[:, sl] = _pool_group(cur, halo_scr[:, sl], t1, window, w_ref[gi], b_ref[:, sl], sc_ref[:, sl])
        halo_scr[:, sl] = cur[tm - halo:, :]
    y = y_scr[...]
    ms = jnp.mean(y * y, axis=-1, keepdims=True)
    o_ref[...] = (y * lax.rsqrt(ms + EPS) * gn_ref[...]).astype(BF16)


def _pool(u3, w_pool, b_pool, pool_scale, gn_pool):
    batch, seq, c = u3.shape
    tm = TM_PROJ
    ns = seq // tm
    vec = lambda b, i: (0, 0)
    return pl.pallas_call(
        _pool_kernel,
        grid=(batch, ns),
        in_specs=[
            pl.BlockSpec((1, tm, c), lambda b, i: (b, i, 0)),
            pl.BlockSpec(w_pool.shape, lambda b, i: (0, 0, 0)),
            pl.BlockSpec((1, c), vec),
            pl.BlockSpec((1, c), vec),
            pl.BlockSpec((1, c), vec),
        ],
        out_specs=pl.BlockSpec((tm, c), lambda b, i: (b * ns + i, 0)),
        out_shape=jax.ShapeDtypeStruct((batch * seq, c), BF16),
        scratch_shapes=[pltpu.VMEM((tm, c), F32), pltpu.VMEM((max(POOL_SIZES), c), F32)],
        compiler_params=_cparams(2),
        name="pool",
    )(u3, w_pool, b_pool, pool_scale, gn_pool)


def _outproj_kernel(mn_ref, mp_ref, x_ref, w_hbm, ln_ref, wr_ref, br_ref,
                    x1_ref, h2_ref, eid_ref, gate_ref, x1_scr, ssq_scr, stage, w_ref, wsem):
    tm, d = x_ref.shape
    half = mn_ref.shape[1]
    chunk = 512

    @pl.when(pl.program_id(0) == 0)
    def _():
        x1_scr[...] = jnp.zeros(x1_scr.shape, F32)
        ssq_scr[...] = jnp.zeros(ssq_scr.shape, F32)
        rc = stage.shape[1]
        n_chunks = w_ref.shape[0] // rc

        def chunk_copy(c, slot):
            return pltpu.make_async_copy(w_hbm.at[0, pl.ds(c * rc, rc)], stage.at[slot], wsem.at[slot])

        chunk_copy(0, 0).start()
        for c in range(n_chunks):
            slot = c % 2
            if c + 1 < n_chunks:
                chunk_copy(c + 1, 1 - slot).start()
            chunk_copy(c, slot).wait()
            w_ref[c * rc:(c + 1) * rc, :] = stage[slot].astype(BF16)

    h2 = x1_scr[...] * lax.rsqrt(ssq_scr[...] * (1.0 / d) + EPS) * ln_ref[...]
    _store_token_tiles(h2_ref, _pack_bf16_halves(h2))

    logits = _dot_nt(wr_ref[...], h2.astype(BF16)) + br_ref[:, 0:1]
    ng, ne = N_EXPERT_GROUPS, EXPERTS_PER_GROUP
    row = lax.broadcasted_iota(I32, (ne, tm), 0)
    lg = jnp.where(row < ng, logits[0:ne], NEG)
    mg = jnp.max(lg, axis=0, keepdims=True)
    g_sel = jnp.min(jnp.where(lg == mg, row, ne), axis=0, keepdims=True)
    p_g = 1.0 / jnp.sum(jnp.exp(lg - mg), axis=0, keepdims=True)
    le = jnp.zeros((ne, tm), F32)
    for g in range(ng):
        le = jnp.where(g_sel == g, logits[ne * (g + 1):ne * (g + 2)], le)
    ex = jnp.exp(le - jnp.max(le, axis=0, keepdims=True))
    pe = ex / jnp.sum(ex, axis=0, keepdims=True)
    p1 = jnp.max(pe, axis=0, keepdims=True)
    i1 = jnp.min(jnp.where(pe == p1, row, ne), axis=0, keepdims=True)
    rest = jnp.where(row == i1, -1.0, pe)
    p2 = jnp.max(rest, axis=0, keepdims=True)
    i2 = jnp.min(jnp.where(rest == p2, row, ne), axis=0, keepdims=True)
    psum = p1 + p2
    zero_i = jnp.zeros((ne, tm), I32)
    eid = jnp.where(row == 0, g_sel * ne + i1, jnp.where(row == 1, g_sel * ne + i2, zero_i))
    gate = jnp.where(row == 0, p_g * p1 / psum, jnp.where(row == 1, p_g * p2 / psum, 0.0))
    eid_ref[...] = eid
    gate_ref[...] = gate

    mn = mn_ref[...]
    mp = mp_ref[...]
    ssq = jnp.zeros((tm, 1), F32)
    for c in range(d // chunk):
        sl = slice(c * chunk, (c + 1) * chunk)
        x1 = x_ref[:, sl] + (_dot(mn, w_ref[:half, sl]) + _dot(mp, w_ref[half:, sl]))
        x1_ref[:, sl] = x1
        x1_scr[:, sl] = x1
        ssq = ssq + jnp.sum(x1 * x1, axis=-1, keepdims=True)
    ssq_scr[...] = ssq


def _outproj(mix_nsa, mix_pool, x2d, w_out, ln_moe, w_router_t, b_router):
    n, d = x2d.shape
    tm = TM_PROJ
    half = mix_nsa.shape[1]
    const = lambda i: (0, 0)
    last = n // tm - 1
    cur = lambda i: (jnp.minimum(i, last), 0)
    prev = lambda i: (jnp.maximum(i - 1, 0), 0)
    prev_t = lambda i: (0, jnp.maximum(i - 1, 0))
    return pl.pallas_call(
        _outproj_kernel,
        grid=(n // tm + 1,),
        in_specs=[
            pl.BlockSpec((tm, half), cur),
            pl.BlockSpec((tm, half), cur),
            pl.BlockSpec((tm, d), cur),
            pl.BlockSpec(memory_space=pl.ANY),
            pl.BlockSpec((1, d), const),
            pl.BlockSpec(w_router_t.shape, const),
            pl.BlockSpec(b_router.shape, const),
        ],
        out_specs=[
            pl.BlockSpec((tm, d), cur),
            pl.BlockSpec((tm * SUBLANES, LANES), prev),
            pl.BlockSpec((SUBLANES, tm), prev_t),
            pl.BlockSpec((SUBLANES, tm), prev_t),
        ],
        scratch_shapes=[pltpu.VMEM((tm, d), F32), pltpu.VMEM((tm, 1), F32),
                        pltpu.VMEM((2, W_STAGE_ROWS, d), F32), pltpu.VMEM(w_out.shape[1:], BF16),
                        pltpu.SemaphoreType.DMA((2,))],
        out_shape=[
            jax.ShapeDtypeStruct((n, d), F32),
            jax.ShapeDtypeStruct((n * SUBLANES, LANES), U32),
            jax.ShapeDtypeStruct((SUBLANES, n), I32),
            jax.ShapeDtypeStruct((SUBLANES, n), F32),
        ],
        compiler_params=_cparams(1),
        name="outproj",
    )(mix_nsa, mix_pool, x2d, w_out, ln_moe, w_router_t, b_router)


def _dispatch_kernel(eid_ref, dest_ref, meta_ref, rank_scr):
    n = eid_ref.shape[1]
    t = T_RANK
    row = lax.broadcasted_iota(I32, (N_EXPERTS, t), 0)
    before = (lax.broadcasted_iota(I32, (t, t), 0) < lax.broadcasted_iota(I32, (t, t), 1)).astype(BF16)
    carry = jnp.zeros((N_EXPERTS, 1), F32)
    for k in range(2):
        for j in range(n // t):
            sl = slice(j * t, (j + 1) * t)
            oh = row == eid_ref[k:k + 1, sl]
            ohf = oh.astype(F32)
            prior = _dot(ohf.astype(BF16), before) + carry
            rank_scr[k:k + 1, sl] = jnp.sum(jnp.where(oh, prior, 0.0), axis=0, keepdims=True)
            carry = carry + jnp.sum(ohf, axis=1, keepdims=True)
    counts = jnp.broadcast_to(carry, (N_EXPERTS, LANES))
    padded = jnp.floor((counts + (TM_MOE - 1.0)) * (1.0 / TM_MOE)) * TM_MOE
    ends = padded
    r_idx = lax.broadcasted_iota(I32, (N_EXPERTS, LANES), 0)
    sh = 1
    while sh < N_EXPERTS:
        ends = ends + jnp.where(r_idx >= sh, pltpu.roll(ends, sh, 0), 0.0)
        sh *= 2
    starts = (ends - padded)[:, 0:1]
    for k in range(2):
        for j in range(n // t):
            sl = slice(j * t, (j + 1) * t)
            oh = row == eid_ref[k:k + 1, sl]
            base = jnp.sum(jnp.where(oh, starts, 0.0), axis=0, keepdims=True)
            dest_ref[k:k + 1, sl] = (rank_scr[k:k + 1, sl] + base).astype(I32)
    for k in range(2, SUBLANES):
        dest_ref[k:k + 1, :] = jnp.zeros((1, n), I32)
    blk_start = lax.broadcasted_iota(I32, (N_EXPERTS, LANES), 1).astype(F32) * TM_MOE
    blk_expert = jnp.sum((ends <= blk_start).astype(F32), axis=0, keepdims=True)
    blk_expert = jnp.minimum(blk_expert, N_EXPERTS - 1.0)
    n_used = ends[N_EXPERTS - 1:N_EXPERTS, :] * (1.0 / TM_MOE)
    on_diag = r_idx == lax.broadcasted_iota(I32, (N_EXPERTS, LANES), 1)
    end_blk = jnp.sum(jnp.where(on_diag, ends, 0.0), axis=0, keepdims=True) * (1.0 / TM_MOE)
    n_blk = jnp.sum(jnp.where(on_diag, padded, 0.0), axis=0, keepdims=True) * (1.0 / TM_MOE)
    r8 = lax.broadcasted_iota(I32, (SUBLANES, LANES), 0)
    meta = jnp.where(r8 == 0, blk_expert,
                     jnp.where(r8 == 1, n_used, jnp.where(r8 == 2, end_blk, jnp.where(r8 == 3, n_blk, 0.0))))
    meta_ref[...] = meta.astype(I32)


def _dispatch(eid):
    n = eid.shape[1]
    return pl.pallas_call(
        _dispatch_kernel,
        out_shape=[jax.ShapeDtypeStruct((SUBLANES, n), I32),
                   jax.ShapeDtypeStruct((SUBLANES, LANES), I32)],
        scratch_shapes=[pltpu.VMEM((SUBLANES, n), F32)],
        compiler_params=pltpu.CompilerParams(vmem_limit_bytes=VMEM_LIMIT_BYTES),
        name="dispatch",
    )(eid)


def _row_copy(src_ref, src_row, dst_ref, dst_row, sem):
    tile = lambda ref, row: ref.at[pl.ds(pl.multiple_of(row * SUBLANES, SUBLANES), SUBLANES)]
    return pltpu.make_async_copy(tile(src_ref, src_row), tile(dst_ref, dst_row), sem)


def _scatter_kernel(dest_ref, meta_ref, h_ref, xs_ref, zero_buf, sem, zsem):
    tm = h_ref.shape[0] // SUBLANES
    blk = zero_buf.shape[0]

    @pl.when(pl.program_id(0) == 0)
    def _():
        zero_buf[...] = jnp.zeros(zero_buf.shape, zero_buf.dtype)
        n_used = meta_ref[1, 0]

        def zero_block(b):
            return pltpu.make_async_copy(zero_buf, xs_ref.at[pl.ds(pl.multiple_of(b * blk, blk), blk)], zsem)

        def sweep(act):
            def per_expert(e, c):
                @pl.when(meta_ref[3, e] > 0)
                def _():
                    act(zero_block(meta_ref[2, e] - 1))
                return c

            def per_tail(b, c):
                @pl.when(b >= n_used)
                def _():
                    act(zero_block(b))
                return c

            lax.fori_loop(0, N_EXPERTS, per_expert, 0)
            lax.fori_loop(0, xs_ref.shape[0] // blk, per_tail, 0)

        sweep(lambda cp: cp.start())
        sweep(lambda cp: cp.wait())

    def start(r, c):
        for k in range(2):
            _row_copy(h_ref, r, xs_ref, dest_ref[0, k, r], sem).start(priority=k)
        return c

    lax.fori_loop(0, tm, start, 0, unroll=DMA_ISSUE_UNROLL)
    for k in range(2):
        pltpu.make_async_copy(h_ref, xs_ref.at[pl.ds(0, tm * SUBLANES)], sem).wait()


def _scatter(dest3, meta, h2, cap):
    nsteps, _, tm = dest3.shape
    return pl.pallas_call(
        _scatter_kernel,
        grid=(nsteps,),
        in_specs=[
            pl.BlockSpec((1, 2, tm), lambda i: (i, 0, 0), memory_space=pltpu.SMEM),
            pl.BlockSpec(memory_space=pltpu.SMEM),
            pl.BlockSpec((tm * SUBLANES, LANES), lambda i: (i, 0)),
        ],
        out_specs=pl.BlockSpec(memory_space=pl.ANY),
        out_shape=jax.ShapeDtypeStruct((cap * SUBLANES, LANES), h2.dtype),
        scratch_shapes=[pltpu.VMEM((TM_MOE * SUBLANES, LANES), h2.dtype), pltpu.SemaphoreType.DMA(()),
                        pltpu.SemaphoreType.DMA(())],
        compiler_params=_cparams(1),
        name="scatter",
    )(dest3, meta, h2)


def _experts_kernel(meta_ref, xs_ref, wg_hbm, wu_hbm, wd_hbm, ys_ref,
                    wg_buf, wu_buf, wd_buf, wg16, wu16, wd16, elist_ref, ord_ref, sem):
    c = pl.program_id(0)
    n_used = meta_ref[1, 0]
    e = meta_ref[0, c]
    first = (c == 0) | (e != meta_ref[0, jnp.maximum(c - 1, 0)])

    def fetch(expert, slot):
        return [pltpu.make_async_copy(w.at[expert], buf.at[slot], sem.at[slot, i])
                for i, (w, buf) in enumerate(((wg_hbm, wg_buf), (wu_hbm, wu_buf), (wd_hbm, wd_buf)))]

    @pl.when(c == 0)
    def _():
        def scan(b, cnt):
            eb = meta_ref[0, b]
            take = ((b == 0) | (eb != meta_ref[0, jnp.maximum(b - 1, 0)])) & (b < n_used)

            @pl.when(take)
            def _():
                elist_ref[cnt] = eb

            return cnt + take.astype(I32)

        cnt = lax.fori_loop(0, pl.num_programs(0), scan, 0)
        ord_ref[0] = 0
        ord_ref[1] = cnt
        for k in range(W_SLOTS):
            @pl.when(k < cnt)
            def _(k=k):
                for cp, prio in zip(fetch(elist_ref[k], k), WEIGHT_DMA_PRIORITIES):
                    cp.start(priority=prio)

    @pl.when(first & (c < n_used))
    def _():
        @pl.when(c > 0)
        def _():
            ord_ref[0] = ord_ref[0] + 1
        for cp in fetch(e, ord_ref[0] % W_SLOTS):
            cp.wait()

    def mlp(weights):
        lo, hi = _unpack_bf16_halves(_load_token_tiles(xs_ref))
        lo, hi = lo.astype(BF16), hi.astype(BF16)
        half = lo.shape[1]
        wg, wu, wd = weights

        def proj(w):
            return _dot(lo, w[:half, :]) + _dot(hi, w[half:, :])

        act = (jax.nn.silu(proj(wg)) * proj(wu)).astype(BF16)
        _store_token_tiles(ys_ref, _pack_bf16_halves(_dot(act, wd)))

    @pl.when(first & (c < n_used))
    def _():
        k = ord_ref[0]
        slot = k % W_SLOTS
        cast = [buf[slot].astype(BF16) for buf in (wg_buf, wu_buf, wd_buf)]
        for ref16, w in zip((wg16, wu16, wd16), cast):
            ref16[...] = w
        mlp(cast)
        ahead = k + W_SLOTS

        @pl.when(ahead < ord_ref[1])
        def _():
            for cp, prio in zip(fetch(elist_ref[jnp.minimum(ahead, N_EXPERTS - 1)], slot), WEIGHT_DMA_PRIORITIES):
                cp.start(priority=prio)

    @pl.when(jnp.logical_not(first) & (c < n_used))
    def _():
        mlp((wg16[...], wu16[...], wd16[...]))

    @pl.when(c >= n_used)
    def _():
        ys_ref[...] = jnp.zeros(ys_ref.shape, U32)


def _experts(meta, xs, w_gate, w_up, w_down):
    rows = TM_MOE * SUBLANES
    nblk = xs.shape[0] // rows
    d, ff = w_gate.shape[1:]

    def row_map(c, meta):
        return (jnp.minimum(c, meta[1, 0] - 1), 0)

    hbm = pl.BlockSpec(memory_space=pl.ANY)
    return pl.pallas_call(
        _experts_kernel,
        grid_spec=pltpu.PrefetchScalarGridSpec(
            num_scalar_prefetch=1,
            grid=(nblk,),
            in_specs=[pl.BlockSpec((rows, LANES), row_map), hbm, hbm, hbm],
            out_specs=pl.BlockSpec((rows, LANES), lambda c, meta: (c, 0)),
            scratch_shapes=[
                pltpu.VMEM((W_SLOTS, d, ff), F32), pltpu.VMEM((W_SLOTS, d, ff), F32),
                pltpu.VMEM((W_SLOTS, ff, d), F32),
                pltpu.VMEM((d, ff), BF16), pltpu.VMEM((d, ff), BF16), pltpu.VMEM((ff, d), BF16),
                pltpu.SMEM((N_EXPERTS,), I32),
                pltpu.SMEM((2,), I32),
                pltpu.SemaphoreType.DMA((W_SLOTS, 3)),
            ],
        ),
        out_shape=jax.ShapeDtypeStruct(xs.shape, U32),
        compiler_params=_cparams(1),
        name="experts",
    )(meta, xs, w_gate, w_up, w_down)


def _combine_kernel(dest_ref, dest_next_ref, x1_ref, gate_ref, ln_ref, ys_ref, o_ref, ybuf, sem):
    i = pl.program_id(0)
    tm = x1_ref.shape[0]
    slot = i % 2

    def issue(d_ref, s):
        def start(r, c):
            for k in range(2):
                _row_copy(ys_ref, d_ref[0, k, r], ybuf.at[s, k], r, sem.at[s]).start(priority=k)
            return c

        lax.fori_loop(0, tm, start, 0, unroll=DMA_ISSUE_UNROLL)

    @pl.when(i == 0)
    def _():
        issue(dest_ref, 0)

    @pl.when(i + 1 < pl.num_programs(0))
    def _():
        issue(dest_next_ref, 1 - slot)

    for k in range(2):
        pltpu.make_async_copy(ys_ref.at[pl.ds(0, tm * SUBLANES)], ybuf.at[slot, k], sem.at[slot]).wait()
    gate = gate_ref[...]
    lo0, hi0 = _unpack_bf16_halves(_load_token_tiles(ybuf.at[slot, 0]))
    lo1, hi1 = _unpack_bf16_halves(_load_token_tiles(ybuf.at[slot, 1]))
    g0, g1 = gate[:, 0:1], gate[:, 1:2]
    moe = jnp.concatenate([lo0 * g0 + lo1 * g1, hi0 * g0 + hi1 * g1], axis=1)
    x2 = x1_ref[...] + moe
    ms = jnp.mean(x2 * x2, axis=-1, keepdims=True)
    o_ref[...] = x2 * lax.rsqrt(ms + EPS) * ln_ref[...]


def _combine(dest3, x1, gate_rows, ln_final, ys):
    n, d = x1.shape
    tm = TM_COMBINE
    last = n // tm - 1
    return pl.pallas_call(
        _combine_kernel,
        grid=(n // tm,),
        in_specs=[
            pl.BlockSpec((1, 2, tm), lambda i: (i, 0, 0), memory_space=pltpu.SMEM),
            pl.BlockSpec((1, 2, tm), lambda i: (jnp.minimum(i + 1, last), 0, 0), memory_space=pltpu.SMEM),
            pl.BlockSpec((tm, d), lambda i: (i, 0)),
            pl.BlockSpec((tm, LANES), lambda i: (i, 0)),
            pl.BlockSpec((1, d), lambda i: (0, 0)),
            pl.BlockSpec(memory_space=pl.ANY),
        ],
        out_specs=pl.BlockSpec((tm, d), lambda i: (i, 0)),
        out_shape=jax.ShapeDtypeStruct((n, d), F32),
        scratch_shapes=[pltpu.VMEM((2, 2, tm * SUBLANES, LANES), U32), pltpu.SemaphoreType.DMA((2,))],
        compiler_params=_cparams(1),
        name="combine",
    )(dest3, dest3, x1, gate_rows, ln_final, ys)


def kernel(x, positions, ln_mix, w_in, pe_cmp_k, w_cmp_k1, w_cmp_k2, pe_cmp_v, w_cmp_v1, w_cmp_v2, w_pool, b_pool, pool_scale, gn_nsa, gn_pool, w_out, ln_moe, w_router_group, b_router_group, w_router_expert, b_router_expert, w_gate, w_up, w_down, ln_final):
    batch, seq, d = x.shape
    n = batch * seq
    depth = w_in.shape[0]

    inv_freq = ROPE_THETA ** (-jnp.arange(0, ROT_DIM, 2, dtype=F32) / ROT_DIM)
    ang_t = inv_freq[:, None] * positions.astype(F32).reshape(1, n)
    kblk = (jnp.arange(seq, dtype=I32)[:, None] // SEL_LEN == jnp.arange(LANES, dtype=I32)[None, :]).astype(BF16)

    x2d = x.reshape(n, d)
    assert depth == 1, "single-layer operation"
    for l in range(depth):
        gate_rows_pad = -(-N_GATES // SUBLANES) * SUBLANES
        qnt, qrt, kvc, kk, vt, gt, u = _inproj(x2d, ln_mix[l][None], ang_t, jnp.swapaxes(w_in, 1, 2),
                                               gate_rows_pad, batch, seq)

        half = CMP_LEN * HEAD_DIM // 2
        def stack_w1(w):
            return jnp.concatenate([w[:half], w[half:]], axis=1)
        w1 = jnp.stack([stack_w1(w_cmp_k1[l]), stack_w1(w_cmp_v1[l])]).astype(BF16)
        w2 = jnp.stack([w_cmp_k2[l], w_cmp_v2[l]]).astype(BF16)
        def pe_rows(pe):
            return jnp.pad(pe.reshape(2, half), ((0, SUBLANES - 2), (0, 0)))
        pe2 = jnp.stack([pe_rows(pe_cmp_k[l]), pe_rows(pe_cmp_v[l])]).astype(BF16)
        cmp = _compress(kvc, pe2, w1, w2)

        mix_nsa = _nsa(qnt, qrt, cmp, kk, vt, kblk, gt, gn_nsa[l][None])
        mix_pool = _pool(u.reshape(batch, seq, -1), w_pool[l].astype(BF16), b_pool[l][None],
                         pool_scale[l][None], gn_pool[l][None])

        ng, ne = N_EXPERT_GROUPS, EXPERTS_PER_GROUP
        w_r = jnp.concatenate([
            w_router_group[l].T, jnp.zeros((ne - ng, d), F32),
            jnp.transpose(w_router_expert[l], (0, 2, 1)).reshape(ng * ne, d),
            jnp.zeros((LANES - ne - ng * ne, d), F32)], axis=0).astype(BF16)
        b_r = jnp.concatenate([
            b_router_group[l], jnp.zeros((ne - ng,), F32), b_router_expert[l].reshape(-1),
            jnp.zeros((LANES - ne - ng * ne,), F32)])
        b_r = jnp.broadcast_to(b_r[:, None], (LANES, LANES))
        x1, h2, eid, gate = _outproj(mix_nsa, mix_pool, x2d, w_out, ln_moe[l][None], w_r, b_r)

        dest, meta = _dispatch(eid)
        cap = (2 * n // TM_MOE + N_EXPERTS) * TM_MOE
        def per_tile(tm):
            return jnp.transpose(dest[:2].reshape(2, n // tm, tm), (1, 0, 2))
        xs = _scatter(per_tile(TS_SCATTER), meta, h2, cap)
        ys = _experts(meta, xs, w_gate[l], w_up[l], w_down[l])
        dest_tok = per_tile(TM_COMBINE)
        gate_rows = jnp.pad(gate[:2].T, ((0, 0), (0, LANES - 2)))
        x2d = _combine(dest_tok, x1, gate_rows, ln_final[None], ys)
    return x2d.reshape(batch, seq, d)
```

```python
import jax
import jax.numpy as jnp
from jax import lax
from jax.experimental import pallas as pl
from jax.experimental.pallas import tpu as pltpu

F32 = jnp.float32
BF16 = jnp.bfloat16
I32 = jnp.int32

HEAD_DIM = 128
N_HEADS = 8
N_KV = 2
GQA = N_HEADS // N_KV
N_GATES = 3 * N_HEADS
ROT_DIM = HEAD_DIM // 4
ROPE_THETA = 500000.0
CMP_LEN = 32
CMP_STRIDE = 16
CMP_HIDDEN = 2 * HEAD_DIM
SEL_LEN = 64
SEL_TOPK = 16
N_LOCAL = 2
WINDOW = 512
POOL_SIZES = (2, 4, 8, 16)
N_EXPERT_GROUPS = 4
EXPERTS_PER_GROUP = 8
N_EXPERTS = N_EXPERT_GROUPS * EXPERTS_PER_GROUP
EPS = 1e-6
NEG = -1e30
BIG = 1e30
SCALE = HEAD_DIM ** -0.5
LOG2E = 1.4426950408889634
Q_SCALE = SCALE * LOG2E

LANES = 128
SUBLANES = 8
VMEM_LIMIT_BYTES = 56 * 1024 * 1024

TM_PROJ = 512
TQ = 256
TK = 256
SEL_MASK_BIAS = 2.0 ** 40
V_ONES_ROWS = 16
TM_MOE = 512
TS_SCATTER = 1024
TM_COMBINE = 512
WEIGHT_DMA_PRIORITIES = (0, 1, 1)
W_STAGE_ROWS = 256
W_SLOTS = 2
DMA_ISSUE_UNROLL = 8
T_RANK = 512


def _cparams(n_axes):
    return pltpu.CompilerParams(dimension_semantics=("arbitrary",) * n_axes, vmem_limit_bytes=VMEM_LIMIT_BYTES)


def _dot(a, b):
    return jnp.dot(a, b, preferred_element_type=F32)


def _dot_nt(a, b):
    return lax.dot_general(a, b, (((1,), (1,)), ((), ())), preferred_element_type=F32)


U32 = jnp.uint32
_HI16 = 0xFFFF0000


def _pack_bf16_halves(x):
    c = x.shape[1] // 2
    bits = lambda t: lax.bitcast_convert_type(t.astype(BF16).astype(F32), U32)
    return lax.shift_right_logical(bits(x[:, :c]), U32(16)) | (bits(x[:, c:]) & U32(_HI16))


def _store_token_tiles(ref, words):
    m, width = words.shape
    assert width == SUBLANES * LANES
    for j in range(SUBLANES):
        ref[pl.ds(j, m, stride=SUBLANES), :] = words[:, j * LANES:(j + 1) * LANES]


def _load_token_tiles(ref):
    m = ref.shape[0] // SUBLANES
    return jnp.concatenate([ref[pl.ds(j, m, stride=SUBLANES), :] for j in range(SUBLANES)], axis=1)


def _unpack_bf16_halves(w):
    lo = lax.bitcast_convert_type(lax.shift_left(w, U32(16)), F32)
    hi = lax.bitcast_convert_type(w & U32(_HI16), F32)
    return lo, hi


def _rope(t, cos, sin, axis):
    idx = lax.broadcasted_iota(I32, t.shape, axis)
    partner = jnp.where(idx < ROT_DIM // 2,
                        pltpu.roll(t, HEAD_DIM - ROT_DIM // 2, axis),
                        pltpu.roll(t, ROT_DIM // 2, axis))
    return t * cos + partner * sin


def _inproj_kernel(x_ref, ln_ref, ang_ref, wt_hbm,
                   qnt_ref, qrt_ref, kvc_ref, kk_ref, vt_ref, gt_ref, u_ref,
                   kc_scr, stage, wq_ref, wv_ref, wg_ref, wkc_ref, wk_ref, wu_ref, wsem):
    @pl.when(pl.program_id(0) == 0)
    def _():
        rc = stage.shape[1]
        q_w, kv_w = N_HEADS * HEAD_DIM, N_KV * HEAD_DIM
        o_kc, o_ks = q_w, q_w + 2 * kv_w
        o_vs, o_kw, o_vw, o_g = o_ks + kv_w, o_ks + 2 * kv_w, o_ks + 3 * kv_w, o_ks + 4 * kv_w
        o_u = o_g + N_GATES
        pieces = ([(r0, rc, wq_ref, r0) for r0 in range(0, q_w, rc)]
                  + [(o_kc + r0, rc, wkc_ref, r0) for r0 in range(0, 2 * kv_w, rc)]
                  + [(o_ks, kv_w, wk_ref, 0), (o_kw, kv_w, wk_ref, kv_w)]
                  + [(o_vs, kv_w, wv_ref, 0), (o_vw, kv_w, wv_ref, kv_w)]
                  + [(o_g, N_GATES, wg_ref, 0)]
                  + [(o_u + r0, rc, wu_ref, r0) for r0 in range(0, wu_ref.shape[0], rc)])
        assert all(n <= rc and src % SUBLANES == 0 for src, n, _, _ in pieces)
        wg_ref[...] = jnp.zeros(wg_ref.shape, BF16)

        def piece_copy(i, slot):
            src, rows, _, _ = pieces[i]
            return pltpu.make_async_copy(wt_hbm.at[0, pl.ds(src, rows)], stage.at[slot, pl.ds(0, rows)],
                                         wsem.at[slot])

        piece_copy(0, 0).start()
        for i, (src, rows, dst, dst_row) in enumerate(pieces):
            slot = i % 2
            if i + 1 < len(pieces):
                piece_copy(i + 1, 1 - slot).start()
            piece_copy(i, slot).wait()
            dst[dst_row:dst_row + rows, :] = stage[slot, :rows, :].astype(BF16)

    x = x_ref[...]
    ms = jnp.mean(x * x, axis=-1, keepdims=True)
    hb = (x * lax.rsqrt(ms + EPS) * ln_ref[...]).astype(BF16)
    chunk = 4 * HEAD_DIM

    ang = ang_ref[...]
    ca, sa = jnp.cos(ang), jnp.sin(ang)
    rest = (HEAD_DIM - ROT_DIM, ang.shape[1])
    cost = jnp.concatenate([ca, ca, jnp.ones(rest, F32)], axis=0)
    sint = jnp.concatenate([-sa, sa, jnp.zeros(rest, F32)], axis=0)
    cos, sin = cost.T, sint.T
    for ch in range(2):
        rt = _dot_nt(wq_ref[ch * chunk:(ch + 1) * chunk, :], hb)
        for j in range(4):
            t = rt[j * HEAD_DIM:(j + 1) * HEAD_DIM, :]
            qnt_ref[0, ch * 4 + j] = (t * Q_SCALE).astype(BF16)
            qrt_ref[0, ch * 4 + j] = (_rope(t, cost, sint, 0) * Q_SCALE).astype(BF16)
    rt = _dot_nt(wv_ref[...], hb)
    ones = jnp.ones((V_ONES_ROWS, TK), BF16)
    for j in range(4):
        for s in range(vt_ref.shape[2]):
            vt_ref[0, j, s] = jnp.concatenate(
                [rt[j * HEAD_DIM:(j + 1) * HEAD_DIM, s * TK:(s + 1) * TK].astype(BF16), ones], axis=0)
    gt_ref[...] = _dot_nt(wg_ref[...], hb)

    r = _dot_nt(hb, wkc_ref[...])
    groups = kvc_ref.shape[2]
    for j in range(4):
        kc_scr[j] = r[:, j * HEAD_DIM:(j + 1) * HEAD_DIM]
        for i in range(CMP_STRIDE):
            kvc_ref[0, j, :, i * HEAD_DIM:(i + 1) * HEAD_DIM] = (
                kc_scr[j, pl.ds(i, groups, stride=CMP_STRIDE), :].astype(BF16))
    r = _dot_nt(hb, wk_ref[...])
    for j in range(4):
        kk_ref[0, j] = _rope(r[:, j * HEAD_DIM:(j + 1) * HEAD_DIM], cos, sin, 1).astype(BF16)
    for ch in range(u_ref.shape[1] // chunk):
        u_ref[:, ch * chunk:(ch + 1) * chunk] = _dot_nt(hb, wu_ref[ch * chunk:(ch + 1) * chunk, :])


def _inproj(x2d, ln, ang_t, w_in_t, gate_rows, batch, seq):
    n, d = x2d.shape
    tm = TM_PROJ
    assert tm % TK == 0
    spb = seq // tm
    const = lambda i: (0, 0)
    q_w, kv_w = N_HEADS * HEAD_DIM, N_KV * HEAD_DIM
    pool_w = w_in_t.shape[1] - (q_w + 6 * kv_w + N_GATES)
    row_heads = lambda i: (i // spb, 0, i % spb, 0)
    col_heads = lambda i: (i // spb, 0, 0, i % spb)
    return pl.pallas_call(
        _inproj_kernel,
        grid=(n // tm,),
        in_specs=[
            pl.BlockSpec((tm, d), lambda i: (i, 0)),
            pl.BlockSpec((1, d), const),
            pl.BlockSpec((ang_t.shape[0], tm), lambda i: (0, i)),
            pl.BlockSpec(memory_space=pl.ANY),
        ],
        out_specs=[
            pl.BlockSpec((1, N_HEADS, HEAD_DIM, tm), col_heads),
            pl.BlockSpec((1, N_HEADS, HEAD_DIM, tm), col_heads),
            pl.BlockSpec((1, 4, tm // CMP_STRIDE, CMP_STRIDE * HEAD_DIM), row_heads),
            pl.BlockSpec((1, 4, tm, HEAD_DIM), row_heads),
            pl.BlockSpec((1, 4, tm // TK, HEAD_DIM + V_ONES_ROWS, TK), lambda i: (i // spb, 0, i % spb, 0, 0)),
            pl.BlockSpec((gate_rows, tm), lambda i: (0, i)),
            pl.BlockSpec((tm, pool_w), lambda i: (i, 0)),
        ],
        out_shape=[
            jax.ShapeDtypeStruct((batch, N_HEADS, HEAD_DIM, seq), BF16),
            jax.ShapeDtypeStruct((batch, N_HEADS, HEAD_DIM, seq), BF16),
            jax.ShapeDtypeStruct((batch, 4, seq // CMP_STRIDE, CMP_STRIDE * HEAD_DIM), BF16),
            jax.ShapeDtypeStruct((batch, 4, seq, HEAD_DIM), BF16),
            jax.ShapeDtypeStruct((batch, 4, seq // TK, HEAD_DIM + V_ONES_ROWS, TK), BF16),
            jax.ShapeDtypeStruct((gate_rows, n), F32),
            jax.ShapeDtypeStruct((n, pool_w), F32),
        ],
        scratch_shapes=[
            pltpu.VMEM((4, tm, HEAD_DIM), F32),
            pltpu.VMEM((2, W_STAGE_ROWS, d), F32),
            pltpu.VMEM((q_w, d), BF16), pltpu.VMEM((2 * kv_w, d), BF16), pltpu.VMEM((gate_rows, d), BF16),
            pltpu.VMEM((2 * kv_w, d), BF16), pltpu.VMEM((2 * kv_w, d), BF16), pltpu.VMEM((pool_w, d), BF16),
            pltpu.SemaphoreType.DMA((2,)),
        ],
        compiler_params=_cparams(1),
        name="inproj",
    )(x2d, ln, ang_t, w_in_t)


def _compress_kernel(t_ref, pe_ref, w1_ref, w2_ref, o_ref):
    t2 = t_ref[0, 0]
    w1 = w1_ref[0]
    a = _dot(t2, w1)
    pb = _dot(pe_ref[0], w1)
    bias = pb[0:1, :CMP_HIDDEN] + pb[1:2, CMP_HIDDEN:]
    nrow = a.shape[0]
    nxt = pltpu.roll(a[:, CMP_HIDDEN:], nrow - 1, 0)
    hid = jax.nn.gelu(a[:, :CMP_HIDDEN] + nxt + bias)
    out = _dot(hid.astype(BF16), w2_ref[0])
    row = lax.broadcasted_iota(I32, out.shape, 0)
    out = jnp.where(row < nrow - 1, out, 0.0)
    is_value = pl.program_id(1) >= N_KV
    o_ref[0, 0] = jnp.where(is_value, out.T, out).astype(BF16)


def _compress(kvc2, pe2, w1, w2):
    batch, four, nrow, width = kvc2.shape
    return pl.pallas_call(
        _compress_kernel,
        grid=(batch, four),
        in_specs=[
            pl.BlockSpec((1, 1, nrow, width), lambda b, j: (b, j, 0, 0)),
            pl.BlockSpec((1, SUBLANES, width), lambda b, j: (j // N_KV, 0, 0)),
            pl.BlockSpec((1, width, 2 * CMP_HIDDEN), lambda b, j: (j // N_KV, 0, 0)),
            pl.BlockSpec((1, CMP_HIDDEN, HEAD_DIM), lambda b, j: (j // N_KV, 0, 0)),
        ],
        out_specs=pl.BlockSpec((1, 1, nrow, HEAD_DIM), lambda b, j: (b, j, 0, 0)),
        out_shape=jax.ShapeDtypeStruct((batch, four, nrow, HEAD_DIM), BF16),
        compiler_params=_cparams(2),
        name="compress",
    )(kvc2, pe2, w1, w2)


def _nsa_kernel(qnt_ref, qrt_ref, cmp_ref, kk_ref, vt_ref, kblk_ref, gt_ref, gn_ref, o_ref,
                m_scr, acc_scr, qa_scr, ocmp_scr, s_scr, o_scr):
    qi = pl.program_id(1)
    tq = qnt_ref.shape[3]
    rows = GQA * tq
    n_cmp_pad = cmp_ref.shape[2]
    n_sel = kk_ref.shape[2] // SEL_LEN
    q0 = qi * tq

    m_scr[...] = jnp.full(m_scr.shape, NEG, F32)
    acc_scr[...] = jnp.zeros(acc_scr.shape, F32)

    def scores(item, slot):
        c, g, q_rows, k_tile, v_t, bias = item
        s_scr[slot] = _dot(k_tile, qa_scr[g, :q_rows, :])

    def softmax_pv(item, slot):
        c, g, q_rows, k_tile, v_t, bias = item
        s = s_scr[slot]
        if bias is not None:
            s = s + bias
        m_old = m_scr[c]
        m_new = jnp.maximum(m_old, jnp.max(s, axis=0, keepdims=True))
        alpha = jnp.exp2(m_old - m_new)
        p = jnp.exp2(s - m_new)
        acc_scr[c] = alpha * acc_scr[c] + _dot(v_t, p.astype(BF16))
        m_scr[c] = m_new

    def flash_out(c):
        return acc_scr[c, :HEAD_DIM, :] / acc_scr[c, HEAD_DIM:HEAD_DIM + 1, :]

    def tile_heads(a):
        return jnp.concatenate([a] * GQA, axis=1)

    kk_i = lax.broadcasted_iota(I32, (TK, tq), 0)
    qq_i = lax.broadcasted_iota(I32, (TK, tq), 1)
    causal_bias = tile_heads(jnp.where(kk_i <= qq_i, 0.0, NEG))
    band_bias = tile_heads(jnp.where(kk_i > qq_i, 0.0, NEG))
    n_win = WINDOW // TK

    for g in range(N_KV):
        qn4 = jnp.concatenate([qnt_ref[0, h] for h in range(g * GQA, (g + 1) * GQA)], axis=1)
        s = _dot(cmp_ref[0, g], qn4)
        n_idx = lax.broadcasted_iota(I32, (n_cmp_pad, rows), 0)
        t_idx = q0 + jnp.bitwise_and(lax.broadcasted_iota(I32, (n_cmp_pad, rows), 1), tq - 1)
        cmp_ok = n_idx * CMP_STRIDE + (CMP_LEN - 1) <= t_idx
        s = jnp.where(cmp_ok, s, NEG)
        m = jnp.max(s, axis=0, keepdims=True)
        e = jnp.where(cmp_ok, jnp.exp2(s - m), 0.0)
        l = jnp.sum(e, axis=0, keepdims=True)
        p = e * jnp.where(l > 0.0, 1.0 / l, 0.0)
        pb = p.astype(BF16)
        ocmp_scr[g] = _dot(cmp_ref[0, N_KV + g], pb)

        jn = lax.broadcasted_iota(I32, (n_sel, n_cmp_pad), 0) * SEL_LEN
        cn = lax.broadcasted_iota(I32, (n_sel, n_cmp_pad), 1) * CMP_STRIDE
        ov = jnp.clip(jnp.minimum(cn + CMP_LEN, jn + SEL_LEN) - jnp.maximum(cn, jn), 0, None)
        w_t = (ov.astype(F32) * (1.0 / CMP_LEN)).astype(BF16)
        imp4 = _dot(w_t, pb)
        imp = imp4[:, 0:tq]
        for r in range(1, GQA):
            imp = imp + imp4[:, r * tq:(r + 1) * tq]
        j_blk = lax.broadcasted_iota(I32, (n_sel, tq), 0)
        t_q = q0 + lax.broadcasted_iota(I32, (n_sel, tq), 1)
        sel_ok = j_blk * SEL_LEN <= t_q
        back = t_q // SEL_LEN - j_blk
        forced = (j_blk == 0) | ((back >= 0) & (back < N_LOCAL))
        val = jnp.where(sel_ok & forced, BIG, jnp.where(sel_ok, imp, -BIG))
        rank = jnp.zeros((n_sel, tq), F32)
        for i in range(n_sel):
            vi = val[i:i + 1, :]
            beats = (vi > val) | ((vi == val) & (j_blk > i))
            rank = rank + beats.astype(F32)
        keep = (rank < float(min(SEL_TOPK, n_sel))) & sel_ok
        bias_t = jnp.where(keep, 0.0, -SEL_MASK_BIAS)
        bias_t = jnp.concatenate([bias_t, jnp.zeros((HEAD_DIM - n_sel, tq), F32)], axis=0)
        qa_scr[g, :HEAD_DIM, :] = jnp.concatenate(
            [qrt_ref[0, h] for h in range(g * GQA, (g + 1) * GQA)], axis=1)
        qa_scr[g, HEAD_DIM:, :] = tile_heads(bias_t.astype(BF16))

    def sel_item(g, kj, bias):
        st = pl.multiple_of(kj * TK, TK)
        k_aug = jnp.concatenate([kk_ref[0, g, pl.ds(st, TK), :], kblk_ref[pl.ds(st, TK), :]], axis=1)
        return (g, g, 2 * HEAD_DIM, k_aug, vt_ref[0, g, kj], bias)

    def win_item(g, back):
        kj = jnp.maximum(qi - back, 0)
        st = pl.multiple_of(kj * TK, TK)
        off = jnp.where(qi >= back, 0.0, NEG)
        bias = causal_bias if back == 0 else (band_bias + off if back == n_win else off)
        return (N_KV + g, g, HEAD_DIM, kk_ref[0, N_KV + g, pl.ds(st, TK), :], vt_ref[0, N_KV + g, kj], bias)

    assert N_KV == 2
    scores(sel_item(0, 0, None), 0)

    def sel_tile(kj):
        scores(sel_item(1, kj, None), 1)
        softmax_pv(sel_item(0, kj, None), 0)
        scores(sel_item(0, kj + 1, None), 0)
        softmax_pv(sel_item(1, kj, None), 1)

    def sel_pair(j, carry):
        sel_tile(2 * j)
        sel_tile(2 * j + 1)
        return carry

    lax.fori_loop(0, qi // 2, sel_pair, 0)

    @pl.when(qi % 2 == 1)
    def _():
        sel_tile(qi - 1)
    tail = ([sel_item(g, qi, causal_bias) for g in range(N_KV)]
            + [win_item(g, back) for back in range(n_win, -1, -1) for g in range(N_KV)])
    for i, item in enumerate(tail):
        if i + 1 < len(tail):
            scores(tail[i + 1], (i + 1) % 2)
        softmax_pv(item, i % 2)

    gsig = jax.nn.sigmoid(gt_ref[...])
    for g in range(N_KV):
        o_cmp, o_sel, o_win = ocmp_scr[g], flash_out(g), flash_out(N_KV + g)
        for r in range(GQA):
            h = g * GQA + r
            sl = slice(r * tq, (r + 1) * tq)
            o_h = (gsig[3 * h:3 * h + 1, :] * o_cmp[:, sl]
                   + gsig[3 * h + 1:3 * h + 2, :] * o_sel[:, sl]
                   + gsig[3 * h + 2:3 * h + 3, :] * o_win[:, sl])
            o_scr[:, h * HEAD_DIM:(h + 1) * HEAD_DIM] = o_h.T

    o = o_scr[...]
    ms = jnp.mean(o * o, axis=-1, keepdims=True)
    o_ref[...] = (o * lax.rsqrt(ms + EPS) * gn_ref[...]).astype(BF16)


def _nsa(qnt, qrt, cmp, kk, vt, kblk, gt, gn):
    batch, _, _, seq = qnt.shape
    assert TQ == TK and WINDOW % TK == 0 and TQ & (TQ - 1) == 0
    nq = seq // TQ
    width = N_HEADS * HEAD_DIM
    return pl.pallas_call(
        _nsa_kernel,
        grid=(batch, nq),
        in_specs=[
            pl.BlockSpec((1, N_HEADS, HEAD_DIM, TQ), lambda b, i: (b, 0, 0, i)),
            pl.BlockSpec((1, N_HEADS, HEAD_DIM, TQ), lambda b, i: (b, 0, 0, i)),
            pl.BlockSpec((1,) + cmp.shape[1:], lambda b, i: (b, 0, 0, 0)),
            pl.BlockSpec((1,) + kk.shape[1:], lambda b, i: (b, 0, 0, 0)),
            pl.BlockSpec((1,) + vt.shape[1:], lambda b, i: (b, 0, 0, 0, 0)),
            pl.BlockSpec((seq, LANES), lambda b, i: (0, 0)),
            pl.BlockSpec((gt.shape[0], TQ), lambda b, i: (0, b * nq + i)),
            pl.BlockSpec((1, width), lambda b, i: (0, 0)),
        ],
        out_specs=pl.BlockSpec((TQ, width), lambda b, i: (b * nq + i, 0)),
        out_shape=jax.ShapeDtypeStruct((batch * seq, width), BF16),
        scratch_shapes=[
            pltpu.VMEM((2 * N_KV, 1, GQA * TQ), F32),
            pltpu.VMEM((2 * N_KV, vt.shape[3], GQA * TQ), F32),
            pltpu.VMEM((N_KV, 2 * HEAD_DIM, GQA * TQ), BF16),
            pltpu.VMEM((N_KV, HEAD_DIM, GQA * TQ), F32),
            pltpu.VMEM((2, TK, GQA * TQ), F32),
            pltpu.VMEM((TQ, width), F32),
        ],
        compiler_params=_cparams(2),
        name="nsa",
    )(qnt, qrt, cmp, kk, vt, kblk, gt, gn)


def _pool_group(cur, halo_rows, t1, window, w_mat, bias, scale):
    halo = halo_rows.shape[0]
    acc = jnp.concatenate([halo_rows, cur], axis=0)
    span = 1
    while span < window:
        acc = acc + jnp.concatenate([jnp.zeros((span, acc.shape[1]), F32), acc[:-span]], axis=0)
        span *= 2
    mean = acc[halo:] / jnp.minimum(t1, float(window))
    return (_dot((mean - cur).astype(BF16), w_mat) + bias) * scale


def _pool_kernel(u_ref, w_ref, b_ref, sc_ref, gn_ref, o_ref, y_scr, halo_scr):
    si = pl.program_id(1)
    tm = u_ref.shape[1]
    halo = halo_scr.shape[0]

    @pl.when(si == 0)
    def _():
        halo_scr[...] = jnp.zeros(halo_scr.shape, F32)

    t1 = (si * tm + 1 + lax.broadcasted_iota(I32, (tm, 1), 0)).astype(F32)
    cg = w_ref.shape[1]
    for gi, window in enumerate(POOL_SIZES):
        sl = slice(gi * cg, (gi + 1) * cg)
        cur = u_ref[0, :, sl]
        y_scr[:, sl] = _pool_group(cur, halo_scr[:, sl], t1, window, w_ref[gi], b_ref[:, sl], sc_ref[:, sl])
        halo_scr[:, sl] = cur[tm - halo:, :]
    y = y_scr[...]
    ms = jnp.mean(y * y, axis=-1, keepdims=True)
    o_ref[...] = (y * lax.rsqrt(ms + EPS) * gn_ref[...]).astype(BF16)


def _pool(u3, w_pool, b_pool, pool_scale, gn_pool):
    batch, seq, c = u3.shape
    tm = TM_PROJ
    ns = seq // tm
    vec = lambda b, i: (0, 0)
    return pl.pallas_call(
        _pool_kernel,
        grid=(batch, ns),
        in_specs=[
            pl.BlockSpec((1, tm, c), lambda b, i: (b, i, 0)),
            pl.BlockSpec(w_pool.shape, lambda b, i: (0, 0, 0)),
            pl.BlockSpec((1, c), vec),
            pl.BlockSpec((1, c), vec),
            pl.BlockSpec((1, c), vec),
        ],
        out_specs=pl.BlockSpec((tm, c), lambda b, i: (b * ns + i, 0)),
        out_shape=jax.ShapeDtypeStruct((batch * seq, c), BF16),
        scratch_shapes=[pltpu.VMEM((tm, c), F32), pltpu.VMEM((max(POOL_SIZES), c), F32)],
        compiler_params=_cparams(2),
        name="pool",
    )(u3, w_pool, b_pool, pool_scale, gn_pool)


def _outproj_kernel(mn_ref, mp_ref, x_ref, w_hbm, ln_ref, wr_ref, br_ref,
                    x1_ref, h2_ref, eid_ref, gate_ref, x1_scr, ssq_scr, stage, w_ref, wsem):
    tm, d = x_ref.shape
    half = mn_ref.shape[1]
    chunk = 512

    @pl.when(pl.program_id(0) == 0)
    def _():
        x1_scr[...] = jnp.zeros(x1_scr.shape, F32)
        ssq_scr[...] = jnp.zeros(ssq_scr.shape, F32)
        rc = stage.shape[1]
        n_chunks = w_ref.shape[0] // rc

        def chunk_copy(c, slot):
            return pltpu.make_async_copy(w_hbm.at[0, pl.ds(c * rc, rc)], stage.at[slot], wsem.at[slot])

        chunk_copy(0, 0).start()
        for c in range(n_chunks):
            slot = c % 2
            if c + 1 < n_chunks:
                chunk_copy(c + 1, 1 - slot).start()
            chunk_copy(c, slot).wait()
            w_ref[c * rc:(c + 1) * rc, :] = stage[slot].astype(BF16)

    h2 = x1_scr[...] * lax.rsqrt(ssq_scr[...] * (1.0 / d) + EPS) * ln_ref[...]
    _store_token_tiles(h2_ref, _pack_bf16_halves(h2))

    logits = _dot_nt(wr_ref[...], h2.astype(BF16)) + br_ref[:, 0:1]
    ng, ne = N_EXPERT_GROUPS, EXPERTS_PER_GROUP
    row = lax.broadcasted_iota(I32, (ne, tm), 0)
    lg = jnp.where(row < ng, logits[0:ne], NEG)
    mg = jnp.max(lg, axis=0, keepdims=True)
    g_sel = jnp.min(jnp.where(lg == mg, row, ne), axis=0, keepdims=True)
    p_g = 1.0 / jnp.sum(jnp.exp(lg - mg), axis=0, keepdims=True)
    le = jnp.zeros((ne, tm), F32)
    for g in range(ng):
        le = jnp.where(g_sel == g, logits[ne * (g + 1):ne * (g + 2)], le)
    ex = jnp.exp(le - jnp.max(le, axis=0, keepdims=True))
    pe = ex / jnp.sum(ex, axis=0, keepdims=True)
    p1 = jnp.max(pe, axis=0, keepdims=True)
    i1 = jnp.min(jnp.where(pe == p1, row, ne), axis=0, keepdims=True)
    rest = jnp.where(row == i1, -1.0, pe)
    p2 = jnp.max(rest, axis=0, keepdims=True)
    i2 = jnp.min(jnp.where(rest == p2, row, ne), axis=0, keepdims=True)
    psum = p1 + p2
    zero_i = jnp.zeros((ne, tm), I32)
    eid = jnp.where(row == 0, g_sel * ne + i1, jnp.where(row == 1, g_sel * ne + i2, zero_i))
    gate = jnp.where(row == 0, p_g * p1 / psum, jnp.where(row == 1, p_g * p2 / psum, 0.0))
    eid_ref[...] = eid
    gate_ref[...] = gate

    mn = mn_ref[...]
    mp = mp_ref[...]
    ssq = jnp.zeros((tm, 1), F32)
    for c in range(d // chunk):
        sl = slice(c * chunk, (c + 1) * chunk)
        x1 = x_ref[:, sl] + (_dot(mn, w_ref[:half, sl]) + _dot(mp, w_ref[half:, sl]))
        x1_ref[:, sl] = x1
        x1_scr[:, sl] = x1
        ssq = ssq + jnp.sum(x1 * x1, axis=-1, keepdims=True)
    ssq_scr[...] = ssq


def _outproj(mix_nsa, mix_pool, x2d, w_out, ln_moe, w_router_t, b_router):
    n, d = x2d.shape
    tm = TM_PROJ
    half = mix_nsa.shape[1]
    const = lambda i: (0, 0)
    last = n // tm - 1
    cur = lambda i: (jnp.minimum(i, last), 0)
    prev = lambda i: (jnp.maximum(i - 1, 0), 0)
    prev_t = lambda i: (0, jnp.maximum(i - 1, 0))
    return pl.pallas_call(
        _outproj_kernel,
        grid=(n // tm + 1,),
        in_specs=[
            pl.BlockSpec((tm, half), cur),
            pl.BlockSpec((tm, half), cur),
            pl.BlockSpec((tm, d), cur),
            pl.BlockSpec(memory_space=pl.ANY),
            pl.BlockSpec((1, d), const),
            pl.BlockSpec(w_router_t.shape, const),
            pl.BlockSpec(b_router.shape, const),
        ],
        out_specs=[
            pl.BlockSpec((tm, d), cur),
            pl.BlockSpec((tm * SUBLANES, LANES), prev),
            pl.BlockSpec((SUBLANES, tm), prev_t),
            pl.BlockSpec((SUBLANES, tm), prev_t),
        ],
        scratch_shapes=[pltpu.VMEM((tm, d), F32), pltpu.VMEM((tm, 1), F32),
                        pltpu.VMEM((2, W_STAGE_ROWS, d), F32), pltpu.VMEM(w_out.shape[1:], BF16),
                        pltpu.SemaphoreType.DMA((2,))],
        out_shape=[
            jax.ShapeDtypeStruct((n, d), F32),
            jax.ShapeDtypeStruct((n * SUBLANES, LANES), U32),
            jax.ShapeDtypeStruct((SUBLANES, n), I32),
            jax.ShapeDtypeStruct((SUBLANES, n), F32),
        ],
        compiler_params=_cparams(1),
        name="outproj",
    )(mix_nsa, mix_pool, x2d, w_out, ln_moe, w_router_t, b_router)


def _dispatch_kernel(eid_ref, dest_ref, meta_ref, rank_scr):
    n = eid_ref.shape[1]
    t = T_RANK
    row = lax.broadcasted_iota(I32, (N_EXPERTS, t), 0)
    before = (lax.broadcasted_iota(I32, (t, t), 0) < lax.broadcasted_iota(I32, (t, t), 1)).astype(BF16)
    carry = jnp.zeros((N_EXPERTS, 1), F32)
    for k in range(2):
        for j in range(n // t):
            sl = slice(j * t, (j + 1) * t)
            oh = row == eid_ref[k:k + 1, sl]
            ohf = oh.astype(F32)
            prior = _dot(ohf.astype(BF16), before) + carry
            rank_scr[k:k + 1, sl] = jnp.sum(jnp.where(oh, prior, 0.0), axis=0, keepdims=True)
            carry = carry + jnp.sum(ohf, axis=1, keepdims=True)
    counts = jnp.broadcast_to(carry, (N_EXPERTS, LANES))
    padded = jnp.floor((counts + (TM_MOE - 1.0)) * (1.0 / TM_MOE)) * TM_MOE
    ends = padded
    r_idx = lax.broadcasted_iota(I32, (N_EXPERTS, LANES), 0)
    sh = 1
    while sh < N_EXPERTS:
        ends = ends + jnp.where(r_idx >= sh, pltpu.roll(ends, sh, 0), 0.0)
        sh *= 2
    starts = (ends - padded)[:, 0:1]
    for k in range(2):
        for j in range(n // t):
            sl = slice(j * t, (j + 1) * t)
            oh = row == eid_ref[k:k + 1, sl]
            base = jnp.sum(jnp.where(oh, starts, 0.0), axis=0, keepdims=True)
            dest_ref[k:k + 1, sl] = (rank_scr[k:k + 1, sl] + base).astype(I32)
    for k in range(2, SUBLANES):
        dest_ref[k:k + 1, :] = jnp.zeros((1, n), I32)
    blk_start = lax.broadcasted_iota(I32, (N_EXPERTS, LANES), 1).astype(F32) * TM_MOE
    blk_expert = jnp.sum((ends <= blk_start).astype(F32), axis=0, keepdims=True)
    blk_expert = jnp.minimum(blk_expert, N_EXPERTS - 1.0)
    n_used = ends[N_EXPERTS - 1:N_EXPERTS, :] * (1.0 / TM_MOE)
    on_diag = r_idx == lax.broadcasted_iota(I32, (N_EXPERTS, LANES), 1)
    end_blk = jnp.sum(jnp.where(on_diag, ends, 0.0), axis=0, keepdims=True) * (1.0 / TM_MOE)
    n_blk = jnp.sum(jnp.where(on_diag, padded, 0.0), axis=0, keepdims=True) * (1.0 / TM_MOE)
    r8 = lax.broadcasted_iota(I32, (SUBLANES, LANES), 0)
    meta = jnp.where(r8 == 0, blk_expert,
                     jnp.where(r8 == 1, n_used, jnp.where(r8 == 2, end_blk, jnp.where(r8 == 3, n_blk, 0.0))))
    meta_ref[...] = meta.astype(I32)


def _dispatch(eid):
    n = eid.shape[1]
    return pl.pallas_call(
        _dispatch_kernel,
        out_shape=[jax.ShapeDtypeStruct((SUBLANES, n), I32),
                   jax.ShapeDtypeStruct((SUBLANES, LANES), I32)],
        scratch_shapes=[pltpu.VMEM((SUBLANES, n), F32)],
        compiler_params=pltpu.CompilerParams(vmem_limit_bytes=VMEM_LIMIT_BYTES),
        name="dispatch",
    )(eid)


def _row_copy(src_ref, src_row, dst_ref, dst_row, sem):
    tile = lambda ref, row: ref.at[pl.ds(pl.multiple_of(row * SUBLANES, SUBLANES), SUBLANES)]
    return pltpu.make_async_copy(tile(src_ref, src_row), tile(dst_ref, dst_row), sem)


def _scatter_kernel(dest_ref, meta_ref, h_ref, xs_ref, zero_buf, sem, zsem):
    tm = h_ref.shape[0] // SUBLANES
    blk = zero_buf.shape[0]

    @pl.when(pl.program_id(0) == 0)
    def _():
        zero_buf[...] = jnp.zeros(zero_buf.shape, zero_buf.dtype)
        n_used = meta_ref[1, 0]

        def zero_block(b):
            return pltpu.make_async_copy(zero_buf, xs_ref.at[pl.ds(pl.multiple_of(b * blk, blk), blk)], zsem)

        def sweep(act):
            def per_expert(e, c):
                @pl.when(meta_ref[3, e] > 0)
                def _():
                    act(zero_block(meta_ref[2, e] - 1))
                return c

            def per_tail(b, c):
                @pl.when(b >= n_used)
                def _():
                    act(zero_block(b))
                return c

            lax.fori_loop(0, N_EXPERTS, per_expert, 0)
            lax.fori_loop(0, xs_ref.shape[0] // blk, per_tail, 0)

        sweep(lambda cp: cp.start())
        sweep(lambda cp: cp.wait())

    def start(r, c):
        for k in range(2):
            _row_copy(h_ref, r, xs_ref, dest_ref[0, k, r], sem).start(priority=k)
        return c

    lax.fori_loop(0, tm, start, 0, unroll=DMA_ISSUE_UNROLL)
    for k in range(2):
        pltpu.make_async_copy(h_ref, xs_ref.at[pl.ds(0, tm * SUBLANES)], sem).wait()


def _scatter(dest3, meta, h2, cap):
    nsteps, _, tm = dest3.shape
    return pl.pallas_call(
        _scatter_kernel,
        grid=(nsteps,),
        in_specs=[
            pl.BlockSpec((1, 2, tm), lambda i: (i, 0, 0), memory_space=pltpu.SMEM),
            pl.BlockSpec(memory_space=pltpu.SMEM),
            pl.BlockSpec((tm * SUBLANES, LANES), lambda i: (i, 0)),
        ],
        out_specs=pl.BlockSpec(memory_space=pl.ANY),
        out_shape=jax.ShapeDtypeStruct((cap * SUBLANES, LANES), h2.dtype),
        scratch_shapes=[pltpu.VMEM((TM_MOE * SUBLANES, LANES), h2.dtype), pltpu.SemaphoreType.DMA(()),
                        pltpu.SemaphoreType.DMA(())],
        compiler_params=_cparams(1),
        name="scatter",
    )(dest3, meta, h2)


def _experts_kernel(meta_ref, xs_ref, wg_hbm, wu_hbm, wd_hbm, ys_ref,
                    wg_buf, wu_buf, wd_buf, wg16, wu16, wd16, elist_ref, ord_ref, sem):
    c = pl.program_id(0)
    n_used = meta_ref[1, 0]
    e = meta_ref[0, c]
    first = (c == 0) | (e != meta_ref[0, jnp.maximum(c - 1, 0)])

    def fetch(expert, slot):
        return [pltpu.make_async_copy(w.at[expert], buf.at[slot], sem.at[slot, i])
                for i, (w, buf) in enumerate(((wg_hbm, wg_buf), (wu_hbm, wu_buf), (wd_hbm, wd_buf)))]

    @pl.when(c == 0)
    def _():
        def scan(b, cnt):
            eb = meta_ref[0, b]
            take = ((b == 0) | (eb != meta_ref[0, jnp.maximum(b - 1, 0)])) & (b < n_used)

            @pl.when(take)
            def _():
                elist_ref[cnt] = eb

            return cnt + take.astype(I32)

        cnt = lax.fori_loop(0, pl.num_programs(0), scan, 0)
        ord_ref[0] = 0
        ord_ref[1] = cnt
        for k in range(W_SLOTS):
            @pl.when(k < cnt)
            def _(k=k):
                for cp, prio in zip(fetch(elist_ref[k], k), WEIGHT_DMA_PRIORITIES):
                    cp.start(priority=prio)

    @pl.when(first & (c < n_used))
    def _():
        @pl.when(c > 0)
        def _():
            ord_ref[0] = ord_ref[0] + 1
        for cp in fetch(e, ord_ref[0] % W_SLOTS):
            cp.wait()

    def mlp(weights):
        lo, hi = _unpack_bf16_halves(_load_token_tiles(xs_ref))
        lo, hi = lo.astype(BF16), hi.astype(BF16)
        half = lo.shape[1]
        wg, wu, wd = weights

        def proj(w):
            return _dot(lo, w[:half, :]) + _dot(hi, w[half:, :])

        act = (jax.nn.silu(proj(wg)) * proj(wu)).astype(BF16)
        _store_token_tiles(ys_ref, _pack_bf16_halves(_dot(act, wd)))

    @pl.when(first & (c < n_used))
    def _():
        k = ord_ref[0]
        slot = k % W_SLOTS
        cast = [buf[slot].astype(BF16) for buf in (wg_buf, wu_buf, wd_buf)]
        for ref16, w in zip((wg16, wu16, wd16), cast):
            ref16[...] = w
        mlp(cast)
        ahead = k + W_SLOTS

        @pl.when(ahead < ord_ref[1])
        def _():
            for cp, prio in zip(fetch(elist_ref[jnp.minimum(ahead, N_EXPERTS - 1)], slot), WEIGHT_DMA_PRIORITIES):
                cp.start(priority=prio)

    @pl.when(jnp.logical_not(first) & (c < n_used))
    def _():
        mlp((wg16[...], wu16[...], wd16[...]))

    @pl.when(c >= n_used)
    def _():
        ys_ref[...] = jnp.zeros(ys_ref.shape, U32)


def _experts(meta, xs, w_gate, w_up, w_down):
    rows = TM_MOE * SUBLANES
    nblk = xs.shape[0] // rows
    d, ff = w_gate.shape[1:]

    def row_map(c, meta):
        return (jnp.minimum(c, meta[1, 0] - 1), 0)

    hbm = pl.BlockSpec(memory_space=pl.ANY)
    return pl.pallas_call(
        _experts_kernel,
        grid_spec=pltpu.PrefetchScalarGridSpec(
            num_scalar_prefetch=1,
            grid=(nblk,),
            in_specs=[pl.BlockSpec((rows, LANES), row_map), hbm, hbm, hbm],
            out_specs=pl.BlockSpec((rows, LANES), lambda c, meta: (c, 0)),
            scratch_shapes=[
                pltpu.VMEM((W_SLOTS, d, ff), F32), pltpu.VMEM((W_SLOTS, d, ff), F32),
                pltpu.VMEM((W_SLOTS, ff, d), F32),
                pltpu.VMEM((d, ff), BF16), pltpu.VMEM((d, ff), BF16), pltpu.VMEM((ff, d), BF16),
                pltpu.SMEM((N_EXPERTS,), I32),
                pltpu.SMEM((2,), I32),
                pltpu.SemaphoreType.DMA((W_SLOTS, 3)),
            ],
        ),
        out_shape=jax.ShapeDtypeStruct(xs.shape, U32),
        compiler_params=_cparams(1),
        name="experts",
    )(meta, xs, w_gate, w_up, w_down)


def _combine_kernel(dest_ref, dest_next_ref, x1_ref, gate_ref, ln_ref, ys_ref, o_ref, ybuf, sem):
    i = pl.program_id(0)
    tm = x1_ref.shape[0]
    slot = i % 2

    def issue(d_ref, s):
        def start(r, c):
            for k in range(2):
                _row_copy(ys_ref, d_ref[0, k, r], ybuf.at[s, k], r, sem.at[s]).start(priority=k)
            return c

        lax.fori_loop(0, tm, start, 0, unroll=DMA_ISSUE_UNROLL)

    @pl.when(i == 0)
    def _():
        issue(dest_ref, 0)

    @pl.when(i + 1 < pl.num_programs(0))
    def _():
        issue(dest_next_ref, 1 - slot)

    for k in range(2):
        pltpu.make_async_copy(ys_ref.at[pl.ds(0, tm * SUBLANES)], ybuf.at[slot, k], sem.at[slot]).wait()
    gate = gate_ref[...]
    lo0, hi0 = _unpack_bf16_halves(_load_token_tiles(ybuf.at[slot, 0]))
    lo1, hi1 = _unpack_bf16_halves(_load_token_tiles(ybuf.at[slot, 1]))
    g0, g1 = gate[:, 0:1], gate[:, 1:2]
    moe = jnp.concatenate([lo0 * g0 + lo1 * g1, hi0 * g0 + hi1 * g1], axis=1)
    x2 = x1_ref[...] + moe
    ms = jnp.mean(x2 * x2, axis=-1, keepdims=True)
    o_ref[...] = x2 * lax.rsqrt(ms + EPS) * ln_ref[...]


def _combine(dest3, x1, gate_rows, ln_final, ys):
    n, d = x1.shape
    tm = TM_COMBINE
    last = n // tm - 1
    return pl.pallas_call(
        _combine_kernel,
        grid=(n // tm,),
        in_specs=[
            pl.BlockSpec((1, 2, tm), lambda i: (i, 0, 0), memory_space=pltpu.SMEM),
            pl.BlockSpec((1, 2, tm), lambda i: (jnp.minimum(i + 1, last), 0, 0), memory_space=pltpu.SMEM),
            pl.BlockSpec((tm, d), lambda i: (i, 0)),
            pl.BlockSpec((tm, LANES), lambda i: (i, 0)),
            pl.BlockSpec((1, d), lambda i: (0, 0)),
            pl.BlockSpec(memory_space=pl.ANY),
        ],
        out_specs=pl.BlockSpec((tm, d), lambda i: (i, 0)),
        out_shape=jax.ShapeDtypeStruct((n, d), F32),
        scratch_shapes=[pltpu.VMEM((2, 2, tm * SUBLANES, LANES), U32), pltpu.SemaphoreType.DMA((2,))],
        compiler_params=_cparams(1),
        name="combine",
    )(dest3, dest3, x1, gate_rows, ln_final, ys)


def kernel(x, positions, ln_mix, w_in, pe_cmp_k, w_cmp_k1, w_cmp_k2, pe_cmp_v, w_cmp_v1, w_cmp_v2, w_pool, b_pool, pool_scale, gn_nsa, gn_pool, w_out, ln_moe, w_router_group, b_router_group, w_router_expert, b_router_expert, w_gate, w_up, w_down, ln_final):
    batch, seq, d = x.shape
    n = batch * seq
    depth = w_in.shape[0]

    inv_freq = ROPE_THETA ** (-jnp.arange(0, ROT_DIM, 2, dtype=F32) / ROT_DIM)
    ang_t = inv_freq[:, None] * positions.astype(F32).reshape(1, n)
    kblk = (jnp.arange(seq, dtype=I32)[:, None] // SEL_LEN == jnp.arange(LANES, dtype=I32)[None, :]).astype(BF16)

    x2d = x.reshape(n, d)
    assert depth == 1, "single-layer operation"
    for l in range(depth):
        gate_rows_pad = -(-N_GATES // SUBLANES) * SUBLANES
        qnt, qrt, kvc, kk, vt, gt, u = _inproj(x2d, ln_mix[l][None], ang_t, jnp.swapaxes(w_in, 1, 2),
                                               gate_rows_pad, batch, seq)

        half = CMP_LEN * HEAD_DIM // 2
        def stack_w1(w):
            return jnp.concatenate([w[:half], w[half:]], axis=1)
        w1 = jnp.stack([stack_w1(w_cmp_k1[l]), stack_w1(w_cmp_v1[l])]).astype(BF16)
        w2 = jnp.stack([w_cmp_k2[l], w_cmp_v2[l]]).astype(BF16)
        def pe_rows(pe):
            return jnp.pad(pe.reshape(2, half), ((0, SUBLANES - 2), (0, 0)))
        pe2 = jnp.stack([pe_rows(pe_cmp_k[l]), pe_rows(pe_cmp_v[l])]).astype(BF16)
        cmp = _compress(kvc, pe2, w1, w2)

        mix_nsa = _nsa(qnt, qrt, cmp, kk, vt, kblk, gt, gn_nsa[l][None])
        mix_pool = _pool(u.reshape(batch, seq, -1), w_pool[l].astype(BF16), b_pool[l][None],
                         pool_scale[l][None], gn_pool[l][None])

        ng, ne = N_EXPERT_GROUPS, EXPERTS_PER_GROUP
        w_r = jnp.concatenate([
            w_router_group[l].T, jnp.zeros((ne - ng, d), F32),
            jnp.transpose(w_router_expert[l], (0, 2, 1)).reshape(ng * ne, d),
            jnp.zeros((LANES - ne - ng * ne, d), F32)], axis=0).astype(BF16)
        b_r = jnp.concatenate([
            b_router_group[l], jnp.zeros((ne - ng,), F32), b_router_expert[l].reshape(-1),
            jnp.zeros((LANES - ne - ng * ne,), F32)])
        b_r = jnp.broadcast_to(b_r[:, None], (LANES, LANES))
        x1, h2, eid, gate = _outproj(mix_nsa, mix_pool, x2d, w_out, ln_moe[l][None], w_r, b_r)

        dest, meta = _dispatch(eid)
        cap = (2 * n // TM_MOE + N_EXPERTS) * TM_MOE
        def per_tile(tm):
            return jnp.transpose(dest[:2].reshape(2, n // tm, tm), (1, 0, 2))
        xs = _scatter(per_tile(TS_SCATTER), meta, h2, cap)
        ys = _experts(meta, xs, w_gate[l], w_up[l], w_down[l])
        dest_tok = per_tile(TM_COMBINE)
        gate_rows = jnp.pad(gate[:2].T, ((0, 0), (0, LANES - 2)))
        x2d = _combine(dest_tok, x1, gate_rows, ln_final[None], ys)
    return x2d.reshape(batch, seq, d)
```

```python
import jax
import jax.numpy as jnp
from jax import lax
from jax.experimental import pallas as pl
from jax.experimental.pallas import tpu as pltpu

F32 = jnp.float32
BF16 = jnp.bfloat16
I32 = jnp.int32

HEAD_DIM = 128
N_HEADS = 8
N_KV = 2
GQA = N_HEADS // N_KV
N_GATES = 3 * N_HEADS
ROT_DIM = HEAD_DIM // 4
ROPE_THETA = 500000.0
CMP_LEN = 32
CMP_STRIDE = 16
CMP_HIDDEN = 2 * HEAD_DIM
SEL_LEN = 64
SEL_TOPK = 16
N_LOCAL = 2
WINDOW = 512
POOL_SIZES = (2, 4, 8, 16)
N_EXPERT_GROUPS = 4
EXPERTS_PER_GROUP = 8
N_EXPERTS = N_EXPERT_GROUPS * EXPERTS_PER_GROUP
EPS = 1e-6
NEG = -1e30
BIG = 1e30
SCALE = HEAD_DIM ** -0.5
LOG2E = 1.4426950408889634
Q_SCALE = SCALE * LOG2E

LANES = 128
SUBLANES = 8
VMEM_LIMIT_BYTES = 56 * 1024 * 1024

TM_PROJ = 512
TQ = 256
TK = 256
SEL_MASK_BIAS = 2.0 ** 40
V_ONES_ROWS = 16
TM_MOE = 256
TS_SCATTER = 1024
TM_COMBINE = 512
WEIGHT_DMA_PRIORITIES = (0, 1, 1)
W_STAGE_ROWS = 256
W_SLOTS = 3
DMA_ISSUE_UNROLL = 8
T_RANK = 512


def _cparams(n_axes):
    return pltpu.CompilerParams(dimension_semantics=("arbitrary",) * n_axes, vmem_limit_bytes=VMEM_LIMIT_BYTES)


def _dot(a, b):
    return jnp.dot(a, b, preferred_element_type=F32)


def _dot_nt(a, b):
    return lax.dot_general(a, b, (((1,), (1,)), ((), ())), preferred_element_type=F32)


U32 = jnp.uint32
_HI16 = 0xFFFF0000


def _pack_bf16_halves(x):
    c = x.shape[1] // 2
    bits = lambda t: lax.bitcast_convert_type(t.astype(BF16).astype(F32), U32)
    return lax.shift_right_logical(bits(x[:, :c]), U32(16)) | (bits(x[:, c:]) & U32(_HI16))


def _store_token_tiles(ref, words):
    m, width = words.shape
    assert width == SUBLANES * LANES
    for j in range(SUBLANES):
        ref[pl.ds(j, m, stride=SUBLANES), :] = words[:, j * LANES:(j + 1) * LANES]


def _load_token_tiles(ref):
    m = ref.shape[0] // SUBLANES
    return jnp.concatenate([ref[pl.ds(j, m, stride=SUBLANES), :] for j in range(SUBLANES)], axis=1)


def _unpack_bf16_halves(w):
    lo = lax.bitcast_convert_type(lax.shift_left(w, U32(16)), F32)
    hi = lax.bitcast_convert_type(w & U32(_HI16), F32)
    return lo, hi


def _rope(t, cos, sin, axis):
    idx = lax.broadcasted_iota(I32, t.shape, axis)
    partner = jnp.where(idx < ROT_DIM // 2,
                        pltpu.roll(t, HEAD_DIM - ROT_DIM // 2, axis),
                        pltpu.roll(t, ROT_DIM // 2, axis))
    return t * cos + partner * sin


def _inproj_kernel(x_ref, ln_ref, ang_ref, wt_hbm,
                   qnt_ref, qrt_ref, kvc_ref, kk_ref, vt_ref, gt_ref, u_ref,
                   kc_scr, stage, wq_ref, wv_ref, wg_ref, wkc_ref, wk_ref, wu_ref, wsem):
    @pl.when(pl.program_id(0) == 0)
    def _():
        rc = stage.shape[1]
        q_w, kv_w = N_HEADS * HEAD_DIM, N_KV * HEAD_DIM
        o_kc, o_ks = q_w, q_w + 2 * kv_w
        o_vs, o_kw, o_vw, o_g = o_ks + kv_w, o_ks + 2 * kv_w, o_ks + 3 * kv_w, o_ks + 4 * kv_w
        o_u = o_g + N_GATES
        pieces = ([(r0, rc, wq_ref, r0) for r0 in range(0, q_w, rc)]
                  + [(o_kc + r0, rc, wkc_ref, r0) for r0 in range(0, 2 * kv_w, rc)]
                  + [(o_ks, kv_w, wk_ref, 0), (o_kw, kv_w, wk_ref, kv_w)]
                  + [(o_vs, kv_w, wv_ref, 0), (o_vw, kv_w, wv_ref, kv_w)]
                  + [(o_g, N_GATES, wg_ref, 0)]
                  + [(o_u + r0, rc, wu_ref, r0) for r0 in range(0, wu_ref.shape[0], rc)])
        assert all(n <= rc and src % SUBLANES == 0 for src, n, _, _ in pieces)
        wg_ref[...] = jnp.zeros(wg_ref.shape, BF16)

        def piece_copy(i, slot):
            src, rows, _, _ = pieces[i]
            return pltpu.make_async_copy(wt_hbm.at[0, pl.ds(src, rows)], stage.at[slot, pl.ds(0, rows)],
                                         wsem.at[slot])

        piece_copy(0, 0).start()
        for i, (src, rows, dst, dst_row) in enumerate(pieces):
            slot = i % 2
            if i + 1 < len(pieces):
                piece_copy(i + 1, 1 - slot).start()
            piece_copy(i, slot).wait()
            dst[dst_row:dst_row + rows, :] = stage[slot, :rows, :].astype(BF16)

    x = x_ref[...]
    ms = jnp.mean(x * x, axis=-1, keepdims=True)
    hb = (x * lax.rsqrt(ms + EPS) * ln_ref[...]).astype(BF16)
    chunk = 4 * HEAD_DIM

    ang = ang_ref[...]
    ca, sa = jnp.cos(ang), jnp.sin(ang)
    rest = (HEAD_DIM - ROT_DIM, ang.shape[1])
    cost = jnp.concatenate([ca, ca, jnp.ones(rest, F32)], axis=0)
    sint = jnp.concatenate([-sa, sa, jnp.zeros(rest, F32)], axis=0)
    cos, sin = cost.T, sint.T
    for ch in range(2):
        rt = _dot_nt(wq_ref[ch * chunk:(ch + 1) * chunk, :], hb)
        for j in range(4):
            t = rt[j * HEAD_DIM:(j + 1) * HEAD_DIM, :]
            qnt_ref[0, ch * 4 + j] = (t * Q_SCALE).astype(BF16)
            qrt_ref[0, ch * 4 + j] = (_rope(t, cost, sint, 0) * Q_SCALE).astype(BF16)
    rt = _dot_nt(wv_ref[...], hb)
    ones = jnp.ones((V_ONES_ROWS, TK), BF16)
    for j in range(4):
        for s in range(vt_ref.shape[2]):
            vt_ref[0, j, s] = jnp.concatenate(
                [rt[j * HEAD_DIM:(j + 1) * HEAD_DIM, s * TK:(s + 1) * TK].astype(BF16), ones], axis=0)
    gt_ref[...] = _dot_nt(wg_ref[...], hb)

    r = _dot_nt(hb, wkc_ref[...])
    groups = kvc_ref.shape[2]
    for j in range(4):
        kc_scr[j] = r[:, j * HEAD_DIM:(j + 1) * HEAD_DIM]
        for i in range(CMP_STRIDE):
            kvc_ref[0, j, :, i * HEAD_DIM:(i + 1) * HEAD_DIM] = (
                kc_scr[j, pl.ds(i, groups, stride=CMP_STRIDE), :].astype(BF16))
    r = _dot_nt(hb, wk_ref[...])
    for j in range(4):
        kk_ref[0, j] = _rope(r[:, j * HEAD_DIM:(j + 1) * HEAD_DIM], cos, sin, 1).astype(BF16)
    for ch in range(u_ref.shape[1] // chunk):
        u_ref[:, ch * chunk:(ch + 1) * chunk] = _dot_nt(hb, wu_ref[ch * chunk:(ch + 1) * chunk, :])


def _inproj(x2d, ln, ang_t, w_in_t, gate_rows, batch, seq):
    n, d = x2d.shape
    tm = TM_PROJ
    assert tm % TK == 0
    spb = seq // tm
    const = lambda i: (0, 0)
    q_w, kv_w = N_HEADS * HEAD_DIM, N_KV * HEAD_DIM
    pool_w = w_in_t.shape[1] - (q_w + 6 * kv_w + N_GATES)
    row_heads = lambda i: (i // spb, 0, i % spb, 0)
    col_heads = lambda i: (i // spb, 0, 0, i % spb)
    return pl.pallas_call(
        _inproj_kernel,
        grid=(n // tm,),
        in_specs=[
            pl.BlockSpec((tm, d), lambda i: (i, 0)),
            pl.BlockSpec((1, d), const),
            pl.BlockSpec((ang_t.shape[0], tm), lambda i: (0, i)),
            pl.BlockSpec(memory_space=pl.ANY),
        ],
        out_specs=[
            pl.BlockSpec((1, N_HEADS, HEAD_DIM, tm), col_heads),
            pl.BlockSpec((1, N_HEADS, HEAD_DIM, tm), col_heads),
            pl.BlockSpec((1, 4, tm // CMP_STRIDE, CMP_STRIDE * HEAD_DIM), row_heads),
            pl.BlockSpec((1, 4, tm, HEAD_DIM), row_heads),
            pl.BlockSpec((1, 4, tm // TK, HEAD_DIM + V_ONES_ROWS, TK), lambda i: (i // spb, 0, i % spb, 0, 0)),
            pl.BlockSpec((gate_rows, tm), lambda i: (0, i)),
            pl.BlockSpec((tm, pool_w), lambda i: (i, 0)),
        ],
        out_shape=[
            jax.ShapeDtypeStruct((batch, N_HEADS, HEAD_DIM, seq), BF16),
            jax.ShapeDtypeStruct((batch, N_HEADS, HEAD_DIM, seq), BF16),
            jax.ShapeDtypeStruct((batch, 4, seq // CMP_STRIDE, CMP_STRIDE * HEAD_DIM), BF16),
            jax.ShapeDtypeStruct((batch, 4, seq, HEAD_DIM), BF16),
            jax.ShapeDtypeStruct((batch, 4, seq // TK, HEAD_DIM + V_ONES_ROWS, TK), BF16),
            jax.ShapeDtypeStruct((gate_rows, n), F32),
            jax.ShapeDtypeStruct((n, pool_w), F32),
        ],
        scratch_shapes=[
            pltpu.VMEM((4, tm, HEAD_DIM), F32),
            pltpu.VMEM((2, W_STAGE_ROWS, d), F32),
            pltpu.VMEM((q_w, d), BF16), pltpu.VMEM((2 * kv_w, d), BF16), pltpu.VMEM((gate_rows, d), BF16),
            pltpu.VMEM((2 * kv_w, d), BF16), pltpu.VMEM((2 * kv_w, d), BF16), pltpu.VMEM((pool_w, d), BF16),
            pltpu.SemaphoreType.DMA((2,)),
        ],
        compiler_params=_cparams(1),
        name="inproj",
    )(x2d, ln, ang_t, w_in_t)


def _compress_kernel(t_ref, pe_ref, w1_ref, w2_ref, o_ref):
    t2 = t_ref[0, 0]
    w1 = w1_ref[0]
    a = _dot(t2, w1)
    pb = _dot(pe_ref[0], w1)
    bias = pb[0:1, :CMP_HIDDEN] + pb[1:2, CMP_HIDDEN:]
    nrow = a.shape[0]
    nxt = pltpu.roll(a[:, CMP_HIDDEN:], nrow - 1, 0)
    hid = jax.nn.gelu(a[:, :CMP_HIDDEN] + nxt + bias)
    out = _dot(hid.astype(BF16), w2_ref[0])
    row = lax.broadcasted_iota(I32, out.shape, 0)
    out = jnp.where(row < nrow - 1, out, 0.0)
    is_value = pl.program_id(1) >= N_KV
    o_ref[0, 0] = jnp.where(is_value, out.T, out).astype(BF16)


def _compress(kvc2, pe2, w1, w2):
    batch, four, nrow, width = kvc2.shape
    return pl.pallas_call(
        _compress_kernel,
        grid=(batch, four),
        in_specs=[
            pl.BlockSpec((1, 1, nrow, width), lambda b, j: (b, j, 0, 0)),
            pl.BlockSpec((1, SUBLANES, width), lambda b, j: (j // N_KV, 0, 0)),
            pl.BlockSpec((1, width, 2 * CMP_HIDDEN), lambda b, j: (j // N_KV, 0, 0)),
            pl.BlockSpec((1, CMP_HIDDEN, HEAD_DIM), lambda b, j: (j // N_KV, 0, 0)),
        ],
        out_specs=pl.BlockSpec((1, 1, nrow, HEAD_DIM), lambda b, j: (b, j, 0, 0)),
        out_shape=jax.ShapeDtypeStruct((batch, four, nrow, HEAD_DIM), BF16),
        compiler_params=_cparams(2),
        name="compress",
    )(kvc2, pe2, w1, w2)


def _nsa_kernel(qnt_ref, qrt_ref, cmp_ref, kk_ref, vt_ref, kblk_ref, gt_ref, gn_ref, o_ref,
                m_scr, acc_scr, qa_scr, ocmp_scr, s_scr, o_scr):
    qi = pl.program_id(1)
    tq = qnt_ref.shape[3]
    rows = GQA * tq
    n_cmp_pad = cmp_ref.shape[2]
    n_sel = kk_ref.shape[2] // SEL_LEN
    q0 = qi * tq

    m_scr[...] = jnp.full(m_scr.shape, NEG, F32)
    acc_scr[...] = jnp.zeros(acc_scr.shape, F32)

    def scores(item, slot):
        c, g, q_rows, k_tile, v_t, bias = item
        s_scr[slot] = _dot(k_tile, qa_scr[g, :q_rows, :])

    def softmax_pv(item, slot):
        c, g, q_rows, k_tile, v_t, bias = item
        s = s_scr[slot]
        if bias is not None:
            s = s + bias
        m_old = m_scr[c]
        m_new = jnp.maximum(m_old, jnp.max(s, axis=0, keepdims=True))
        alpha = jnp.exp2(m_old - m_new)
        p = jnp.exp2(s - m_new)
        acc_scr[c] = alpha * acc_scr[c] + _dot(v_t, p.astype(BF16))
        m_scr[c] = m_new

    def flash_out(c):
        return acc_scr[c, :HEAD_DIM, :] / acc_scr[c, HEAD_DIM:HEAD_DIM + 1, :]

    def tile_heads(a):
        return jnp.concatenate([a] * GQA, axis=1)

    kk_i = lax.broadcasted_iota(I32, (TK, tq), 0)
    qq_i = lax.broadcasted_iota(I32, (TK, tq), 1)
    causal_bias = tile_heads(jnp.where(kk_i <= qq_i, 0.0, NEG))
    band_bias = tile_heads(jnp.where(kk_i > qq_i, 0.0, NEG))
    n_win = WINDOW // TK

    for g in range(N_KV):
        qn4 = jnp.concatenate([qnt_ref[0, h] for h in range(g * GQA, (g + 1) * GQA)], axis=1)
        s = _dot(cmp_ref[0, g], qn4)
        n_idx = lax.broadcasted_iota(I32, (n_cmp_pad, rows), 0)
        t_idx = q0 + jnp.bitwise_and(lax.broadcasted_iota(I32, (n_cmp_pad, rows), 1), tq - 1)
        cmp_ok = n_idx * CMP_STRIDE + (CMP_LEN - 1) <= t_idx
        s = jnp.where(cmp_ok, s, NEG)
        m = jnp.max(s, axis=0, keepdims=True)
        e = jnp.where(cmp_ok, jnp.exp2(s - m), 0.0)
        l = jnp.sum(e, axis=0, keepdims=True)
        p = e * jnp.where(l > 0.0, 1.0 / l, 0.0)
        pb = p.astype(BF16)
        ocmp_scr[g] = _dot(cmp_ref[0, N_KV + g], pb)

        jn = lax.broadcasted_iota(I32, (n_sel, n_cmp_pad), 0) * SEL_LEN
        cn = lax.broadcasted_iota(I32, (n_sel, n_cmp_pad), 1) * CMP_STRIDE
        ov = jnp.clip(jnp.minimum(cn + CMP_LEN, jn + SEL_LEN) - jnp.maximum(cn, jn), 0, None)
        w_t = (ov.astype(F32) * (1.0 / CMP_LEN)).astype(BF16)
        imp4 = _dot(w_t, pb)
        imp = imp4[:, 0:tq]
        for r in range(1, GQA):
            imp = imp + imp4[:, r * tq:(r + 1) * tq]
        j_blk = lax.broadcasted_iota(I32, (n_sel, tq), 0)
        t_q = q0 + lax.broadcasted_iota(I32, (n_sel, tq), 1)
        sel_ok = j_blk * SEL_LEN <= t_q
        back = t_q // SEL_LEN - j_blk
        forced = (j_blk == 0) | ((back >= 0) & (back < N_LOCAL))
        val = jnp.where(sel_ok & forced, BIG, jnp.where(sel_ok, imp, -BIG))
        rank = jnp.zeros((n_sel, tq), F32)
        for i in range(n_sel):
            vi = val[i:i + 1, :]
            beats = (vi > val) | ((vi == val) & (j_blk > i))
            rank = rank + beats.astype(F32)
        keep = (rank < float(min(SEL_TOPK, n_sel))) & sel_ok
        bias_t = jnp.where(keep, 0.0, -SEL_MASK_BIAS)
        bias_t = jnp.concatenate([bias_t, jnp.zeros((HEAD_DIM - n_sel, tq), F32)], axis=0)
        qa_scr[g, :HEAD_DIM, :] = jnp.concatenate(
            [qrt_ref[0, h] for h in range(g * GQA, (g + 1) * GQA)], axis=1)
        qa_scr[g, HEAD_DIM:, :] = tile_heads(bias_t.astype(BF16))

    def sel_item(g, kj, bias):
        st = pl.multiple_of(kj * TK, TK)
        k_aug = jnp.concatenate([kk_ref[0, g, pl.ds(st, TK), :], kblk_ref[pl.ds(st, TK), :]], axis=1)
        return (g, g, 2 * HEAD_DIM, k_aug, vt_ref[0, g, kj], bias)

    def win_item(g, back):
        kj = qi - back
        st = pl.multiple_of(kj * TK, TK)
        bias = causal_bias if back == 0 else (band_bias if back == n_win else None)
        return (N_KV + g, g, HEAD_DIM, kk_ref[0, N_KV + g, pl.ds(st, TK), :], vt_ref[0, N_KV + g, kj], bias)

    assert N_KV == 2
    scores(sel_item(0, 0, None), 0)

    def sel_tile(kj):
        scores(sel_item(1, kj, None), 1)
        softmax_pv(sel_item(0, kj, None), 0)
        scores(sel_item(0, kj + 1, None), 0)
        softmax_pv(sel_item(1, kj, None), 1)

    def sel_pair(j, carry):
        sel_tile(2 * j)
        sel_tile(2 * j + 1)
        return carry

    lax.fori_loop(0, qi // 2, sel_pair, 0)

    @pl.when(qi % 2 == 1)
    def _():
        sel_tile(qi - 1)

    def run_tail(farthest):
        tail = ([sel_item(g, qi, causal_bias) for g in range(N_KV)]
                + [win_item(g, back) for back in range(farthest, -1, -1) for g in range(N_KV)])
        for i, item in enumerate(tail):
            if i + 1 < len(tail):
                scores(tail[i + 1], (i + 1) % 2)
            softmax_pv(item, i % 2)

    for nb in range(n_win):
        @pl.when(qi == nb)
        def _(nb=nb):
            run_tail(nb)

    @pl.when(qi >= n_win)
    def _():
        run_tail(n_win)

    gsig = jax.nn.sigmoid(gt_ref[...])
    for g in range(N_KV):
        o_cmp, o_sel, o_win = ocmp_scr[g], flash_out(g), flash_out(N_KV + g)
        for r in range(GQA):
            h = g * GQA + r
            sl = slice(r * tq, (r + 1) * tq)
            o_h = (gsig[3 * h:3 * h + 1, :] * o_cmp[:, sl]
                   + gsig[3 * h + 1:3 * h + 2, :] * o_sel[:, sl]
                   + gsig[3 * h + 2:3 * h + 3, :] * o_win[:, sl])
            o_scr[:, h * HEAD_DIM:(h + 1) * HEAD_DIM] = o_h.T

    o = o_scr[...]
    ms = jnp.mean(o * o, axis=-1, keepdims=True)
    o_ref[...] = (o * lax.rsqrt(ms + EPS) * gn_ref[...]).astype(BF16)


def _nsa(qnt, qrt, cmp, kk, vt, kblk, gt, gn):
    batch, _, _, seq = qnt.shape
    assert TQ == TK and WINDOW % TK == 0 and TQ & (TQ - 1) == 0
    nq = seq // TQ
    width = N_HEADS * HEAD_DIM
    return pl.pallas_call(
        _nsa_kernel,
        grid=(batch, nq),
        in_specs=[
            pl.BlockSpec((1, N_HEADS, HEAD_DIM, TQ), lambda b, i: (b, 0, 0, i)),
            pl.BlockSpec((1, N_HEADS, HEAD_DIM, TQ), lambda b, i: (b, 0, 0, i)),
            pl.BlockSpec((1,) + cmp.shape[1:], lambda b, i: (b, 0, 0, 0)),
            pl.BlockSpec((1,) + kk.shape[1:], lambda b, i: (b, 0, 0, 0)),
            pl.BlockSpec((1,) + vt.shape[1:], lambda b, i: (b, 0, 0, 0, 0)),
            pl.BlockSpec((seq, LANES), lambda b, i: (0, 0)),
            pl.BlockSpec((gt.shape[0], TQ), lambda b, i: (0, b * nq + i)),
            pl.BlockSpec((1, width), lambda b, i: (0, 0)),
        ],
        out_specs=pl.BlockSpec((TQ, width), lambda b, i: (b * nq + i, 0)),
        out_shape=jax.ShapeDtypeStruct((batch * seq, width), BF16),
        scratch_shapes=[
            pltpu.VMEM((2 * N_KV, 1, GQA * TQ), F32),
            pltpu.VMEM((2 * N_KV, vt.shape[3], GQA * TQ), F32),
            pltpu.VMEM((N_KV, 2 * HEAD_DIM, GQA * TQ), BF16),
            pltpu.VMEM((N_KV, HEAD_DIM, GQA * TQ), F32),
            pltpu.VMEM((2, TK, GQA * TQ), F32),
            pltpu.VMEM((TQ, width), F32),
        ],
        compiler_params=_cparams(2),
        name="nsa",
    )(qnt, qrt, cmp, kk, vt, kblk, gt, gn)


def _pool_group(cur, halo_rows, t1, window, w_mat, bias, scale):
    halo = halo_rows.shape[0]
    acc = jnp.concatenate([halo_rows, cur], axis=0)
    span = 1
    while span < window:
        acc = acc + jnp.concatenate([jnp.zeros((span, acc.shape[1]), F32), acc[:-span]], axis=0)
        span *= 2
    mean = acc[halo:] / jnp.minimum(t1, float(window))
    return (_dot((mean - cur).astype(BF16), w_mat) + bias) * scale


def _pool_kernel(u_ref, w_ref, b_ref, sc_ref, gn_ref, o_ref, y_scr, halo_scr):
    si = pl.program_id(1)
    tm = u_ref.shape[1]
    halo = halo_scr.shape[0]

    @pl.when(si == 0)
    def _():
        halo_scr[...] = jnp.zeros(halo_scr.shape, F32)

    t1 = (si * tm + 1 + lax.broadcasted_iota(I32, (tm, 1), 0)).astype(F32)
    cg = w_ref.shape[1]
    for gi, window in enumerate(POOL_SIZES):
        sl = slice(gi * cg, (gi + 1) * cg)
        cur = u_ref[0, :, sl]
        y_scr[:, sl] = _pool_group(cur, halo_scr[:, sl], t1, window, w_ref[gi], b_ref[:, sl], sc_ref[:, sl])
        halo_scr[:, sl] = cur[tm - halo:, :]
    y = y_scr[...]
    ms = jnp.mean(y * y, axis=-1, keepdims=True)
    o_ref[...] = (y * lax.rsqrt(ms + EPS) * gn_ref[...]).astype(BF16)


def _pool(u3, w_pool, b_pool, pool_scale, gn_pool):
    batch, seq, c = u3.shape
    tm = TM_PROJ
    ns = seq // tm
    vec = lambda b, i: (0, 0)
    return pl.pallas_call(
        _pool_kernel,
        grid=(batch, ns),
        in_specs=[
            pl.BlockSpec((1, tm, c), lambda b, i: (b, i, 0)),
            pl.BlockSpec(w_pool.shape, lambda b, i: (0, 0, 0)),
            pl.BlockSpec((1, c), vec),
            pl.BlockSpec((1, c), vec),
            pl.BlockSpec((1, c), vec),
        ],
        out_specs=pl.BlockSpec((tm, c), lambda b, i: (b * ns + i, 0)),
        out_shape=jax.ShapeDtypeStruct((batch * seq, c), BF16),
        scratch_shapes=[pltpu.VMEM((tm, c), F32), pltpu.VMEM((max(POOL_SIZES), c), F32)],
        compiler_params=_cparams(2),
        name="pool",
    )(u3, w_pool, b_pool, pool_scale, gn_pool)


def _outproj_kernel(mn_ref, mp_ref, x_ref, w_hbm, ln_ref, wr_ref, br_ref,
                    x1_ref, h2_ref, eid_ref, gate_ref, x1_scr, ssq_scr, stage, w_ref, wsem):
    tm, d = x_ref.shape
    half = mn_ref.shape[1]
    chunk = 512

    @pl.when(pl.program_id(0) == 0)
    def _():
        x1_scr[...] = jnp.zeros(x1_scr.shape, F32)
        ssq_scr[...] = jnp.zeros(ssq_scr.shape, F32)
        rc = stage.shape[1]
        n_chunks = w_ref.shape[0] // rc

        def chunk_copy(c, slot):
            return pltpu.make_async_copy(w_hbm.at[0, pl.ds(c * rc, rc)], stage.at[slot], wsem.at[slot])

        chunk_copy(0, 0).start()
        for c in range(n_chunks):
            slot = c % 2
            if c + 1 < n_chunks:
                chunk_copy(c + 1, 1 - slot).start()
            chunk_copy(c, slot).wait()
            w_ref[c * rc:(c + 1) * rc, :] = stage[slot].astype(BF16)

    h2 = x1_scr[...] * lax.rsqrt(ssq_scr[...] * (1.0 / d) + EPS) * ln_ref[...]
    _store_token_tiles(h2_ref, _pack_bf16_halves(h2))

    logits = _dot_nt(wr_ref[...], h2.astype(BF16)) + br_ref[:, 0:1]
    ng, ne = N_EXPERT_GROUPS, EXPERTS_PER_GROUP
    row = lax.broadcasted_iota(I32, (ne, tm), 0)
    lg = jnp.where(row < ng, logits[0:ne], NEG)
    mg = jnp.max(lg, axis=0, keepdims=True)
    g_sel = jnp.min(jnp.where(lg == mg, row, ne), axis=0, keepdims=True)
    p_g = 1.0 / jnp.sum(jnp.exp(lg - mg), axis=0, keepdims=True)
    le = jnp.zeros((ne, tm), F32)
    for g in range(ng):
        le = jnp.where(g_sel == g, logits[ne * (g + 1):ne * (g + 2)], le)
    ex = jnp.exp(le - jnp.max(le, axis=0, keepdims=True))
    pe = ex / jnp.sum(ex, axis=0, keepdims=True)
    p1 = jnp.max(pe, axis=0, keepdims=True)
    i1 = jnp.min(jnp.where(pe == p1, row, ne), axis=0, keepdims=True)
    rest = jnp.where(row == i1, -1.0, pe)
    p2 = jnp.max(rest, axis=0, keepdims=True)
    i2 = jnp.min(jnp.where(rest == p2, row, ne), axis=0, keepdims=True)
    psum = p1 + p2
    zero_i = jnp.zeros((ne, tm), I32)
    eid = jnp.where(row == 0, g_sel * ne + i1, jnp.where(row == 1, g_sel * ne + i2, zero_i))
    gate = jnp.where(row == 0, p_g * p1 / psum, jnp.where(row == 1, p_g * p2 / psum, 0.0))
    eid_ref[...] = eid
    gate_ref[...] = gate

    mn = mn_ref[...]
    mp = mp_ref[...]
    ssq = jnp.zeros((tm, 1), F32)
    for c in range(d // chunk):
        sl = slice(c * chunk, (c + 1) * chunk)
        x1 = x_ref[:, sl] + (_dot(mn, w_ref[:half, sl]) + _dot(mp, w_ref[half:, sl]))
        x1_ref[:, sl] = x1
        x1_scr[:, sl] = x1
        ssq = ssq + jnp.sum(x1 * x1, axis=-1, keepdims=True)
    ssq_scr[...] = ssq


def _outproj(mix_nsa, mix_pool, x2d, w_out, ln_moe, w_router_t, b_router):
    n, d = x2d.shape
    tm = TM_PROJ
    half = mix_nsa.shape[1]
    const = lambda i: (0, 0)
    last = n // tm - 1
    cur = lambda i: (jnp.minimum(i, last), 0)
    prev = lambda i: (jnp.maximum(i - 1, 0), 0)
    prev_t = lambda i: (0, jnp.maximum(i - 1, 0))
    return pl.pallas_call(
        _outproj_kernel,
        grid=(n // tm + 1,),
        in_specs=[
            pl.BlockSpec((tm, half), cur),
            pl.BlockSpec((tm, half), cur),
            pl.BlockSpec((tm, d), cur),
            pl.BlockSpec(memory_space=pl.ANY),
            pl.BlockSpec((1, d), const),
            pl.BlockSpec(w_router_t.shape, const),
            pl.BlockSpec(b_router.shape, const),
        ],
        out_specs=[
            pl.BlockSpec((tm, d), cur),
            pl.BlockSpec((tm * SUBLANES, LANES), prev),
            pl.BlockSpec((SUBLANES, tm), prev_t),
            pl.BlockSpec((SUBLANES, tm), prev_t),
        ],
        scratch_shapes=[pltpu.VMEM((tm, d), F32), pltpu.VMEM((tm, 1), F32),
                        pltpu.VMEM((2, W_STAGE_ROWS, d), F32), pltpu.VMEM(w_out.shape[1:], BF16),
                        pltpu.SemaphoreType.DMA((2,))],
        out_shape=[
            jax.ShapeDtypeStruct((n, d), F32),
            jax.ShapeDtypeStruct((n * SUBLANES, LANES), U32),
            jax.ShapeDtypeStruct((SUBLANES, n), I32),
            jax.ShapeDtypeStruct((SUBLANES, n), F32),
        ],
        compiler_params=_cparams(1),
        name="outproj",
    )(mix_nsa, mix_pool, x2d, w_out, ln_moe, w_router_t, b_router)


def _dispatch_kernel(eid_ref, dest_ref, meta_ref, rank_scr):
    n = eid_ref.shape[1]
    t = T_RANK
    row = lax.broadcasted_iota(I32, (N_EXPERTS, t), 0)
    before = (lax.broadcasted_iota(I32, (t, t), 0) < lax.broadcasted_iota(I32, (t, t), 1)).astype(BF16)
    carry = jnp.zeros((N_EXPERTS, 1), F32)
    for k in range(2):
        for j in range(n // t):
            sl = slice(j * t, (j + 1) * t)
            oh = row == eid_ref[k:k + 1, sl]
            ohf = oh.astype(F32)
            prior = _dot(ohf.astype(BF16), before) + carry
            rank_scr[k:k + 1, sl] = jnp.sum(jnp.where(oh, prior, 0.0), axis=0, keepdims=True)
            carry = carry + jnp.sum(ohf, axis=1, keepdims=True)
    counts = jnp.broadcast_to(carry, (N_EXPERTS, LANES))
    padded = jnp.floor((counts + (TM_MOE - 1.0)) * (1.0 / TM_MOE)) * TM_MOE
    ends = padded
    r_idx = lax.broadcasted_iota(I32, (N_EXPERTS, LANES), 0)
    sh = 1
    while sh < N_EXPERTS:
        ends = ends + jnp.where(r_idx >= sh, pltpu.roll(ends, sh, 0), 0.0)
        sh *= 2
    starts = (ends - padded)[:, 0:1]
    for k in range(2):
        for j in range(n // t):
            sl = slice(j * t, (j + 1) * t)
            oh = row == eid_ref[k:k + 1, sl]
            base = jnp.sum(jnp.where(oh, starts, 0.0), axis=0, keepdims=True)
            dest_ref[k:k + 1, sl] = (rank_scr[k:k + 1, sl] + base).astype(I32)
    for k in range(2, SUBLANES):
        dest_ref[k:k + 1, :] = jnp.zeros((1, n), I32)
    blk_start = lax.broadcasted_iota(I32, (N_EXPERTS, LANES), 1).astype(F32) * TM_MOE
    blk_expert = jnp.sum((ends <= blk_start).astype(F32), axis=0, keepdims=True)
    blk_expert = jnp.minimum(blk_expert, N_EXPERTS - 1.0)
    n_used = ends[N_EXPERTS - 1:N_EXPERTS, :] * (1.0 / TM_MOE)
    on_diag = r_idx == lax.broadcasted_iota(I32, (N_EXPERTS, LANES), 1)
    end_blk = jnp.sum(jnp.where(on_diag, ends, 0.0), axis=0, keepdims=True) * (1.0 / TM_MOE)
    n_blk = jnp.sum(jnp.where(on_diag, padded, 0.0), axis=0, keepdims=True) * (1.0 / TM_MOE)
    r8 = lax.broadcasted_iota(I32, (SUBLANES, LANES), 0)
    meta = jnp.where(r8 == 0, blk_expert,
                     jnp.where(r8 == 1, n_used, jnp.where(r8 == 2, end_blk, jnp.where(r8 == 3, n_blk, 0.0))))
    meta_ref[...] = meta.astype(I32)


def _dispatch(eid):
    n = eid.shape[1]
    return pl.pallas_call(
        _dispatch_kernel,
        out_shape=[jax.ShapeDtypeStruct((SUBLANES, n), I32),
                   jax.ShapeDtypeStruct((SUBLANES, LANES), I32)],
        scratch_shapes=[pltpu.VMEM((SUBLANES, n), F32)],
        compiler_params=pltpu.CompilerParams(vmem_limit_bytes=VMEM_LIMIT_BYTES),
        name="dispatch",
    )(eid)


def _row_copy(src_ref, src_row, dst_ref, dst_row, sem):
    tile = lambda ref, row: ref.at[pl.ds(pl.multiple_of(row * SUBLANES, SUBLANES), SUBLANES)]
    return pltpu.make_async_copy(tile(src_ref, src_row), tile(dst_ref, dst_row), sem)


def _scatter_kernel(dest_ref, meta_ref, h_ref, xs_ref, zero_buf, sem, zsem):
    tm = h_ref.shape[0] // SUBLANES
    blk = zero_buf.shape[0]

    @pl.when(pl.program_id(0) == 0)
    def _():
        zero_buf[...] = jnp.zeros(zero_buf.shape, zero_buf.dtype)
        n_used = meta_ref[1, 0]

        def zero_block(b):
            return pltpu.make_async_copy(zero_buf, xs_ref.at[pl.ds(pl.multiple_of(b * blk, blk), blk)], zsem)

        def sweep(act):
            def per_expert(e, c):
                @pl.when(meta_ref[3, e] > 0)
                def _():
                    act(zero_block(meta_ref[2, e] - 1))
                return c

            def per_tail(b, c):
                @pl.when(b >= n_used)
                def _():
                    act(zero_block(b))
                return c

            lax.fori_loop(0, N_EXPERTS, per_expert, 0)
            lax.fori_loop(0, xs_ref.shape[0] // blk, per_tail, 0)

        sweep(lambda cp: cp.start())
        sweep(lambda cp: cp.wait())

    def start(r, c):
        for k in range(2):
            _row_copy(h_ref, r, xs_ref, dest_ref[0, k, r], sem).start(priority=k)
        return c

    lax.fori_loop(0, tm, start, 0, unroll=DMA_ISSUE_UNROLL)
    for k in range(2):
        pltpu.make_async_copy(h_ref, xs_ref.at[pl.ds(0, tm * SUBLANES)], sem).wait()


def _scatter(dest3, meta, h2, cap):
    nsteps, _, tm = dest3.shape
    return pl.pallas_call(
        _scatter_kernel,
        grid=(nsteps,),
        in_specs=[
            pl.BlockSpec((1, 2, tm), lambda i: (i, 0, 0), memory_space=pltpu.SMEM),
            pl.BlockSpec(memory_space=pltpu.SMEM),
            pl.BlockSpec((tm * SUBLANES, LANES), lambda i: (i, 0)),
        ],
        out_specs=pl.BlockSpec(memory_space=pl.ANY),
        out_shape=jax.ShapeDtypeStruct((cap * SUBLANES, LANES), h2.dtype),
        scratch_shapes=[pltpu.VMEM((TM_MOE * SUBLANES, LANES), h2.dtype), pltpu.SemaphoreType.DMA(()),
                        pltpu.SemaphoreType.DMA(())],
        compiler_params=_cparams(1),
        name="scatter",
    )(dest3, meta, h2)


def _experts_kernel(meta_ref, xs_ref, wg_hbm, wu_hbm, wd_hbm, ys_ref,
                    wg_buf, wu_buf, wd_buf, wg16, wu16, wd16, elist_ref, ord_ref, sem):
    c = pl.program_id(0)
    n_used = meta_ref[1, 0]
    e = meta_ref[0, c]
    first = (c == 0) | (e != meta_ref[0, jnp.maximum(c - 1, 0)])

    def fetch(expert, slot):
        return [pltpu.make_async_copy(w.at[expert], buf.at[slot], sem.at[slot, i])
                for i, (w, buf) in enumerate(((wg_hbm, wg_buf), (wu_hbm, wu_buf), (wd_hbm, wd_buf)))]

    @pl.when(c == 0)
    def _():
        def scan(b, cnt):
            eb = meta_ref[0, b]
            take = ((b == 0) | (eb != meta_ref[0, jnp.maximum(b - 1, 0)])) & (b < n_used)

            @pl.when(take)
            def _():
                elist_ref[cnt] = eb

            return cnt + take.astype(I32)

        cnt = lax.fori_loop(0, pl.num_programs(0), scan, 0)
        ord_ref[0] = 0
        ord_ref[1] = cnt
        for k in range(W_SLOTS):
            @pl.when(k < cnt)
            def _(k=k):
                for cp, prio in zip(fetch(elist_ref[k], k), WEIGHT_DMA_PRIORITIES):
                    cp.start(priority=prio)

    @pl.when(first & (c < n_used))
    def _():
        @pl.when(c > 0)
        def _():
            ord_ref[0] = ord_ref[0] + 1
        for cp in fetch(e, ord_ref[0] % W_SLOTS):
            cp.wait()

    def mlp(weights):
        lo, hi = _unpack_bf16_halves(_load_token_tiles(xs_ref))
        lo, hi = lo.astype(BF16), hi.astype(BF16)
        half = lo.shape[1]
        wg, wu, wd = weights

        def proj(w):
            return _dot(lo, w[:half, :]) + _dot(hi, w[half:, :])

        act = (jax.nn.silu(proj(wg)) * proj(wu)).astype(BF16)
        _store_token_tiles(ys_ref, _pack_bf16_halves(_dot(act, wd)))

    @pl.when(first & (c < n_used))
    def _():
        k = ord_ref[0]
        slot = k % W_SLOTS
        cast = [buf[slot].astype(BF16) for buf in (wg_buf, wu_buf, wd_buf)]
        for ref16, w in zip((wg16, wu16, wd16), cast):
            ref16[...] = w
        mlp(cast)
        ahead = k + W_SLOTS

        @pl.when(ahead < ord_ref[1])
        def _():
            for cp, prio in zip(fetch(elist_ref[jnp.minimum(ahead, N_EXPERTS - 1)], slot), WEIGHT_DMA_PRIORITIES):
                cp.start(priority=prio)

    @pl.when(jnp.logical_not(first) & (c < n_used))
    def _():
        mlp((wg16[...], wu16[...], wd16[...]))

    @pl.when(c >= n_used)
    def _():
        ys_ref[...] = jnp.zeros(ys_ref.shape, U32)


def _experts(meta, xs, w_gate, w_up, w_down):
    rows = TM_MOE * SUBLANES
    nblk = xs.shape[0] // rows
    d, ff = w_gate.shape[1:]

    def row_map(c, meta):
        return (jnp.minimum(c, meta[1, 0] - 1), 0)

    hbm = pl.BlockSpec(memory_space=pl.ANY)
    return pl.pallas_call(
        _experts_kernel,
        grid_spec=pltpu.PrefetchScalarGridSpec(
            num_scalar_prefetch=1,
            grid=(nblk,),
            in_specs=[pl.BlockSpec((rows, LANES), row_map), hbm, hbm, hbm],
            out_specs=pl.BlockSpec((rows, LANES), lambda c, meta: (c, 0)),
            scratch_shapes=[
                pltpu.VMEM((W_SLOTS, d, ff), F32), pltpu.VMEM((W_SLOTS, d, ff), F32),
                pltpu.VMEM((W_SLOTS, ff, d), F32),
                pltpu.VMEM((d, ff), BF16), pltpu.VMEM((d, ff), BF16), pltpu.VMEM((ff, d), BF16),
                pltpu.SMEM((N_EXPERTS,), I32),
                pltpu.SMEM((2,), I32),
                pltpu.SemaphoreType.DMA((W_SLOTS, 3)),
            ],
        ),
        out_shape=jax.ShapeDtypeStruct(xs.shape, U32),
        compiler_params=_cparams(1),
        name="experts",
    )(meta, xs, w_gate, w_up, w_down)


def _combine_kernel(dest_ref, dest_next_ref, x1_ref, gate_ref, ln_ref, ys_ref, o_ref, ybuf, sem):
    i = pl.program_id(0)
    tm = x1_ref.shape[0]
    slot = i % 2

    def issue(d_ref, s):
        def start(r, c):
            for k in range(2):
                _row_copy(ys_ref, d_ref[0, k, r], ybuf.at[s, k], r, sem.at[s]).start(priority=k)
            return c

        lax.fori_loop(0, tm, start, 0, unroll=DMA_ISSUE_UNROLL)

    @pl.when(i == 0)
    def _():
        issue(dest_ref, 0)

    @pl.when(i + 1 < pl.num_programs(0))
    def _():
        issue(dest_next_ref, 1 - slot)

    for k in range(2):
        pltpu.make_async_copy(ys_ref.at[pl.ds(0, tm * SUBLANES)], ybuf.at[slot, k], sem.at[slot]).wait()
    gate = gate_ref[...]
    lo0, hi0 = _unpack_bf16_halves(_load_token_tiles(ybuf.at[slot, 0]))
    lo1, hi1 = _unpack_bf16_halves(_load_token_tiles(ybuf.at[slot, 1]))
    g0, g1 = gate[:, 0:1], gate[:, 1:2]
    moe = jnp.concatenate([lo0 * g0 + lo1 * g1, hi0 * g0 + hi1 * g1], axis=1)
    x2 = x1_ref[...] + moe
    ms = jnp.mean(x2 * x2, axis=-1, keepdims=True)
    o_ref[...] = x2 * lax.rsqrt(ms + EPS) * ln_ref[...]


def _combine(dest3, x1, gate_rows, ln_final, ys):
    n, d = x1.shape
    tm = TM_COMBINE
    last = n // tm - 1
    return pl.pallas_call(
        _combine_kernel,
        grid=(n // tm,),
        in_specs=[
            pl.BlockSpec((1, 2, tm), lambda i: (i, 0, 0), memory_space=pltpu.SMEM),
            pl.BlockSpec((1, 2, tm), lambda i: (jnp.minimum(i + 1, last), 0, 0), memory_space=pltpu.SMEM),
            pl.BlockSpec((tm, d), lambda i: (i, 0)),
            pl.BlockSpec((tm, LANES), lambda i: (i, 0)),
            pl.BlockSpec((1, d), lambda i: (0, 0)),
            pl.BlockSpec(memory_space=pl.ANY),
        ],
        out_specs=pl.BlockSpec((tm, d), lambda i: (i, 0)),
        out_shape=jax.ShapeDtypeStruct((n, d), F32),
        scratch_shapes=[pltpu.VMEM((2, 2, tm * SUBLANES, LANES), U32), pltpu.SemaphoreType.DMA((2,))],
        compiler_params=_cparams(1),
        name="combine",
    )(dest3, dest3, x1, gate_rows, ln_final, ys)


def kernel(x, positions, ln_mix, w_in, pe_cmp_k, w_cmp_k1, w_cmp_k2, pe_cmp_v, w_cmp_v1, w_cmp_v2, w_pool, b_pool, pool_scale, gn_nsa, gn_pool, w_out, ln_moe, w_router_group, b_router_group, w_router_expert, b_router_expert, w_gate, w_up, w_down, ln_final):
    batch, seq, d = x.shape
    n = batch * seq
    depth = w_in.shape[0]

    inv_freq = ROPE_THETA ** (-jnp.arange(0, ROT_DIM, 2, dtype=F32) / ROT_DIM)
    ang_t = inv_freq[:, None] * positions.astype(F32).reshape(1, n)
    kblk = (jnp.arange(seq, dtype=I32)[:, None] // SEL_LEN == jnp.arange(LANES, dtype=I32)[None, :]).astype(BF16)

    x2d = x.reshape(n, d)
    assert depth == 1, "single-layer operation"
    for l in range(depth):
        gate_rows_pad = -(-N_GATES // SUBLANES) * SUBLANES
        qnt, qrt, kvc, kk, vt, gt, u = _inproj(x2d, ln_mix[l][None], ang_t, jnp.swapaxes(w_in, 1, 2),
                                               gate_rows_pad, batch, seq)

        half = CMP_LEN * HEAD_DIM // 2
        def stack_w1(w):
            return jnp.concatenate([w[:half], w[half:]], axis=1)
        w1 = jnp.stack([stack_w1(w_cmp_k1[l]), stack_w1(w_cmp_v1[l])]).astype(BF16)
        w2 = jnp.stack([w_cmp_k2[l], w_cmp_v2[l]]).astype(BF16)
        def pe_rows(pe):
            return jnp.pad(pe.reshape(2, half), ((0, SUBLANES - 2), (0, 0)))
        pe2 = jnp.stack([pe_rows(pe_cmp_k[l]), pe_rows(pe_cmp_v[l])]).astype(BF16)
        cmp = _compress(kvc, pe2, w1, w2)

        mix_nsa = _nsa(qnt, qrt, cmp, kk, vt, kblk, gt, gn_nsa[l][None])
        mix_pool = _pool(u.reshape(batch, seq, -1), w_pool[l].astype(BF16), b_pool[l][None],
                         pool_scale[l][None], gn_pool[l][None])

        ng, ne = N_EXPERT_GROUPS, EXPERTS_PER_GROUP
        w_r = jnp.concatenate([
            w_router_group[l].T, jnp.zeros((ne - ng, d), F32),
            jnp.transpose(w_router_expert[l], (0, 2, 1)).reshape(ng * ne, d),
            jnp.zeros((LANES - ne - ng * ne, d), F32)], axis=0).astype(BF16)
        b_r = jnp.concatenate([
            b_router_group[l], jnp.zeros((ne - ng,), F32), b_router_expert[l].reshape(-1),
            jnp.zeros((LANES - ne - ng * ne,), F32)])
        b_r = jnp.broadcast_to(b_r[:, None], (LANES, LANES))
        x1, h2, eid, gate = _outproj(mix_nsa, mix_pool, x2d, w_out, ln_moe[l][None], w_r, b_r)

        dest, meta = _dispatch(eid)
        cap = (2 * n // TM_MOE + N_EXPERTS) * TM_MOE
        def per_tile(tm):
            return jnp.transpose(dest[:2].reshape(2, n // tm, tm), (1, 0, 2))
        xs = _scatter(per_tile(TS_SCATTER), meta, h2, cap)
        ys = _experts(meta, xs, w_gate[l], w_up[l], w_down[l])
        dest_tok = per_tile(TM_COMBINE)
        gate_rows = jnp.pad(gate[:2].T, ((0, 0), (0, LANES - 2)))
        x2d = _combine(dest_tok, x1, gate_rows, ln_final[None], ys)
    return x2d.reshape(batch, seq, d)
```

```python
import jax
import jax.numpy as jnp
from jax import lax
from jax.experimental import pallas as pl
from jax.experimental.pallas import tpu as pltpu

F32 = jnp.float32
BF16 = jnp.bfloat16
I32 = jnp.int32

HEAD_DIM = 128
N_HEADS = 8
N_KV = 2
GQA = N_HEADS // N_KV
N_GATES = 3 * N_HEADS
ROT_DIM = HEAD_DIM // 4
ROPE_THETA = 500000.0
CMP_LEN = 32
CMP_STRIDE = 16
CMP_HIDDEN = 2 * HEAD_DIM
SEL_LEN = 64
SEL_TOPK = 16
N_LOCAL = 2
WINDOW = 512
POOL_SIZES = (2, 4, 8, 16)
N_EXPERT_GROUPS = 4
EXPERTS_PER_GROUP = 8
N_EXPERTS = N_EXPERT_GROUPS * EXPERTS_PER_GROUP
EPS = 1e-6
NEG = -1e30
BIG = 1e30
SCALE = HEAD_DIM ** -0.5
LOG2E = 1.4426950408889634
Q_SCALE = SCALE * LOG2E

LANES = 128
SUBLANES = 8
VMEM_LIMIT_BYTES = 56 * 1024 * 1024

TM_PROJ = 512
TQ = 256
TK = 256
SEL_MASK_BIAS = 2.0 ** 40
V_ONES_ROWS = 16
TM_MOE = 256
TS_SCATTER = 2048
TM_COMBINE = 512
WEIGHT_DMA_PRIORITIES = (0, 1, 1)
W_STAGE_ROWS = 256
W_SLOTS = 3
DMA_ISSUE_UNROLL = 8
T_RANK = 512


def _cparams(n_axes):
    return pltpu.CompilerParams(dimension_semantics=("arbitrary",) * n_axes, vmem_limit_bytes=VMEM_LIMIT_BYTES)


def _dot(a, b):
    return jnp.dot(a, b, preferred_element_type=F32)


def _dot_nt(a, b):
    return lax.dot_general(a, b, (((1,), (1,)), ((), ())), preferred_element_type=F32)


U32 = jnp.uint32
_HI16 = 0xFFFF0000


def _pack_bf16_halves(x):
    c = x.shape[1] // 2
    bits = lambda t: lax.bitcast_convert_type(t.astype(BF16).astype(F32), U32)
    return lax.shift_right_logical(bits(x[:, :c]), U32(16)) | (bits(x[:, c:]) & U32(_HI16))


def _store_token_tiles(ref, words):
    m, width = words.shape
    assert width == SUBLANES * LANES
    for j in range(SUBLANES):
        ref[pl.ds(j, m, stride=SUBLANES), :] = words[:, j * LANES:(j + 1) * LANES]


def _load_token_tiles(ref):
    m = ref.shape[0] // SUBLANES
    return jnp.concatenate([ref[pl.ds(j, m, stride=SUBLANES), :] for j in range(SUBLANES)], axis=1)


def _unpack_bf16_halves(w):
    lo = lax.bitcast_convert_type(lax.shift_left(w, U32(16)), F32)
    hi = lax.bitcast_convert_type(w & U32(_HI16), F32)
    return lo, hi


def _rope(t, cos, sin, axis):
    idx = lax.broadcasted_iota(I32, t.shape, axis)
    partner = jnp.where(idx < ROT_DIM // 2,
                        pltpu.roll(t, HEAD_DIM - ROT_DIM // 2, axis),
                        pltpu.roll(t, ROT_DIM // 2, axis))
    return t * cos + partner * sin


def _inproj_kernel(x_ref, ln_ref, ang_ref, wt_hbm,
                   qnt_ref, qrt_ref, kvc_ref, kk_ref, vt_ref, gt_ref, u_ref,
                   kc_scr, stage, wq_ref, wv_ref, wg_ref, wkc_ref, wk_ref, wu_ref, wsem):
    @pl.when(pl.program_id(0) == 0)
    def _():
        rc = stage.shape[1]
        q_w, kv_w = N_HEADS * HEAD_DIM, N_KV * HEAD_DIM
        o_kc, o_ks = q_w, q_w + 2 * kv_w
        o_vs, o_kw, o_vw, o_g = o_ks + kv_w, o_ks + 2 * kv_w, o_ks + 3 * kv_w, o_ks + 4 * kv_w
        o_u = o_g + N_GATES
        pieces = ([(r0, rc, wq_ref, r0) for r0 in range(0, q_w, rc)]
                  + [(o_kc + r0, rc, wkc_ref, r0) for r0 in range(0, 2 * kv_w, rc)]
                  + [(o_ks, kv_w, wk_ref, 0), (o_kw, kv_w, wk_ref, kv_w)]
                  + [(o_vs, kv_w, wv_ref, 0), (o_vw, kv_w, wv_ref, kv_w)]
                  + [(o_g, N_GATES, wg_ref, 0)]
                  + [(o_u + r0, rc, wu_ref, r0) for r0 in range(0, wu_ref.shape[0], rc)])
        assert all(n <= rc and src % SUBLANES == 0 for src, n, _, _ in pieces)
        wg_ref[...] = jnp.zeros(wg_ref.shape, BF16)

        def piece_copy(i, slot):
            src, rows, _, _ = pieces[i]
            return pltpu.make_async_copy(wt_hbm.at[0, pl.ds(src, rows)], stage.at[slot, pl.ds(0, rows)],
                                         wsem.at[slot])

        piece_copy(0, 0).start()
        for i, (src, rows, dst, dst_row) in enumerate(pieces):
            slot = i % 2
            if i + 1 < len(pieces):
                piece_copy(i + 1, 1 - slot).start()
            piece_copy(i, slot).wait()
            dst[dst_row:dst_row + rows, :] = stage[slot, :rows, :].astype(BF16)

    x = x_ref[...]
    ms = jnp.mean(x * x, axis=-1, keepdims=True)
    hb = (x * lax.rsqrt(ms + EPS) * ln_ref[...]).astype(BF16)
    chunk = 4 * HEAD_DIM

    ang = ang_ref[...]
    ca, sa = jnp.cos(ang), jnp.sin(ang)
    rest = (HEAD_DIM - ROT_DIM, ang.shape[1])
    cost = jnp.concatenate([ca, ca, jnp.ones(rest, F32)], axis=0)
    sint = jnp.concatenate([-sa, sa, jnp.zeros(rest, F32)], axis=0)
    cos, sin = cost.T, sint.T
    for ch in range(2):
        rt = _dot_nt(wq_ref[ch * chunk:(ch + 1) * chunk, :], hb)
        for j in range(4):
            t = rt[j * HEAD_DIM:(j + 1) * HEAD_DIM, :]
            qnt_ref[0, ch * 4 + j] = (t * Q_SCALE).astype(BF16)
            qrt_ref[0, ch * 4 + j] = (_rope(t, cost, sint, 0) * Q_SCALE).astype(BF16)
    rt = _dot_nt(wv_ref[...], hb)
    ones = jnp.ones((V_ONES_ROWS, TK), BF16)
    for j in range(4):
        for s in range(vt_ref.shape[2]):
            vt_ref[0, j, s] = jnp.concatenate(
                [rt[j * HEAD_DIM:(j + 1) * HEAD_DIM, s * TK:(s + 1) * TK].astype(BF16), ones], axis=0)
    gt_ref[...] = _dot_nt(wg_ref[...], hb)

    r = _dot_nt(hb, wkc_ref[...])
    groups = kvc_ref.shape[2]
    for j in range(4):
        kc_scr[j] = r[:, j * HEAD_DIM:(j + 1) * HEAD_DIM]
        for i in range(CMP_STRIDE):
            kvc_ref[0, j, :, i * HEAD_DIM:(i + 1) * HEAD_DIM] = (
                kc_scr[j, pl.ds(i, groups, stride=CMP_STRIDE), :].astype(BF16))
    r = _dot_nt(hb, wk_ref[...])
    for j in range(4):
        kk_ref[0, j] = _rope(r[:, j * HEAD_DIM:(j + 1) * HEAD_DIM], cos, sin, 1).astype(BF16)
    for ch in range(u_ref.shape[1] // chunk):
        u_ref[:, ch * chunk:(ch + 1) * chunk] = _dot_nt(hb, wu_ref[ch * chunk:(ch + 1) * chunk, :])


def _inproj(x2d, ln, ang_t, w_in_t, gate_rows, batch, seq):
    n, d = x2d.shape
    tm = TM_PROJ
    assert tm % TK == 0
    spb = seq // tm
    const = lambda i: (0, 0)
    q_w, kv_w = N_HEADS * HEAD_DIM, N_KV * HEAD_DIM
    pool_w = w_in_t.shape[1] - (q_w + 6 * kv_w + N_GATES)
    row_heads = lambda i: (i // spb, 0, i % spb, 0)
    col_heads = lambda i: (i // spb, 0, 0, i % spb)
    return pl.pallas_call(
        _inproj_kernel,
        grid=(n // tm,),
        in_specs=[
            pl.BlockSpec((tm, d), lambda i: (i, 0)),
            pl.BlockSpec((1, d), const),
            pl.BlockSpec((ang_t.shape[0], tm), lambda i: (0, i)),
            pl.BlockSpec(memory_space=pl.ANY),
        ],
        out_specs=[
            pl.BlockSpec((1, N_HEADS, HEAD_DIM, tm), col_heads),
            pl.BlockSpec((1, N_HEADS, HEAD_DIM, tm), col_heads),
            pl.BlockSpec((1, 4, tm // CMP_STRIDE, CMP_STRIDE * HEAD_DIM), row_heads),
            pl.BlockSpec((1, 4, tm, HEAD_DIM), row_heads),
            pl.BlockSpec((1, 4, tm // TK, HEAD_DIM + V_ONES_ROWS, TK), lambda i: (i // spb, 0, i % spb, 0, 0)),
            pl.BlockSpec((gate_rows, tm), lambda i: (0, i)),
            pl.BlockSpec((tm, pool_w), lambda i: (i, 0)),
        ],
        out_shape=[
            jax.ShapeDtypeStruct((batch, N_HEADS, HEAD_DIM, seq), BF16),
            jax.ShapeDtypeStruct((batch, N_HEADS, HEAD_DIM, seq), BF16),
            jax.ShapeDtypeStruct((batch, 4, seq // CMP_STRIDE, CMP_STRIDE * HEAD_DIM), BF16),
            jax.ShapeDtypeStruct((batch, 4, seq, HEAD_DIM), BF16),
            jax.ShapeDtypeStruct((batch, 4, seq // TK, HEAD_DIM + V_ONES_ROWS, TK), BF16),
            jax.ShapeDtypeStruct((gate_rows, n), F32),
            jax.ShapeDtypeStruct((n, pool_w), F32),
        ],
        scratch_shapes=[
            pltpu.VMEM((4, tm, HEAD_DIM), F32),
            pltpu.VMEM((2, W_STAGE_ROWS, d), F32),
            pltpu.VMEM((q_w, d), BF16), pltpu.VMEM((2 * kv_w, d), BF16), pltpu.VMEM((gate_rows, d), BF16),
            pltpu.VMEM((2 * kv_w, d), BF16), pltpu.VMEM((2 * kv_w, d), BF16), pltpu.VMEM((pool_w, d), BF16),
            pltpu.SemaphoreType.DMA((2,)),
        ],
        compiler_params=_cparams(1),
        name="inproj",
    )(x2d, ln, ang_t, w_in_t)


def _compress_kernel(t_ref, pe_ref, w1_ref, w2_ref, o_ref):
    for j in range(t_ref.shape[1]):
        kind = j // N_KV
        t2 = t_ref[0, j]
        w1 = w1_ref[kind]
        a = _dot(t2, w1)
        pb = _dot(pe_ref[kind], w1)
        bias = pb[0:1, :CMP_HIDDEN] + pb[1:2, CMP_HIDDEN:]
        nrow = a.shape[0]
        nxt = pltpu.roll(a[:, CMP_HIDDEN:], nrow - 1, 0)
        hid = jax.nn.gelu(a[:, :CMP_HIDDEN] + nxt + bias)
        out = _dot(hid.astype(BF16), w2_ref[kind])
        row = lax.broadcasted_iota(I32, out.shape, 0)
        out = jnp.where(row < nrow - 1, out, 0.0)
        o_ref[0, j] = (out.T if kind else out).astype(BF16)


def _compress(kvc2, pe2, w1, w2):
    batch, four, nrow, width = kvc2.shape
    whole = lambda a: pl.BlockSpec(a.shape, lambda b: (0,) * a.ndim)
    return pl.pallas_call(
        _compress_kernel,
        grid=(batch,),
        in_specs=[pl.BlockSpec((1, four, nrow, width), lambda b: (b, 0, 0, 0)), whole(pe2), whole(w1), whole(w2)],
        out_specs=pl.BlockSpec((1, four, nrow, HEAD_DIM), lambda b: (b, 0, 0, 0)),
        out_shape=jax.ShapeDtypeStruct((batch, four, nrow, HEAD_DIM), BF16),
        compiler_params=_cparams(1),
        name="compress",
    )(kvc2, pe2, w1, w2)


def _nsa_kernel(qnt_ref, qrt_ref, cmp_ref, kk_ref, vt_ref, kblk_ref, gt_ref, gn_ref, o_ref,
                m_scr, acc_scr, qa_scr, ocmp_scr, s_scr, o_scr):
    qi = pl.program_id(1)
    tq = qnt_ref.shape[3]
    rows = GQA * tq
    n_cmp_pad = cmp_ref.shape[2]
    n_sel = kk_ref.shape[2] // SEL_LEN
    q0 = qi * tq

    m_scr[...] = jnp.full(m_scr.shape, NEG, F32)
    acc_scr[...] = jnp.zeros(acc_scr.shape, F32)

    def scores(item, slot):
        c, g, q_rows, k_tile, v_t, bias = item
        s_scr[slot] = _dot(k_tile, qa_scr[g, :q_rows, :])

    def softmax_pv(item, slot):
        c, g, q_rows, k_tile, v_t, bias = item
        s = s_scr[slot]
        if bias is not None:
            s = s + bias
        m_old = m_scr[c]
        m_new = jnp.maximum(m_old, jnp.max(s, axis=0, keepdims=True))
        alpha = jnp.exp2(m_old - m_new)
        p = jnp.exp2(s - m_new)
        acc_scr[c] = alpha * acc_scr[c] + _dot(v_t, p.astype(BF16))
        m_scr[c] = m_new

    def flash_out(c):
        return acc_scr[c, :HEAD_DIM, :] / acc_scr[c, HEAD_DIM:HEAD_DIM + 1, :]

    def tile_heads(a):
        return jnp.concatenate([a] * GQA, axis=1)

    kk_i = lax.broadcasted_iota(I32, (TK, tq), 0)
    qq_i = lax.broadcasted_iota(I32, (TK, tq), 1)
    causal_bias = tile_heads(jnp.where(kk_i <= qq_i, 0.0, NEG))
    band_bias = tile_heads(jnp.where(kk_i > qq_i, 0.0, NEG))
    n_win = WINDOW // TK

    for g in range(N_KV):
        qn4 = jnp.concatenate([qnt_ref[0, h] for h in range(g * GQA, (g + 1) * GQA)], axis=1)
        s = _dot(cmp_ref[0, g], qn4)
        n_idx = lax.broadcasted_iota(I32, (n_cmp_pad, rows), 0)
        t_idx = q0 + jnp.bitwise_and(lax.broadcasted_iota(I32, (n_cmp_pad, rows), 1), tq - 1)
        cmp_ok = n_idx * CMP_STRIDE + (CMP_LEN - 1) <= t_idx
        s = jnp.where(cmp_ok, s, NEG)
        m = jnp.max(s, axis=0, keepdims=True)
        e = jnp.where(cmp_ok, jnp.exp2(s - m), 0.0)
        l = jnp.sum(e, axis=0, keepdims=True)
        p = e * jnp.where(l > 0.0, 1.0 / l, 0.0)
        pb = p.astype(BF16)
        ocmp_scr[g] = _dot(cmp_ref[0, N_KV + g], pb)

        jn = lax.broadcasted_iota(I32, (n_sel, n_cmp_pad), 0) * SEL_LEN
        cn = lax.broadcasted_iota(I32, (n_sel, n_cmp_pad), 1) * CMP_STRIDE
        ov = jnp.clip(jnp.minimum(cn + CMP_LEN, jn + SEL_LEN) - jnp.maximum(cn, jn), 0, None)
        w_t = (ov.astype(F32) * (1.0 / CMP_LEN)).astype(BF16)
        imp4 = _dot(w_t, pb)
        imp = imp4[:, 0:tq]
        for r in range(1, GQA):
            imp = imp + imp4[:, r * tq:(r + 1) * tq]
        j_blk = lax.broadcasted_iota(I32, (n_sel, tq), 0)
        t_q = q0 + lax.broadcasted_iota(I32, (n_sel, tq), 1)
        sel_ok = j_blk * SEL_LEN <= t_q
        back = t_q // SEL_LEN - j_blk
        forced = (j_blk == 0) | ((back >= 0) & (back < N_LOCAL))
        val = jnp.where(sel_ok & forced, BIG, jnp.where(sel_ok, imp, -BIG))
        rank = jnp.zeros((n_sel, tq), F32)
        for i in range(n_sel):
            vi = val[i:i + 1, :]
            beats = (vi > val) | ((vi == val) & (j_blk > i))
            rank = rank + beats.astype(F32)
        keep = (rank < float(min(SEL_TOPK, n_sel))) & sel_ok
        bias_t = jnp.where(keep, 0.0, -SEL_MASK_BIAS)
        bias_t = jnp.concatenate([bias_t, jnp.zeros((HEAD_DIM - n_sel, tq), F32)], axis=0)
        qa_scr[g, :HEAD_DIM, :] = jnp.concatenate(
            [qrt_ref[0, h] for h in range(g * GQA, (g + 1) * GQA)], axis=1)
        qa_scr[g, HEAD_DIM:, :] = tile_heads(bias_t.astype(BF16))

    def sel_item(g, kj, bias):
        st = pl.multiple_of(kj * TK, TK)
        k_aug = jnp.concatenate([kk_ref[0, g, pl.ds(st, TK), :], kblk_ref[pl.ds(st, TK), :]], axis=1)
        return (g, g, 2 * HEAD_DIM, k_aug, vt_ref[0, g, kj], bias)

    def win_item(g, back):
        kj = qi - back
        st = pl.multiple_of(kj * TK, TK)
        bias = causal_bias if back == 0 else (band_bias if back == n_win else None)
        return (N_KV + g, g, HEAD_DIM, kk_ref[0, N_KV + g, pl.ds(st, TK), :], vt_ref[0, N_KV + g, kj], bias)

    assert N_KV == 2
    scores(sel_item(0, 0, None), 0)

    def sel_tile(kj):
        scores(sel_item(1, kj, None), 1)
        softmax_pv(sel_item(0, kj, None), 0)
        scores(sel_item(0, kj + 1, None), 0)
        softmax_pv(sel_item(1, kj, None), 1)

    def sel_pair(j, carry):
        sel_tile(2 * j)
        sel_tile(2 * j + 1)
        return carry

    lax.fori_loop(0, qi // 2, sel_pair, 0)

    @pl.when(qi % 2 == 1)
    def _():
        sel_tile(qi - 1)

    def run_tail(farthest):
        tail = ([sel_item(g, qi, causal_bias) for g in range(N_KV)]
                + [win_item(g, back) for back in range(farthest, -1, -1) for g in range(N_KV)])
        for i, item in enumerate(tail):
            if i + 1 < len(tail):
                scores(tail[i + 1], (i + 1) % 2)
            softmax_pv(item, i % 2)

    for nb in range(n_win):
        @pl.when(qi == nb)
        def _(nb=nb):
            run_tail(nb)

    @pl.when(qi >= n_win)
    def _():
        run_tail(n_win)

    gsig = jax.nn.sigmoid(gt_ref[...])
    for g in range(N_KV):
        o_cmp, o_sel, o_win = ocmp_scr[g], flash_out(g), flash_out(N_KV + g)
        for r in range(GQA):
            h = g * GQA + r
            sl = slice(r * tq, (r + 1) * tq)
            o_h = (gsig[3 * h:3 * h + 1, :] * o_cmp[:, sl]
                   + gsig[3 * h + 1:3 * h + 2, :] * o_sel[:, sl]
                   + gsig[3 * h + 2:3 * h + 3, :] * o_win[:, sl])
            o_scr[:, h * HEAD_DIM:(h + 1) * HEAD_DIM] = o_h.T

    o = o_scr[...]
    ms = jnp.mean(o * o, axis=-1, keepdims=True)
    o_ref[...] = (o * lax.rsqrt(ms + EPS) * gn_ref[...]).astype(BF16)


def _nsa(qnt, qrt, cmp, kk, vt, kblk, gt, gn):
    batch, _, _, seq = qnt.shape
    assert TQ == TK and WINDOW % TK == 0 and TQ & (TQ - 1) == 0
    nq = seq // TQ
    width = N_HEADS * HEAD_DIM
    return pl.pallas_call(
        _nsa_kernel,
        grid=(batch, nq),
        in_specs=[
            pl.BlockSpec((1, N_HEADS, HEAD_DIM, TQ), lambda b, i: (b, 0, 0, i)),
            pl.BlockSpec((1, N_HEADS, HEAD_DIM, TQ), lambda b, i: (b, 0, 0, i)),
            pl.BlockSpec((1,) + cmp.shape[1:], lambda b, i: (b, 0, 0, 0)),
            pl.BlockSpec((1,) + kk.shape[1:], lambda b, i: (b, 0, 0, 0)),
            pl.BlockSpec((1,) + vt.shape[1:], lambda b, i: (b, 0, 0, 0, 0)),
            pl.BlockSpec((seq, LANES), lambda b, i: (0, 0)),
            pl.BlockSpec((gt.shape[0], TQ), lambda b, i: (0, b * nq + i)),
            pl.BlockSpec((1, width), lambda b, i: (0, 0)),
        ],
        out_specs=pl.BlockSpec((TQ, width), lambda b, i: (b * nq + i, 0)),
        out_shape=jax.ShapeDtypeStruct((batch * seq, width), BF16),
        scratch_shapes=[
            pltpu.VMEM((2 * N_KV, 1, GQA * TQ), F32),
            pltpu.VMEM((2 * N_KV, vt.shape[3], GQA * TQ), F32),
            pltpu.VMEM((N_KV, 2 * HEAD_DIM, GQA * TQ), BF16),
            pltpu.VMEM((N_KV, HEAD_DIM, GQA * TQ), F32),
            pltpu.VMEM((2, TK, GQA * TQ), F32),
            pltpu.VMEM((TQ, width), F32),
        ],
        compiler_params=_cparams(2),
        name="nsa",
    )(qnt, qrt, cmp, kk, vt, kblk, gt, gn)


def _pool_group(cur, halo_rows, t1, window, w_mat, bias, scale):
    halo = halo_rows.shape[0]
    acc = jnp.concatenate([halo_rows, cur], axis=0)
    span = 1
    while span < window:
        acc = acc + jnp.concatenate([jnp.zeros((span, acc.shape[1]), F32), acc[:-span]], axis=0)
        span *= 2
    mean = acc[halo:] / jnp.minimum(t1, float(window))
    return (_dot((mean - cur).astype(BF16), w_mat) + bias) * scale


def _pool_kernel(u_ref, w_ref, b_ref, sc_ref, gn_ref, o_ref, y_scr, halo_scr):
    si = pl.program_id(1)
    tm = u_ref.shape[1]
    halo = halo_scr.shape[0]

    @pl.when(si == 0)
    def _():
        halo_scr[...] = jnp.zeros(halo_scr.shape, F32)

    t1 = (si * tm + 1 + lax.broadcasted_iota(I32, (tm, 1), 0)).astype(F32)
    cg = w_ref.shape[1]
    for gi, window in enumerate(POOL_SIZES):
        sl = slice(gi * cg, (gi + 1) * cg)
        cur = u_ref[0, :, sl]
        y_scr[:, sl] = _pool_group(cur, halo_scr[:, sl], t1, window, w_ref[gi], b_ref[:, sl], sc_ref[:, sl])
        halo_scr[:, sl] = cur[tm - halo:, :]
    y = y_scr[...]
    ms = jnp.mean(y * y, axis=-1, keepdims=True)
    o_ref[...] = (y * lax.rsqrt(ms + EPS) * gn_ref[...]).astype(BF16)


def _pool(u3, w_pool, b_pool, pool_scale, gn_pool):
    batch, seq, c = u3.shape
    tm = TM_PROJ
    ns = seq // tm
    vec = lambda b, i: (0, 0)
    return pl.pallas_call(
        _pool_kernel,
        grid=(batch, ns),
        in_specs=[
            pl.BlockSpec((1, tm, c), lambda b, i: (b, i, 0)),
            pl.BlockSpec(w_pool.shape, lambda b, i: (0, 0, 0)),
            pl.BlockSpec((1, c), vec),
            pl.BlockSpec((1, c), vec),
            pl.BlockSpec((1, c), vec),
        ],
        out_specs=pl.BlockSpec((tm, c), lambda b, i: (b * ns + i, 0)),
        out_shape=jax.ShapeDtypeStruct((batch * seq, c), BF16),
        scratch_shapes=[pltpu.VMEM((tm, c), F32), pltpu.VMEM((max(POOL_SIZES), c), F32)],
        compiler_params=_cparams(2),
        name="pool",
    )(u3, w_pool, b_pool, pool_scale, gn_pool)


def _outproj_kernel(mn_ref, mp_ref, x_ref, w_hbm, ln_ref, wr_ref, br_ref,
                    x1_ref, h2_ref, eid_ref, gate_ref, x1_scr, ssq_scr, stage, w_ref, wsem):
    tm, d = x_ref.shape
    half = mn_ref.shape[1]
    chunk = 512

    @pl.when(pl.program_id(0) == 0)
    def _():
        x1_scr[...] = jnp.zeros(x1_scr.shape, F32)
        ssq_scr[...] = jnp.zeros(ssq_scr.shape, F32)
        rc = stage.shape[1]
        n_chunks = w_ref.shape[0] // rc

        def chunk_copy(c, slot):
            return pltpu.make_async_copy(w_hbm.at[0, pl.ds(c * rc, rc)], stage.at[slot], wsem.at[slot])

        chunk_copy(0, 0).start()
        for c in range(n_chunks):
            slot = c % 2
            if c + 1 < n_chunks:
                chunk_copy(c + 1, 1 - slot).start()
            chunk_copy(c, slot).wait()
            w_ref[c * rc:(c + 1) * rc, :] = stage[slot].astype(BF16)

    h2 = x1_scr[...] * lax.rsqrt(ssq_scr[...] * (1.0 / d) + EPS) * ln_ref[...]
    _store_token_tiles(h2_ref, _pack_bf16_halves(h2))

    logits = _dot_nt(wr_ref[...], h2.astype(BF16)) + br_ref[:, 0:1]
    ng, ne = N_EXPERT_GROUPS, EXPERTS_PER_GROUP
    row = lax.broadcasted_iota(I32, (ne, tm), 0)
    lg = jnp.where(row < ng, logits[0:ne], NEG)
    mg = jnp.max(lg, axis=0, keepdims=True)
    g_sel = jnp.min(jnp.where(lg == mg, row, ne), axis=0, keepdims=True)
    p_g = 1.0 / jnp.sum(jnp.exp(lg - mg), axis=0, keepdims=True)
    le = jnp.zeros((ne, tm), F32)
    for g in range(ng):
        le = jnp.where(g_sel == g, logits[ne * (g + 1):ne * (g + 2)], le)
    ex = jnp.exp(le - jnp.max(le, axis=0, keepdims=True))
    pe = ex / jnp.sum(ex, axis=0, keepdims=True)
    p1 = jnp.max(pe, axis=0, keepdims=True)
    i1 = jnp.min(jnp.where(pe == p1, row, ne), axis=0, keepdims=True)
    rest = jnp.where(row == i1, -1.0, pe)
    p2 = jnp.max(rest, axis=0, keepdims=True)
    i2 = jnp.min(jnp.where(rest == p2, row, ne), axis=0, keepdims=True)
    psum = p1 + p2
    zero_i = jnp.zeros((ne, tm), I32)
    eid = jnp.where(row == 0, g_sel * ne + i1, jnp.where(row == 1, g_sel * ne + i2, zero_i))
    gate = jnp.where(row == 0, p_g * p1 / psum, jnp.where(row == 1, p_g * p2 / psum, 0.0))
    eid_ref[...] = eid
    gate_ref[...] = gate

    mn = mn_ref[...]
    mp = mp_ref[...]
    ssq = jnp.zeros((tm, 1), F32)
    for c in range(d // chunk):
        sl = slice(c * chunk, (c + 1) * chunk)
        x1 = x_ref[:, sl] + (_dot(mn, w_ref[:half, sl]) + _dot(mp, w_ref[half:, sl]))
        x1_ref[:, sl] = x1
        x1_scr[:, sl] = x1
        ssq = ssq + jnp.sum(x1 * x1, axis=-1, keepdims=True)
    ssq_scr[...] = ssq


def _outproj(mix_nsa, mix_pool, x2d, w_out, ln_moe, w_router_t, b_router):
    n, d = x2d.shape
    tm = TM_PROJ
    half = mix_nsa.shape[1]
    const = lambda i: (0, 0)
    last = n // tm - 1
    cur = lambda i: (jnp.minimum(i, last), 0)
    prev = lambda i: (jnp.maximum(i - 1, 0), 0)
    prev_t = lambda i: (0, jnp.maximum(i - 1, 0))
    return pl.pallas_call(
        _outproj_kernel,
        grid=(n // tm + 1,),
        in_specs=[
            pl.BlockSpec((tm, half), cur),
            pl.BlockSpec((tm, half), cur),
            pl.BlockSpec((tm, d), cur),
            pl.BlockSpec(memory_space=pl.ANY),
            pl.BlockSpec((1, d), const),
            pl.BlockSpec(w_router_t.shape, const),
            pl.BlockSpec(b_router.shape, const),
        ],
        out_specs=[
            pl.BlockSpec((tm, d), cur),
            pl.BlockSpec((tm * SUBLANES, LANES), prev),
            pl.BlockSpec((SUBLANES, tm), prev_t),
            pl.BlockSpec((SUBLANES, tm), prev_t),
        ],
        scratch_shapes=[pltpu.VMEM((tm, d), F32), pltpu.VMEM((tm, 1), F32),
                        pltpu.VMEM((2, W_STAGE_ROWS, d), F32), pltpu.VMEM(w_out.shape[1:], BF16),
                        pltpu.SemaphoreType.DMA((2,))],
        out_shape=[
            jax.ShapeDtypeStruct((n, d), F32),
            jax.ShapeDtypeStruct((n * SUBLANES, LANES), U32),
            jax.ShapeDtypeStruct((SUBLANES, n), I32),
            jax.ShapeDtypeStruct((SUBLANES, n), F32),
        ],
        compiler_params=_cparams(1),
        name="outproj",
    )(mix_nsa, mix_pool, x2d, w_out, ln_moe, w_router_t, b_router)


def _dispatch_kernel(eid_ref, dest_ref, meta_ref, rank_scr):
    n = eid_ref.shape[1]
    t = T_RANK
    row = lax.broadcasted_iota(I32, (N_EXPERTS, t), 0)
    before = (lax.broadcasted_iota(I32, (t, t), 0) < lax.broadcasted_iota(I32, (t, t), 1)).astype(BF16)
    carry = jnp.zeros((N_EXPERTS, 1), F32)
    for k in range(2):
        for j in range(n // t):
            sl = slice(j * t, (j + 1) * t)
            oh = row == eid_ref[k:k + 1, sl]
            ohf = oh.astype(F32)
            prior = _dot(ohf.astype(BF16), before) + carry
            rank_scr[k:k + 1, sl] = jnp.sum(jnp.where(oh, prior, 0.0), axis=0, keepdims=True)
            carry = carry + jnp.sum(ohf, axis=1, keepdims=True)
    counts = jnp.broadcast_to(carry, (N_EXPERTS, LANES))
    padded = jnp.floor((counts + (TM_MOE - 1.0)) * (1.0 / TM_MOE)) * TM_MOE
    ends = padded
    r_idx = lax.broadcasted_iota(I32, (N_EXPERTS, LANES), 0)
    sh = 1
    while sh < N_EXPERTS:
        ends = ends + jnp.where(r_idx >= sh, pltpu.roll(ends, sh, 0), 0.0)
        sh *= 2
    starts = (ends - padded)[:, 0:1]
    for k in range(2):
        for j in range(n // t):
            sl = slice(j * t, (j + 1) * t)
            oh = row == eid_ref[k:k + 1, sl]
            base = jnp.sum(jnp.where(oh, starts, 0.0), axis=0, keepdims=True)
            dest_ref[k:k + 1, sl] = (rank_scr[k:k + 1, sl] + base).astype(I32)
    for k in range(2, SUBLANES):
        dest_ref[k:k + 1, :] = jnp.zeros((1, n), I32)
    blk_start = lax.broadcasted_iota(I32, (N_EXPERTS, LANES), 1).astype(F32) * TM_MOE
    blk_expert = jnp.sum((ends <= blk_start).astype(F32), axis=0, keepdims=True)
    blk_expert = jnp.minimum(blk_expert, N_EXPERTS - 1.0)
    n_used = ends[N_EXPERTS - 1:N_EXPERTS, :] * (1.0 / TM_MOE)
    on_diag = r_idx == lax.broadcasted_iota(I32, (N_EXPERTS, LANES), 1)
    end_blk = jnp.sum(jnp.where(on_diag, ends, 0.0), axis=0, keepdims=True) * (1.0 / TM_MOE)
    n_blk = jnp.sum(jnp.where(on_diag, padded, 0.0), axis=0, keepdims=True) * (1.0 / TM_MOE)
    r8 = lax.broadcasted_iota(I32, (SUBLANES, LANES), 0)
    meta = jnp.where(r8 == 0, blk_expert,
                     jnp.where(r8 == 1, n_used, jnp.where(r8 == 2, end_blk, jnp.where(r8 == 3, n_blk, 0.0))))
    meta_ref[...] = meta.astype(I32)


def _dispatch(eid):
    n = eid.shape[1]
    return pl.pallas_call(
        _dispatch_kernel,
        out_shape=[jax.ShapeDtypeStruct((SUBLANES, n), I32),
                   jax.ShapeDtypeStruct((SUBLANES, LANES), I32)],
        scratch_shapes=[pltpu.VMEM((SUBLANES, n), F32)],
        compiler_params=pltpu.CompilerParams(vmem_limit_bytes=VMEM_LIMIT_BYTES),
        name="dispatch",
    )(eid)


def _row_copy(src_ref, src_row, dst_ref, dst_row, sem):
    tile = lambda ref, row: ref.at[pl.ds(pl.multiple_of(row * SUBLANES, SUBLANES), SUBLANES)]
    return pltpu.make_async_copy(tile(src_ref, src_row), tile(dst_ref, dst_row), sem)


def _scatter_kernel(dest_ref, meta_ref, h_ref, xs_ref, zero_buf, sem, zsem):
    tm = h_ref.shape[0] // SUBLANES
    blk = zero_buf.shape[0]

    @pl.when(pl.program_id(0) == 0)
    def _():
        zero_buf[...] = jnp.zeros(zero_buf.shape, zero_buf.dtype)
        n_used = meta_ref[1, 0]

        def zero_block(b):
            return pltpu.make_async_copy(zero_buf, xs_ref.at[pl.ds(pl.multiple_of(b * blk, blk), blk)], zsem)

        def sweep(act):
            def per_expert(e, c):
                @pl.when(meta_ref[3, e] > 0)
                def _():
                    act(zero_block(meta_ref[2, e] - 1))
                return c

            def per_tail(b, c):
                @pl.when(b >= n_used)
                def _():
                    act(zero_block(b))
                return c

            lax.fori_loop(0, N_EXPERTS, per_expert, 0)
            lax.fori_loop(0, xs_ref.shape[0] // blk, per_tail, 0)

        sweep(lambda cp: cp.start())
        sweep(lambda cp: cp.wait())

    def start(r, c):
        for k in range(2):
            _row_copy(h_ref, r, xs_ref, dest_ref[0, k, r], sem).start(priority=k)
        return c

    lax.fori_loop(0, tm, start, 0, unroll=DMA_ISSUE_UNROLL)
    for k in range(2):
        pltpu.make_async_copy(h_ref, xs_ref.at[pl.ds(0, tm * SUBLANES)], sem).wait()


def _scatter(dest3, meta, h2, cap):
    nsteps, _, tm = dest3.shape
    return pl.pallas_call(
        _scatter_kernel,
        grid=(nsteps,),
        in_specs=[
            pl.BlockSpec((1, 2, tm), lambda i: (i, 0, 0), memory_space=pltpu.SMEM),
            pl.BlockSpec(memory_space=pltpu.SMEM),
            pl.BlockSpec((tm * SUBLANES, LANES), lambda i: (i, 0)),
        ],
        out_specs=pl.BlockSpec(memory_space=pl.ANY),
        out_shape=jax.ShapeDtypeStruct((cap * SUBLANES, LANES), h2.dtype),
        scratch_shapes=[pltpu.VMEM((TM_MOE * SUBLANES, LANES), h2.dtype), pltpu.SemaphoreType.DMA(()),
                        pltpu.SemaphoreType.DMA(())],
        compiler_params=_cparams(1),
        name="scatter",
    )(dest3, meta, h2)


def _experts_kernel(meta_ref, xs_ref, wg_hbm, wu_hbm, wd_hbm, ys_ref,
                    wg_buf, wu_buf, wd_buf, wg16, wu16, wd16, elist_ref, ord_ref, sem):
    c = pl.program_id(0)
    n_used = meta_ref[1, 0]
    e = meta_ref[0, c]
    first = (c == 0) | (e != meta_ref[0, jnp.maximum(c - 1, 0)])

    def fetch(expert, slot):
        return [pltpu.make_async_copy(w.at[expert], buf.at[slot], sem.at[slot, i])
                for i, (w, buf) in enumerate(((wg_hbm, wg_buf), (wu_hbm, wu_buf), (wd_hbm, wd_buf)))]

    @pl.when(c == 0)
    def _():
        def scan(b, cnt):
            eb = meta_ref[0, b]
            take = ((b == 0) | (eb != meta_ref[0, jnp.maximum(b - 1, 0)])) & (b < n_used)

            @pl.when(take)
            def _():
                elist_ref[cnt] = eb

            return cnt + take.astype(I32)

        cnt = lax.fori_loop(0, pl.num_programs(0), scan, 0)
        ord_ref[0] = 0
        ord_ref[1] = cnt
        for k in range(W_SLOTS):
            @pl.when(k < cnt)
            def _(k=k):
                for cp, prio in zip(fetch(elist_ref[k], k), WEIGHT_DMA_PRIORITIES):
                    cp.start(priority=prio)

    @pl.when(first & (c < n_used))
    def _():
        @pl.when(c > 0)
        def _():
            ord_ref[0] = ord_ref[0] + 1
        for cp in fetch(e, ord_ref[0] % W_SLOTS):
            cp.wait()

    def mlp(weights):
        lo, hi = _unpack_bf16_halves(_load_token_tiles(xs_ref))
        lo, hi = lo.astype(BF16), hi.astype(BF16)
        half = lo.shape[1]
        wg, wu, wd = weights

        def proj(w):
            return _dot(lo, w[:half, :]) + _dot(hi, w[half:, :])

        act = (jax.nn.silu(proj(wg)) * proj(wu)).astype(BF16)
        _store_token_tiles(ys_ref, _pack_bf16_halves(_dot(act, wd)))

    @pl.when(first & (c < n_used))
    def _():
        k = ord_ref[0]
        slot = k % W_SLOTS
        cast = [buf[slot].astype(BF16) for buf in (wg_buf, wu_buf, wd_buf)]
        for ref16, w in zip((wg16, wu16, wd16), cast):
            ref16[...] = w
        mlp(cast)
        ahead = k + W_SLOTS

        @pl.when(ahead < ord_ref[1])
        def _():
            for cp, prio in zip(fetch(elist_ref[jnp.minimum(ahead, N_EXPERTS - 1)], slot), WEIGHT_DMA_PRIORITIES):
                cp.start(priority=prio)

    @pl.when(jnp.logical_not(first) & (c < n_used))
    def _():
        mlp((wg16[...], wu16[...], wd16[...]))

    @pl.when(c >= n_used)
    def _():
        ys_ref[...] = jnp.zeros(ys_ref.shape, U32)


def _experts(meta, xs, w_gate, w_up, w_down):
    rows = TM_MOE * SUBLANES
    nblk = xs.shape[0] // rows
    d, ff = w_gate.shape[1:]

    def row_map(c, meta):
        return (jnp.minimum(c, meta[1, 0] - 1), 0)

    hbm = pl.BlockSpec(memory_space=pl.ANY)
    return pl.pallas_call(
        _experts_kernel,
        grid_spec=pltpu.PrefetchScalarGridSpec(
            num_scalar_prefetch=1,
            grid=(nblk,),
            in_specs=[pl.BlockSpec((rows, LANES), row_map), hbm, hbm, hbm],
            out_specs=pl.BlockSpec((rows, LANES), lambda c, meta: (c, 0)),
            scratch_shapes=[
                pltpu.VMEM((W_SLOTS, d, ff), F32), pltpu.VMEM((W_SLOTS, d, ff), F32),
                pltpu.VMEM((W_SLOTS, ff, d), F32),
                pltpu.VMEM((d, ff), BF16), pltpu.VMEM((d, ff), BF16), pltpu.VMEM((ff, d), BF16),
                pltpu.SMEM((N_EXPERTS,), I32),
                pltpu.SMEM((2,), I32),
                pltpu.SemaphoreType.DMA((W_SLOTS, 3)),
            ],
        ),
        out_shape=jax.ShapeDtypeStruct(xs.shape, U32),
        compiler_params=_cparams(1),
        name="experts",
    )(meta, xs, w_gate, w_up, w_down)


def _combine_kernel(dest_ref, dest_next_ref, x1_ref, gate_ref, ln_ref, ys_ref, o_ref, ybuf, sem):
    i = pl.program_id(0)
    tm = x1_ref.shape[0]
    slot = i % 2

    def issue(d_ref, s):
        def start(r, c):
            for k in range(2):
                _row_copy(ys_ref, d_ref[0, k, r], ybuf.at[s, k], r, sem.at[s]).start(priority=k)
            return c

        lax.fori_loop(0, tm, start, 0, unroll=DMA_ISSUE_UNROLL)

    @pl.when(i == 0)
    def _():
        issue(dest_ref, 0)

    @pl.when(i + 1 < pl.num_programs(0))
    def _():
        issue(dest_next_ref, 1 - slot)

    for k in range(2):
        pltpu.make_async_copy(ys_ref.at[pl.ds(0, tm * SUBLANES)], ybuf.at[slot, k], sem.at[slot]).wait()
    gate = gate_ref[...]
    lo0, hi0 = _unpack_bf16_halves(_load_token_tiles(ybuf.at[slot, 0]))
    lo1, hi1 = _unpack_bf16_halves(_load_token_tiles(ybuf.at[slot, 1]))
    g0, g1 = gate[:, 0:1], gate[:, 1:2]
    moe = jnp.concatenate([lo0 * g0 + lo1 * g1, hi0 * g0 + hi1 * g1], axis=1)
    x2 = x1_ref[...] + moe
    ms = jnp.mean(x2 * x2, axis=-1, keepdims=True)
    o_ref[...] = x2 * lax.rsqrt(ms + EPS) * ln_ref[...]


def _combine(dest3, x1, gate_rows, ln_final, ys):
    n, d = x1.shape
    tm = TM_COMBINE
    last = n // tm - 1
    return pl.pallas_call(
        _combine_kernel,
        grid=(n // tm,),
        in_specs=[
            pl.BlockSpec((1, 2, tm), lambda i: (i, 0, 0), memory_space=pltpu.SMEM),
            pl.BlockSpec((1, 2, tm), lambda i: (jnp.minimum(i + 1, last), 0, 0), memory_space=pltpu.SMEM),
            pl.BlockSpec((tm, d), lambda i: (i, 0)),
            pl.BlockSpec((tm, LANES), lambda i: (i, 0)),
            pl.BlockSpec((1, d), lambda i: (0, 0)),
            pl.BlockSpec(memory_space=pl.ANY),
        ],
        out_specs=pl.BlockSpec((tm, d), lambda i: (i, 0)),
        out_shape=jax.ShapeDtypeStruct((n, d), F32),
        scratch_shapes=[pltpu.VMEM((2, 2, tm * SUBLANES, LANES), U32), pltpu.SemaphoreType.DMA((2,))],
        compiler_params=_cparams(1),
        name="combine",
    )(dest3, dest3, x1, gate_rows, ln_final, ys)


def kernel(x, positions, ln_mix, w_in, pe_cmp_k, w_cmp_k1, w_cmp_k2, pe_cmp_v, w_cmp_v1, w_cmp_v2, w_pool, b_pool, pool_scale, gn_nsa, gn_pool, w_out, ln_moe, w_router_group, b_router_group, w_router_expert, b_router_expert, w_gate, w_up, w_down, ln_final):
    batch, seq, d = x.shape
    n = batch * seq
    depth = w_in.shape[0]

    inv_freq = ROPE_THETA ** (-jnp.arange(0, ROT_DIM, 2, dtype=F32) / ROT_DIM)
    ang_t = inv_freq[:, None] * positions.astype(F32).reshape(1, n)
    kblk = (jnp.arange(seq, dtype=I32)[:, None] // SEL_LEN == jnp.arange(LANES, dtype=I32)[None, :]).astype(BF16)

    x2d = x.reshape(n, d)
    assert depth == 1, "single-layer operation"
    for l in range(depth):
        gate_rows_pad = -(-N_GATES // SUBLANES) * SUBLANES
        qnt, qrt, kvc, kk, vt, gt, u = _inproj(x2d, ln_mix[l][None], ang_t, jnp.swapaxes(w_in, 1, 2),
                                               gate_rows_pad, batch, seq)

        half = CMP_LEN * HEAD_DIM // 2
        def stack_w1(w):
            return jnp.concatenate([w[:half], w[half:]], axis=1)
        w1 = jnp.stack([stack_w1(w_cmp_k1[l]), stack_w1(w_cmp_v1[l])]).astype(BF16)
        w2 = jnp.stack([w_cmp_k2[l], w_cmp_v2[l]]).astype(BF16)
        def pe_rows(pe):
            return jnp.pad(pe.reshape(2, half), ((0, SUBLANES - 2), (0, 0)))
        pe2 = jnp.stack([pe_rows(pe_cmp_k[l]), pe_rows(pe_cmp_v[l])]).astype(BF16)
        cmp = _compress(kvc, pe2, w1, w2)

        mix_nsa = _nsa(qnt, qrt, cmp, kk, vt, kblk, gt, gn_nsa[l][None])
        mix_pool = _pool(u.reshape(batch, seq, -1), w_pool[l].astype(BF16), b_pool[l][None],
                         pool_scale[l][None], gn_pool[l][None])

        ng, ne = N_EXPERT_GROUPS, EXPERTS_PER_GROUP
        w_r = jnp.concatenate([
            w_router_group[l].T, jnp.zeros((ne - ng, d), F32),
            jnp.transpose(w_router_expert[l], (0, 2, 1)).reshape(ng * ne, d),
            jnp.zeros((LANES - ne - ng * ne, d), F32)], axis=0).astype(BF16)
        b_r = jnp.concatenate([
            b_router_group[l], jnp.zeros((ne - ng,), F32), b_router_expert[l].reshape(-1),
            jnp.zeros((LANES - ne - ng * ne,), F32)])
        b_r = jnp.broadcast_to(b_r[:, None], (LANES, LANES))
        x1, h2, eid, gate = _outproj(mix_nsa, mix_pool, x2d, w_out, ln_moe[l][None], w_r, b_r)

        dest, meta = _dispatch(eid)
        cap = (2 * n // TM_MOE + N_EXPERTS) * TM_MOE
        def per_tile(tm):
            return jnp.transpose(dest[:2].reshape(2, n // tm, tm), (1, 0, 2))
        xs = _scatter(per_tile(TS_SCATTER), meta, h2, cap)
        ys = _experts(meta, xs, w_gate[l], w_up[l], w_down[l])
        dest_tok = per_tile(TM_COMBINE)
        gate_rows = jnp.pad(gate[:2].T, ((0, 0), (0, LANES - 2)))
        x2d = _combine(dest_tok, x1, gate_rows, ln_final[None], ys)
    return x2d.reshape(batch, seq, d)
```

```python
import jax
import jax.numpy as jnp
from jax import lax
from jax.experimental import pallas as pl
from jax.experimental.pallas import tpu as pltpu

F32 = jnp.float32
BF16 = jnp.bfloat16
I32 = jnp.int32

HEAD_DIM = 128
N_HEADS = 8
N_KV = 2
GQA = N_HEADS // N_KV
N_GATES = 3 * N_HEADS
ROT_DIM = HEAD_DIM // 4
ROPE_THETA = 500000.0
CMP_LEN = 32
CMP_STRIDE = 16
CMP_HIDDEN = 2 * HEAD_DIM
SEL_LEN = 64
SEL_TOPK = 16
N_LOCAL = 2
WINDOW = 512
POOL_SIZES = (2, 4, 8, 16)
N_EXPERT_GROUPS = 4
EXPERTS_PER_GROUP = 8
N_EXPERTS = N_EXPERT_GROUPS * EXPERTS_PER_GROUP
EPS = 1e-6
NEG = -1e30
BIG = 1e30
SCALE = HEAD_DIM ** -0.5
LOG2E = 1.4426950408889634
Q_SCALE = SCALE * LOG2E

LANES = 128
SUBLANES = 8
VMEM_LIMIT_BYTES = 56 * 1024 * 1024

TM_PROJ = 512
TQ = 256
TK = 256
SEL_MASK_BIAS = 2.0 ** 40
V_ONES_ROWS = 16
TM_MOE = 256
TS_SCATTER = 2048
TM_COMBINE = 512
WEIGHT_DMA_PRIORITIES = (0, 1, 1)
W_STAGE_ROWS = 256
W_SLOTS = 4
W_CAST_CHUNK = 128 * SUBLANES * LANES
DMA_ISSUE_UNROLL = 8
T_RANK = 512


def _cparams(n_axes):
    return pltpu.CompilerParams(dimension_semantics=("arbitrary",) * n_axes, vmem_limit_bytes=VMEM_LIMIT_BYTES)


def _dot(a, b):
    return jnp.dot(a, b, preferred_element_type=F32)


def _dot_nt(a, b):
    return lax.dot_general(a, b, (((1,), (1,)), ((), ())), preferred_element_type=F32)


U32 = jnp.uint32
_HI16 = 0xFFFF0000


def _pack_bf16_halves(x):
    c = x.shape[1] // 2
    bits = lambda t: lax.bitcast_convert_type(t.astype(BF16).astype(F32), U32)
    return lax.shift_right_logical(bits(x[:, :c]), U32(16)) | (bits(x[:, c:]) & U32(_HI16))


def _store_token_tiles(ref, words):
    m, width = words.shape
    assert width == SUBLANES * LANES
    for j in range(SUBLANES):
        ref[pl.ds(j, m, stride=SUBLANES), :] = words[:, j * LANES:(j + 1) * LANES]


def _load_token_tiles(ref):
    m = ref.shape[0] // SUBLANES
    return jnp.concatenate([ref[pl.ds(j, m, stride=SUBLANES), :] for j in range(SUBLANES)], axis=1)


def _unpack_bf16_halves(w):
    lo = lax.bitcast_convert_type(lax.shift_left(w, U32(16)), F32)
    hi = lax.bitcast_convert_type(w & U32(_HI16), F32)
    return lo, hi


def _rope(t, cos, sin, axis):
    idx = lax.broadcasted_iota(I32, t.shape, axis)
    partner = jnp.where(idx < ROT_DIM // 2,
                        pltpu.roll(t, HEAD_DIM - ROT_DIM // 2, axis),
                        pltpu.roll(t, ROT_DIM // 2, axis))
    return t * cos + partner * sin


def _inproj_kernel(x_ref, ln_ref, ang_ref, wt_hbm,
                   qnt_ref, qrt_ref, kvc_ref, kk_ref, vt_ref, gt_ref, u_ref,
                   kc_scr, stage, wq_ref, wv_ref, wg_ref, wkc_ref, wk_ref, wu_ref, wsem):
    @pl.when(pl.program_id(0) == 0)
    def _():
        rc = stage.shape[1]
        q_w, kv_w = N_HEADS * HEAD_DIM, N_KV * HEAD_DIM
        o_kc, o_ks = q_w, q_w + 2 * kv_w
        o_vs, o_kw, o_vw, o_g = o_ks + kv_w, o_ks + 2 * kv_w, o_ks + 3 * kv_w, o_ks + 4 * kv_w
        o_u = o_g + N_GATES
        pieces = ([(r0, rc, wq_ref, r0) for r0 in range(0, q_w, rc)]
                  + [(o_kc + r0, rc, wkc_ref, r0) for r0 in range(0, 2 * kv_w, rc)]
                  + [(o_ks, kv_w, wk_ref, 0), (o_kw, kv_w, wk_ref, kv_w)]
                  + [(o_vs, kv_w, wv_ref, 0), (o_vw, kv_w, wv_ref, kv_w)]
                  + [(o_g, N_GATES, wg_ref, 0)]
                  + [(o_u + r0, rc, wu_ref, r0) for r0 in range(0, wu_ref.shape[0], rc)])
        assert all(n <= rc and src % SUBLANES == 0 for src, n, _, _ in pieces)
        wg_ref[...] = jnp.zeros(wg_ref.shape, BF16)

        def piece_copy(i, slot):
            src, rows, _, _ = pieces[i]
            return pltpu.make_async_copy(wt_hbm.at[0, pl.ds(src, rows)], stage.at[slot, pl.ds(0, rows)],
                                         wsem.at[slot])

        piece_copy(0, 0).start()
        for i, (src, rows, dst, dst_row) in enumerate(pieces):
            slot = i % 2
            if i + 1 < len(pieces):
                piece_copy(i + 1, 1 - slot).start()
            piece_copy(i, slot).wait()
            dst[dst_row:dst_row + rows, :] = stage[slot, :rows, :].astype(BF16)

    x = x_ref[...]
    ms = jnp.mean(x * x, axis=-1, keepdims=True)
    hb = (x * lax.rsqrt(ms + EPS) * ln_ref[...]).astype(BF16)
    chunk = 4 * HEAD_DIM

    ang = ang_ref[...]
    ca, sa = jnp.cos(ang), jnp.sin(ang)
    rest = (HEAD_DIM - ROT_DIM, ang.shape[1])
    cost = jnp.concatenate([ca, ca, jnp.ones(rest, F32)], axis=0)
    sint = jnp.concatenate([-sa, sa, jnp.zeros(rest, F32)], axis=0)
    cos, sin = cost.T, sint.T
    for ch in range(2):
        rt = _dot_nt(wq_ref[ch * chunk:(ch + 1) * chunk, :], hb)
        for j in range(4):
            t = rt[j * HEAD_DIM:(j + 1) * HEAD_DIM, :]
            qnt_ref[0, ch * 4 + j] = (t * Q_SCALE).astype(BF16)
            qrt_ref[0, ch * 4 + j] = (_rope(t, cost, sint, 0) * Q_SCALE).astype(BF16)
    rt = _dot_nt(wv_ref[...], hb)
    ones = jnp.ones((V_ONES_ROWS, TK), BF16)
    for j in range(4):
        for s in range(vt_ref.shape[2]):
            vt_ref[0, j, s] = jnp.concatenate(
                [rt[j * HEAD_DIM:(j + 1) * HEAD_DIM, s * TK:(s + 1) * TK].astype(BF16), ones], axis=0)
    gt_ref[...] = _dot_nt(wg_ref[...], hb)

    r = _dot_nt(hb, wkc_ref[...])
    groups = kvc_ref.shape[2]
    for j in range(4):
        kc_scr[j] = r[:, j * HEAD_DIM:(j + 1) * HEAD_DIM]
        for i in range(CMP_STRIDE):
            kvc_ref[0, j, :, i * HEAD_DIM:(i + 1) * HEAD_DIM] = (
                kc_scr[j, pl.ds(i, groups, stride=CMP_STRIDE), :].astype(BF16))
    r = _dot_nt(hb, wk_ref[...])
    for j in range(4):
        kk_ref[0, j] = _rope(r[:, j * HEAD_DIM:(j + 1) * HEAD_DIM], cos, sin, 1).astype(BF16)
    for ch in range(u_ref.shape[1] // chunk):
        u_ref[:, ch * chunk:(ch + 1) * chunk] = _dot_nt(hb, wu_ref[ch * chunk:(ch + 1) * chunk, :])


def _inproj(x2d, ln, ang_t, w_in_t, gate_rows, batch, seq):
    n, d = x2d.shape
    tm = TM_PROJ
    assert tm % TK == 0
    spb = seq // tm
    const = lambda i: (0, 0)
    q_w, kv_w = N_HEADS * HEAD_DIM, N_KV * HEAD_DIM
    pool_w = w_in_t.shape[1] - (q_w + 6 * kv_w + N_GATES)
    row_heads = lambda i: (i // spb, 0, i % spb, 0)
    col_heads = lambda i: (i // spb, 0, 0, i % spb)
    return pl.pallas_call(
        _inproj_kernel,
        grid=(n // tm,),
        in_specs=[
            pl.BlockSpec((tm, d), lambda i: (i, 0)),
            pl.BlockSpec((1, d), const),
            pl.BlockSpec((ang_t.shape[0], tm), lambda i: (0, i)),
            pl.BlockSpec(memory_space=pl.ANY),
        ],
        out_specs=[
            pl.BlockSpec((1, N_HEADS, HEAD_DIM, tm), col_heads),
            pl.BlockSpec((1, N_HEADS, HEAD_DIM, tm), col_heads),
            pl.BlockSpec((1, 4, tm // CMP_STRIDE, CMP_STRIDE * HEAD_DIM), row_heads),
            pl.BlockSpec((1, 4, tm, HEAD_DIM), row_heads),
            pl.BlockSpec((1, 4, tm // TK, HEAD_DIM + V_ONES_ROWS, TK), lambda i: (i // spb, 0, i % spb, 0, 0)),
            pl.BlockSpec((gate_rows, tm), lambda i: (0, i)),
            pl.BlockSpec((tm, pool_w), lambda i: (i, 0)),
        ],
        out_shape=[
            jax.ShapeDtypeStruct((batch, N_HEADS, HEAD_DIM, seq), BF16),
            jax.ShapeDtypeStruct((batch, N_HEADS, HEAD_DIM, seq), BF16),
            jax.ShapeDtypeStruct((batch, 4, seq // CMP_STRIDE, CMP_STRIDE * HEAD_DIM), BF16),
            jax.ShapeDtypeStruct((batch, 4, seq, HEAD_DIM), BF16),
            jax.ShapeDtypeStruct((batch, 4, seq // TK, HEAD_DIM + V_ONES_ROWS, TK), BF16),
            jax.ShapeDtypeStruct((gate_rows, n), F32),
            jax.ShapeDtypeStruct((n, pool_w), F32),
        ],
        scratch_shapes=[
            pltpu.VMEM((4, tm, HEAD_DIM), F32),
            pltpu.VMEM((2, W_STAGE_ROWS, d), F32),
            pltpu.VMEM((q_w, d), BF16), pltpu.VMEM((2 * kv_w, d), BF16), pltpu.VMEM((gate_rows, d), BF16),
            pltpu.VMEM((2 * kv_w, d), BF16), pltpu.VMEM((2 * kv_w, d), BF16), pltpu.VMEM((pool_w, d), BF16),
            pltpu.SemaphoreType.DMA((2,)),
        ],
        compiler_params=_cparams(1),
        name="inproj",
    )(x2d, ln, ang_t, w_in_t)


def _compress_kernel(t_ref, pe_ref, w1_ref, w2_ref, o_ref):
    for j in range(t_ref.shape[1]):
        kind = j // N_KV
        t2 = t_ref[0, j]
        w1 = w1_ref[kind]
        a = _dot(t2, w1)
        pb = _dot(pe_ref[kind], w1)
        bias = pb[0:1, :CMP_HIDDEN] + pb[1:2, CMP_HIDDEN:]
        nrow = a.shape[0]
        nxt = pltpu.roll(a[:, CMP_HIDDEN:], nrow - 1, 0)
        hid = jax.nn.gelu(a[:, :CMP_HIDDEN] + nxt + bias)
        out = _dot(hid.astype(BF16), w2_ref[kind])
        row = lax.broadcasted_iota(I32, out.shape, 0)
        out = jnp.where(row < nrow - 1, out, 0.0)
        o_ref[0, j] = (out.T if kind else out).astype(BF16)


def _compress(kvc2, pe2, w1, w2):
    batch, four, nrow, width = kvc2.shape
    whole = lambda a: pl.BlockSpec(a.shape, lambda b: (0,) * a.ndim)
    return pl.pallas_call(
        _compress_kernel,
        grid=(batch,),
        in_specs=[pl.BlockSpec((1, four, nrow, width), lambda b: (b, 0, 0, 0)), whole(pe2), whole(w1), whole(w2)],
        out_specs=pl.BlockSpec((1, four, nrow, HEAD_DIM), lambda b: (b, 0, 0, 0)),
        out_shape=jax.ShapeDtypeStruct((batch, four, nrow, HEAD_DIM), BF16),
        compiler_params=_cparams(1),
        name="compress",
    )(kvc2, pe2, w1, w2)


def _cast_expert_weights(step, n_steps, srcs, dsts, stages, outs, sem):
    def fetch(e):
        return [pltpu.make_async_copy(w.at[e], st, sem.at[0, i]) for i, (w, st) in enumerate(zip(srcs, stages))]

    def store(e):
        return [pltpu.make_async_copy(ob, w.at[e], sem.at[1, i]) for i, (ob, w) in enumerate(zip(outs, dsts))]

    @pl.when(step == 0)
    def _():
        for cp in fetch(0):
            cp.start(priority=1)

    for cp in fetch(step):
        cp.wait()

    @pl.when(step > 0)
    def _():
        for cp in store(step - 1):
            cp.wait()

    for st, ob in zip(stages, outs):
        n_chunks = st.shape[0] * st.shape[1] // W_CAST_CHUNK
        rows = st.shape[0] // n_chunks

        def cast(j, carry, st=st, ob=ob, rows=rows):
            r = pl.multiple_of(j * rows, rows)
            ob[pl.ds(r, rows), :] = st[pl.ds(r, rows), :].astype(BF16)
            return carry

        lax.fori_loop(0, n_chunks, cast, 0)

    for cp in store(step):
        cp.start(priority=1)

    @pl.when(step + 1 < n_steps)
    def _():
        for cp in fetch(jnp.minimum(step + 1, n_steps - 1)):
            cp.start(priority=1)

    @pl.when(step == n_steps - 1)
    def _():
        for cp in store(step):
            cp.wait()


def _nsa_kernel(qnt_ref, qrt_ref, cmp_ref, kk_ref, vt_ref, kblk_ref, gt_ref, gn_ref, wg_hbm, wu_hbm, wd_hbm,
                o_ref, wg16_hbm, wu16_hbm, wd16_hbm,
                m_scr, acc_scr, qa_scr, ocmp_scr, s_scr, o_scr, wg_st, wu_st, wd_st, wg_ob, wu_ob, wd_ob, wsem):
    _cast_expert_weights(pl.program_id(0) * pl.num_programs(1) + pl.program_id(1),
                         pl.num_programs(0) * pl.num_programs(1),
                         (wg_hbm, wu_hbm, wd_hbm), (wg16_hbm, wu16_hbm, wd16_hbm),
                         (wg_st, wu_st, wd_st), (wg_ob, wu_ob, wd_ob), wsem)
    qi = pl.program_id(1)
    tq = qnt_ref.shape[3]
    rows = GQA * tq
    n_cmp_pad = cmp_ref.shape[2]
    n_sel = kk_ref.shape[2] // SEL_LEN
    q0 = qi * tq

    m_scr[...] = jnp.full(m_scr.shape, NEG, F32)
    acc_scr[...] = jnp.zeros(acc_scr.shape, F32)

    def scores(item, slot):
        c, g, q_rows, k_tile, v_t, bias = item
        s_scr[slot] = _dot(k_tile, qa_scr[g, :q_rows, :])

    def softmax_pv(item, slot):
        c, g, q_rows, k_tile, v_t, bias = item
        s = s_scr[slot]
        if bias is not None:
            s = s + bias
        m_old = m_scr[c]
        m_new = jnp.maximum(m_old, jnp.max(s, axis=0, keepdims=True))
        alpha = jnp.exp2(m_old - m_new)
        p = jnp.exp2(s - m_new)
        acc_scr[c] = alpha * acc_scr[c] + _dot(v_t, p.astype(BF16))
        m_scr[c] = m_new

    def flash_out(c):
        return acc_scr[c, :HEAD_DIM, :] / acc_scr[c, HEAD_DIM:HEAD_DIM + 1, :]

    def tile_heads(a):
        return jnp.concatenate([a] * GQA, axis=1)

    kk_i = lax.broadcasted_iota(I32, (TK, tq), 0)
    qq_i = lax.broadcasted_iota(I32, (TK, tq), 1)
    causal_bias = tile_heads(jnp.where(kk_i <= qq_i, 0.0, NEG))
    band_bias = tile_heads(jnp.where(kk_i > qq_i, 0.0, NEG))
    n_win = WINDOW // TK

    for g in range(N_KV):
        qn4 = jnp.concatenate([qnt_ref[0, h] for h in range(g * GQA, (g + 1) * GQA)], axis=1)
        s = _dot(cmp_ref[0, g], qn4)
        n_idx = lax.broadcasted_iota(I32, (n_cmp_pad, rows), 0)
        t_idx = q0 + jnp.bitwise_and(lax.broadcasted_iota(I32, (n_cmp_pad, rows), 1), tq - 1)
        cmp_ok = n_idx * CMP_STRIDE + (CMP_LEN - 1) <= t_idx
        s = jnp.where(cmp_ok, s, NEG)
        m = jnp.max(s, axis=0, keepdims=True)
        e = jnp.where(cmp_ok, jnp.exp2(s - m), 0.0)
        l = jnp.sum(e, axis=0, keepdims=True)
        p = e * jnp.where(l > 0.0, 1.0 / l, 0.0)
        pb = p.astype(BF16)
        ocmp_scr[g] = _dot(cmp_ref[0, N_KV + g], pb)

        jn = lax.broadcasted_iota(I32, (n_sel, n_cmp_pad), 0) * SEL_LEN
        cn = lax.broadcasted_iota(I32, (n_sel, n_cmp_pad), 1) * CMP_STRIDE
        ov = jnp.clip(jnp.minimum(cn + CMP_LEN, jn + SEL_LEN) - jnp.maximum(cn, jn), 0, None)
        w_t = (ov.astype(F32) * (1.0 / CMP_LEN)).astype(BF16)
        imp4 = _dot(w_t, pb)
        imp = imp4[:, 0:tq]
        for r in range(1, GQA):
            imp = imp + imp4[:, r * tq:(r + 1) * tq]
        j_blk = lax.broadcasted_iota(I32, (n_sel, tq), 0)
        t_q = q0 + lax.broadcasted_iota(I32, (n_sel, tq), 1)
        sel_ok = j_blk * SEL_LEN <= t_q
        back = t_q // SEL_LEN - j_blk
        forced = (j_blk == 0) | ((back >= 0) & (back < N_LOCAL))
        val = jnp.where(sel_ok & forced, BIG, jnp.where(sel_ok, imp, -BIG))
        rank = jnp.zeros((n_sel, tq), F32)
        for i in range(n_sel):
            vi = val[i:i + 1, :]
            beats = (vi > val) | ((vi == val) & (j_blk > i))
            rank = rank + beats.astype(F32)
        keep = (rank < float(min(SEL_TOPK, n_sel))) & sel_ok
        bias_t = jnp.where(keep, 0.0, -SEL_MASK_BIAS)
        bias_t = jnp.concatenate([bias_t, jnp.zeros((HEAD_DIM - n_sel, tq), F32)], axis=0)
        qa_scr[g, :HEAD_DIM, :] = jnp.concatenate(
            [qrt_ref[0, h] for h in range(g * GQA, (g + 1) * GQA)], axis=1)
        qa_scr[g, HEAD_DIM:, :] = tile_heads(bias_t.astype(BF16))

    def sel_item(g, kj, bias):
        st = pl.multiple_of(kj * TK, TK)
        k_aug = jnp.concatenate([kk_ref[0, g, pl.ds(st, TK), :], kblk_ref[pl.ds(st, TK), :]], axis=1)
        return (g, g, 2 * HEAD_DIM, k_aug, vt_ref[0, g, kj], bias)

    def win_item(g, back):
        kj = qi - back
        st = pl.multiple_of(kj * TK, TK)
        bias = causal_bias if back == 0 else (band_bias if back == n_win else None)
        return (N_KV + g, g, HEAD_DIM, kk_ref[0, N_KV + g, pl.ds(st, TK), :], vt_ref[0, N_KV + g, kj], bias)

    assert N_KV == 2
    scores(sel_item(0, 0, None), 0)

    def sel_tile(kj):
        scores(sel_item(1, kj, None), 1)
        softmax_pv(sel_item(0, kj, None), 0)
        scores(sel_item(0, kj + 1, None), 0)
        softmax_pv(sel_item(1, kj, None), 1)

    def sel_pair(j, carry):
        sel_tile(2 * j)
        sel_tile(2 * j + 1)
        return carry

    lax.fori_loop(0, qi // 2, sel_pair, 0)

    @pl.when(qi % 2 == 1)
    def _():
        sel_tile(qi - 1)

    def run_tail(farthest):
        tail = ([sel_item(g, qi, causal_bias) for g in range(N_KV)]
                + [win_item(g, back) for back in range(farthest, -1, -1) for g in range(N_KV)])
        for i, item in enumerate(tail):
            if i + 1 < len(tail):
                scores(tail[i + 1], (i + 1) % 2)
            softmax_pv(item, i % 2)

    for nb in range(n_win):
        @pl.when(qi == nb)
        def _(nb=nb):
            run_tail(nb)

    @pl.when(qi >= n_win)
    def _():
        run_tail(n_win)

    gsig = jax.nn.sigmoid(gt_ref[...])
    for g in range(N_KV):
        o_cmp, o_sel, o_win = ocmp_scr[g], flash_out(g), flash_out(N_KV + g)
        for r in range(GQA):
            h = g * GQA + r
            sl = slice(r * tq, (r + 1) * tq)
            o_h = (gsig[3 * h:3 * h + 1, :] * o_cmp[:, sl]
                   + gsig[3 * h + 1:3 * h + 2, :] * o_sel[:, sl]
                   + gsig[3 * h + 2:3 * h + 3, :] * o_win[:, sl])
            o_scr[:, h * HEAD_DIM:(h + 1) * HEAD_DIM] = o_h.T

    o = o_scr[...]
    ms = jnp.mean(o * o, axis=-1, keepdims=True)
    o_ref[...] = (o * lax.rsqrt(ms + EPS) * gn_ref[...]).astype(BF16)


def _nsa(qnt, qrt, cmp, kk, vt, kblk, gt, gn, expert_weights):
    batch, _, _, seq = qnt.shape
    assert TQ == TK and WINDOW % TK == 0 and TQ & (TQ - 1) == 0
    nq = seq // TQ
    width = N_HEADS * HEAD_DIM
    assert all(w.shape[0] == batch * nq for w in expert_weights), "one expert's weights are cast per grid step"
    hbm = pl.BlockSpec(memory_space=pl.ANY)
    return pl.pallas_call(
        _nsa_kernel,
        grid=(batch, nq),
        in_specs=[
            pl.BlockSpec((1, N_HEADS, HEAD_DIM, TQ), lambda b, i: (b, 0, 0, i)),
            pl.BlockSpec((1, N_HEADS, HEAD_DIM, TQ), lambda b, i: (b, 0, 0, i)),
            pl.BlockSpec((1,) + cmp.shape[1:], lambda b, i: (b, 0, 0, 0)),
            pl.BlockSpec((1,) + kk.shape[1:], lambda b, i: (b, 0, 0, 0)),
            pl.BlockSpec((1,) + vt.shape[1:], lambda b, i: (b, 0, 0, 0, 0)),
            pl.BlockSpec((seq, LANES), lambda b, i: (0, 0)),
            pl.BlockSpec((gt.shape[0], TQ), lambda b, i: (0, b * nq + i)),
            pl.BlockSpec((1, width), lambda b, i: (0, 0)),
            hbm, hbm, hbm,
        ],
        out_specs=[pl.BlockSpec((TQ, width), lambda b, i: (b * nq + i, 0)), hbm, hbm, hbm],
        out_shape=[jax.ShapeDtypeStruct((batch * seq, width), BF16)]
        + [jax.ShapeDtypeStruct(w.shape, BF16) for w in expert_weights],
        scratch_shapes=[
            pltpu.VMEM((2 * N_KV, 1, GQA * TQ), F32),
            pltpu.VMEM((2 * N_KV, vt.shape[3], GQA * TQ), F32),
            pltpu.VMEM((N_KV, 2 * HEAD_DIM, GQA * TQ), BF16),
            pltpu.VMEM((N_KV, HEAD_DIM, GQA * TQ), F32),
            pltpu.VMEM((2, TK, GQA * TQ), F32),
            pltpu.VMEM((TQ, width), F32),
        ]
        + [pltpu.VMEM(w.shape[1:], F32) for w in expert_weights]
        + [pltpu.VMEM(w.shape[1:], BF16) for w in expert_weights]
        + [pltpu.SemaphoreType.DMA((2, 3))],
        compiler_params=_cparams(2),
        name="nsa",
    )(qnt, qrt, cmp, kk, vt, kblk, gt, gn, *expert_weights)


def _pool_group(cur, halo_rows, t1, window, w_mat, bias, scale):
    halo = halo_rows.shape[0]
    acc = jnp.concatenate([halo_rows, cur], axis=0)
    span = 1
    while span < window:
        acc = acc + jnp.concatenate([jnp.zeros((span, acc.shape[1]), F32), acc[:-span]], axis=0)
        span *= 2
    mean = acc[halo:] / jnp.minimum(t1, float(window))
    return (_dot((mean - cur).astype(BF16), w_mat) + bias) * scale


def _pool_kernel(u_ref, w_ref, b_ref, sc_ref, gn_ref, o_ref, y_scr, halo_scr):
    si = pl.program_id(1)
    tm = u_ref.shape[1]
    halo = halo_scr.shape[0]

    @pl.when(si == 0)
    def _():
        halo_scr[...] = jnp.zeros(halo_scr.shape, F32)

    t1 = (si * tm + 1 + lax.broadcasted_iota(I32, (tm, 1), 0)).astype(F32)
    cg = w_ref.shape[1]
    for gi, window in enumerate(POOL_SIZES):
        sl = slice(gi * cg, (gi + 1) * cg)
        cur = u_ref[0, :, sl]
        y_scr[:, sl] = _pool_group(cur, halo_scr[:, sl], t1, window, w_ref[gi], b_ref[:, sl], sc_ref[:, sl])
        halo_scr[:, sl] = cur[tm - halo:, :]
    y = y_scr[...]
    ms = jnp.mean(y * y, axis=-1, keepdims=True)
    o_ref[...] = (y * lax.rsqrt(ms + EPS) * gn_ref[...]).astype(BF16)


def _pool(u3, w_pool, b_pool, pool_scale, gn_pool):
    batch, seq, c = u3.shape
    tm = TM_PROJ
    ns = seq // tm
    vec = lambda b, i: (0, 0)
    return pl.pallas_call(
        _pool_kernel,
        grid=(batch, ns),
        in_specs=[
            pl.BlockSpec((1, tm, c), lambda b, i: (b, i, 0)),
            pl.BlockSpec(w_pool.shape, lambda b, i: (0, 0, 0)),
            pl.BlockSpec((1, c), vec),
            pl.BlockSpec((1, c), vec),
            pl.BlockSpec((1, c), vec),
        ],
        out_specs=pl.BlockSpec((tm, c), lambda b, i: (b * ns + i, 0)),
        out_shape=jax.ShapeDtypeStruct((batch * seq, c), BF16),
        scratch_shapes=[pltpu.VMEM((tm, c), F32), pltpu.VMEM((max(POOL_SIZES), c), F32)],
        compiler_params=_cparams(2),
        name="pool",
    )(u3, w_pool, b_pool, pool_scale, gn_pool)


def _outproj_kernel(mn_ref, mp_ref, x_ref, w_hbm, ln_ref, wr_ref, br_ref,
                    x1_ref, h2_ref, eid_ref, gate_ref, x1_scr, ssq_scr, stage, w_ref, wsem):
    tm, d = x_ref.shape
    half = mn_ref.shape[1]
    chunk = 512

    @pl.when(pl.program_id(0) == 0)
    def _():
        x1_scr[...] = jnp.zeros(x1_scr.shape, F32)
        ssq_scr[...] = jnp.zeros(ssq_scr.shape, F32)
        rc = stage.shape[1]
        n_chunks = w_ref.shape[0] // rc

        def chunk_copy(c, slot):
            return pltpu.make_async_copy(w_hbm.at[0, pl.ds(c * rc, rc)], stage.at[slot], wsem.at[slot])

        chunk_copy(0, 0).start()
        for c in range(n_chunks):
            slot = c % 2
            if c + 1 < n_chunks:
                chunk_copy(c + 1, 1 - slot).start()
            chunk_copy(c, slot).wait()
            w_ref[c * rc:(c + 1) * rc, :] = stage[slot].astype(BF16)

    h2 = x1_scr[...] * lax.rsqrt(ssq_scr[...] * (1.0 / d) + EPS) * ln_ref[...]
    _store_token_tiles(h2_ref, _pack_bf16_halves(h2))

    logits = _dot_nt(wr_ref[...], h2.astype(BF16)) + br_ref[:, 0:1]
    ng, ne = N_EXPERT_GROUPS, EXPERTS_PER_GROUP
    row = lax.broadcasted_iota(I32, (ne, tm), 0)
    lg = jnp.where(row < ng, logits[0:ne], NEG)
    mg = jnp.max(lg, axis=0, keepdims=True)
    g_sel = jnp.min(jnp.where(lg == mg, row, ne), axis=0, keepdims=True)
    p_g = 1.0 / jnp.sum(jnp.exp(lg - mg), axis=0, keepdims=True)
    le = jnp.zeros((ne, tm), F32)
    for g in range(ng):
        le = jnp.where(g_sel == g, logits[ne * (g + 1):ne * (g + 2)], le)
    ex = jnp.exp(le - jnp.max(le, axis=0, keepdims=True))
    pe = ex / jnp.sum(ex, axis=0, keepdims=True)
    p1 = jnp.max(pe, axis=0, keepdims=True)
    i1 = jnp.min(jnp.where(pe == p1, row, ne), axis=0, keepdims=True)
    rest = jnp.where(row == i1, -1.0, pe)
    p2 = jnp.max(rest, axis=0, keepdims=True)
    i2 = jnp.min(jnp.where(rest == p2, row, ne), axis=0, keepdims=True)
    psum = p1 + p2
    zero_i = jnp.zeros((ne, tm), I32)
    eid = jnp.where(row == 0, g_sel * ne + i1, jnp.where(row == 1, g_sel * ne + i2, zero_i))
    gate = jnp.where(row == 0, p_g * p1 / psum, jnp.where(row == 1, p_g * p2 / psum, 0.0))
    eid_ref[...] = eid
    gate_ref[...] = gate

    mn = mn_ref[...]
    mp = mp_ref[...]
    ssq = jnp.zeros((tm, 1), F32)
    for c in range(d // chunk):
        sl = slice(c * chunk, (c + 1) * chunk)
        x1 = x_ref[:, sl] + (_dot(mn, w_ref[:half, sl]) + _dot(mp, w_ref[half:, sl]))
        x1_ref[:, sl] = x1
        x1_scr[:, sl] = x1
        ssq = ssq + jnp.sum(x1 * x1, axis=-1, keepdims=True)
    ssq_scr[...] = ssq


def _outproj(mix_nsa, mix_pool, x2d, w_out, ln_moe, w_router_t, b_router):
    n, d = x2d.shape
    tm = TM_PROJ
    half = mix_nsa.shape[1]
    const = lambda i: (0, 0)
    last = n // tm - 1
    cur = lambda i: (jnp.minimum(i, last), 0)
    prev = lambda i: (jnp.maximum(i - 1, 0), 0)
    prev_t = lambda i: (0, jnp.maximum(i - 1, 0))
    return pl.pallas_call(
        _outproj_kernel,
        grid=(n // tm + 1,),
        in_specs=[
            pl.BlockSpec((tm, half), cur),
            pl.BlockSpec((tm, half), cur),
            pl.BlockSpec((tm, d), cur),
            pl.BlockSpec(memory_space=pl.ANY),
            pl.BlockSpec((1, d), const),
            pl.BlockSpec(w_router_t.shape, const),
            pl.BlockSpec(b_router.shape, const),
        ],
        out_specs=[
            pl.BlockSpec((tm, d), cur),
            pl.BlockSpec((tm * SUBLANES, LANES), prev),
            pl.BlockSpec((SUBLANES, tm), prev_t),
            pl.BlockSpec((SUBLANES, tm), prev_t),
        ],
        scratch_shapes=[pltpu.VMEM((tm, d), F32), pltpu.VMEM((tm, 1), F32),
                        pltpu.VMEM((2, W_STAGE_ROWS, d), F32), pltpu.VMEM(w_out.shape[1:], BF16),
                        pltpu.SemaphoreType.DMA((2,))],
        out_shape=[
            jax.ShapeDtypeStruct((n, d), F32),
            jax.ShapeDtypeStruct((n * SUBLANES, LANES), U32),
            jax.ShapeDtypeStruct((SUBLANES, n), I32),
            jax.ShapeDtypeStruct((SUBLANES, n), F32),
        ],
        compiler_params=_cparams(1),
        name="outproj",
    )(mix_nsa, mix_pool, x2d, w_out, ln_moe, w_router_t, b_router)


def _dispatch_kernel(eid_ref, dest_ref, meta_ref, rank_scr):
    n = eid_ref.shape[1]
    t = T_RANK
    row = lax.broadcasted_iota(I32, (N_EXPERTS, t), 0)
    before = (lax.broadcasted_iota(I32, (t, t), 0) < lax.broadcasted_iota(I32, (t, t), 1)).astype(BF16)
    carry = jnp.zeros((N_EXPERTS, 1), F32)
    for k in range(2):
        for j in range(n // t):
            sl = slice(j * t, (j + 1) * t)
            oh = row == eid_ref[k:k + 1, sl]
            ohf = oh.astype(F32)
            prior = _dot(ohf.astype(BF16), before) + carry
            rank_scr[k:k + 1, sl] = jnp.sum(jnp.where(oh, prior, 0.0), axis=0, keepdims=True)
            carry = carry + jnp.sum(ohf, axis=1, keepdims=True)
    counts = jnp.broadcast_to(carry, (N_EXPERTS, LANES))
    padded = jnp.floor((counts + (TM_MOE - 1.0)) * (1.0 / TM_MOE)) * TM_MOE
    ends = padded
    r_idx = lax.broadcasted_iota(I32, (N_EXPERTS, LANES), 0)
    sh = 1
    while sh < N_EXPERTS:
        ends = ends + jnp.where(r_idx >= sh, pltpu.roll(ends, sh, 0), 0.0)
        sh *= 2
    starts = (ends - padded)[:, 0:1]
    for k in range(2):
        for j in range(n // t):
            sl = slice(j * t, (j + 1) * t)
            oh = row == eid_ref[k:k + 1, sl]
            base = jnp.sum(jnp.where(oh, starts, 0.0), axis=0, keepdims=True)
            dest_ref[k:k + 1, sl] = (rank_scr[k:k + 1, sl] + base).astype(I32)
    for k in range(2, SUBLANES):
        dest_ref[k:k + 1, :] = jnp.zeros((1, n), I32)
    blk_start = lax.broadcasted_iota(I32, (N_EXPERTS, LANES), 1).astype(F32) * TM_MOE
    blk_expert = jnp.sum((ends <= blk_start).astype(F32), axis=0, keepdims=True)
    blk_expert = jnp.minimum(blk_expert, N_EXPERTS - 1.0)
    n_used = ends[N_EXPERTS - 1:N_EXPERTS, :] * (1.0 / TM_MOE)
    on_diag = r_idx == lax.broadcasted_iota(I32, (N_EXPERTS, LANES), 1)
    end_blk = jnp.sum(jnp.where(on_diag, ends, 0.0), axis=0, keepdims=True) * (1.0 / TM_MOE)
    n_blk = jnp.sum(jnp.where(on_diag, padded, 0.0), axis=0, keepdims=True) * (1.0 / TM_MOE)
    r8 = lax.broadcasted_iota(I32, (SUBLANES, LANES), 0)
    meta = jnp.where(r8 == 0, blk_expert,
                     jnp.where(r8 == 1, n_used, jnp.where(r8 == 2, end_blk, jnp.where(r8 == 3, n_blk, 0.0))))
    meta_ref[...] = meta.astype(I32)


def _dispatch(eid):
    n = eid.shape[1]
    return pl.pallas_call(
        _dispatch_kernel,
        out_shape=[jax.ShapeDtypeStruct((SUBLANES, n), I32),
                   jax.ShapeDtypeStruct((SUBLANES, LANES), I32)],
        scratch_shapes=[pltpu.VMEM((SUBLANES, n), F32)],
        compiler_params=pltpu.CompilerParams(vmem_limit_bytes=VMEM_LIMIT_BYTES),
        name="dispatch",
    )(eid)


def _row_copy(src_ref, src_row, dst_ref, dst_row, sem):
    tile = lambda ref, row: ref.at[pl.ds(pl.multiple_of(row * SUBLANES, SUBLANES), SUBLANES)]
    return pltpu.make_async_copy(tile(src_ref, src_row), tile(dst_ref, dst_row), sem)


def _scatter_kernel(dest_ref, meta_ref, h_ref, xs_ref, zero_buf, sem, zsem):
    tm = h_ref.shape[0] // SUBLANES
    blk = zero_buf.shape[0]

    @pl.when(pl.program_id(0) == 0)
    def _():
        zero_buf[...] = jnp.zeros(zero_buf.shape, zero_buf.dtype)
        n_used = meta_ref[1, 0]

        def zero_block(b):
            return pltpu.make_async_copy(zero_buf, xs_ref.at[pl.ds(pl.multiple_of(b * blk, blk), blk)], zsem)

        def sweep(act):
            def per_expert(e, c):
                @pl.when(meta_ref[3, e] > 0)
                def _():
                    act(zero_block(meta_ref[2, e] - 1))
                return c

            def per_tail(b, c):
                @pl.when(b >= n_used)
                def _():
                    act(zero_block(b))
                return c

            lax.fori_loop(0, N_EXPERTS, per_expert, 0)
            lax.fori_loop(0, xs_ref.shape[0] // blk, per_tail, 0)

        sweep(lambda cp: cp.start())
        sweep(lambda cp: cp.wait())

    def start(r, c):
        for k in range(2):
            _row_copy(h_ref, r, xs_ref, dest_ref[0, k, r], sem).start(priority=k)
        return c

    lax.fori_loop(0, tm, start, 0, unroll=DMA_ISSUE_UNROLL)
    for k in range(2):
        pltpu.make_async_copy(h_ref, xs_ref.at[pl.ds(0, tm * SUBLANES)], sem).wait()


def _scatter(dest3, meta, h2, cap):
    nsteps, _, tm = dest3.shape
    return pl.pallas_call(
        _scatter_kernel,
        grid=(nsteps,),
        in_specs=[
            pl.BlockSpec((1, 2, tm), lambda i: (i, 0, 0), memory_space=pltpu.SMEM),
            pl.BlockSpec(memory_space=pltpu.SMEM),
            pl.BlockSpec((tm * SUBLANES, LANES), lambda i: (i, 0)),
        ],
        out_specs=pl.BlockSpec(memory_space=pl.ANY),
        out_shape=jax.ShapeDtypeStruct((cap * SUBLANES, LANES), h2.dtype),
        scratch_shapes=[pltpu.VMEM((TM_MOE * SUBLANES, LANES), h2.dtype), pltpu.SemaphoreType.DMA(()),
                        pltpu.SemaphoreType.DMA(())],
        compiler_params=_cparams(1),
        name="scatter",
    )(dest3, meta, h2)


def _experts_kernel(meta_ref, xs_ref, wg_hbm, wu_hbm, wd_hbm, ys_ref, wg_buf, wu_buf, wd_buf, elist_ref, ord_ref, sem):
    c = pl.program_id(0)
    n_used = meta_ref[1, 0]
    e = meta_ref[0, c]
    first = (c == 0) | (e != meta_ref[0, jnp.maximum(c - 1, 0)])

    def fetch(expert, slot):
        return [pltpu.make_async_copy(w.at[expert], buf.at[slot], sem.at[slot, i])
                for i, (w, buf) in enumerate(((wg_hbm, wg_buf), (wu_hbm, wu_buf), (wd_hbm, wd_buf)))]

    @pl.when(c == 0)
    def _():
        def scan(b, cnt):
            eb = meta_ref[0, b]
            take = ((b == 0) | (eb != meta_ref[0, jnp.maximum(b - 1, 0)])) & (b < n_used)

            @pl.when(take)
            def _():
                elist_ref[cnt] = eb

            return cnt + take.astype(I32)

        cnt = lax.fori_loop(0, pl.num_programs(0), scan, 0)
        ord_ref[0] = 0
        ord_ref[1] = cnt
        for k in range(W_SLOTS - 1):
            @pl.when(k < cnt)
            def _(k=k):
                for cp, prio in zip(fetch(elist_ref[k], k), WEIGHT_DMA_PRIORITIES):
                    cp.start(priority=prio)

    @pl.when(first & (c < n_used))
    def _():
        @pl.when(c > 0)
        def _():
            ord_ref[0] = ord_ref[0] + 1
        k = ord_ref[0]
        ahead = k + W_SLOTS - 1

        @pl.when(ahead < ord_ref[1])
        def _():
            for cp, prio in zip(fetch(elist_ref[jnp.minimum(ahead, N_EXPERTS - 1)], ahead % W_SLOTS),
                                WEIGHT_DMA_PRIORITIES):
                cp.start(priority=prio)

        for cp in fetch(e, k % W_SLOTS):
            cp.wait()

    @pl.when(c < n_used)
    def _():
        slot = ord_ref[0] % W_SLOTS
        lo, hi = _unpack_bf16_halves(_load_token_tiles(xs_ref))
        lo, hi = lo.astype(BF16), hi.astype(BF16)
        half = lo.shape[1]

        def proj(buf):
            return _dot(lo, buf[slot, :half, :]) + _dot(hi, buf[slot, half:, :])

        act = (jax.nn.silu(proj(wg_buf)) * proj(wu_buf)).astype(BF16)
        _store_token_tiles(ys_ref, _pack_bf16_halves(_dot(act, wd_buf[slot])))

    @pl.when(c >= n_used)
    def _():
        ys_ref[...] = jnp.zeros(ys_ref.shape, U32)


def _experts(meta, xs, w_gate, w_up, w_down):
    rows = TM_MOE * SUBLANES
    nblk = xs.shape[0] // rows
    d, ff = w_gate.shape[1:]

    def row_map(c, meta):
        return (jnp.minimum(c, meta[1, 0] - 1), 0)

    hbm = pl.BlockSpec(memory_space=pl.ANY)
    return pl.pallas_call(
        _experts_kernel,
        grid_spec=pltpu.PrefetchScalarGridSpec(
            num_scalar_prefetch=1,
            grid=(nblk,),
            in_specs=[pl.BlockSpec((rows, LANES), row_map), hbm, hbm, hbm],
            out_specs=pl.BlockSpec((rows, LANES), lambda c, meta: (c, 0)),
            scratch_shapes=[
                pltpu.VMEM((W_SLOTS, d, ff), BF16), pltpu.VMEM((W_SLOTS, d, ff), BF16),
                pltpu.VMEM((W_SLOTS, ff, d), BF16),
                pltpu.SMEM((N_EXPERTS,), I32),
                pltpu.SMEM((2,), I32),
                pltpu.SemaphoreType.DMA((W_SLOTS, 3)),
            ],
        ),
        out_shape=jax.ShapeDtypeStruct(xs.shape, U32),
        compiler_params=_cparams(1),
        name="experts",
    )(meta, xs, w_gate, w_up, w_down)


def _combine_kernel(dest_ref, dest_next_ref, x1_ref, gate_ref, ln_ref, ys_ref, o_ref, ybuf, sem):
    i = pl.program_id(0)
    tm = x1_ref.shape[0]
    slot = i % 2

    def issue(d_ref, s):
        def start(r, c):
            for k in range(2):
                _row_copy(ys_ref, d_ref[0, k, r], ybuf.at[s, k], r, sem.at[s]).start(priority=k)
            return c

        lax.fori_loop(0, tm, start, 0, unroll=DMA_ISSUE_UNROLL)

    @pl.when(i == 0)
    def _():
        issue(dest_ref, 0)

    @pl.when(i + 1 < pl.num_programs(0))
    def _():
        issue(dest_next_ref, 1 - slot)

    for k in range(2):
        pltpu.make_async_copy(ys_ref.at[pl.ds(0, tm * SUBLANES)], ybuf.at[slot, k], sem.at[slot]).wait()
    gate = gate_ref[...]
    lo0, hi0 = _unpack_bf16_halves(_load_token_tiles(ybuf.at[slot, 0]))
    lo1, hi1 = _unpack_bf16_halves(_load_token_tiles(ybuf.at[slot, 1]))
    g0, g1 = gate[:, 0:1], gate[:, 1:2]
    moe = jnp.concatenate([lo0 * g0 + lo1 * g1, hi0 * g0 + hi1 * g1], axis=1)
    x2 = x1_ref[...] + moe
    ms = jnp.mean(x2 * x2, axis=-1, keepdims=True)
    o_ref[...] = x2 * lax.rsqrt(ms + EPS) * ln_ref[...]


def _combine(dest3, x1, gate_rows, ln_final, ys):
    n, d = x1.shape
    tm = TM_COMBINE
    last = n // tm - 1
    return pl.pallas_call(
        _combine_kernel,
        grid=(n // tm,),
        in_specs=[
            pl.BlockSpec((1, 2, tm), lambda i: (i, 0, 0), memory_space=pltpu.SMEM),
            pl.BlockSpec((1, 2, tm), lambda i: (jnp.minimum(i + 1, last), 0, 0), memory_space=pltpu.SMEM),
            pl.BlockSpec((tm, d), lambda i: (i, 0)),
            pl.BlockSpec((tm, LANES), lambda i: (i, 0)),
            pl.BlockSpec((1, d), lambda i: (0, 0)),
            pl.BlockSpec(memory_space=pl.ANY),
        ],
        out_specs=pl.BlockSpec((tm, d), lambda i: (i, 0)),
        out_shape=jax.ShapeDtypeStruct((n, d), F32),
        scratch_shapes=[pltpu.VMEM((2, 2, tm * SUBLANES, LANES), U32), pltpu.SemaphoreType.DMA((2,))],
        compiler_params=_cparams(1),
        name="combine",
    )(dest3, dest3, x1, gate_rows, ln_final, ys)


def kernel(x, positions, ln_mix, w_in, pe_cmp_k, w_cmp_k1, w_cmp_k2, pe_cmp_v, w_cmp_v1, w_cmp_v2, w_pool, b_pool, pool_scale, gn_nsa, gn_pool, w_out, ln_moe, w_router_group, b_router_group, w_router_expert, b_router_expert, w_gate, w_up, w_down, ln_final):
    batch, seq, d = x.shape
    n = batch * seq
    depth = w_in.shape[0]

    inv_freq = ROPE_THETA ** (-jnp.arange(0, ROT_DIM, 2, dtype=F32) / ROT_DIM)
    ang_t = inv_freq[:, None] * positions.astype(F32).reshape(1, n)
    kblk = (jnp.arange(seq, dtype=I32)[:, None] // SEL_LEN == jnp.arange(LANES, dtype=I32)[None, :]).astype(BF16)

    x2d = x.reshape(n, d)
    assert depth == 1, "single-layer operation"
    for l in range(depth):
        gate_rows_pad = -(-N_GATES // SUBLANES) * SUBLANES
        qnt, qrt, kvc, kk, vt, gt, u = _inproj(x2d, ln_mix[l][None], ang_t, jnp.swapaxes(w_in, 1, 2),
                                               gate_rows_pad, batch, seq)

        half = CMP_LEN * HEAD_DIM // 2
        def stack_w1(w):
            return jnp.concatenate([w[:half], w[half:]], axis=1)
        w1 = jnp.stack([stack_w1(w_cmp_k1[l]), stack_w1(w_cmp_v1[l])]).astype(BF16)
        w2 = jnp.stack([w_cmp_k2[l], w_cmp_v2[l]]).astype(BF16)
        def pe_rows(pe):
            return jnp.pad(pe.reshape(2, half), ((0, SUBLANES - 2), (0, 0)))
        pe2 = jnp.stack([pe_rows(pe_cmp_k[l]), pe_rows(pe_cmp_v[l])]).astype(BF16)
        cmp = _compress(kvc, pe2, w1, w2)

        mix_nsa, wg16, wu16, wd16 = _nsa(qnt, qrt, cmp, kk, vt, kblk, gt, gn_nsa[l][None],
                                         (w_gate[l], w_up[l], w_down[l]))
        mix_pool = _pool(u.reshape(batch, seq, -1), w_pool[l].astype(BF16), b_pool[l][None],
                         pool_scale[l][None], gn_pool[l][None])

        ng, ne = N_EXPERT_GROUPS, EXPERTS_PER_GROUP
        w_r = jnp.concatenate([
            w_router_group[l].T, jnp.zeros((ne - ng, d), F32),
            jnp.transpose(w_router_expert[l], (0, 2, 1)).reshape(ng * ne, d),
            jnp.zeros((LANES - ne - ng * ne, d), F32)], axis=0).astype(BF16)
        b_r = jnp.concatenate([
            b_router_group[l], jnp.zeros((ne - ng,), F32), b_router_expert[l].reshape(-1),
            jnp.zeros((LANES - ne - ng * ne,), F32)])
        b_r = jnp.broadcast_to(b_r[:, None], (LANES, LANES))
        x1, h2, eid, gate = _outproj(mix_nsa, mix_pool, x2d, w_out, ln_moe[l][None], w_r, b_r)

        dest, meta = _dispatch(eid)
        cap = (2 * n // TM_MOE + N_EXPERTS) * TM_MOE
        def per_tile(tm):
            return jnp.transpose(dest[:2].reshape(2, n // tm, tm), (1, 0, 2))
        xs = _scatter(per_tile(TS_SCATTER), meta, h2, cap)
        ys = _experts(meta, xs, wg16, wu16, wd16)
        dest_tok = per_tile(TM_COMBINE)
        gate_rows = jnp.pad(gate[:2].T, ((0, 0), (0, LANES - 2)))
        x2d = _combine(dest_tok, x1, gate_rows, ln_final[None], ys)
    return x2d.reshape(batch, seq, d)
```

```python
import jax
import jax.numpy as jnp
from jax import lax
from jax.experimental import pallas as pl
from jax.experimental.pallas import tpu as pltpu

F32 = jnp.float32
BF16 = jnp.bfloat16
I32 = jnp.int32

HEAD_DIM = 128
N_HEADS = 8
N_KV = 2
GQA = N_HEADS // N_KV
N_GATES = 3 * N_HEADS
ROT_DIM = HEAD_DIM // 4
ROPE_THETA = 500000.0
CMP_LEN = 32
CMP_STRIDE = 16
CMP_HIDDEN = 2 * HEAD_DIM
SEL_LEN = 64
SEL_TOPK = 16
N_LOCAL = 2
WINDOW = 512
POOL_SIZES = (2, 4, 8, 16)
N_EXPERT_GROUPS = 4
EXPERTS_PER_GROUP = 8
N_EXPERTS = N_EXPERT_GROUPS * EXPERTS_PER_GROUP
EPS = 1e-6
NEG = -1e30
BIG = 1e30
SCALE = HEAD_DIM ** -0.5
LOG2E = 1.4426950408889634
Q_SCALE = SCALE * LOG2E

LANES = 128
SUBLANES = 8
VMEM_LIMIT_BYTES = 56 * 1024 * 1024

TM_PROJ = 512
TQ = 256
TK = 256
SEL_MASK_BIAS = 2.0 ** 40
V_ONES_ROWS = 16
TM_MOE = 256
TS_SCATTER = 2048
TM_COMBINE = 512
WEIGHT_DMA_PRIORITIES = (0, 1, 1)
W_STAGE_ROWS = 256
W_SLOTS = 3
DMA_ISSUE_GROUP = 64
T_RANK = 512


def _cparams(n_axes):
    return pltpu.CompilerParams(dimension_semantics=("arbitrary",) * n_axes, vmem_limit_bytes=VMEM_LIMIT_BYTES)


def _dot(a, b):
    return jnp.dot(a, b, preferred_element_type=F32)


def _dot_nt(a, b):
    return lax.dot_general(a, b, (((1,), (1,)), ((), ())), preferred_element_type=F32)


U32 = jnp.uint32
_HI16 = 0xFFFF0000


def _pack_bf16_halves(x):
    c = x.shape[1] // 2
    bits = lambda t: lax.bitcast_convert_type(t.astype(BF16).astype(F32), U32)
    return lax.shift_right_logical(bits(x[:, :c]), U32(16)) | (bits(x[:, c:]) & U32(_HI16))


def _store_token_tiles(ref, words):
    m, width = words.shape
    assert width == SUBLANES * LANES
    for j in range(SUBLANES):
        ref[pl.ds(j, m, stride=SUBLANES), :] = words[:, j * LANES:(j + 1) * LANES]


def _load_token_tiles(ref):
    m = ref.shape[0] // SUBLANES
    return jnp.concatenate([ref[pl.ds(j, m, stride=SUBLANES), :] for j in range(SUBLANES)], axis=1)


def _unpack_bf16_halves(w):
    lo = lax.bitcast_convert_type(lax.shift_left(w, U32(16)), F32)
    hi = lax.bitcast_convert_type(w & U32(_HI16), F32)
    return lo, hi


def _rope(t, cos, sin, axis):
    idx = lax.broadcasted_iota(I32, t.shape, axis)
    partner = jnp.where(idx < ROT_DIM // 2,
                        pltpu.roll(t, HEAD_DIM - ROT_DIM // 2, axis),
                        pltpu.roll(t, ROT_DIM // 2, axis))
    return t * cos + partner * sin


def _inproj_kernel(x_ref, ln_ref, ang_ref, wt_hbm,
                   qnt_ref, qrt_ref, kvc_ref, kk_ref, vt_ref, gt_ref, u_ref,
                   kc_scr, stage, wq_ref, wv_ref, wg_ref, wkc_ref, wk_ref, wu_ref, wsem):
    @pl.when(pl.program_id(0) == 0)
    def _():
        rc = stage.shape[1]
        q_w, kv_w = N_HEADS * HEAD_DIM, N_KV * HEAD_DIM
        o_kc, o_ks = q_w, q_w + 2 * kv_w
        o_vs, o_kw, o_vw, o_g = o_ks + kv_w, o_ks + 2 * kv_w, o_ks + 3 * kv_w, o_ks + 4 * kv_w
        o_u = o_g + N_GATES
        pieces = ([(r0, rc, wq_ref, r0) for r0 in range(0, q_w, rc)]
                  + [(o_kc + r0, rc, wkc_ref, r0) for r0 in range(0, 2 * kv_w, rc)]
                  + [(o_ks, kv_w, wk_ref, 0), (o_kw, kv_w, wk_ref, kv_w)]
                  + [(o_vs, kv_w, wv_ref, 0), (o_vw, kv_w, wv_ref, kv_w)]
                  + [(o_g, N_GATES, wg_ref, 0)]
                  + [(o_u + r0, rc, wu_ref, r0) for r0 in range(0, wu_ref.shape[0], rc)])
        assert all(n <= rc and src % SUBLANES == 0 for src, n, _, _ in pieces)
        wg_ref[...] = jnp.zeros(wg_ref.shape, BF16)

        def piece_copy(i, slot):
            src, rows, _, _ = pieces[i]
            return pltpu.make_async_copy(wt_hbm.at[0, pl.ds(src, rows)], stage.at[slot, pl.ds(0, rows)],
                                         wsem.at[slot])

        piece_copy(0, 0).start()
        for i, (src, rows, dst, dst_row) in enumerate(pieces):
            slot = i % 2
            if i + 1 < len(pieces):
                piece_copy(i + 1, 1 - slot).start()
            piece_copy(i, slot).wait()
            dst[dst_row:dst_row + rows, :] = stage[slot, :rows, :].astype(BF16)

    x = x_ref[...]
    ms = jnp.mean(x * x, axis=-1, keepdims=True)
    hb = (x * lax.rsqrt(ms + EPS) * ln_ref[...]).astype(BF16)
    chunk = 4 * HEAD_DIM

    ang = ang_ref[...]
    ca, sa = jnp.cos(ang), jnp.sin(ang)
    rest = (HEAD_DIM - ROT_DIM, ang.shape[1])
    cost = jnp.concatenate([ca, ca, jnp.ones(rest, F32)], axis=0)
    sint = jnp.concatenate([-sa, sa, jnp.zeros(rest, F32)], axis=0)
    cos, sin = cost.T, sint.T
    for ch in range(2):
        rt = _dot_nt(wq_ref[ch * chunk:(ch + 1) * chunk, :], hb)
        for j in range(4):
            t = rt[j * HEAD_DIM:(j + 1) * HEAD_DIM, :]
            qnt_ref[0, ch * 4 + j] = (t * Q_SCALE).astype(BF16)
            qrt_ref[0, ch * 4 + j] = (_rope(t, cost, sint, 0) * Q_SCALE).astype(BF16)
    rt = _dot_nt(wv_ref[...], hb)
    ones = jnp.ones((V_ONES_ROWS, TK), BF16)
    for j in range(4):
        for s in range(vt_ref.shape[2]):
            vt_ref[0, j, s] = jnp.concatenate(
                [rt[j * HEAD_DIM:(j + 1) * HEAD_DIM, s * TK:(s + 1) * TK].astype(BF16), ones], axis=0)
    gt_ref[...] = _dot_nt(wg_ref[...], hb)

    r = _dot_nt(hb, wkc_ref[...])
    groups = kvc_ref.shape[2]
    for j in range(4):
        kc_scr[j] = r[:, j * HEAD_DIM:(j + 1) * HEAD_DIM]
        for i in range(CMP_STRIDE):
            kvc_ref[0, j, :, i * HEAD_DIM:(i + 1) * HEAD_DIM] = (
                kc_scr[j, pl.ds(i, groups, stride=CMP_STRIDE), :].astype(BF16))
    r = _dot_nt(hb, wk_ref[...])
    for j in range(4):
        kk_ref[0, j] = _rope(r[:, j * HEAD_DIM:(j + 1) * HEAD_DIM], cos, sin, 1).astype(BF16)
    for ch in range(u_ref.shape[1] // chunk):
        u_ref[:, ch * chunk:(ch + 1) * chunk] = _dot_nt(hb, wu_ref[ch * chunk:(ch + 1) * chunk, :])


def _inproj(x2d, ln, ang_t, w_in_t, gate_rows, batch, seq):
    n, d = x2d.shape
    tm = TM_PROJ
    assert tm % TK == 0
    spb = seq // tm
    const = lambda i: (0, 0)
    q_w, kv_w = N_HEADS * HEAD_DIM, N_KV * HEAD_DIM
    pool_w = w_in_t.shape[1] - (q_w + 6 * kv_w + N_GATES)
    row_heads = lambda i: (i // spb, 0, i % spb, 0)
    col_heads = lambda i: (i // spb, 0, 0, i % spb)
    return pl.pallas_call(
        _inproj_kernel,
        grid=(n // tm,),
        in_specs=[
            pl.BlockSpec((tm, d), lambda i: (i, 0)),
            pl.BlockSpec((1, d), const),
            pl.BlockSpec((ang_t.shape[0], tm), lambda i: (0, i)),
            pl.BlockSpec(memory_space=pl.ANY),
        ],
        out_specs=[
            pl.BlockSpec((1, N_HEADS, HEAD_DIM, tm), col_heads),
            pl.BlockSpec((1, N_HEADS, HEAD_DIM, tm), col_heads),
            pl.BlockSpec((1, 4, tm // CMP_STRIDE, CMP_STRIDE * HEAD_DIM), row_heads),
            pl.BlockSpec((1, 4, tm, HEAD_DIM), row_heads),
            pl.BlockSpec((1, 4, tm // TK, HEAD_DIM + V_ONES_ROWS, TK), lambda i: (i // spb, 0, i % spb, 0, 0)),
            pl.BlockSpec((gate_rows, tm), lambda i: (0, i)),
            pl.BlockSpec((tm, pool_w), lambda i: (i, 0)),
        ],
        out_shape=[
            jax.ShapeDtypeStruct((batch, N_HEADS, HEAD_DIM, seq), BF16),
            jax.ShapeDtypeStruct((batch, N_HEADS, HEAD_DIM, seq), BF16),
            jax.ShapeDtypeStruct((batch, 4, seq // CMP_STRIDE, CMP_STRIDE * HEAD_DIM), BF16),
            jax.ShapeDtypeStruct((batch, 4, seq, HEAD_DIM), BF16),
            jax.ShapeDtypeStruct((batch, 4, seq // TK, HEAD_DIM + V_ONES_ROWS, TK), BF16),
            jax.ShapeDtypeStruct((gate_rows, n), F32),
            jax.ShapeDtypeStruct((n, pool_w), F32),
        ],
        scratch_shapes=[
            pltpu.VMEM((4, tm, HEAD_DIM), F32),
            pltpu.VMEM((2, W_STAGE_ROWS, d), F32),
            pltpu.VMEM((q_w, d), BF16), pltpu.VMEM((2 * kv_w, d), BF16), pltpu.VMEM((gate_rows, d), BF16),
            pltpu.VMEM((2 * kv_w, d), BF16), pltpu.VMEM((2 * kv_w, d), BF16), pltpu.VMEM((pool_w, d), BF16),
            pltpu.SemaphoreType.DMA((2,)),
        ],
        compiler_params=_cparams(1),
        name="inproj",
    )(x2d, ln, ang_t, w_in_t)


def _compress_kernel(t_ref, pe_ref, w1_ref, w2_ref, o_ref):
    for j in range(t_ref.shape[1]):
        kind = j // N_KV
        t2 = t_ref[0, j]
        w1 = w1_ref[kind]
        a = _dot(t2, w1)
        pb = _dot(pe_ref[kind], w1)
        bias = pb[0:1, :CMP_HIDDEN] + pb[1:2, CMP_HIDDEN:]
        nrow = a.shape[0]
        nxt = pltpu.roll(a[:, CMP_HIDDEN:], nrow - 1, 0)
        hid = jax.nn.gelu(a[:, :CMP_HIDDEN] + nxt + bias)
        out = _dot(hid.astype(BF16), w2_ref[kind])
        row = lax.broadcasted_iota(I32, out.shape, 0)
        out = jnp.where(row < nrow - 1, out, 0.0)
        o_ref[0, j] = (out.T if kind else out).astype(BF16)


def _compress(kvc2, pe2, w1, w2):
    batch, four, nrow, width = kvc2.shape
    whole = lambda a: pl.BlockSpec(a.shape, lambda b: (0,) * a.ndim)
    return pl.pallas_call(
        _compress_kernel,
        grid=(batch,),
        in_specs=[pl.BlockSpec((1, four, nrow, width), lambda b: (b, 0, 0, 0)), whole(pe2), whole(w1), whole(w2)],
        out_specs=pl.BlockSpec((1, four, nrow, HEAD_DIM), lambda b: (b, 0, 0, 0)),
        out_shape=jax.ShapeDtypeStruct((batch, four, nrow, HEAD_DIM), BF16),
        compiler_params=_cparams(1),
        name="compress",
    )(kvc2, pe2, w1, w2)


def _nsa_kernel(qnt_ref, qrt_ref, cmp_ref, kk_ref, vt_ref, kblk_ref, gt_ref, gn_ref, o_ref,
                m_scr, acc_scr, qa_scr, ocmp_scr, s_scr, o_scr):
    qi = pl.program_id(1)
    tq = qnt_ref.shape[3]
    rows = GQA * tq
    n_cmp_pad = cmp_ref.shape[2]
    n_sel = kk_ref.shape[2] // SEL_LEN
    q0 = qi * tq

    m_scr[...] = jnp.full(m_scr.shape, NEG, F32)
    acc_scr[...] = jnp.zeros(acc_scr.shape, F32)

    def scores(item, slot):
        c, g, q_rows, k_tile, v_t, bias = item
        s_scr[slot] = _dot(k_tile, qa_scr[g, :q_rows, :])

    def softmax_pv(item, slot):
        c, g, q_rows, k_tile, v_t, bias = item
        s = s_scr[slot]
        if bias is not None:
            s = s + bias
        m_old = m_scr[c]
        m_new = jnp.maximum(m_old, jnp.max(s, axis=0, keepdims=True))
        alpha = jnp.exp2(m_old - m_new)
        p = jnp.exp2(s - m_new)
        acc_scr[c] = alpha * acc_scr[c] + _dot(v_t, p.astype(BF16))
        m_scr[c] = m_new

    def flash_out(c):
        return acc_scr[c, :HEAD_DIM, :] / acc_scr[c, HEAD_DIM:HEAD_DIM + 1, :]

    def tile_heads(a):
        return jnp.concatenate([a] * GQA, axis=1)

    kk_i = lax.broadcasted_iota(I32, (TK, tq), 0)
    qq_i = lax.broadcasted_iota(I32, (TK, tq), 1)
    causal_bias = tile_heads(jnp.where(kk_i <= qq_i, 0.0, NEG))
    band_bias = tile_heads(jnp.where(kk_i > qq_i, 0.0, NEG))
    n_win = WINDOW // TK

    for g in range(N_KV):
        qn4 = jnp.concatenate([qnt_ref[0, h] for h in range(g * GQA, (g + 1) * GQA)], axis=1)
        s = _dot(cmp_ref[0, g], qn4)
        n_idx = lax.broadcasted_iota(I32, (n_cmp_pad, rows), 0)
        t_idx = q0 + jnp.bitwise_and(lax.broadcasted_iota(I32, (n_cmp_pad, rows), 1), tq - 1)
        cmp_ok = n_idx * CMP_STRIDE + (CMP_LEN - 1) <= t_idx
        s = jnp.where(cmp_ok, s, NEG)
        m = jnp.max(s, axis=0, keepdims=True)
        e = jnp.where(cmp_ok, jnp.exp2(s - m), 0.0)
        l = jnp.sum(e, axis=0, keepdims=True)
        p = e * jnp.where(l > 0.0, 1.0 / l, 0.0)
        pb = p.astype(BF16)
        ocmp_scr[g] = _dot(cmp_ref[0, N_KV + g], pb)

        jn = lax.broadcasted_iota(I32, (n_sel, n_cmp_pad), 0) * SEL_LEN
        cn = lax.broadcasted_iota(I32, (n_sel, n_cmp_pad), 1) * CMP_STRIDE
        ov = jnp.clip(jnp.minimum(cn + CMP_LEN, jn + SEL_LEN) - jnp.maximum(cn, jn), 0, None)
        w_t = (ov.astype(F32) * (1.0 / CMP_LEN)).astype(BF16)
        imp4 = _dot(w_t, pb)
        imp = imp4[:, 0:tq]
        for r in range(1, GQA):
            imp = imp + imp4[:, r * tq:(r + 1) * tq]
        j_blk = lax.broadcasted_iota(I32, (n_sel, tq), 0)
        t_q = q0 + lax.broadcasted_iota(I32, (n_sel, tq), 1)
        sel_ok = j_blk * SEL_LEN <= t_q
        back = t_q // SEL_LEN - j_blk
        forced = (j_blk == 0) | ((back >= 0) & (back < N_LOCAL))
        val = jnp.where(sel_ok & forced, BIG, jnp.where(sel_ok, imp, -BIG))
        rank = jnp.zeros((n_sel, tq), F32)
        for i in range(n_sel):
            vi = val[i:i + 1, :]
            beats = (vi > val) | ((vi == val) & (j_blk > i))
            rank = rank + beats.astype(F32)
        keep = (rank < float(min(SEL_TOPK, n_sel))) & sel_ok
        bias_t = jnp.where(keep, 0.0, -SEL_MASK_BIAS)
        bias_t = jnp.concatenate([bias_t, jnp.zeros((HEAD_DIM - n_sel, tq), F32)], axis=0)
        qa_scr[g, :HEAD_DIM, :] = jnp.concatenate(
            [qrt_ref[0, h] for h in range(g * GQA, (g + 1) * GQA)], axis=1)
        qa_scr[g, HEAD_DIM:, :] = tile_heads(bias_t.astype(BF16))

    def sel_item(g, kj, bias):
        st = pl.multiple_of(kj * TK, TK)
        k_aug = jnp.concatenate([kk_ref[0, g, pl.ds(st, TK), :], kblk_ref[pl.ds(st, TK), :]], axis=1)
        return (g, g, 2 * HEAD_DIM, k_aug, vt_ref[0, g, kj], bias)

    def win_item(g, back):
        kj = qi - back
        st = pl.multiple_of(kj * TK, TK)
        bias = causal_bias if back == 0 else (band_bias if back == n_win else None)
        return (N_KV + g, g, HEAD_DIM, kk_ref[0, N_KV + g, pl.ds(st, TK), :], vt_ref[0, N_KV + g, kj], bias)

    assert N_KV == 2
    scores(sel_item(0, 0, None), 0)

    def sel_tile(kj):
        scores(sel_item(1, kj, None), 1)
        softmax_pv(sel_item(0, kj, None), 0)
        scores(sel_item(0, kj + 1, None), 0)
        softmax_pv(sel_item(1, kj, None), 1)

    def sel_pair(j, carry):
        sel_tile(2 * j)
        sel_tile(2 * j + 1)
        return carry

    lax.fori_loop(0, qi // 2, sel_pair, 0)

    @pl.when(qi % 2 == 1)
    def _():
        sel_tile(qi - 1)

    def run_tail(farthest):
        tail = ([sel_item(g, qi, causal_bias) for g in range(N_KV)]
                + [win_item(g, back) for back in range(farthest, -1, -1) for g in range(N_KV)])
        for i, item in enumerate(tail):
            if i + 1 < len(tail):
                scores(tail[i + 1], (i + 1) % 2)
            softmax_pv(item, i % 2)

    for nb in range(n_win):
        @pl.when(qi == nb)
        def _(nb=nb):
            run_tail(nb)

    @pl.when(qi >= n_win)
    def _():
        run_tail(n_win)

    gsig = jax.nn.sigmoid(gt_ref[...])
    for g in range(N_KV):
        o_cmp, o_sel, o_win = ocmp_scr[g], flash_out(g), flash_out(N_KV + g)
        for r in range(GQA):
            h = g * GQA + r
            sl = slice(r * tq, (r + 1) * tq)
            o_h = (gsig[3 * h:3 * h + 1, :] * o_cmp[:, sl]
                   + gsig[3 * h + 1:3 * h + 2, :] * o_sel[:, sl]
                   + gsig[3 * h + 2:3 * h + 3, :] * o_win[:, sl])
            o_scr[:, h * HEAD_DIM:(h + 1) * HEAD_DIM] = o_h.T

    o = o_scr[...]
    ms = jnp.mean(o * o, axis=-1, keepdims=True)
    o_ref[...] = (o * lax.rsqrt(ms + EPS) * gn_ref[...]).astype(BF16)


def _nsa(qnt, qrt, cmp, kk, vt, kblk, gt, gn):
    batch, _, _, seq = qnt.shape
    assert TQ == TK and WINDOW % TK == 0 and TQ & (TQ - 1) == 0
    nq = seq // TQ
    width = N_HEADS * HEAD_DIM
    return pl.pallas_call(
        _nsa_kernel,
        grid=(batch, nq),
        in_specs=[
            pl.BlockSpec((1, N_HEADS, HEAD_DIM, TQ), lambda b, i: (b, 0, 0, i)),
            pl.BlockSpec((1, N_HEADS, HEAD_DIM, TQ), lambda b, i: (b, 0, 0, i)),
            pl.BlockSpec((1,) + cmp.shape[1:], lambda b, i: (b, 0, 0, 0)),
            pl.BlockSpec((1,) + kk.shape[1:], lambda b, i: (b, 0, 0, 0)),
            pl.BlockSpec((1,) + vt.shape[1:], lambda b, i: (b, 0, 0, 0, 0)),
            pl.BlockSpec((seq, LANES), lambda b, i: (0, 0)),
            pl.BlockSpec((gt.shape[0], TQ), lambda b, i: (0, b * nq + i)),
            pl.BlockSpec((1, width), lambda b, i: (0, 0)),
        ],
        out_specs=pl.BlockSpec((TQ, width), lambda b, i: (b * nq + i, 0)),
        out_shape=jax.ShapeDtypeStruct((batch * seq, width), BF16),
        scratch_shapes=[
            pltpu.VMEM((2 * N_KV, 1, GQA * TQ), F32),
            pltpu.VMEM((2 * N_KV, vt.shape[3], GQA * TQ), F32),
            pltpu.VMEM((N_KV, 2 * HEAD_DIM, GQA * TQ), BF16),
            pltpu.VMEM((N_KV, HEAD_DIM, GQA * TQ), F32),
            pltpu.VMEM((2, TK, GQA * TQ), F32),
            pltpu.VMEM((TQ, width), F32),
        ],
        compiler_params=_cparams(2),
        name="nsa",
    )(qnt, qrt, cmp, kk, vt, kblk, gt, gn)


def _pool_group(cur, halo_rows, t1, window, w_mat, bias, scale):
    halo = halo_rows.shape[0]
    acc = jnp.concatenate([halo_rows, cur], axis=0)
    span = 1
    while span < window:
        acc = acc + jnp.concatenate([jnp.zeros((span, acc.shape[1]), F32), acc[:-span]], axis=0)
        span *= 2
    mean = acc[halo:] / jnp.minimum(t1, float(window))
    return (_dot((mean - cur).astype(BF16), w_mat) + bias) * scale


def _pool_kernel(u_ref, w_ref, b_ref, sc_ref, gn_ref, o_ref, y_scr, halo_scr):
    si = pl.program_id(1)
    tm = u_ref.shape[1]
    halo = halo_scr.shape[0]

    @pl.when(si == 0)
    def _():
        halo_scr[...] = jnp.zeros(halo_scr.shape, F32)

    t1 = (si * tm + 1 + lax.broadcasted_iota(I32, (tm, 1), 0)).astype(F32)
    cg = w_ref.shape[1]
    for gi, window in enumerate(POOL_SIZES):
        sl = slice(gi * cg, (gi + 1) * cg)
        cur = u_ref[0, :, sl]
        y_scr[:, sl] = _pool_group(cur, halo_scr[:, sl], t1, window, w_ref[gi], b_ref[:, sl], sc_ref[:, sl])
        halo_scr[:, sl] = cur[tm - halo:, :]
    y = y_scr[...]
    ms = jnp.mean(y * y, axis=-1, keepdims=True)
    o_ref[...] = (y * lax.rsqrt(ms + EPS) * gn_ref[...]).astype(BF16)


def _pool(u3, w_pool, b_pool, pool_scale, gn_pool):
    batch, seq, c = u3.shape
    tm = TM_PROJ
    ns = seq // tm
    vec = lambda b, i: (0, 0)
    return pl.pallas_call(
        _pool_kernel,
        grid=(batch, ns),
        in_specs=[
            pl.BlockSpec((1, tm, c), lambda b, i: (b, i, 0)),
            pl.BlockSpec(w_pool.shape, lambda b, i: (0, 0, 0)),
            pl.BlockSpec((1, c), vec),
            pl.BlockSpec((1, c), vec),
            pl.BlockSpec((1, c), vec),
        ],
        out_specs=pl.BlockSpec((tm, c), lambda b, i: (b * ns + i, 0)),
        out_shape=jax.ShapeDtypeStruct((batch * seq, c), BF16),
        scratch_shapes=[pltpu.VMEM((tm, c), F32), pltpu.VMEM((max(POOL_SIZES), c), F32)],
        compiler_params=_cparams(2),
        name="pool",
    )(u3, w_pool, b_pool, pool_scale, gn_pool)


def _outproj_kernel(mn_ref, mp_ref, x_ref, w_hbm, ln_ref, wr_ref, br_ref,
                    x1_ref, h2_ref, eid_ref, gate_ref, x1_scr, ssq_scr, stage, w_ref, wsem):
    tm, d = x_ref.shape
    half = mn_ref.shape[1]
    chunk = 512

    @pl.when(pl.program_id(0) == 0)
    def _():
        x1_scr[...] = jnp.zeros(x1_scr.shape, F32)
        ssq_scr[...] = jnp.zeros(ssq_scr.shape, F32)
        rc = stage.shape[1]
        n_chunks = w_ref.shape[0] // rc

        def chunk_copy(c, slot):
            return pltpu.make_async_copy(w_hbm.at[0, pl.ds(c * rc, rc)], stage.at[slot], wsem.at[slot])

        chunk_copy(0, 0).start()
        for c in range(n_chunks):
            slot = c % 2
            if c + 1 < n_chunks:
                chunk_copy(c + 1, 1 - slot).start()
            chunk_copy(c, slot).wait()
            w_ref[c * rc:(c + 1) * rc, :] = stage[slot].astype(BF16)

    h2 = x1_scr[...] * lax.rsqrt(ssq_scr[...] * (1.0 / d) + EPS) * ln_ref[...]
    _store_token_tiles(h2_ref, _pack_bf16_halves(h2))

    logits = _dot_nt(wr_ref[...], h2.astype(BF16)) + br_ref[:, 0:1]
    ng, ne = N_EXPERT_GROUPS, EXPERTS_PER_GROUP
    row = lax.broadcasted_iota(I32, (ne, tm), 0)
    lg = jnp.where(row < ng, logits[0:ne], NEG)
    mg = jnp.max(lg, axis=0, keepdims=True)
    g_sel = jnp.min(jnp.where(lg == mg, row, ne), axis=0, keepdims=True)
    p_g = 1.0 / jnp.sum(jnp.exp(lg - mg), axis=0, keepdims=True)
    le = jnp.zeros((ne, tm), F32)
    for g in range(ng):
        le = jnp.where(g_sel == g, logits[ne * (g + 1):ne * (g + 2)], le)
    ex = jnp.exp(le - jnp.max(le, axis=0, keepdims=True))
    pe = ex / jnp.sum(ex, axis=0, keepdims=True)
    p1 = jnp.max(pe, axis=0, keepdims=True)
    i1 = jnp.min(jnp.where(pe == p1, row, ne), axis=0, keepdims=True)
    rest = jnp.where(row == i1, -1.0, pe)
    p2 = jnp.max(rest, axis=0, keepdims=True)
    i2 = jnp.min(jnp.where(rest == p2, row, ne), axis=0, keepdims=True)
    psum = p1 + p2
    zero_i = jnp.zeros((ne, tm), I32)
    eid = jnp.where(row == 0, g_sel * ne + i1, jnp.where(row == 1, g_sel * ne + i2, zero_i))
    gate = jnp.where(row == 0, p_g * p1 / psum, jnp.where(row == 1, p_g * p2 / psum, 0.0))
    eid_ref[...] = eid
    gate_ref[...] = gate

    mn = mn_ref[...]
    mp = mp_ref[...]
    ssq = jnp.zeros((tm, 1), F32)
    for c in range(d // chunk):
        sl = slice(c * chunk, (c + 1) * chunk)
        x1 = x_ref[:, sl] + (_dot(mn, w_ref[:half, sl]) + _dot(mp, w_ref[half:, sl]))
        x1_ref[:, sl] = x1
        x1_scr[:, sl] = x1
        ssq = ssq + jnp.sum(x1 * x1, axis=-1, keepdims=True)
    ssq_scr[...] = ssq


def _outproj(mix_nsa, mix_pool, x2d, w_out, ln_moe, w_router_t, b_router):
    n, d = x2d.shape
    tm = TM_PROJ
    half = mix_nsa.shape[1]
    const = lambda i: (0, 0)
    last = n // tm - 1
    cur = lambda i: (jnp.minimum(i, last), 0)
    prev = lambda i: (jnp.maximum(i - 1, 0), 0)
    prev_t = lambda i: (0, jnp.maximum(i - 1, 0))
    return pl.pallas_call(
        _outproj_kernel,
        grid=(n // tm + 1,),
        in_specs=[
            pl.BlockSpec((tm, half), cur),
            pl.BlockSpec((tm, half), cur),
            pl.BlockSpec((tm, d), cur),
            pl.BlockSpec(memory_space=pl.ANY),
            pl.BlockSpec((1, d), const),
            pl.BlockSpec(w_router_t.shape, const),
            pl.BlockSpec(b_router.shape, const),
        ],
        out_specs=[
            pl.BlockSpec((tm, d), cur),
            pl.BlockSpec((tm * SUBLANES, LANES), prev),
            pl.BlockSpec((SUBLANES, tm), prev_t),
            pl.BlockSpec((SUBLANES, tm), prev_t),
        ],
        scratch_shapes=[pltpu.VMEM((tm, d), F32), pltpu.VMEM((tm, 1), F32),
                        pltpu.VMEM((2, W_STAGE_ROWS, d), F32), pltpu.VMEM(w_out.shape[1:], BF16),
                        pltpu.SemaphoreType.DMA((2,))],
        out_shape=[
            jax.ShapeDtypeStruct((n, d), F32),
            jax.ShapeDtypeStruct((n * SUBLANES, LANES), U32),
            jax.ShapeDtypeStruct((SUBLANES, n), I32),
            jax.ShapeDtypeStruct((SUBLANES, n), F32),
        ],
        compiler_params=_cparams(1),
        name="outproj",
    )(mix_nsa, mix_pool, x2d, w_out, ln_moe, w_router_t, b_router)


def _dispatch_kernel(eid_ref, dest_ref, meta_ref, rank_scr):
    n = eid_ref.shape[1]
    t = T_RANK
    row = lax.broadcasted_iota(I32, (N_EXPERTS, t), 0)
    before = (lax.broadcasted_iota(I32, (t, t), 0) < lax.broadcasted_iota(I32, (t, t), 1)).astype(BF16)
    carry = jnp.zeros((N_EXPERTS, 1), F32)
    for k in range(2):
        for j in range(n // t):
            sl = slice(j * t, (j + 1) * t)
            oh = row == eid_ref[k:k + 1, sl]
            ohf = oh.astype(F32)
            prior = _dot(ohf.astype(BF16), before) + carry
            rank_scr[k:k + 1, sl] = jnp.sum(jnp.where(oh, prior, 0.0), axis=0, keepdims=True)
            carry = carry + jnp.sum(ohf, axis=1, keepdims=True)
    counts = jnp.broadcast_to(carry, (N_EXPERTS, LANES))
    padded = jnp.floor((counts + (TM_MOE - 1.0)) * (1.0 / TM_MOE)) * TM_MOE
    ends = padded
    r_idx = lax.broadcasted_iota(I32, (N_EXPERTS, LANES), 0)
    sh = 1
    while sh < N_EXPERTS:
        ends = ends + jnp.where(r_idx >= sh, pltpu.roll(ends, sh, 0), 0.0)
        sh *= 2
    starts = (ends - padded)[:, 0:1]
    for k in range(2):
        for j in range(n // t):
            sl = slice(j * t, (j + 1) * t)
            oh = row == eid_ref[k:k + 1, sl]
            base = jnp.sum(jnp.where(oh, starts, 0.0), axis=0, keepdims=True)
            dest_ref[k:k + 1, sl] = (rank_scr[k:k + 1, sl] + base).astype(I32)
    for k in range(2, SUBLANES):
        dest_ref[k:k + 1, :] = jnp.zeros((1, n), I32)
    blk_start = lax.broadcasted_iota(I32, (N_EXPERTS, LANES), 1).astype(F32) * TM_MOE
    blk_expert = jnp.sum((ends <= blk_start).astype(F32), axis=0, keepdims=True)
    blk_expert = jnp.minimum(blk_expert, N_EXPERTS - 1.0)
    n_used = ends[N_EXPERTS - 1:N_EXPERTS, :] * (1.0 / TM_MOE)
    on_diag = r_idx == lax.broadcasted_iota(I32, (N_EXPERTS, LANES), 1)
    end_blk = jnp.sum(jnp.where(on_diag, ends, 0.0), axis=0, keepdims=True) * (1.0 / TM_MOE)
    n_blk = jnp.sum(jnp.where(on_diag, padded, 0.0), axis=0, keepdims=True) * (1.0 / TM_MOE)
    r8 = lax.broadcasted_iota(I32, (SUBLANES, LANES), 0)
    meta = jnp.where(r8 == 0, blk_expert,
                     jnp.where(r8 == 1, n_used, jnp.where(r8 == 2, end_blk, jnp.where(r8 == 3, n_blk, 0.0))))
    meta_ref[...] = meta.astype(I32)


def _dispatch(eid):
    n = eid.shape[1]
    return pl.pallas_call(
        _dispatch_kernel,
        out_shape=[jax.ShapeDtypeStruct((SUBLANES, n), I32),
                   jax.ShapeDtypeStruct((SUBLANES, LANES), I32)],
        scratch_shapes=[pltpu.VMEM((SUBLANES, n), F32)],
        compiler_params=pltpu.CompilerParams(vmem_limit_bytes=VMEM_LIMIT_BYTES),
        name="dispatch",
    )(eid)


def _row_copy(src_ref, src_row, dst_ref, dst_row, sem):
    tile = lambda ref, row: ref.at[pl.ds(pl.multiple_of(row * SUBLANES, SUBLANES), SUBLANES)]
    return pltpu.make_async_copy(tile(src_ref, src_row), tile(dst_ref, dst_row), sem)


def _scatter_kernel(dest_ref, meta_ref, h_ref, xs_ref, zero_buf, sem, zsem):
    tm = h_ref.shape[0] // SUBLANES
    blk = zero_buf.shape[0]

    @pl.when(pl.program_id(0) == 0)
    def _():
        zero_buf[...] = jnp.zeros(zero_buf.shape, zero_buf.dtype)
        n_used = meta_ref[1, 0]

        def zero_block(b):
            return pltpu.make_async_copy(zero_buf, xs_ref.at[pl.ds(pl.multiple_of(b * blk, blk), blk)], zsem)

        def sweep(act):
            def per_expert(e, c):
                @pl.when(meta_ref[3, e] > 0)
                def _():
                    act(zero_block(meta_ref[2, e] - 1))
                return c

            def per_tail(b, c):
                @pl.when(b >= n_used)
                def _():
                    act(zero_block(b))
                return c

            lax.fori_loop(0, N_EXPERTS, per_expert, 0)
            lax.fori_loop(0, xs_ref.shape[0] // blk, per_tail, 0)

        sweep(lambda cp: cp.start())
        sweep(lambda cp: cp.wait())

    grp = dest_ref.shape[2]
    n_grp = tm // grp

    def start(j, c):
        for u in range(grp):
            for k in range(2):
                _row_copy(h_ref, j * grp + u, xs_ref, dest_ref[0, k * n_grp + j, u], sem).start(priority=k)
        return c

    lax.fori_loop(0, n_grp, start, 0)
    for k in range(2):
        pltpu.make_async_copy(h_ref, xs_ref.at[pl.ds(0, tm * SUBLANES)], sem).wait()


def _grouped_dest(dest, tm, grp):
    n = dest.shape[1]
    return jnp.transpose(dest[:2].reshape(2, n // tm, tm // grp, grp), (1, 0, 2, 3)).reshape(n // tm, 2 * tm // grp, grp)


def _scatter(dest3, meta, h2, cap):
    nsteps, rows, grp = dest3.shape
    tm = rows * grp // 2
    return pl.pallas_call(
        _scatter_kernel,
        grid=(nsteps,),
        in_specs=[
            pl.BlockSpec((1, rows, grp), lambda i: (i, 0, 0), memory_space=pltpu.SMEM),
            pl.BlockSpec(memory_space=pltpu.SMEM),
            pl.BlockSpec((tm * SUBLANES, LANES), lambda i: (i, 0)),
        ],
        out_specs=pl.BlockSpec(memory_space=pl.ANY),
        out_shape=jax.ShapeDtypeStruct((cap * SUBLANES, LANES), h2.dtype),
        scratch_shapes=[pltpu.VMEM((TM_MOE * SUBLANES, LANES), h2.dtype), pltpu.SemaphoreType.DMA(()),
                        pltpu.SemaphoreType.DMA(())],
        compiler_params=_cparams(1),
        name="scatter",
    )(dest3, meta, h2)


def _experts_kernel(meta_ref, xs_ref, wg_hbm, wu_hbm, wd_hbm, ys_ref,
                    wg_buf, wu_buf, wd_buf, wg16, wu16, wd16, elist_ref, ord_ref, sem):
    c = pl.program_id(0)
    n_used = meta_ref[1, 0]
    e = meta_ref[0, c]
    first = (c == 0) | (e != meta_ref[0, jnp.maximum(c - 1, 0)])

    def fetch(expert, slot):
        return [pltpu.make_async_copy(w.at[expert], buf.at[slot], sem.at[slot, i])
                for i, (w, buf) in enumerate(((wg_hbm, wg_buf), (wu_hbm, wu_buf), (wd_hbm, wd_buf)))]

    @pl.when(c == 0)
    def _():
        def scan(b, cnt):
            eb = meta_ref[0, b]
            take = ((b == 0) | (eb != meta_ref[0, jnp.maximum(b - 1, 0)])) & (b < n_used)

            @pl.when(take)
            def _():
                elist_ref[cnt] = eb

            return cnt + take.astype(I32)

        cnt = lax.fori_loop(0, pl.num_programs(0), scan, 0)
        ord_ref[0] = 0
        ord_ref[1] = cnt
        for k in range(W_SLOTS):
            @pl.when(k < cnt)
            def _(k=k):
                for cp, prio in zip(fetch(elist_ref[k], k), WEIGHT_DMA_PRIORITIES):
                    cp.start(priority=prio)

    @pl.when(first & (c < n_used))
    def _():
        @pl.when(c > 0)
        def _():
            ord_ref[0] = ord_ref[0] + 1
        for cp in fetch(e, ord_ref[0] % W_SLOTS):
            cp.wait()

    def mlp(weights):
        lo, hi = _unpack_bf16_halves(_load_token_tiles(xs_ref))
        lo, hi = lo.astype(BF16), hi.astype(BF16)
        half = lo.shape[1]
        wg, wu, wd = weights

        def proj(w):
            return _dot(lo, w[:half, :]) + _dot(hi, w[half:, :])

        act = (jax.nn.silu(proj(wg)) * proj(wu)).astype(BF16)
        _store_token_tiles(ys_ref, _pack_bf16_halves(_dot(act, wd)))

    @pl.when(first & (c < n_used))
    def _():
        k = ord_ref[0]
        slot = k % W_SLOTS
        cast = [buf[slot].astype(BF16) for buf in (wg_buf, wu_buf, wd_buf)]
        for ref16, w in zip((wg16, wu16, wd16), cast):
            ref16[...] = w
        mlp(cast)
        ahead = k + W_SLOTS

        @pl.when(ahead < ord_ref[1])
        def _():
            for cp, prio in zip(fetch(elist_ref[jnp.minimum(ahead, N_EXPERTS - 1)], slot), WEIGHT_DMA_PRIORITIES):
                cp.start(priority=prio)

    @pl.when(jnp.logical_not(first) & (c < n_used))
    def _():
        mlp((wg16[...], wu16[...], wd16[...]))

    @pl.when(c >= n_used)
    def _():
        ys_ref[...] = jnp.zeros(ys_ref.shape, U32)


def _experts(meta, xs, w_gate, w_up, w_down):
    rows = TM_MOE * SUBLANES
    nblk = xs.shape[0] // rows
    d, ff = w_gate.shape[1:]

    def row_map(c, meta):
        return (jnp.minimum(c, meta[1, 0] - 1), 0)

    hbm = pl.BlockSpec(memory_space=pl.ANY)
    return pl.pallas_call(
        _experts_kernel,
        grid_spec=pltpu.PrefetchScalarGridSpec(
            num_scalar_prefetch=1,
            grid=(nblk,),
            in_specs=[pl.BlockSpec((rows, LANES), row_map), hbm, hbm, hbm],
            out_specs=pl.BlockSpec((rows, LANES), lambda c, meta: (c, 0)),
            scratch_shapes=[
                pltpu.VMEM((W_SLOTS, d, ff), F32), pltpu.VMEM((W_SLOTS, d, ff), F32),
                pltpu.VMEM((W_SLOTS, ff, d), F32),
                pltpu.VMEM((d, ff), BF16), pltpu.VMEM((d, ff), BF16), pltpu.VMEM((ff, d), BF16),
                pltpu.SMEM((N_EXPERTS,), I32),
                pltpu.SMEM((2,), I32),
                pltpu.SemaphoreType.DMA((W_SLOTS, 3)),
            ],
        ),
        out_shape=jax.ShapeDtypeStruct(xs.shape, U32),
        compiler_params=_cparams(1),
        name="experts",
    )(meta, xs, w_gate, w_up, w_down)


def _combine_kernel(dest_ref, dest_next_ref, x1_ref, gate_ref, ln_ref, ys_ref, o_ref, ybuf, sem):
    i = pl.program_id(0)
    tm = x1_ref.shape[0]
    slot = i % 2
    grp = dest_ref.shape[2]
    n_grp = tm // grp

    def issue_group(d_ref, s, j):
        for u in range(grp):
            for k in range(2):
                _row_copy(ys_ref, d_ref[0, k * n_grp + j, u], ybuf.at[s, k], j * grp + u, sem.at[s]).start(priority=k)

    @pl.when(i == 0)
    def _():
        def start(j, c):
            issue_group(dest_ref, 0, j)
            return c

        lax.fori_loop(0, n_grp, start, 0)

    for k in range(2):
        pltpu.make_async_copy(ys_ref.at[pl.ds(0, tm * SUBLANES)], ybuf.at[slot, k], sem.at[slot]).wait()
    ln = ln_ref[...]

    def finish(j):
        r0 = pl.multiple_of(j * grp, grp)

        def expert_rows(k):
            words = jnp.concatenate(
                [ybuf[slot, k, pl.ds(r0 * SUBLANES + c, grp, stride=SUBLANES), :] for c in range(SUBLANES)], axis=1)
            return _unpack_bf16_halves(words)

        (lo0, hi0), (lo1, hi1) = expert_rows(0), expert_rows(1)
        gate = gate_ref[pl.ds(r0, grp), :]
        g0, g1 = gate[:, 0:1], gate[:, 1:2]
        moe = jnp.concatenate([lo0 * g0 + lo1 * g1, hi0 * g0 + hi1 * g1], axis=1)
        x2 = x1_ref[pl.ds(r0, grp), :] + moe
        ms = jnp.mean(x2 * x2, axis=-1, keepdims=True)
        o_ref[pl.ds(r0, grp), :] = x2 * lax.rsqrt(ms + EPS) * ln

    def finish_and_prefetch(j, carry):
        finish(j)
        issue_group(dest_next_ref, 1 - slot, j)
        return carry

    def finish_only(j, carry):
        finish(j)
        return carry

    has_next = i + 1 < pl.num_programs(0)

    @pl.when(has_next)
    def _():
        lax.fori_loop(0, tm // grp, finish_and_prefetch, 0)

    @pl.when(jnp.logical_not(has_next))
    def _():
        lax.fori_loop(0, tm // grp, finish_only, 0)


def _combine(dest3, x1, gate_rows, ln_final, ys):
    n, d = x1.shape
    tm = TM_COMBINE
    last = n // tm - 1
    return pl.pallas_call(
        _combine_kernel,
        grid=(n // tm,),
        in_specs=[
            pl.BlockSpec((1,) + dest3.shape[1:], lambda i: (i, 0, 0), memory_space=pltpu.SMEM),
            pl.BlockSpec((1,) + dest3.shape[1:], lambda i: (jnp.minimum(i + 1, last), 0, 0), memory_space=pltpu.SMEM),
            pl.BlockSpec((tm, d), lambda i: (i, 0)),
            pl.BlockSpec((tm, LANES), lambda i: (i, 0)),
            pl.BlockSpec((1, d), lambda i: (0, 0)),
            pl.BlockSpec(memory_space=pl.ANY),
        ],
        out_specs=pl.BlockSpec((tm, d), lambda i: (i, 0)),
        out_shape=jax.ShapeDtypeStruct((n, d), F32),
        scratch_shapes=[pltpu.VMEM((2, 2, tm * SUBLANES, LANES), U32), pltpu.SemaphoreType.DMA((2,))],
        compiler_params=_cparams(1),
        name="combine",
    )(dest3, dest3, x1, gate_rows, ln_final, ys)


def kernel(x, positions, ln_mix, w_in, pe_cmp_k, w_cmp_k1, w_cmp_k2, pe_cmp_v, w_cmp_v1, w_cmp_v2, w_pool, b_pool, pool_scale, gn_nsa, gn_pool, w_out, ln_moe, w_router_group, b_router_group, w_router_expert, b_router_expert, w_gate, w_up, w_down, ln_final):
    batch, seq, d = x.shape
    n = batch * seq
    depth = w_in.shape[0]

    inv_freq = ROPE_THETA ** (-jnp.arange(0, ROT_DIM, 2, dtype=F32) / ROT_DIM)
    ang_t = inv_freq[:, None] * positions.astype(F32).reshape(1, n)
    kblk = (jnp.arange(seq, dtype=I32)[:, None] // SEL_LEN == jnp.arange(LANES, dtype=I32)[None, :]).astype(BF16)

    x2d = x.reshape(n, d)
    assert depth == 1, "single-layer operation"
    for l in range(depth):
        gate_rows_pad = -(-N_GATES // SUBLANES) * SUBLANES
        qnt, qrt, kvc, kk, vt, gt, u = _inproj(x2d, ln_mix[l][None], ang_t, jnp.swapaxes(w_in, 1, 2),
                                               gate_rows_pad, batch, seq)

        half = CMP_LEN * HEAD_DIM // 2
        def stack_w1(w):
            return jnp.concatenate([w[:half], w[half:]], axis=1)
        w1 = jnp.stack([stack_w1(w_cmp_k1[l]), stack_w1(w_cmp_v1[l])]).astype(BF16)
        w2 = jnp.stack([w_cmp_k2[l], w_cmp_v2[l]]).astype(BF16)
        def pe_rows(pe):
            return jnp.pad(pe.reshape(2, half), ((0, SUBLANES - 2), (0, 0)))
        pe2 = jnp.stack([pe_rows(pe_cmp_k[l]), pe_rows(pe_cmp_v[l])]).astype(BF16)
        cmp = _compress(kvc, pe2, w1, w2)

        mix_nsa = _nsa(qnt, qrt, cmp, kk, vt, kblk, gt, gn_nsa[l][None])
        mix_pool = _pool(u.reshape(batch, seq, -1), w_pool[l].astype(BF16), b_pool[l][None],
                         pool_scale[l][None], gn_pool[l][None])

        ng, ne = N_EXPERT_GROUPS, EXPERTS_PER_GROUP
        w_r = jnp.concatenate([
            w_router_group[l].T, jnp.zeros((ne - ng, d), F32),
            jnp.transpose(w_router_expert[l], (0, 2, 1)).reshape(ng * ne, d),
            jnp.zeros((LANES - ne - ng * ne, d), F32)], axis=0).astype(BF16)
        b_r = jnp.concatenate([
            b_router_group[l], jnp.zeros((ne - ng,), F32), b_router_expert[l].reshape(-1),
            jnp.zeros((LANES - ne - ng * ne,), F32)])
        b_r = jnp.broadcast_to(b_r[:, None], (LANES, LANES))
        x1, h2, eid, gate = _outproj(mix_nsa, mix_pool, x2d, w_out, ln_moe[l][None], w_r, b_r)

        dest, meta = _dispatch(eid)
        cap = (2 * n // TM_MOE + N_EXPERTS) * TM_MOE
        xs = _scatter(_grouped_dest(dest, TS_SCATTER, DMA_ISSUE_GROUP), meta, h2, cap)
        ys = _experts(meta, xs, w_gate[l], w_up[l], w_down[l])
        dest_tok = _grouped_dest(dest, TM_COMBINE, DMA_ISSUE_GROUP)
        gate_rows = jnp.pad(gate[:2].T, ((0, 0), (0, LANES - 2)))
        x2d = _combine(dest_tok, x1, gate_rows, ln_final[None], ys)
    return x2d.reshape(batch, seq, d)
```

```python
import jax
import jax.numpy as jnp
from jax import lax
from jax.experimental import pallas as pl
from jax.experimental.pallas import tpu as pltpu

F32 = jnp.float32
BF16 = jnp.bfloat16
I32 = jnp.int32

HEAD_DIM = 128
N_HEADS = 8
N_KV = 2
GQA = N_HEADS // N_KV
N_GATES = 3 * N_HEADS
ROT_DIM = HEAD_DIM // 4
ROPE_THETA = 500000.0
CMP_LEN = 32
CMP_STRIDE = 16
CMP_HIDDEN = 2 * HEAD_DIM
SEL_LEN = 64
SEL_TOPK = 16
N_LOCAL = 2
WINDOW = 512
POOL_SIZES = (2, 4, 8, 16)
N_EXPERT_GROUPS = 4
EXPERTS_PER_GROUP = 8
N_EXPERTS = N_EXPERT_GROUPS * EXPERTS_PER_GROUP
EPS = 1e-6
NEG = -1e30
BIG = 1e30
SCALE = HEAD_DIM ** -0.5
LOG2E = 1.4426950408889634
Q_SCALE = SCALE * LOG2E

LANES = 128
SUBLANES = 8
VMEM_LIMIT_BYTES = 56 * 1024 * 1024

TM_PROJ = 512
TQ = 256
TK = 256
SEL_MASK_BIAS = 2.0 ** 40
V_ONES_ROWS = 16
TM_MOE = 256
TS_SCATTER = 2048
TM_COMBINE = 512
WEIGHT_DMA_PRIORITIES = (0, 1, 1)
W_STAGE_ROWS = 256
W_SLOTS = 3
DMA_ISSUE_GROUP = 64
T_RANK = 512


def _cparams(n_axes):
    return pltpu.CompilerParams(dimension_semantics=("arbitrary",) * n_axes, vmem_limit_bytes=VMEM_LIMIT_BYTES)


def _dot(a, b):
    return jnp.dot(a, b, preferred_element_type=F32)


def _dot_nt(a, b):
    return lax.dot_general(a, b, (((1,), (1,)), ((), ())), preferred_element_type=F32)


U32 = jnp.uint32
_HI16 = 0xFFFF0000


def _pack_bf16_halves(x):
    c = x.shape[1] // 2
    bits = lambda t: lax.bitcast_convert_type(t.astype(BF16).astype(F32), U32)
    return lax.shift_right_logical(bits(x[:, :c]), U32(16)) | (bits(x[:, c:]) & U32(_HI16))


def _store_token_tiles(ref, words):
    m, width = words.shape
    assert width == SUBLANES * LANES
    for j in range(SUBLANES):
        ref[pl.ds(j, m, stride=SUBLANES), :] = words[:, j * LANES:(j + 1) * LANES]


def _load_token_tiles(ref):
    m = ref.shape[0] // SUBLANES
    return jnp.concatenate([ref[pl.ds(j, m, stride=SUBLANES), :] for j in range(SUBLANES)], axis=1)


def _unpack_bf16_halves(w):
    lo = lax.bitcast_convert_type(lax.shift_left(w, U32(16)), F32)
    hi = lax.bitcast_convert_type(w & U32(_HI16), F32)
    return lo, hi


def _rope(t, cos, sin, axis):
    idx = lax.broadcasted_iota(I32, t.shape, axis)
    partner = jnp.where(idx < ROT_DIM // 2,
                        pltpu.roll(t, HEAD_DIM - ROT_DIM // 2, axis),
                        pltpu.roll(t, ROT_DIM // 2, axis))
    return t * cos + partner * sin


def _inproj_kernel(x_ref, ln_ref, ang_ref, wt_hbm,
                   qnt_ref, qrt_ref, kvc_ref, kk_ref, vt_ref, gt_ref, u_ref,
                   kc_scr, stage, wq_ref, wv_ref, wg_ref, wkc_ref, wk_ref, wu_ref, wsem):
    @pl.when(pl.program_id(0) == 0)
    def _():
        rc = stage.shape[1]
        q_w, kv_w = N_HEADS * HEAD_DIM, N_KV * HEAD_DIM
        o_kc, o_ks = q_w, q_w + 2 * kv_w
        o_vs, o_kw, o_vw, o_g = o_ks + kv_w, o_ks + 2 * kv_w, o_ks + 3 * kv_w, o_ks + 4 * kv_w
        o_u = o_g + N_GATES
        pieces = ([(r0, rc, wq_ref, r0) for r0 in range(0, q_w, rc)]
                  + [(o_kc + r0, rc, wkc_ref, r0) for r0 in range(0, 2 * kv_w, rc)]
                  + [(o_ks, kv_w, wk_ref, 0), (o_kw, kv_w, wk_ref, kv_w)]
                  + [(o_vs, kv_w, wv_ref, 0), (o_vw, kv_w, wv_ref, kv_w)]
                  + [(o_g, N_GATES, wg_ref, 0)]
                  + [(o_u + r0, rc, wu_ref, r0) for r0 in range(0, wu_ref.shape[0], rc)])
        assert all(n <= rc and src % SUBLANES == 0 for src, n, _, _ in pieces)
        wg_ref[...] = jnp.zeros(wg_ref.shape, BF16)

        def piece_copy(i, slot):
            src, rows, _, _ = pieces[i]
            return pltpu.make_async_copy(wt_hbm.at[0, pl.ds(src, rows)], stage.at[slot, pl.ds(0, rows)],
                                         wsem.at[slot])

        piece_copy(0, 0).start()
        for i, (src, rows, dst, dst_row) in enumerate(pieces):
            slot = i % 2
            if i + 1 < len(pieces):
                piece_copy(i + 1, 1 - slot).start()
            piece_copy(i, slot).wait()
            dst[dst_row:dst_row + rows, :] = stage[slot, :rows, :].astype(BF16)

    x = x_ref[...]
    ms = jnp.mean(x * x, axis=-1, keepdims=True)
    hb = (x * lax.rsqrt(ms + EPS) * ln_ref[...]).astype(BF16)
    chunk = 4 * HEAD_DIM

    ang = ang_ref[...]
    ca, sa = jnp.cos(ang), jnp.sin(ang)
    rest = (HEAD_DIM - ROT_DIM, ang.shape[1])
    cost = jnp.concatenate([ca, ca, jnp.ones(rest, F32)], axis=0)
    sint = jnp.concatenate([-sa, sa, jnp.zeros(rest, F32)], axis=0)
    cos, sin = cost.T, sint.T
    for ch in range(2):
        rt = _dot_nt(wq_ref[ch * chunk:(ch + 1) * chunk, :], hb)
        for j in range(4):
            t = rt[j * HEAD_DIM:(j + 1) * HEAD_DIM, :]
            qnt_ref[0, ch * 4 + j] = (t * Q_SCALE).astype(BF16)
            qrt_ref[0, ch * 4 + j] = (_rope(t, cost, sint, 0) * Q_SCALE).astype(BF16)
    rt = _dot_nt(wv_ref[...], hb)
    ones = jnp.ones((V_ONES_ROWS, TK), BF16)
    for j in range(4):
        for s in range(vt_ref.shape[2]):
            vt_ref[0, j, s] = jnp.concatenate(
                [rt[j * HEAD_DIM:(j + 1) * HEAD_DIM, s * TK:(s + 1) * TK].astype(BF16), ones], axis=0)
    gt_ref[...] = _dot_nt(wg_ref[...], hb)

    r = _dot_nt(hb, wkc_ref[...])
    groups = kvc_ref.shape[2]
    for j in range(4):
        kc_scr[j] = r[:, j * HEAD_DIM:(j + 1) * HEAD_DIM]
        for i in range(CMP_STRIDE):
            kvc_ref[0, j, :, i * HEAD_DIM:(i + 1) * HEAD_DIM] = (
                kc_scr[j, pl.ds(i, groups, stride=CMP_STRIDE), :].astype(BF16))
    r = _dot_nt(hb, wk_ref[...])
    for j in range(4):
        kk_ref[0, j] = _rope(r[:, j * HEAD_DIM:(j + 1) * HEAD_DIM], cos, sin, 1).astype(BF16)
    for ch in range(u_ref.shape[1] // chunk):
        u_ref[:, ch * chunk:(ch + 1) * chunk] = _dot_nt(hb, wu_ref[ch * chunk:(ch + 1) * chunk, :])


def _inproj(x2d, ln, ang_t, w_in_t, gate_rows, batch, seq):
    n, d = x2d.shape
    tm = TM_PROJ
    assert tm % TK == 0
    spb = seq // tm
    const = lambda i: (0, 0)
    q_w, kv_w = N_HEADS * HEAD_DIM, N_KV * HEAD_DIM
    pool_w = w_in_t.shape[1] - (q_w + 6 * kv_w + N_GATES)
    row_heads = lambda i: (i // spb, 0, i % spb, 0)
    col_heads = lambda i: (i // spb, 0, 0, i % spb)
    return pl.pallas_call(
        _inproj_kernel,
        grid=(n // tm,),
        in_specs=[
            pl.BlockSpec((tm, d), lambda i: (i, 0)),
            pl.BlockSpec((1, d), const),
            pl.BlockSpec((ang_t.shape[0], tm), lambda i: (0, i)),
            pl.BlockSpec(memory_space=pl.ANY),
        ],
        out_specs=[
            pl.BlockSpec((1, N_HEADS, HEAD_DIM, tm), col_heads),
            pl.BlockSpec((1, N_HEADS, HEAD_DIM, tm), col_heads),
            pl.BlockSpec((1, 4, tm // CMP_STRIDE, CMP_STRIDE * HEAD_DIM), row_heads),
            pl.BlockSpec((1, 4, tm, HEAD_DIM), row_heads),
            pl.BlockSpec((1, 4, tm // TK, HEAD_DIM + V_ONES_ROWS, TK), lambda i: (i // spb, 0, i % spb, 0, 0)),
            pl.BlockSpec((gate_rows, tm), lambda i: (0, i)),
            pl.BlockSpec((tm, pool_w), lambda i: (i, 0)),
        ],
        out_shape=[
            jax.ShapeDtypeStruct((batch, N_HEADS, HEAD_DIM, seq), BF16),
            jax.ShapeDtypeStruct((batch, N_HEADS, HEAD_DIM, seq), BF16),
            jax.ShapeDtypeStruct((batch, 4, seq // CMP_STRIDE, CMP_STRIDE * HEAD_DIM), BF16),
            jax.ShapeDtypeStruct((batch, 4, seq, HEAD_DIM), BF16),
            jax.ShapeDtypeStruct((batch, 4, seq // TK, HEAD_DIM + V_ONES_ROWS, TK), BF16),
            jax.ShapeDtypeStruct((gate_rows, n), F32),
            jax.ShapeDtypeStruct((n, pool_w), F32),
        ],
        scratch_shapes=[
            pltpu.VMEM((4, tm, HEAD_DIM), F32),
            pltpu.VMEM((2, W_STAGE_ROWS, d), F32),
            pltpu.VMEM((q_w, d), BF16), pltpu.VMEM((2 * kv_w, d), BF16), pltpu.VMEM((gate_rows, d), BF16),
            pltpu.VMEM((2 * kv_w, d), BF16), pltpu.VMEM((2 * kv_w, d), BF16), pltpu.VMEM((pool_w, d), BF16),
            pltpu.SemaphoreType.DMA((2,)),
        ],
        compiler_params=_cparams(1),
        name="inproj",
    )(x2d, ln, ang_t, w_in_t)


def _compress_kernel(t_ref, pe_ref, wk1_ref, wv1_ref, w2_ref, o_ref, w1_scr):
    half = w1_scr.shape[1]

    @pl.when(pl.program_id(0) == 0)
    def _():
        for kind, w_ref in enumerate((wk1_ref, wv1_ref)):
            for h in range(2):
                w1_scr[kind, :, h * CMP_HIDDEN:(h + 1) * CMP_HIDDEN] = w_ref[pl.ds(h * half, half), :].astype(BF16)

    for j in range(t_ref.shape[1]):
        kind = j // N_KV
        t2 = t_ref[0, j]
        w1 = w1_scr[kind]
        a = _dot(t2, w1)
        pb = _dot(pe_ref[kind], w1)
        bias = pb[0:1, :CMP_HIDDEN] + pb[1:2, CMP_HIDDEN:]
        nrow = a.shape[0]
        nxt = pltpu.roll(a[:, CMP_HIDDEN:], nrow - 1, 0)
        hid = jax.nn.gelu(a[:, :CMP_HIDDEN] + nxt + bias)
        out = _dot(hid.astype(BF16), w2_ref[kind])
        row = lax.broadcasted_iota(I32, out.shape, 0)
        out = jnp.where(row < nrow - 1, out, 0.0)
        o_ref[0, j] = (out.T if kind else out).astype(BF16)


def _compress(kvc2, pe2, wk1, wv1, w2):
    batch, four, nrow, width = kvc2.shape
    assert wk1.shape == wv1.shape == (2 * width, CMP_HIDDEN)
    whole = lambda a: pl.BlockSpec(a.shape, lambda b: (0,) * a.ndim)
    return pl.pallas_call(
        _compress_kernel,
        grid=(batch,),
        in_specs=[pl.BlockSpec((1, four, nrow, width), lambda b: (b, 0, 0, 0)), whole(pe2), whole(wk1), whole(wv1),
                  whole(w2)],
        out_specs=pl.BlockSpec((1, four, nrow, HEAD_DIM), lambda b: (b, 0, 0, 0)),
        out_shape=jax.ShapeDtypeStruct((batch, four, nrow, HEAD_DIM), BF16),
        scratch_shapes=[pltpu.VMEM((2, width, 2 * CMP_HIDDEN), BF16)],
        compiler_params=_cparams(1),
        name="compress",
    )(kvc2, pe2, wk1, wv1, w2)


def _nsa_kernel(qnt_ref, qrt_ref, cmp_ref, kk_ref, vt_ref, kblk_ref, gt_ref, gn_ref, o_ref,
                m_scr, acc_scr, qa_scr, ocmp_scr, s_scr, o_scr):
    qi = pl.program_id(1)
    tq = qnt_ref.shape[3]
    rows = GQA * tq
    n_cmp_pad = cmp_ref.shape[2]
    n_sel = kk_ref.shape[2] // SEL_LEN
    q0 = qi * tq

    m_scr[...] = jnp.full(m_scr.shape, NEG, F32)
    acc_scr[...] = jnp.zeros(acc_scr.shape, F32)

    def scores(item, slot):
        c, g, q_rows, k_tile, v_t, bias = item
        s_scr[slot] = _dot(k_tile, qa_scr[g, :q_rows, :])

    def softmax_pv(item, slot):
        c, g, q_rows, k_tile, v_t, bias = item
        s = s_scr[slot]
        if bias is not None:
            s = s + bias
        m_old = m_scr[c]
        m_new = jnp.maximum(m_old, jnp.max(s, axis=0, keepdims=True))
        alpha = jnp.exp2(m_old - m_new)
        p = jnp.exp2(s - m_new)
        acc_scr[c] = alpha * acc_scr[c] + _dot(v_t, p.astype(BF16))
        m_scr[c] = m_new

    def flash_out(c):
        return acc_scr[c, :HEAD_DIM, :] / acc_scr[c, HEAD_DIM:HEAD_DIM + 1, :]

    def tile_heads(a):
        return jnp.concatenate([a] * GQA, axis=1)

    kk_i = lax.broadcasted_iota(I32, (TK, tq), 0)
    qq_i = lax.broadcasted_iota(I32, (TK, tq), 1)
    causal_bias = tile_heads(jnp.where(kk_i <= qq_i, 0.0, NEG))
    band_bias = tile_heads(jnp.where(kk_i > qq_i, 0.0, NEG))
    n_win = WINDOW // TK

    for g in range(N_KV):
        qn4 = jnp.concatenate([qnt_ref[0, h] for h in range(g * GQA, (g + 1) * GQA)], axis=1)
        s = _dot(cmp_ref[0, g], qn4)
        n_idx = lax.broadcasted_iota(I32, (n_cmp_pad, rows), 0)
        t_idx = q0 + jnp.bitwise_and(lax.broadcasted_iota(I32, (n_cmp_pad, rows), 1), tq - 1)
        cmp_ok = n_idx * CMP_STRIDE + (CMP_LEN - 1) <= t_idx
        s = jnp.where(cmp_ok, s, NEG)
        m = jnp.max(s, axis=0, keepdims=True)
        e = jnp.where(cmp_ok, jnp.exp2(s - m), 0.0)
        l = jnp.sum(e, axis=0, keepdims=True)
        p = e * jnp.where(l > 0.0, 1.0 / l, 0.0)
        pb = p.astype(BF16)
        ocmp_scr[g] = _dot(cmp_ref[0, N_KV + g], pb)

        jn = lax.broadcasted_iota(I32, (n_sel, n_cmp_pad), 0) * SEL_LEN
        cn = lax.broadcasted_iota(I32, (n_sel, n_cmp_pad), 1) * CMP_STRIDE
        ov = jnp.clip(jnp.minimum(cn + CMP_LEN, jn + SEL_LEN) - jnp.maximum(cn, jn), 0, None)
        w_t = (ov.astype(F32) * (1.0 / CMP_LEN)).astype(BF16)
        imp4 = _dot(w_t, pb)
        imp = imp4[:, 0:tq]
        for r in range(1, GQA):
            imp = imp + imp4[:, r * tq:(r + 1) * tq]
        j_blk = lax.broadcasted_iota(I32, (n_sel, tq), 0)
        t_q = q0 + lax.broadcasted_iota(I32, (n_sel, tq), 1)
        sel_ok = j_blk * SEL_LEN <= t_q
        back = t_q // SEL_LEN - j_blk
        forced = (j_blk == 0) | ((back >= 0) & (back < N_LOCAL))
        val = jnp.where(sel_ok & forced, BIG, jnp.where(sel_ok, imp, -BIG))
        rank = jnp.zeros((n_sel, tq), F32)
        for i in range(n_sel):
            vi = val[i:i + 1, :]
            beats = (vi > val) | ((vi == val) & (j_blk > i))
            rank = rank + beats.astype(F32)
        keep = (rank < float(min(SEL_TOPK, n_sel))) & sel_ok
        bias_t = jnp.where(keep, 0.0, -SEL_MASK_BIAS)
        bias_t = jnp.concatenate([bias_t, jnp.zeros((HEAD_DIM - n_sel, tq), F32)], axis=0)
        qa_scr[g, :HEAD_DIM, :] = jnp.concatenate(
            [qrt_ref[0, h] for h in range(g * GQA, (g + 1) * GQA)], axis=1)
        qa_scr[g, HEAD_DIM:, :] = tile_heads(bias_t.astype(BF16))

    def sel_item(g, kj, bias):
        st = pl.multiple_of(kj * TK, TK)
        k_aug = jnp.concatenate([kk_ref[0, g, pl.ds(st, TK), :], kblk_ref[pl.ds(st, TK), :]], axis=1)
        return (g, g, 2 * HEAD_DIM, k_aug, vt_ref[0, g, kj], bias)

    def win_item(g, back):
        kj = qi - back
        st = pl.multiple_of(kj * TK, TK)
        bias = causal_bias if back == 0 else (band_bias if back == n_win else None)
        return (N_KV + g, g, HEAD_DIM, kk_ref[0, N_KV + g, pl.ds(st, TK), :], vt_ref[0, N_KV + g, kj], bias)

    assert N_KV == 2
    scores(sel_item(0, 0, None), 0)

    def sel_tile(kj):
        scores(sel_item(1, kj, None), 1)
        softmax_pv(sel_item(0, kj, None), 0)
        scores(sel_item(0, kj + 1, None), 0)
        softmax_pv(sel_item(1, kj, None), 1)

    def sel_pair(j, carry):
        sel_tile(2 * j)
        sel_tile(2 * j + 1)
        return carry

    lax.fori_loop(0, qi // 2, sel_pair, 0)

    @pl.when(qi % 2 == 1)
    def _():
        sel_tile(qi - 1)

    def run_tail(farthest):
        tail = ([sel_item(g, qi, causal_bias) for g in range(N_KV)]
                + [win_item(g, back) for back in range(farthest, -1, -1) for g in range(N_KV)])
        for i, item in enumerate(tail):
            if i + 1 < len(tail):
                scores(tail[i + 1], (i + 1) % 2)
            softmax_pv(item, i % 2)

    for nb in range(n_win):
        @pl.when(qi == nb)
        def _(nb=nb):
            run_tail(nb)

    @pl.when(qi >= n_win)
    def _():
        run_tail(n_win)

    gsig = jax.nn.sigmoid(gt_ref[...])
    for g in range(N_KV):
        o_cmp, o_sel, o_win = ocmp_scr[g], flash_out(g), flash_out(N_KV + g)
        for r in range(GQA):
            h = g * GQA + r
            sl = slice(r * tq, (r + 1) * tq)
            o_h = (gsig[3 * h:3 * h + 1, :] * o_cmp[:, sl]
                   + gsig[3 * h + 1:3 * h + 2, :] * o_sel[:, sl]
                   + gsig[3 * h + 2:3 * h + 3, :] * o_win[:, sl])
            o_scr[:, h * HEAD_DIM:(h + 1) * HEAD_DIM] = o_h.T

    o = o_scr[...]
    ms = jnp.mean(o * o, axis=-1, keepdims=True)
    o_ref[...] = (o * lax.rsqrt(ms + EPS) * gn_ref[...]).astype(BF16)


def _nsa(qnt, qrt, cmp, kk, vt, kblk, gt, gn):
    batch, _, _, seq = qnt.shape
    assert TQ == TK and WINDOW % TK == 0 and TQ & (TQ - 1) == 0
    nq = seq // TQ
    width = N_HEADS * HEAD_DIM
    return pl.pallas_call(
        _nsa_kernel,
        grid=(batch, nq),
        in_specs=[
            pl.BlockSpec((1, N_HEADS, HEAD_DIM, TQ), lambda b, i: (b, 0, 0, i)),
            pl.BlockSpec((1, N_HEADS, HEAD_DIM, TQ), lambda b, i: (b, 0, 0, i)),
            pl.BlockSpec((1,) + cmp.shape[1:], lambda b, i: (b, 0, 0, 0)),
            pl.BlockSpec((1,) + kk.shape[1:], lambda b, i: (b, 0, 0, 0)),
            pl.BlockSpec((1,) + vt.shape[1:], lambda b, i: (b, 0, 0, 0, 0)),
            pl.BlockSpec((seq, LANES), lambda b, i: (0, 0)),
            pl.BlockSpec((gt.shape[0], TQ), lambda b, i: (0, b * nq + i)),
            pl.BlockSpec((1, width), lambda b, i: (0, 0)),
        ],
        out_specs=pl.BlockSpec((TQ, width), lambda b, i: (b * nq + i, 0)),
        out_shape=jax.ShapeDtypeStruct((batch * seq, width), BF16),
        scratch_shapes=[
            pltpu.VMEM((2 * N_KV, 1, GQA * TQ), F32),
            pltpu.VMEM((2 * N_KV, vt.shape[3], GQA * TQ), F32),
            pltpu.VMEM((N_KV, 2 * HEAD_DIM, GQA * TQ), BF16),
            pltpu.VMEM((N_KV, HEAD_DIM, GQA * TQ), F32),
            pltpu.VMEM((2, TK, GQA * TQ), F32),
            pltpu.VMEM((TQ, width), F32),
        ],
        compiler_params=_cparams(2),
        name="nsa",
    )(qnt, qrt, cmp, kk, vt, kblk, gt, gn)


def _pool_group(cur, halo_rows, t1, window, w_mat, bias, scale):
    halo = halo_rows.shape[0]
    acc = jnp.concatenate([halo_rows, cur], axis=0)
    span = 1
    while span < window:
        acc = acc + jnp.concatenate([jnp.zeros((span, acc.shape[1]), F32), acc[:-span]], axis=0)
        span *= 2
    mean = acc[halo:] / jnp.minimum(t1, float(window))
    return (_dot((mean - cur).astype(BF16), w_mat) + bias) * scale


def _pool_kernel(u_ref, w_ref, b_ref, sc_ref, gn_ref, o_ref, y_scr, halo_scr):
    si = pl.program_id(1)
    tm = u_ref.shape[1]
    halo = halo_scr.shape[0]

    @pl.when(si == 0)
    def _():
        halo_scr[...] = jnp.zeros(halo_scr.shape, F32)

    t1 = (si * tm + 1 + lax.broadcasted_iota(I32, (tm, 1), 0)).astype(F32)
    cg = w_ref.shape[1]
    for gi, window in enumerate(POOL_SIZES):
        sl = slice(gi * cg, (gi + 1) * cg)
        cur = u_ref[0, :, sl]
        y_scr[:, sl] = _pool_group(cur, halo_scr[:, sl], t1, window, w_ref[gi], b_ref[:, sl], sc_ref[:, sl])
        halo_scr[:, sl] = cur[tm - halo:, :]
    y = y_scr[...]
    ms = jnp.mean(y * y, axis=-1, keepdims=True)
    o_ref[...] = (y * lax.rsqrt(ms + EPS) * gn_ref[...]).astype(BF16)


def _pool(u3, w_pool, b_pool, pool_scale, gn_pool):
    batch, seq, c = u3.shape
    tm = TM_PROJ
    ns = seq // tm
    vec = lambda b, i: (0, 0)
    return pl.pallas_call(
        _pool_kernel,
        grid=(batch, ns),
        in_specs=[
            pl.BlockSpec((1, tm, c), lambda b, i: (b, i, 0)),
            pl.BlockSpec(w_pool.shape, lambda b, i: (0, 0, 0)),
            pl.BlockSpec((1, c), vec),
            pl.BlockSpec((1, c), vec),
            pl.BlockSpec((1, c), vec),
        ],
        out_specs=pl.BlockSpec((tm, c), lambda b, i: (b * ns + i, 0)),
        out_shape=jax.ShapeDtypeStruct((batch * seq, c), BF16),
        scratch_shapes=[pltpu.VMEM((tm, c), F32), pltpu.VMEM((max(POOL_SIZES), c), F32)],
        compiler_params=_cparams(2),
        name="pool",
    )(u3, w_pool, b_pool, pool_scale, gn_pool)


def _outproj_kernel(mn_ref, mp_ref, x_ref, w_hbm, ln_ref, wr_ref, br_ref,
                    x1_ref, h2_ref, eid_ref, gate_ref, x1_scr, ssq_scr, stage, w_ref, wsem):
    tm, d = x_ref.shape
    half = mn_ref.shape[1]
    chunk = 512

    @pl.when(pl.program_id(0) == 0)
    def _():
        x1_scr[...] = jnp.zeros(x1_scr.shape, F32)
        ssq_scr[...] = jnp.zeros(ssq_scr.shape, F32)
        rc = stage.shape[1]
        n_chunks = w_ref.shape[0] // rc

        def chunk_copy(c, slot):
            return pltpu.make_async_copy(w_hbm.at[0, pl.ds(c * rc, rc)], stage.at[slot], wsem.at[slot])

        chunk_copy(0, 0).start()
        for c in range(n_chunks):
            slot = c % 2
            if c + 1 < n_chunks:
                chunk_copy(c + 1, 1 - slot).start()
            chunk_copy(c, slot).wait()
            w_ref[c * rc:(c + 1) * rc, :] = stage[slot].astype(BF16)

    h2 = x1_scr[...] * lax.rsqrt(ssq_scr[...] * (1.0 / d) + EPS) * ln_ref[...]
    _store_token_tiles(h2_ref, _pack_bf16_halves(h2))

    logits = _dot_nt(wr_ref[...], h2.astype(BF16)) + br_ref[:, 0:1]
    ng, ne = N_EXPERT_GROUPS, EXPERTS_PER_GROUP
    row = lax.broadcasted_iota(I32, (ne, tm), 0)
    lg = jnp.where(row < ng, logits[0:ne], NEG)
    mg = jnp.max(lg, axis=0, keepdims=True)
    g_sel = jnp.min(jnp.where(lg == mg, row, ne), axis=0, keepdims=True)
    p_g = 1.0 / jnp.sum(jnp.exp(lg - mg), axis=0, keepdims=True)
    le = jnp.zeros((ne, tm), F32)
    for g in range(ng):
        le = jnp.where(g_sel == g, logits[ne * (g + 1):ne * (g + 2)], le)
    ex = jnp.exp(le - jnp.max(le, axis=0, keepdims=True))
    pe = ex / jnp.sum(ex, axis=0, keepdims=True)
    p1 = jnp.max(pe, axis=0, keepdims=True)
    i1 = jnp.min(jnp.where(pe == p1, row, ne), axis=0, keepdims=True)
    rest = jnp.where(row == i1, -1.0, pe)
    p2 = jnp.max(rest, axis=0, keepdims=True)
    i2 = jnp.min(jnp.where(rest == p2, row, ne), axis=0, keepdims=True)
    psum = p1 + p2
    zero_i = jnp.zeros((ne, tm), I32)
    eid = jnp.where(row == 0, g_sel * ne + i1, jnp.where(row == 1, g_sel * ne + i2, zero_i))
    gate = jnp.where(row == 0, p_g * p1 / psum, jnp.where(row == 1, p_g * p2 / psum, 0.0))
    eid_ref[...] = eid
    gate_ref[...] = gate

    mn = mn_ref[...]
    mp = mp_ref[...]
    ssq = jnp.zeros((tm, 1), F32)
    for c in range(d // chunk):
        sl = slice(c * chunk, (c + 1) * chunk)
        x1 = x_ref[:, sl] + (_dot(mn, w_ref[:half, sl]) + _dot(mp, w_ref[half:, sl]))
        x1_ref[:, sl] = x1
        x1_scr[:, sl] = x1
        ssq = ssq + jnp.sum(x1 * x1, axis=-1, keepdims=True)
    ssq_scr[...] = ssq


def _outproj(mix_nsa, mix_pool, x2d, w_out, ln_moe, w_router_t, b_router):
    n, d = x2d.shape
    tm = TM_PROJ
    half = mix_nsa.shape[1]
    const = lambda i: (0, 0)
    last = n // tm - 1
    cur = lambda i: (jnp.minimum(i, last), 0)
    prev = lambda i: (jnp.maximum(i - 1, 0), 0)
    prev_t = lambda i: (0, jnp.maximum(i - 1, 0))
    return pl.pallas_call(
        _outproj_kernel,
        grid=(n // tm + 1,),
        in_specs=[
            pl.BlockSpec((tm, half), cur),
            pl.BlockSpec((tm, half), cur),
            pl.BlockSpec((tm, d), cur),
            pl.BlockSpec(memory_space=pl.ANY),
            pl.BlockSpec((1, d), const),
            pl.BlockSpec(w_router_t.shape, const),
            pl.BlockSpec(b_router.shape, const),
        ],
        out_specs=[
            pl.BlockSpec((tm, d), cur),
            pl.BlockSpec((tm * SUBLANES, LANES), prev),
            pl.BlockSpec((SUBLANES, tm), prev_t),
            pl.BlockSpec((SUBLANES, tm), prev_t),
        ],
        scratch_shapes=[pltpu.VMEM((tm, d), F32), pltpu.VMEM((tm, 1), F32),
                        pltpu.VMEM((2, W_STAGE_ROWS, d), F32), pltpu.VMEM(w_out.shape[1:], BF16),
                        pltpu.SemaphoreType.DMA((2,))],
        out_shape=[
            jax.ShapeDtypeStruct((n, d), F32),
            jax.ShapeDtypeStruct((n * SUBLANES, LANES), U32),
            jax.ShapeDtypeStruct((SUBLANES, n), I32),
            jax.ShapeDtypeStruct((SUBLANES, n), F32),
        ],
        compiler_params=_cparams(1),
        name="outproj",
    )(mix_nsa, mix_pool, x2d, w_out, ln_moe, w_router_t, b_router)


def _dispatch_kernel(eid_ref, dest_ref, meta_ref, rank_scr):
    n = eid_ref.shape[1]
    t = T_RANK
    row = lax.broadcasted_iota(I32, (N_EXPERTS, t), 0)
    before = (lax.broadcasted_iota(I32, (t, t), 0) < lax.broadcasted_iota(I32, (t, t), 1)).astype(BF16)
    carry = jnp.zeros((N_EXPERTS, 1), F32)
    for k in range(2):
        for j in range(n // t):
            sl = slice(j * t, (j + 1) * t)
            oh = row == eid_ref[k:k + 1, sl]
            ohf = oh.astype(F32)
            prior = _dot(ohf.astype(BF16), before) + carry
            rank_scr[k:k + 1, sl] = jnp.sum(jnp.where(oh, prior, 0.0), axis=0, keepdims=True)
            carry = carry + jnp.sum(ohf, axis=1, keepdims=True)
    counts = jnp.broadcast_to(carry, (N_EXPERTS, LANES))
    padded = jnp.floor((counts + (TM_MOE - 1.0)) * (1.0 / TM_MOE)) * TM_MOE
    ends = padded
    r_idx = lax.broadcasted_iota(I32, (N_EXPERTS, LANES), 0)
    sh = 1
    while sh < N_EXPERTS:
        ends = ends + jnp.where(r_idx >= sh, pltpu.roll(ends, sh, 0), 0.0)
        sh *= 2
    starts = (ends - padded)[:, 0:1]
    for k in range(2):
        for j in range(n // t):
            sl = slice(j * t, (j + 1) * t)
            oh = row == eid_ref[k:k + 1, sl]
            base = jnp.sum(jnp.where(oh, starts, 0.0), axis=0, keepdims=True)
            dest_ref[k:k + 1, sl] = (rank_scr[k:k + 1, sl] + base).astype(I32)
    for k in range(2, SUBLANES):
        dest_ref[k:k + 1, :] = jnp.zeros((1, n), I32)
    blk_start = lax.broadcasted_iota(I32, (N_EXPERTS, LANES), 1).astype(F32) * TM_MOE
    blk_expert = jnp.sum((ends <= blk_start).astype(F32), axis=0, keepdims=True)
    blk_expert = jnp.minimum(blk_expert, N_EXPERTS - 1.0)
    n_used = ends[N_EXPERTS - 1:N_EXPERTS, :] * (1.0 / TM_MOE)
    on_diag = r_idx == lax.broadcasted_iota(I32, (N_EXPERTS, LANES), 1)
    end_blk = jnp.sum(jnp.where(on_diag, ends, 0.0), axis=0, keepdims=True) * (1.0 / TM_MOE)
    n_blk = jnp.sum(jnp.where(on_diag, padded, 0.0), axis=0, keepdims=True) * (1.0 / TM_MOE)
    r8 = lax.broadcasted_iota(I32, (SUBLANES, LANES), 0)
    meta = jnp.where(r8 == 0, blk_expert,
                     jnp.where(r8 == 1, n_used, jnp.where(r8 == 2, end_blk, jnp.where(r8 == 3, n_blk, 0.0))))
    meta_ref[...] = meta.astype(I32)


def _dispatch(eid):
    n = eid.shape[1]
    return pl.pallas_call(
        _dispatch_kernel,
        out_shape=[jax.ShapeDtypeStruct((SUBLANES, n), I32),
                   jax.ShapeDtypeStruct((SUBLANES, LANES), I32)],
        scratch_shapes=[pltpu.VMEM((SUBLANES, n), F32)],
        compiler_params=pltpu.CompilerParams(vmem_limit_bytes=VMEM_LIMIT_BYTES),
        name="dispatch",
    )(eid)


def _row_copy(src_ref, src_row, dst_ref, dst_row, sem):
    tile = lambda ref, row: ref.at[pl.ds(pl.multiple_of(row * SUBLANES, SUBLANES), SUBLANES)]
    return pltpu.make_async_copy(tile(src_ref, src_row), tile(dst_ref, dst_row), sem)


def _scatter_kernel(dest_ref, meta_ref, h_ref, xs_ref, zero_buf, sem, zsem):
    tm = h_ref.shape[0] // SUBLANES
    blk = zero_buf.shape[0]

    @pl.when(pl.program_id(0) == 0)
    def _():
        zero_buf[...] = jnp.zeros(zero_buf.shape, zero_buf.dtype)
        n_used = meta_ref[1, 0]

        def zero_block(b):
            return pltpu.make_async_copy(zero_buf, xs_ref.at[pl.ds(pl.multiple_of(b * blk, blk), blk)], zsem)

        def sweep(act):
            def per_expert(e, c):
                @pl.when(meta_ref[3, e] > 0)
                def _():
                    act(zero_block(meta_ref[2, e] - 1))
                return c

            def per_tail(b, c):
                @pl.when(b >= n_used)
                def _():
                    act(zero_block(b))
                return c

            lax.fori_loop(0, N_EXPERTS, per_expert, 0)
            lax.fori_loop(0, xs_ref.shape[0] // blk, per_tail, 0)

        sweep(lambda cp: cp.start())
        sweep(lambda cp: cp.wait())

    grp = dest_ref.shape[2]
    n_grp = tm // grp

    def start(j, c):
        for u in range(grp):
            for k in range(2):
                _row_copy(h_ref, j * grp + u, xs_ref, dest_ref[0, k * n_grp + j, u], sem).start(priority=k)
        return c

    lax.fori_loop(0, n_grp, start, 0)
    for k in range(2):
        pltpu.make_async_copy(h_ref, xs_ref.at[pl.ds(0, tm * SUBLANES)], sem).wait()


def _grouped_dest(dest, tm, grp):
    n = dest.shape[1]
    return jnp.transpose(dest[:2].reshape(2, n // tm, tm // grp, grp), (1, 0, 2, 3)).reshape(n // tm, 2 * tm // grp, grp)


def _scatter(dest3, meta, h2, cap):
    nsteps, rows, grp = dest3.shape
    tm = rows * grp // 2
    return pl.pallas_call(
        _scatter_kernel,
        grid=(nsteps,),
        in_specs=[
            pl.BlockSpec((1, rows, grp), lambda i: (i, 0, 0), memory_space=pltpu.SMEM),
            pl.BlockSpec(memory_space=pltpu.SMEM),
            pl.BlockSpec((tm * SUBLANES, LANES), lambda i: (i, 0)),
        ],
        out_specs=pl.BlockSpec(memory_space=pl.ANY),
        out_shape=jax.ShapeDtypeStruct((cap * SUBLANES, LANES), h2.dtype),
        scratch_shapes=[pltpu.VMEM((TM_MOE * SUBLANES, LANES), h2.dtype), pltpu.SemaphoreType.DMA(()),
                        pltpu.SemaphoreType.DMA(())],
        compiler_params=_cparams(1),
        name="scatter",
    )(dest3, meta, h2)


def _experts_kernel(meta_ref, xs_ref, wg_hbm, wu_hbm, wd_hbm, ys_ref,
                    wg_buf, wu_buf, wd_buf, wg16, wu16, wd16, elist_ref, ord_ref, sem):
    c = pl.program_id(0)
    n_used = meta_ref[1, 0]
    e = meta_ref[0, c]
    first = (c == 0) | (e != meta_ref[0, jnp.maximum(c - 1, 0)])

    def fetch(expert, slot):
        return [pltpu.make_async_copy(w.at[expert], buf.at[slot], sem.at[slot, i])
                for i, (w, buf) in enumerate(((wg_hbm, wg_buf), (wu_hbm, wu_buf), (wd_hbm, wd_buf)))]

    @pl.when(c == 0)
    def _():
        def scan(b, cnt):
            eb = meta_ref[0, b]
            take = ((b == 0) | (eb != meta_ref[0, jnp.maximum(b - 1, 0)])) & (b < n_used)

            @pl.when(take)
            def _():
                elist_ref[cnt] = eb

            return cnt + take.astype(I32)

        cnt = lax.fori_loop(0, pl.num_programs(0), scan, 0)
        ord_ref[0] = 0
        ord_ref[1] = cnt
        for k in range(W_SLOTS):
            @pl.when(k < cnt)
            def _(k=k):
                for cp, prio in zip(fetch(elist_ref[k], k), WEIGHT_DMA_PRIORITIES):
                    cp.start(priority=prio)

    @pl.when(first & (c < n_used))
    def _():
        @pl.when(c > 0)
        def _():
            ord_ref[0] = ord_ref[0] + 1
        for cp in fetch(e, ord_ref[0] % W_SLOTS):
            cp.wait()

    def mlp(weights):
        lo, hi = _unpack_bf16_halves(_load_token_tiles(xs_ref))
        lo, hi = lo.astype(BF16), hi.astype(BF16)
        half = lo.shape[1]
        wg, wu, wd = weights

        def proj(w):
            return _dot(lo, w[:half, :]) + _dot(hi, w[half:, :])

        act = (jax.nn.silu(proj(wg)) * proj(wu)).astype(BF16)
        _store_token_tiles(ys_ref, _pack_bf16_halves(_dot(act, wd)))

    @pl.when(first & (c < n_used))
    def _():
        k = ord_ref[0]
        slot = k % W_SLOTS
        cast = [buf[slot].astype(BF16) for buf in (wg_buf, wu_buf, wd_buf)]
        for ref16, w in zip((wg16, wu16, wd16), cast):
            ref16[...] = w
        mlp(cast)
        ahead = k + W_SLOTS

        @pl.when(ahead < ord_ref[1])
        def _():
            for cp, prio in zip(fetch(elist_ref[jnp.minimum(ahead, N_EXPERTS - 1)], slot), WEIGHT_DMA_PRIORITIES):
                cp.start(priority=prio)

    @pl.when(jnp.logical_not(first) & (c < n_used))
    def _():
        mlp((wg16[...], wu16[...], wd16[...]))

    @pl.when(c >= n_used)
    def _():
        ys_ref[...] = jnp.zeros(ys_ref.shape, U32)


def _experts(meta, xs, w_gate, w_up, w_down):
    rows = TM_MOE * SUBLANES
    nblk = xs.shape[0] // rows
    d, ff = w_gate.shape[1:]

    def row_map(c, meta):
        return (jnp.minimum(c, meta[1, 0] - 1), 0)

    hbm = pl.BlockSpec(memory_space=pl.ANY)
    return pl.pallas_call(
        _experts_kernel,
        grid_spec=pltpu.PrefetchScalarGridSpec(
            num_scalar_prefetch=1,
            grid=(nblk,),
            in_specs=[pl.BlockSpec((rows, LANES), row_map), hbm, hbm, hbm],
            out_specs=pl.BlockSpec((rows, LANES), lambda c, meta: (c, 0)),
            scratch_shapes=[
                pltpu.VMEM((W_SLOTS, d, ff), F32), pltpu.VMEM((W_SLOTS, d, ff), F32),
                pltpu.VMEM((W_SLOTS, ff, d), F32),
                pltpu.VMEM((d, ff), BF16), pltpu.VMEM((d, ff), BF16), pltpu.VMEM((ff, d), BF16),
                pltpu.SMEM((N_EXPERTS,), I32),
                pltpu.SMEM((2,), I32),
                pltpu.SemaphoreType.DMA((W_SLOTS, 3)),
            ],
        ),
        out_shape=jax.ShapeDtypeStruct(xs.shape, U32),
        compiler_params=_cparams(1),
        name="experts",
    )(meta, xs, w_gate, w_up, w_down)


def _combine_kernel(dest_ref, dest_next_ref, x1_ref, gate_ref, ln_ref, ys_ref, o_ref, ybuf, sem):
    i = pl.program_id(0)
    tm = x1_ref.shape[0]
    slot = i % 2
    grp = dest_ref.shape[2]
    n_grp = tm // grp

    def issue_group(d_ref, s, j):
        for u in range(grp):
            for k in range(2):
                _row_copy(ys_ref, d_ref[0, k * n_grp + j, u], ybuf.at[s, k], j * grp + u, sem.at[s]).start(priority=k)

    @pl.when(i == 0)
    def _():
        def start(j, c):
            issue_group(dest_ref, 0, j)
            return c

        lax.fori_loop(0, n_grp, start, 0)

    for k in range(2):
        pltpu.make_async_copy(ys_ref.at[pl.ds(0, tm * SUBLANES)], ybuf.at[slot, k], sem.at[slot]).wait()
    ln = ln_ref[...]

    def finish(j):
        r0 = pl.multiple_of(j * grp, grp)

        def expert_rows(k):
            words = jnp.concatenate(
                [ybuf[slot, k, pl.ds(r0 * SUBLANES + c, grp, stride=SUBLANES), :] for c in range(SUBLANES)], axis=1)
            return _unpack_bf16_halves(words)

        (lo0, hi0), (lo1, hi1) = expert_rows(0), expert_rows(1)
        gate = gate_ref[pl.ds(r0, grp), :]
        g0, g1 = gate[:, 0:1], gate[:, 1:2]
        moe = jnp.concatenate([lo0 * g0 + lo1 * g1, hi0 * g0 + hi1 * g1], axis=1)
        x2 = x1_ref[pl.ds(r0, grp), :] + moe
        ms = jnp.mean(x2 * x2, axis=-1, keepdims=True)
        o_ref[pl.ds(r0, grp), :] = x2 * lax.rsqrt(ms + EPS) * ln

    def finish_and_prefetch(j, carry):
        finish(j)
        issue_group(dest_next_ref, 1 - slot, j)
        return carry

    def finish_only(j, carry):
        finish(j)
        return carry

    has_next = i + 1 < pl.num_programs(0)

    @pl.when(has_next)
    def _():
        lax.fori_loop(0, tm // grp, finish_and_prefetch, 0)

    @pl.when(jnp.logical_not(has_next))
    def _():
        lax.fori_loop(0, tm // grp, finish_only, 0)


def _combine(dest3, x1, gate_rows, ln_final, ys):
    n, d = x1.shape
    tm = TM_COMBINE
    last = n // tm - 1
    return pl.pallas_call(
        _combine_kernel,
        grid=(n // tm,),
        in_specs=[
            pl.BlockSpec((1,) + dest3.shape[1:], lambda i: (i, 0, 0), memory_space=pltpu.SMEM),
            pl.BlockSpec((1,) + dest3.shape[1:], lambda i: (jnp.minimum(i + 1, last), 0, 0), memory_space=pltpu.SMEM),
            pl.BlockSpec((tm, d), lambda i: (i, 0)),
            pl.BlockSpec((tm, LANES), lambda i: (i, 0)),
            pl.BlockSpec((1, d), lambda i: (0, 0)),
            pl.BlockSpec(memory_space=pl.ANY),
        ],
        out_specs=pl.BlockSpec((tm, d), lambda i: (i, 0)),
        out_shape=jax.ShapeDtypeStruct((n, d), F32),
        scratch_shapes=[pltpu.VMEM((2, 2, tm * SUBLANES, LANES), U32), pltpu.SemaphoreType.DMA((2,))],
        compiler_params=_cparams(1),
        name="combine",
    )(dest3, dest3, x1, gate_rows, ln_final, ys)


def kernel(x, positions, ln_mix, w_in, pe_cmp_k, w_cmp_k1, w_cmp_k2, pe_cmp_v, w_cmp_v1, w_cmp_v2, w_pool, b_pool, pool_scale, gn_nsa, gn_pool, w_out, ln_moe, w_router_group, b_router_group, w_router_expert, b_router_expert, w_gate, w_up, w_down, ln_final):
    batch, seq, d = x.shape
    n = batch * seq
    depth = w_in.shape[0]

    inv_freq = ROPE_THETA ** (-jnp.arange(0, ROT_DIM, 2, dtype=F32) / ROT_DIM)
    ang_t = inv_freq[:, None] * positions.astype(F32).reshape(1, n)
    kblk = (jnp.arange(seq, dtype=I32)[:, None] // SEL_LEN == jnp.arange(LANES, dtype=I32)[None, :]).astype(BF16)

    x2d = x.reshape(n, d)
    assert depth == 1, "single-layer operation"
    for l in range(depth):
        gate_rows_pad = -(-N_GATES // SUBLANES) * SUBLANES
        qnt, qrt, kvc, kk, vt, gt, u = _inproj(x2d, ln_mix[l][None], ang_t, jnp.swapaxes(w_in, 1, 2),
                                               gate_rows_pad, batch, seq)

        half = CMP_LEN * HEAD_DIM // 2
        w2 =jnp.stack([w_cmp_k2[l], w_cmp_v2[l]]).astype(BF16)
        def pe_rows(pe):
            return jnp.pad(pe.reshape(2, half), ((0, SUBLANES - 2), (0, 0)))
        pe2 = jnp.stack([pe_rows(pe_cmp_k[l]), pe_rows(pe_cmp_v[l])]).astype(BF16)
        cmp = _compress(kvc, pe2, w_cmp_k1[l], w_cmp_v1[l], w2)

        mix_nsa = _nsa(qnt, qrt, cmp, kk, vt, kblk, gt, gn_nsa[l][None])
        mix_pool = _pool(u.reshape(batch, seq, -1), w_pool[l].astype(BF16), b_pool[l][None],
                         pool_scale[l][None], gn_pool[l][None])

        ng, ne = N_EXPERT_GROUPS, EXPERTS_PER_GROUP
        w_r = jnp.concatenate([
            w_router_group[l].T, jnp.zeros((ne - ng, d), F32),
            jnp.transpose(w_router_expert[l], (0, 2, 1)).reshape(ng * ne, d),
            jnp.zeros((LANES - ne - ng * ne, d), F32)], axis=0).astype(BF16)
        b_r = jnp.concatenate([
            b_router_group[l], jnp.zeros((ne - ng,), F32), b_router_expert[l].reshape(-1),
            jnp.zeros((LANES - ne - ng * ne,), F32)])
        b_r = jnp.broadcast_to(b_r[:, None], (LANES, LANES))
        x1, h2, eid, gate = _outproj(mix_nsa, mix_pool, x2d, w_out, ln_moe[l][None], w_r, b_r)

        dest, meta = _dispatch(eid)
        cap = (2 * n // TM_MOE + N_EXPERTS) * TM_MOE
        xs = _scatter(_grouped_dest(dest, TS_SCATTER, DMA_ISSUE_GROUP), meta, h2, cap)
        ys = _experts(meta, xs, w_gate[l], w_up[l], w_down[l])
        dest_tok = _grouped_dest(dest, TM_COMBINE, DMA_ISSUE_GROUP)
        gate_rows = jnp.pad(gate[:2].T, ((0, 0), (0, LANES - 2)))
        x2d = _combine(dest_tok, x1, gate_rows, ln_final[None], ys)
    return x2d.reshape(batch, seq, d)
```

```python
import jax
import jax.numpy as jnp
from jax import lax
from jax.experimental import pallas as pl
from jax.experimental.pallas import tpu as pltpu

F32 = jnp.float32
BF16 = jnp.bfloat16
I32 = jnp.int32

HEAD_DIM = 128
N_HEADS = 8
N_KV = 2
GQA = N_HEADS // N_KV
N_GATES = 3 * N_HEADS
ROT_DIM = HEAD_DIM // 4
ROPE_THETA = 500000.0
CMP_LEN = 32
CMP_STRIDE = 16
CMP_HIDDEN = 2 * HEAD_DIM
SEL_LEN = 64
SEL_TOPK = 16
N_LOCAL = 2
WINDOW = 512
POOL_SIZES = (2, 4, 8, 16)
N_EXPERT_GROUPS = 4
EXPERTS_PER_GROUP = 8
N_EXPERTS = N_EXPERT_GROUPS * EXPERTS_PER_GROUP
EPS = 1e-6
NEG = -1e30
BIG = 1e30
SCALE = HEAD_DIM ** -0.5
LOG2E = 1.4426950408889634
Q_SCALE = SCALE * LOG2E

LANES = 128
SUBLANES = 8
VMEM_LIMIT_BYTES = 56 * 1024 * 1024

TM_PROJ = 512
TQ = 256
TK = 256
SEL_MASK_BIAS = 2.0 ** 40
V_ONES_ROWS = 16
TM_MOE = 256
TS_SCATTER = 2048
TM_COMBINE = 512
WEIGHT_DMA_PRIORITIES = (0, 1, 1)
W_STAGE_ROWS = 256
W_SLOTS = 3
X1_SLOTS = 3
DMA_ISSUE_GROUP = 64
T_RANK = 512


def _cparams(n_axes):
    return pltpu.CompilerParams(dimension_semantics=("arbitrary",) * n_axes, vmem_limit_bytes=VMEM_LIMIT_BYTES)


def _dot(a, b):
    return jnp.dot(a, b, preferred_element_type=F32)


def _dot_nt(a, b):
    return lax.dot_general(a, b, (((1,), (1,)), ((), ())), preferred_element_type=F32)


U32 = jnp.uint32
_HI16 = 0xFFFF0000


def _pack_bf16_halves(x):
    c = x.shape[1] // 2
    bits = lambda t: lax.bitcast_convert_type(t.astype(BF16).astype(F32), U32)
    return lax.shift_right_logical(bits(x[:, :c]), U32(16)) | (bits(x[:, c:]) & U32(_HI16))


def _store_token_tiles(ref, words):
    m, width = words.shape
    assert width == SUBLANES * LANES
    for j in range(SUBLANES):
        ref[pl.ds(j, m, stride=SUBLANES), :] = words[:, j * LANES:(j + 1) * LANES]


def _load_token_tiles(ref):
    m = ref.shape[0] // SUBLANES
    return jnp.concatenate([ref[pl.ds(j, m, stride=SUBLANES), :] for j in range(SUBLANES)], axis=1)


def _unpack_bf16_halves(w):
    lo = lax.bitcast_convert_type(lax.shift_left(w, U32(16)), F32)
    hi = lax.bitcast_convert_type(w & U32(_HI16), F32)
    return lo, hi


def _rope(t, cos, sin, axis):
    idx = lax.broadcasted_iota(I32, t.shape, axis)
    partner = jnp.where(idx < ROT_DIM // 2,
                        pltpu.roll(t, HEAD_DIM - ROT_DIM // 2, axis),
                        pltpu.roll(t, ROT_DIM // 2, axis))
    return t * cos + partner * sin


def _inproj_kernel(x_ref, ln_ref, ang_ref, wt_hbm,
                   qnt_ref, qrt_ref, kvc_ref, kk_ref, vt_ref, gt_ref, u_ref,
                   kc_scr, stage, wq_ref, wv_ref, wg_ref, wkc_ref, wk_ref, wu_ref, wsem):
    @pl.when(pl.program_id(0) == 0)
    def _():
        rc = stage.shape[1]
        q_w, kv_w = N_HEADS * HEAD_DIM, N_KV * HEAD_DIM
        o_kc, o_ks = q_w, q_w + 2 * kv_w
        o_vs, o_kw, o_vw, o_g = o_ks + kv_w, o_ks + 2 * kv_w, o_ks + 3 * kv_w, o_ks + 4 * kv_w
        o_u = o_g + N_GATES
        pieces = ([(r0, rc, wq_ref, r0) for r0 in range(0, q_w, rc)]
                  + [(o_kc + r0, rc, wkc_ref, r0) for r0 in range(0, 2 * kv_w, rc)]
                  + [(o_ks, kv_w, wk_ref, 0), (o_kw, kv_w, wk_ref, kv_w)]
                  + [(o_vs, kv_w, wv_ref, 0), (o_vw, kv_w, wv_ref, kv_w)]
                  + [(o_g, N_GATES, wg_ref, 0)]
                  + [(o_u + r0, rc, wu_ref, r0) for r0 in range(0, wu_ref.shape[0], rc)])
        assert all(n <= rc and src % SUBLANES == 0 for src, n, _, _ in pieces)
        wg_ref[...] = jnp.zeros(wg_ref.shape, BF16)

        def piece_copy(i, slot):
            src, rows, _, _ = pieces[i]
            return pltpu.make_async_copy(wt_hbm.at[0, pl.ds(src, rows)], stage.at[slot, pl.ds(0, rows)],
                                         wsem.at[slot])

        piece_copy(0, 0).start()
        for i, (src, rows, dst, dst_row) in enumerate(pieces):
            slot = i % 2
            if i + 1 < len(pieces):
                piece_copy(i + 1, 1 - slot).start()
            piece_copy(i, slot).wait()
            dst[dst_row:dst_row + rows, :] = stage[slot, :rows, :].astype(BF16)

    x = x_ref[...]
    ms = jnp.mean(x * x, axis=-1, keepdims=True)
    hb = (x * lax.rsqrt(ms + EPS) * ln_ref[...]).astype(BF16)
    chunk = 4 * HEAD_DIM

    ang = ang_ref[...]
    ca, sa = jnp.cos(ang), jnp.sin(ang)
    rest = (HEAD_DIM - ROT_DIM, ang.shape[1])
    cost = jnp.concatenate([ca, ca, jnp.ones(rest, F32)], axis=0)
    sint = jnp.concatenate([-sa, sa, jnp.zeros(rest, F32)], axis=0)
    cos, sin = cost.T, sint.T
    for ch in range(2):
        rt = _dot_nt(wq_ref[ch * chunk:(ch + 1) * chunk, :], hb)
        for j in range(4):
            t = rt[j * HEAD_DIM:(j + 1) * HEAD_DIM, :]
            qnt_ref[0, ch * 4 + j] = (t * Q_SCALE).astype(BF16)
            qrt_ref[0, ch * 4 + j] = (_rope(t, cost, sint, 0) * Q_SCALE).astype(BF16)
    rt = _dot_nt(wv_ref[...], hb)
    ones = jnp.ones((V_ONES_ROWS, TK), BF16)
    for j in range(4):
        for s in range(vt_ref.shape[2]):
            vt_ref[0, j, s] = jnp.concatenate(
                [rt[j * HEAD_DIM:(j + 1) * HEAD_DIM, s * TK:(s + 1) * TK].astype(BF16), ones], axis=0)
    gt_ref[...] = _dot_nt(wg_ref[...], hb)

    r = _dot_nt(hb, wkc_ref[...])
    groups = kvc_ref.shape[2]
    for j in range(4):
        kc_scr[j] = r[:, j * HEAD_DIM:(j + 1) * HEAD_DIM]
        for i in range(CMP_STRIDE):
            kvc_ref[0, j, :, i * HEAD_DIM:(i + 1) * HEAD_DIM] = (
                kc_scr[j, pl.ds(i, groups, stride=CMP_STRIDE), :].astype(BF16))
    r = _dot_nt(hb, wk_ref[...])
    for j in range(4):
        kk_ref[0, j] = _rope(r[:, j * HEAD_DIM:(j + 1) * HEAD_DIM], cos, sin, 1).astype(BF16)
    for ch in range(u_ref.shape[1] // chunk):
        u_ref[:, ch * chunk:(ch + 1) * chunk] = _dot_nt(hb, wu_ref[ch * chunk:(ch + 1) * chunk, :])


def _inproj(x2d, ln, ang_t, w_in_t, gate_rows, batch, seq):
    n, d = x2d.shape
    tm = TM_PROJ
    assert tm % TK == 0
    spb = seq // tm
    const = lambda i: (0, 0)
    q_w, kv_w = N_HEADS * HEAD_DIM, N_KV * HEAD_DIM
    pool_w = w_in_t.shape[1] - (q_w + 6 * kv_w + N_GATES)
    row_heads = lambda i: (i // spb, 0, i % spb, 0)
    col_heads = lambda i: (i // spb, 0, 0, i % spb)
    return pl.pallas_call(
        _inproj_kernel,
        grid=(n // tm,),
        in_specs=[
            pl.BlockSpec((tm, d), lambda i: (i, 0)),
            pl.BlockSpec((1, d), const),
            pl.BlockSpec((ang_t.shape[0], tm), lambda i: (0, i)),
            pl.BlockSpec(memory_space=pl.ANY),
        ],
        out_specs=[
            pl.BlockSpec((1, N_HEADS, HEAD_DIM, tm), col_heads),
            pl.BlockSpec((1, N_HEADS, HEAD_DIM, tm), col_heads),
            pl.BlockSpec((1, 4, tm // CMP_STRIDE, CMP_STRIDE * HEAD_DIM), row_heads),
            pl.BlockSpec((1, 4, tm, HEAD_DIM), row_heads),
            pl.BlockSpec((1, 4, tm // TK, HEAD_DIM + V_ONES_ROWS, TK), lambda i: (i // spb, 0, i % spb, 0, 0)),
            pl.BlockSpec((gate_rows, tm), lambda i: (0, i)),
            pl.BlockSpec((tm, pool_w), lambda i: (i, 0)),
        ],
        out_shape=[
            jax.ShapeDtypeStruct((batch, N_HEADS, HEAD_DIM, seq), BF16),
            jax.ShapeDtypeStruct((batch, N_HEADS, HEAD_DIM, seq), BF16),
            jax.ShapeDtypeStruct((batch, 4, seq // CMP_STRIDE, CMP_STRIDE * HEAD_DIM), BF16),
            jax.ShapeDtypeStruct((batch, 4, seq, HEAD_DIM), BF16),
            jax.ShapeDtypeStruct((batch, 4, seq // TK, HEAD_DIM + V_ONES_ROWS, TK), BF16),
            jax.ShapeDtypeStruct((gate_rows, n), F32),
            jax.ShapeDtypeStruct((n, pool_w), F32),
        ],
        scratch_shapes=[
            pltpu.VMEM((4, tm, HEAD_DIM), F32),
            pltpu.VMEM((2, W_STAGE_ROWS, d), F32),
            pltpu.VMEM((q_w, d), BF16), pltpu.VMEM((2 * kv_w, d), BF16), pltpu.VMEM((gate_rows, d), BF16),
            pltpu.VMEM((2 * kv_w, d), BF16), pltpu.VMEM((2 * kv_w, d), BF16), pltpu.VMEM((pool_w, d), BF16),
            pltpu.SemaphoreType.DMA((2,)),
        ],
        compiler_params=_cparams(1),
        name="inproj",
    )(x2d, ln, ang_t, w_in_t)


def _compress_kernel(t_ref, pe_ref, wk1_ref, wv1_ref, w2_ref, o_ref, w1_scr):
    half = w1_scr.shape[1]

    @pl.when(pl.program_id(0) == 0)
    def _():
        for kind, w_ref in enumerate((wk1_ref, wv1_ref)):
            for h in range(2):
                w1_scr[kind, :, h * CMP_HIDDEN:(h + 1) * CMP_HIDDEN] = w_ref[pl.ds(h * half, half), :].astype(BF16)

    for j in range(t_ref.shape[1]):
        kind = j // N_KV
        t2 = t_ref[0, j]
        w1 = w1_scr[kind]
        a = _dot(t2, w1)
        pb = _dot(pe_ref[kind], w1)
        bias = pb[0:1, :CMP_HIDDEN] + pb[1:2, CMP_HIDDEN:]
        nrow = a.shape[0]
        nxt = pltpu.roll(a[:, CMP_HIDDEN:], nrow - 1, 0)
        hid = jax.nn.gelu(a[:, :CMP_HIDDEN] + nxt + bias)
        out = _dot(hid.astype(BF16), w2_ref[kind])
        row = lax.broadcasted_iota(I32, out.shape, 0)
        out = jnp.where(row < nrow - 1, out, 0.0)
        o_ref[0, j] = (out.T if kind else out).astype(BF16)


def _compress(kvc2, pe2, wk1, wv1, w2):
    batch, four, nrow, width = kvc2.shape
    assert wk1.shape == wv1.shape == (2 * width, CMP_HIDDEN)
    whole = lambda a: pl.BlockSpec(a.shape, lambda b: (0,) * a.ndim)
    return pl.pallas_call(
        _compress_kernel,
        grid=(batch,),
        in_specs=[pl.BlockSpec((1, four, nrow, width), lambda b: (b, 0, 0, 0)), whole(pe2), whole(wk1), whole(wv1),
                  whole(w2)],
        out_specs=pl.BlockSpec((1, four, nrow, HEAD_DIM), lambda b: (b, 0, 0, 0)),
        out_shape=jax.ShapeDtypeStruct((batch, four, nrow, HEAD_DIM), BF16),
        scratch_shapes=[pltpu.VMEM((2, width, 2 * CMP_HIDDEN), BF16)],
        compiler_params=_cparams(1),
        name="compress",
    )(kvc2, pe2, wk1, wv1, w2)


def _nsa_kernel(qnt_ref, qrt_ref, cmp_ref, kk_ref, vt_ref, kblk_ref, gt_ref, gn_ref, o_ref,
                m_scr, acc_scr, qa_scr, ocmp_scr, s_scr, o_scr):
    qi = pl.program_id(1)
    tq = qnt_ref.shape[3]
    rows = GQA * tq
    n_cmp_pad = cmp_ref.shape[2]
    n_sel = kk_ref.shape[2] // SEL_LEN
    q0 = qi * tq

    m_scr[...] = jnp.full(m_scr.shape, NEG, F32)
    acc_scr[...] = jnp.zeros(acc_scr.shape, F32)

    def scores(item, slot):
        c, g, q_rows, k_tile, v_t, bias = item
        s_scr[slot] = _dot(k_tile, qa_scr[g, :q_rows, :])

    def softmax_pv(item, slot):
        c, g, q_rows, k_tile, v_t, bias = item
        s = s_scr[slot]
        if bias is not None:
            s = s + bias
        m_old = m_scr[c]
        m_new = jnp.maximum(m_old, jnp.max(s, axis=0, keepdims=True))
        alpha = jnp.exp2(m_old - m_new)
        p = jnp.exp2(s - m_new)
        acc_scr[c] = alpha * acc_scr[c] + _dot(v_t, p.astype(BF16))
        m_scr[c] = m_new

    def flash_out(c):
        return acc_scr[c, :HEAD_DIM, :] / acc_scr[c, HEAD_DIM:HEAD_DIM + 1, :]

    def tile_heads(a):
        return jnp.concatenate([a] * GQA, axis=1)

    kk_i = lax.broadcasted_iota(I32, (TK, tq), 0)
    qq_i = lax.broadcasted_iota(I32, (TK, tq), 1)
    causal_bias = tile_heads(jnp.where(kk_i <= qq_i, 0.0, NEG))
    band_bias = tile_heads(jnp.where(kk_i > qq_i, 0.0, NEG))
    n_win = WINDOW // TK

    for g in range(N_KV):
        qn4 = jnp.concatenate([qnt_ref[0, h] for h in range(g * GQA, (g + 1) * GQA)], axis=1)
        s = _dot(cmp_ref[0, g], qn4)
        n_idx = lax.broadcasted_iota(I32, (n_cmp_pad, rows), 0)
        t_idx = q0 + jnp.bitwise_and(lax.broadcasted_iota(I32, (n_cmp_pad, rows), 1), tq - 1)
        cmp_ok = n_idx * CMP_STRIDE + (CMP_LEN - 1) <= t_idx
        s = jnp.where(cmp_ok, s, NEG)
        m = jnp.max(s, axis=0, keepdims=True)
        e = jnp.where(cmp_ok, jnp.exp2(s - m), 0.0)
        l = jnp.sum(e, axis=0, keepdims=True)
        p = e * jnp.where(l > 0.0, 1.0 / l, 0.0)
        pb = p.astype(BF16)
        ocmp_scr[g] = _dot(cmp_ref[0, N_KV + g], pb)

        jn = lax.broadcasted_iota(I32, (n_sel, n_cmp_pad), 0) * SEL_LEN
        cn = lax.broadcasted_iota(I32, (n_sel, n_cmp_pad), 1) * CMP_STRIDE
        ov = jnp.clip(jnp.minimum(cn + CMP_LEN, jn + SEL_LEN) - jnp.maximum(cn, jn), 0, None)
        w_t = (ov.astype(F32) * (1.0 / CMP_LEN)).astype(BF16)
        imp4 = _dot(w_t, pb)
        imp = imp4[:, 0:tq]
        for r in range(1, GQA):
            imp = imp + imp4[:, r * tq:(r + 1) * tq]
        j_blk = lax.broadcasted_iota(I32, (n_sel, tq), 0)
        t_q = q0 + lax.broadcasted_iota(I32, (n_sel, tq), 1)
        sel_ok = j_blk * SEL_LEN <= t_q
        back = t_q // SEL_LEN - j_blk
        forced = (j_blk == 0) | ((back >= 0) & (back < N_LOCAL))
        val = jnp.where(sel_ok & forced, BIG, jnp.where(sel_ok, imp, -BIG))
        rank = jnp.zeros((n_sel, tq), F32)
        for i in range(n_sel):
            vi = val[i:i + 1, :]
            beats = (vi > val) | ((vi == val) & (j_blk > i))
            rank = rank + beats.astype(F32)
        keep = (rank < float(min(SEL_TOPK, n_sel))) & sel_ok
        bias_t = jnp.where(keep, 0.0, -SEL_MASK_BIAS)
        bias_t = jnp.concatenate([bias_t, jnp.zeros((HEAD_DIM - n_sel, tq), F32)], axis=0)
        qa_scr[g, :HEAD_DIM, :] = jnp.concatenate(
            [qrt_ref[0, h] for h in range(g * GQA, (g + 1) * GQA)], axis=1)
        qa_scr[g, HEAD_DIM:, :] = tile_heads(bias_t.astype(BF16))

    def sel_item(g, kj, bias):
        st = pl.multiple_of(kj * TK, TK)
        k_aug = jnp.concatenate([kk_ref[0, g, pl.ds(st, TK), :], kblk_ref[pl.ds(st, TK), :]], axis=1)
        return (g, g, 2 * HEAD_DIM, k_aug, vt_ref[0, g, kj], bias)

    def win_item(g, back):
        kj = qi - back
        st = pl.multiple_of(kj * TK, TK)
        bias = causal_bias if back == 0 else (band_bias if back == n_win else None)
        return (N_KV + g, g, HEAD_DIM, kk_ref[0, N_KV + g, pl.ds(st, TK), :], vt_ref[0, N_KV + g, kj], bias)

    assert N_KV == 2
    scores(sel_item(0, 0, None), 0)

    def sel_tile(kj):
        scores(sel_item(1, kj, None), 1)
        softmax_pv(sel_item(0, kj, None), 0)
        scores(sel_item(0, kj + 1, None), 0)
        softmax_pv(sel_item(1, kj, None), 1)

    def sel_pair(j, carry):
        sel_tile(2 * j)
        sel_tile(2 * j + 1)
        return carry

    lax.fori_loop(0, qi // 2, sel_pair, 0)

    @pl.when(qi % 2 == 1)
    def _():
        sel_tile(qi - 1)

    def run_tail(farthest):
        tail = ([sel_item(g, qi, causal_bias) for g in range(N_KV)]
                + [win_item(g, back) for back in range(farthest, -1, -1) for g in range(N_KV)])
        for i, item in enumerate(tail):
            if i + 1 < len(tail):
                scores(tail[i + 1], (i + 1) % 2)
            softmax_pv(item, i % 2)

    for nb in range(n_win):
        @pl.when(qi == nb)
        def _(nb=nb):
            run_tail(nb)

    @pl.when(qi >= n_win)
    def _():
        run_tail(n_win)

    gsig = jax.nn.sigmoid(gt_ref[...])
    for g in range(N_KV):
        o_cmp, o_sel, o_win = ocmp_scr[g], flash_out(g), flash_out(N_KV + g)
        for r in range(GQA):
            h = g * GQA + r
            sl = slice(r * tq, (r + 1) * tq)
            o_h = (gsig[3 * h:3 * h + 1, :] * o_cmp[:, sl]
                   + gsig[3 * h + 1:3 * h + 2, :] * o_sel[:, sl]
                   + gsig[3 * h + 2:3 * h + 3, :] * o_win[:, sl])
            o_scr[:, h * HEAD_DIM:(h + 1) * HEAD_DIM] = o_h.T

    o = o_scr[...]
    ms = jnp.mean(o * o, axis=-1, keepdims=True)
    o_ref[...] = (o * lax.rsqrt(ms + EPS) * gn_ref[...]).astype(BF16)


def _nsa(qnt, qrt, cmp, kk, vt, kblk, gt, gn):
    batch, _, _, seq = qnt.shape
    assert TQ == TK and WINDOW % TK == 0 and TQ & (TQ - 1) == 0
    nq = seq // TQ
    width = N_HEADS * HEAD_DIM
    return pl.pallas_call(
        _nsa_kernel,
        grid=(batch, nq),
        in_specs=[
            pl.BlockSpec((1, N_HEADS, HEAD_DIM, TQ), lambda b, i: (b, 0, 0, i)),
            pl.BlockSpec((1, N_HEADS, HEAD_DIM, TQ), lambda b, i: (b, 0, 0, i)),
            pl.BlockSpec((1,) + cmp.shape[1:], lambda b, i: (b, 0, 0, 0)),
            pl.BlockSpec((1,) + kk.shape[1:], lambda b, i: (b, 0, 0, 0)),
            pl.BlockSpec((1,) + vt.shape[1:], lambda b, i: (b, 0, 0, 0, 0)),
            pl.BlockSpec((seq, LANES), lambda b, i: (0, 0)),
            pl.BlockSpec((gt.shape[0], TQ), lambda b, i: (0, b * nq + i)),
            pl.BlockSpec((1, width), lambda b, i: (0, 0)),
        ],
        out_specs=pl.BlockSpec((TQ, width), lambda b, i: (b * nq + i, 0)),
        out_shape=jax.ShapeDtypeStruct((batch * seq, width), BF16),
        scratch_shapes=[
            pltpu.VMEM((2 * N_KV, 1, GQA * TQ), F32),
            pltpu.VMEM((2 * N_KV, vt.shape[3], GQA * TQ), F32),
            pltpu.VMEM((N_KV, 2 * HEAD_DIM, GQA * TQ), BF16),
            pltpu.VMEM((N_KV, HEAD_DIM, GQA * TQ), F32),
            pltpu.VMEM((2, TK, GQA * TQ), F32),
            pltpu.VMEM((TQ, width), F32),
        ],
        compiler_params=_cparams(2),
        name="nsa",
    )(qnt, qrt, cmp, kk, vt, kblk, gt, gn)


def _pool_group(cur, halo_rows, t1, window, w_mat, bias, scale):
    halo = halo_rows.shape[0]
    acc = jnp.concatenate([halo_rows, cur], axis=0)
    span = 1
    while span < window:
        acc = acc + jnp.concatenate([jnp.zeros((span, acc.shape[1]), F32), acc[:-span]], axis=0)
        span *= 2
    mean = acc[halo:] / jnp.minimum(t1, float(window))
    return (_dot((mean - cur).astype(BF16), w_mat) + bias) * scale


def _pool_kernel(u_ref, w_ref, b_ref, sc_ref, gn_ref, o_ref, y_scr, halo_scr):
    si = pl.program_id(1)
    tm = u_ref.shape[1]
    halo = halo_scr.shape[0]

    @pl.when(si == 0)
    def _():
        halo_scr[...] = jnp.zeros(halo_scr.shape, F32)

    t1 = (si * tm + 1 + lax.broadcasted_iota(I32, (tm, 1), 0)).astype(F32)
    cg = w_ref.shape[1]
    for gi, window in enumerate(POOL_SIZES):
        sl = slice(gi * cg, (gi + 1) * cg)
        cur = u_ref[0, :, sl]
        y_scr[:, sl] = _pool_group(cur, halo_scr[:, sl], t1, window, w_ref[gi], b_ref[:, sl], sc_ref[:, sl])
        halo_scr[:, sl] = cur[tm - halo:, :]
    y = y_scr[...]
    ms = jnp.mean(y * y, axis=-1, keepdims=True)
    o_ref[...] = (y * lax.rsqrt(ms + EPS) * gn_ref[...]).astype(BF16)


def _pool(u3, w_pool, b_pool, pool_scale, gn_pool):
    batch, seq, c = u3.shape
    tm = TM_PROJ
    ns = seq // tm
    vec = lambda b, i: (0, 0)
    return pl.pallas_call(
        _pool_kernel,
        grid=(batch, ns),
        in_specs=[
            pl.BlockSpec((1, tm, c), lambda b, i: (b, i, 0)),
            pl.BlockSpec(w_pool.shape, lambda b, i: (0, 0, 0)),
            pl.BlockSpec((1, c), vec),
            pl.BlockSpec((1, c), vec),
            pl.BlockSpec((1, c), vec),
        ],
        out_specs=pl.BlockSpec((tm, c), lambda b, i: (b * ns + i, 0)),
        out_shape=jax.ShapeDtypeStruct((batch * seq, c), BF16),
        scratch_shapes=[pltpu.VMEM((tm, c), F32), pltpu.VMEM((max(POOL_SIZES), c), F32)],
        compiler_params=_cparams(2),
        name="pool",
    )(u3, w_pool, b_pool, pool_scale, gn_pool)


def _outproj_kernel(mn_ref, mp_ref, x_ref, w_hbm, ln_ref, wr_ref, br_ref,
                    x1_ref, h2_ref, eid_ref, gate_ref, x1_scr, ssq_scr, stage, w_ref, wsem):
    tm, d = x_ref.shape
    half = mn_ref.shape[1]
    chunk = 512

    @pl.when(pl.program_id(0) == 0)
    def _():
        x1_scr[...] = jnp.zeros(x1_scr.shape, F32)
        ssq_scr[...] = jnp.zeros(ssq_scr.shape, F32)
        rc = stage.shape[1]
        n_chunks = w_ref.shape[0] // rc

        def chunk_copy(c, slot):
            return pltpu.make_async_copy(w_hbm.at[0, pl.ds(c * rc, rc)], stage.at[slot], wsem.at[slot])

        chunk_copy(0, 0).start()
        for c in range(n_chunks):
            slot = c % 2
            if c + 1 < n_chunks:
                chunk_copy(c + 1, 1 - slot).start()
            chunk_copy(c, slot).wait()
            w_ref[c * rc:(c + 1) * rc, :] = stage[slot].astype(BF16)

    h2 = x1_scr[...] * lax.rsqrt(ssq_scr[...] * (1.0 / d) + EPS) * ln_ref[...]
    _store_token_tiles(h2_ref, _pack_bf16_halves(h2))

    logits = _dot_nt(wr_ref[...], h2.astype(BF16)) + br_ref[:, 0:1]
    ng, ne = N_EXPERT_GROUPS, EXPERTS_PER_GROUP
    row = lax.broadcasted_iota(I32, (ne, tm), 0)
    lg = jnp.where(row < ng, logits[0:ne], NEG)
    mg = jnp.max(lg, axis=0, keepdims=True)
    g_sel = jnp.min(jnp.where(lg == mg, row, ne), axis=0, keepdims=True)
    p_g = 1.0 / jnp.sum(jnp.exp(lg - mg), axis=0, keepdims=True)
    le = jnp.zeros((ne, tm), F32)
    for g in range(ng):
        le = jnp.where(g_sel == g, logits[ne * (g + 1):ne * (g + 2)], le)
    ex = jnp.exp(le - jnp.max(le, axis=0, keepdims=True))
    pe = ex / jnp.sum(ex, axis=0, keepdims=True)
    p1 = jnp.max(pe, axis=0, keepdims=True)
    i1 = jnp.min(jnp.where(pe == p1, row, ne), axis=0, keepdims=True)
    rest = jnp.where(row == i1, -1.0, pe)
    p2 = jnp.max(rest, axis=0, keepdims=True)
    i2 = jnp.min(jnp.where(rest == p2, row, ne), axis=0, keepdims=True)
    psum = p1 + p2
    zero_i = jnp.zeros((ne, tm), I32)
    eid = jnp.where(row == 0, g_sel * ne + i1, jnp.where(row == 1, g_sel * ne + i2, zero_i))
    gate = jnp.where(row == 0, p_g * p1 / psum, jnp.where(row == 1, p_g * p2 / psum, 0.0))
    eid_ref[...] = eid
    gate_ref[...] = gate

    mn = mn_ref[...]
    mp = mp_ref[...]
    ssq = jnp.zeros((tm, 1), F32)
    for c in range(d // chunk):
        sl = slice(c * chunk, (c + 1) * chunk)
        x1 = x_ref[:, sl] + (_dot(mn, w_ref[:half, sl]) + _dot(mp, w_ref[half:, sl]))
        x1_ref[:, sl] = x1
        x1_scr[:, sl] = x1
        ssq = ssq + jnp.sum(x1 * x1, axis=-1, keepdims=True)
    ssq_scr[...] = ssq


def _outproj(mix_nsa, mix_pool, x2d, w_out, ln_moe, w_router_t, b_router):
    n, d = x2d.shape
    tm = TM_PROJ
    half = mix_nsa.shape[1]
    const = lambda i: (0, 0)
    last = n // tm - 1
    cur = lambda i: (jnp.minimum(i, last), 0)
    prev = lambda i: (jnp.maximum(i - 1, 0), 0)
    prev_t = lambda i: (0, jnp.maximum(i - 1, 0))
    return pl.pallas_call(
        _outproj_kernel,
        grid=(n // tm + 1,),
        in_specs=[
            pl.BlockSpec((tm, half), cur),
            pl.BlockSpec((tm, half), cur),
            pl.BlockSpec((tm, d), cur),
            pl.BlockSpec(memory_space=pl.ANY),
            pl.BlockSpec((1, d), const),
            pl.BlockSpec(w_router_t.shape, const),
            pl.BlockSpec(b_router.shape, const),
        ],
        out_specs=[
            pl.BlockSpec((tm, d), cur),
            pl.BlockSpec((tm * SUBLANES, LANES), prev),
            pl.BlockSpec((SUBLANES, tm), prev_t),
            pl.BlockSpec((SUBLANES, tm), prev_t),
        ],
        scratch_shapes=[pltpu.VMEM((tm, d), F32), pltpu.VMEM((tm, 1), F32),
                        pltpu.VMEM((2, W_STAGE_ROWS, d), F32), pltpu.VMEM(w_out.shape[1:], BF16),
                        pltpu.SemaphoreType.DMA((2,))],
        out_shape=[
            jax.ShapeDtypeStruct((n, d), F32),
            jax.ShapeDtypeStruct((n * SUBLANES, LANES), U32),
            jax.ShapeDtypeStruct((SUBLANES, n), I32),
            jax.ShapeDtypeStruct((SUBLANES, n), F32),
        ],
        compiler_params=_cparams(1),
        name="outproj",
    )(mix_nsa, mix_pool, x2d, w_out, ln_moe, w_router_t, b_router)


def _dispatch_kernel(eid_ref, dest_ref, meta_ref, rank_scr):
    n = eid_ref.shape[1]
    t = T_RANK
    row = lax.broadcasted_iota(I32, (N_EXPERTS, t), 0)
    before = (lax.broadcasted_iota(I32, (t, t), 0) < lax.broadcasted_iota(I32, (t, t), 1)).astype(BF16)
    carry = jnp.zeros((N_EXPERTS, 1), F32)
    for k in range(2):
        for j in range(n // t):
            sl = slice(j * t, (j + 1) * t)
            oh = row == eid_ref[k:k + 1, sl]
            ohf = oh.astype(F32)
            prior = _dot(ohf.astype(BF16), before) + carry
            rank_scr[k:k + 1, sl] = jnp.sum(jnp.where(oh, prior, 0.0), axis=0, keepdims=True)
            carry = carry + jnp.sum(ohf, axis=1, keepdims=True)
    counts = jnp.broadcast_to(carry, (N_EXPERTS, LANES))
    padded = jnp.floor((counts + (TM_MOE - 1.0)) * (1.0 / TM_MOE)) * TM_MOE
    ends = padded
    r_idx = lax.broadcasted_iota(I32, (N_EXPERTS, LANES), 0)
    sh = 1
    while sh < N_EXPERTS:
        ends = ends + jnp.where(r_idx >= sh, pltpu.roll(ends, sh, 0), 0.0)
        sh *= 2
    starts = (ends - padded)[:, 0:1]
    for k in range(2):
        for j in range(n // t):
            sl = slice(j * t, (j + 1) * t)
            oh = row == eid_ref[k:k + 1, sl]
            base = jnp.sum(jnp.where(oh, starts, 0.0), axis=0, keepdims=True)
            dest_ref[k:k + 1, sl] = (rank_scr[k:k + 1, sl] + base).astype(I32)
    for k in range(2, SUBLANES):
        dest_ref[k:k + 1, :] = jnp.zeros((1, n), I32)
    blk_start = lax.broadcasted_iota(I32, (N_EXPERTS, LANES), 1).astype(F32) * TM_MOE
    blk_expert = jnp.sum((ends <= blk_start).astype(F32), axis=0, keepdims=True)
    blk_expert = jnp.minimum(blk_expert, N_EXPERTS - 1.0)
    n_used = ends[N_EXPERTS - 1:N_EXPERTS, :] * (1.0 / TM_MOE)
    on_diag = r_idx == lax.broadcasted_iota(I32, (N_EXPERTS, LANES), 1)
    end_blk = jnp.sum(jnp.where(on_diag, ends, 0.0), axis=0, keepdims=True) * (1.0 / TM_MOE)
    n_blk = jnp.sum(jnp.where(on_diag, padded, 0.0), axis=0, keepdims=True) * (1.0 / TM_MOE)
    r8 = lax.broadcasted_iota(I32, (SUBLANES, LANES), 0)
    meta = jnp.where(r8 == 0, blk_expert,
                     jnp.where(r8 == 1, n_used, jnp.where(r8 == 2, end_blk, jnp.where(r8 == 3, n_blk, 0.0))))
    meta_ref[...] = meta.astype(I32)


def _dispatch(eid):
    n = eid.shape[1]
    return pl.pallas_call(
        _dispatch_kernel,
        out_shape=[jax.ShapeDtypeStruct((SUBLANES, n), I32),
                   jax.ShapeDtypeStruct((SUBLANES, LANES), I32)],
        scratch_shapes=[pltpu.VMEM((SUBLANES, n), F32)],
        compiler_params=pltpu.CompilerParams(vmem_limit_bytes=VMEM_LIMIT_BYTES),
        name="dispatch",
    )(eid)


def _row_copy(src_ref, src_row, dst_ref, dst_row, sem):
    tile = lambda ref, row: ref.at[pl.ds(pl.multiple_of(row * SUBLANES, SUBLANES), SUBLANES)]
    return pltpu.make_async_copy(tile(src_ref, src_row), tile(dst_ref, dst_row), sem)


def _scatter_kernel(dest_ref, meta_ref, h_ref, xs_ref, zero_buf, sem, zsem):
    tm = h_ref.shape[0] // SUBLANES
    blk = zero_buf.shape[0]

    @pl.when(pl.program_id(0) == 0)
    def _():
        zero_buf[...] = jnp.zeros(zero_buf.shape, zero_buf.dtype)
        n_used = meta_ref[1, 0]

        def zero_block(b):
            return pltpu.make_async_copy(zero_buf, xs_ref.at[pl.ds(pl.multiple_of(b * blk, blk), blk)], zsem)

        def sweep(act):
            def per_expert(e, c):
                @pl.when(meta_ref[3, e] > 0)
                def _():
                    act(zero_block(meta_ref[2, e] - 1))
                return c

            def per_tail(b, c):
                @pl.when(b >= n_used)
                def _():
                    act(zero_block(b))
                return c

            lax.fori_loop(0, N_EXPERTS, per_expert, 0)
            lax.fori_loop(0, xs_ref.shape[0] // blk, per_tail, 0)

        sweep(lambda cp: cp.start())
        sweep(lambda cp: cp.wait())

    grp = dest_ref.shape[2]
    n_grp = tm // grp

    def start(j, c):
        for u in range(grp):
            for k in range(2):
                _row_copy(h_ref, j * grp + u, xs_ref, dest_ref[0, k * n_grp + j, u], sem).start(priority=k)
        return c

    lax.fori_loop(0, n_grp, start, 0)
    for k in range(2):
        pltpu.make_async_copy(h_ref, xs_ref.at[pl.ds(0, tm * SUBLANES)], sem).wait()


def _grouped_dest(dest, tm, grp):
    n = dest.shape[1]
    return jnp.transpose(dest[:2].reshape(2, n // tm, tm // grp, grp), (1, 0, 2, 3)).reshape(n // tm, 2 * tm // grp, grp)


def _scatter(dest3, meta, h2, cap):
    nsteps, rows, grp = dest3.shape
    tm = rows * grp // 2
    return pl.pallas_call(
        _scatter_kernel,
        grid=(nsteps,),
        in_specs=[
            pl.BlockSpec((1, rows, grp), lambda i: (i, 0, 0), memory_space=pltpu.SMEM),
            pl.BlockSpec(memory_space=pltpu.SMEM),
            pl.BlockSpec((tm * SUBLANES, LANES), lambda i: (i, 0)),
        ],
        out_specs=pl.BlockSpec(memory_space=pl.ANY),
        out_shape=jax.ShapeDtypeStruct((cap * SUBLANES, LANES), h2.dtype),
        scratch_shapes=[pltpu.VMEM((TM_MOE * SUBLANES, LANES), h2.dtype), pltpu.SemaphoreType.DMA(()),
                        pltpu.SemaphoreType.DMA(())],
        compiler_params=_cparams(1),
        name="scatter",
    )(dest3, meta, h2)


def _experts_kernel(meta_ref, xs_ref, wg_hbm, wu_hbm, wd_hbm, ys_ref,
                    wg_buf, wu_buf, wd_buf, wg16, wu16, wd16, elist_ref, ord_ref, sem):
    c = pl.program_id(0)
    n_used = meta_ref[1, 0]
    e = meta_ref[0, c]
    first = (c == 0) | (e != meta_ref[0, jnp.maximum(c - 1, 0)])

    def fetch(expert, slot):
        return [pltpu.make_async_copy(w.at[expert], buf.at[slot], sem.at[slot, i])
                for i, (w, buf) in enumerate(((wg_hbm, wg_buf), (wu_hbm, wu_buf), (wd_hbm, wd_buf)))]

    @pl.when(c == 0)
    def _():
        def scan(b, cnt):
            eb = meta_ref[0, b]
            take = ((b == 0) | (eb != meta_ref[0, jnp.maximum(b - 1, 0)])) & (b < n_used)

            @pl.when(take)
            def _():
                elist_ref[cnt] = eb

            return cnt + take.astype(I32)

        cnt = lax.fori_loop(0, pl.num_programs(0), scan, 0)
        ord_ref[0] = 0
        ord_ref[1] = cnt
        for k in range(W_SLOTS):
            @pl.when(k < cnt)
            def _(k=k):
                for cp, prio in zip(fetch(elist_ref[k], k), WEIGHT_DMA_PRIORITIES):
                    cp.start(priority=prio)

    @pl.when(first & (c < n_used))
    def _():
        @pl.when(c > 0)
        def _():
            ord_ref[0] = ord_ref[0] + 1
        for cp in fetch(e, ord_ref[0] % W_SLOTS):
            cp.wait()

    def mlp(weights):
        lo, hi = _unpack_bf16_halves(_load_token_tiles(xs_ref))
        lo, hi = lo.astype(BF16), hi.astype(BF16)
        half = lo.shape[1]
        wg, wu, wd = weights

        def proj(w):
            return _dot(lo, w[:half, :]) + _dot(hi, w[half:, :])

        act = (jax.nn.silu(proj(wg)) * proj(wu)).astype(BF16)
        _store_token_tiles(ys_ref, _pack_bf16_halves(_dot(act, wd)))

    @pl.when(first & (c < n_used))
    def _():
        k = ord_ref[0]
        slot = k % W_SLOTS
        cast = [buf[slot].astype(BF16) for buf in (wg_buf, wu_buf, wd_buf)]
        for ref16, w in zip((wg16, wu16, wd16), cast):
            ref16[...] = w
        mlp(cast)
        ahead = k + W_SLOTS

        @pl.when(ahead < ord_ref[1])
        def _():
            for cp, prio in zip(fetch(elist_ref[jnp.minimum(ahead, N_EXPERTS - 1)], slot), WEIGHT_DMA_PRIORITIES):
                cp.start(priority=prio)

    @pl.when(jnp.logical_not(first) & (c < n_used))
    def _():
        mlp((wg16[...], wu16[...], wd16[...]))

    @pl.when(c >= n_used)
    def _():
        ys_ref[...] = jnp.zeros(ys_ref.shape, U32)


def _experts(meta, xs, w_gate, w_up, w_down):
    rows = TM_MOE * SUBLANES
    nblk = xs.shape[0] // rows
    d, ff = w_gate.shape[1:]

    def row_map(c, meta):
        return (jnp.minimum(c, meta[1, 0] - 1), 0)

    hbm = pl.BlockSpec(memory_space=pl.ANY)
    return pl.pallas_call(
        _experts_kernel,
        grid_spec=pltpu.PrefetchScalarGridSpec(
            num_scalar_prefetch=1,
            grid=(nblk,),
            in_specs=[pl.BlockSpec((rows, LANES), row_map), hbm, hbm, hbm],
            out_specs=pl.BlockSpec((rows, LANES), lambda c, meta: (c, 0)),
            scratch_shapes=[
                pltpu.VMEM((W_SLOTS, d, ff), F32), pltpu.VMEM((W_SLOTS, d, ff), F32),
                pltpu.VMEM((W_SLOTS, ff, d), F32),
                pltpu.VMEM((d, ff), BF16), pltpu.VMEM((d, ff), BF16), pltpu.VMEM((ff, d), BF16),
                pltpu.SMEM((N_EXPERTS,), I32),
                pltpu.SMEM((2,), I32),
                pltpu.SemaphoreType.DMA((W_SLOTS, 3)),
            ],
        ),
        out_shape=jax.ShapeDtypeStruct(xs.shape, U32),
        compiler_params=_cparams(1),
        name="experts",
    )(meta, xs, w_gate, w_up, w_down)


def _combine_kernel(dest_ref, dest_next_ref, gate_ref, ln_ref, x1_hbm, ys_ref, o_ref, ybuf, x1_buf, sem, xsem):
    i = pl.program_id(0)
    n_steps = pl.num_programs(0)
    tm = o_ref.shape[0]
    slot = i % 2
    grp = dest_ref.shape[2]
    n_grp = tm // grp

    def x1_copy(t):
        s = t % X1_SLOTS
        return pltpu.make_async_copy(x1_hbm.at[pl.ds(pl.multiple_of(t * tm, tm), tm)], x1_buf.at[s], xsem.at[s])

    @pl.when(i == 0)
    def _():
        for t in range(X1_SLOTS - 1):
            @pl.when(t < n_steps)
            def _(t=t):
                x1_copy(t).start()

    ahead = i + X1_SLOTS - 1

    @pl.when(ahead < n_steps)
    def _():
        x1_copy(jnp.minimum(ahead, n_steps - 1)).start()

    x1_ref = x1_buf.at[i % X1_SLOTS]

    def issue_group(d_ref, s, j):
        for u in range(grp):
            for k in range(2):
                _row_copy(ys_ref, d_ref[0, k * n_grp + j, u], ybuf.at[s, k], j * grp + u, sem.at[s]).start(priority=k)

    @pl.when(i == 0)
    def _():
        def start(j, c):
            issue_group(dest_ref, 0, j)
            return c

        lax.fori_loop(0, n_grp, start, 0)

    for k in range(2):
        pltpu.make_async_copy(ys_ref.at[pl.ds(0, tm * SUBLANES)], ybuf.at[slot, k], sem.at[slot]).wait()
    x1_copy(i).wait()
    ln = ln_ref[...]

    def finish(j):
        r0 = pl.multiple_of(j * grp, grp)

        def expert_rows(k):
            words = jnp.concatenate(
                [ybuf[slot, k, pl.ds(r0 * SUBLANES + c, grp, stride=SUBLANES), :] for c in range(SUBLANES)], axis=1)
            return _unpack_bf16_halves(words)

        (lo0, hi0), (lo1, hi1) = expert_rows(0), expert_rows(1)
        gate = gate_ref[pl.ds(r0, grp), :]
        g0, g1 = gate[:, 0:1], gate[:, 1:2]
        moe = jnp.concatenate([lo0 * g0 + lo1 * g1, hi0 * g0 + hi1 * g1], axis=1)
        x2 = x1_ref[pl.ds(r0, grp), :] + moe
        ms = jnp.mean(x2 * x2, axis=-1, keepdims=True)
        o_ref[pl.ds(r0, grp), :] = x2 * lax.rsqrt(ms + EPS) * ln

    def finish_and_prefetch(j, carry):
        finish(j)
        issue_group(dest_next_ref, 1 - slot, j)
        return carry

    def finish_only(j, carry):
        finish(j)
        return carry

    has_next = i + 1 < pl.num_programs(0)

    @pl.when(has_next)
    def _():
        lax.fori_loop(0, tm // grp, finish_and_prefetch, 0)

    @pl.when(jnp.logical_not(has_next))
    def _():
        lax.fori_loop(0, tm // grp, finish_only, 0)


def _combine(dest3, x1, gate_rows, ln_final, ys):
    n, d = x1.shape
    tm = TM_COMBINE
    last = n // tm - 1
    return pl.pallas_call(
        _combine_kernel,
        grid=(n // tm,),
        in_specs=[
            pl.BlockSpec((1,) + dest3.shape[1:], lambda i: (i, 0, 0), memory_space=pltpu.SMEM),
            pl.BlockSpec((1,) + dest3.shape[1:], lambda i: (jnp.minimum(i + 1, last), 0, 0), memory_space=pltpu.SMEM),
            pl.BlockSpec((tm, LANES), lambda i: (i, 0)),
            pl.BlockSpec((1, d), lambda i: (0, 0)),
            pl.BlockSpec(memory_space=pl.ANY),
            pl.BlockSpec(memory_space=pl.ANY),
        ],
        out_specs=pl.BlockSpec((tm, d), lambda i: (i, 0)),
        out_shape=jax.ShapeDtypeStruct((n, d), F32),
        scratch_shapes=[pltpu.VMEM((2, 2, tm * SUBLANES, LANES), U32), pltpu.VMEM((X1_SLOTS, tm, d), F32),
                        pltpu.SemaphoreType.DMA((2,)), pltpu.SemaphoreType.DMA((X1_SLOTS,))],
        compiler_params=_cparams(1),
        name="combine",
    )(dest3, dest3, gate_rows, ln_final, x1, ys)


def kernel(x, positions, ln_mix, w_in, pe_cmp_k, w_cmp_k1, w_cmp_k2, pe_cmp_v, w_cmp_v1, w_cmp_v2, w_pool, b_pool, pool_scale, gn_nsa, gn_pool, w_out, ln_moe, w_router_group, b_router_group, w_router_expert, b_router_expert, w_gate, w_up, w_down, ln_final):
    batch, seq, d = x.shape
    n = batch * seq
    depth = w_in.shape[0]

    inv_freq = ROPE_THETA ** (-jnp.arange(0, ROT_DIM, 2, dtype=F32) / ROT_DIM)
    ang_t = inv_freq[:, None] * positions.astype(F32).reshape(1, n)
    kblk = (jnp.arange(seq, dtype=I32)[:, None] // SEL_LEN == jnp.arange(LANES, dtype=I32)[None, :]).astype(BF16)

    x2d = x.reshape(n, d)
    assert depth == 1, "single-layer operation"
    for l in range(depth):
        gate_rows_pad = -(-N_GATES // SUBLANES) * SUBLANES
        qnt, qrt, kvc, kk, vt, gt, u = _inproj(x2d, ln_mix[l][None], ang_t, jnp.swapaxes(w_in, 1, 2),
                                               gate_rows_pad, batch, seq)

        half = CMP_LEN * HEAD_DIM // 2
        w2 =jnp.stack([w_cmp_k2[l], w_cmp_v2[l]]).astype(BF16)
        def pe_rows(pe):
            return jnp.pad(pe.reshape(2, half), ((0, SUBLANES - 2), (0, 0)))
        pe2 = jnp.stack([pe_rows(pe_cmp_k[l]), pe_rows(pe_cmp_v[l])]).astype(BF16)
        cmp = _compress(kvc, pe2, w_cmp_k1[l], w_cmp_v1[l], w2)

        mix_nsa = _nsa(qnt, qrt, cmp, kk, vt, kblk, gt, gn_nsa[l][None])
        mix_pool = _pool(u.reshape(batch, seq, -1), w_pool[l].astype(BF16), b_pool[l][None],
                         pool_scale[l][None], gn_pool[l][None])

        ng, ne = N_EXPERT_GROUPS, EXPERTS_PER_GROUP
        w_r = jnp.concatenate([
            w_router_group[l].T, jnp.zeros((ne - ng, d), F32),
            jnp.transpose(w_router_expert[l], (0, 2, 1)).reshape(ng * ne, d),
            jnp.zeros((LANES - ne - ng * ne, d), F32)], axis=0).astype(BF16)
        b_r = jnp.concatenate([
            b_router_group[l], jnp.zeros((ne - ng,), F32), b_router_expert[l].reshape(-1),
            jnp.zeros((LANES - ne - ng * ne,), F32)])
        b_r = jnp.broadcast_to(b_r[:, None], (LANES, LANES))
        x1, h2, eid, gate = _outproj(mix_nsa, mix_pool, x2d, w_out, ln_moe[l][None], w_r, b_r)

        dest, meta = _dispatch(eid)
        cap = (2 * n // TM_MOE + N_EXPERTS) * TM_MOE
        xs = _scatter(_grouped_dest(dest, TS_SCATTER, DMA_ISSUE_GROUP), meta, h2, cap)
        ys = _experts(meta, xs, w_gate[l], w_up[l], w_down[l])
        dest_tok = _grouped_dest(dest, TM_COMBINE, DMA_ISSUE_GROUP)
        gate_rows = jnp.pad(gate[:2].T, ((0, 0), (0, LANES - 2)))
        x2d = _combine(dest_tok, x1, gate_rows, ln_final[None], ys)
    return x2d.reshape(batch, seq, d)
```

```python
import jax
import jax.numpy as jnp
from jax import lax
from jax.experimental import pallas as pl
from jax.experimental.pallas import tpu as pltpu

F32 = jnp.float32
BF16 = jnp.bfloat16
I32 = jnp.int32

HEAD_DIM = 128
N_HEADS = 8
N_KV = 2
GQA = N_HEADS // N_KV
N_GATES = 3 * N_HEADS
ROT_DIM = HEAD_DIM // 4
ROPE_THETA = 500000.0
CMP_LEN = 32
CMP_STRIDE = 16
CMP_HIDDEN = 2 * HEAD_DIM
SEL_LEN = 64
SEL_TOPK = 16
N_LOCAL = 2
WINDOW = 512
POOL_SIZES = (2, 4, 8, 16)
N_EXPERT_GROUPS = 4
EXPERTS_PER_GROUP = 8
N_EXPERTS = N_EXPERT_GROUPS * EXPERTS_PER_GROUP
EPS = 1e-6
NEG = -1e30
BIG = 1e30
SCALE = HEAD_DIM ** -0.5
LOG2E = 1.4426950408889634
Q_SCALE = SCALE * LOG2E

LANES = 128
SUBLANES = 8
VMEM_LIMIT_BYTES = 56 * 1024 * 1024

TM_PROJ = 512
TQ = 256
TK = 256
SEL_MASK_BIAS = 2.0 ** 40
V_ONES_ROWS = 16
TM_MOE = 256
TS_SCATTER = 2048
TM_COMBINE = 512
WEIGHT_DMA_PRIORITIES = (0, 1, 1)
W_STAGE_ROWS = 256
W_SLOTS = 3
DMA_ISSUE_GROUP = 64
T_RANK = 512


def _cparams(n_axes):
    return pltpu.CompilerParams(dimension_semantics=("arbitrary",) * n_axes, vmem_limit_bytes=VMEM_LIMIT_BYTES)


def _dot(a, b):
    return jnp.dot(a, b, preferred_element_type=F32)


def _dot_nt(a, b):
    return lax.dot_general(a, b, (((1,), (1,)), ((), ())), preferred_element_type=F32)


U32 = jnp.uint32
_HI16 = 0xFFFF0000


def _pack_bf16_halves(x):
    c = x.shape[1] // 2
    bits = lambda t: lax.bitcast_convert_type(t.astype(BF16).astype(F32), U32)
    return lax.shift_right_logical(bits(x[:, :c]), U32(16)) | (bits(x[:, c:]) & U32(_HI16))


def _store_token_tiles(ref, words):
    m, width = words.shape
    assert width == SUBLANES * LANES
    for j in range(SUBLANES):
        ref[pl.ds(j, m, stride=SUBLANES), :] = words[:, j * LANES:(j + 1) * LANES]


def _load_token_tiles(ref):
    m = ref.shape[0] // SUBLANES
    return jnp.concatenate([ref[pl.ds(j, m, stride=SUBLANES), :] for j in range(SUBLANES)], axis=1)


def _unpack_bf16_halves(w):
    lo = lax.bitcast_convert_type(lax.shift_left(w, U32(16)), F32)
    hi = lax.bitcast_convert_type(w & U32(_HI16), F32)
    return lo, hi


def _rope(t, cos, sin, axis):
    idx = lax.broadcasted_iota(I32, t.shape, axis)
    partner = jnp.where(idx < ROT_DIM // 2,
                        pltpu.roll(t, HEAD_DIM - ROT_DIM // 2, axis),
                        pltpu.roll(t, ROT_DIM // 2, axis))
    return t * cos + partner * sin


def _inproj_kernel(x_ref, ln_ref, ang_ref, wt_hbm,
                   qnt_ref, qrt_ref, kvc_ref, kk_ref, vt_ref, gt_ref, u_ref,
                   kc_scr, stage, wq_ref, wv_ref, wg_ref, wkc_ref, wk_ref, wu_ref, wsem):
    @pl.when(pl.program_id(0) == 0)
    def _():
        rc = stage.shape[1]
        q_w, kv_w = N_HEADS * HEAD_DIM, N_KV * HEAD_DIM
        o_kc, o_ks = q_w, q_w + 2 * kv_w
        o_vs, o_kw, o_vw, o_g = o_ks + kv_w, o_ks + 2 * kv_w, o_ks + 3 * kv_w, o_ks + 4 * kv_w
        o_u = o_g + N_GATES
        pieces = ([(r0, rc, wq_ref, r0) for r0 in range(0, q_w, rc)]
                  + [(o_kc + r0, rc, wkc_ref, r0) for r0 in range(0, 2 * kv_w, rc)]
                  + [(o_ks, kv_w, wk_ref, 0), (o_kw, kv_w, wk_ref, kv_w)]
                  + [(o_vs, kv_w, wv_ref, 0), (o_vw, kv_w, wv_ref, kv_w)]
                  + [(o_g, N_GATES, wg_ref, 0)]
                  + [(o_u + r0, rc, wu_ref, r0) for r0 in range(0, wu_ref.shape[0], rc)])
        assert all(n <= rc and src % SUBLANES == 0 for src, n, _, _ in pieces)
        wg_ref[...] = jnp.zeros(wg_ref.shape, BF16)

        def piece_copy(i, slot):
            src, rows, _, _ = pieces[i]
            return pltpu.make_async_copy(wt_hbm.at[0, pl.ds(src, rows)], stage.at[slot, pl.ds(0, rows)],
                                         wsem.at[slot])

        piece_copy(0, 0).start()
        for i, (src, rows, dst, dst_row) in enumerate(pieces):
            slot = i % 2
            if i + 1 < len(pieces):
                piece_copy(i + 1, 1 - slot).start()
            piece_copy(i, slot).wait()
            dst[dst_row:dst_row + rows, :] = stage[slot, :rows, :].astype(BF16)

    x = x_ref[...]
    ms = jnp.mean(x * x, axis=-1, keepdims=True)
    hb = (x * lax.rsqrt(ms + EPS) * ln_ref[...]).astype(BF16)
    chunk = 4 * HEAD_DIM

    ang = ang_ref[...]
    ca, sa = jnp.cos(ang), jnp.sin(ang)
    rest = (HEAD_DIM - ROT_DIM, ang.shape[1])
    cost = jnp.concatenate([ca, ca, jnp.ones(rest, F32)], axis=0)
    sint = jnp.concatenate([-sa, sa, jnp.zeros(rest, F32)], axis=0)
    cos, sin = cost.T, sint.T
    for ch in range(2):
        rt = _dot_nt(wq_ref[ch * chunk:(ch + 1) * chunk, :], hb)
        for j in range(4):
            t = rt[j * HEAD_DIM:(j + 1) * HEAD_DIM, :]
            qnt_ref[0, ch * 4 + j] = (t * Q_SCALE).astype(BF16)
            qrt_ref[0, ch * 4 + j] = (_rope(t, cost, sint, 0) * Q_SCALE).astype(BF16)
    rt = _dot_nt(wv_ref[...], hb)
    ones = jnp.ones((V_ONES_ROWS, TK), BF16)
    for j in range(4):
        for s in range(vt_ref.shape[2]):
            vt_ref[0, j, s] = jnp.concatenate(
                [rt[j * HEAD_DIM:(j + 1) * HEAD_DIM, s * TK:(s + 1) * TK].astype(BF16), ones], axis=0)
    gt_ref[...] = _dot_nt(wg_ref[...], hb)

    r = _dot_nt(hb, wkc_ref[...])
    groups = kvc_ref.shape[2]
    for j in range(4):
        kc_scr[j] = r[:, j * HEAD_DIM:(j + 1) * HEAD_DIM]
        for i in range(CMP_STRIDE):
            kvc_ref[0, j, :, i * HEAD_DIM:(i + 1) * HEAD_DIM] = (
                kc_scr[j, pl.ds(i, groups, stride=CMP_STRIDE), :].astype(BF16))
    r = _dot_nt(hb, wk_ref[...])
    for j in range(4):
        kk_ref[0, j] = _rope(r[:, j * HEAD_DIM:(j + 1) * HEAD_DIM], cos, sin, 1).astype(BF16)
    for ch in range(u_ref.shape[1] // chunk):
        u_ref[:, ch * chunk:(ch + 1) * chunk] = _dot_nt(hb, wu_ref[ch * chunk:(ch + 1) * chunk, :])


def _inproj(x2d, ln, ang_t, w_in_t, gate_rows, batch, seq):
    n, d = x2d.shape
    tm = TM_PROJ
    assert tm % TK == 0
    spb = seq // tm
    const = lambda i: (0, 0)
    q_w, kv_w = N_HEADS * HEAD_DIM, N_KV * HEAD_DIM
    pool_w = w_in_t.shape[1] - (q_w + 6 * kv_w + N_GATES)
    row_heads = lambda i: (i // spb, 0, i % spb, 0)
    col_heads = lambda i: (i // spb, 0, 0, i % spb)
    return pl.pallas_call(
        _inproj_kernel,
        grid=(n // tm,),
        in_specs=[
            pl.BlockSpec((tm, d), lambda i: (i, 0)),
            pl.BlockSpec((1, d), const),
            pl.BlockSpec((ang_t.shape[0], tm), lambda i: (0, i)),
            pl.BlockSpec(memory_space=pl.ANY),
        ],
        out_specs=[
            pl.BlockSpec((1, N_HEADS, HEAD_DIM, tm), col_heads),
            pl.BlockSpec((1, N_HEADS, HEAD_DIM, tm), col_heads),
            pl.BlockSpec((1, 4, tm // CMP_STRIDE, CMP_STRIDE * HEAD_DIM), row_heads),
            pl.BlockSpec((1, 4, tm, HEAD_DIM), row_heads),
            pl.BlockSpec((1, 4, tm // TK, HEAD_DIM + V_ONES_ROWS, TK), lambda i: (i // spb, 0, i % spb, 0, 0)),
            pl.BlockSpec((gate_rows, tm), lambda i: (0, i)),
            pl.BlockSpec((tm, pool_w), lambda i: (i, 0)),
        ],
        out_shape=[
            jax.ShapeDtypeStruct((batch, N_HEADS, HEAD_DIM, seq), BF16),
            jax.ShapeDtypeStruct((batch, N_HEADS, HEAD_DIM, seq), BF16),
            jax.ShapeDtypeStruct((batch, 4, seq // CMP_STRIDE, CMP_STRIDE * HEAD_DIM), BF16),
            jax.ShapeDtypeStruct((batch, 4, seq, HEAD_DIM), BF16),
            jax.ShapeDtypeStruct((batch, 4, seq // TK, HEAD_DIM + V_ONES_ROWS, TK), BF16),
            jax.ShapeDtypeStruct((gate_rows, n), F32),
            jax.ShapeDtypeStruct((n, pool_w), F32),
        ],
        scratch_shapes=[
            pltpu.VMEM((4, tm, HEAD_DIM), F32),
            pltpu.VMEM((2, W_STAGE_ROWS, d), F32),
            pltpu.VMEM((q_w, d), BF16), pltpu.VMEM((2 * kv_w, d), BF16), pltpu.VMEM((gate_rows, d), BF16),
            pltpu.VMEM((2 * kv_w, d), BF16), pltpu.VMEM((2 * kv_w, d), BF16), pltpu.VMEM((pool_w, d), BF16),
            pltpu.SemaphoreType.DMA((2,)),
        ],
        compiler_params=_cparams(1),
        name="inproj",
    )(x2d, ln, ang_t, w_in_t)


def _compress_kernel(t_ref, pe_ref, wk1_ref, wv1_ref, w2_ref, o_ref, w1_scr):
    half = w1_scr.shape[1]

    @pl.when(pl.program_id(0) == 0)
    def _():
        for kind, w_ref in enumerate((wk1_ref, wv1_ref)):
            for h in range(2):
                w1_scr[kind, :, h * CMP_HIDDEN:(h + 1) * CMP_HIDDEN] = w_ref[pl.ds(h * half, half), :].astype(BF16)

    for j in range(t_ref.shape[1]):
        kind = j // N_KV
        t2 = t_ref[0, j]
        w1 = w1_scr[kind]
        a = _dot(t2, w1)
        pb = _dot(pe_ref[kind], w1)
        bias = pb[0:1, :CMP_HIDDEN] + pb[1:2, CMP_HIDDEN:]
        nrow = a.shape[0]
        nxt = pltpu.roll(a[:, CMP_HIDDEN:], nrow - 1, 0)
        hid = jax.nn.gelu(a[:, :CMP_HIDDEN] + nxt + bias)
        out = _dot(hid.astype(BF16), w2_ref[kind])
        row = lax.broadcasted_iota(I32, out.shape, 0)
        out = jnp.where(row < nrow - 1, out, 0.0)
        o_ref[0, j] = (out.T if kind else out).astype(BF16)


def _compress(kvc2, pe2, wk1, wv1, w2):
    batch, four, nrow, width = kvc2.shape
    assert wk1.shape == wv1.shape == (2 * width, CMP_HIDDEN)
    whole = lambda a: pl.BlockSpec(a.shape, lambda b: (0,) * a.ndim)
    return pl.pallas_call(
        _compress_kernel,
        grid=(batch,),
        in_specs=[pl.BlockSpec((1, four, nrow, width), lambda b: (b, 0, 0, 0)), whole(pe2), whole(wk1), whole(wv1),
                  whole(w2)],
        out_specs=pl.BlockSpec((1, four, nrow, HEAD_DIM), lambda b: (b, 0, 0, 0)),
        out_shape=jax.ShapeDtypeStruct((batch, four, nrow, HEAD_DIM), BF16),
        scratch_shapes=[pltpu.VMEM((2, width, 2 * CMP_HIDDEN), BF16)],
        compiler_params=_cparams(1),
        name="compress",
    )(kvc2, pe2, wk1, wv1, w2)


def _nsa_kernel(qnt_ref, qrt_ref, cmp_ref, kk_ref, vt_ref, kblk_ref, gt_ref, gn_ref, o_ref,
                m_scr, acc_scr, qa_scr, ocmp_scr, s_scr, o_scr):
    qi = pl.program_id(1)
    tq = qnt_ref.shape[3]
    rows = GQA * tq
    n_cmp_pad = cmp_ref.shape[2]
    n_sel = kk_ref.shape[2] // SEL_LEN
    q0 = qi * tq

    m_scr[...] = jnp.full(m_scr.shape, NEG, F32)
    acc_scr[...] = jnp.zeros(acc_scr.shape, F32)

    def scores(item, slot):
        c, g, q_rows, k_tile, v_t, bias = item
        s_scr[slot] = _dot(k_tile, qa_scr[g, :q_rows, :])

    def softmax_pv(item, slot):
        c, g, q_rows, k_tile, v_t, bias = item
        s = s_scr[slot]
        if bias is not None:
            s = s + bias
        m_old = m_scr[c]
        m_new = jnp.maximum(m_old, jnp.max(s, axis=0, keepdims=True))
        alpha = jnp.exp2(m_old - m_new)
        p = jnp.exp2(s - m_new)
        acc_scr[c] = alpha * acc_scr[c] + _dot(v_t, p.astype(BF16))
        m_scr[c] = m_new

    def flash_out(c):
        return acc_scr[c, :HEAD_DIM, :] / acc_scr[c, HEAD_DIM:HEAD_DIM + 1, :]

    def tile_heads(a):
        return jnp.concatenate([a] * GQA, axis=1)

    kk_i = lax.broadcasted_iota(I32, (TK, tq), 0)
    qq_i = lax.broadcasted_iota(I32, (TK, tq), 1)
    causal_bias = tile_heads(jnp.where(kk_i <= qq_i, 0.0, NEG))
    band_bias = tile_heads(jnp.where(kk_i > qq_i, 0.0, NEG))
    n_win = WINDOW // TK

    for g in range(N_KV):
        qn4 = jnp.concatenate([qnt_ref[0, h] for h in range(g * GQA, (g + 1) * GQA)], axis=1)
        s = _dot(cmp_ref[0, g], qn4)
        n_idx = lax.broadcasted_iota(I32, (n_cmp_pad, rows), 0)
        t_idx = q0 + jnp.bitwise_and(lax.broadcasted_iota(I32, (n_cmp_pad, rows), 1), tq - 1)
        cmp_ok = n_idx * CMP_STRIDE + (CMP_LEN - 1) <= t_idx
        s = jnp.where(cmp_ok, s, NEG)
        m = jnp.max(s, axis=0, keepdims=True)
        e = jnp.where(cmp_ok, jnp.exp2(s - m), 0.0)
        l = jnp.sum(e, axis=0, keepdims=True)
        p = e * jnp.where(l > 0.0, 1.0 / l, 0.0)
        pb = p.astype(BF16)
        ocmp_scr[g] = _dot(cmp_ref[0, N_KV + g], pb)

        jn = lax.broadcasted_iota(I32, (n_sel, n_cmp_pad), 0) * SEL_LEN
        cn = lax.broadcasted_iota(I32, (n_sel, n_cmp_pad), 1) * CMP_STRIDE
        ov = jnp.clip(jnp.minimum(cn + CMP_LEN, jn + SEL_LEN) - jnp.maximum(cn, jn), 0, None)
        w_t = (ov.astype(F32) * (1.0 / CMP_LEN)).astype(BF16)
        imp4 = _dot(w_t, pb)
        imp = imp4[:, 0:tq]
        for r in range(1, GQA):
            imp = imp + imp4[:, r * tq:(r + 1) * tq]
        j_blk = lax.broadcasted_iota(I32, (n_sel, tq), 0)
        t_q = q0 + lax.broadcasted_iota(I32, (n_sel, tq), 1)
        sel_ok = j_blk * SEL_LEN <= t_q
        back = t_q // SEL_LEN - j_blk
        forced = (j_blk == 0) | ((back >= 0) & (back < N_LOCAL))
        val = jnp.where(sel_ok & forced, BIG, jnp.where(sel_ok, imp, -BIG))
        rank = jnp.zeros((n_sel, tq), F32)
        for i in range(n_sel):
            vi = val[i:i + 1, :]
            beats = (vi > val) | ((vi == val) & (j_blk > i))
            rank = rank + beats.astype(F32)
        keep = (rank < float(min(SEL_TOPK, n_sel))) & sel_ok
        bias_t = jnp.where(keep, 0.0, -SEL_MASK_BIAS)
        bias_t = jnp.concatenate([bias_t, jnp.zeros((HEAD_DIM - n_sel, tq), F32)], axis=0)
        qa_scr[g, :HEAD_DIM, :] = jnp.concatenate(
            [qrt_ref[0, h] for h in range(g * GQA, (g + 1) * GQA)], axis=1)
        qa_scr[g, HEAD_DIM:, :] = tile_heads(bias_t.astype(BF16))

    def sel_item(g, kj, bias):
        st = pl.multiple_of(kj * TK, TK)
        k_aug = jnp.concatenate([kk_ref[0, g, pl.ds(st, TK), :], kblk_ref[pl.ds(st, TK), :]], axis=1)
        return (g, g, 2 * HEAD_DIM, k_aug, vt_ref[0, g, kj], bias)

    def win_item(g, back):
        kj = qi - back
        st = pl.multiple_of(kj * TK, TK)
        bias = causal_bias if back == 0 else (band_bias if back == n_win else None)
        return (N_KV + g, g, HEAD_DIM, kk_ref[0, N_KV + g, pl.ds(st, TK), :], vt_ref[0, N_KV + g, kj], bias)

    assert N_KV == 2
    scores(sel_item(0, 0, None), 0)

    def sel_tile(kj):
        scores(sel_item(1, kj, None), 1)
        softmax_pv(sel_item(0, kj, None), 0)
        scores(sel_item(0, kj + 1, None), 0)
        softmax_pv(sel_item(1, kj, None), 1)

    def sel_pair(j, carry):
        sel_tile(2 * j)
        sel_tile(2 * j + 1)
        return carry

    lax.fori_loop(0, qi // 2, sel_pair, 0)

    @pl.when(qi % 2 == 1)
    def _():
        sel_tile(qi - 1)

    def run_tail(farthest):
        tail = ([sel_item(g, qi, causal_bias) for g in range(N_KV)]
                + [win_item(g, back) for back in range(farthest, -1, -1) for g in range(N_KV)])
        for i, item in enumerate(tail):
            if i + 1 < len(tail):
                scores(tail[i + 1], (i + 1) % 2)
            softmax_pv(item, i % 2)

    for nb in range(n_win):
        @pl.when(qi == nb)
        def _(nb=nb):
            run_tail(nb)

    @pl.when(qi >= n_win)
    def _():
        run_tail(n_win)

    gsig = jax.nn.sigmoid(gt_ref[...])
    for g in range(N_KV):
        o_cmp, o_sel, o_win = ocmp_scr[g], flash_out(g), flash_out(N_KV + g)
        for r in range(GQA):
            h = g * GQA + r
            sl = slice(r * tq, (r + 1) * tq)
            o_h = (gsig[3 * h:3 * h + 1, :] * o_cmp[:, sl]
                   + gsig[3 * h + 1:3 * h + 2, :] * o_sel[:, sl]
                   + gsig[3 * h + 2:3 * h + 3, :] * o_win[:, sl])
            o_scr[:, h * HEAD_DIM:(h + 1) * HEAD_DIM] = o_h.T

    o = o_scr[...]
    ms = jnp.mean(o * o, axis=-1, keepdims=True)
    o_ref[...] = (o * lax.rsqrt(ms + EPS) * gn_ref[...]).astype(BF16)


def _nsa(qnt, qrt, cmp, kk, vt, kblk, gt, gn):
    batch, _, _, seq = qnt.shape
    assert TQ == TK and WINDOW % TK == 0 and TQ & (TQ - 1) == 0
    nq = seq // TQ
    width = N_HEADS * HEAD_DIM
    return pl.pallas_call(
        _nsa_kernel,
        grid=(batch, nq),
        in_specs=[
            pl.BlockSpec((1, N_HEADS, HEAD_DIM, TQ), lambda b, i: (b, 0, 0, i)),
            pl.BlockSpec((1, N_HEADS, HEAD_DIM, TQ), lambda b, i: (b, 0, 0, i)),
            pl.BlockSpec((1,) + cmp.shape[1:], lambda b, i: (b, 0, 0, 0)),
            pl.BlockSpec((1,) + kk.shape[1:], lambda b, i: (b, 0, 0, 0)),
            pl.BlockSpec((1,) + vt.shape[1:], lambda b, i: (b, 0, 0, 0, 0)),
            pl.BlockSpec((seq, LANES), lambda b, i: (0, 0)),
            pl.BlockSpec((gt.shape[0], TQ), lambda b, i: (0, b * nq + i)),
            pl.BlockSpec((1, width), lambda b, i: (0, 0)),
        ],
        out_specs=pl.BlockSpec((TQ, width), lambda b, i: (b * nq + i, 0)),
        out_shape=jax.ShapeDtypeStruct((batch * seq, width), BF16),
        scratch_shapes=[
            pltpu.VMEM((2 * N_KV, 1, GQA * TQ), F32),
            pltpu.VMEM((2 * N_KV, vt.shape[3], GQA * TQ), F32),
            pltpu.VMEM((N_KV, 2 * HEAD_DIM, GQA * TQ), BF16),
            pltpu.VMEM((N_KV, HEAD_DIM, GQA * TQ), F32),
            pltpu.VMEM((2, TK, GQA * TQ), F32),
            pltpu.VMEM((TQ, width), F32),
        ],
        compiler_params=_cparams(2),
        name="nsa",
    )(qnt, qrt, cmp, kk, vt, kblk, gt, gn)


def _pool_group(cur, halo_rows, t1, window, w_mat, bias, scale):
    halo = halo_rows.shape[0]
    acc = jnp.concatenate([halo_rows, cur], axis=0)
    span = 1
    while span < window:
        acc = acc + jnp.concatenate([jnp.zeros((span, acc.shape[1]), F32), acc[:-span]], axis=0)
        span *= 2
    mean = acc[halo:] / jnp.minimum(t1, float(window))
    return (_dot((mean - cur).astype(BF16), w_mat) + bias) * scale


def _pool_kernel(u_ref, w_ref, b_ref, sc_ref, gn_ref, o_ref, y_scr, halo_scr):
    si = pl.program_id(1)
    tm = u_ref.shape[1]
    halo = halo_scr.shape[0]

    @pl.when(si == 0)
    def _():
        halo_scr[...] = jnp.zeros(halo_scr.shape, F32)

    t1 = (si * tm + 1 + lax.broadcasted_iota(I32, (tm, 1), 0)).astype(F32)
    cg = w_ref.shape[1]
    for gi, window in enumerate(POOL_SIZES):
        sl = slice(gi * cg, (gi + 1) * cg)
        cur = u_ref[0, :, sl]
        y_scr[:, sl] = _pool_group(cur, halo_scr[:, sl], t1, window, w_ref[gi], b_ref[:, sl], sc_ref[:, sl])
        halo_scr[:, sl] = cur[tm - halo:, :]
    y = y_scr[...]
    ms = jnp.mean(y * y, axis=-1, keepdims=True)
    o_ref[...] = (y * lax.rsqrt(ms + EPS) * gn_ref[...]).astype(BF16)


def _pool(u3, w_pool, b_pool, pool_scale, gn_pool):
    batch, seq, c = u3.shape
    tm = TM_PROJ
    ns = seq // tm
    vec = lambda b, i: (0, 0)
    return pl.pallas_call(
        _pool_kernel,
        grid=(batch, ns),
        in_specs=[
            pl.BlockSpec((1, tm, c), lambda b, i: (b, i, 0)),
            pl.BlockSpec(w_pool.shape, lambda b, i: (0, 0, 0)),
            pl.BlockSpec((1, c), vec),
            pl.BlockSpec((1, c), vec),
            pl.BlockSpec((1, c), vec),
        ],
        out_specs=pl.BlockSpec((tm, c), lambda b, i: (b * ns + i, 0)),
        out_shape=jax.ShapeDtypeStruct((batch * seq, c), BF16),
        scratch_shapes=[pltpu.VMEM((tm, c), F32), pltpu.VMEM((max(POOL_SIZES), c), F32)],
        compiler_params=_cparams(2),
        name="pool",
    )(u3, w_pool, b_pool, pool_scale, gn_pool)


def _outproj_kernel(mn_ref, mp_ref, x_ref, w_hbm, ln_ref, wr_ref, br_ref,
                    x1_ref, h2_ref, eid_ref, gate_ref, x1_scr, ssq_scr, stage, w_ref, wsem):
    tm, d = x_ref.shape
    half = mn_ref.shape[1]
    chunk = 512

    @pl.when(pl.program_id(0) == 0)
    def _():
        x1_scr[...] = jnp.zeros(x1_scr.shape, F32)
        ssq_scr[...] = jnp.zeros(ssq_scr.shape, F32)
        rc = stage.shape[1]
        n_chunks = w_ref.shape[0] // rc

        def chunk_copy(c, slot):
            return pltpu.make_async_copy(w_hbm.at[0, pl.ds(c * rc, rc)], stage.at[slot], wsem.at[slot])

        chunk_copy(0, 0).start()
        for c in range(n_chunks):
            slot = c % 2
            if c + 1 < n_chunks:
                chunk_copy(c + 1, 1 - slot).start()
            chunk_copy(c, slot).wait()
            w_ref[c * rc:(c + 1) * rc, :] = stage[slot].astype(BF16)

    h2 = x1_scr[...] * lax.rsqrt(ssq_scr[...] * (1.0 / d) + EPS) * ln_ref[...]
    _store_token_tiles(h2_ref, _pack_bf16_halves(h2))

    logits = _dot_nt(wr_ref[...], h2.astype(BF16)) + br_ref[:, 0:1]
    ng, ne = N_EXPERT_GROUPS, EXPERTS_PER_GROUP
    row = lax.broadcasted_iota(I32, (ne, tm), 0)
    lg = jnp.where(row < ng, logits[0:ne], NEG)
    mg = jnp.max(lg, axis=0, keepdims=True)
    g_sel = jnp.min(jnp.where(lg == mg, row, ne), axis=0, keepdims=True)
    p_g = 1.0 / jnp.sum(jnp.exp(lg - mg), axis=0, keepdims=True)
    le = jnp.zeros((ne, tm), F32)
    for g in range(ng):
        le = jnp.where(g_sel == g, logits[ne * (g + 1):ne * (g + 2)], le)
    ex = jnp.exp(le - jnp.max(le, axis=0, keepdims=True))
    pe = ex / jnp.sum(ex, axis=0, keepdims=True)
    p1 = jnp.max(pe, axis=0, keepdims=True)
    i1 = jnp.min(jnp.where(pe == p1, row, ne), axis=0, keepdims=True)
    rest = jnp.where(row == i1, -1.0, pe)
    p2 = jnp.max(rest, axis=0, keepdims=True)
    i2 = jnp.min(jnp.where(rest == p2, row, ne), axis=0, keepdims=True)
    psum = p1 + p2
    zero_i = jnp.zeros((ne, tm), I32)
    eid = jnp.where(row == 0, g_sel * ne + i1, jnp.where(row == 1, g_sel * ne + i2, zero_i))
    gate = jnp.where(row == 0, p_g * p1 / psum, jnp.where(row == 1, p_g * p2 / psum, 0.0))
    eid_ref[...] = eid
    gate_ref[...] = gate

    mn = mn_ref[...]
    mp = mp_ref[...]
    ssq = jnp.zeros((tm, 1), F32)
    for c in range(d // chunk):
        sl = slice(c * chunk, (c + 1) * chunk)
        x1 = x_ref[:, sl] + (_dot(mn, w_ref[:half, sl]) + _dot(mp, w_ref[half:, sl]))
        x1_ref[:, sl] = x1
        x1_scr[:, sl] = x1
        ssq = ssq + jnp.sum(x1 * x1, axis=-1, keepdims=True)
    ssq_scr[...] = ssq


def _outproj(mix_nsa, mix_pool, x2d, w_out, ln_moe, w_router_t, b_router):
    n, d = x2d.shape
    tm = TM_PROJ
    half = mix_nsa.shape[1]
    const = lambda i: (0, 0)
    last = n // tm - 1
    cur = lambda i: (jnp.minimum(i, last), 0)
    prev = lambda i: (jnp.maximum(i - 1, 0), 0)
    prev_t = lambda i: (0, jnp.maximum(i - 1, 0))
    return pl.pallas_call(
        _outproj_kernel,
        grid=(n // tm + 1,),
        in_specs=[
            pl.BlockSpec((tm, half), cur),
            pl.BlockSpec((tm, half), cur),
            pl.BlockSpec((tm, d), cur),
            pl.BlockSpec(memory_space=pl.ANY),
            pl.BlockSpec((1, d), const),
            pl.BlockSpec(w_router_t.shape, const),
            pl.BlockSpec(b_router.shape, const),
        ],
        out_specs=[
            pl.BlockSpec((tm, d), cur),
            pl.BlockSpec((tm * SUBLANES, LANES), prev),
            pl.BlockSpec((SUBLANES, tm), prev_t),
            pl.BlockSpec((SUBLANES, tm), prev_t),
        ],
        scratch_shapes=[pltpu.VMEM((tm, d), F32), pltpu.VMEM((tm, 1), F32),
                        pltpu.VMEM((2, W_STAGE_ROWS, d), F32), pltpu.VMEM(w_out.shape[1:], BF16),
                        pltpu.SemaphoreType.DMA((2,))],
        out_shape=[
            jax.ShapeDtypeStruct((n, d), F32),
            jax.ShapeDtypeStruct((n * SUBLANES, LANES), U32),
            jax.ShapeDtypeStruct((SUBLANES, n), I32),
            jax.ShapeDtypeStruct((SUBLANES, n), F32),
        ],
        compiler_params=_cparams(1),
        name="outproj",
    )(mix_nsa, mix_pool, x2d, w_out, ln_moe, w_router_t, b_router)


def _dispatch_kernel(eid_ref, dest_ref, meta_ref, rank_scr):
    n = eid_ref.shape[1]
    t = T_RANK
    row = lax.broadcasted_iota(I32, (N_EXPERTS, t), 0)
    before = (lax.broadcasted_iota(I32, (t, t), 0) < lax.broadcasted_iota(I32, (t, t), 1)).astype(BF16)
    carry = jnp.zeros((N_EXPERTS, 1), F32)
    for k in range(2):
        for j in range(n // t):
            sl = slice(j * t, (j + 1) * t)
            oh = row == eid_ref[k:k + 1, sl]
            ohf = oh.astype(F32)
            prior = _dot(ohf.astype(BF16), before) + carry
            rank_scr[k:k + 1, sl] = jnp.sum(jnp.where(oh, prior, 0.0), axis=0, keepdims=True)
            carry = carry + jnp.sum(ohf, axis=1, keepdims=True)
    counts = jnp.broadcast_to(carry, (N_EXPERTS, LANES))
    padded = jnp.floor((counts + (TM_MOE - 1.0)) * (1.0 / TM_MOE)) * TM_MOE
    ends = padded
    r_idx = lax.broadcasted_iota(I32, (N_EXPERTS, LANES), 0)
    sh = 1
    while sh < N_EXPERTS:
        ends = ends + jnp.where(r_idx >= sh, pltpu.roll(ends, sh, 0), 0.0)
        sh *= 2
    starts = (ends - padded)[:, 0:1]
    for k in range(2):
        for j in range(n // t):
            sl = slice(j * t, (j + 1) * t)
            oh = row == eid_ref[k:k + 1, sl]
            base = jnp.sum(jnp.where(oh, starts, 0.0), axis=0, keepdims=True)
            dest_ref[k:k + 1, sl] = (rank_scr[k:k + 1, sl] + base).astype(I32)
    for k in range(2, SUBLANES):
        dest_ref[k:k + 1, :] = jnp.zeros((1, n), I32)
    blk_start = lax.broadcasted_iota(I32, (N_EXPERTS, LANES), 1).astype(F32) * TM_MOE
    blk_expert = jnp.sum((ends <= blk_start).astype(F32), axis=0, keepdims=True)
    blk_expert = jnp.minimum(blk_expert, N_EXPERTS - 1.0)
    n_used = ends[N_EXPERTS - 1:N_EXPERTS, :] * (1.0 / TM_MOE)
    on_diag = r_idx == lax.broadcasted_iota(I32, (N_EXPERTS, LANES), 1)
    end_blk = jnp.sum(jnp.where(on_diag, ends, 0.0), axis=0, keepdims=True) * (1.0 / TM_MOE)
    n_blk = jnp.sum(jnp.where(on_diag, padded, 0.0), axis=0, keepdims=True) * (1.0 / TM_MOE)
    r8 = lax.broadcasted_iota(I32, (SUBLANES, LANES), 0)
    meta = jnp.where(r8 == 0, blk_expert,
                     jnp.where(r8 == 1, n_used, jnp.where(r8 == 2, end_blk, jnp.where(r8 == 3, n_blk, 0.0))))
    meta_ref[...] = meta.astype(I32)


def _dispatch(eid):
    n = eid.shape[1]
    return pl.pallas_call(
        _dispatch_kernel,
        out_shape=[jax.ShapeDtypeStruct((SUBLANES, n), I32),
                   jax.ShapeDtypeStruct((SUBLANES, LANES), I32)],
        scratch_shapes=[pltpu.VMEM((SUBLANES, n), F32)],
        compiler_params=pltpu.CompilerParams(vmem_limit_bytes=VMEM_LIMIT_BYTES),
        name="dispatch",
    )(eid)


def _row_copy(src_ref, src_row, dst_ref, dst_row, sem):
    tile = lambda ref, row: ref.at[pl.ds(pl.multiple_of(row * SUBLANES, SUBLANES), SUBLANES)]
    return pltpu.make_async_copy(tile(src_ref, src_row), tile(dst_ref, dst_row), sem)


def _scatter_kernel(dest_ref, meta_ref, h_ref, xs_ref, zero_buf, sem, zsem):
    tm = h_ref.shape[0] // SUBLANES
    blk = zero_buf.shape[0]

    @pl.when(pl.program_id(0) == 0)
    def _():
        zero_buf[...] = jnp.zeros(zero_buf.shape, zero_buf.dtype)
        n_used = meta_ref[1, 0]

        def zero_block(b):
            return pltpu.make_async_copy(zero_buf, xs_ref.at[pl.ds(pl.multiple_of(b * blk, blk), blk)], zsem)

        def sweep(act):
            def per_expert(e, c):
                @pl.when(meta_ref[3, e] > 0)
                def _():
                    act(zero_block(meta_ref[2, e] - 1))
                return c

            def per_tail(b, c):
                @pl.when(b >= n_used)
                def _():
                    act(zero_block(b))
                return c

            lax.fori_loop(0, N_EXPERTS, per_expert, 0)
            lax.fori_loop(0, xs_ref.shape[0] // blk, per_tail, 0)

        sweep(lambda cp: cp.start())
        sweep(lambda cp: cp.wait())

    grp = dest_ref.shape[2]
    n_grp = tm // grp

    def start(j, c):
        for u in range(grp):
            for k in range(2):
                _row_copy(h_ref, j * grp + u, xs_ref, dest_ref[0, k * n_grp + j, u], sem).start(priority=k)
        return c

    lax.fori_loop(0, n_grp, start, 0)
    for k in range(2):
        pltpu.make_async_copy(h_ref, xs_ref.at[pl.ds(0, tm * SUBLANES)], sem).wait()


def _grouped_dest(dest, tm, grp):
    n = dest.shape[1]
    return jnp.transpose(dest[:2].reshape(2, n // tm, tm // grp, grp), (1, 0, 2, 3)).reshape(n // tm, 2 * tm // grp, grp)


def _scatter(dest3, meta, h2, cap):
    nsteps, rows, grp = dest3.shape
    tm = rows * grp // 2
    return pl.pallas_call(
        _scatter_kernel,
        grid=(nsteps,),
        in_specs=[
            pl.BlockSpec((1, rows, grp), lambda i: (i, 0, 0), memory_space=pltpu.SMEM),
            pl.BlockSpec(memory_space=pltpu.SMEM),
            pl.BlockSpec((tm * SUBLANES, LANES), lambda i: (i, 0)),
        ],
        out_specs=pl.BlockSpec(memory_space=pl.ANY),
        out_shape=jax.ShapeDtypeStruct((cap * SUBLANES, LANES), h2.dtype),
        scratch_shapes=[pltpu.VMEM((TM_MOE * SUBLANES, LANES), h2.dtype), pltpu.SemaphoreType.DMA(()),
                        pltpu.SemaphoreType.DMA(())],
        compiler_params=_cparams(1),
        name="scatter",
    )(dest3, meta, h2)


def _experts_kernel(meta_ref, xs_ref, wg_hbm, wu_hbm, wd_hbm, ys_ref,
                    wg_buf, wu_buf, wd_buf, wg16, wu16, wd16, elist_ref, ord_ref, sem):
    c = pl.program_id(0)
    n_used = meta_ref[1, 0]
    e = meta_ref[0, c]
    first = (c == 0) | (e != meta_ref[0, jnp.maximum(c - 1, 0)])

    def fetch(expert, slot):
        return [pltpu.make_async_copy(w.at[expert], buf.at[slot], sem.at[slot, i])
                for i, (w, buf) in enumerate(((wg_hbm, wg_buf), (wu_hbm, wu_buf), (wd_hbm, wd_buf)))]

    @pl.when(c == 0)
    def _():
        def scan(b, cnt):
            eb = meta_ref[0, b]
            take = ((b == 0) | (eb != meta_ref[0, jnp.maximum(b - 1, 0)])) & (b < n_used)

            @pl.when(take)
            def _():
                elist_ref[cnt] = eb

            return cnt + take.astype(I32)

        cnt = lax.fori_loop(0, pl.num_programs(0), scan, 0)
        ord_ref[0] = 0
        ord_ref[1] = cnt
        for k in range(W_SLOTS):
            @pl.when(k < cnt)
            def _(k=k):
                for cp, prio in zip(fetch(elist_ref[k], k), WEIGHT_DMA_PRIORITIES):
                    cp.start(priority=prio)

    @pl.when(first & (c < n_used))
    def _():
        @pl.when(c > 0)
        def _():
            ord_ref[0] = ord_ref[0] + 1
        for cp in fetch(e, ord_ref[0] % W_SLOTS):
            cp.wait()

    def mlp(weights):
        lo, hi = _unpack_bf16_halves(_load_token_tiles(xs_ref))
        lo, hi = lo.astype(BF16), hi.astype(BF16)
        half = lo.shape[1]
        wg, wu, wd = weights

        def proj(w):
            return _dot(lo, w[:half, :]) + _dot(hi, w[half:, :])

        act = (jax.nn.silu(proj(wg)) * proj(wu)).astype(BF16)
        _store_token_tiles(ys_ref, _pack_bf16_halves(_dot(act, wd)))

    @pl.when(first & (c < n_used))
    def _():
        k = ord_ref[0]
        slot = k % W_SLOTS
        cast = [buf[slot].astype(BF16) for buf in (wg_buf, wu_buf, wd_buf)]
        for ref16, w in zip((wg16, wu16, wd16), cast):
            ref16[...] = w
        mlp(cast)
        ahead = k + W_SLOTS

        @pl.when(ahead < ord_ref[1])
        def _():
            for cp, prio in zip(fetch(elist_ref[jnp.minimum(ahead, N_EXPERTS - 1)], slot), WEIGHT_DMA_PRIORITIES):
                cp.start(priority=prio)

    @pl.when(jnp.logical_not(first) & (c < n_used))
    def _():
        mlp((wg16[...], wu16[...], wd16[...]))

    @pl.when(c >= n_used)
    def _():
        ys_ref[...] = jnp.zeros(ys_ref.shape, U32)


def _experts(meta, xs, w_gate, w_up, w_down):
    rows = TM_MOE * SUBLANES
    nblk = xs.shape[0] // rows
    d, ff = w_gate.shape[1:]

    def row_map(c, meta):
        return (jnp.minimum(c, meta[1, 0] - 1), 0)

    hbm = pl.BlockSpec(memory_space=pl.ANY)
    return pl.pallas_call(
        _experts_kernel,
        grid_spec=pltpu.PrefetchScalarGridSpec(
            num_scalar_prefetch=1,
            grid=(nblk,),
            in_specs=[pl.BlockSpec((rows, LANES), row_map), hbm, hbm, hbm],
            out_specs=pl.BlockSpec((rows, LANES), lambda c, meta: (c, 0)),
            scratch_shapes=[
                pltpu.VMEM((W_SLOTS, d, ff), F32), pltpu.VMEM((W_SLOTS, d, ff), F32),
                pltpu.VMEM((W_SLOTS, ff, d), F32),
                pltpu.VMEM((d, ff), BF16), pltpu.VMEM((d, ff), BF16), pltpu.VMEM((ff, d), BF16),
                pltpu.SMEM((N_EXPERTS,), I32),
                pltpu.SMEM((2,), I32),
                pltpu.SemaphoreType.DMA((W_SLOTS, 3)),
            ],
        ),
        out_shape=jax.ShapeDtypeStruct(xs.shape, U32),
        compiler_params=_cparams(1),
        name="experts",
    )(meta, xs, w_gate, w_up, w_down)


def _combine_kernel(dest_ref, dest_next_ref, x1_ref, gate_ref, ln_ref, ys_ref, o_ref, ybuf, sem):
    i = pl.program_id(0)
    tm = x1_ref.shape[0]
    slot = i % 2
    grp = dest_ref.shape[2]
    n_grp = tm // grp

    def issue_group(d_ref, s, j):
        for u in range(grp):
            for k in range(2):
                _row_copy(ys_ref, d_ref[0, k * n_grp + j, u], ybuf.at[s, k], j * grp + u, sem.at[s]).start(priority=1)

    @pl.when(i == 0)
    def _():
        def start(j, c):
            issue_group(dest_ref, 0, j)
            return c

        lax.fori_loop(0, n_grp, start, 0)

    for k in range(2):
        pltpu.make_async_copy(ys_ref.at[pl.ds(0, tm * SUBLANES)], ybuf.at[slot, k], sem.at[slot]).wait()
    ln = ln_ref[...]

    def finish(j):
        r0 = pl.multiple_of(j * grp, grp)

        def expert_rows(k):
            words = jnp.concatenate(
                [ybuf[slot, k, pl.ds(r0 * SUBLANES + c, grp, stride=SUBLANES), :] for c in range(SUBLANES)], axis=1)
            return _unpack_bf16_halves(words)

        (lo0, hi0), (lo1, hi1) = expert_rows(0), expert_rows(1)
        gate = gate_ref[pl.ds(r0, grp), :]
        g0, g1 = gate[:, 0:1], gate[:, 1:2]
        moe = jnp.concatenate([lo0 * g0 + lo1 * g1, hi0 * g0 + hi1 * g1], axis=1)
        x2 = x1_ref[pl.ds(r0, grp), :] + moe
        ms = jnp.mean(x2 * x2, axis=-1, keepdims=True)
        o_ref[pl.ds(r0, grp), :] = x2 * lax.rsqrt(ms + EPS) * ln

    def finish_and_prefetch(j, carry):
        finish(j)
        issue_group(dest_next_ref, 1 - slot, j)
        return carry

    def finish_only(j, carry):
        finish(j)
        return carry

    has_next = i + 1 < pl.num_programs(0)

    @pl.when(has_next)
    def _():
        lax.fori_loop(0, tm // grp, finish_and_prefetch, 0)

    @pl.when(jnp.logical_not(has_next))
    def _():
        lax.fori_loop(0, tm // grp, finish_only, 0)


def _combine(dest3, x1, gate_rows, ln_final, ys):
    n, d = x1.shape
    tm = TM_COMBINE
    last = n // tm - 1
    return pl.pallas_call(
        _combine_kernel,
        grid=(n // tm,),
        in_specs=[
            pl.BlockSpec((1,) + dest3.shape[1:], lambda i: (i, 0, 0), memory_space=pltpu.SMEM),
            pl.BlockSpec((1,) + dest3.shape[1:], lambda i: (jnp.minimum(i + 1, last), 0, 0), memory_space=pltpu.SMEM),
            pl.BlockSpec((tm, d), lambda i: (i, 0)),
            pl.BlockSpec((tm, LANES), lambda i: (i, 0)),
            pl.BlockSpec((1, d), lambda i: (0, 0)),
            pl.BlockSpec(memory_space=pl.ANY),
        ],
        out_specs=pl.BlockSpec((tm, d), lambda i: (i, 0)),
        out_shape=jax.ShapeDtypeStruct((n, d), F32),
        scratch_shapes=[pltpu.VMEM((2, 2, tm * SUBLANES, LANES), U32), pltpu.SemaphoreType.DMA((2,))],
        compiler_params=_cparams(1),
        name="combine",
    )(dest3, dest3, x1, gate_rows, ln_final, ys)


def kernel(x, positions, ln_mix, w_in, pe_cmp_k, w_cmp_k1, w_cmp_k2, pe_cmp_v, w_cmp_v1, w_cmp_v2, w_pool, b_pool, pool_scale, gn_nsa, gn_pool, w_out, ln_moe, w_router_group, b_router_group, w_router_expert, b_router_expert, w_gate, w_up, w_down, ln_final):
    batch, seq, d = x.shape
    n = batch * seq
    depth = w_in.shape[0]

    inv_freq = ROPE_THETA ** (-jnp.arange(0, ROT_DIM, 2, dtype=F32) / ROT_DIM)
    ang_t = inv_freq[:, None] * positions.astype(F32).reshape(1, n)
    kblk = (jnp.arange(seq, dtype=I32)[:, None] // SEL_LEN == jnp.arange(LANES, dtype=I32)[None, :]).astype(BF16)

    x2d = x.reshape(n, d)
    assert depth == 1, "single-layer operation"
    for l in range(depth):
        gate_rows_pad = -(-N_GATES // SUBLANES) * SUBLANES
        qnt, qrt, kvc, kk, vt, gt, u = _inproj(x2d, ln_mix[l][None], ang_t, jnp.swapaxes(w_in, 1, 2),
                                               gate_rows_pad, batch, seq)

        half = CMP_LEN * HEAD_DIM // 2
        w2 =jnp.stack([w_cmp_k2[l], w_cmp_v2[l]]).astype(BF16)
        def pe_rows(pe):
            return jnp.pad(pe.reshape(2, half), ((0, SUBLANES - 2), (0, 0)))
        pe2 = jnp.stack([pe_rows(pe_cmp_k[l]), pe_rows(pe_cmp_v[l])]).astype(BF16)
        cmp = _compress(kvc, pe2, w_cmp_k1[l], w_cmp_v1[l], w2)

        mix_nsa = _nsa(qnt, qrt, cmp, kk, vt, kblk, gt, gn_nsa[l][None])
        mix_pool = _pool(u.reshape(batch, seq, -1), w_pool[l].astype(BF16), b_pool[l][None],
                         pool_scale[l][None], gn_pool[l][None])

        ng, ne = N_EXPERT_GROUPS, EXPERTS_PER_GROUP
        w_r = jnp.concatenate([
            w_router_group[l].T, jnp.zeros((ne - ng, d), F32),
            jnp.transpose(w_router_expert[l], (0, 2, 1)).reshape(ng * ne, d),
            jnp.zeros((LANES - ne - ng * ne, d), F32)], axis=0).astype(BF16)
        b_r = jnp.concatenate([
            b_router_group[l], jnp.zeros((ne - ng,), F32), b_router_expert[l].reshape(-1),
            jnp.zeros((LANES - ne - ng * ne,), F32)])
        b_r = jnp.broadcast_to(b_r[:, None], (LANES, LANES))
        x1, h2, eid, gate = _outproj(mix_nsa, mix_pool, x2d, w_out, ln_moe[l][None], w_r, b_r)

        dest, meta = _dispatch(eid)
        cap = (2 * n // TM_MOE + N_EXPERTS) * TM_MOE
        xs = _scatter(_grouped_dest(dest, TS_SCATTER, DMA_ISSUE_GROUP), meta, h2, cap)
        ys = _experts(meta, xs, w_gate[l], w_up[l], w_down[l])
        dest_tok = _grouped_dest(dest, TM_COMBINE, DMA_ISSUE_GROUP)
        gate_rows = jnp.pad(gate[:2].T, ((0, 0), (0, LANES - 2)))
        x2d = _combine(dest_tok, x1, gate_rows, ln_final[None], ys)
    return x2d.reshape(batch, seq, d)
```

```python
import jax
import jax.numpy as jnp
from jax import lax
from jax.experimental import pallas as pl
from jax.experimental.pallas import tpu as pltpu

F32 = jnp.float32
BF16 = jnp.bfloat16
I32 = jnp.int32

HEAD_DIM = 128
N_HEADS = 8
N_KV = 2
GQA = N_HEADS // N_KV
N_GATES = 3 * N_HEADS
ROT_DIM = HEAD_DIM // 4
ROPE_THETA = 500000.0
CMP_LEN = 32
CMP_STRIDE = 16
CMP_HIDDEN = 2 * HEAD_DIM
SEL_LEN = 64
SEL_TOPK = 16
N_LOCAL = 2
WINDOW = 512
POOL_SIZES = (2, 4, 8, 16)
N_EXPERT_GROUPS = 4
EXPERTS_PER_GROUP = 8
N_EXPERTS = N_EXPERT_GROUPS * EXPERTS_PER_GROUP
EPS = 1e-6
NEG = -1e30
BIG = 1e30
SCALE = HEAD_DIM ** -0.5
LOG2E = 1.4426950408889634
Q_SCALE = SCALE * LOG2E

LANES = 128
SUBLANES = 8
VMEM_LIMIT_BYTES = 56 * 1024 * 1024

TM_PROJ = 512
TQ = 256
TK = 256
SEL_MASK_BIAS = 2.0 ** 40
V_ONES_ROWS = 16
TM_MOE = 256
TS_SCATTER = 2048
TM_COMBINE = 512
WEIGHT_DMA_PRIORITIES = (0, 1, 1)
W_STAGE_ROWS = 256
W_SLOTS = 3
DMA_ISSUE_GROUP = 64
T_RANK = 512


def _cparams(n_axes):
    return pltpu.CompilerParams(dimension_semantics=("arbitrary",) * n_axes, vmem_limit_bytes=VMEM_LIMIT_BYTES)


def _dot(a, b):
    return jnp.dot(a, b, preferred_element_type=F32)


def _dot_nt(a, b):
    return lax.dot_general(a, b, (((1,), (1,)), ((), ())), preferred_element_type=F32)


U32 = jnp.uint32
_HI16 = 0xFFFF0000


def _pack_bf16_halves(x):
    c = x.shape[1] // 2
    bits = lambda t: lax.bitcast_convert_type(t.astype(BF16).astype(F32), U32)
    return lax.shift_right_logical(bits(x[:, :c]), U32(16)) | (bits(x[:, c:]) & U32(_HI16))


def _store_token_tiles(ref, words):
    m, width = words.shape
    assert width == SUBLANES * LANES
    for j in range(SUBLANES):
        ref[pl.ds(j, m, stride=SUBLANES), :] = words[:, j * LANES:(j + 1) * LANES]


def _load_token_tiles(ref):
    m = ref.shape[0] // SUBLANES
    return jnp.concatenate([ref[pl.ds(j, m, stride=SUBLANES), :] for j in range(SUBLANES)], axis=1)


def _unpack_bf16_halves(w):
    lo = lax.bitcast_convert_type(lax.shift_left(w, U32(16)), F32)
    hi = lax.bitcast_convert_type(w & U32(_HI16), F32)
    return lo, hi


def _rope(t, cos, sin, axis):
    idx = lax.broadcasted_iota(I32, t.shape, axis)
    partner = jnp.where(idx < ROT_DIM // 2,
                        pltpu.roll(t, HEAD_DIM - ROT_DIM // 2, axis),
                        pltpu.roll(t, ROT_DIM // 2, axis))
    return t * cos + partner * sin


def _inproj_kernel(x_ref, ln_ref, ang_ref, wt_hbm,
                   qnt_ref, qrt_ref, kvc_ref, kk_ref, vt_ref, gt_ref, u_ref,
                   kc_scr, stage, wq_ref, wv_ref, wg_ref, wkc_ref, wk_ref, wu_ref, wsem):
    @pl.when(pl.program_id(0) == 0)
    def _():
        rc = stage.shape[1]
        q_w, kv_w = N_HEADS * HEAD_DIM, N_KV * HEAD_DIM
        o_kc, o_ks = q_w, q_w + 2 * kv_w
        o_vs, o_kw, o_vw, o_g = o_ks + kv_w, o_ks + 2 * kv_w, o_ks + 3 * kv_w, o_ks + 4 * kv_w
        o_u = o_g + N_GATES
        pieces = ([(r0, rc, wq_ref, r0) for r0 in range(0, q_w, rc)]
                  + [(o_kc + r0, rc, wkc_ref, r0) for r0 in range(0, 2 * kv_w, rc)]
                  + [(o_ks, kv_w, wk_ref, 0), (o_kw, kv_w, wk_ref, kv_w)]
                  + [(o_vs, kv_w, wv_ref, 0), (o_vw, kv_w, wv_ref, kv_w)]
                  + [(o_g, N_GATES, wg_ref, 0)]
                  + [(o_u + r0, rc, wu_ref, r0) for r0 in range(0, wu_ref.shape[0], rc)])
        assert all(n <= rc and src % SUBLANES == 0 for src, n, _, _ in pieces)
        wg_ref[...] = jnp.zeros(wg_ref.shape, BF16)

        def piece_copy(i, slot):
            src, rows, _, _ = pieces[i]
            return pltpu.make_async_copy(wt_hbm.at[0, pl.ds(src, rows)], stage.at[slot, pl.ds(0, rows)],
                                         wsem.at[slot])

        piece_copy(0, 0).start()
        for i, (src, rows, dst, dst_row) in enumerate(pieces):
            slot = i % 2
            if i + 1 < len(pieces):
                piece_copy(i + 1, 1 - slot).start()
            piece_copy(i, slot).wait()
            dst[dst_row:dst_row + rows, :] = stage[slot, :rows, :].astype(BF16)

    x = x_ref[...]
    ms = jnp.mean(x * x, axis=-1, keepdims=True)
    hb = (x * lax.rsqrt(ms + EPS) * ln_ref[...]).astype(BF16)
    chunk = 4 * HEAD_DIM

    ang = ang_ref[...]
    ca, sa = jnp.cos(ang), jnp.sin(ang)
    rest = (HEAD_DIM - ROT_DIM, ang.shape[1])
    cost = jnp.concatenate([ca, ca, jnp.ones(rest, F32)], axis=0)
    sint = jnp.concatenate([-sa, sa, jnp.zeros(rest, F32)], axis=0)
    cos, sin = cost.T, sint.T
    for ch in range(2):
        rt = _dot_nt(wq_ref[ch * chunk:(ch + 1) * chunk, :], hb)
        for j in range(4):
            t = rt[j * HEAD_DIM:(j + 1) * HEAD_DIM, :]
            qnt_ref[0, ch * 4 + j] = (t * Q_SCALE).astype(BF16)
            qrt_ref[0, ch * 4 + j] = (_rope(t, cost, sint, 0) * Q_SCALE).astype(BF16)
    rt = _dot_nt(wv_ref[...], hb)
    ones = jnp.ones((V_ONES_ROWS, TK), BF16)
    for j in range(4):
        for s in range(vt_ref.shape[2]):
            vt_ref[0, j, s] = jnp.concatenate(
                [rt[j * HEAD_DIM:(j + 1) * HEAD_DIM, s * TK:(s + 1) * TK].astype(BF16), ones], axis=0)
    gt_ref[...] = _dot_nt(wg_ref[...], hb)

    r = _dot_nt(hb, wkc_ref[...])
    groups = kvc_ref.shape[2]
    for j in range(4):
        kc_scr[j] = r[:, j * HEAD_DIM:(j + 1) * HEAD_DIM]
        for i in range(CMP_STRIDE):
            kvc_ref[0, j, :, i * HEAD_DIM:(i + 1) * HEAD_DIM] = (
                kc_scr[j, pl.ds(i, groups, stride=CMP_STRIDE), :].astype(BF16))
    r = _dot_nt(hb, wk_ref[...])
    for j in range(4):
        kk_ref[0, j] = _rope(r[:, j * HEAD_DIM:(j + 1) * HEAD_DIM], cos, sin, 1).astype(BF16)
    for ch in range(u_ref.shape[1] // chunk):
        u_ref[:, ch * chunk:(ch + 1) * chunk] = _dot_nt(hb, wu_ref[ch * chunk:(ch + 1) * chunk, :])


def _inproj(x2d, ln, ang_t, w_in_t, gate_rows, batch, seq):
    n, d = x2d.shape
    tm = TM_PROJ
    assert tm % TK == 0
    spb = seq // tm
    const = lambda i: (0, 0)
    q_w, kv_w = N_HEADS * HEAD_DIM, N_KV * HEAD_DIM
    pool_w = w_in_t.shape[1] - (q_w + 6 * kv_w + N_GATES)
    row_heads = lambda i: (i // spb, 0, i % spb, 0)
    col_heads = lambda i: (i // spb, 0, 0, i % spb)
    return pl.pallas_call(
        _inproj_kernel,
        grid=(n // tm,),
        in_specs=[
            pl.BlockSpec((tm, d), lambda i: (i, 0)),
            pl.BlockSpec((1, d), const),
            pl.BlockSpec((ang_t.shape[0], tm), lambda i: (0, i)),
            pl.BlockSpec(memory_space=pl.ANY),
        ],
        out_specs=[
            pl.BlockSpec((1, N_HEADS, HEAD_DIM, tm), col_heads),
            pl.BlockSpec((1, N_HEADS, HEAD_DIM, tm), col_heads),
            pl.BlockSpec((1, 4, tm // CMP_STRIDE, CMP_STRIDE * HEAD_DIM), row_heads),
            pl.BlockSpec((1, 4, tm, HEAD_DIM), row_heads),
            pl.BlockSpec((1, 4, tm // TK, HEAD_DIM + V_ONES_ROWS, TK), lambda i: (i // spb, 0, i % spb, 0, 0)),
            pl.BlockSpec((gate_rows, tm), lambda i: (0, i)),
            pl.BlockSpec((tm, pool_w), lambda i: (i, 0)),
        ],
        out_shape=[
            jax.ShapeDtypeStruct((batch, N_HEADS, HEAD_DIM, seq), BF16),
            jax.ShapeDtypeStruct((batch, N_HEADS, HEAD_DIM, seq), BF16),
            jax.ShapeDtypeStruct((batch, 4, seq // CMP_STRIDE, CMP_STRIDE * HEAD_DIM), BF16),
            jax.ShapeDtypeStruct((batch, 4, seq, HEAD_DIM), BF16),
            jax.ShapeDtypeStruct((batch, 4, seq // TK, HEAD_DIM + V_ONES_ROWS, TK), BF16),
            jax.ShapeDtypeStruct((gate_rows, n), F32),
            jax.ShapeDtypeStruct((n, pool_w), F32),
        ],
        scratch_shapes=[
            pltpu.VMEM((4, tm, HEAD_DIM), F32),
            pltpu.VMEM((2, W_STAGE_ROWS, d), F32),
            pltpu.VMEM((q_w, d), BF16), pltpu.VMEM((2 * kv_w, d), BF16), pltpu.VMEM((gate_rows, d), BF16),
            pltpu.VMEM((2 * kv_w, d), BF16), pltpu.VMEM((2 * kv_w, d), BF16), pltpu.VMEM((pool_w, d), BF16),
            pltpu.SemaphoreType.DMA((2,)),
        ],
        compiler_params=_cparams(1),
        name="inproj",
    )(x2d, ln, ang_t, w_in_t)


def _compress_kernel(t_ref, pe_ref, wk1_ref, wv1_ref, w2_ref, o_ref, w1_scr):
    half = w1_scr.shape[1]

    @pl.when(pl.program_id(0) == 0)
    def _():
        for kind, w_ref in enumerate((wk1_ref, wv1_ref)):
            for h in range(2):
                w1_scr[kind, :, h * CMP_HIDDEN:(h + 1) * CMP_HIDDEN] = w_ref[pl.ds(h * half, half), :].astype(BF16)

    for j in range(t_ref.shape[1]):
        kind = j // N_KV
        t2 = t_ref[0, j]
        w1 = w1_scr[kind]
        a = _dot(t2, w1)
        pb = _dot(pe_ref[kind], w1)
        bias = pb[0:1, :CMP_HIDDEN] + pb[1:2, CMP_HIDDEN:]
        nrow = a.shape[0]
        nxt = pltpu.roll(a[:, CMP_HIDDEN:], nrow - 1, 0)
        hid = jax.nn.gelu(a[:, :CMP_HIDDEN] + nxt + bias)
        out = _dot(hid.astype(BF16), w2_ref[kind])
        row = lax.broadcasted_iota(I32, out.shape, 0)
        out = jnp.where(row < nrow - 1, out, 0.0)
        o_ref[0, j] = (out.T if kind else out).astype(BF16)


def _compress(kvc2, pe2, wk1, wv1, w2):
    batch, four, nrow, width = kvc2.shape
    assert wk1.shape == wv1.shape == (2 * width, CMP_HIDDEN)
    whole = lambda a: pl.BlockSpec(a.shape, lambda b: (0,) * a.ndim)
    return pl.pallas_call(
        _compress_kernel,
        grid=(batch,),
        in_specs=[pl.BlockSpec((1, four, nrow, width), lambda b: (b, 0, 0, 0)), whole(pe2), whole(wk1), whole(wv1),
                  whole(w2)],
        out_specs=pl.BlockSpec((1, four, nrow, HEAD_DIM), lambda b: (b, 0, 0, 0)),
        out_shape=jax.ShapeDtypeStruct((batch, four, nrow, HEAD_DIM), BF16),
        scratch_shapes=[pltpu.VMEM((2, width, 2 * CMP_HIDDEN), BF16)],
        compiler_params=_cparams(1),
        name="compress",
    )(kvc2, pe2, wk1, wv1, w2)


def _nsa_kernel(qnt_ref, qrt_ref, cmp_ref, kk_ref, vt_ref, kblk_ref, gt_ref, gn_ref, o_ref,
                m_scr, acc_scr, qa_scr, ocmp_scr, s_scr, o_scr):
    qi = pl.program_id(1)
    tq = qnt_ref.shape[3]
    rows = GQA * tq
    n_cmp_pad = cmp_ref.shape[2]
    n_sel = kk_ref.shape[2] // SEL_LEN
    q0 = qi * tq

    m_scr[...] = jnp.full(m_scr.shape, NEG, F32)
    acc_scr[...] = jnp.zeros(acc_scr.shape, F32)

    def scores(item, slot):
        c, g, q_rows, k_tile, v_t, bias = item
        s_scr[slot] = _dot(k_tile, qa_scr[g, :q_rows, :])

    def softmax_pv(item, slot):
        c, g, q_rows, k_tile, v_t, bias = item
        s = s_scr[slot]
        if bias is not None:
            s = s + bias
        m_old = m_scr[c]
        m_new = jnp.maximum(m_old, jnp.max(s, axis=0, keepdims=True))
        alpha = jnp.exp2(m_old - m_new)
        p = jnp.exp2(s - m_new)
        acc_scr[c] = alpha * acc_scr[c] + _dot(v_t, p.astype(BF16))
        m_scr[c] = m_new

    def flash_out(c):
        return acc_scr[c, :HEAD_DIM, :] / acc_scr[c, HEAD_DIM:HEAD_DIM + 1, :]

    def tile_heads(a):
        return jnp.concatenate([a] * GQA, axis=1)

    kk_i = lax.broadcasted_iota(I32, (TK, tq), 0)
    qq_i = lax.broadcasted_iota(I32, (TK, tq), 1)
    causal_bias = tile_heads(jnp.where(kk_i <= qq_i, 0.0, NEG))
    band_bias = tile_heads(jnp.where(kk_i > qq_i, 0.0, NEG))
    n_win = WINDOW // TK

    for g in range(N_KV):
        qn4 = jnp.concatenate([qnt_ref[0, h] for h in range(g * GQA, (g + 1) * GQA)], axis=1)
        s = _dot(cmp_ref[0, g], qn4)
        n_idx = lax.broadcasted_iota(I32, (n_cmp_pad, rows), 0)
        t_idx = q0 + jnp.bitwise_and(lax.broadcasted_iota(I32, (n_cmp_pad, rows), 1), tq - 1)
        cmp_ok = n_idx * CMP_STRIDE + (CMP_LEN - 1) <= t_idx
        s = jnp.where(cmp_ok, s, NEG)
        m = jnp.max(s, axis=0, keepdims=True)
        e = jnp.where(cmp_ok, jnp.exp2(s - m), 0.0)
        l = jnp.sum(e, axis=0, keepdims=True)
        p = e * jnp.where(l > 0.0, 1.0 / l, 0.0)
        pb = p.astype(BF16)
        ocmp_scr[g] = _dot(cmp_ref[0, N_KV + g], pb)

        jn = lax.broadcasted_iota(I32, (n_sel, n_cmp_pad), 0) * SEL_LEN
        cn = lax.broadcasted_iota(I32, (n_sel, n_cmp_pad), 1) * CMP_STRIDE
        ov = jnp.clip(jnp.minimum(cn + CMP_LEN, jn + SEL_LEN) - jnp.maximum(cn, jn), 0, None)
        w_t = (ov.astype(F32) * (1.0 / CMP_LEN)).astype(BF16)
        imp4 = _dot(w_t, pb)
        imp = imp4[:, 0:tq]
        for r in range(1, GQA):
            imp = imp + imp4[:, r * tq:(r + 1) * tq]
        j_blk = lax.broadcasted_iota(I32, (n_sel, tq), 0)
        t_q = q0 + lax.broadcasted_iota(I32, (n_sel, tq), 1)
        sel_ok = j_blk * SEL_LEN <= t_q
        back = t_q // SEL_LEN - j_blk
        forced = (j_blk == 0) | ((back >= 0) & (back < N_LOCAL))
        val = jnp.where(sel_ok & forced, BIG, jnp.where(sel_ok, imp, -BIG))
        rank = jnp.zeros((n_sel, tq), F32)
        for i in range(n_sel):
            vi = val[i:i + 1, :]
            beats = (vi > val) | ((vi == val) & (j_blk > i))
            rank = rank + beats.astype(F32)
        keep = (rank < float(min(SEL_TOPK, n_sel))) & sel_ok
        bias_t = jnp.where(keep, 0.0, -SEL_MASK_BIAS)
        bias_t = jnp.concatenate([bias_t, jnp.zeros((HEAD_DIM - n_sel, tq), F32)], axis=0)
        qa_scr[g, :HEAD_DIM, :] = jnp.concatenate(
            [qrt_ref[0, h] for h in range(g * GQA, (g + 1) * GQA)], axis=1)
        qa_scr[g, HEAD_DIM:, :] = tile_heads(bias_t.astype(BF16))

    def sel_item(g, kj, bias):
        st = pl.multiple_of(kj * TK, TK)
        k_aug = jnp.concatenate([kk_ref[0, g, pl.ds(st, TK), :], kblk_ref[pl.ds(st, TK), :]], axis=1)
        return (g, g, 2 * HEAD_DIM, k_aug, vt_ref[0, g, kj], bias)

    def win_item(g, back):
        kj = qi - back
        st = pl.multiple_of(kj * TK, TK)
        bias = causal_bias if back == 0 else (band_bias if back == n_win else None)
        return (N_KV + g, g, HEAD_DIM, kk_ref[0, N_KV + g, pl.ds(st, TK), :], vt_ref[0, N_KV + g, kj], bias)

    assert N_KV == 2
    scores(sel_item(0, 0, None), 0)

    def sel_tile(kj):
        scores(sel_item(1, kj, None), 1)
        softmax_pv(sel_item(0, kj, None), 0)
        scores(sel_item(0, kj + 1, None), 0)
        softmax_pv(sel_item(1, kj, None), 1)

    def sel_pair(j, carry):
        sel_tile(2 * j)
        sel_tile(2 * j + 1)
        return carry

    lax.fori_loop(0, qi // 2, sel_pair, 0)

    @pl.when(qi % 2 == 1)
    def _():
        sel_tile(qi - 1)

    def run_tail(farthest):
        tail = ([sel_item(g, qi, causal_bias) for g in range(N_KV)]
                + [win_item(g, back) for back in range(farthest, -1, -1) for g in range(N_KV)])
        for i, item in enumerate(tail):
            if i + 1 < len(tail):
                scores(tail[i + 1], (i + 1) % 2)
            softmax_pv(item, i % 2)

    for nb in range(n_win):
        @pl.when(qi == nb)
        def _(nb=nb):
            run_tail(nb)

    @pl.when(qi >= n_win)
    def _():
        run_tail(n_win)

    gsig = jax.nn.sigmoid(gt_ref[...])
    for g in range(N_KV):
        o_cmp, o_sel, o_win = ocmp_scr[g], flash_out(g), flash_out(N_KV + g)
        for r in range(GQA):
            h = g * GQA + r
            sl = slice(r * tq, (r + 1) * tq)
            o_h = (gsig[3 * h:3 * h + 1, :] * o_cmp[:, sl]
                   + gsig[3 * h + 1:3 * h + 2, :] * o_sel[:, sl]
                   + gsig[3 * h + 2:3 * h + 3, :] * o_win[:, sl])
            o_scr[:, h * HEAD_DIM:(h + 1) * HEAD_DIM] = o_h.T

    o = o_scr[...]
    ms = jnp.mean(o * o, axis=-1, keepdims=True)
    o_ref[...] = (o * lax.rsqrt(ms + EPS) * gn_ref[...]).astype(BF16)


def _nsa(qnt, qrt, cmp, kk, vt, kblk, gt, gn):
    batch, _, _, seq = qnt.shape
    assert TQ == TK and WINDOW % TK == 0 and TQ & (TQ - 1) == 0
    nq = seq // TQ
    width = N_HEADS * HEAD_DIM
    return pl.pallas_call(
        _nsa_kernel,
        grid=(batch, nq),
        in_specs=[
            pl.BlockSpec((1, N_HEADS, HEAD_DIM, TQ), lambda b, i: (b, 0, 0, i)),
            pl.BlockSpec((1, N_HEADS, HEAD_DIM, TQ), lambda b, i: (b, 0, 0, i)),
            pl.BlockSpec((1,) + cmp.shape[1:], lambda b, i: (b, 0, 0, 0)),
            pl.BlockSpec((1,) + kk.shape[1:], lambda b, i: (b, 0, 0, 0)),
            pl.BlockSpec((1,) + vt.shape[1:], lambda b, i: (b, 0, 0, 0, 0)),
            pl.BlockSpec((seq, LANES), lambda b, i: (0, 0)),
            pl.BlockSpec((gt.shape[0], TQ), lambda b, i: (0, b * nq + i)),
            pl.BlockSpec((1, width), lambda b, i: (0, 0)),
        ],
        out_specs=pl.BlockSpec((TQ, width), lambda b, i: (b * nq + i, 0)),
        out_shape=jax.ShapeDtypeStruct((batch * seq, width), BF16),
        scratch_shapes=[
            pltpu.VMEM((2 * N_KV, 1, GQA * TQ), F32),
            pltpu.VMEM((2 * N_KV, vt.shape[3], GQA * TQ), F32),
            pltpu.VMEM((N_KV, 2 * HEAD_DIM, GQA * TQ), BF16),
            pltpu.VMEM((N_KV, HEAD_DIM, GQA * TQ), F32),
            pltpu.VMEM((2, TK, GQA * TQ), F32),
            pltpu.VMEM((TQ, width), F32),
        ],
        compiler_params=_cparams(2),
        name="nsa",
    )(qnt, qrt, cmp, kk, vt, kblk, gt, gn)


def _pool_group(cur, halo_rows, t1, window, w_mat, bias, scale):
    halo = halo_rows.shape[0]
    acc = jnp.concatenate([halo_rows, cur], axis=0)
    span = 1
    while span < window:
        acc = acc + jnp.concatenate([jnp.zeros((span, acc.shape[1]), F32), acc[:-span]], axis=0)
        span *= 2
    mean = acc[halo:] / jnp.minimum(t1, float(window))
    return (_dot((mean - cur).astype(BF16), w_mat) + bias) * scale


def _pool_kernel(u_ref, w_ref, b_ref, sc_ref, gn_ref, o_ref, y_scr, halo_scr):
    si = pl.program_id(1)
    tm = u_ref.shape[1]
    halo = halo_scr.shape[0]

    @pl.when(si == 0)
    def _():
        halo_scr[...] = jnp.zeros(halo_scr.shape, F32)

    t1 = (si * tm + 1 + lax.broadcasted_iota(I32, (tm, 1), 0)).astype(F32)
    cg = w_ref.shape[1]
    for gi, window in enumerate(POOL_SIZES):
        sl = slice(gi * cg, (gi + 1) * cg)
        cur = u_ref[0, :, sl]
        y_scr[:, sl] = _pool_group(cur, halo_scr[:, sl], t1, window, w_ref[gi], b_ref[:, sl], sc_ref[:, sl])
        halo_scr[:, sl] = cur[tm - halo:, :]
    y = y_scr[...]
    ms = jnp.mean(y * y, axis=-1, keepdims=True)
    o_ref[...] = (y * lax.rsqrt(ms + EPS) * gn_ref[...]).astype(BF16)


def _pool(u3, w_pool, b_pool, pool_scale, gn_pool):
    batch, seq, c = u3.shape
    tm = TM_PROJ
    ns = seq // tm
    vec = lambda b, i: (0, 0)
    return pl.pallas_call(
        _pool_kernel,
        grid=(batch, ns),
        in_specs=[
            pl.BlockSpec((1, tm, c), lambda b, i: (b, i, 0)),
            pl.BlockSpec(w_pool.shape, lambda b, i: (0, 0, 0)),
            pl.BlockSpec((1, c), vec),
            pl.BlockSpec((1, c), vec),
            pl.BlockSpec((1, c), vec),
        ],
        out_specs=pl.BlockSpec((tm, c), lambda b, i: (b * ns + i, 0)),
        out_shape=jax.ShapeDtypeStruct((batch * seq, c), BF16),
        scratch_shapes=[pltpu.VMEM((tm, c), F32), pltpu.VMEM((max(POOL_SIZES), c), F32)],
        compiler_params=_cparams(2),
        name="pool",
    )(u3, w_pool, b_pool, pool_scale, gn_pool)


def _outproj_kernel(mn_ref, mp_ref, x_ref, w_hbm, ln_ref, wr_ref, br_ref,
                    x1_ref, h2_ref, eid_ref, gate_ref, x1_scr, ssq_scr, stage, w_ref, wsem):
    tm, d = x_ref.shape
    half = mn_ref.shape[1]
    chunk = 512

    @pl.when(pl.program_id(0) == 0)
    def _():
        x1_scr[...] = jnp.zeros(x1_scr.shape, F32)
        ssq_scr[...] = jnp.zeros(ssq_scr.shape, F32)
        rc = stage.shape[1]
        n_chunks = w_ref.shape[0] // rc

        def chunk_copy(c, slot):
            return pltpu.make_async_copy(w_hbm.at[0, pl.ds(c * rc, rc)], stage.at[slot], wsem.at[slot])

        chunk_copy(0, 0).start()
        for c in range(n_chunks):
            slot = c % 2
            if c + 1 < n_chunks:
                chunk_copy(c + 1, 1 - slot).start()
            chunk_copy(c, slot).wait()
            w_ref[c * rc:(c + 1) * rc, :] = stage[slot].astype(BF16)

    h2 = x1_scr[...] * lax.rsqrt(ssq_scr[...] * (1.0 / d) + EPS) * ln_ref[...]
    _store_token_tiles(h2_ref, _pack_bf16_halves(h2))

    logits = _dot_nt(wr_ref[...], h2.astype(BF16)) + br_ref[:, 0:1]
    ng, ne = N_EXPERT_GROUPS, EXPERTS_PER_GROUP
    row = lax.broadcasted_iota(I32, (ne, tm), 0)
    lg = jnp.where(row < ng, logits[0:ne], NEG)
    mg = jnp.max(lg, axis=0, keepdims=True)
    g_sel = jnp.min(jnp.where(lg == mg, row, ne), axis=0, keepdims=True)
    p_g = 1.0 / jnp.sum(jnp.exp(lg - mg), axis=0, keepdims=True)
    le = jnp.zeros((ne, tm), F32)
    for g in range(ng):
        le = jnp.where(g_sel == g, logits[ne * (g + 1):ne * (g + 2)], le)
    ex = jnp.exp(le - jnp.max(le, axis=0, keepdims=True))
    pe = ex / jnp.sum(ex, axis=0, keepdims=True)
    p1 = jnp.max(pe, axis=0, keepdims=True)
    i1 = jnp.min(jnp.where(pe == p1, row, ne), axis=0, keepdims=True)
    rest = jnp.where(row == i1, -1.0, pe)
    p2 = jnp.max(rest, axis=0, keepdims=True)
    i2 = jnp.min(jnp.where(rest == p2, row, ne), axis=0, keepdims=True)
    psum = p1 + p2
    zero_i = jnp.zeros((ne, tm), I32)
    eid = jnp.where(row == 0, g_sel * ne + i1, jnp.where(row == 1, g_sel * ne + i2, zero_i))
    gate = jnp.where(row == 0, p_g * p1 / psum, jnp.where(row == 1, p_g * p2 / psum, 0.0))
    eid_ref[...] = eid
    gate_ref[...] = jnp.concatenate([gate, jnp.zeros((LANES - ne, tm), F32)], axis=0).T

    mn = mn_ref[...]
    mp = mp_ref[...]
    ssq = jnp.zeros((tm, 1), F32)
    for c in range(d // chunk):
        sl = slice(c * chunk, (c + 1) * chunk)
        x1 = x_ref[:, sl] + (_dot(mn, w_ref[:half, sl]) + _dot(mp, w_ref[half:, sl]))
        x1_ref[:, sl] = x1
        x1_scr[:, sl] = x1
        ssq = ssq + jnp.sum(x1 * x1, axis=-1, keepdims=True)
    ssq_scr[...] = ssq


def _outproj(mix_nsa, mix_pool, x2d, w_out, ln_moe, w_router_t, b_router):
    n, d = x2d.shape
    tm = TM_PROJ
    half = mix_nsa.shape[1]
    const = lambda i: (0, 0)
    last = n // tm - 1
    cur = lambda i: (jnp.minimum(i, last), 0)
    prev = lambda i: (jnp.maximum(i - 1, 0), 0)
    prev_t = lambda i: (0, jnp.maximum(i - 1, 0))
    return pl.pallas_call(
        _outproj_kernel,
        grid=(n // tm + 1,),
        in_specs=[
            pl.BlockSpec((tm, half), cur),
            pl.BlockSpec((tm, half), cur),
            pl.BlockSpec((tm, d), cur),
            pl.BlockSpec(memory_space=pl.ANY),
            pl.BlockSpec((1, d), const),
            pl.BlockSpec(w_router_t.shape, const),
            pl.BlockSpec(b_router.shape, const),
        ],
        out_specs=[
            pl.BlockSpec((tm, d), cur),
            pl.BlockSpec((tm * SUBLANES, LANES), prev),
            pl.BlockSpec((SUBLANES, tm), prev_t),
            pl.BlockSpec((tm, LANES), prev),
        ],
        scratch_shapes=[pltpu.VMEM((tm, d), F32), pltpu.VMEM((tm, 1), F32),
                        pltpu.VMEM((2, W_STAGE_ROWS, d), F32), pltpu.VMEM(w_out.shape[1:], BF16),
                        pltpu.SemaphoreType.DMA((2,))],
        out_shape=[
            jax.ShapeDtypeStruct((n, d), F32),
            jax.ShapeDtypeStruct((n * SUBLANES, LANES), U32),
            jax.ShapeDtypeStruct((SUBLANES, n), I32),
            jax.ShapeDtypeStruct((n, LANES), F32),
        ],
        compiler_params=_cparams(1),
        name="outproj",
    )(mix_nsa, mix_pool, x2d, w_out, ln_moe, w_router_t, b_router)


def _dispatch_kernel(eid_ref, dest_ref, meta_ref, rank_scr):
    n = eid_ref.shape[1]
    t = T_RANK
    row = lax.broadcasted_iota(I32, (N_EXPERTS, t), 0)
    before = (lax.broadcasted_iota(I32, (t, t), 0) < lax.broadcasted_iota(I32, (t, t), 1)).astype(BF16)
    carry = jnp.zeros((N_EXPERTS, 1), F32)
    for k in range(2):
        for j in range(n // t):
            sl = slice(j * t, (j + 1) * t)
            oh = row == eid_ref[k:k + 1, sl]
            ohf = oh.astype(F32)
            prior = _dot(ohf.astype(BF16), before) + carry
            rank_scr[k:k + 1, sl] = jnp.sum(jnp.where(oh, prior, 0.0), axis=0, keepdims=True)
            carry = carry + jnp.sum(ohf, axis=1, keepdims=True)
    counts = jnp.broadcast_to(carry, (N_EXPERTS, LANES))
    padded = jnp.floor((counts + (TM_MOE - 1.0)) * (1.0 / TM_MOE)) * TM_MOE
    ends = padded
    r_idx = lax.broadcasted_iota(I32, (N_EXPERTS, LANES), 0)
    sh = 1
    while sh < N_EXPERTS:
        ends = ends + jnp.where(r_idx >= sh, pltpu.roll(ends, sh, 0), 0.0)
        sh *= 2
    starts = (ends - padded)[:, 0:1]
    for k in range(2):
        for j in range(n // t):
            sl = slice(j * t, (j + 1) * t)
            oh = row == eid_ref[k:k + 1, sl]
            base = jnp.sum(jnp.where(oh, starts, 0.0), axis=0, keepdims=True)
            dest_ref[k:k + 1, sl] = (rank_scr[k:k + 1, sl] + base).astype(I32)
    for k in range(2, SUBLANES):
        dest_ref[k:k + 1, :] = jnp.zeros((1, n), I32)
    blk_start = lax.broadcasted_iota(I32, (N_EXPERTS, LANES), 1).astype(F32) * TM_MOE
    blk_expert = jnp.sum((ends <= blk_start).astype(F32), axis=0, keepdims=True)
    blk_expert = jnp.minimum(blk_expert, N_EXPERTS - 1.0)
    n_used = ends[N_EXPERTS - 1:N_EXPERTS, :] * (1.0 / TM_MOE)
    on_diag = r_idx == lax.broadcasted_iota(I32, (N_EXPERTS, LANES), 1)
    end_blk = jnp.sum(jnp.where(on_diag, ends, 0.0), axis=0, keepdims=True) * (1.0 / TM_MOE)
    n_blk = jnp.sum(jnp.where(on_diag, padded, 0.0), axis=0, keepdims=True) * (1.0 / TM_MOE)
    r8 = lax.broadcasted_iota(I32, (SUBLANES, LANES), 0)
    meta = jnp.where(r8 == 0, blk_expert,
                     jnp.where(r8 == 1, n_used, jnp.where(r8 == 2, end_blk, jnp.where(r8 == 3, n_blk, 0.0))))
    meta_ref[...] = meta.astype(I32)


def _dispatch(eid):
    n = eid.shape[1]
    return pl.pallas_call(
        _dispatch_kernel,
        out_shape=[jax.ShapeDtypeStruct((SUBLANES, n), I32),
                   jax.ShapeDtypeStruct((SUBLANES, LANES), I32)],
        scratch_shapes=[pltpu.VMEM((SUBLANES, n), F32)],
        compiler_params=pltpu.CompilerParams(vmem_limit_bytes=VMEM_LIMIT_BYTES),
        name="dispatch",
    )(eid)


def _row_copy(src_ref, src_row, dst_ref, dst_row, sem):
    tile = lambda ref, row: ref.at[pl.ds(pl.multiple_of(row * SUBLANES, SUBLANES), SUBLANES)]
    return pltpu.make_async_copy(tile(src_ref, src_row), tile(dst_ref, dst_row), sem)


def _scatter_kernel(dest_ref, meta_ref, h_ref, xs_ref, zero_buf, sem, zsem):
    tm = h_ref.shape[0] // SUBLANES
    blk = zero_buf.shape[0]

    @pl.when(pl.program_id(0) == 0)
    def _():
        zero_buf[...] = jnp.zeros(zero_buf.shape, zero_buf.dtype)
        n_used = meta_ref[1, 0]

        def zero_block(b):
            return pltpu.make_async_copy(zero_buf, xs_ref.at[pl.ds(pl.multiple_of(b * blk, blk), blk)], zsem)

        def sweep(act):
            def per_expert(e, c):
                @pl.when(meta_ref[3, e] > 0)
                def _():
                    act(zero_block(meta_ref[2, e] - 1))
                return c

            def per_tail(b, c):
                @pl.when(b >= n_used)
                def _():
                    act(zero_block(b))
                return c

            lax.fori_loop(0, N_EXPERTS, per_expert, 0)
            lax.fori_loop(0, xs_ref.shape[0] // blk, per_tail, 0)

        sweep(lambda cp: cp.start())
        sweep(lambda cp: cp.wait())

    grp = dest_ref.shape[2]
    n_grp = tm // grp

    def start(j, c):
        for u in range(grp):
            for k in range(2):
                _row_copy(h_ref, j * grp + u, xs_ref, dest_ref[0, k * n_grp + j, u], sem).start(priority=k)
        return c

    lax.fori_loop(0, n_grp, start, 0)
    for k in range(2):
        pltpu.make_async_copy(h_ref, xs_ref.at[pl.ds(0, tm * SUBLANES)], sem).wait()


def _grouped_dest(dest, tm, grp):
    n = dest.shape[1]
    return jnp.transpose(dest[:2].reshape(2, n // tm, tm // grp, grp), (1, 0, 2, 3)).reshape(n // tm, 2 * tm // grp, grp)


def _scatter(dest3, meta, h2, cap):
    nsteps, rows, grp = dest3.shape
    tm = rows * grp // 2
    return pl.pallas_call(
        _scatter_kernel,
        grid=(nsteps,),
        in_specs=[
            pl.BlockSpec((1, rows, grp), lambda i: (i, 0, 0), memory_space=pltpu.SMEM),
            pl.BlockSpec(memory_space=pltpu.SMEM),
            pl.BlockSpec((tm * SUBLANES, LANES), lambda i: (i, 0)),
        ],
        out_specs=pl.BlockSpec(memory_space=pl.ANY),
        out_shape=jax.ShapeDtypeStruct((cap * SUBLANES, LANES), h2.dtype),
        scratch_shapes=[pltpu.VMEM((TM_MOE * SUBLANES, LANES), h2.dtype), pltpu.SemaphoreType.DMA(()),
                        pltpu.SemaphoreType.DMA(())],
        compiler_params=_cparams(1),
        name="scatter",
    )(dest3, meta, h2)


def _experts_kernel(meta_ref, xs_ref, wg_hbm, wu_hbm, wd_hbm, ys_ref,
                    wg_buf, wu_buf, wd_buf, wg16, wu16, wd16, elist_ref, ord_ref, sem):
    c = pl.program_id(0)
    n_used = meta_ref[1, 0]
    e = meta_ref[0, c]
    first = (c == 0) | (e != meta_ref[0, jnp.maximum(c - 1, 0)])

    def fetch(expert, slot):
        return [pltpu.make_async_copy(w.at[expert], buf.at[slot], sem.at[slot, i])
                for i, (w, buf) in enumerate(((wg_hbm, wg_buf), (wu_hbm, wu_buf), (wd_hbm, wd_buf)))]

    @pl.when(c == 0)
    def _():
        def scan(b, cnt):
            eb = meta_ref[0, b]
            take = ((b == 0) | (eb != meta_ref[0, jnp.maximum(b - 1, 0)])) & (b < n_used)

            @pl.when(take)
            def _():
                elist_ref[cnt] = eb

            return cnt + take.astype(I32)

        cnt = lax.fori_loop(0, pl.num_programs(0), scan, 0)
        ord_ref[0] = 0
        ord_ref[1] = cnt
        for k in range(W_SLOTS):
            @pl.when(k < cnt)
            def _(k=k):
                for cp, prio in zip(fetch(elist_ref[k], k), WEIGHT_DMA_PRIORITIES):
                    cp.start(priority=prio)

    @pl.when(first & (c < n_used))
    def _():
        @pl.when(c > 0)
        def _():
            ord_ref[0] = ord_ref[0] + 1
        for cp in fetch(e, ord_ref[0] % W_SLOTS):
            cp.wait()

    def mlp(weights):
        lo, hi = _unpack_bf16_halves(_load_token_tiles(xs_ref))
        lo, hi = lo.astype(BF16), hi.astype(BF16)
        half = lo.shape[1]
        wg, wu, wd = weights

        def proj(w):
            return _dot(lo, w[:half, :]) + _dot(hi, w[half:, :])

        act = (jax.nn.silu(proj(wg)) * proj(wu)).astype(BF16)
        _store_token_tiles(ys_ref, _pack_bf16_halves(_dot(act, wd)))

    @pl.when(first & (c < n_used))
    def _():
        k = ord_ref[0]
        slot = k % W_SLOTS
        cast = [buf[slot].astype(BF16) for buf in (wg_buf, wu_buf, wd_buf)]
        for ref16, w in zip((wg16, wu16, wd16), cast):
            ref16[...] = w
        mlp(cast)
        ahead = k + W_SLOTS

        @pl.when(ahead < ord_ref[1])
        def _():
            for cp, prio in zip(fetch(elist_ref[jnp.minimum(ahead, N_EXPERTS - 1)], slot), WEIGHT_DMA_PRIORITIES):
                cp.start(priority=prio)

    @pl.when(jnp.logical_not(first) & (c < n_used))
    def _():
        mlp((wg16[...], wu16[...], wd16[...]))

    @pl.when(c >= n_used)
    def _():
        ys_ref[...] = jnp.zeros(ys_ref.shape, U32)


def _experts(meta, xs, w_gate, w_up, w_down):
    rows = TM_MOE * SUBLANES
    nblk = xs.shape[0] // rows
    d, ff = w_gate.shape[1:]

    def row_map(c, meta):
        return (jnp.minimum(c, meta[1, 0] - 1), 0)

    hbm = pl.BlockSpec(memory_space=pl.ANY)
    return pl.pallas_call(
        _experts_kernel,
        grid_spec=pltpu.PrefetchScalarGridSpec(
            num_scalar_prefetch=1,
            grid=(nblk,),
            in_specs=[pl.BlockSpec((rows, LANES), row_map), hbm, hbm, hbm],
            out_specs=pl.BlockSpec((rows, LANES), lambda c, meta: (c, 0)),
            scratch_shapes=[
                pltpu.VMEM((W_SLOTS, d, ff), F32), pltpu.VMEM((W_SLOTS, d, ff), F32),
                pltpu.VMEM((W_SLOTS, ff, d), F32),
                pltpu.VMEM((d, ff), BF16), pltpu.VMEM((d, ff), BF16), pltpu.VMEM((ff, d), BF16),
                pltpu.SMEM((N_EXPERTS,), I32),
                pltpu.SMEM((2,), I32),
                pltpu.SemaphoreType.DMA((W_SLOTS, 3)),
            ],
        ),
        out_shape=jax.ShapeDtypeStruct(xs.shape, U32),
        compiler_params=_cparams(1),
        name="experts",
    )(meta, xs, w_gate, w_up, w_down)


def _combine_kernel(dest_ref, dest_next_ref, x1_ref, gate_ref, ln_ref, ys_ref, o_ref, ybuf, sem):
    i = pl.program_id(0)
    tm = x1_ref.shape[0]
    slot = i % 2
    grp = dest_ref.shape[2]
    n_grp = tm // grp

    def issue_group(d_ref, s, j):
        for u in range(grp):
            for k in range(2):
                _row_copy(ys_ref, d_ref[0, k * n_grp + j, u], ybuf.at[s, k], j * grp + u, sem.at[s]).start(priority=k)

    @pl.when(i == 0)
    def _():
        def start(j, c):
            issue_group(dest_ref, 0, j)
            return c

        lax.fori_loop(0, n_grp, start, 0)

    for k in range(2):
        pltpu.make_async_copy(ys_ref.at[pl.ds(0, tm * SUBLANES)], ybuf.at[slot, k], sem.at[slot]).wait()
    ln = ln_ref[...]

    def finish(j):
        r0 = pl.multiple_of(j * grp, grp)

        def expert_rows(k):
            words = jnp.concatenate(
                [ybuf[slot, k, pl.ds(r0 * SUBLANES + c, grp, stride=SUBLANES), :] for c in range(SUBLANES)], axis=1)
            return _unpack_bf16_halves(words)

        (lo0, hi0), (lo1, hi1) = expert_rows(0), expert_rows(1)
        gate = gate_ref[pl.ds(r0, grp), :]
        g0, g1 = gate[:, 0:1], gate[:, 1:2]
        moe = jnp.concatenate([lo0 * g0 + lo1 * g1, hi0 * g0 + hi1 * g1], axis=1)
        x2 = x1_ref[pl.ds(r0, grp), :] + moe
        ms = jnp.mean(x2 * x2, axis=-1, keepdims=True)
        o_ref[pl.ds(r0, grp), :] = x2 * lax.rsqrt(ms + EPS) * ln

    def finish_and_prefetch(j, carry):
        finish(j)
        issue_group(dest_next_ref, 1 - slot, j)
        return carry

    def finish_only(j, carry):
        finish(j)
        return carry

    has_next = i + 1 < pl.num_programs(0)

    @pl.when(has_next)
    def _():
        lax.fori_loop(0, tm // grp, finish_and_prefetch, 0)

    @pl.when(jnp.logical_not(has_next))
    def _():
        lax.fori_loop(0, tm // grp, finish_only, 0)


def _combine(dest3, x1, gate_rows, ln_final, ys):
    n, d = x1.shape
    tm = TM_COMBINE
    last = n // tm - 1
    return pl.pallas_call(
        _combine_kernel,
        grid=(n // tm,),
        in_specs=[
            pl.BlockSpec((1,) + dest3.shape[1:], lambda i: (i, 0, 0), memory_space=pltpu.SMEM),
            pl.BlockSpec((1,) + dest3.shape[1:], lambda i: (jnp.minimum(i + 1, last), 0, 0), memory_space=pltpu.SMEM),
            pl.BlockSpec((tm, d), lambda i: (i, 0)),
            pl.BlockSpec((tm, LANES), lambda i: (i, 0)),
            pl.BlockSpec((1, d), lambda i: (0, 0)),
            pl.BlockSpec(memory_space=pl.ANY),
        ],
        out_specs=pl.BlockSpec((tm, d), lambda i: (i, 0)),
        out_shape=jax.ShapeDtypeStruct((n, d), F32),
        scratch_shapes=[pltpu.VMEM((2, 2, tm * SUBLANES, LANES), U32), pltpu.SemaphoreType.DMA((2,))],
        compiler_params=_cparams(1),
        name="combine",
    )(dest3, dest3, x1, gate_rows, ln_final, ys)


def kernel(x, positions, ln_mix, w_in, pe_cmp_k, w_cmp_k1, w_cmp_k2, pe_cmp_v, w_cmp_v1, w_cmp_v2, w_pool, b_pool, pool_scale, gn_nsa, gn_pool, w_out, ln_moe, w_router_group, b_router_group, w_router_expert, b_router_expert, w_gate, w_up, w_down, ln_final):
    batch, seq, d = x.shape
    n = batch * seq
    depth = w_in.shape[0]

    inv_freq = ROPE_THETA ** (-jnp.arange(0, ROT_DIM, 2, dtype=F32) / ROT_DIM)
    ang_t = inv_freq[:, None] * positions.astype(F32).reshape(1, n)
    kblk = (jnp.arange(seq, dtype=I32)[:, None] // SEL_LEN == jnp.arange(LANES, dtype=I32)[None, :]).astype(BF16)

    x2d = x.reshape(n, d)
    assert depth == 1, "single-layer operation"
    for l in range(depth):
        gate_rows_pad = -(-N_GATES // SUBLANES) * SUBLANES
        qnt, qrt, kvc, kk, vt, gt, u = _inproj(x2d, ln_mix[l][None], ang_t, jnp.swapaxes(w_in, 1, 2),
                                               gate_rows_pad, batch, seq)

        half = CMP_LEN * HEAD_DIM // 2
        w2 =jnp.stack([w_cmp_k2[l], w_cmp_v2[l]]).astype(BF16)
        def pe_rows(pe):
            return jnp.pad(pe.reshape(2, half), ((0, SUBLANES - 2), (0, 0)))
        pe2 = jnp.stack([pe_rows(pe_cmp_k[l]), pe_rows(pe_cmp_v[l])]).astype(BF16)
        cmp = _compress(kvc, pe2, w_cmp_k1[l], w_cmp_v1[l], w2)

        mix_nsa = _nsa(qnt, qrt, cmp, kk, vt, kblk, gt, gn_nsa[l][None])
        mix_pool = _pool(u.reshape(batch, seq, -1), w_pool[l].astype(BF16), b_pool[l][None],
                         pool_scale[l][None], gn_pool[l][None])

        ng, ne = N_EXPERT_GROUPS, EXPERTS_PER_GROUP
        w_r = jnp.concatenate([
            w_router_group[l].T, jnp.zeros((ne - ng, d), F32),
            jnp.transpose(w_router_expert[l], (0, 2, 1)).reshape(ng * ne, d),
            jnp.zeros((LANES - ne - ng * ne, d), F32)], axis=0).astype(BF16)
        b_r = jnp.concatenate([
            b_router_group[l], jnp.zeros((ne - ng,), F32), b_router_expert[l].reshape(-1),
            jnp.zeros((LANES - ne - ng * ne,), F32)])
        b_r = jnp.broadcast_to(b_r[:, None], (LANES, LANES))
        x1, h2, eid, gate = _outproj(mix_nsa, mix_pool, x2d, w_out, ln_moe[l][None], w_r, b_r)

        dest, meta = _dispatch(eid)
        cap = (2 * n // TM_MOE + N_EXPERTS) * TM_MOE
        xs = _scatter(_grouped_dest(dest, TS_SCATTER, DMA_ISSUE_GROUP), meta, h2, cap)
        ys = _experts(meta, xs, w_gate[l], w_up[l], w_down[l])
        dest_tok = _grouped_dest(dest, TM_COMBINE, DMA_ISSUE_GROUP)
        x2d = _combine(dest_tok, x1, gate, ln_final[None], ys)
    return x2d.reshape(batch, seq, d)
```
